```python
import jax, jax.numpy as jnp
from jax import lax
import numpy as np

D_MODEL = 1024
BATCH = 32
SEQ = 2048
DEPTH = 2

N_META = 16
HEAD_DIM = 64
CONV_WIDTH = D_MODEL // 4
CONV_HEADS = CONV_WIDTH // HEAD_DIM
CONV_K = 3
POOL_WIDTH = D_MODEL // 4
POOL_WINDOWS = (2, 4, 8, 16)
N_POOL_GROUPS = len(POOL_WINDOWS)
PG = POOL_WIDTH // N_POOL_GROUPS
ATTN_WIDTH = D_MODEL // 2
ATTN_HEADS = ATTN_WIDTH // HEAD_DIM
MIX_WIDTH = CONV_WIDTH + POOL_WIDTH + ATTN_WIDTH
IN_WIDTH = 3 * CONV_WIDTH + POOL_WIDTH + 3 * ATTN_WIDTH
D_FF = 4 * D_MODEL
Q_BLOCK = 128
EPS = 1e-6
SPLITS = (CONV_WIDTH, 2 * CONV_WIDTH, 3 * CONV_WIDTH,
          3 * CONV_WIDTH + POOL_WIDTH,
          3 * CONV_WIDTH + POOL_WIDTH + ATTN_WIDTH,
          3 * CONV_WIDTH + POOL_WIDTH + 2 * ATTN_WIDTH)

kernel_name = "hybrid_conv_pool_stickbreak_trunk"


def rms_norm(x, g):
    xf = x.astype(jnp.float32)
    y = xf * lax.rsqrt(jnp.mean(xf * xf, axis=-1, keepdims=True) + EPS)
    return (y * g.astype(jnp.float32)).astype(x.dtype)


def causal_dwconv(u, w):
    c = u.shape[-1]
    return lax.conv_general_dilated(
        u, w[:, None, :].astype(u.dtype), window_strides=(1,),
        padding=[(CONV_K - 1, 0)], dimension_numbers=("NWC", "WIO", "NWC"),
        feature_group_count=c)


def multiscale_pool(u, w_grp, scale):
    b, l, _ = u.shape
    ug = u.reshape(b, l, N_POOL_GROUPS, PG).astype(jnp.float32)
    cs = jnp.concatenate([jnp.zeros((b, 1, N_POOL_GROUPS, PG), jnp.float32),
                          lax.cumsum(ug, axis=1)], axis=1)
    t = jnp.arange(l)
    means = []
    for g, w in enumerate(POOL_WINDOWS):
        lo = jnp.maximum(t + 1 - w, 0)
        s = cs[:, 1:, g] - cs[:, lo, g]
        cnt = (t + 1 - lo).astype(jnp.float32)
        means.append(s / cnt[None, :, None])
    pooled = (jnp.stack(means, axis=2) - ug).astype(u.dtype)
    y = jnp.einsum("blgc,gcd->blgd", pooled, w_grp)
    return y.reshape(b, l, POOL_WIDTH) * scale


def stick_breaking_attention(q, k, v):
    l = q.shape[2]
    starts = [0] + list(range(N_META, l, Q_BLOCK))
    ends = starts[1:] + [l]
    scale = HEAD_DIM ** -0.5
    outs = []
    for start, end in zip(starts, ends):
        qb = q[:, :, start:end].astype(jnp.float32)
        kb = k[:, :, :end].astype(jnp.float32)
        z = jnp.einsum("bhqd,bhkd->bhqk", qb, kb) * scale
        mask = jnp.arange(end)[None, :] < jnp.arange(start, end)[:, None]
        log_keep = jnp.where(mask, jax.nn.log_sigmoid(-z), 0.0)
        between = lax.cumsum(log_keep, axis=3, reverse=True) - log_keep
        a = jnp.where(mask, jnp.exp(jax.nn.log_sigmoid(z) + between), 0.0)
        outs.append(jnp.einsum("bhqk,bhkd->bhqd", a, v[:, :, :end].astype(jnp.float32)))
    return jnp.concatenate(outs, axis=2).astype(v.dtype)


def _fwd_setup_inputs(seed: int = 0) -> dict:
    key = jax.random.key(seed)
    ks = jax.random.split(key, 12)
    f32 = jnp.float32
    nrm = lambda k, s: jax.random.normal(k, s, f32)
    return {
        "x": nrm(ks[0], (BATCH, SEQ, D_MODEL)),
        "meta_tokens": nrm(ks[1], (N_META, D_MODEL)),
        "g_mix": 1.0 + 0.02 * nrm(ks[2], (DEPTH, D_MODEL)),
        "w_in": nrm(ks[3], (DEPTH, D_MODEL, IN_WIDTH)) * D_MODEL ** -0.5,
        "w_conv": nrm(ks[4], (DEPTH, CONV_K, CONV_WIDTH)) * CONV_K ** -0.5,
        "w_pool": nrm(ks[5], (DEPTH, N_POOL_GROUPS, PG, PG)) * PG ** -0.5,
        "pool_scale": 1.0 + 0.02 * nrm(ks[6], (DEPTH, POOL_WIDTH)),
        "w_out": nrm(ks[7], (DEPTH, MIX_WIDTH, D_MODEL)) * MIX_WIDTH ** -0.5,
        "g_mlp": 1.0 + 0.02 * nrm(ks[8], (DEPTH, D_MODEL)),
        "w_up": nrm(ks[9], (DEPTH, D_MODEL, D_FF)) * D_MODEL ** -0.5,
        "w_down": nrm(ks[10], (DEPTH, D_FF, D_MODEL)) * D_FF ** -0.5,
        "g_final": 1.0 + 0.02 * nrm(ks[11], (D_MODEL,)),
    }


def _fwd_reference(x, meta_tokens, g_mix, w_in, w_conv, w_pool, pool_scale, w_out,
              g_mlp, w_up, w_down, g_final):
    b = x.shape[0]
    meta = jnp.broadcast_to(meta_tokens[None].astype(x.dtype), (b, N_META, D_MODEL))
    h = jnp.concatenate([meta, x], axis=1)
    l = h.shape[1]
    to_heads = lambda t: t.reshape(b, l, ATTN_HEADS, HEAD_DIM).transpose(0, 2, 1, 3)
    for i in range(DEPTH):
        u = rms_norm(h, g_mix[i]) @ w_in[i]
        c_b, c_c, c_x, p_in, q, k, v = jnp.split(u, SPLITS, axis=-1)
        y_conv = c_b * causal_dwconv(c_c * c_x, w_conv[i])
        y_pool = multiscale_pool(p_in, w_pool[i], pool_scale[i])
        y_attn = stick_breaking_attention(to_heads(q), to_heads(k), to_heads(v))
        y_attn = y_attn.transpose(0, 2, 1, 3).reshape(b, l, ATTN_WIDTH)
        h = h + jnp.concatenate([y_conv, y_pool, y_attn], axis=-1) @ w_out[i]
        m = rms_norm(h, g_mlp[i]) @ w_up[i]
        h = h + jnp.square(jax.nn.relu(m)) @ w_down[i]
    return rms_norm(h, g_final)[:, N_META:]


import jax as _jax
import jax.numpy as _jnp

TWIN_FORMAT = 'train_step'
FWD_PARAMS = ['x', 'meta_tokens', 'g_mix', 'w_in', 'w_conv', 'w_pool', 'pool_scale', 'w_out', 'g_mlp', 'w_up', 'w_down', 'g_final']
TWIN_WEIGHTS = ['meta_tokens', 'g_mix', 'w_in', 'w_conv', 'w_pool', 'pool_scale', 'w_out', 'g_mlp', 'w_up', 'w_down', 'g_final']
TWIN_DIFF_INPUT = 'x'
TWIN_INPUTS = ['x', 'meta_tokens', 'g_mix', 'w_in', 'w_conv', 'w_pool', 'pool_scale', 'w_out', 'g_mlp', 'w_up', 'w_down', 'g_final', 'loss_target', 'm_meta_tokens', 'm_g_mix', 'm_w_in', 'm_w_conv', 'm_w_pool', 'm_pool_scale', 'm_w_out', 'm_g_mlp', 'm_w_up', 'm_w_down', 'm_g_final', 'v_meta_tokens', 'v_g_mix', 'v_w_in', 'v_w_conv', 'v_w_pool', 'v_pool_scale', 'v_w_out', 'v_g_mlp', 'v_w_up', 'v_w_down', 'v_g_final']
TWIN_OUTPUTS = ['loss', 'grad_x', 'grad_meta_tokens', 'grad_g_mix', 'grad_w_in', 'grad_w_conv', 'grad_w_pool', 'grad_pool_scale', 'grad_w_out', 'grad_g_mlp', 'grad_w_up', 'grad_w_down', 'grad_g_final', 'delta_meta_tokens', 'delta_g_mix', 'delta_w_in', 'delta_w_conv', 'delta_w_pool', 'delta_pool_scale', 'delta_w_out', 'delta_g_mlp', 'delta_w_up', 'delta_w_down', 'delta_g_final', 'new_m_meta_tokens', 'new_m_g_mix', 'new_m_w_in', 'new_m_w_conv', 'new_m_w_pool', 'new_m_pool_scale', 'new_m_w_out', 'new_m_g_mlp', 'new_m_w_up', 'new_m_w_down', 'new_m_g_final', 'new_v_meta_tokens', 'new_v_g_mix', 'new_v_w_in', 'new_v_w_conv', 'new_v_w_pool', 'new_v_pool_scale', 'new_v_w_out', 'new_v_g_mlp', 'new_v_w_up', 'new_v_w_down', 'new_v_g_final']
TWIN_LEAF_KINDS = {'loss': 'loss', 'grad_x': 'grad_x', 'grad_meta_tokens': 'grad_w', 'grad_g_mix': 'grad_w', 'grad_w_in': 'grad_w', 'grad_w_conv': 'grad_w', 'grad_w_pool': 'grad_w', 'grad_pool_scale': 'grad_w', 'grad_w_out': 'grad_w', 'grad_g_mlp': 'grad_w', 'grad_w_up': 'grad_w', 'grad_w_down': 'grad_w', 'grad_g_final': 'grad_w', 'delta_meta_tokens': 'delta_w', 'delta_g_mix': 'delta_w', 'delta_w_in': 'delta_w', 'delta_w_conv': 'delta_w', 'delta_w_pool': 'delta_w', 'delta_pool_scale': 'delta_w', 'delta_w_out': 'delta_w', 'delta_g_mlp': 'delta_w', 'delta_w_up': 'delta_w', 'delta_w_down': 'delta_w', 'delta_g_final': 'delta_w', 'new_m_meta_tokens': 'new_m', 'new_m_g_mix': 'new_m', 'new_m_w_in': 'new_m', 'new_m_w_conv': 'new_m', 'new_m_w_pool': 'new_m', 'new_m_pool_scale': 'new_m', 'new_m_w_out': 'new_m', 'new_m_g_mlp': 'new_m', 'new_m_w_up': 'new_m', 'new_m_w_down': 'new_m', 'new_m_g_final': 'new_m', 'new_v_meta_tokens': 'new_v', 'new_v_g_mix': 'new_v', 'new_v_w_in': 'new_v', 'new_v_w_conv': 'new_v', 'new_v_w_pool': 'new_v', 'new_v_pool_scale': 'new_v', 'new_v_w_out': 'new_v', 'new_v_g_mlp': 'new_v', 'new_v_w_up': 'new_v', 'new_v_w_down': 'new_v', 'new_v_g_final': 'new_v'}


def _forward(args):
    return _fwd_reference(*[args[k] for k in FWD_PARAMS])


def _output_shape():
    out = _jax.eval_shape(lambda: _forward(_fwd_setup_inputs(0)))
    return out.shape, out.dtype

N_MICROBATCH = 1
ADAM_LR = 0.001
ADAM_B1 = 0.9
ADAM_B2 = 0.999
ADAM_EPS = 1e-08
ADAM_WD = 0.01
ADAM_STEP = 10
PER_EXAMPLE_BATCH_AXIS = {'x': 0, 'loss_target': 0}
SHARED_INPUTS = []
_WEIGHT_DTYPES = {'meta_tokens': _jnp.float32, 'g_mix': _jnp.float32, 'w_in': _jnp.float32, 'w_conv': _jnp.float32, 'w_pool': _jnp.float32, 'pool_scale': _jnp.float32, 'w_out': _jnp.float32, 'g_mlp': _jnp.float32, 'w_up': _jnp.float32, 'w_down': _jnp.float32, 'g_final': _jnp.float32}
MOMENT_SCALE = {'meta_tokens': 5.076577e-03, 'g_mix': 2.367089e-01, 'w_in': 1.362471e-01, 'w_conv': 1.898682e-01, 'w_pool': 1.668404e-01, 'pool_scale': 1.768455e-01, 'w_out': 1.576649e-01, 'g_mlp': 1.935789e-01, 'w_up': 9.753796e-02, 'w_down': 1.884321e-01, 'g_final': 6.473265e+01}


def _to_microbatches(a, axis):
    t = _jnp.moveaxis(a, axis, 0)
    t = t.reshape((N_MICROBATCH, t.shape[0] // N_MICROBATCH) + t.shape[1:])
    return _jnp.moveaxis(t, 1, axis + 1)


def setup_inputs(seed: int = 0) -> dict:
    inp = _fwd_setup_inputs(seed)
    key = _jax.random.fold_in(_jax.random.key(seed), 7919)
    shape, _ = _output_shape()
    out = dict(inp)
    out["loss_target"] = _jax.random.normal(_jax.random.fold_in(key, 0), shape, _jnp.float32)
    for i, name in enumerate(TWIN_WEIGHTS):
        w = inp[name].astype(_jnp.float32)
        if MOMENT_SCALE is None:
            s = _jnp.sqrt(_jnp.mean(_jnp.square(w)) + 1e-30)
        else:
            s = MOMENT_SCALE[name]
        km, kv = _jax.random.split(_jax.random.fold_in(key, i + 1))
        out[name] = w
        out["m_" + name] = s * _jax.random.normal(km, w.shape, _jnp.float32)
        out["v_" + name] = (s * s) * _jax.random.uniform(kv, w.shape, _jnp.float32, 0.5, 1.5)
    if N_MICROBATCH > 1:
        for name, axis in PER_EXAMPLE_BATCH_AXIS.items():
            out[name] = _to_microbatches(out[name], axis)
    return {'x': out['x'], 'meta_tokens': out['meta_tokens'], 'g_mix': out['g_mix'], 'w_in': out['w_in'], 'w_conv': out['w_conv'], 'w_pool': out['w_pool'], 'pool_scale': out['pool_scale'], 'w_out': out['w_out'], 'g_mlp': out['g_mlp'], 'w_up': out['w_up'], 'w_down': out['w_down'], 'g_final': out['g_final'], 'loss_target': out['loss_target'], 'm_meta_tokens': out['m_meta_tokens'], 'm_g_mix': out['m_g_mix'], 'm_w_in': out['m_w_in'], 'm_w_conv': out['m_w_conv'], 'm_w_pool': out['m_w_pool'], 'm_pool_scale': out['m_pool_scale'], 'm_w_out': out['m_w_out'], 'm_g_mlp': out['m_g_mlp'], 'm_w_up': out['m_w_up'], 'm_w_down': out['m_w_down'], 'm_g_final': out['m_g_final'], 'v_meta_tokens': out['v_meta_tokens'], 'v_g_mix': out['v_g_mix'], 'v_w_in': out['v_w_in'], 'v_w_conv': out['v_w_conv'], 'v_w_pool': out['v_w_pool'], 'v_pool_scale': out['v_pool_scale'], 'v_w_out': out['v_w_out'], 'v_g_mlp': out['v_g_mlp'], 'v_w_up': out['v_w_up'], 'v_w_down': out['v_w_down'], 'v_g_final': out['v_g_final']}


def _loss(weights, diff, rest, loss_target):
    with _jax.named_scope("forward"):
        args = {**rest, TWIN_DIFF_INPUT: diff, **{k: w.astype(_WEIGHT_DTYPES[k]) for k, w in weights.items()}}
        y = _forward(args)
    with _jax.named_scope("loss_head"):
        err = _jnp.square(y.astype(_jnp.float32) - loss_target)
        return 0.5 * _jnp.sum(_jnp.mean(err, axis=-1)) if err.ndim else 0.5 * err


def _adamw(w, g, m, v):
    m = ADAM_B1 * m + (1.0 - ADAM_B1) * g
    v = ADAM_B2 * v + (1.0 - ADAM_B2) * _jnp.square(g)
    m_hat = m / (1.0 - ADAM_B1 ** ADAM_STEP)
    v_hat = v / (1.0 - ADAM_B2 ** ADAM_STEP)
    delta = -ADAM_LR * (m_hat / (_jnp.sqrt(v_hat) + ADAM_EPS) + ADAM_WD * w)
    return delta, m, v


def reference(x, meta_tokens, g_mix, w_in, w_conv, w_pool, pool_scale, w_out, g_mlp, w_up, w_down, g_final, loss_target, m_meta_tokens, m_g_mix, m_w_in, m_w_conv, m_w_pool, m_pool_scale, m_w_out, m_g_mlp, m_w_up, m_w_down, m_g_final, v_meta_tokens, v_g_mix, v_w_in, v_w_conv, v_w_pool, v_pool_scale, v_w_out, v_g_mlp, v_w_up, v_w_down, v_g_final):
    given = dict(x=x, meta_tokens=meta_tokens, g_mix=g_mix, w_in=w_in, w_conv=w_conv, w_pool=w_pool, pool_scale=pool_scale, w_out=w_out, g_mlp=g_mlp, w_up=w_up, w_down=w_down, g_final=g_final, loss_target=loss_target, m_meta_tokens=m_meta_tokens, m_g_mix=m_g_mix, m_w_in=m_w_in, m_w_conv=m_w_conv, m_w_pool=m_w_pool, m_pool_scale=m_pool_scale, m_w_out=m_w_out, m_g_mlp=m_g_mlp, m_w_up=m_w_up, m_w_down=m_w_down, m_g_final=m_g_final, v_meta_tokens=v_meta_tokens, v_g_mix=v_g_mix, v_w_in=v_w_in, v_w_conv=v_w_conv, v_w_pool=v_w_pool, v_pool_scale=v_pool_scale, v_w_out=v_w_out, v_g_mlp=v_g_mlp, v_w_up=v_w_up, v_w_down=v_w_down, v_g_final=v_g_final)
    weights = {n: given[n] for n in TWIN_WEIGHTS}
    shared = {n: given[n] for n in SHARED_INPUTS}
    per_example = {n: given[n] for n in ['x']}
    grad_fn = _jax.value_and_grad(_loss, argnums=(0, 1))

    def one_microbatch(ex, loss_target):
        ex = dict(ex)
        diff = ex.pop(TWIN_DIFF_INPUT)
        return grad_fn(weights, diff, {**shared, **ex}, loss_target)

    if N_MICROBATCH == 1:
        loss, (grad_w, grad_x) = one_microbatch(per_example, given["loss_target"])
    else:
        def body(carry, xs):
            loss_sum, grad_sum = carry
            l_k, (gw_k, gx_k) = one_microbatch(xs[0], xs[1])
            with _jax.named_scope("update"):
                return (loss_sum + l_k, _jax.tree.map(_jnp.add, grad_sum, gw_k)), gx_k

        init = (_jnp.zeros((), _jnp.float32), _jax.tree.map(_jnp.zeros_like, weights))
        (loss, grad_w), grad_x = _jax.lax.scan(body, init, (per_example, given["loss_target"]))
    with _jax.named_scope("update"):
        delta_w, new_m, new_v = {}, {}, {}
        for n in TWIN_WEIGHTS:
            delta_w[n], new_m[n], new_v[n] = _adamw(weights[n], grad_w[n], given["m_" + n], given["v_" + n])
    return (loss, grad_x, *[grad_w[n] for n in TWIN_WEIGHTS], *[delta_w[n] for n in TWIN_WEIGHTS],
            *[new_m[n] for n in TWIN_WEIGHTS], *[new_v[n] for n in TWIN_WEIGHTS])
```

```python
import functools

import jax
import jax.numpy as jnp
from jax import lax
from jax.experimental import pallas as pl
from jax.experimental.pallas import tpu as pltpu

F32, BF16 = jnp.float32, jnp.bfloat16
MESH = pl.DeviceIdType.MESH
EPS = 1e-6
N_META = 16
QB = 128
PAD = QB - N_META
HALO = 16
POOL_WINDOWS = (2.0, 4.0, 8.0, 16.0)
HEAD_SCALE = 0.125
LR, B1, B2, ADAM_EPS, WD, STEP = 0.001, 0.9, 0.999, 1e-08, 0.01, 10
VMEM_LIMIT = 56 * 1024 * 1024


def _params(sem=None):
    return pltpu.CompilerParams(dimension_semantics=sem, vmem_limit_bytes=VMEM_LIMIT)


def _nt(a, b):
    return lax.dot_general(a, b, (((1,), (1,)), ((), ())), preferred_element_type=F32)


def _tn(a, b):
    return lax.dot_general(a, b, (((0,), (0,)), ((), ())), preferred_element_type=F32)


def _nn(a, b):
    return jnp.dot(a, b, preferred_element_type=F32)


def _fold8(v):
    r, c = v.shape
    return jnp.sum(v.reshape(r // 8, 8, c), axis=0)


def _mm_nn(name, a_list, w, layer, n_off, n, tm, tn, epi, extras, out_dtypes):
    m = a_list[0].shape[0]
    kdim = w.shape[1]
    ks = [a.shape[1] for a in a_list]
    assert sum(ks) == kdim and n_off % tn == 0 and n % tn == 0 and m % tm == 0
    na, ne = len(a_list), len(extras)

    def body(*refs):
        a_refs, w_ref = refs[:na], refs[na]
        e_refs, o_refs = refs[na + 1:na + 1 + ne], refs[na + 1 + ne:]
        acc, off = None, 0
        for a_ref, k in zip(a_refs, ks):
            part = _nn(a_ref[...].astype(BF16), w_ref[off:off + k, :])
            acc = part if acc is None else acc + part
            off += k
        outs = epi(acc, *[e[...] for e in e_refs])
        for o_ref, o in zip(o_refs, outs):
            o_ref[...] = o.astype(o_ref.dtype)

    in_specs = [pl.BlockSpec((tm, k), lambda i, j: (i, 0)) for k in ks]
    in_specs.append(pl.BlockSpec((None, kdim, tn), lambda i, j: (layer, 0, n_off // tn + j)))
    in_specs += [pl.BlockSpec((tm, tn), lambda i, j: (i, j)) for _ in extras]
    return pl.pallas_call(
        body, name=name, grid=(m // tm, n // tn), in_specs=in_specs,
        out_specs=[pl.BlockSpec((tm, tn), lambda i, j: (i, j)) for _ in out_dtypes],
        out_shape=[jax.ShapeDtypeStruct((m, n), d) for d in out_dtypes],
        compiler_params=_params(("parallel", "arbitrary")),
    )(*a_list, w, *extras)


def _mm_nt(name, a_list, w, layer, tm, tko, epi, extras, out_dtypes):
    m = a_list[0].shape[0]
    kout, ntot = w.shape[1], w.shape[2]
    ns = [a.shape[1] for a in a_list]
    assert sum(ns) == ntot and kout % tko == 0 and m % tm == 0
    na, ne = len(a_list), len(extras)

    def body(*refs):
        a_refs, w_ref = refs[:na], refs[na]
        e_refs, o_refs = refs[na + 1:na + 1 + ne], refs[na + 1 + ne:]
        acc, off = None, 0
        for a_ref, k in zip(a_refs, ns):
            part = _nt(a_ref[...].astype(BF16), w_ref[:, off:off + k])
            acc = part if acc is None else acc + part
            off += k
        outs = epi(acc, *[e[...] for e in e_refs])
        for o_ref, o in zip(o_refs, outs):
            o_ref[...] = o.astype(o_ref.dtype)

    in_specs = [pl.BlockSpec((tm, k), lambda i, j: (i, 0)) for k in ns]
    in_specs.append(pl.BlockSpec((None, tko, ntot), lambda i, j: (layer, j, 0)))
    in_specs += [pl.BlockSpec((tm, tko), lambda i, j: (i, j)) for _ in extras]
    return pl.pallas_call(
        body, name=name, grid=(m // tm, kout // tko), in_specs=in_specs,
        out_specs=[pl.BlockSpec((tm, tko), lambda i, j: (i, j)) for _ in out_dtypes],
        out_shape=[jax.ShapeDtypeStruct((m, kout), d) for d in out_dtypes],
        compiler_params=_params(("parallel", "arbitrary")),
    )(*a_list, w, *extras)


def _mm_tn(name, a, b, tt, tka, tn):
    t, ka = a.shape
    n = b.shape[1]
    assert t % tt == 0 and ka % tka == 0 and n % tn == 0

    def body(a_ref, b_ref, o_ref):
        @pl.when(pl.program_id(2) == 0)
        def _():
            o_ref[...] = jnp.zeros_like(o_ref)
        o_ref[...] += _tn(a_ref[...].astype(BF16), b_ref[...].astype(BF16))

    return pl.pallas_call(
        body, name=name, grid=(ka // tka, n // tn, t // tt),
        in_specs=[pl.BlockSpec((tt, tka), lambda i, j, s: (s, i)), pl.BlockSpec((tt, tn), lambda i, j, s: (s, j))],
        out_specs=pl.BlockSpec((tka, tn), lambda i, j, s: (i, j)),
        out_shape=jax.ShapeDtypeStruct((ka, n), F32),
        compiler_params=_params(("parallel", "parallel", "arbitrary")),
    )(a, b)


def _rms_fwd(name, h, g, tm):
    t, d = h.shape

    def body(h_ref, g_ref, o_ref):
        x = h_ref[...]
        r = lax.rsqrt(jnp.mean(x * x, axis=-1, keepdims=True) + EPS)
        o_ref[...] = (x * r * g_ref[...]).astype(o_ref.dtype)

    return pl.pallas_call(
        body, name=name, grid=(t // tm,),
        in_specs=[pl.BlockSpec((tm, d), lambda i: (i, 0)), pl.BlockSpec((1, d), lambda i: (0, 0))],
        out_specs=pl.BlockSpec((tm, d), lambda i: (i, 0)),
        out_shape=jax.ShapeDtypeStruct((t, d), BF16),
        compiler_params=_params(("parallel",)),
    )(h, g.reshape(1, d))


def _rms_bwd(name, x, dyn, g, resid, tm):
    t, d = x.shape

    def body(x_ref, dy_ref, g_ref, r_ref, o_ref, dg_ref):
        @pl.when(pl.program_id(0) == 0)
        def _():
            dg_ref[...] = jnp.zeros_like(dg_ref)
        xv, dy = x_ref[...], dy_ref[...]
        r = lax.rsqrt(jnp.mean(xv * xv, axis=-1, keepdims=True) + EPS)
        xh = xv * r
        w = dy * g_ref[...]
        o_ref[...] = r_ref[...] + r * (w - xh * jnp.mean(w * xh, axis=-1, keepdims=True))
        dg_ref[...] += _fold8(dy * xh)

    row = pl.BlockSpec((tm, d), lambda i: (i, 0))
    return pl.pallas_call(
        body, name=name, grid=(t // tm,),
        in_specs=[row, row, pl.BlockSpec((1, d), lambda i: (0, 0)), row],
        out_specs=[row, pl.BlockSpec((8, d), lambda i: (0, 0))],
        out_shape=[jax.ShapeDtypeStruct((t, d), F32), jax.ShapeDtypeStruct((8, d), F32)],
        compiler_params=_params(("arbitrary",)),
    )(x, dyn, g.reshape(1, d), resid)


def _loss_bwd(name, h, g, target, lp):
    t, d = h.shape
    bl = target.shape[0]
    nq = lp // QB

    def body(h_ref, g_ref, t_ref, dh_ref, ls_ref, dg_ref):
        b, j = pl.program_id(0), pl.program_id(1)

        @pl.when((b == 0) & (j == 0))
        def _():
            ls_ref[...] = jnp.zeros_like(ls_ref)
            dg_ref[...] = jnp.zeros_like(dg_ref)
        xv = h_ref[...]
        r = lax.rsqrt(jnp.mean(xv * xv, axis=-1, keepdims=True) + EPS)
        xh = xv * r
        gv = g_ref[...]
        err = jnp.where(j >= 1, xh * gv - t_ref[...], 0.0)
        ls_ref[...] += _fold8(err * err) * (0.5 / d)
        dy = err * (1.0 / d)
        w = dy * gv
        dh_ref[...] = r * (w - xh * jnp.mean(w * xh, axis=-1, keepdims=True))
        dg_ref[...] += _fold8(dy * xh)

    return pl.pallas_call(
        body, name=name, grid=(bl, nq),
        in_specs=[pl.BlockSpec((QB, d), lambda b, j: (b * nq + j, 0)), pl.BlockSpec((1, d), lambda b, j: (0, 0)),
                  pl.BlockSpec((None, QB, d), lambda b, j: (b, jnp.maximum(j - 1, 0), 0))],
        out_specs=[pl.BlockSpec((QB, d), lambda b, j: (b * nq + j, 0)), pl.BlockSpec((8, d), lambda b, j: (0, 0)),
                   pl.BlockSpec((8, d), lambda b, j: (0, 0))],
        out_shape=[jax.ShapeDtypeStruct((t, d), F32), jax.ShapeDtypeStruct((8, d), F32),
                   jax.ShapeDtypeStruct((8, d), F32)],
        compiler_params=_params(("arbitrary", "arbitrary")),
    )(h, g.reshape(1, d), target)


def _pool_select(grp, a2, a4, a8, a16):
    return jnp.where(grp == 0, a2, jnp.where(grp == 1, a4, jnp.where(grp == 2, a8, a16)))


def _trailing_sums(v):
    s2 = v + pltpu.roll(v, 1, 0)
    s4 = s2 + pltpu.roll(s2, 2, 0)
    s8 = s4 + pltpu.roll(s4, 4, 0)
    s16 = s8 + pltpu.roll(s8, 8, 0)
    return s2, s4, s8, s16


def _leading_sums(v):
    n = v.shape[0]
    s2 = v + pltpu.roll(v, n - 1, 0)
    s4 = s2 + pltpu.roll(s2, n - 2, 0)
    s8 = s4 + pltpu.roll(s4, n - 4, 0)
    s16 = s8 + pltpu.roll(s8, n - 8, 0)
    return s2, s4, s8, s16


def _convpool_fwd(name, u_cp, wconv, wbd, pscale, lp, r):
    t = u_cp.shape[0]
    cw = u_cp.shape[1] // 4
    tps, hb = lp // r, r // HALO

    def body(cb_ref, cc_ref, cx_ref, pi_ref, cch_ref, cxh_ref, pih_ref, wc_ref, wbd_ref, ps_ref, y_ref):
        i = pl.program_id(0)
        lrow = (i % tps) * r + lax.broadcasted_iota(jnp.int32, (r, 1), 0)
        valid = lrow >= PAD
        xx = jnp.concatenate([cch_ref[...] * cxh_ref[...], cc_ref[...] * cx_ref[...]], axis=0)
        conv = (wc_ref[0:1, :] * pltpu.roll(xx, 2, 0) + wc_ref[1:2, :] * pltpu.roll(xx, 1, 0)
                + wc_ref[2:3, :] * xx)
        y_ref[:, 0:cw] = (cb_ref[...] * conv[HALO:]).astype(y_ref.dtype)
        p = pi_ref[...]
        grp = lax.broadcasted_iota(jnp.int32, (1, cw), 1) // (cw // 4)
        sel = _pool_select(grp, *_trailing_sums(jnp.concatenate([pih_ref[...], p], axis=0)))[HALO:]
        cnt = jnp.maximum(jnp.minimum((lrow - (PAD - 1)).astype(F32), _pool_select(grp, *POOL_WINDOWS)), 1.0)
        pooled = jnp.where(valid, sel / cnt - p, 0.0)
        y_ref[:, cw:2 * cw] = (_nn(pooled.astype(BF16), wbd_ref[...]) * ps_ref[...]).astype(y_ref.dtype)

    def main(col):
        return pl.BlockSpec((r, cw), lambda i: (i, col))

    def prev(col):
        return pl.BlockSpec((HALO, cw), lambda i: (jnp.maximum(i * hb - 1, 0), col))

    def whole(a):
        return pl.BlockSpec(a.shape, lambda i: (0, 0))

    return pl.pallas_call(
        body, name=name, grid=(t // r,),
        in_specs=[main(0), main(1), main(2), main(3), prev(1), prev(2), prev(3), whole(wconv), whole(wbd),
                  whole(pscale)],
        out_specs=pl.BlockSpec((r, 2 * cw), lambda i: (i, 0)),
        out_shape=jax.ShapeDtypeStruct((t, 2 * cw), BF16),
        compiler_params=_params(("parallel",)),
    )(u_cp, u_cp, u_cp, u_cp, u_cp, u_cp, u_cp, wconv, wbd, pscale)


def _convpool_bwd(name, u_cp, dy, wconv, wbd, pscale, lp, r):
    t = u_cp.shape[0]
    cw = u_cp.shape[1] // 4
    tps, hb = lp // r, r // HALO
    e = r + HALO

    def body(cb_ref, cc_ref, cx_ref, pi_ref, cbn_ref, cch_ref, cxh_ref, pih_ref, dyc_ref, dyp_ref, dycn_ref,
             dypn_ref, wc_ref, wbd_ref, ps_ref, du_ref, sm_ref, dwbd_ref):
        i = pl.program_id(0)

        @pl.when(i == 0)
        def _():
            sm_ref[...] = jnp.zeros_like(sm_ref)
            dwbd_ref[...] = jnp.zeros_like(dwbd_ref)
        lrow_e = (i % tps) * r + lax.broadcasted_iota(jnp.int32, (e, 1), 0)
        valid_e = (lrow_e >= PAD) & (lrow_e < lp)
        lrow, valid = lrow_e[:r], lrow_e[:r] >= PAD
        w0, w1, w2 = wc_ref[0:1, :], wc_ref[1:2, :], wc_ref[2:3, :]
        cb, cc, cx = cb_ref[...], cc_ref[...], cx_ref[...]
        prod = cc * cx
        xx = jnp.concatenate([cch_ref[...] * cxh_ref[...], prod], axis=0)
        back1, back2 = pltpu.roll(xx, 1, 0)[HALO:], pltpu.roll(xx, 2, 0)[HALO:]
        dyc = dyc_ref[...]
        du_ref[:, 0:cw] = (dyc * (w0 * back2 + w1 * back1 + w2 * prod)).astype(du_ref.dtype)
        dconv_e = jnp.where(valid_e, jnp.concatenate([dyc * cb, dycn_ref[...] * cbn_ref[...]], axis=0), 0.0)
        dconv = dconv_e[:r]
        dprod = (w2 * dconv + w1 * pltpu.roll(dconv_e, e - 1, 0)[:r] + w0 * pltpu.roll(dconv_e, e - 2, 0)[:r])
        du_ref[:, cw:2 * cw] = (dprod * cx).astype(du_ref.dtype)
        du_ref[:, 2 * cw:3 * cw] = (dprod * cc).astype(du_ref.dtype)
        sm_ref[0:8, :] += _fold8(dconv * back2)
        sm_ref[8:16, :] += _fold8(dconv * back1)
        sm_ref[16:24, :] += _fold8(dconv * prod)
        p = pi_ref[...]
        grp = lax.broadcasted_iota(jnp.int32, (1, cw), 1) // (cw // 4)
        win = _pool_select(grp, *POOL_WINDOWS)
        sel = _pool_select(grp, *_trailing_sums(jnp.concatenate([pih_ref[...], p], axis=0)))[HALO:]
        cnt_e = jnp.maximum(jnp.minimum((lrow_e - (PAD - 1)).astype(F32), win), 1.0)
        pooled = jnp.where(valid, sel / cnt_e[:r] - p, 0.0).astype(BF16)
        dyp = dyp_ref[...]
        sm_ref[24:32, :] += _fold8(dyp * _nn(pooled, wbd_ref[...]))
        dpre_e = (jnp.concatenate([dyp, dypn_ref[...]], axis=0) * ps_ref[...]).astype(BF16)
        dwbd_ref[...] += _tn(pooled, dpre_e[:r])
        dpooled_e = jnp.where(valid_e, _nt(dpre_e, wbd_ref[...]), 0.0)
        ahead = _pool_select(grp, *_leading_sums(dpooled_e / cnt_e))[:r]
        du_ref[:, 3 * cw:4 * cw] = (ahead - dpooled_e[:r]).astype(du_ref.dtype)

    last_halo = t // HALO - 1

    def main(col):
        return pl.BlockSpec((r, cw), lambda i: (i, col))

    def prev(col):
        return pl.BlockSpec((HALO, cw), lambda i: (jnp.maximum(i * hb - 1, 0), col))

    def nxt(col):
        return pl.BlockSpec((HALO, cw), lambda i: (jnp.minimum((i + 1) * hb, last_halo), col))

    def whole(a):
        return pl.BlockSpec(a.shape, lambda i: (0, 0))

    return pl.pallas_call(
        body, name=name, grid=(t // r,),
        in_specs=[main(0), main(1), main(2), main(3), nxt(0), prev(1), prev(2), prev(3), main(0), main(1), nxt(0),
                  nxt(1), whole(wconv), whole(wbd), whole(pscale)],
        out_specs=[pl.BlockSpec((r, 4 * cw), lambda i: (i, 0)), pl.BlockSpec((32, cw), lambda i: (0, 0)),
                   pl.BlockSpec((cw, cw), lambda i: (0, 0))],
        out_shape=[jax.ShapeDtypeStruct((t, 4 * cw), BF16), jax.ShapeDtypeStruct((32, cw), F32),
                   jax.ShapeDtypeStruct((cw, cw), F32)],
        compiler_params=_params(("arbitrary",)),
    )(u_cp, u_cp, u_cp, u_cp, u_cp, u_cp, u_cp, u_cp, dy, dy, dy, dy, wconv, wbd, pscale)


def _split3(v):
    h1 = v.astype(BF16)
    r1 = v - h1.astype(F32)
    h2 = r1.astype(BF16)
    return h1, h2, (r1 - h2.astype(F32)).astype(BF16)


def _log_sigmoid_parts(z):
    ez = jnp.exp(-jnp.abs(z))
    return jnp.minimum(z, 0.0) - jnp.log(1.0 + ez), ez


def _attn_fwd(name, qkv, bl, lp):
    t = qkv.shape[0]
    nq, npair = lp // QB, qkv.shape[1] // (3 * QB)

    def body(q_ref, k_ref, v_ref, o_ref, lt_ref):
        qi = pl.program_id(2)
        lane = lax.broadcasted_iota(jnp.int32, (QB, QB), 1)
        row = lax.broadcasted_iota(jnp.int32, (QB, QB), 0)
        head0 = lane < QB // 2
        qs = q_ref[...] * jnp.asarray(HEAD_SCALE, BF16)
        zero = jnp.zeros_like(qs)
        qh = (jnp.where(head0, qs, zero), jnp.where(head0, zero, qs))
        later = (row > lane).astype(BF16)
        q_pos = qi * QB + row

        def step(it, carry):
            kb = qi - it
            off = pl.multiple_of(kb * QB, QB)
            kblk, vblk = k_ref[pl.ds(off, QB), :], v_ref[pl.ds(off, QB), :]
            k_pos = kb * QB + lane
            valid = (k_pos < q_pos) & (k_pos >= PAD)
            out = []
            for h in range(2):
                run, acc = carry[2 * h], carry[2 * h + 1]
                z = _nt(qh[h], kblk)
                logp, _ = _log_sigmoid_parts(z)
                lk = jnp.where(valid, logp - z, 0.0)
                l1, l2, l3 = _split3(lk)
                between = _nn(l1, later) + _nn(l2, later) + _nn(l3, later) + run
                a = jnp.where(valid, jnp.exp(logp + between), 0.0)
                out += [run + jnp.sum(lk, axis=1, keepdims=True), acc + _nn(a.astype(BF16), vblk)]
            return tuple(out)

        col0, blk0 = jnp.zeros((QB, 1), F32), jnp.zeros((QB, QB), F32)
        r0, o0, r1, o1 = lax.fori_loop(0, qi + 1, step, (col0, blk0, col0, blk0))
        o_ref[...] = jnp.where(head0, o0, o1).astype(o_ref.dtype)
        lt_ref[...] = jnp.where(head0, r0, r1)

    blk = pl.BlockSpec((QB, QB), lambda b, p, i: (b * nq + i, p))
    return pl.pallas_call(
        body, name=name, grid=(bl, npair, nq),
        in_specs=[blk, pl.BlockSpec((lp, QB), lambda b, p, i: (b, npair + p)),
                  pl.BlockSpec((lp, QB), lambda b, p, i: (b, 2 * npair + p))],
        out_specs=[blk, blk],
        out_shape=[jax.ShapeDtypeStruct((t, npair * QB), BF16), jax.ShapeDtypeStruct((t, npair * QB), F32)],
        compiler_params=_params(("parallel", "parallel", "arbitrary")),
    )(qkv, qkv, qkv)


def _attn_bwd(name, qkv, lt, dy, bl, lp):
    t = qkv.shape[0]
    nq, npair = lp // QB, qkv.shape[1] // (3 * QB)

    def body(q_ref, k_ref, v_ref, lt_ref, do_ref, dq_ref, dk_ref, dv_ref, dk_acc, dv_acc):
        qi = pl.program_id(2)

        @pl.when(qi == 0)
        def _():
            dk_acc[...] = jnp.zeros_like(dk_acc)
            dv_acc[...] = jnp.zeros_like(dv_acc)
        lane = lax.broadcasted_iota(jnp.int32, (QB, QB), 1)
        row = lax.broadcasted_iota(jnp.int32, (QB, QB), 0)
        head0 = lane < QB // 2
        qs = q_ref[...] * jnp.asarray(HEAD_SCALE, BF16)
        do = do_ref[...].astype(BF16)
        zero = jnp.zeros_like(qs)
        qh = (jnp.where(head0, qs, zero), jnp.where(head0, zero, qs))
        doh = (jnp.where(head0, do, zero), jnp.where(head0, zero, do))
        ltv = lt_ref[...]
        total = (jnp.sum(jnp.where(lane == 0, ltv, 0.0), axis=1, keepdims=True),
                 jnp.sum(jnp.where(lane == QB // 2, ltv, 0.0), axis=1, keepdims=True))
        later = (row > lane).astype(BF16)
        earlier = (row < lane).astype(BF16)
        q_pos = qi * QB + row

        def step(kb, carry):
            off = pl.multiple_of(kb * QB, QB)
            kblk, vblk = k_ref[pl.ds(off, QB), :], v_ref[pl.ds(off, QB), :]
            k_pos = kb * QB + lane
            valid = (k_pos < q_pos) & (k_pos >= PAD)
            out = []
            dk_blk, dv_blk = jnp.zeros((QB, QB), F32), jnp.zeros((QB, QB), F32)
            for h in range(2):
                seen, gsum, dq = carry[3 * h], carry[3 * h + 1], carry[3 * h + 2]
                z = _nt(qh[h], kblk)
                logp, ez = _log_sigmoid_parts(z)
                lk = jnp.where(valid, logp - z, 0.0)
                rs = jnp.sum(lk, axis=1, keepdims=True)
                l1, l2, l3 = _split3(lk)
                between = _nn(l1, later) + _nn(l2, later) + _nn(l3, later) + (total[h] - seen - rs)
                a = jnp.where(valid, jnp.exp(logp + between), 0.0)
                g = a * _nt(doh[h], vblk)
                g1 = g.astype(BF16)
                g2 = (g - g1.astype(F32)).astype(BF16)
                gbefore = _nn(g1, earlier) + _nn(g2, earlier) + gsum
                sig = jnp.where(z >= 0.0, 1.0, ez) / (1.0 + ez)
                dz = jnp.where(valid, g * (1.0 - sig) - gbefore * sig, 0.0).astype(BF16)
                dk_blk += _tn(dz, qh[h])
                dv_blk += _tn(a.astype(BF16), doh[h])
                out += [seen + rs, gsum + jnp.sum(g, axis=1, keepdims=True), dq + _nn(dz, kblk)]
            dk_acc[pl.ds(off, QB), :] += dk_blk
            dv_acc[pl.ds(off, QB), :] += dv_blk
            return tuple(out)

        col0, blk0 = jnp.zeros((QB, 1), F32), jnp.zeros((QB, QB), F32)
        res = lax.fori_loop(0, qi + 1, step, (col0, col0, blk0, col0, col0, blk0))
        dq_ref[...] = (jnp.where(head0, res[2], res[5]) * HEAD_SCALE).astype(dq_ref.dtype)

        @pl.when(qi == nq - 1)
        def _():
            dk_ref[...] = dk_acc[...].astype(dk_ref.dtype)
            dv_ref[...] = dv_acc[...].astype(dv_ref.dtype)

    blk = pl.BlockSpec((QB, QB), lambda b, p, i: (b * nq + i, p))
    seq = pl.BlockSpec((lp, QB), lambda b, p, i: (b, p))
    out = jax.ShapeDtypeStruct((t, npair * QB), BF16)
    return pl.pallas_call(
        body, name=name, grid=(bl, npair, nq),
        in_specs=[blk, pl.BlockSpec((lp, QB), lambda b, p, i: (b, npair + p)),
                  pl.BlockSpec((lp, QB), lambda b, p, i: (b, 2 * npair + p)), blk,
                  pl.BlockSpec((QB, QB), lambda b, p, i: (b * nq + i, npair + p))],
        out_specs=[blk, seq, seq],
        out_shape=[out, out, out],
        scratch_shapes=[pltpu.VMEM((lp, QB), F32), pltpu.VMEM((lp, QB), F32)],
        compiler_params=_params(("parallel", "parallel", "arbitrary")),
    )(qkv, qkv, qkv, lt, dy)


def _place():
    return lax.axis_index("x"), lax.axis_index("y"), lax.axis_index("c")


def _slab(ref, axis, size, chip, *lead):
    if axis == 0:
        return ref.at[(*lead, pl.ds(chip * size, size), slice(None))]
    return ref.at[(*lead, slice(None), pl.ds(chip * size, size))]


def _peers(chip):
    kx, ky = chip // 2, chip % 2
    return ((1 - kx, ky), (kx, 1 - ky), (1 - kx, 1 - ky))


def _hbm_specs(n):
    return [pl.BlockSpec(memory_space=pl.ANY) for _ in range(n)]


def _all_gather(name, shards, axes):
    n = len(shards)
    out_shape = []
    for a, ax in zip(shards, axes):
        shp = list(a.shape)
        shp[1 + ax] *= 4
        out_shape.append(jax.ShapeDtypeStruct(tuple(shp), a.dtype))

    def body(*refs):
        ins, outs = refs[:n], refs[n:2 * n]
        send, recv, lsem = refs[2 * n:]
        x, y, c = _place()
        sib = (x, y, 1 - c)

        def slab(w, half, chip):
            return _slab(outs[w], axes[w], shards[w].shape[1 + axes[w]], chip, half)

        def protocol(me):
            local, sends, passes = [], [], []
            for w in range(n):
                for half in range(2):
                    cp = pltpu.make_async_copy(ins[w].at[half], slab(w, half, me), lsem.at[2 * w + half])
                    cp.start()
                    local.append(cp)
            for w in range(n):
                for j, (px, py) in enumerate(_peers(me)):
                    cp = pltpu.make_async_remote_copy(ins[w].at[c], slab(w, c, me), send.at[3 * w + j],
                                                      recv.at[3 * w + j], device_id=(px, py, c),
                                                      device_id_type=MESH)
                    cp.start()
                    sends.append(cp)
            for w in range(n):
                for j, (px, py) in enumerate(_peers(me)):
                    got = slab(w, c, 2 * px + py)
                    pltpu.make_async_remote_copy(ins[w].at[c], got, send.at[3 * w + j], recv.at[3 * w + j],
                                                 device_id=(px, py, c), device_id_type=MESH).wait_recv()
                    cp = pltpu.make_async_remote_copy(got, got, send.at[3 * (n + w) + j],
                                                      recv.at[3 * (n + w) + j], device_id=sib, device_id_type=MESH)
                    cp.start()
                    passes.append(cp)
            for w in range(n):
                for j, (px, py) in enumerate(_peers(me)):
                    got = slab(w, 1 - c, 2 * px + py)
                    pltpu.make_async_remote_copy(got, got, send.at[3 * (n + w) + j], recv.at[3 * (n + w) + j],
                                                 device_id=sib, device_id_type=MESH).wait_recv()
            for cp in sends + passes:
                cp.wait_send()
            for cp in local:
                cp.wait()

        for me in range(4):
            pl.when(2 * x + y == me)(functools.partial(protocol, me))

    return pl.pallas_call(
        body, name=name, in_specs=_hbm_specs(n), out_specs=_hbm_specs(n), out_shape=out_shape,
        scratch_shapes=[pltpu.SemaphoreType.DMA((6 * n,)), pltpu.SemaphoreType.DMA((6 * n,)),
                        pltpu.SemaphoreType.DMA((2 * n,))],
        compiler_params=pltpu.CompilerParams(has_side_effects=True),
    )(*shards)


def _swap_halves(name, grads):
    n = len(grads)

    def body(*refs):
        ins, outs, send, recv = refs[:n], refs[n:2 * n], refs[2 * n], refs[2 * n + 1]
        x, y, c = _place()
        cps = [pltpu.make_async_remote_copy(ins[w].at[1 - c], outs[w], send.at[w], recv.at[w],
                                            device_id=(x, y, 1 - c), device_id_type=MESH) for w in range(n)]
        for cp in cps:
            cp.start()
        for cp in cps:
            cp.wait()

    return pl.pallas_call(
        body, name=name, in_specs=_hbm_specs(n), out_specs=_hbm_specs(n),
        out_shape=[jax.ShapeDtypeStruct(g.shape[1:], g.dtype) for g in grads],
        scratch_shapes=[pltpu.SemaphoreType.DMA((n,)), pltpu.SemaphoreType.DMA((n,))],
        compiler_params=pltpu.CompilerParams(has_side_effects=True),
    )(*grads)


def _add_half(name, g, got, cidx, tr):
    _, r, cdim = g.shape

    def body(c_ref, g_ref, r_ref, o_ref):
        o_ref[...] = (g_ref[...] + r_ref[...]).astype(o_ref.dtype)

    return pl.pallas_call(
        body, name=name,
        grid_spec=pltpu.PrefetchScalarGridSpec(
            num_scalar_prefetch=1, grid=(r // tr,),
            in_specs=[pl.BlockSpec((None, tr, cdim), lambda i, c_ref: (c_ref[0], i, 0)),
                      pl.BlockSpec((tr, cdim), lambda i, c_ref: (i, 0))],
            out_specs=pl.BlockSpec((tr, cdim), lambda i, c_ref: (i, 0))),
        out_shape=jax.ShapeDtypeStruct((r, cdim), BF16),
        compiler_params=_params(("arbitrary",)),
    )(cidx, g, got)


def _scatter_chips(name, sums, axes):
    n = len(sums)
    sizes = [s.shape[ax] // 4 for s, ax in zip(sums, axes)]
    out_shape = []
    for s, ax, sz in zip(sums, axes, sizes):
        shp = list(s.shape)
        shp[ax] = sz
        out_shape.append(jax.ShapeDtypeStruct((3, *shp), s.dtype))

    def body(*refs):
        ins, outs, send, recv = refs[:n], refs[n:2 * n], refs[2 * n], refs[2 * n + 1]
        x, y, c = _place()

        def protocol(me):
            cps = []
            for w in range(n):
                for j, (px, py) in enumerate(_peers(me)):
                    cp = pltpu.make_async_remote_copy(_slab(ins[w], axes[w], sizes[w], 2 * px + py), outs[w].at[j],
                                                      send.at[3 * w + j], recv.at[3 * w + j],
                                                      device_id=(px, py, c), device_id_type=MESH)
                    cp.start()
                    cps.append(cp)
            for cp in cps:
                cp.wait()

        for me in range(4):
            pl.when(2 * x + y == me)(functools.partial(protocol, me))

    return pl.pallas_call(
        body, name=name, in_specs=_hbm_specs(n), out_specs=_hbm_specs(n), out_shape=out_shape,
        scratch_shapes=[pltpu.SemaphoreType.DMA((3 * n,)), pltpu.SemaphoreType.DMA((3 * n,))],
        compiler_params=pltpu.CompilerParams(has_side_effects=True),
    )(*sums)


def _add_chips(name, own, got, axis, kidx, tr):
    _, rs, cs = got.shape
    nb = rs // tr

    def body(k_ref, o_ref, g_ref, out_ref):
        out_ref[...] = (o_ref[...].astype(F32) + g_ref[0].astype(F32) + g_ref[1].astype(F32)
                        + g_ref[2].astype(F32))

    if axis == 0:
        own_spec = pl.BlockSpec((tr, cs), lambda i, k_ref: (k_ref[0] * nb + i, 0))
    else:
        own_spec = pl.BlockSpec((tr, cs), lambda i, k_ref: (i, k_ref[0]))
    return pl.pallas_call(
        body, name=name,
        grid_spec=pltpu.PrefetchScalarGridSpec(
            num_scalar_prefetch=1, grid=(nb,),
            in_specs=[own_spec, pl.BlockSpec((3, tr, cs), lambda i, k_ref: (0, i, 0))],
            out_specs=pl.BlockSpec((tr, cs), lambda i, k_ref: (i, 0))),
        out_shape=jax.ShapeDtypeStruct((rs, cs), F32),
        compiler_params=_params(("arbitrary",)),
    )(kidx, own, got)


def _join_halves(name, parts):
    n = len(parts)

    def body(*refs):
        ins, outs, send, recv, lsem = refs[:n], refs[n:2 * n], refs[2 * n], refs[2 * n + 1], refs[2 * n + 2]
        x, y, c = _place()
        local = [pltpu.make_async_copy(ins[w], outs[w].at[c], lsem.at[w]) for w in range(n)]
        cps = [pltpu.make_async_remote_copy(ins[w], outs[w].at[c], send.at[w], recv.at[w],
                                            device_id=(x, y, 1 - c), device_id_type=MESH) for w in range(n)]
        for cp in local + cps:
            cp.start()
        for w in range(n):
            cps[w].wait_send()
            pltpu.make_async_remote_copy(ins[w], outs[w].at[1 - c], send.at[w], recv.at[w],
                                         device_id=(x, y, 1 - c), device_id_type=MESH).wait_recv()
        for cp in local:
            cp.wait()

    return pl.pallas_call(
        body, name=name, in_specs=_hbm_specs(n), out_specs=_hbm_specs(n),
        out_shape=[jax.ShapeDtypeStruct((2, *p.shape), p.dtype) for p in parts],
        scratch_shapes=[pltpu.SemaphoreType.DMA((n,)), pltpu.SemaphoreType.DMA((n,)),
                        pltpu.SemaphoreType.DMA((n,))],
        compiler_params=pltpu.CompilerParams(has_side_effects=True),
    )(*parts)


def _all_reduce_small(name, pack, fold_rows, groups):
    nr, d = pack.shape

    def body(in_ref, sum_ref, meta_ref, slots, send, recv):
        x, y, c = _place()
        me = 4 * x + 2 * y + c
        slots[me] = in_ref[...]
        cps = []
        for r in range(1, 8):
            rx, ry, rc = r // 4, (r // 2) % 2, r % 2
            peer = (x + rx - 2 * x * rx, y + ry - 2 * y * ry, c + rc - 2 * c * rc)
            cp = pltpu.make_async_remote_copy(in_ref, slots.at[me], send.at[r - 1], recv.at[r - 1],
                                              device_id=peer, device_id_type=MESH)
            cp.start()
            cps.append(cp)
        for cp in cps:
            cp.wait()
        acc = slots[0]
        for dev in range(1, 8):
            acc = acc + slots[dev]
        sum_ref[...] = acc
        fold = acc[0:fold_rows]
        for grp in range(1, groups):
            fold = fold + acc[grp * fold_rows:(grp + 1) * fold_rows]
        meta_ref[...] = fold

    vmem = pl.BlockSpec(memory_space=pltpu.VMEM)
    return pl.pallas_call(
        body, name=name, in_specs=[vmem], out_specs=[vmem, vmem],
        out_shape=[jax.ShapeDtypeStruct((nr, d), F32), jax.ShapeDtypeStruct((fold_rows, d), F32)],
        scratch_shapes=[pltpu.VMEM((8, nr, d), F32), pltpu.SemaphoreType.DMA((7,)), pltpu.SemaphoreType.DMA((7,))],
        compiler_params=pltpu.CompilerParams(has_side_effects=True, vmem_limit_bytes=VMEM_LIMIT),
    )(pack)


def _adamw_math(w, g, m, v):
    m = B1 * m + (1.0 - B1) * g
    v = B2 * v + (1.0 - B2) * (g * g)
    m_hat = m / (1.0 - B1 ** STEP)
    v_hat = v / (1.0 - B2 ** STEP)
    return -LR * (m_hat / (jnp.sqrt(v_hat) + ADAM_EPS) + WD * w), m, v


def _adamw(name, w, g, m, v, tr):
    shape = w.shape
    flat = [a.reshape(-1, shape[-1]) for a in (w, g, m, v)]
    r, cdim = flat[0].shape

    def body(w_ref, g_ref, m_ref, v_ref, d_ref, nm_ref, nv_ref):
        d_ref[...], nm_ref[...], nv_ref[...] = _adamw_math(w_ref[...], g_ref[...], m_ref[...], v_ref[...])

    spec = pl.BlockSpec((tr, cdim), lambda i: (i, 0))
    outs = pl.pallas_call(
        body, name=name, grid=(r // tr,), in_specs=[spec] * 4, out_specs=[spec] * 3,
        out_shape=[jax.ShapeDtypeStruct((r, cdim), F32)] * 3,
        compiler_params=_params(("parallel",)),
    )(*flat)
    return [o.reshape(shape) for o in outs]


def _adamw_small(name, groups):
    n = len(groups)
    shapes = [grp[0].shape for grp in groups]
    flat = [a.reshape(-1, a.shape[-1]) for grp in groups for a in grp]

    def body(*refs):
        ins, outs = refs[:4 * n], refs[4 * n:]
        for i in range(n):
            w_ref, g_ref, m_ref, v_ref = ins[4 * i:4 * i + 4]
            outs[3 * i][...], outs[3 * i + 1][...], outs[3 * i + 2][...] = _adamw_math(
                w_ref[...], g_ref[...], m_ref[...], v_ref[...])

    vmem = pl.BlockSpec(memory_space=pltpu.VMEM)
    out_shape = [jax.ShapeDtypeStruct(flat[4 * i].shape, F32) for i in range(n) for _ in range(3)]
    outs = pl.pallas_call(body, name=name, in_specs=[vmem] * (4 * n), out_specs=[vmem] * (3 * n),
                          out_shape=out_shape)(*flat)
    return [[outs[3 * i + j].reshape(shapes[i]) for j in range(3)] for i in range(n)]


def _block_diag(w_grp):
    g, pg, _ = w_grp.shape
    eye = jnp.eye(g, dtype=w_grp.dtype)
    return (eye[:, None, :, None] * w_grp[:, :, None, :]).reshape(g * pg, g * pg)


def _diag_blocks(m, g):
    pg = m.shape[0] // g
    return jnp.stack([m[i * pg:(i + 1) * pg, i * pg:(i + 1) * pg] for i in range(g)])


def _local_step(x, meta, g_mix, w_in, w_conv, w_pool, pool_scale, w_out, g_mlp, w_up, w_down, g_final, target):
    bl, s, d = x.shape
    depth = g_mix.shape[0]
    lp = PAD + N_META + s
    t = bl * lp
    tm = lp // 2
    rt = lp // 4
    cw = w_conv.shape[2]
    ngrp = w_pool.shape[1]
    ident = lambda acc: (acc,)

    h = jnp.concatenate([jnp.zeros((bl, PAD, d), F32), jnp.broadcast_to(meta[None], (bl, N_META, d)), x],
                        axis=1).reshape(t, d)
    wbd = [_block_diag(w_pool[i]).astype(BF16) for i in range(depth)]
    saved = []
    for i in range(depth):
        hn = _rms_fwd(f"mix_norm{i}", h, g_mix[i], tm)
        (u_cp,) = _mm_nn(f"in_proj_cp{i}", [hn], w_in, i, 0, 4 * cw, tm, 512, ident, [], [F32])
        (qkv,) = _mm_nn(f"in_proj_qkv{i}", [hn], w_in, i, 4 * cw, w_in.shape[2] - 4 * cw, tm, 512, ident, [],
                        [BF16])
        y_cp = _convpool_fwd(f"convpool{i}", u_cp, w_conv[i], wbd[i], pool_scale[i:i + 1], lp, rt)
        y_at, lt = _attn_fwd(f"attn{i}", qkv, bl, lp)
        (h_mid,) = _mm_nn(f"out_proj{i}", [y_cp, y_at], w_out, i, 0, d, tm, 512, lambda acc, res: (acc + res,),
                          [h], [F32])
        hn2 = _rms_fwd(f"mlp_norm{i}", h_mid, g_mlp[i], tm)
        m_pre, act = _mm_nn(f"up_proj{i}", [hn2], w_up, i, 0, w_up.shape[2], tm, 512,
                            lambda acc: (acc, jnp.square(jnp.maximum(acc, 0.0))), [], [BF16, BF16])
        (h_next,) = _mm_nn(f"down_proj{i}", [act], w_down, i, 0, d, tm, 512, lambda acc, res: (acc + res,),
                           [h_mid], [F32])
        saved.append((h, hn, u_cp, qkv, y_cp, y_at, lt, h_mid, hn2, m_pre, act))
        h = h_next

    dh, loss8, dgf8 = _loss_bwd("loss", h, g_final, target, lp)
    grads = {"g_final": dgf8.sum(0)}
    per_layer = {k: [] for k in ("g_mix", "w_in", "w_conv", "w_pool", "pool_scale", "w_out", "g_mlp", "w_up",
                                 "w_down")}
    for i in reversed(range(depth)):
        h_in, hn, u_cp, qkv, y_cp, y_at, lt, h_mid, hn2, m_pre, act = saved[i]
        (dm,) = _mm_nt(f"down_proj_dx{i}", [dh], w_down, i, tm, 512,
                       lambda acc, mp: (acc * (2.0 * jnp.maximum(mp.astype(F32), 0.0)),), [m_pre], [BF16])
        per_layer["w_down"].append(_mm_tn(f"down_proj_dw{i}", act, dh, tm, 1024, 1024))
        per_layer["w_up"].append(_mm_tn(f"up_proj_dw{i}", hn2, dm, tm, 1024, 1024))
        (dhn2,) = _mm_nt(f"up_proj_dx{i}", [dm], w_up, i, tm, 512, ident, [], [F32])
        dh_mid, dg8 = _rms_bwd(f"mlp_norm_bwd{i}", h_mid, dhn2, g_mlp[i], dh, tm)
        per_layer["g_mlp"].append(dg8.sum(0))
        (dy,) = _mm_nt(f"out_proj_dx{i}", [dh_mid], w_out, i, tm, 512, ident, [], [F32])
        per_layer["w_out"].append(jnp.concatenate(
            [_mm_tn(f"out_proj_dw_cp{i}", y_cp, dh_mid, tm, 512, 1024),
             _mm_tn(f"out_proj_dw_at{i}", y_at, dh_mid, tm, 512, 1024)], axis=0))
        dq, dk, dv = _attn_bwd(f"attn_bwd{i}", qkv, lt, dy, bl, lp)
        du_cp, sm, dwbd = _convpool_bwd(f"convpool_bwd{i}", u_cp, dy, w_conv[i], wbd[i], pool_scale[i:i + 1], lp,
                                        rt)
        sm = sm.reshape(4, 8, cw).sum(1)
        per_layer["w_conv"].append(sm[0:3])
        per_layer["pool_scale"].append(sm[3])
        per_layer["w_pool"].append(_diag_blocks(dwbd, ngrp))
        dus = [du_cp, dq, dk, dv]
        per_layer["w_in"].append(jnp.concatenate(
            [_mm_tn(f"in_proj_dw{j}_{i}", hn, du, tm, 1024, du.shape[1]) for j, du in enumerate(dus)], axis=1))
        (dhn,) = _mm_nt(f"in_proj_dx{i}", dus, w_in, i, tm, 512, ident, [], [F32])
        dh, dg8 = _rms_bwd(f"mix_norm_bwd{i}", h_in, dhn, g_mix[i], dh_mid, tm)
        per_layer["g_mix"].append(dg8.sum(0))
    for k, v in per_layer.items():
        grads[k] = jnp.stack(v[::-1])
    return loss8, dh, grads


def kernel(x, meta_tokens, g_mix, w_in, w_conv, w_pool, pool_scale, w_out, g_mlp, w_up, w_down, g_final, loss_target, m_meta_tokens, m_g_mix, m_w_in, m_w_conv, m_w_pool, m_pool_scale, m_w_out, m_g_mlp, m_w_up, m_w_down, m_g_final, v_meta_tokens, v_g_mix, v_w_in, v_w_conv, v_w_pool, v_pool_scale, v_w_out, v_g_mlp, v_w_up, v_w_down, v_g_final):
    bl, s, d = x.shape
    lp = PAD + N_META + s
    xi, yi, ci = _place()
    chip = (2 * xi + yi).astype(jnp.int32)
    cidx, kidx = ci.astype(jnp.int32).reshape(1), chip.reshape(1)

    big = {"w_in": (w_in, 1), "w_out": (w_out, 0), "w_up": (w_up, 1), "w_down": (w_down, 0)}
    f_in, f_out, f_up, f_down = _all_gather("gather_weights", [big[k][0].astype(BF16) for k in big],
                                            [big[k][1] for k in big])
    cs = w_conv.shape[2]
    placed = jnp.zeros((32, d), F32)
    placed = lax.dynamic_update_slice(placed, meta_tokens, (0, chip * meta_tokens.shape[1]))
    placed = lax.dynamic_update_slice(placed, w_conv.reshape(-1, cs), (N_META, chip * cs))
    placed = jnp.where(ci == 0, placed, 0.0)
    whole, _ = _all_reduce_small("gather_small", placed, 8, 1)
    meta_full = whole[:N_META]
    conv_full = whole[N_META:N_META + 2 * 3, :4 * cs].reshape(2, 3, 4 * cs)

    loss8, dh0, grads = _local_step(x, meta_full, g_mix, f_in, conv_full, w_pool, pool_scale, f_out, g_mlp, f_up,
                                    f_down, g_final, loss_target)
    loss = lax.psum(jnp.sum(loss8), ("x", "y", "c"))
    dh0 = dh0.reshape(bl, lp, d)
    grad_x = dh0[:, PAD + N_META:]

    names = list(big)
    axes = [big[k][1] for k in names]
    from_sib = _swap_halves("grads_to_sibling", [grads[k] for k in names])
    chip_sums = [_add_half(f"chip_sum_{k}", grads[k], got, cidx, 256) for k, got in zip(names, from_sib)]
    from_chips = _scatter_chips("grads_to_chips", chip_sums, axes)
    reduced = [_add_chips(f"reduce_{k}", own, got, ax, kidx, 256)
               for k, own, got, ax in zip(names, chip_sums, from_chips, axes)]
    big_grads = dict(zip(names, _join_halves("grads_join", reduced)))

    cw = w_conv.shape[2] * 4
    pieces = [dh0[:, PAD:PAD + N_META].reshape(bl * N_META, d), grads["g_mix"], grads["g_mlp"],
              grads["g_final"].reshape(1, d),
              jnp.pad(grads["w_conv"].reshape(-1), (0, 2 * d - grads["w_conv"].size)).reshape(2, d),
              jnp.pad(grads["pool_scale"].reshape(-1), (0, d - grads["pool_scale"].size)).reshape(1, d),
              grads["w_pool"].reshape(-1, d)]
    pack = jnp.concatenate(pieces, axis=0)
    summed, meta_sum = _all_reduce_small("small_grads", pack, N_META, bl)
    o = bl * N_META
    g_small = {
        "meta_tokens": lax.dynamic_slice_in_dim(meta_sum, chip * meta_tokens.shape[1], meta_tokens.shape[1], 1),
        "g_mix": summed[o:o + 2], "g_mlp": summed[o + 2:o + 4], "g_final": summed[o + 4],
        "w_conv": lax.dynamic_slice_in_dim(summed[o + 5:o + 7].reshape(-1)[:2 * 3 * cw].reshape(2, 3, cw),
                                           chip * w_conv.shape[2], w_conv.shape[2], 2),
        "pool_scale": summed[o + 7].reshape(-1)[:pool_scale.size].reshape(pool_scale.shape),
        "w_pool": summed[o + 8:].reshape(w_pool.shape),
    }

    weights = dict(meta_tokens=meta_tokens, g_mix=g_mix, w_in=w_in, w_conv=w_conv, w_pool=w_pool,
                   pool_scale=pool_scale, w_out=w_out, g_mlp=g_mlp, w_up=w_up, w_down=w_down, g_final=g_final)
    ms = dict(meta_tokens=m_meta_tokens, g_mix=m_g_mix, w_in=m_w_in, w_conv=m_w_conv, w_pool=m_w_pool,
              pool_scale=m_pool_scale, w_out=m_w_out, g_mlp=m_g_mlp, w_up=m_w_up, w_down=m_w_down,
              g_final=m_g_final)
    vs = dict(meta_tokens=v_meta_tokens, g_mix=v_g_mix, w_in=v_w_in, w_conv=v_w_conv, w_pool=v_w_pool,
              pool_scale=v_pool_scale, w_out=v_w_out, g_mlp=v_g_mlp, w_up=v_w_up, w_down=v_w_down,
              g_final=v_g_final)
    order = list(weights)
    grad = {**g_small, **big_grads}
    upd = {}
    for k in names:
        upd[k] = _adamw(f"adamw_{k}", weights[k], grad[k], ms[k], vs[k], 256)
    small = [k for k in order if k not in big]
    for k, res in zip(small, _adamw_small("adamw_small", [(weights[k], grad[k].reshape(weights[k].shape), ms[k],
                                                           vs[k]) for k in small])):
        upd[k] = res
    grad = {k: grad[k].reshape(weights[k].shape) for k in order}
    return (loss, grad_x, *[grad[k] for k in order], *[upd[k][0] for k in order], *[upd[k][1] for k in order],
            *[upd[k][2] for k in order])
```

```python
import functools

import jax
import jax.numpy as jnp
from jax import lax
from jax.experimental import pallas as pl
from jax.experimental.pallas import tpu as pltpu

F32, BF16 = jnp.float32, jnp.bfloat16
MESH = pl.DeviceIdType.MESH
EPS = 1e-6
N_META = 16
QB = 128
PAD = QB - N_META
HALO = 16
POOL_WINDOWS = (2.0, 4.0, 8.0, 16.0)
HEAD_SCALE = 0.125
LR, B1, B2, ADAM_EPS, WD, STEP = 0.001, 0.9, 0.999, 1e-08, 0.01, 10
VMEM_LIMIT = 56 * 1024 * 1024


def _params(sem=None):
    return pltpu.CompilerParams(dimension_semantics=sem, vmem_limit_bytes=VMEM_LIMIT)


def _nt(a, b):
    return lax.dot_general(a, b, (((1,), (1,)), ((), ())), preferred_element_type=F32)


def _tn(a, b):
    return lax.dot_general(a, b, (((0,), (0,)), ((), ())), preferred_element_type=F32)


def _nn(a, b):
    return jnp.dot(a, b, preferred_element_type=F32)


def _fold8(v):
    r, c = v.shape
    return jnp.sum(v.reshape(r // 8, 8, c), axis=0)


def _mm_nn(name, a_list, w, layer, n_off, n, tm, tn, epi, extras, out_dtypes):
    m = a_list[0].shape[0]
    kdim = w.shape[1]
    ks = [a.shape[1] for a in a_list]
    assert sum(ks) == kdim and n_off % tn == 0 and n % tn == 0 and m % tm == 0
    na, ne = len(a_list), len(extras)

    def body(*refs):
        a_refs, w_ref = refs[:na], refs[na]
        e_refs, o_refs = refs[na + 1:na + 1 + ne], refs[na + 1 + ne:]
        acc, off = None, 0
        for a_ref, k in zip(a_refs, ks):
            part = _nn(a_ref[...].astype(BF16), w_ref[off:off + k, :])
            acc = part if acc is None else acc + part
            off += k
        outs = epi(acc, *[e[...] for e in e_refs])
        for o_ref, o in zip(o_refs, outs):
            o_ref[...] = o.astype(o_ref.dtype)

    in_specs = [pl.BlockSpec((tm, k), lambda i, j: (i, 0)) for k in ks]
    in_specs.append(pl.BlockSpec((None, kdim, tn), lambda i, j: (layer, 0, n_off // tn + j)))
    in_specs += [pl.BlockSpec((tm, tn), lambda i, j: (i, j)) for _ in extras]
    return pl.pallas_call(
        body, name=name, grid=(m // tm, n // tn), in_specs=in_specs,
        out_specs=[pl.BlockSpec((tm, tn), lambda i, j: (i, j)) for _ in out_dtypes],
        out_shape=[jax.ShapeDtypeStruct((m, n), d) for d in out_dtypes],
        compiler_params=_params(("parallel", "arbitrary")),
    )(*a_list, w, *extras)


def _mm_nt(name, a_list, w, layer, tm, tko, epi, extras, out_dtypes):
    m = a_list[0].shape[0]
    kout, ntot = w.shape[1], w.shape[2]
    ns = [a.shape[1] for a in a_list]
    assert sum(ns) == ntot and kout % tko == 0 and m % tm == 0
    na, ne = len(a_list), len(extras)

    def body(*refs):
        a_refs, w_ref = refs[:na], refs[na]
        e_refs, o_refs = refs[na + 1:na + 1 + ne], refs[na + 1 + ne:]
        acc, off = None, 0
        for a_ref, k in zip(a_refs, ns):
            part = _nt(a_ref[...].astype(BF16), w_ref[:, off:off + k])
            acc = part if acc is None else acc + part
            off += k
        outs = epi(acc, *[e[...] for e in e_refs])
        for o_ref, o in zip(o_refs, outs):
            o_ref[...] = o.astype(o_ref.dtype)

    in_specs = [pl.BlockSpec((tm, k), lambda i, j: (i, 0)) for k in ns]
    in_specs.append(pl.BlockSpec((None, tko, ntot), lambda i, j: (layer, j, 0)))
    in_specs += [pl.BlockSpec((tm, tko), lambda i, j: (i, j)) for _ in extras]
    return pl.pallas_call(
        body, name=name, grid=(m // tm, kout // tko), in_specs=in_specs,
        out_specs=[pl.BlockSpec((tm, tko), lambda i, j: (i, j)) for _ in out_dtypes],
        out_shape=[jax.ShapeDtypeStruct((m, kout), d) for d in out_dtypes],
        compiler_params=_params(("parallel", "arbitrary")),
    )(*a_list, w, *extras)


def _mm_tn(name, a, b, tt, tka, tn):
    t, ka = a.shape
    n = b.shape[1]
    assert t % tt == 0 and ka % tka == 0 and n % tn == 0

    def body(a_ref, b_ref, o_ref):
        @pl.when(pl.program_id(2) == 0)
        def _():
            o_ref[...] = jnp.zeros_like(o_ref)
        o_ref[...] += _tn(a_ref[...].astype(BF16), b_ref[...].astype(BF16))

    return pl.pallas_call(
        body, name=name, grid=(ka // tka, n // tn, t // tt),
        in_specs=[pl.BlockSpec((tt, tka), lambda i, j, s: (s, i)), pl.BlockSpec((tt, tn), lambda i, j, s: (s, j))],
        out_specs=pl.BlockSpec((tka, tn), lambda i, j, s: (i, j)),
        out_shape=jax.ShapeDtypeStruct((ka, n), F32),
        compiler_params=_params(("parallel", "parallel", "arbitrary")),
    )(a, b)


def _rms_fwd(name, h, g, tm):
    t, d = h.shape

    def body(h_ref, g_ref, o_ref):
        x = h_ref[...]
        r = lax.rsqrt(jnp.mean(x * x, axis=-1, keepdims=True) + EPS)
        o_ref[...] = (x * r * g_ref[...]).astype(o_ref.dtype)

    return pl.pallas_call(
        body, name=name, grid=(t // tm,),
        in_specs=[pl.BlockSpec((tm, d), lambda i: (i, 0)), pl.BlockSpec((1, d), lambda i: (0, 0))],
        out_specs=pl.BlockSpec((tm, d), lambda i: (i, 0)),
        out_shape=jax.ShapeDtypeStruct((t, d), BF16),
        compiler_params=_params(("parallel",)),
    )(h, g.reshape(1, d))


def _rms_bwd(name, x, dyn, g, resid, tm):
    t, d = x.shape

    def body(x_ref, dy_ref, g_ref, r_ref, o_ref, dg_ref):
        @pl.when(pl.program_id(0) == 0)
        def _():
            dg_ref[...] = jnp.zeros_like(dg_ref)
        xv, dy = x_ref[...], dy_ref[...]
        r = lax.rsqrt(jnp.mean(xv * xv, axis=-1, keepdims=True) + EPS)
        xh = xv * r
        w = dy * g_ref[...]
        o_ref[...] = r_ref[...] + r * (w - xh * jnp.mean(w * xh, axis=-1, keepdims=True))
        dg_ref[...] += _fold8(dy * xh)

    row = pl.BlockSpec((tm, d), lambda i: (i, 0))
    return pl.pallas_call(
        body, name=name, grid=(t // tm,),
        in_specs=[row, row, pl.BlockSpec((1, d), lambda i: (0, 0)), row],
        out_specs=[row, pl.BlockSpec((8, d), lambda i: (0, 0))],
        out_shape=[jax.ShapeDtypeStruct((t, d), F32), jax.ShapeDtypeStruct((8, d), F32)],
        compiler_params=_params(("arbitrary",)),
    )(x, dyn, g.reshape(1, d), resid)


def _loss_bwd(name, h, g, target, lp):
    t, d = h.shape
    bl = target.shape[0]
    nq = lp // QB

    def body(h_ref, g_ref, t_ref, dh_ref, ls_ref, dg_ref):
        b, j = pl.program_id(0), pl.program_id(1)

        @pl.when((b == 0) & (j == 0))
        def _():
            ls_ref[...] = jnp.zeros_like(ls_ref)
            dg_ref[...] = jnp.zeros_like(dg_ref)
        xv = h_ref[...]
        r = lax.rsqrt(jnp.mean(xv * xv, axis=-1, keepdims=True) + EPS)
        xh = xv * r
        gv = g_ref[...]
        err = jnp.where(j >= 1, xh * gv - t_ref[...], 0.0)
        ls_ref[...] += _fold8(err * err) * (0.5 / d)
        dy = err * (1.0 / d)
        w = dy * gv
        dh_ref[...] = r * (w - xh * jnp.mean(w * xh, axis=-1, keepdims=True))
        dg_ref[...] += _fold8(dy * xh)

    return pl.pallas_call(
        body, name=name, grid=(bl, nq),
        in_specs=[pl.BlockSpec((QB, d), lambda b, j: (b * nq + j, 0)), pl.BlockSpec((1, d), lambda b, j: (0, 0)),
                  pl.BlockSpec((None, QB, d), lambda b, j: (b, jnp.maximum(j - 1, 0), 0))],
        out_specs=[pl.BlockSpec((QB, d), lambda b, j: (b * nq + j, 0)), pl.BlockSpec((8, d), lambda b, j: (0, 0)),
                   pl.BlockSpec((8, d), lambda b, j: (0, 0))],
        out_shape=[jax.ShapeDtypeStruct((t, d), F32), jax.ShapeDtypeStruct((8, d), F32),
                   jax.ShapeDtypeStruct((8, d), F32)],
        compiler_params=_params(("arbitrary", "arbitrary")),
    )(h, g.reshape(1, d), target)


def _pool_select(grp, a2, a4, a8, a16):
    return jnp.where(grp == 0, a2, jnp.where(grp == 1, a4, jnp.where(grp == 2, a8, a16)))


def _trailing_sums(v):
    s2 = v + pltpu.roll(v, 1, 0)
    s4 = s2 + pltpu.roll(s2, 2, 0)
    s8 = s4 + pltpu.roll(s4, 4, 0)
    s16 = s8 + pltpu.roll(s8, 8, 0)
    return s2, s4, s8, s16


def _leading_sums(v):
    n = v.shape[0]
    s2 = v + pltpu.roll(v, n - 1, 0)
    s4 = s2 + pltpu.roll(s2, n - 2, 0)
    s8 = s4 + pltpu.roll(s4, n - 4, 0)
    s16 = s8 + pltpu.roll(s8, n - 8, 0)
    return s2, s4, s8, s16


def _convpool_fwd(name, u_cp, wconv, wbd, pscale, lp, r):
    t = u_cp.shape[0]
    cw = u_cp.shape[1] // 4
    tps, hb = lp // r, r // HALO

    def body(cb_ref, cc_ref, cx_ref, pi_ref, cch_ref, cxh_ref, pih_ref, wc_ref, wbd_ref, ps_ref, y_ref):
        i = pl.program_id(0)
        lrow = (i % tps) * r + lax.broadcasted_iota(jnp.int32, (r, 1), 0)
        valid = lrow >= PAD
        xx = jnp.concatenate([cch_ref[...] * cxh_ref[...], cc_ref[...] * cx_ref[...]], axis=0)
        conv = (wc_ref[0:1, :] * pltpu.roll(xx, 2, 0) + wc_ref[1:2, :] * pltpu.roll(xx, 1, 0)
                + wc_ref[2:3, :] * xx)
        y_ref[:, 0:cw] = (cb_ref[...] * conv[HALO:]).astype(y_ref.dtype)
        p = pi_ref[...]
        grp = lax.broadcasted_iota(jnp.int32, (1, cw), 1) // (cw // 4)
        sel = _pool_select(grp, *_trailing_sums(jnp.concatenate([pih_ref[...], p], axis=0)))[HALO:]
        cnt = jnp.maximum(jnp.minimum((lrow - (PAD - 1)).astype(F32), _pool_select(grp, *POOL_WINDOWS)), 1.0)
        pooled = jnp.where(valid, sel / cnt - p, 0.0)
        y_ref[:, cw:2 * cw] = (_nn(pooled.astype(BF16), wbd_ref[...]) * ps_ref[...]).astype(y_ref.dtype)

    def main(col):
        return pl.BlockSpec((r, cw), lambda i: (i, col))

    def prev(col):
        return pl.BlockSpec((HALO, cw), lambda i: (jnp.maximum(i * hb - 1, 0), col))

    def whole(a):
        return pl.BlockSpec(a.shape, lambda i: (0, 0))

    return pl.pallas_call(
        body, name=name, grid=(t // r,),
        in_specs=[main(0), main(1), main(2), main(3), prev(1), prev(2), prev(3), whole(wconv), whole(wbd),
                  whole(pscale)],
        out_specs=pl.BlockSpec((r, 2 * cw), lambda i: (i, 0)),
        out_shape=jax.ShapeDtypeStruct((t, 2 * cw), BF16),
        compiler_params=_params(("parallel",)),
    )(u_cp, u_cp, u_cp, u_cp, u_cp, u_cp, u_cp, wconv, wbd, pscale)


def _convpool_bwd(name, u_cp, dy, wconv, wbd, pscale, lp, r):
    t = u_cp.shape[0]
    cw = u_cp.shape[1] // 4
    tps, hb = lp // r, r // HALO
    e = r + HALO

    def body(cb_ref, cc_ref, cx_ref, pi_ref, cbn_ref, cch_ref, cxh_ref, pih_ref, dyc_ref, dyp_ref, dycn_ref,
             dypn_ref, wc_ref, wbd_ref, ps_ref, du_ref, sm_ref, dwbd_ref):
        i = pl.program_id(0)

        @pl.when(i == 0)
        def _():
            sm_ref[...] = jnp.zeros_like(sm_ref)
            dwbd_ref[...] = jnp.zeros_like(dwbd_ref)
        lrow_e = (i % tps) * r + lax.broadcasted_iota(jnp.int32, (e, 1), 0)
        valid_e = (lrow_e >= PAD) & (lrow_e < lp)
        lrow, valid = lrow_e[:r], lrow_e[:r] >= PAD
        w0, w1, w2 = wc_ref[0:1, :], wc_ref[1:2, :], wc_ref[2:3, :]
        cb, cc, cx = cb_ref[...], cc_ref[...], cx_ref[...]
        prod = cc * cx
        xx = jnp.concatenate([cch_ref[...] * cxh_ref[...], prod], axis=0)
        back1, back2 = pltpu.roll(xx, 1, 0)[HALO:], pltpu.roll(xx, 2, 0)[HALO:]
        dyc = dyc_ref[...]
        du_ref[:, 0:cw] = (dyc * (w0 * back2 + w1 * back1 + w2 * prod)).astype(du_ref.dtype)
        dconv_e = jnp.where(valid_e, jnp.concatenate([dyc * cb, dycn_ref[...] * cbn_ref[...]], axis=0), 0.0)
        dconv = dconv_e[:r]
        dprod = (w2 * dconv + w1 * pltpu.roll(dconv_e, e - 1, 0)[:r] + w0 * pltpu.roll(dconv_e, e - 2, 0)[:r])
        du_ref[:, cw:2 * cw] = (dprod * cx).astype(du_ref.dtype)
        du_ref[:, 2 * cw:3 * cw] = (dprod * cc).astype(du_ref.dtype)
        sm_ref[0:8, :] += _fold8(dconv * back2)
        sm_ref[8:16, :] += _fold8(dconv * back1)
        sm_ref[16:24, :] += _fold8(dconv * prod)
        p = pi_ref[...]
        grp = lax.broadcasted_iota(jnp.int32, (1, cw), 1) // (cw // 4)
        win = _pool_select(grp, *POOL_WINDOWS)
        sel = _pool_select(grp, *_trailing_sums(jnp.concatenate([pih_ref[...], p], axis=0)))[HALO:]
        cnt_e = jnp.maximum(jnp.minimum((lrow_e - (PAD - 1)).astype(F32), win), 1.0)
        pooled = jnp.where(valid, sel / cnt_e[:r] - p, 0.0).astype(BF16)
        dyp = dyp_ref[...]
        sm_ref[24:32, :] += _fold8(dyp * _nn(pooled, wbd_ref[...]))
        dpre_e = (jnp.concatenate([dyp, dypn_ref[...]], axis=0) * ps_ref[...]).astype(BF16)
        dwbd_ref[...] += _tn(pooled, dpre_e[:r])
        dpooled_e = jnp.where(valid_e, _nt(dpre_e, wbd_ref[...]), 0.0)
        ahead = _pool_select(grp, *_leading_sums(dpooled_e / cnt_e))[:r]
        du_ref[:, 3 * cw:4 * cw] = (ahead - dpooled_e[:r]).astype(du_ref.dtype)

    last_halo = t // HALO - 1

    def main(col):
        return pl.BlockSpec((r, cw), lambda i: (i, col))

    def prev(col):
        return pl.BlockSpec((HALO, cw), lambda i: (jnp.maximum(i * hb - 1, 0), col))

    def nxt(col):
        return pl.BlockSpec((HALO, cw), lambda i: (jnp.minimum((i + 1) * hb, last_halo), col))

    def whole(a):
        return pl.BlockSpec(a.shape, lambda i: (0, 0))

    return pl.pallas_call(
        body, name=name, grid=(t // r,),
        in_specs=[main(0), main(1), main(2), main(3), nxt(0), prev(1), prev(2), prev(3), main(0), main(1), nxt(0),
                  nxt(1), whole(wconv), whole(wbd), whole(pscale)],
        out_specs=[pl.BlockSpec((r, 4 * cw), lambda i: (i, 0)), pl.BlockSpec((32, cw), lambda i: (0, 0)),
                   pl.BlockSpec((cw, cw), lambda i: (0, 0))],
        out_shape=[jax.ShapeDtypeStruct((t, 4 * cw), BF16), jax.ShapeDtypeStruct((32, cw), F32),
                   jax.ShapeDtypeStruct((cw, cw), F32)],
        compiler_params=_params(("arbitrary",)),
    )(u_cp, u_cp, u_cp, u_cp, u_cp, u_cp, u_cp, u_cp, dy, dy, dy, dy, wconv, wbd, pscale)


KW = 2 * QB


def _split2(v):
    hi = v.astype(BF16)
    return hi, (v - hi.astype(F32)).astype(BF16)


def _cumsum_matrix(before):
    r = lax.broadcasted_iota(jnp.int32, (KW, KW + QB), 0)
    c = lax.broadcasted_iota(jnp.int32, (KW, KW + QB), 1)
    return (((r < c) if before else (r > c)) | (c >= KW)).astype(BF16)


def _sums(v, mat):
    hi, lo = _split2(v)
    ext = _nn(hi, mat) + _nn(lo, mat)
    return ext[:, :KW], ext[:, KW:]


def _stack_heads(v, head0):
    zero = jnp.zeros_like(v)
    return jnp.concatenate([jnp.where(head0, v, zero), jnp.where(head0, zero, v)], axis=0)


def _attn_fwd(name, qkv, bl, lp):
    t = qkv.shape[0]
    nq, npair = lp // QB, qkv.shape[1] // (3 * QB)

    def body(q_ref, k_ref, v_ref, o_ref, lt_ref):
        qi = pl.program_id(2)
        head0 = lax.broadcasted_iota(jnp.int32, (QB, QB), 1) < QB // 2
        q2 = _stack_heads(q_ref[...] * jnp.asarray(HEAD_SCALE, BF16), head0)
        later = _cumsum_matrix(before=False)
        q_pos = qi * QB + (lax.broadcasted_iota(jnp.int32, (2 * QB, KW), 0) & (QB - 1))
        col = lax.broadcasted_iota(jnp.int32, (2 * QB, KW), 1)
        ng = qi // 2 + 1

        def group(g, carry, masked):
            run, acc = carry
            start = pl.multiple_of(jnp.minimum(g * KW, lp - KW) if masked else g * KW, QB)
            kg, vg = k_ref[pl.ds(start, KW), :], v_ref[pl.ds(start, KW), :]
            z = _nt(q2, kg)
            logp = jnp.minimum(z, 0.0) - jnp.log(1.0 + jnp.exp(-jnp.abs(z)))
            lk = logp - z
            if masked:
                k_pos = start + col
                valid = (k_pos < q_pos) & (k_pos >= PAD) & (k_pos >= g * KW)
                lk = jnp.where(valid, lk, 0.0)
            after, rs = _sums(lk, later)
            a = jnp.exp(logp + after + jnp.concatenate([run, run], axis=1))
            if masked:
                a = jnp.where(valid, a, 0.0)
            return run + rs, acc + _nn(a.astype(BF16), vg)

        zero = jnp.zeros((2 * QB, QB), F32)
        carry = group(ng - 1, (zero, zero), True)
        carry = lax.fori_loop(0, jnp.maximum(ng - 2, 0), lambda i, c: group(ng - 2 - i, c, False), carry)
        run, acc = lax.fori_loop(0, jnp.minimum(ng - 1, 1), lambda i, c: group(0, c, True), carry)
        o_ref[...] = jnp.where(head0, acc[:QB], acc[QB:]).astype(o_ref.dtype)
        lt_ref[...] = jnp.where(head0, run[:QB], run[QB:])

    blk = pl.BlockSpec((QB, QB), lambda b, p, i: (b * nq + i, p))
    return pl.pallas_call(
        body, name=name, grid=(bl, npair, nq),
        in_specs=[blk, pl.BlockSpec((lp, QB), lambda b, p, i: (b, npair + p)),
                  pl.BlockSpec((lp, QB), lambda b, p, i: (b, 2 * npair + p))],
        out_specs=[blk, blk],
        out_shape=[jax.ShapeDtypeStruct((t, npair * QB), BF16), jax.ShapeDtypeStruct((t, npair * QB), F32)],
        compiler_params=_params(("parallel", "parallel", "arbitrary")),
    )(qkv, qkv, qkv)


def _attn_bwd(name, qkv, lt, dy, bl, lp):
    t = qkv.shape[0]
    nq, npair = lp // QB, qkv.shape[1] // (3 * QB)

    def body(q_ref, k_ref, v_ref, lt_ref, do_ref, dq_ref, dk_ref, dv_ref, dk_acc, dv_acc):
        qi = pl.program_id(2)

        @pl.when(qi == 0)
        def _():
            dk_acc[...] = jnp.zeros_like(dk_acc)
            dv_acc[...] = jnp.zeros_like(dv_acc)
        head0 = lax.broadcasted_iota(jnp.int32, (QB, QB), 1) < QB // 2
        q2 = _stack_heads(q_ref[...] * jnp.asarray(HEAD_SCALE, BF16), head0)
        do2 = _stack_heads(do_ref[...].astype(BF16), head0)
        ltv = lt_ref[...]
        swapped = pltpu.roll(ltv, QB // 2, 1)
        total = jnp.concatenate([jnp.where(head0, ltv, swapped), jnp.where(head0, swapped, ltv)], axis=0)
        later, earlier = _cumsum_matrix(before=False), _cumsum_matrix(before=True)
        q_pos = qi * QB + (lax.broadcasted_iota(jnp.int32, (2 * QB, KW), 0) & (QB - 1))
        col = lax.broadcasted_iota(jnp.int32, (2 * QB, KW), 1)
        ng = qi // 2 + 1

        def group(g, carry, masked):
            seen, gsum, dq = carry
            start = pl.multiple_of(jnp.minimum(g * KW, lp - KW) if masked else g * KW, QB)
            kg, vg = k_ref[pl.ds(start, KW), :], v_ref[pl.ds(start, KW), :]
            z = _nt(q2, kg)
            ez = jnp.exp(-jnp.abs(z))
            den = 1.0 + ez
            logp = jnp.minimum(z, 0.0) - jnp.log(den)
            lk = logp - z
            if masked:
                k_pos = start + col
                valid = (k_pos < q_pos) & (k_pos >= PAD) & (k_pos >= g * KW)
                lk = jnp.where(valid, lk, 0.0)
            after, rs = _sums(lk, later)
            newer = total - seen - rs
            a = jnp.exp(logp + after + jnp.concatenate([newer, newer], axis=1))
            if masked:
                a = jnp.where(valid, a, 0.0)
            gg = a * _nt(do2, vg)
            before, grs = _sums(gg, earlier)
            sig = jnp.where(z >= 0.0, 1.0, ez) / den
            dz = gg - (gg + before + jnp.concatenate([gsum, gsum], axis=1)) * sig
            if masked:
                dz = jnp.where(valid, dz, 0.0)
            dz = dz.astype(BF16)
            dk_acc[pl.ds(start, KW), :] += _tn(dz, q2)
            dv_acc[pl.ds(start, KW), :] += _tn(a.astype(BF16), do2)
            return seen + rs, gsum + grs, dq + _nn(dz, kg)

        zero = jnp.zeros((2 * QB, QB), F32)
        carry = group(0, (zero, zero, zero), True)
        carry = lax.fori_loop(1, ng - 1, lambda g, c: group(g, c, False), carry)
        res = lax.fori_loop(0, jnp.minimum(ng - 1, 1), lambda i, c: group(ng - 1, c, True), carry)
        dq_ref[...] = (jnp.where(head0, res[2][:QB], res[2][QB:]) * HEAD_SCALE).astype(dq_ref.dtype)

        @pl.when(qi == nq - 1)
        def _():
            dk_ref[...] = dk_acc[...].astype(dk_ref.dtype)
            dv_ref[...] = dv_acc[...].astype(dv_ref.dtype)

    blk = pl.BlockSpec((QB, QB), lambda b, p, i: (b * nq + i, p))
    seq = pl.BlockSpec((lp, QB), lambda b, p, i: (b, p))
    out = jax.ShapeDtypeStruct((t, npair * QB), BF16)
    return pl.pallas_call(
        body, name=name, grid=(bl, npair, nq),
        in_specs=[blk, pl.BlockSpec((lp, QB), lambda b, p, i: (b, npair + p)),
                  pl.BlockSpec((lp, QB), lambda b, p, i: (b, 2 * npair + p)), blk,
                  pl.BlockSpec((QB, QB), lambda b, p, i: (b * nq + i, npair + p))],
        out_specs=[blk, seq, seq],
        out_shape=[out, out, out],
        scratch_shapes=[pltpu.VMEM((lp, QB), F32), pltpu.VMEM((lp, QB), F32)],
        compiler_params=_params(("parallel", "parallel", "arbitrary")),
    )(qkv, qkv, qkv, lt, dy)


def _place():
    return lax.axis_index("x"), lax.axis_index("y"), lax.axis_index("c")


def _slab(ref, axis, size, chip, *lead):
    if axis == 0:
        return ref.at[(*lead, pl.ds(chip * size, size), slice(None))]
    return ref.at[(*lead, slice(None), pl.ds(chip * size, size))]


def _peers(chip):
    kx, ky = chip // 2, chip % 2
    return ((1 - kx, ky), (kx, 1 - ky), (1 - kx, 1 - ky))


def _hbm_specs(n):
    return [pl.BlockSpec(memory_space=pl.ANY) for _ in range(n)]


def _all_gather(name, shards, axes):
    n = len(shards)
    out_shape = []
    for a, ax in zip(shards, axes):
        shp = list(a.shape)
        shp[1 + ax] *= 4
        out_shape.append(jax.ShapeDtypeStruct(tuple(shp), a.dtype))

    def body(*refs):
        ins, outs = refs[:n], refs[n:2 * n]
        send, recv, lsem = refs[2 * n:]
        x, y, c = _place()
        sib = (x, y, 1 - c)

        def slab(w, half, chip):
            return _slab(outs[w], axes[w], shards[w].shape[1 + axes[w]], chip, half)

        def protocol(me):
            local, sends, passes = [], [], []
            for w in range(n):
                for half in range(2):
                    cp = pltpu.make_async_copy(ins[w].at[half], slab(w, half, me), lsem.at[2 * w + half])
                    cp.start()
                    local.append(cp)
            for w in range(n):
                for j, (px, py) in enumerate(_peers(me)):
                    cp = pltpu.make_async_remote_copy(ins[w].at[c], slab(w, c, me), send.at[3 * w + j],
                                                      recv.at[3 * w + j], device_id=(px, py, c),
                                                      device_id_type=MESH)
                    cp.start()
                    sends.append(cp)
            for w in range(n):
                for j, (px, py) in enumerate(_peers(me)):
                    got = slab(w, c, 2 * px + py)
                    pltpu.make_async_remote_copy(ins[w].at[c], got, send.at[3 * w + j], recv.at[3 * w + j],
                                                 device_id=(px, py, c), device_id_type=MESH).wait_recv()
                    cp = pltpu.make_async_remote_copy(got, got, send.at[3 * (n + w) + j],
                                                      recv.at[3 * (n + w) + j], device_id=sib, device_id_type=MESH)
                    cp.start()
                    passes.append(cp)
            for w in range(n):
                for j, (px, py) in enumerate(_peers(me)):
                    got = slab(w, 1 - c, 2 * px + py)
                    pltpu.make_async_remote_copy(got, got, send.at[3 * (n + w) + j], recv.at[3 * (n + w) + j],
                                                 device_id=sib, device_id_type=MESH).wait_recv()
            for cp in sends + passes:
                cp.wait_send()
            for cp in local:
                cp.wait()

        for me in range(4):
            pl.when(2 * x + y == me)(functools.partial(protocol, me))

    return pl.pallas_call(
        body, name=name, in_specs=_hbm_specs(n), out_specs=_hbm_specs(n), out_shape=out_shape,
        scratch_shapes=[pltpu.SemaphoreType.DMA((6 * n,)), pltpu.SemaphoreType.DMA((6 * n,)),
                        pltpu.SemaphoreType.DMA((2 * n,))],
        compiler_params=pltpu.CompilerParams(has_side_effects=True),
    )(*shards)


def _swap_halves(name, grads):
    n = len(grads)

    def body(*refs):
        ins, outs, send, recv = refs[:n], refs[n:2 * n], refs[2 * n], refs[2 * n + 1]
        x, y, c = _place()
        cps = [pltpu.make_async_remote_copy(ins[w].at[1 - c], outs[w], send.at[w], recv.at[w],
                                            device_id=(x, y, 1 - c), device_id_type=MESH) for w in range(n)]
        for cp in cps:
            cp.start()
        for cp in cps:
            cp.wait()

    return pl.pallas_call(
        body, name=name, in_specs=_hbm_specs(n), out_specs=_hbm_specs(n),
        out_shape=[jax.ShapeDtypeStruct(g.shape[1:], g.dtype) for g in grads],
        scratch_shapes=[pltpu.SemaphoreType.DMA((n,)), pltpu.SemaphoreType.DMA((n,))],
        compiler_params=pltpu.CompilerParams(has_side_effects=True),
    )(*grads)


def _add_half(name, g, got, cidx, tr):
    _, r, cdim = g.shape

    def body(c_ref, g_ref, r_ref, o_ref):
        o_ref[...] = (g_ref[...] + r_ref[...]).astype(o_ref.dtype)

    return pl.pallas_call(
        body, name=name,
        grid_spec=pltpu.PrefetchScalarGridSpec(
            num_scalar_prefetch=1, grid=(r // tr,),
            in_specs=[pl.BlockSpec((None, tr, cdim), lambda i, c_ref: (c_ref[0], i, 0)),
                      pl.BlockSpec((tr, cdim), lambda i, c_ref: (i, 0))],
            out_specs=pl.BlockSpec((tr, cdim), lambda i, c_ref: (i, 0))),
        out_shape=jax.ShapeDtypeStruct((r, cdim), BF16),
        compiler_params=_params(("arbitrary",)),
    )(cidx, g, got)


def _scatter_chips(name, sums, axes):
    n = len(sums)
    sizes = [s.shape[ax] // 4 for s, ax in zip(sums, axes)]
    out_shape = []
    for s, ax, sz in zip(sums, axes, sizes):
        shp = list(s.shape)
        shp[ax] = sz
        out_shape.append(jax.ShapeDtypeStruct((3, *shp), s.dtype))

    def body(*refs):
        ins, outs, send, recv = refs[:n], refs[n:2 * n], refs[2 * n], refs[2 * n + 1]
        x, y, c = _place()

        def protocol(me):
            cps = []
            for w in range(n):
                for j, (px, py) in enumerate(_peers(me)):
                    cp = pltpu.make_async_remote_copy(_slab(ins[w], axes[w], sizes[w], 2 * px + py), outs[w].at[j],
                                                      send.at[3 * w + j], recv.at[3 * w + j],
                                                      device_id=(px, py, c), device_id_type=MESH)
                    cp.start()
                    cps.append(cp)
            for cp in cps:
                cp.wait()

        for me in range(4):
            pl.when(2 * x + y == me)(functools.partial(protocol, me))

    return pl.pallas_call(
        body, name=name, in_specs=_hbm_specs(n), out_specs=_hbm_specs(n), out_shape=out_shape,
        scratch_shapes=[pltpu.SemaphoreType.DMA((3 * n,)), pltpu.SemaphoreType.DMA((3 * n,))],
        compiler_params=pltpu.CompilerParams(has_side_effects=True),
    )(*sums)


def _add_chips(name, own, got, axis, kidx, tr):
    _, rs, cs = got.shape
    nb = rs // tr

    def body(k_ref, o_ref, g_ref, out_ref):
        out_ref[...] = (o_ref[...].astype(F32) + g_ref[0].astype(F32) + g_ref[1].astype(F32)
                        + g_ref[2].astype(F32))

    if axis == 0:
        own_spec = pl.BlockSpec((tr, cs), lambda i, k_ref: (k_ref[0] * nb + i, 0))
    else:
        own_spec = pl.BlockSpec((tr, cs), lambda i, k_ref: (i, k_ref[0]))
    return pl.pallas_call(
        body, name=name,
        grid_spec=pltpu.PrefetchScalarGridSpec(
            num_scalar_prefetch=1, grid=(nb,),
            in_specs=[own_spec, pl.BlockSpec((3, tr, cs), lambda i, k_ref: (0, i, 0))],
            out_specs=pl.BlockSpec((tr, cs), lambda i, k_ref: (i, 0))),
        out_shape=jax.ShapeDtypeStruct((rs, cs), F32),
        compiler_params=_params(("arbitrary",)),
    )(kidx, own, got)


def _join_halves(name, parts):
    n = len(parts)

    def body(*refs):
        ins, outs, send, recv, lsem = refs[:n], refs[n:2 * n], refs[2 * n], refs[2 * n + 1], refs[2 * n + 2]
        x, y, c = _place()
        local = [pltpu.make_async_copy(ins[w], outs[w].at[c], lsem.at[w]) for w in range(n)]
        cps = [pltpu.make_async_remote_copy(ins[w], outs[w].at[c], send.at[w], recv.at[w],
                                            device_id=(x, y, 1 - c), device_id_type=MESH) for w in range(n)]
        for cp in local + cps:
            cp.start()
        for w in range(n):
            cps[w].wait_send()
            pltpu.make_async_remote_copy(ins[w], outs[w].at[1 - c], send.at[w], recv.at[w],
                                         device_id=(x, y, 1 - c), device_id_type=MESH).wait_recv()
        for cp in local:
            cp.wait()

    return pl.pallas_call(
        body, name=name, in_specs=_hbm_specs(n), out_specs=_hbm_specs(n),
        out_shape=[jax.ShapeDtypeStruct((2, *p.shape), p.dtype) for p in parts],
        scratch_shapes=[pltpu.SemaphoreType.DMA((n,)), pltpu.SemaphoreType.DMA((n,)),
                        pltpu.SemaphoreType.DMA((n,))],
        compiler_params=pltpu.CompilerParams(has_side_effects=True),
    )(*parts)


def _all_reduce_small(name, pack, fold_rows, groups):
    nr, d = pack.shape

    def body(in_ref, sum_ref, meta_ref, slots, send, recv):
        x, y, c = _place()
        me = 4 * x + 2 * y + c
        slots[me] = in_ref[...]
        cps = []
        for r in range(1, 8):
            rx, ry, rc = r // 4, (r // 2) % 2, r % 2
            peer = (x + rx - 2 * x * rx, y + ry - 2 * y * ry, c + rc - 2 * c * rc)
            cp = pltpu.make_async_remote_copy(in_ref, slots.at[me], send.at[r - 1], recv.at[r - 1],
                                              device_id=peer, device_id_type=MESH)
            cp.start()
            cps.append(cp)
        for cp in cps:
            cp.wait()
        acc = slots[0]
        for dev in range(1, 8):
            acc = acc + slots[dev]
        sum_ref[...] = acc
        fold = acc[0:fold_rows]
        for grp in range(1, groups):
            fold = fold + acc[grp * fold_rows:(grp + 1) * fold_rows]
        meta_ref[...] = fold

    vmem = pl.BlockSpec(memory_space=pltpu.VMEM)
    return pl.pallas_call(
        body, name=name, in_specs=[vmem], out_specs=[vmem, vmem],
        out_shape=[jax.ShapeDtypeStruct((nr, d), F32), jax.ShapeDtypeStruct((fold_rows, d), F32)],
        scratch_shapes=[pltpu.VMEM((8, nr, d), F32), pltpu.SemaphoreType.DMA((7,)), pltpu.SemaphoreType.DMA((7,))],
        compiler_params=pltpu.CompilerParams(has_side_effects=True, vmem_limit_bytes=VMEM_LIMIT),
    )(pack)


def _adamw_math(w, g, m, v):
    m = B1 * m + (1.0 - B1) * g
    v = B2 * v + (1.0 - B2) * (g * g)
    m_hat = m / (1.0 - B1 ** STEP)
    v_hat = v / (1.0 - B2 ** STEP)
    return -LR * (m_hat / (jnp.sqrt(v_hat) + ADAM_EPS) + WD * w), m, v


def _adamw(name, w, g, m, v, tr):
    shape = w.shape
    flat = [a.reshape(-1, shape[-1]) for a in (w, g, m, v)]
    r, cdim = flat[0].shape

    def body(w_ref, g_ref, m_ref, v_ref, d_ref, nm_ref, nv_ref):
        d_ref[...], nm_ref[...], nv_ref[...] = _adamw_math(w_ref[...], g_ref[...], m_ref[...], v_ref[...])

    spec = pl.BlockSpec((tr, cdim), lambda i: (i, 0))
    outs = pl.pallas_call(
        body, name=name, grid=(r // tr,), in_specs=[spec] * 4, out_specs=[spec] * 3,
        out_shape=[jax.ShapeDtypeStruct((r, cdim), F32)] * 3,
        compiler_params=_params(("parallel",)),
    )(*flat)
    return [o.reshape(shape) for o in outs]


def _adamw_small(name, groups):
    n = len(groups)
    shapes = [grp[0].shape for grp in groups]
    flat = [a.reshape(-1, a.shape[-1]) for grp in groups for a in grp]

    def body(*refs):
        ins, outs = refs[:4 * n], refs[4 * n:]
        for i in range(n):
            w_ref, g_ref, m_ref, v_ref = ins[4 * i:4 * i + 4]
            outs[3 * i][...], outs[3 * i + 1][...], outs[3 * i + 2][...] = _adamw_math(
                w_ref[...], g_ref[...], m_ref[...], v_ref[...])

    vmem = pl.BlockSpec(memory_space=pltpu.VMEM)
    out_shape = [jax.ShapeDtypeStruct(flat[4 * i].shape, F32) for i in range(n) for _ in range(3)]
    outs = pl.pallas_call(body, name=name, in_specs=[vmem] * (4 * n), out_specs=[vmem] * (3 * n),
                          out_shape=out_shape)(*flat)
    return [[outs[3 * i + j].reshape(shapes[i]) for j in range(3)] for i in range(n)]


def _block_diag(w_grp):
    g, pg, _ = w_grp.shape
    eye = jnp.eye(g, dtype=w_grp.dtype)
    return (eye[:, None, :, None] * w_grp[:, :, None, :]).reshape(g * pg, g * pg)


def _diag_blocks(m, g):
    pg = m.shape[0] // g
    return jnp.stack([m[i * pg:(i + 1) * pg, i * pg:(i + 1) * pg] for i in range(g)])


def _local_step(x, meta, g_mix, w_in, w_conv, w_pool, pool_scale, w_out, g_mlp, w_up, w_down, g_final, target):
    bl, s, d = x.shape
    depth = g_mix.shape[0]
    lp = PAD + N_META + s
    t = bl * lp
    tm = lp // 2
    rt = lp // 4
    cw = w_conv.shape[2]
    ngrp = w_pool.shape[1]
    ident = lambda acc: (acc,)

    h = jnp.concatenate([jnp.zeros((bl, PAD, d), F32), jnp.broadcast_to(meta[None], (bl, N_META, d)), x],
                        axis=1).reshape(t, d)
    wbd = [_block_diag(w_pool[i]).astype(BF16) for i in range(depth)]
    saved = []
    for i in range(depth):
        hn = _rms_fwd(f"mix_norm{i}", h, g_mix[i], tm)
        (u_cp,) = _mm_nn(f"in_proj_cp{i}", [hn], w_in, i, 0, 4 * cw, tm, 512, ident, [], [F32])
        (qkv,) = _mm_nn(f"in_proj_qkv{i}", [hn], w_in, i, 4 * cw, w_in.shape[2] - 4 * cw, tm, 512, ident, [],
                        [BF16])
        y_cp = _convpool_fwd(f"convpool{i}", u_cp, w_conv[i], wbd[i], pool_scale[i:i + 1], lp, rt)
        y_at, lt = _attn_fwd(f"attn{i}", qkv, bl, lp)
        (h_mid,) = _mm_nn(f"out_proj{i}", [y_cp, y_at], w_out, i, 0, d, tm, 512, lambda acc, res: (acc + res,),
                          [h], [F32])
        hn2 = _rms_fwd(f"mlp_norm{i}", h_mid, g_mlp[i], tm)
        m_pre, act = _mm_nn(f"up_proj{i}", [hn2], w_up, i, 0, w_up.shape[2], tm, 512,
                            lambda acc: (acc, jnp.square(jnp.maximum(acc, 0.0))), [], [BF16, BF16])
        (h_next,) = _mm_nn(f"down_proj{i}", [act], w_down, i, 0, d, tm, 512, lambda acc, res: (acc + res,),
                           [h_mid], [F32])
        saved.append((h, hn, u_cp, qkv, y_cp, y_at, lt, h_mid, hn2, m_pre, act))
        h = h_next

    dh, loss8, dgf8 = _loss_bwd("loss", h, g_final, target, lp)
    grads = {"g_final": dgf8.sum(0)}
    per_layer = {k: [] for k in ("g_mix", "w_in", "w_conv", "w_pool", "pool_scale", "w_out", "g_mlp", "w_up",
                                 "w_down")}
    for i in reversed(range(depth)):
        h_in, hn, u_cp, qkv, y_cp, y_at, lt, h_mid, hn2, m_pre, act = saved[i]
        (dm,) = _mm_nt(f"down_proj_dx{i}", [dh], w_down, i, tm, 512,
                       lambda acc, mp: (acc * (2.0 * jnp.maximum(mp.astype(F32), 0.0)),), [m_pre], [BF16])
        per_layer["w_down"].append(_mm_tn(f"down_proj_dw{i}", act, dh, tm, 1024, 1024))
        per_layer["w_up"].append(_mm_tn(f"up_proj_dw{i}", hn2, dm, tm, 1024, 1024))
        (dhn2,) = _mm_nt(f"up_proj_dx{i}", [dm], w_up, i, tm, 512, ident, [], [F32])
        dh_mid, dg8 = _rms_bwd(f"mlp_norm_bwd{i}", h_mid, dhn2, g_mlp[i], dh, tm)
        per_layer["g_mlp"].append(dg8.sum(0))
        (dy,) = _mm_nt(f"out_proj_dx{i}", [dh_mid], w_out, i, tm, 512, ident, [], [F32])
        per_layer["w_out"].append(jnp.concatenate(
            [_mm_tn(f"out_proj_dw_cp{i}", y_cp, dh_mid, tm, 512, 1024),
             _mm_tn(f"out_proj_dw_at{i}", y_at, dh_mid, tm, 512, 1024)], axis=0))
        dq, dk, dv = _attn_bwd(f"attn_bwd{i}", qkv, lt, dy, bl, lp)
        du_cp, sm, dwbd = _convpool_bwd(f"convpool_bwd{i}", u_cp, dy, w_conv[i], wbd[i], pool_scale[i:i + 1], lp,
                                        rt)
        sm = sm.reshape(4, 8, cw).sum(1)
        per_layer["w_conv"].append(sm[0:3])
        per_layer["pool_scale"].append(sm[3])
        per_layer["w_pool"].append(_diag_blocks(dwbd, ngrp))
        dus = [du_cp, dq, dk, dv]
        per_layer["w_in"].append(jnp.concatenate(
            [_mm_tn(f"in_proj_dw{j}_{i}", hn, du, tm, 1024, du.shape[1]) for j, du in enumerate(dus)], axis=1))
        (dhn,) = _mm_nt(f"in_proj_dx{i}", dus, w_in, i, tm, 512, ident, [], [F32])
        dh, dg8 = _rms_bwd(f"mix_norm_bwd{i}", h_in, dhn, g_mix[i], dh_mid, tm)
        per_layer["g_mix"].append(dg8.sum(0))
    for k, v in per_layer.items():
        grads[k] = jnp.stack(v[::-1])
    return loss8, dh, grads


def kernel(x, meta_tokens, g_mix, w_in, w_conv, w_pool, pool_scale, w_out, g_mlp, w_up, w_down, g_final, loss_target, m_meta_tokens, m_g_mix, m_w_in, m_w_conv, m_w_pool, m_pool_scale, m_w_out, m_g_mlp, m_w_up, m_w_down, m_g_final, v_meta_tokens, v_g_mix, v_w_in, v_w_conv, v_w_pool, v_pool_scale, v_w_out, v_g_mlp, v_w_up, v_w_down, v_g_final):
    bl, s, d = x.shape
    lp = PAD + N_META + s
    xi, yi, ci = _place()
    chip = (2 * xi + yi).astype(jnp.int32)
    cidx, kidx = ci.astype(jnp.int32).reshape(1), chip.reshape(1)

    big = {"w_in": (w_in, 1), "w_out": (w_out, 0), "w_up": (w_up, 1), "w_down": (w_down, 0)}
    f_in, f_out, f_up, f_down = _all_gather("gather_weights", [big[k][0].astype(BF16) for k in big],
                                            [big[k][1] for k in big])
    cs = w_conv.shape[2]
    placed = jnp.zeros((32, d), F32)
    placed = lax.dynamic_update_slice(placed, meta_tokens, (0, chip * meta_tokens.shape[1]))
    placed = lax.dynamic_update_slice(placed, w_conv.reshape(-1, cs), (N_META, chip * cs))
    placed = jnp.where(ci == 0, placed, 0.0)
    whole, _ = _all_reduce_small("gather_small", placed, 8, 1)
    meta_full = whole[:N_META]
    conv_full = whole[N_META:N_META + 2 * 3, :4 * cs].reshape(2, 3, 4 * cs)

    loss8, dh0, grads = _local_step(x, meta_full, g_mix, f_in, conv_full, w_pool, pool_scale, f_out, g_mlp, f_up,
                                    f_down, g_final, loss_target)
    loss = lax.psum(jnp.sum(loss8), ("x", "y", "c"))
    dh0 = dh0.reshape(bl, lp, d)
    grad_x = dh0[:, PAD + N_META:]

    names = list(big)
    axes = [big[k][1] for k in names]
    from_sib = _swap_halves("grads_to_sibling", [grads[k] for k in names])
    chip_sums = [_add_half(f"chip_sum_{k}", grads[k], got, cidx, 256) for k, got in zip(names, from_sib)]
    from_chips = _scatter_chips("grads_to_chips", chip_sums, axes)
    reduced = [_add_chips(f"reduce_{k}", own, got, ax, kidx, 256)
               for k, own, got, ax in zip(names, chip_sums, from_chips, axes)]
    big_grads = dict(zip(names, _join_halves("grads_join", reduced)))

    cw = w_conv.shape[2] * 4
    pieces = [dh0[:, PAD:PAD + N_META].reshape(bl * N_META, d), grads["g_mix"], grads["g_mlp"],
              grads["g_final"].reshape(1, d),
              jnp.pad(grads["w_conv"].reshape(-1), (0, 2 * d - grads["w_conv"].size)).reshape(2, d),
              jnp.pad(grads["pool_scale"].reshape(-1), (0, d - grads["pool_scale"].size)).reshape(1, d),
              grads["w_pool"].reshape(-1, d)]
    pack = jnp.concatenate(pieces, axis=0)
    summed, meta_sum = _all_reduce_small("small_grads", pack, N_META, bl)
    o = bl * N_META
    g_small = {
        "meta_tokens": lax.dynamic_slice_in_dim(meta_sum, chip * meta_tokens.shape[1], meta_tokens.shape[1], 1),
        "g_mix": summed[o:o + 2], "g_mlp": summed[o + 2:o + 4], "g_final": summed[o + 4],
        "w_conv": lax.dynamic_slice_in_dim(summed[o + 5:o + 7].reshape(-1)[:2 * 3 * cw].reshape(2, 3, cw),
                                           chip * w_conv.shape[2], w_conv.shape[2], 2),
        "pool_scale": summed[o + 7].reshape(-1)[:pool_scale.size].reshape(pool_scale.shape),
        "w_pool": summed[o + 8:].reshape(w_pool.shape),
    }

    weights = dict(meta_tokens=meta_tokens, g_mix=g_mix, w_in=w_in, w_conv=w_conv, w_pool=w_pool,
                   pool_scale=pool_scale, w_out=w_out, g_mlp=g_mlp, w_up=w_up, w_down=w_down, g_final=g_final)
    ms = dict(meta_tokens=m_meta_tokens, g_mix=m_g_mix, w_in=m_w_in, w_conv=m_w_conv, w_pool=m_w_pool,
              pool_scale=m_pool_scale, w_out=m_w_out, g_mlp=m_g_mlp, w_up=m_w_up, w_down=m_w_down,
              g_final=m_g_final)
    vs = dict(meta_tokens=v_meta_tokens, g_mix=v_g_mix, w_in=v_w_in, w_conv=v_w_conv, w_pool=v_w_pool,
              pool_scale=v_pool_scale, w_out=v_w_out, g_mlp=v_g_mlp, w_up=v_w_up, w_down=v_w_down,
              g_final=v_g_final)
    order = list(weights)
    grad = {**g_small, **big_grads}
    upd = {}
    for k in names:
        upd[k] = _adamw(f"adamw_{k}", weights[k], grad[k], ms[k], vs[k], 256)
    small = [k for k in order if k not in big]
    for k, res in zip(small, _adamw_small("adamw_small", [(weights[k], grad[k].reshape(weights[k].shape), ms[k],
                                                           vs[k]) for k in small])):
        upd[k] = res
    grad = {k: grad[k].reshape(weights[k].shape) for k in order}
    return (loss, grad_x, *[grad[k] for k in order], *[upd[k][0] for k in order], *[upd[k][1] for k in order],
            *[upd[k][2] for k in order])
```

```python
import functools

import jax
import jax.numpy as jnp
from jax import lax
from jax.experimental import pallas as pl
from jax.experimental.pallas import tpu as pltpu

F32, BF16 = jnp.float32, jnp.bfloat16
MESH = pl.DeviceIdType.MESH
EPS = 1e-6
N_META = 16
QB = 128
PAD = QB - N_META
HALO = 16
POOL_WINDOWS = (2.0, 4.0, 8.0, 16.0)
HEAD_SCALE = 0.125
LR, B1, B2, ADAM_EPS, WD, STEP = 0.001, 0.9, 0.999, 1e-08, 0.01, 10
VMEM_LIMIT = 56 * 1024 * 1024


def _params(sem=None):
    return pltpu.CompilerParams(dimension_semantics=sem, vmem_limit_bytes=VMEM_LIMIT)


def _nt(a, b):
    return lax.dot_general(a, b, (((1,), (1,)), ((), ())), preferred_element_type=F32)


def _tn(a, b):
    return lax.dot_general(a, b, (((0,), (0,)), ((), ())), preferred_element_type=F32)


def _nn(a, b):
    return jnp.dot(a, b, preferred_element_type=F32)


def _fold8(v):
    r, c = v.shape
    return jnp.sum(v.reshape(r // 8, 8, c), axis=0)


def _mm_nn(name, a_list, w, layer, n_off, n, tm, tn, epi, extras, out_dtypes):
    m = a_list[0].shape[0]
    kdim = w.shape[1]
    ks = [a.shape[1] for a in a_list]
    assert sum(ks) == kdim and n_off % tn == 0 and n % tn == 0 and m % tm == 0
    na, ne = len(a_list), len(extras)

    def body(*refs):
        a_refs, w_ref = refs[:na], refs[na]
        e_refs, o_refs = refs[na + 1:na + 1 + ne], refs[na + 1 + ne:]
        acc, off = None, 0
        for a_ref, k in zip(a_refs, ks):
            part = _nn(a_ref[...].astype(BF16), w_ref[off:off + k, :])
            acc = part if acc is None else acc + part
            off += k
        outs = epi(acc, *[e[...] for e in e_refs])
        for o_ref, o in zip(o_refs, outs):
            o_ref[...] = o.astype(o_ref.dtype)

    in_specs = [pl.BlockSpec((tm, k), lambda i, j: (i, 0)) for k in ks]
    in_specs.append(pl.BlockSpec((None, kdim, tn), lambda i, j: (layer, 0, n_off // tn + j)))
    in_specs += [pl.BlockSpec((tm, tn), lambda i, j: (i, j)) for _ in extras]
    return pl.pallas_call(
        body, name=name, grid=(m // tm, n // tn), in_specs=in_specs,
        out_specs=[pl.BlockSpec((tm, tn), lambda i, j: (i, j)) for _ in out_dtypes],
        out_shape=[jax.ShapeDtypeStruct((m, n), d) for d in out_dtypes],
        compiler_params=_params(("parallel", "arbitrary")),
    )(*a_list, w, *extras)


def _mm_nt(name, a_list, w, layer, tm, tko, epi, extras, out_dtypes):
    m = a_list[0].shape[0]
    kout, ntot = w.shape[1], w.shape[2]
    ns = [a.shape[1] for a in a_list]
    assert sum(ns) == ntot and kout % tko == 0 and m % tm == 0
    na, ne = len(a_list), len(extras)

    def body(*refs):
        a_refs, w_ref = refs[:na], refs[na]
        e_refs, o_refs = refs[na + 1:na + 1 + ne], refs[na + 1 + ne:]
        acc, off = None, 0
        for a_ref, k in zip(a_refs, ns):
            part = _nt(a_ref[...].astype(BF16), w_ref[:, off:off + k])
            acc = part if acc is None else acc + part
            off += k
        outs = epi(acc, *[e[...] for e in e_refs])
        for o_ref, o in zip(o_refs, outs):
            o_ref[...] = o.astype(o_ref.dtype)

    in_specs = [pl.BlockSpec((tm, k), lambda i, j: (i, 0)) for k in ns]
    in_specs.append(pl.BlockSpec((None, tko, ntot), lambda i, j: (layer, j, 0)))
    in_specs += [pl.BlockSpec((tm, tko), lambda i, j: (i, j)) for _ in extras]
    return pl.pallas_call(
        body, name=name, grid=(m // tm, kout // tko), in_specs=in_specs,
        out_specs=[pl.BlockSpec((tm, tko), lambda i, j: (i, j)) for _ in out_dtypes],
        out_shape=[jax.ShapeDtypeStruct((m, kout), d) for d in out_dtypes],
        compiler_params=_params(("parallel", "arbitrary")),
    )(*a_list, w, *extras)


def _mm_tn(name, a, b, tt, tka, tn):
    t, ka = a.shape
    n = b.shape[1]
    assert t % tt == 0 and ka % tka == 0 and n % tn == 0

    def body(a_ref, b_ref, o_ref):
        @pl.when(pl.program_id(2) == 0)
        def _():
            o_ref[...] = jnp.zeros_like(o_ref)
        o_ref[...] += _tn(a_ref[...].astype(BF16), b_ref[...].astype(BF16))

    return pl.pallas_call(
        body, name=name, grid=(ka // tka, n // tn, t // tt),
        in_specs=[pl.BlockSpec((tt, tka), lambda i, j, s: (s, i)), pl.BlockSpec((tt, tn), lambda i, j, s: (s, j))],
        out_specs=pl.BlockSpec((tka, tn), lambda i, j, s: (i, j)),
        out_shape=jax.ShapeDtypeStruct((ka, n), F32),
        compiler_params=_params(("parallel", "parallel", "arbitrary")),
    )(a, b)


def _rms_fwd(name, h, g, tm):
    t, d = h.shape

    def body(h_ref, g_ref, o_ref):
        x = h_ref[...]
        r = lax.rsqrt(jnp.mean(x * x, axis=-1, keepdims=True) + EPS)
        o_ref[...] = (x * r * g_ref[...]).astype(o_ref.dtype)

    return pl.pallas_call(
        body, name=name, grid=(t // tm,),
        in_specs=[pl.BlockSpec((tm, d), lambda i: (i, 0)), pl.BlockSpec((1, d), lambda i: (0, 0))],
        out_specs=pl.BlockSpec((tm, d), lambda i: (i, 0)),
        out_shape=jax.ShapeDtypeStruct((t, d), BF16),
        compiler_params=_params(("parallel",)),
    )(h, g.reshape(1, d))


def _rms_bwd(name, x, dyn, g, resid, tm):
    t, d = x.shape

    def body(x_ref, dy_ref, g_ref, r_ref, o_ref, dg_ref):
        @pl.when(pl.program_id(0) == 0)
        def _():
            dg_ref[...] = jnp.zeros_like(dg_ref)
        xv, dy = x_ref[...], dy_ref[...]
        r = lax.rsqrt(jnp.mean(xv * xv, axis=-1, keepdims=True) + EPS)
        xh = xv * r
        w = dy * g_ref[...]
        o_ref[...] = r_ref[...] + r * (w - xh * jnp.mean(w * xh, axis=-1, keepdims=True))
        dg_ref[...] += _fold8(dy * xh)

    row = pl.BlockSpec((tm, d), lambda i: (i, 0))
    return pl.pallas_call(
        body, name=name, grid=(t // tm,),
        in_specs=[row, row, pl.BlockSpec((1, d), lambda i: (0, 0)), row],
        out_specs=[row, pl.BlockSpec((8, d), lambda i: (0, 0))],
        out_shape=[jax.ShapeDtypeStruct((t, d), F32), jax.ShapeDtypeStruct((8, d), F32)],
        compiler_params=_params(("arbitrary",)),
    )(x, dyn, g.reshape(1, d), resid)


def _loss_bwd(name, h, g, target, lp):
    t, d = h.shape
    bl = target.shape[0]
    nq = lp // QB

    def body(h_ref, g_ref, t_ref, dh_ref, ls_ref, dg_ref):
        b, j = pl.program_id(0), pl.program_id(1)

        @pl.when((b == 0) & (j == 0))
        def _():
            ls_ref[...] = jnp.zeros_like(ls_ref)
            dg_ref[...] = jnp.zeros_like(dg_ref)
        xv = h_ref[...]
        r = lax.rsqrt(jnp.mean(xv * xv, axis=-1, keepdims=True) + EPS)
        xh = xv * r
        gv = g_ref[...]
        err = jnp.where(j >= 1, xh * gv - t_ref[...], 0.0)
        ls_ref[...] += _fold8(err * err) * (0.5 / d)
        dy = err * (1.0 / d)
        w = dy * gv
        dh_ref[...] = r * (w - xh * jnp.mean(w * xh, axis=-1, keepdims=True))
        dg_ref[...] += _fold8(dy * xh)

    return pl.pallas_call(
        body, name=name, grid=(bl, nq),
        in_specs=[pl.BlockSpec((QB, d), lambda b, j: (b * nq + j, 0)), pl.BlockSpec((1, d), lambda b, j: (0, 0)),
                  pl.BlockSpec((None, QB, d), lambda b, j: (b, jnp.maximum(j - 1, 0), 0))],
        out_specs=[pl.BlockSpec((QB, d), lambda b, j: (b * nq + j, 0)), pl.BlockSpec((8, d), lambda b, j: (0, 0)),
                   pl.BlockSpec((8, d), lambda b, j: (0, 0))],
        out_shape=[jax.ShapeDtypeStruct((t, d), F32), jax.ShapeDtypeStruct((8, d), F32),
                   jax.ShapeDtypeStruct((8, d), F32)],
        compiler_params=_params(("arbitrary", "arbitrary")),
    )(h, g.reshape(1, d), target)


def _pool_select(grp, a2, a4, a8, a16):
    return jnp.where(grp == 0, a2, jnp.where(grp == 1, a4, jnp.where(grp == 2, a8, a16)))


def _trailing_sums(v):
    s2 = v + pltpu.roll(v, 1, 0)
    s4 = s2 + pltpu.roll(s2, 2, 0)
    s8 = s4 + pltpu.roll(s4, 4, 0)
    s16 = s8 + pltpu.roll(s8, 8, 0)
    return s2, s4, s8, s16


def _leading_sums(v):
    n = v.shape[0]
    s2 = v + pltpu.roll(v, n - 1, 0)
    s4 = s2 + pltpu.roll(s2, n - 2, 0)
    s8 = s4 + pltpu.roll(s4, n - 4, 0)
    s16 = s8 + pltpu.roll(s8, n - 8, 0)
    return s2, s4, s8, s16


def _convpool_fwd(name, u_cp, wconv, wbd, pscale, lp, r):
    t = u_cp.shape[0]
    cw = u_cp.shape[1] // 4
    tps, hb = lp // r, r // HALO

    def body(cb_ref, cc_ref, cx_ref, pi_ref, cch_ref, cxh_ref, pih_ref, wc_ref, wbd_ref, ps_ref, y_ref):
        i = pl.program_id(0)
        lrow = (i % tps) * r + lax.broadcasted_iota(jnp.int32, (r, 1), 0)
        valid = lrow >= PAD
        xx = jnp.concatenate([cch_ref[...] * cxh_ref[...], cc_ref[...] * cx_ref[...]], axis=0)
        conv = (wc_ref[0:1, :] * pltpu.roll(xx, 2, 0) + wc_ref[1:2, :] * pltpu.roll(xx, 1, 0)
                + wc_ref[2:3, :] * xx)
        y_ref[:, 0:cw] = (cb_ref[...] * conv[HALO:]).astype(y_ref.dtype)
        p = pi_ref[...]
        grp = lax.broadcasted_iota(jnp.int32, (1, cw), 1) // (cw // 4)
        sel = _pool_select(grp, *_trailing_sums(jnp.concatenate([pih_ref[...], p], axis=0)))[HALO:]
        cnt = jnp.maximum(jnp.minimum((lrow - (PAD - 1)).astype(F32), _pool_select(grp, *POOL_WINDOWS)), 1.0)
        pooled = jnp.where(valid, sel / cnt - p, 0.0)
        y_ref[:, cw:2 * cw] = (_nn(pooled.astype(BF16), wbd_ref[...]) * ps_ref[...]).astype(y_ref.dtype)

    def main(col):
        return pl.BlockSpec((r, cw), lambda i: (i, col))

    def prev(col):
        return pl.BlockSpec((HALO, cw), lambda i: (jnp.maximum(i * hb - 1, 0), col))

    def whole(a):
        return pl.BlockSpec(a.shape, lambda i: (0, 0))

    return pl.pallas_call(
        body, name=name, grid=(t // r,),
        in_specs=[main(0), main(1), main(2), main(3), prev(1), prev(2), prev(3), whole(wconv), whole(wbd),
                  whole(pscale)],
        out_specs=pl.BlockSpec((r, 2 * cw), lambda i: (i, 0)),
        out_shape=jax.ShapeDtypeStruct((t, 2 * cw), BF16),
        compiler_params=_params(("parallel",)),
    )(u_cp, u_cp, u_cp, u_cp, u_cp, u_cp, u_cp, wconv, wbd, pscale)


def _convpool_bwd(name, u_cp, dy, wconv, wbd, pscale, lp, r):
    t = u_cp.shape[0]
    cw = u_cp.shape[1] // 4
    tps, hb = lp // r, r // HALO
    e = r + HALO

    def body(cb_ref, cc_ref, cx_ref, pi_ref, cbn_ref, cch_ref, cxh_ref, pih_ref, dyc_ref, dyp_ref, dycn_ref,
             dypn_ref, wc_ref, wbd_ref, ps_ref, du_ref, sm_ref, dwbd_ref):
        i = pl.program_id(0)

        @pl.when(i == 0)
        def _():
            sm_ref[...] = jnp.zeros_like(sm_ref)
            dwbd_ref[...] = jnp.zeros_like(dwbd_ref)
        lrow_e = (i % tps) * r + lax.broadcasted_iota(jnp.int32, (e, 1), 0)
        valid_e = (lrow_e >= PAD) & (lrow_e < lp)
        lrow, valid = lrow_e[:r], lrow_e[:r] >= PAD
        w0, w1, w2 = wc_ref[0:1, :], wc_ref[1:2, :], wc_ref[2:3, :]
        cb, cc, cx = cb_ref[...], cc_ref[...], cx_ref[...]
        prod = cc * cx
        xx = jnp.concatenate([cch_ref[...] * cxh_ref[...], prod], axis=0)
        back1, back2 = pltpu.roll(xx, 1, 0)[HALO:], pltpu.roll(xx, 2, 0)[HALO:]
        dyc = dyc_ref[...]
        du_ref[:, 0:cw] = (dyc * (w0 * back2 + w1 * back1 + w2 * prod)).astype(du_ref.dtype)
        dconv_e = jnp.where(valid_e, jnp.concatenate([dyc * cb, dycn_ref[...] * cbn_ref[...]], axis=0), 0.0)
        dconv = dconv_e[:r]
        dprod = (w2 * dconv + w1 * pltpu.roll(dconv_e, e - 1, 0)[:r] + w0 * pltpu.roll(dconv_e, e - 2, 0)[:r])
        du_ref[:, cw:2 * cw] = (dprod * cx).astype(du_ref.dtype)
        du_ref[:, 2 * cw:3 * cw] = (dprod * cc).astype(du_ref.dtype)
        sm_ref[0:8, :] += _fold8(dconv * back2)
        sm_ref[8:16, :] += _fold8(dconv * back1)
        sm_ref[16:24, :] += _fold8(dconv * prod)
        p = pi_ref[...]
        grp = lax.broadcasted_iota(jnp.int32, (1, cw), 1) // (cw // 4)
        win = _pool_select(grp, *POOL_WINDOWS)
        sel = _pool_select(grp, *_trailing_sums(jnp.concatenate([pih_ref[...], p], axis=0)))[HALO:]
        cnt_e = jnp.maximum(jnp.minimum((lrow_e - (PAD - 1)).astype(F32), win), 1.0)
        pooled = jnp.where(valid, sel / cnt_e[:r] - p, 0.0).astype(BF16)
        dyp = dyp_ref[...]
        sm_ref[24:32, :] += _fold8(dyp * _nn(pooled, wbd_ref[...]))
        dpre_e = (jnp.concatenate([dyp, dypn_ref[...]], axis=0) * ps_ref[...]).astype(BF16)
        dwbd_ref[...] += _tn(pooled, dpre_e[:r])
        dpooled_e = jnp.where(valid_e, _nt(dpre_e, wbd_ref[...]), 0.0)
        ahead = _pool_select(grp, *_leading_sums(dpooled_e / cnt_e))[:r]
        du_ref[:, 3 * cw:4 * cw] = (ahead - dpooled_e[:r]).astype(du_ref.dtype)

    last_halo = t // HALO - 1

    def main(col):
        return pl.BlockSpec((r, cw), lambda i: (i, col))

    def prev(col):
        return pl.BlockSpec((HALO, cw), lambda i: (jnp.maximum(i * hb - 1, 0), col))

    def nxt(col):
        return pl.BlockSpec((HALO, cw), lambda i: (jnp.minimum((i + 1) * hb, last_halo), col))

    def whole(a):
        return pl.BlockSpec(a.shape, lambda i: (0, 0))

    return pl.pallas_call(
        body, name=name, grid=(t // r,),
        in_specs=[main(0), main(1), main(2), main(3), nxt(0), prev(1), prev(2), prev(3), main(0), main(1), nxt(0),
                  nxt(1), whole(wconv), whole(wbd), whole(pscale)],
        out_specs=[pl.BlockSpec((r, 4 * cw), lambda i: (i, 0)), pl.BlockSpec((32, cw), lambda i: (0, 0)),
                   pl.BlockSpec((cw, cw), lambda i: (0, 0))],
        out_shape=[jax.ShapeDtypeStruct((t, 4 * cw), BF16), jax.ShapeDtypeStruct((32, cw), F32),
                   jax.ShapeDtypeStruct((cw, cw), F32)],
        compiler_params=_params(("arbitrary",)),
    )(u_cp, u_cp, u_cp, u_cp, u_cp, u_cp, u_cp, u_cp, dy, dy, dy, dy, wconv, wbd, pscale)


KW = 2 * QB
HP = 4


def _cumsum_matrix(before):
    r = lax.broadcasted_iota(jnp.int32, (KW, KW), 0)
    c = lax.broadcasted_iota(jnp.int32, (KW, KW), 1)
    return ((r < c) if before else (r > c)).astype(BF16)


def _running(v, mat):
    m = v.shape[0]
    hi = v.astype(BF16)
    ext = _nn(jnp.concatenate([hi, (v - hi.astype(F32)).astype(BF16)], axis=0), mat)
    return ext[:m] + ext[m:]


def _log_sigmoid(z):
    neg_abs = lax.bitcast_convert_type(lax.bitcast_convert_type(z, jnp.int32) | jnp.int32(-2 ** 31), F32)
    return jnp.minimum(z, 0.0) - jnp.log(1.0 + jnp.exp(neg_abs))


def _stack_heads(v, head0):
    zero = jnp.zeros_like(v)
    return jnp.concatenate([jnp.where(head0, v, zero), jnp.where(head0, zero, v)], axis=0)


def _lanes(hp):
    return slice(hp * QB, (hp + 1) * QB)


def _attn_fwd(name, qkv, bl, lp):
    t = qkv.shape[0]
    nq, nblk = lp // QB, qkv.shape[1] // (3 * HP * QB)

    def body(q_ref, k_ref, v_ref, o_ref, lt_ref):
        qi = pl.program_id(2)
        head0 = lax.broadcasted_iota(jnp.int32, (QB, QB), 1) < QB // 2
        q2 = [_stack_heads(q_ref[:, _lanes(hp)] * jnp.asarray(HEAD_SCALE, BF16), head0) for hp in range(HP)]
        later = _cumsum_matrix(before=False)
        q_pos = qi * QB + (lax.broadcasted_iota(jnp.int32, (2 * QB, KW), 0) & (QB - 1))
        col = lax.broadcasted_iota(jnp.int32, (2 * QB, KW), 1)
        ng = qi // 2 + 1

        def group(g, carry, masked):
            start = pl.multiple_of(jnp.minimum(g * KW, lp - KW) if masked else g * KW, QB)
            if masked:
                k_pos = start + col
                valid = (k_pos < q_pos) & (k_pos >= PAD) & (k_pos >= g * KW)
            z = [_nt(q2[hp], k_ref[pl.ds(start, KW), _lanes(hp)]) for hp in range(HP)]
            logp, after, rs = [], [], []
            for hp in range(HP):
                lp_ = _log_sigmoid(z[hp])
                lk = lp_ - z[hp]
                if masked:
                    lk = jnp.where(valid, lk, 0.0)
                logp.append(lp_)
                rs.append(jnp.sum(lk, axis=1, keepdims=True))
                after.append(_running(lk, later))
            out = []
            for hp in range(HP):
                run, acc = carry[2 * hp], carry[2 * hp + 1]
                a = jnp.exp(logp[hp] + after[hp] + run)
                if masked:
                    a = jnp.where(valid, a, 0.0)
                out += [run + rs[hp], acc + _nn(a.astype(BF16), v_ref[pl.ds(start, KW), _lanes(hp)])]
            return tuple(out)

        carry = (jnp.zeros((2 * QB, 1), F32), jnp.zeros((2 * QB, QB), F32)) * HP
        carry = group(ng - 1, carry, True)
        carry = lax.fori_loop(0, jnp.maximum(ng - 2, 0), lambda i, c: group(ng - 2 - i, c, False), carry)
        carry = lax.fori_loop(0, jnp.minimum(ng - 1, 1), lambda i, c: group(0, c, True), carry)
        for hp in range(HP):
            run, acc = carry[2 * hp], carry[2 * hp + 1]
            o_ref[:, _lanes(hp)] = jnp.where(head0, acc[:QB], acc[QB:]).astype(o_ref.dtype)
            lt_ref[:, _lanes(hp)] = jnp.where(head0, run[:QB], run[QB:])

    wb = HP * QB
    blk = pl.BlockSpec((QB, wb), lambda b, p, i: (b * nq + i, p))
    return pl.pallas_call(
        body, name=name, grid=(bl, nblk, nq),
        in_specs=[blk, pl.BlockSpec((lp, wb), lambda b, p, i: (b, nblk + p)),
                  pl.BlockSpec((lp, wb), lambda b, p, i: (b, 2 * nblk + p))],
        out_specs=[blk, blk],
        out_shape=[jax.ShapeDtypeStruct((t, nblk * wb), BF16), jax.ShapeDtypeStruct((t, nblk * wb), F32)],
        compiler_params=_params(("parallel", "parallel", "arbitrary")),
    )(qkv, qkv, qkv)


def _attn_bwd(name, qkv, lt, dy, bl, lp):
    t = qkv.shape[0]
    nq, nblk = lp // QB, qkv.shape[1] // (3 * HP * QB)

    def body(q_ref, k_ref, v_ref, lt_ref, do_ref, dq_ref, dk_ref, dv_ref, dk_acc, dv_acc):
        qi = pl.program_id(2)

        @pl.when(qi == 0)
        def _():
            dk_acc[...] = jnp.zeros_like(dk_acc)
            dv_acc[...] = jnp.zeros_like(dv_acc)
        lane = lax.broadcasted_iota(jnp.int32, (QB, QB), 1)
        head0 = lane < QB // 2
        q2, do2, total = [], [], []
        for hp in range(HP):
            q2.append(_stack_heads(q_ref[:, _lanes(hp)] * jnp.asarray(HEAD_SCALE, BF16), head0))
            do2.append(_stack_heads(do_ref[:, _lanes(hp)].astype(BF16), head0))
            ltv = lt_ref[:, _lanes(hp)]
            total.append(jnp.concatenate(
                [jnp.sum(jnp.where(lane == 0, ltv, 0.0), axis=1, keepdims=True),
                 jnp.sum(jnp.where(lane == QB // 2, ltv, 0.0), axis=1, keepdims=True)], axis=0))
        later, earlier = _cumsum_matrix(before=False), _cumsum_matrix(before=True)
        q_pos = qi * QB + (lax.broadcasted_iota(jnp.int32, (2 * QB, KW), 0) & (QB - 1))
        col = lax.broadcasted_iota(jnp.int32, (2 * QB, KW), 1)
        ng = qi // 2 + 1

        def group(g, carry, masked):
            start = pl.multiple_of(jnp.minimum(g * KW, lp - KW) if masked else g * KW, QB)
            if masked:
                k_pos = start + col
                valid = (k_pos < q_pos) & (k_pos >= PAD) & (k_pos >= g * KW)
            hps = range(HP)
            kg = [k_ref[pl.ds(start, KW), _lanes(hp)] for hp in hps]
            z = [_nt(q2[hp], kg[hp]) for hp in hps]
            da = [_nt(do2[hp], v_ref[pl.ds(start, KW), _lanes(hp)]) for hp in hps]
            logp, sig, after, rs = [], [], [], []
            for hp in hps:
                lp_ = _log_sigmoid(z[hp])
                lk = lp_ - z[hp]
                if masked:
                    lk = jnp.where(valid, lk, 0.0)
                logp.append(lp_)
                sig.append(jnp.exp(lp_))
                rs.append(jnp.sum(lk, axis=1, keepdims=True))
                after.append(_running(lk, later))
            a, gg, before = [], [], []
            for hp in hps:
                a_ = jnp.exp(logp[hp] + after[hp] + (total[hp] - carry[3 * hp] - rs[hp]))
                if masked:
                    a_ = jnp.where(valid, a_, 0.0)
                a.append(a_.astype(BF16))
                gg.append(a_ * da[hp])
                before.append(_nn(gg[hp].astype(BF16), earlier))
            out = []
            for hp in hps:
                seen, gsum, dq = carry[3 * hp], carry[3 * hp + 1], carry[3 * hp + 2]
                dz = gg[hp] - (gg[hp] + before[hp] + gsum) * sig[hp]
                if masked:
                    dz = jnp.where(valid, dz, 0.0)
                dz = dz.astype(BF16)
                dk_acc[pl.ds(start, KW), _lanes(hp)] += _tn(dz, q2[hp])
                dv_acc[pl.ds(start, KW), _lanes(hp)] += _tn(a[hp], do2[hp])
                out += [seen + rs[hp], gsum + jnp.sum(gg[hp], axis=1, keepdims=True), dq + _nn(dz, kg[hp])]
            return tuple(out)

        col0 = jnp.zeros((2 * QB, 1), F32)
        carry = (col0, col0, jnp.zeros((2 * QB, QB), F32)) * HP
        carry = group(0, carry, True)
        carry = lax.fori_loop(1, ng - 1, lambda g, c: group(g, c, False), carry)
        carry = lax.fori_loop(0, jnp.minimum(ng - 1, 1), lambda i, c: group(ng - 1, c, True), carry)
        for hp in range(HP):
            dq = carry[3 * hp + 2]
            dq_ref[:, _lanes(hp)] = (jnp.where(head0, dq[:QB], dq[QB:]) * HEAD_SCALE).astype(dq_ref.dtype)

        @pl.when(qi == nq - 1)
        def _():
            dk_ref[...] = dk_acc[...].astype(dk_ref.dtype)
            dv_ref[...] = dv_acc[...].astype(dv_ref.dtype)

    wb = HP * QB
    blk = pl.BlockSpec((QB, wb), lambda b, p, i: (b * nq + i, p))
    seq = pl.BlockSpec((lp, wb), lambda b, p, i: (b, p))
    out = jax.ShapeDtypeStruct((t, nblk * wb), BF16)
    return pl.pallas_call(
        body, name=name, grid=(bl, nblk, nq),
        in_specs=[blk, pl.BlockSpec((lp, wb), lambda b, p, i: (b, nblk + p)),
                  pl.BlockSpec((lp, wb), lambda b, p, i: (b, 2 * nblk + p)), blk,
                  pl.BlockSpec((QB, wb), lambda b, p, i: (b * nq + i, nblk + p))],
        out_specs=[blk, seq, seq],
        out_shape=[out, out, out],
        scratch_shapes=[pltpu.VMEM((lp, wb), F32), pltpu.VMEM((lp, wb), F32)],
        compiler_params=_params(("parallel", "parallel", "arbitrary")),
    )(qkv, qkv, qkv, lt, dy)


def _place():
    return lax.axis_index("x"), lax.axis_index("y"), lax.axis_index("c")


def _slab(ref, axis, size, chip, *lead):
    if axis == 0:
        return ref.at[(*lead, pl.ds(chip * size, size), slice(None))]
    return ref.at[(*lead, slice(None), pl.ds(chip * size, size))]


def _peers(chip):
    kx, ky = chip // 2, chip % 2
    return ((1 - kx, ky), (kx, 1 - ky), (1 - kx, 1 - ky))


def _hbm_specs(n):
    return [pl.BlockSpec(memory_space=pl.ANY) for _ in range(n)]


def _place_shard(name, w, axis, kidx, tr):
    _, r, cdim = w.shape
    shp = [2, r, cdim]
    shp[1 + axis] *= 4
    nb = r // tr

    def body(k_ref, w_ref, o_ref):
        o_ref[...] = w_ref[...].astype(o_ref.dtype)

    if axis == 0:
        out_spec = pl.BlockSpec((None, tr, cdim), lambda l, i, k_ref: (l, k_ref[0] * nb + i, 0))
    else:
        out_spec = pl.BlockSpec((None, tr, cdim), lambda l, i, k_ref: (l, i, k_ref[0]))
    return pl.pallas_call(
        body, name=name,
        grid_spec=pltpu.PrefetchScalarGridSpec(
            num_scalar_prefetch=1, grid=(2, nb),
            in_specs=[pl.BlockSpec((None, tr, cdim), lambda l, i, k_ref: (l, i, 0))], out_specs=out_spec),
        out_shape=jax.ShapeDtypeStruct(tuple(shp), BF16),
        compiler_params=_params(("arbitrary", "arbitrary")),
    )(kidx, w)


def _all_gather(name, wholes, axes):
    n = len(wholes)

    def body(*refs):
        ins, outs = refs[:n], refs[n:2 * n]
        send, recv = refs[2 * n:]
        x, y, c = _place()
        sib = (x, y, 1 - c)

        def slab(ref, w, half, chip):
            return _slab(ref[w], axes[w], wholes[w].shape[1 + axes[w]] // 4, chip, half)

        def protocol(me):
            sends, passes = [], []
            for w in range(n):
                for j, (px, py) in enumerate(_peers(me)):
                    cp = pltpu.make_async_remote_copy(slab(ins, w, c, me), slab(outs, w, c, me), send.at[3 * w + j],
                                                      recv.at[3 * w + j], device_id=(px, py, c),
                                                      device_id_type=MESH)
                    cp.start()
                    sends.append(cp)
            for w in range(n):
                for j, (px, py) in enumerate(_peers(me)):
                    got = slab(outs, w, c, 2 * px + py)
                    pltpu.make_async_remote_copy(got, got, send.at[3 * w + j], recv.at[3 * w + j],
                                                 device_id=(px, py, c), device_id_type=MESH).wait_recv()
                    cp = pltpu.make_async_remote_copy(got, got, send.at[3 * (n + w) + j],
                                                      recv.at[3 * (n + w) + j], device_id=sib, device_id_type=MESH)
                    cp.start()
                    passes.append(cp)
            for w in range(n):
                for j, (px, py) in enumerate(_peers(me)):
                    got = slab(outs, w, 1 - c, 2 * px + py)
                    pltpu.make_async_remote_copy(got, got, send.at[3 * (n + w) + j], recv.at[3 * (n + w) + j],
                                                 device_id=sib, device_id_type=MESH).wait_recv()
            for cp in sends + passes:
                cp.wait_send()

        for me in range(4):
            pl.when(2 * x + y == me)(functools.partial(protocol, me))

    return pl.pallas_call(
        body, name=name, in_specs=_hbm_specs(n), out_specs=_hbm_specs(n),
        out_shape=[jax.ShapeDtypeStruct(a.shape, a.dtype) for a in wholes],
        input_output_aliases={w: w for w in range(n)},
        scratch_shapes=[pltpu.SemaphoreType.DMA((6 * n,)), pltpu.SemaphoreType.DMA((6 * n,))],
        compiler_params=pltpu.CompilerParams(has_side_effects=True),
    )(*wholes)


def _swap_halves(name, grads):
    n = len(grads)

    def body(*refs):
        ins, outs, send, recv = refs[:n], refs[n:2 * n], refs[2 * n], refs[2 * n + 1]
        x, y, c = _place()
        cps = [pltpu.make_async_remote_copy(ins[w].at[1 - c], outs[w], send.at[w], recv.at[w],
                                            device_id=(x, y, 1 - c), device_id_type=MESH) for w in range(n)]
        for cp in cps:
            cp.start()
        for cp in cps:
            cp.wait()

    return pl.pallas_call(
        body, name=name, in_specs=_hbm_specs(n), out_specs=_hbm_specs(n),
        out_shape=[jax.ShapeDtypeStruct(g.shape[1:], g.dtype) for g in grads],
        scratch_shapes=[pltpu.SemaphoreType.DMA((n,)), pltpu.SemaphoreType.DMA((n,))],
        compiler_params=pltpu.CompilerParams(has_side_effects=True),
    )(*grads)


def _add_half(name, g, got, cidx, tr):
    _, r, cdim = g.shape

    def body(c_ref, g_ref, r_ref, o_ref):
        o_ref[...] = (g_ref[...] + r_ref[...]).astype(o_ref.dtype)

    return pl.pallas_call(
        body, name=name,
        grid_spec=pltpu.PrefetchScalarGridSpec(
            num_scalar_prefetch=1, grid=(r // tr,),
            in_specs=[pl.BlockSpec((None, tr, cdim), lambda i, c_ref: (c_ref[0], i, 0)),
                      pl.BlockSpec((tr, cdim), lambda i, c_ref: (i, 0))],
            out_specs=pl.BlockSpec((tr, cdim), lambda i, c_ref: (i, 0))),
        out_shape=jax.ShapeDtypeStruct((r, cdim), BF16),
        compiler_params=_params(("arbitrary",)),
    )(cidx, g, got)


def _scatter_chips(name, sums, axes):
    n = len(sums)
    sizes = [s.shape[ax] // 4 for s, ax in zip(sums, axes)]
    out_shape = []
    for s, ax, sz in zip(sums, axes, sizes):
        shp = list(s.shape)
        shp[ax] = sz
        out_shape.append(jax.ShapeDtypeStruct((3, *shp), s.dtype))

    def body(*refs):
        ins, outs, send, recv = refs[:n], refs[n:2 * n], refs[2 * n], refs[2 * n + 1]
        x, y, c = _place()

        def protocol(me):
            cps = []
            for w in range(n):
                for j, (px, py) in enumerate(_peers(me)):
                    cp = pltpu.make_async_remote_copy(_slab(ins[w], axes[w], sizes[w], 2 * px + py), outs[w].at[j],
                                                      send.at[3 * w + j], recv.at[3 * w + j],
                                                      device_id=(px, py, c), device_id_type=MESH)
                    cp.start()
                    cps.append(cp)
            for cp in cps:
                cp.wait()

        for me in range(4):
            pl.when(2 * x + y == me)(functools.partial(protocol, me))

    return pl.pallas_call(
        body, name=name, in_specs=_hbm_specs(n), out_specs=_hbm_specs(n), out_shape=out_shape,
        scratch_shapes=[pltpu.SemaphoreType.DMA((3 * n,)), pltpu.SemaphoreType.DMA((3 * n,))],
        compiler_params=pltpu.CompilerParams(has_side_effects=True),
    )(*sums)


def _add_chips(name, own, got, axis, kidx, tr):
    _, rs, cs = got.shape
    nb = rs // tr

    def body(k_ref, o_ref, g_ref, out_ref):
        out_ref[...] = (o_ref[...].astype(F32) + g_ref[0].astype(F32) + g_ref[1].astype(F32)
                        + g_ref[2].astype(F32))

    if axis == 0:
        own_spec = pl.BlockSpec((tr, cs), lambda i, k_ref: (k_ref[0] * nb + i, 0))
    else:
        own_spec = pl.BlockSpec((tr, cs), lambda i, k_ref: (i, k_ref[0]))
    return pl.pallas_call(
        body, name=name,
        grid_spec=pltpu.PrefetchScalarGridSpec(
            num_scalar_prefetch=1, grid=(nb,),
            in_specs=[own_spec, pl.BlockSpec((3, tr, cs), lambda i, k_ref: (0, i, 0))],
            out_specs=pl.BlockSpec((None, tr, cs), lambda i, k_ref: (k_ref[1], i, 0))),
        out_shape=jax.ShapeDtypeStruct((2, rs, cs), F32),
        compiler_params=_params(("arbitrary",)),
    )(kidx, own, got)


def _join_halves(name, parts):
    n = len(parts)

    def body(*refs):
        ins, outs, send, recv = refs[:n], refs[n:2 * n], refs[2 * n], refs[2 * n + 1]
        x, y, c = _place()
        cps = [pltpu.make_async_remote_copy(ins[w].at[c], outs[w].at[c], send.at[w], recv.at[w],
                                            device_id=(x, y, 1 - c), device_id_type=MESH) for w in range(n)]
        for cp in cps:
            cp.start()
        for w in range(n):
            cps[w].wait_send()
            pltpu.make_async_remote_copy(ins[w].at[c], outs[w].at[1 - c], send.at[w], recv.at[w],
                                         device_id=(x, y, 1 - c), device_id_type=MESH).wait_recv()

    return pl.pallas_call(
        body, name=name, in_specs=_hbm_specs(n), out_specs=_hbm_specs(n),
        out_shape=[jax.ShapeDtypeStruct(p.shape, p.dtype) for p in parts],
        input_output_aliases={w: w for w in range(n)},
        scratch_shapes=[pltpu.SemaphoreType.DMA((n,)), pltpu.SemaphoreType.DMA((n,))],
        compiler_params=pltpu.CompilerParams(has_side_effects=True),
    )(*parts)


def _all_reduce_small(name, pack, fold_rows, groups):
    nr, d = pack.shape

    def body(in_ref, sum_ref, meta_ref, slots, send, recv):
        x, y, c = _place()
        me = 4 * x + 2 * y + c
        slots[me] = in_ref[...]
        cps = []
        for r in range(1, 8):
            rx, ry, rc = r // 4, (r // 2) % 2, r % 2
            peer = (x + rx - 2 * x * rx, y + ry - 2 * y * ry, c + rc - 2 * c * rc)
            cp = pltpu.make_async_remote_copy(in_ref, slots.at[me], send.at[r - 1], recv.at[r - 1],
                                              device_id=peer, device_id_type=MESH)
            cp.start()
            cps.append(cp)
        for cp in cps:
            cp.wait()
        acc = slots[0]
        for dev in range(1, 8):
            acc = acc + slots[dev]
        sum_ref[...] = acc
        fold = acc[0:fold_rows]
        for grp in range(1, groups):
            fold = fold + acc[grp * fold_rows:(grp + 1) * fold_rows]
        meta_ref[...] = fold

    vmem = pl.BlockSpec(memory_space=pltpu.VMEM)
    return pl.pallas_call(
        body, name=name, in_specs=[vmem], out_specs=[vmem, vmem],
        out_shape=[jax.ShapeDtypeStruct((nr, d), F32), jax.ShapeDtypeStruct((fold_rows, d), F32)],
        scratch_shapes=[pltpu.VMEM((8, nr, d), F32), pltpu.SemaphoreType.DMA((7,)), pltpu.SemaphoreType.DMA((7,))],
        compiler_params=pltpu.CompilerParams(has_side_effects=True, vmem_limit_bytes=VMEM_LIMIT),
    )(pack)


def _adamw_math(w, g, m, v):
    m = B1 * m + (1.0 - B1) * g
    v = B2 * v + (1.0 - B2) * (g * g)
    m_hat = m / (1.0 - B1 ** STEP)
    v_hat = v / (1.0 - B2 ** STEP)
    return -LR * (m_hat / (jnp.sqrt(v_hat) + ADAM_EPS) + WD * w), m, v


def _adamw(name, w, g, m, v, tr):
    shape = w.shape
    flat = [a.reshape(-1, shape[-1]) for a in (w, g, m, v)]
    r, cdim = flat[0].shape

    def body(w_ref, g_ref, m_ref, v_ref, d_ref, nm_ref, nv_ref):
        d_ref[...], nm_ref[...], nv_ref[...] = _adamw_math(w_ref[...], g_ref[...], m_ref[...], v_ref[...])

    spec = pl.BlockSpec((tr, cdim), lambda i: (i, 0))
    outs = pl.pallas_call(
        body, name=name, grid=(r // tr,), in_specs=[spec] * 4, out_specs=[spec] * 3,
        out_shape=[jax.ShapeDtypeStruct((r, cdim), F32)] * 3,
        compiler_params=_params(("parallel",)),
    )(*flat)
    return [o.reshape(shape) for o in outs]


def _adamw_small(name, groups):
    n = len(groups)
    shapes = [grp[0].shape for grp in groups]
    flat = [a.reshape(-1, a.shape[-1]) for grp in groups for a in grp]

    def body(*refs):
        ins, outs = refs[:4 * n], refs[4 * n:]
        for i in range(n):
            w_ref, g_ref, m_ref, v_ref = ins[4 * i:4 * i + 4]
            outs[3 * i][...], outs[3 * i + 1][...], outs[3 * i + 2][...] = _adamw_math(
                w_ref[...], g_ref[...], m_ref[...], v_ref[...])

    vmem = pl.BlockSpec(memory_space=pltpu.VMEM)
    out_shape = [jax.ShapeDtypeStruct(flat[4 * i].shape, F32) for i in range(n) for _ in range(3)]
    outs = pl.pallas_call(body, name=name, in_specs=[vmem] * (4 * n), out_specs=[vmem] * (3 * n),
                          out_shape=out_shape)(*flat)
    return [[outs[3 * i + j].reshape(shapes[i]) for j in range(3)] for i in range(n)]


def _block_diag(w_grp):
    g, pg, _ = w_grp.shape
    eye = jnp.eye(g, dtype=w_grp.dtype)
    return (eye[:, None, :, None] * w_grp[:, :, None, :]).reshape(g * pg, g * pg)


def _diag_blocks(m, g):
    pg = m.shape[0] // g
    return jnp.stack([m[i * pg:(i + 1) * pg, i * pg:(i + 1) * pg] for i in range(g)])


def _local_step(x, meta, g_mix, w_in, w_conv, w_pool, pool_scale, w_out, g_mlp, w_up, w_down, g_final, target):
    bl, s, d = x.shape
    depth = g_mix.shape[0]
    lp = PAD + N_META + s
    t = bl * lp
    tm = lp // 2
    rt = lp // 4
    cw = w_conv.shape[2]
    ngrp = w_pool.shape[1]
    ident = lambda acc: (acc,)

    h = jnp.concatenate([jnp.zeros((bl, PAD, d), F32), jnp.broadcast_to(meta[None], (bl, N_META, d)), x],
                        axis=1).reshape(t, d)
    wbd = [_block_diag(w_pool[i]).astype(BF16) for i in range(depth)]
    saved = []
    for i in range(depth):
        hn = _rms_fwd(f"mix_norm{i}", h, g_mix[i], tm)
        (u_cp,) = _mm_nn(f"in_proj_cp{i}", [hn], w_in, i, 0, 4 * cw, tm, 512, ident, [], [F32])
        (qkv,) = _mm_nn(f"in_proj_qkv{i}", [hn], w_in, i, 4 * cw, w_in.shape[2] - 4 * cw, tm, 512, ident, [],
                        [BF16])
        y_cp = _convpool_fwd(f"convpool{i}", u_cp, w_conv[i], wbd[i], pool_scale[i:i + 1], lp, rt)
        y_at, lt = _attn_fwd(f"attn{i}", qkv, bl, lp)
        (h_mid,) = _mm_nn(f"out_proj{i}", [y_cp, y_at], w_out, i, 0, d, tm, 512, lambda acc, res: (acc + res,),
                          [h], [F32])
        hn2 = _rms_fwd(f"mlp_norm{i}", h_mid, g_mlp[i], tm)
        m_pre, act = _mm_nn(f"up_proj{i}", [hn2], w_up, i, 0, w_up.shape[2], tm, 512,
                            lambda acc: (acc, jnp.square(jnp.maximum(acc, 0.0))), [], [BF16, BF16])
        (h_next,) = _mm_nn(f"down_proj{i}", [act], w_down, i, 0, d, tm, 512, lambda acc, res: (acc + res,),
                           [h_mid], [F32])
        saved.append((h, hn, u_cp, qkv, y_cp, y_at, lt, h_mid, hn2, m_pre, act))
        h = h_next

    dh, loss8, dgf8 = _loss_bwd("loss", h, g_final, target, lp)
    grads = {"g_final": dgf8.sum(0)}
    per_layer = {k: [] for k in ("g_mix", "w_in", "w_conv", "w_pool", "pool_scale", "w_out", "g_mlp", "w_up",
                                 "w_down")}
    for i in reversed(range(depth)):
        h_in, hn, u_cp, qkv, y_cp, y_at, lt, h_mid, hn2, m_pre, act = saved[i]
        (dm,) = _mm_nt(f"down_proj_dx{i}", [dh], w_down, i, tm, 512,
                       lambda acc, mp: (acc * (2.0 * jnp.maximum(mp.astype(F32), 0.0)),), [m_pre], [BF16])
        per_layer["w_down"].append(_mm_tn(f"down_proj_dw{i}", act, dh, tm, 1024, 1024))
        per_layer["w_up"].append(_mm_tn(f"up_proj_dw{i}", hn2, dm, tm, 1024, 1024))
        (dhn2,) = _mm_nt(f"up_proj_dx{i}", [dm], w_up, i, tm, 512, ident, [], [F32])
        dh_mid, dg8 = _rms_bwd(f"mlp_norm_bwd{i}", h_mid, dhn2, g_mlp[i], dh, tm)
        per_layer["g_mlp"].append(dg8.sum(0))
        (dy,) = _mm_nt(f"out_proj_dx{i}", [dh_mid], w_out, i, tm, 512, ident, [], [F32])
        per_layer["w_out"].append(jnp.concatenate(
            [_mm_tn(f"out_proj_dw_cp{i}", y_cp, dh_mid, tm, 512, 1024),
             _mm_tn(f"out_proj_dw_at{i}", y_at, dh_mid, tm, 512, 1024)], axis=0))
        dq, dk, dv = _attn_bwd(f"attn_bwd{i}", qkv, lt, dy, bl, lp)
        du_cp, sm, dwbd = _convpool_bwd(f"convpool_bwd{i}", u_cp, dy, w_conv[i], wbd[i], pool_scale[i:i + 1], lp,
                                        rt)
        sm = sm.reshape(4, 8, cw).sum(1)
        per_layer["w_conv"].append(sm[0:3])
        per_layer["pool_scale"].append(sm[3])
        per_layer["w_pool"].append(_diag_blocks(dwbd, ngrp))
        dus = [du_cp, dq, dk, dv]
        per_layer["w_in"].append(jnp.concatenate(
            [_mm_tn(f"in_proj_dw{j}_{i}", hn, du, tm, 1024, du.shape[1]) for j, du in enumerate(dus)], axis=1))
        (dhn,) = _mm_nt(f"in_proj_dx{i}", dus, w_in, i, tm, 512, ident, [], [F32])
        dh, dg8 = _rms_bwd(f"mix_norm_bwd{i}", h_in, dhn, g_mix[i], dh_mid, tm)
        per_layer["g_mix"].append(dg8.sum(0))
    for k, v in per_layer.items():
        grads[k] = jnp.stack(v[::-1])
    return loss8, dh, grads


def kernel(x, meta_tokens, g_mix, w_in, w_conv, w_pool, pool_scale, w_out, g_mlp, w_up, w_down, g_final, loss_target, m_meta_tokens, m_g_mix, m_w_in, m_w_conv, m_w_pool, m_pool_scale, m_w_out, m_g_mlp, m_w_up, m_w_down, m_g_final, v_meta_tokens, v_g_mix, v_w_in, v_w_conv, v_w_pool, v_pool_scale, v_w_out, v_g_mlp, v_w_up, v_w_down, v_g_final):
    bl, s, d = x.shape
    lp = PAD + N_META + s
    xi, yi, ci = _place()
    chip = (2 * xi + yi).astype(jnp.int32)
    cidx, kidx = ci.astype(jnp.int32).reshape(1), chip.reshape(1)
    kc_idx = jnp.stack([chip, ci.astype(jnp.int32)])

    big = {"w_in": (w_in, 1), "w_out": (w_out, 0), "w_up": (w_up, 1), "w_down": (w_down, 0)}
    f_in, f_out, f_up, f_down = _all_gather(
        "gather_weights", [_place_shard(f"place_{k}", big[k][0], big[k][1], kidx, 256) for k in big],
        [big[k][1] for k in big])
    cs = w_conv.shape[2]
    placed = jnp.zeros((32, d), F32)
    placed = lax.dynamic_update_slice(placed, meta_tokens, (0, chip * meta_tokens.shape[1]))
    placed = lax.dynamic_update_slice(placed, w_conv.reshape(-1, cs), (N_META, chip * cs))
    placed = jnp.where(ci == 0, placed, 0.0)
    whole, _ = _all_reduce_small("gather_small", placed, 8, 1)
    meta_full = whole[:N_META]
    conv_full = whole[N_META:N_META + 2 * 3, :4 * cs].reshape(2, 3, 4 * cs)

    loss8, dh0, grads = _local_step(x, meta_full, g_mix, f_in, conv_full, w_pool, pool_scale, f_out, g_mlp, f_up,
                                    f_down, g_final, loss_target)
    loss = lax.psum(jnp.sum(loss8), ("x", "y", "c"))
    dh0 = dh0.reshape(bl, lp, d)
    grad_x = dh0[:, PAD + N_META:]

    names = list(big)
    axes = [big[k][1] for k in names]
    from_sib = _swap_halves("grads_to_sibling", [grads[k] for k in names])
    chip_sums = [_add_half(f"chip_sum_{k}", grads[k], got, cidx, 256) for k, got in zip(names, from_sib)]
    from_chips = _scatter_chips("grads_to_chips", chip_sums, axes)
    reduced = [_add_chips(f"reduce_{k}", own, got, ax, kc_idx, 256)
               for k, own, got, ax in zip(names, chip_sums, from_chips, axes)]
    big_grads = dict(zip(names, _join_halves("grads_join", reduced)))

    cw = w_conv.shape[2] * 4
    pieces = [dh0[:, PAD:PAD + N_META].reshape(bl * N_META, d), grads["g_mix"], grads["g_mlp"],
              grads["g_final"].reshape(1, d),
              jnp.pad(grads["w_conv"].reshape(-1), (0, 2 * d - grads["w_conv"].size)).reshape(2, d),
              jnp.pad(grads["pool_scale"].reshape(-1), (0, d - grads["pool_scale"].size)).reshape(1, d),
              grads["w_pool"].reshape(-1, d)]
    pack = jnp.concatenate(pieces, axis=0)
    summed, meta_sum = _all_reduce_small("small_grads", pack, N_META, bl)
    o = bl * N_META
    g_small = {
        "meta_tokens": lax.dynamic_slice_in_dim(meta_sum, chip * meta_tokens.shape[1], meta_tokens.shape[1], 1),
        "g_mix": summed[o:o + 2], "g_mlp": summed[o + 2:o + 4], "g_final": summed[o + 4],
        "w_conv": lax.dynamic_slice_in_dim(summed[o + 5:o + 7].reshape(-1)[:2 * 3 * cw].reshape(2, 3, cw),
                                           chip * w_conv.shape[2], w_conv.shape[2], 2),
        "pool_scale": summed[o + 7].reshape(-1)[:pool_scale.size].reshape(pool_scale.shape),
        "w_pool": summed[o + 8:].reshape(w_pool.shape),
    }

    weights = dict(meta_tokens=meta_tokens, g_mix=g_mix, w_in=w_in, w_conv=w_conv, w_pool=w_pool,
                   pool_scale=pool_scale, w_out=w_out, g_mlp=g_mlp, w_up=w_up, w_down=w_down, g_final=g_final)
    ms = dict(meta_tokens=m_meta_tokens, g_mix=m_g_mix, w_in=m_w_in, w_conv=m_w_conv, w_pool=m_w_pool,
              pool_scale=m_pool_scale, w_out=m_w_out, g_mlp=m_g_mlp, w_up=m_w_up, w_down=m_w_down,
              g_final=m_g_final)
    vs = dict(meta_tokens=v_meta_tokens, g_mix=v_g_mix, w_in=v_w_in, w_conv=v_w_conv, w_pool=v_w_pool,
              pool_scale=v_pool_scale, w_out=v_w_out, g_mlp=v_g_mlp, w_up=v_w_up, w_down=v_w_down,
              g_final=v_g_final)
    order = list(weights)
    grad = {**g_small, **big_grads}
    upd = {}
    for k in names:
        upd[k] = _adamw(f"adamw_{k}", weights[k], grad[k], ms[k], vs[k], 256)
    small = [k for k in order if k not in big]
    for k, res in zip(small, _adamw_small("adamw_small", [(weights[k], grad[k].reshape(weights[k].shape), ms[k],
                                                           vs[k]) for k in small])):
        upd[k] = res
    grad = {k: grad[k].reshape(weights[k].shape) for k in order}
    return (loss, grad_x, *[grad[k] for k in order], *[upd[k][0] for k in order], *[upd[k][1] for k in order],
            *[upd[k][2] for k in order])
```

```python
import functools

import jax
import jax.numpy as jnp
from jax import lax
from jax.experimental import pallas as pl
from jax.experimental.pallas import tpu as pltpu

F32, BF16 = jnp.float32, jnp.bfloat16
MESH = pl.DeviceIdType.MESH
EPS = 1e-6
N_META = 16
QB = 128
PAD = QB - N_META
HALO = 16
POOL_WINDOWS = (2.0, 4.0, 8.0, 16.0)
HEAD_SCALE = 0.125
LR, B1, B2, ADAM_EPS, WD, STEP = 0.001, 0.9, 0.999, 1e-08, 0.01, 10
VMEM_LIMIT = 56 * 1024 * 1024


def _params(sem=None):
    return pltpu.CompilerParams(dimension_semantics=sem, vmem_limit_bytes=VMEM_LIMIT)


def _nt(a, b):
    return lax.dot_general(a, b, (((1,), (1,)), ((), ())), preferred_element_type=F32)


def _tn(a, b):
    return lax.dot_general(a, b, (((0,), (0,)), ((), ())), preferred_element_type=F32)


def _nn(a, b):
    return jnp.dot(a, b, preferred_element_type=F32)


def _fold8(v):
    r, c = v.shape
    return jnp.sum(v.reshape(r // 8, 8, c), axis=0)


NCH = 512


def _rows_call(name, body, tm, row_ins, consts, row_outs, accs=()):
    t = row_ins[0].shape[0]
    in_specs = [pl.BlockSpec((tm, a.shape[1]), lambda i: (i, 0)) for a in row_ins]
    for a, layer in consts:
        if layer is None:
            in_specs.append(pl.BlockSpec(a.shape, lambda i: (0, 0)))
        else:
            in_specs.append(pl.BlockSpec((None, *a.shape[1:]), lambda i, l=layer: (l, 0, 0)))
    return pl.pallas_call(
        body, name=name, grid=(t // tm,), in_specs=in_specs,
        out_specs=[pl.BlockSpec((tm, c), lambda i: (i, 0)) for c, _ in row_outs]
        + [pl.BlockSpec(s, lambda i: (0, 0)) for s in accs],
        out_shape=[jax.ShapeDtypeStruct((t, c), dt) for c, dt in row_outs]
        + [jax.ShapeDtypeStruct(s, F32) for s in accs],
        compiler_params=_params(("arbitrary",) if accs else ("parallel",)),
    )(*row_ins, *[a for a, _ in consts])


def _norm_parts(x):
    r = lax.rsqrt(jnp.mean(x * x, axis=-1, keepdims=True) + EPS)
    return r, x * r


def _norm_bwd(r, xh, dyn, g):
    w = dyn * g
    return r * (w - xh * jnp.mean(w * xh, axis=-1, keepdims=True))


def _in_proj(name, h, g, w, layer, tm, ncp):
    d, n = h.shape[1], w.shape[2]

    def body(h_ref, g_ref, w_ref, hn_ref, ucp_ref, qkv_ref):
        _, xh = _norm_parts(h_ref[...])
        hn = (xh * g_ref[...]).astype(BF16)
        hn_ref[...] = hn
        for n0 in range(0, n, NCH):
            acc = _nn(hn, w_ref[:, n0:n0 + NCH])
            if n0 < ncp:
                ucp_ref[:, n0:n0 + NCH] = acc
            else:
                qkv_ref[:, n0 - ncp:n0 - ncp + NCH] = acc.astype(BF16)

    return _rows_call(name, body, tm, [h], [(g.reshape(1, d), None), (w, layer)],
                      [(d, BF16), (ncp, F32), (n - ncp, BF16)])


def _out_proj(name, y_cp, y_at, h, w, layer, tm):
    d, k1 = h.shape[1], y_cp.shape[1]

    def body(ycp_ref, yat_ref, h_ref, w_ref, o_ref):
        for n0 in range(0, d, NCH):
            o_ref[:, n0:n0 + NCH] = (h_ref[:, n0:n0 + NCH] + _nn(ycp_ref[...], w_ref[0:k1, n0:n0 + NCH])
                                     + _nn(yat_ref[...], w_ref[k1:, n0:n0 + NCH]))

    return _rows_call(name, body, tm, [y_cp, y_at, h], [(w, layer)], [(d, F32)])[0]


def _up_proj(name, h_mid, g, w, layer, tm):
    d, n = h_mid.shape[1], w.shape[2]

    def body(h_ref, g_ref, w_ref, hn_ref, m_ref, act_ref):
        _, xh = _norm_parts(h_ref[...])
        hn = (xh * g_ref[...]).astype(BF16)
        hn_ref[...] = hn
        for n0 in range(0, n, NCH):
            acc = _nn(hn, w_ref[:, n0:n0 + NCH])
            m_ref[:, n0:n0 + NCH] = acc.astype(BF16)
            act_ref[:, n0:n0 + NCH] = jnp.square(jnp.maximum(acc, 0.0)).astype(BF16)

    return _rows_call(name, body, tm, [h_mid], [(g.reshape(1, d), None), (w, layer)],
                      [(d, BF16), (n, BF16), (n, BF16)])


def _down_proj(name, act, h_mid, w, layer, tm):
    d = h_mid.shape[1]

    def body(a_ref, h_ref, w_ref, o_ref):
        for n0 in range(0, d, NCH):
            o_ref[:, n0:n0 + NCH] = h_ref[:, n0:n0 + NCH] + _nn(a_ref[...], w_ref[:, n0:n0 + NCH])

    return _rows_call(name, body, tm, [act, h_mid], [(w, layer)], [(d, F32)])[0]


def _down_proj_dx(name, dh, m_pre, w, layer, tm):
    n = w.shape[1]

    def body(dh_ref, m_ref, w_ref, dm_ref):
        dhb = dh_ref[...].astype(BF16)
        for n0 in range(0, n, NCH):
            dm_ref[:, n0:n0 + NCH] = (_nt(dhb, w_ref[n0:n0 + NCH, :])
                                      * (2.0 * jnp.maximum(m_ref[:, n0:n0 + NCH].astype(F32), 0.0))).astype(BF16)

    return _rows_call(name, body, tm, [dh, m_pre], [(w, layer)], [(n, BF16)])[0]


def _up_proj_dx(name, dm, h_mid, dh, g, w_up, w_out, layer, tm):
    d = h_mid.shape[1]

    def body(dm_ref, h_ref, dh_ref, g_ref, wup_ref, wout_ref, dhm_ref, dy_ref, dg_ref):
        @pl.when(pl.program_id(0) == 0)
        def _():
            dg_ref[...] = jnp.zeros_like(dg_ref)
        dyn = _nt(dm_ref[...], wup_ref[...])
        r, xh = _norm_parts(h_ref[...])
        dhm = dh_ref[...] + _norm_bwd(r, xh, dyn, g_ref[...])
        dhm_ref[...] = dhm
        dg_ref[...] += _fold8(dyn * xh)
        dy_ref[...] = _nt(dhm.astype(BF16), wout_ref[...])

    return _rows_call(name, body, tm, [dm, h_mid, dh], [(g.reshape(1, d), None), (w_up, layer), (w_out, layer)],
                      [(d, F32), (w_out.shape[1], F32)], [(8, d)])


def _in_proj_dx(name, dus, h, dh_mid, g, w, layer, tm):
    d = h.shape[1]
    ns = [du.shape[1] for du in dus]
    nd = len(dus)

    def body(*refs):
        du_refs = refs[:nd]
        h_ref, dhm_ref, g_ref, w_ref, dh_ref, dg_ref = refs[nd:]

        @pl.when(pl.program_id(0) == 0)
        def _():
            dg_ref[...] = jnp.zeros_like(dg_ref)
        dyn, off = None, 0
        for du_ref, n in zip(du_refs, ns):
            part = _nt(du_ref[...], w_ref[:, off:off + n])
            dyn = part if dyn is None else dyn + part
            off += n
        r, xh = _norm_parts(h_ref[...])
        dh_ref[...] = dhm_ref[...] + _norm_bwd(r, xh, dyn, g_ref[...])
        dg_ref[...] += _fold8(dyn * xh)

    return _rows_call(name, body, tm, [*dus, h, dh_mid], [(g.reshape(1, d), None), (w, layer)], [(d, F32)],
                      [(8, d)])


def _mm_tn(name, a, b, tt, tka, tn):
    t, ka = a.shape
    n = b.shape[1]
    assert t % tt == 0 and ka % tka == 0 and n % tn == 0

    def body(a_ref, b_ref, o_ref):
        @pl.when(pl.program_id(2) == 0)
        def _():
            o_ref[...] = jnp.zeros_like(o_ref)
        o_ref[...] += _tn(a_ref[...].astype(BF16), b_ref[...].astype(BF16))

    return pl.pallas_call(
        body, name=name, grid=(ka // tka, n // tn, t // tt),
        in_specs=[pl.BlockSpec((tt, tka), lambda i, j, s: (s, i)), pl.BlockSpec((tt, tn), lambda i, j, s: (s, j))],
        out_specs=pl.BlockSpec((tka, tn), lambda i, j, s: (i, j)),
        out_shape=jax.ShapeDtypeStruct((ka, n), F32),
        compiler_params=_params(("parallel", "parallel", "arbitrary")),
    )(a, b)


def _loss_bwd(name, h, g, target, lp):
    t, d = h.shape
    bl = target.shape[0]
    nq = lp // QB

    def body(h_ref, g_ref, t_ref, dh_ref, ls_ref, dg_ref):
        b, j = pl.program_id(0), pl.program_id(1)

        @pl.when((b == 0) & (j == 0))
        def _():
            ls_ref[...] = jnp.zeros_like(ls_ref)
            dg_ref[...] = jnp.zeros_like(dg_ref)
        xv = h_ref[...]
        r = lax.rsqrt(jnp.mean(xv * xv, axis=-1, keepdims=True) + EPS)
        xh = xv * r
        gv = g_ref[...]
        err = jnp.where(j >= 1, xh * gv - t_ref[...], 0.0)
        ls_ref[...] += _fold8(err * err) * (0.5 / d)
        dy = err * (1.0 / d)
        w = dy * gv
        dh_ref[...] = r * (w - xh * jnp.mean(w * xh, axis=-1, keepdims=True))
        dg_ref[...] += _fold8(dy * xh)

    return pl.pallas_call(
        body, name=name, grid=(bl, nq),
        in_specs=[pl.BlockSpec((QB, d), lambda b, j: (b * nq + j, 0)), pl.BlockSpec((1, d), lambda b, j: (0, 0)),
                  pl.BlockSpec((None, QB, d), lambda b, j: (b, jnp.maximum(j - 1, 0), 0))],
        out_specs=[pl.BlockSpec((QB, d), lambda b, j: (b * nq + j, 0)), pl.BlockSpec((8, d), lambda b, j: (0, 0)),
                   pl.BlockSpec((8, d), lambda b, j: (0, 0))],
        out_shape=[jax.ShapeDtypeStruct((t, d), F32), jax.ShapeDtypeStruct((8, d), F32),
                   jax.ShapeDtypeStruct((8, d), F32)],
        compiler_params=_params(("arbitrary", "arbitrary")),
    )(h, g.reshape(1, d), target)


def _pool_select(grp, a2, a4, a8, a16):
    return jnp.where(grp == 0, a2, jnp.where(grp == 1, a4, jnp.where(grp == 2, a8, a16)))


def _trailing_sums(v):
    s2 = v + pltpu.roll(v, 1, 0)
    s4 = s2 + pltpu.roll(s2, 2, 0)
    s8 = s4 + pltpu.roll(s4, 4, 0)
    s16 = s8 + pltpu.roll(s8, 8, 0)
    return s2, s4, s8, s16


def _leading_sums(v):
    n = v.shape[0]
    s2 = v + pltpu.roll(v, n - 1, 0)
    s4 = s2 + pltpu.roll(s2, n - 2, 0)
    s8 = s4 + pltpu.roll(s4, n - 4, 0)
    s16 = s8 + pltpu.roll(s8, n - 8, 0)
    return s2, s4, s8, s16


def _convpool_fwd(name, u_cp, wconv, wbd, pscale, lp, r):
    t = u_cp.shape[0]
    cw = u_cp.shape[1] // 4
    tps, hb = lp // r, r // HALO

    def body(cb_ref, cc_ref, cx_ref, pi_ref, cch_ref, cxh_ref, pih_ref, wc_ref, wbd_ref, ps_ref, y_ref):
        i = pl.program_id(0)
        lrow = (i % tps) * r + lax.broadcasted_iota(jnp.int32, (r, 1), 0)
        valid = lrow >= PAD
        xx = jnp.concatenate([cch_ref[...] * cxh_ref[...], cc_ref[...] * cx_ref[...]], axis=0)
        conv = (wc_ref[0:1, :] * pltpu.roll(xx, 2, 0) + wc_ref[1:2, :] * pltpu.roll(xx, 1, 0)
                + wc_ref[2:3, :] * xx)
        y_ref[:, 0:cw] = (cb_ref[...] * conv[HALO:]).astype(y_ref.dtype)
        p = pi_ref[...]
        grp = lax.broadcasted_iota(jnp.int32, (1, cw), 1) // (cw // 4)
        sel = _pool_select(grp, *_trailing_sums(jnp.concatenate([pih_ref[...], p], axis=0)))[HALO:]
        cnt = jnp.maximum(jnp.minimum((lrow - (PAD - 1)).astype(F32), _pool_select(grp, *POOL_WINDOWS)), 1.0)
        pooled = jnp.where(valid, sel / cnt - p, 0.0)
        y_ref[:, cw:2 * cw] = (_nn(pooled.astype(BF16), wbd_ref[...]) * ps_ref[...]).astype(y_ref.dtype)

    def main(col):
        return pl.BlockSpec((r, cw), lambda i: (i, col))

    def prev(col):
        return pl.BlockSpec((HALO, cw), lambda i: (jnp.maximum(i * hb - 1, 0), col))

    def whole(a):
        return pl.BlockSpec(a.shape, lambda i: (0, 0))

    return pl.pallas_call(
        body, name=name, grid=(t // r,),
        in_specs=[main(0), main(1), main(2), main(3), prev(1), prev(2), prev(3), whole(wconv), whole(wbd),
                  whole(pscale)],
        out_specs=pl.BlockSpec((r, 2 * cw), lambda i: (i, 0)),
        out_shape=jax.ShapeDtypeStruct((t, 2 * cw), BF16),
        compiler_params=_params(("parallel",)),
    )(u_cp, u_cp, u_cp, u_cp, u_cp, u_cp, u_cp, wconv, wbd, pscale)


def _convpool_bwd(name, u_cp, dy, wconv, wbd, pscale, lp, r):
    t = u_cp.shape[0]
    cw = u_cp.shape[1] // 4
    tps, hb = lp // r, r // HALO
    e = r + HALO

    def body(cb_ref, cc_ref, cx_ref, pi_ref, cbn_ref, cch_ref, cxh_ref, pih_ref, dyc_ref, dyp_ref, dycn_ref,
             dypn_ref, wc_ref, wbd_ref, ps_ref, du_ref, sm_ref, dwbd_ref):
        i = pl.program_id(0)

        @pl.when(i == 0)
        def _():
            sm_ref[...] = jnp.zeros_like(sm_ref)
            dwbd_ref[...] = jnp.zeros_like(dwbd_ref)
        lrow_e = (i % tps) * r + lax.broadcasted_iota(jnp.int32, (e, 1), 0)
        valid_e = (lrow_e >= PAD) & (lrow_e < lp)
        lrow, valid = lrow_e[:r], lrow_e[:r] >= PAD
        w0, w1, w2 = wc_ref[0:1, :], wc_ref[1:2, :], wc_ref[2:3, :]
        cb, cc, cx = cb_ref[...], cc_ref[...], cx_ref[...]
        prod = cc * cx
        xx = jnp.concatenate([cch_ref[...] * cxh_ref[...], prod], axis=0)
        back1, back2 = pltpu.roll(xx, 1, 0)[HALO:], pltpu.roll(xx, 2, 0)[HALO:]
        dyc = dyc_ref[...]
        du_ref[:, 0:cw] = (dyc * (w0 * back2 + w1 * back1 + w2 * prod)).astype(du_ref.dtype)
        dconv_e = jnp.where(valid_e, jnp.concatenate([dyc * cb, dycn_ref[...] * cbn_ref[...]], axis=0), 0.0)
        dconv = dconv_e[:r]
        dprod = (w2 * dconv + w1 * pltpu.roll(dconv_e, e - 1, 0)[:r] + w0 * pltpu.roll(dconv_e, e - 2, 0)[:r])
        du_ref[:, cw:2 * cw] = (dprod * cx).astype(du_ref.dtype)
        du_ref[:, 2 * cw:3 * cw] = (dprod * cc).astype(du_ref.dtype)
        sm_ref[0:8, :] += _fold8(dconv * back2)
        sm_ref[8:16, :] += _fold8(dconv * back1)
        sm_ref[16:24, :] += _fold8(dconv * prod)
        p = pi_ref[...]
        grp = lax.broadcasted_iota(jnp.int32, (1, cw), 1) // (cw // 4)
        win = _pool_select(grp, *POOL_WINDOWS)
        sel = _pool_select(grp, *_trailing_sums(jnp.concatenate([pih_ref[...], p], axis=0)))[HALO:]
        cnt_e = jnp.maximum(jnp.minimum((lrow_e - (PAD - 1)).astype(F32), win), 1.0)
        pooled = jnp.where(valid, sel / cnt_e[:r] - p, 0.0).astype(BF16)
        dyp = dyp_ref[...]
        sm_ref[24:32, :] += _fold8(dyp * _nn(pooled, wbd_ref[...]))
        dpre_e = (jnp.concatenate([dyp, dypn_ref[...]], axis=0) * ps_ref[...]).astype(BF16)
        dwbd_ref[...] += _tn(pooled, dpre_e[:r])
        dpooled_e = jnp.where(valid_e, _nt(dpre_e, wbd_ref[...]), 0.0)
        ahead = _pool_select(grp, *_leading_sums(dpooled_e / cnt_e))[:r]
        du_ref[:, 3 * cw:4 * cw] = (ahead - dpooled_e[:r]).astype(du_ref.dtype)

    last_halo = t // HALO - 1

    def main(col):
        return pl.BlockSpec((r, cw), lambda i: (i, col))

    def prev(col):
        return pl.BlockSpec((HALO, cw), lambda i: (jnp.maximum(i * hb - 1, 0), col))

    def nxt(col):
        return pl.BlockSpec((HALO, cw), lambda i: (jnp.minimum((i + 1) * hb, last_halo), col))

    def whole(a):
        return pl.BlockSpec(a.shape, lambda i: (0, 0))

    return pl.pallas_call(
        body, name=name, grid=(t // r,),
        in_specs=[main(0), main(1), main(2), main(3), nxt(0), prev(1), prev(2), prev(3), main(0), main(1), nxt(0),
                  nxt(1), whole(wconv), whole(wbd), whole(pscale)],
        out_specs=[pl.BlockSpec((r, 4 * cw), lambda i: (i, 0)), pl.BlockSpec((32, cw), lambda i: (0, 0)),
                   pl.BlockSpec((cw, cw), lambda i: (0, 0))],
        out_shape=[jax.ShapeDtypeStruct((t, 4 * cw), BF16), jax.ShapeDtypeStruct((32, cw), F32),
                   jax.ShapeDtypeStruct((cw, cw), F32)],
        compiler_params=_params(("arbitrary",)),
    )(u_cp, u_cp, u_cp, u_cp, u_cp, u_cp, u_cp, u_cp, dy, dy, dy, dy, wconv, wbd, pscale)


KW = 2 * QB
HP = 4


def _cumsum_matrix(before):
    r = lax.broadcasted_iota(jnp.int32, (KW, KW), 0)
    c = lax.broadcasted_iota(jnp.int32, (KW, KW), 1)
    return ((r < c) if before else (r > c)).astype(BF16)


def _running(v, mat):
    m = v.shape[0]
    hi = v.astype(BF16)
    ext = _nn(jnp.concatenate([hi, (v - hi.astype(F32)).astype(BF16)], axis=0), mat)
    return ext[:m] + ext[m:]


def _log_sigmoid(z):
    neg_abs = lax.bitcast_convert_type(lax.bitcast_convert_type(z, jnp.int32) | jnp.int32(-2 ** 31), F32)
    return jnp.minimum(z, 0.0) - jnp.log(1.0 + jnp.exp(neg_abs))


def _stack_heads(v, head0):
    zero = jnp.zeros_like(v)
    return jnp.concatenate([jnp.where(head0, v, zero), jnp.where(head0, zero, v)], axis=0)


def _lanes(hp):
    return slice(hp * QB, (hp + 1) * QB)


def _attn_fwd(name, qkv, bl, lp):
    t = qkv.shape[0]
    nq, nblk = lp // QB, qkv.shape[1] // (3 * HP * QB)

    def body(q_ref, k_ref, v_ref, o_ref, lt_ref):
        qi = pl.program_id(2)
        head0 = lax.broadcasted_iota(jnp.int32, (QB, QB), 1) < QB // 2
        q2 = [_stack_heads(q_ref[:, _lanes(hp)] * jnp.asarray(HEAD_SCALE, BF16), head0) for hp in range(HP)]
        later = _cumsum_matrix(before=False)
        q_pos = qi * QB + (lax.broadcasted_iota(jnp.int32, (2 * QB, KW), 0) & (QB - 1))
        col = lax.broadcasted_iota(jnp.int32, (2 * QB, KW), 1)
        ng = qi // 2 + 1

        def group(g, carry, masked):
            start = pl.multiple_of(jnp.minimum(g * KW, lp - KW) if masked else g * KW, QB)
            if masked:
                k_pos = start + col
                valid = (k_pos < q_pos) & (k_pos >= PAD) & (k_pos >= g * KW)
            z = [_nt(q2[hp], k_ref[pl.ds(start, KW), _lanes(hp)]) for hp in range(HP)]
            logp, after, rs = [], [], []
            for hp in range(HP):
                lp_ = _log_sigmoid(z[hp])
                lk = lp_ - z[hp]
                if masked:
                    lk = jnp.where(valid, lk, 0.0)
                logp.append(lp_)
                rs.append(jnp.sum(lk, axis=1, keepdims=True))
                after.append(_running(lk, later))
            out = []
            for hp in range(HP):
                run, acc = carry[2 * hp], carry[2 * hp + 1]
                a = jnp.exp(logp[hp] + after[hp] + run)
                if masked:
                    a = jnp.where(valid, a, 0.0)
                out += [run + rs[hp], acc + _nn(a.astype(BF16), v_ref[pl.ds(start, KW), _lanes(hp)])]
            return tuple(out)

        carry = (jnp.zeros((2 * QB, 1), F32), jnp.zeros((2 * QB, QB), F32)) * HP
        carry = group(ng - 1, carry, True)
        carry = lax.fori_loop(0, jnp.maximum(ng - 2, 0), lambda i, c: group(ng - 2 - i, c, False), carry)
        carry = lax.fori_loop(0, jnp.minimum(ng - 1, 1), lambda i, c: group(0, c, True), carry)
        for hp in range(HP):
            run, acc = carry[2 * hp], carry[2 * hp + 1]
            o_ref[:, _lanes(hp)] = jnp.where(head0, acc[:QB], acc[QB:]).astype(o_ref.dtype)
            lt_ref[:, _lanes(hp)] = jnp.where(head0, run[:QB], run[QB:])

    wb = HP * QB
    blk = pl.BlockSpec((QB, wb), lambda b, p, i: (b * nq + i, p))
    return pl.pallas_call(
        body, name=name, grid=(bl, nblk, nq),
        in_specs=[blk, pl.BlockSpec((lp, wb), lambda b, p, i: (b, nblk + p)),
                  pl.BlockSpec((lp, wb), lambda b, p, i: (b, 2 * nblk + p))],
        out_specs=[blk, blk],
        out_shape=[jax.ShapeDtypeStruct((t, nblk * wb), BF16), jax.ShapeDtypeStruct((t, nblk * wb), F32)],
        compiler_params=_params(("parallel", "parallel", "arbitrary")),
    )(qkv, qkv, qkv)


def _attn_bwd(name, qkv, lt, dy, bl, lp):
    t = qkv.shape[0]
    nq, nblk = lp // QB, qkv.shape[1] // (3 * HP * QB)

    def body(q_ref, k_ref, v_ref, lt_ref, do_ref, dq_ref, dk_ref, dv_ref, dk_acc, dv_acc):
        qi = pl.program_id(2)

        @pl.when(qi == 0)
        def _():
            dk_acc[...] = jnp.zeros_like(dk_acc)
            dv_acc[...] = jnp.zeros_like(dv_acc)
        lane = lax.broadcasted_iota(jnp.int32, (QB, QB), 1)
        head0 = lane < QB // 2
        q2, do2, total = [], [], []
        for hp in range(HP):
            q2.append(_stack_heads(q_ref[:, _lanes(hp)] * jnp.asarray(HEAD_SCALE, BF16), head0))
            do2.append(_stack_heads(do_ref[:, _lanes(hp)].astype(BF16), head0))
            ltv = lt_ref[:, _lanes(hp)]
            total.append(jnp.concatenate(
                [jnp.sum(jnp.where(lane == 0, ltv, 0.0), axis=1, keepdims=True),
                 jnp.sum(jnp.where(lane == QB // 2, ltv, 0.0), axis=1, keepdims=True)], axis=0))
        later, earlier = _cumsum_matrix(before=False), _cumsum_matrix(before=True)
        q_pos = qi * QB + (lax.broadcasted_iota(jnp.int32, (2 * QB, KW), 0) & (QB - 1))
        col = lax.broadcasted_iota(jnp.int32, (2 * QB, KW), 1)
        ng = qi // 2 + 1

        def group(g, carry, masked):
            start = pl.multiple_of(jnp.minimum(g * KW, lp - KW) if masked else g * KW, QB)
            if masked:
                k_pos = start + col
                valid = (k_pos < q_pos) & (k_pos >= PAD) & (k_pos >= g * KW)
            hps = range(HP)
            kg = [k_ref[pl.ds(start, KW), _lanes(hp)] for hp in hps]
            z = [_nt(q2[hp], kg[hp]) for hp in hps]
            da = [_nt(do2[hp], v_ref[pl.ds(start, KW), _lanes(hp)]) for hp in hps]
            logp, sig, after, rs = [], [], [], []
            for hp in hps:
                lp_ = _log_sigmoid(z[hp])
                lk = lp_ - z[hp]
                if masked:
                    lk = jnp.where(valid, lk, 0.0)
                logp.append(lp_)
                sig.append(jnp.exp(lp_))
                rs.append(jnp.sum(lk, axis=1, keepdims=True))
                after.append(_running(lk, later))
            a, gg, before = [], [], []
            for hp in hps:
                a_ = jnp.exp(logp[hp] + after[hp] + (total[hp] - carry[3 * hp] - rs[hp]))
                if masked:
                    a_ = jnp.where(valid, a_, 0.0)
                a.append(a_.astype(BF16))
                gg.append(a_ * da[hp])
                before.append(_nn(gg[hp].astype(BF16), earlier))
            out = []
            for hp in hps:
                seen, gsum, dq = carry[3 * hp], carry[3 * hp + 1], carry[3 * hp + 2]
                dz = gg[hp] - (gg[hp] + before[hp] + gsum) * sig[hp]
                if masked:
                    dz = jnp.where(valid, dz, 0.0)
                dz = dz.astype(BF16)
                dk_acc[pl.ds(start, KW), _lanes(hp)] += _tn(dz, q2[hp])
                dv_acc[pl.ds(start, KW), _lanes(hp)] += _tn(a[hp], do2[hp])
                out += [seen + rs[hp], gsum + jnp.sum(gg[hp], axis=1, keepdims=True), dq + _nn(dz, kg[hp])]
            return tuple(out)

        col0 = jnp.zeros((2 * QB, 1), F32)
        carry = (col0, col0, jnp.zeros((2 * QB, QB), F32)) * HP
        carry = group(0, carry, True)
        carry = lax.fori_loop(1, ng - 1, lambda g, c: group(g, c, False), carry)
        carry = lax.fori_loop(0, jnp.minimum(ng - 1, 1), lambda i, c: group(ng - 1, c, True), carry)
        for hp in range(HP):
            dq = carry[3 * hp + 2]
            dq_ref[:, _lanes(hp)] = (jnp.where(head0, dq[:QB], dq[QB:]) * HEAD_SCALE).astype(dq_ref.dtype)

        @pl.when(qi == nq - 1)
        def _():
            dk_ref[...] = dk_acc[...].astype(dk_ref.dtype)
            dv_ref[...] = dv_acc[...].astype(dv_ref.dtype)

    wb = HP * QB
    blk = pl.BlockSpec((QB, wb), lambda b, p, i: (b * nq + i, p))
    seq = pl.BlockSpec((lp, wb), lambda b, p, i: (b, p))
    out = jax.ShapeDtypeStruct((t, nblk * wb), BF16)
    return pl.pallas_call(
        body, name=name, grid=(bl, nblk, nq),
        in_specs=[blk, pl.BlockSpec((lp, wb), lambda b, p, i: (b, nblk + p)),
                  pl.BlockSpec((lp, wb), lambda b, p, i: (b, 2 * nblk + p)), blk,
                  pl.BlockSpec((QB, wb), lambda b, p, i: (b * nq + i, nblk + p))],
        out_specs=[blk, seq, seq],
        out_shape=[out, out, out],
        scratch_shapes=[pltpu.VMEM((lp, wb), F32), pltpu.VMEM((lp, wb), F32)],
        compiler_params=_params(("parallel", "parallel", "arbitrary")),
    )(qkv, qkv, qkv, lt, dy)


def _place():
    return lax.axis_index("x"), lax.axis_index("y"), lax.axis_index("c")


def _slab(ref, axis, size, chip, *lead):
    if axis == 0:
        return ref.at[(*lead, pl.ds(chip * size, size), slice(None))]
    return ref.at[(*lead, slice(None), pl.ds(chip * size, size))]


def _peers(chip):
    kx, ky = chip // 2, chip % 2
    return ((1 - kx, ky), (kx, 1 - ky), (1 - kx, 1 - ky))


def _hbm_specs(n):
    return [pl.BlockSpec(memory_space=pl.ANY) for _ in range(n)]


def _place_shard(name, w, axis, kidx, tr):
    _, r, cdim = w.shape
    shp = [2, r, cdim]
    shp[1 + axis] *= 4
    nb = r // tr

    def body(k_ref, w_ref, o_ref):
        o_ref[...] = w_ref[...].astype(o_ref.dtype)

    if axis == 0:
        out_spec = pl.BlockSpec((None, tr, cdim), lambda l, i, k_ref: (l, k_ref[0] * nb + i, 0))
    else:
        out_spec = pl.BlockSpec((None, tr, cdim), lambda l, i, k_ref: (l, i, k_ref[0]))
    return pl.pallas_call(
        body, name=name,
        grid_spec=pltpu.PrefetchScalarGridSpec(
            num_scalar_prefetch=1, grid=(2, nb),
            in_specs=[pl.BlockSpec((None, tr, cdim), lambda l, i, k_ref: (l, i, 0))], out_specs=out_spec),
        out_shape=jax.ShapeDtypeStruct(tuple(shp), BF16),
        compiler_params=_params(("arbitrary", "arbitrary")),
    )(kidx, w)


def _all_gather(name, wholes, axes):
    n = len(wholes)

    def body(*refs):
        ins, outs = refs[:n], refs[n:2 * n]
        send, recv = refs[2 * n:]
        x, y, c = _place()
        sib = (x, y, 1 - c)

        def slab(ref, w, half, chip):
            return _slab(ref[w], axes[w], wholes[w].shape[1 + axes[w]] // 4, chip, half)

        def protocol(me):
            sends, passes = [], []
            for w in range(n):
                for j, (px, py) in enumerate(_peers(me)):
                    cp = pltpu.make_async_remote_copy(slab(ins, w, c, me), slab(outs, w, c, me), send.at[3 * w + j],
                                                      recv.at[3 * w + j], device_id=(px, py, c),
                                                      device_id_type=MESH)
                    cp.start()
                    sends.append(cp)
            for w in range(n):
                for j, (px, py) in enumerate(_peers(me)):
                    got = slab(outs, w, c, 2 * px + py)
                    pltpu.make_async_remote_copy(got, got, send.at[3 * w + j], recv.at[3 * w + j],
                                                 device_id=(px, py, c), device_id_type=MESH).wait_recv()
                    cp = pltpu.make_async_remote_copy(got, got, send.at[3 * (n + w) + j],
                                                      recv.at[3 * (n + w) + j], device_id=sib, device_id_type=MESH)
                    cp.start()
                    passes.append(cp)
            for w in range(n):
                for j, (px, py) in enumerate(_peers(me)):
                    got = slab(outs, w, 1 - c, 2 * px + py)
                    pltpu.make_async_remote_copy(got, got, send.at[3 * (n + w) + j], recv.at[3 * (n + w) + j],
                                                 device_id=sib, device_id_type=MESH).wait_recv()
            for cp in sends + passes:
                cp.wait_send()

        for me in range(4):
            pl.when(2 * x + y == me)(functools.partial(protocol, me))

    return pl.pallas_call(
        body, name=name, in_specs=_hbm_specs(n), out_specs=_hbm_specs(n),
        out_shape=[jax.ShapeDtypeStruct(a.shape, a.dtype) for a in wholes],
        input_output_aliases={w: w for w in range(n)},
        scratch_shapes=[pltpu.SemaphoreType.DMA((6 * n,)), pltpu.SemaphoreType.DMA((6 * n,))],
        compiler_params=pltpu.CompilerParams(has_side_effects=True),
    )(*wholes)


def _swap_halves(name, grads):
    n = len(grads)

    def body(*refs):
        ins, outs, send, recv = refs[:n], refs[n:2 * n], refs[2 * n], refs[2 * n + 1]
        x, y, c = _place()
        cps = [pltpu.make_async_remote_copy(ins[w].at[1 - c], outs[w], send.at[w], recv.at[w],
                                            device_id=(x, y, 1 - c), device_id_type=MESH) for w in range(n)]
        for cp in cps:
            cp.start()
        for cp in cps:
            cp.wait()

    return pl.pallas_call(
        body, name=name, in_specs=_hbm_specs(n), out_specs=_hbm_specs(n),
        out_shape=[jax.ShapeDtypeStruct(g.shape[1:], g.dtype) for g in grads],
        scratch_shapes=[pltpu.SemaphoreType.DMA((n,)), pltpu.SemaphoreType.DMA((n,))],
        compiler_params=pltpu.CompilerParams(has_side_effects=True),
    )(*grads)


def _add_half(name, g, got, cidx, tr):
    _, r, cdim = g.shape

    def body(c_ref, g_ref, r_ref, o_ref):
        o_ref[...] = (g_ref[...] + r_ref[...]).astype(o_ref.dtype)

    return pl.pallas_call(
        body, name=name,
        grid_spec=pltpu.PrefetchScalarGridSpec(
            num_scalar_prefetch=1, grid=(r // tr,),
            in_specs=[pl.BlockSpec((None, tr, cdim), lambda i, c_ref: (c_ref[0], i, 0)),
                      pl.BlockSpec((tr, cdim), lambda i, c_ref: (i, 0))],
            out_specs=pl.BlockSpec((tr, cdim), lambda i, c_ref: (i, 0))),
        out_shape=jax.ShapeDtypeStruct((r, cdim), BF16),
        compiler_params=_params(("arbitrary",)),
    )(cidx, g, got)


def _scatter_chips(name, sums, axes):
    n = len(sums)
    sizes = [s.shape[ax] // 4 for s, ax in zip(sums, axes)]
    out_shape = []
    for s, ax, sz in zip(sums, axes, sizes):
        shp = list(s.shape)
        shp[ax] = sz
        out_shape.append(jax.ShapeDtypeStruct((3, *shp), s.dtype))

    def body(*refs):
        ins, outs, send, recv = refs[:n], refs[n:2 * n], refs[2 * n], refs[2 * n + 1]
        x, y, c = _place()

        def protocol(me):
            cps = []
            for w in range(n):
                for j, (px, py) in enumerate(_peers(me)):
                    cp = pltpu.make_async_remote_copy(_slab(ins[w], axes[w], sizes[w], 2 * px + py), outs[w].at[j],
                                                      send.at[3 * w + j], recv.at[3 * w + j],
                                                      device_id=(px, py, c), device_id_type=MESH)
                    cp.start()
                    cps.append(cp)
            for cp in cps:
                cp.wait()

        for me in range(4):
            pl.when(2 * x + y == me)(functools.partial(protocol, me))

    return pl.pallas_call(
        body, name=name, in_specs=_hbm_specs(n), out_specs=_hbm_specs(n), out_shape=out_shape,
        scratch_shapes=[pltpu.SemaphoreType.DMA((3 * n,)), pltpu.SemaphoreType.DMA((3 * n,))],
        compiler_params=pltpu.CompilerParams(has_side_effects=True),
    )(*sums)


def _add_chips(name, own, got, axis, kidx, tr):
    _, rs, cs = got.shape
    nb = rs // tr

    def body(k_ref, o_ref, g_ref, out_ref):
        out_ref[...] = (o_ref[...].astype(F32) + g_ref[0].astype(F32) + g_ref[1].astype(F32)
                        + g_ref[2].astype(F32))

    if axis == 0:
        own_spec = pl.BlockSpec((tr, cs), lambda i, k_ref: (k_ref[0] * nb + i, 0))
    else:
        own_spec = pl.BlockSpec((tr, cs), lambda i, k_ref: (i, k_ref[0]))
    return pl.pallas_call(
        body, name=name,
        grid_spec=pltpu.PrefetchScalarGridSpec(
            num_scalar_prefetch=1, grid=(nb,),
            in_specs=[own_spec, pl.BlockSpec((3, tr, cs), lambda i, k_ref: (0, i, 0))],
            out_specs=pl.BlockSpec((None, tr, cs), lambda i, k_ref: (k_ref[1], i, 0))),
        out_shape=jax.ShapeDtypeStruct((2, rs, cs), F32),
        compiler_params=_params(("arbitrary",)),
    )(kidx, own, got)


def _join_halves(name, parts):
    n = len(parts)

    def body(*refs):
        ins, outs, send, recv = refs[:n], refs[n:2 * n], refs[2 * n], refs[2 * n + 1]
        x, y, c = _place()
        cps = [pltpu.make_async_remote_copy(ins[w].at[c], outs[w].at[c], send.at[w], recv.at[w],
                                            device_id=(x, y, 1 - c), device_id_type=MESH) for w in range(n)]
        for cp in cps:
            cp.start()
        for w in range(n):
            cps[w].wait_send()
            pltpu.make_async_remote_copy(ins[w].at[c], outs[w].at[1 - c], send.at[w], recv.at[w],
                                         device_id=(x, y, 1 - c), device_id_type=MESH).wait_recv()

    return pl.pallas_call(
        body, name=name, in_specs=_hbm_specs(n), out_specs=_hbm_specs(n),
        out_shape=[jax.ShapeDtypeStruct(p.shape, p.dtype) for p in parts],
        input_output_aliases={w: w for w in range(n)},
        scratch_shapes=[pltpu.SemaphoreType.DMA((n,)), pltpu.SemaphoreType.DMA((n,))],
        compiler_params=pltpu.CompilerParams(has_side_effects=True),
    )(*parts)


def _all_reduce_small(name, pack, fold_rows, groups):
    nr, d = pack.shape

    def body(in_ref, sum_ref, meta_ref, slots, send, recv):
        x, y, c = _place()
        me = 4 * x + 2 * y + c
        slots[me] = in_ref[...]
        cps = []
        for r in range(1, 8):
            rx, ry, rc = r // 4, (r // 2) % 2, r % 2
            peer = (x + rx - 2 * x * rx, y + ry - 2 * y * ry, c + rc - 2 * c * rc)
            cp = pltpu.make_async_remote_copy(in_ref, slots.at[me], send.at[r - 1], recv.at[r - 1],
                                              device_id=peer, device_id_type=MESH)
            cp.start()
            cps.append(cp)
        for cp in cps:
            cp.wait()
        acc = slots[0]
        for dev in range(1, 8):
            acc = acc + slots[dev]
        sum_ref[...] = acc
        fold = acc[0:fold_rows]
        for grp in range(1, groups):
            fold = fold + acc[grp * fold_rows:(grp + 1) * fold_rows]
        meta_ref[...] = fold

    vmem = pl.BlockSpec(memory_space=pltpu.VMEM)
    return pl.pallas_call(
        body, name=name, in_specs=[vmem], out_specs=[vmem, vmem],
        out_shape=[jax.ShapeDtypeStruct((nr, d), F32), jax.ShapeDtypeStruct((fold_rows, d), F32)],
        scratch_shapes=[pltpu.VMEM((8, nr, d), F32), pltpu.SemaphoreType.DMA((7,)), pltpu.SemaphoreType.DMA((7,))],
        compiler_params=pltpu.CompilerParams(has_side_effects=True, vmem_limit_bytes=VMEM_LIMIT),
    )(pack)


def _adamw_math(w, g, m, v):
    m = B1 * m + (1.0 - B1) * g
    v = B2 * v + (1.0 - B2) * (g * g)
    m_hat = m / (1.0 - B1 ** STEP)
    v_hat = v / (1.0 - B2 ** STEP)
    return -LR * (m_hat / (jnp.sqrt(v_hat) + ADAM_EPS) + WD * w), m, v


def _adamw(name, w, g, m, v, tr):
    shape = w.shape
    flat = [a.reshape(-1, shape[-1]) for a in (w, g, m, v)]
    r, cdim = flat[0].shape

    def body(w_ref, g_ref, m_ref, v_ref, d_ref, nm_ref, nv_ref):
        d_ref[...], nm_ref[...], nv_ref[...] = _adamw_math(w_ref[...], g_ref[...], m_ref[...], v_ref[...])

    spec = pl.BlockSpec((tr, cdim), lambda i: (i, 0))
    outs = pl.pallas_call(
        body, name=name, grid=(r // tr,), in_specs=[spec] * 4, out_specs=[spec] * 3,
        out_shape=[jax.ShapeDtypeStruct((r, cdim), F32)] * 3,
        compiler_params=_params(("parallel",)),
    )(*flat)
    return [o.reshape(shape) for o in outs]


def _adamw_small(name, groups):
    n = len(groups)
    shapes = [grp[0].shape for grp in groups]
    flat = [a.reshape(-1, a.shape[-1]) for grp in groups for a in grp]

    def body(*refs):
        ins, outs = refs[:4 * n], refs[4 * n:]
        for i in range(n):
            w_ref, g_ref, m_ref, v_ref = ins[4 * i:4 * i + 4]
            outs[3 * i][...], outs[3 * i + 1][...], outs[3 * i + 2][...] = _adamw_math(
                w_ref[...], g_ref[...], m_ref[...], v_ref[...])

    vmem = pl.BlockSpec(memory_space=pltpu.VMEM)
    out_shape = [jax.ShapeDtypeStruct(flat[4 * i].shape, F32) for i in range(n) for _ in range(3)]
    outs = pl.pallas_call(body, name=name, in_specs=[vmem] * (4 * n), out_specs=[vmem] * (3 * n),
                          out_shape=out_shape)(*flat)
    return [[outs[3 * i + j].reshape(shapes[i]) for j in range(3)] for i in range(n)]


def _block_diag(w_grp):
    g, pg, _ = w_grp.shape
    eye = jnp.eye(g, dtype=w_grp.dtype)
    return (eye[:, None, :, None] * w_grp[:, :, None, :]).reshape(g * pg, g * pg)


def _diag_blocks(m, g):
    pg = m.shape[0] // g
    return jnp.stack([m[i * pg:(i + 1) * pg, i * pg:(i + 1) * pg] for i in range(g)])


def _local_step(x, meta, g_mix, w_in, w_conv, w_pool, pool_scale, w_out, g_mlp, w_up, w_down, g_final, target):
    bl, s, d = x.shape
    depth = g_mix.shape[0]
    lp = PAD + N_META + s
    t = bl * lp
    tt = lp // 2
    tm = lp // 4
    rt = tm
    cw = w_conv.shape[2]
    ngrp = w_pool.shape[1]

    h = jnp.concatenate([jnp.zeros((bl, PAD, d), F32), jnp.broadcast_to(meta[None], (bl, N_META, d)), x],
                        axis=1).reshape(t, d)
    wbd = [_block_diag(w_pool[i]).astype(BF16) for i in range(depth)]
    saved = []
    for i in range(depth):
        hn, u_cp, qkv = _in_proj(f"in_proj{i}", h, g_mix[i], w_in, i, tm, 4 * cw)
        y_cp = _convpool_fwd(f"convpool{i}", u_cp, w_conv[i], wbd[i], pool_scale[i:i + 1], lp, rt)
        y_at, lt = _attn_fwd(f"attn{i}", qkv, bl, lp)
        h_mid = _out_proj(f"out_proj{i}", y_cp, y_at, h, w_out, i, tm)
        hn2, m_pre, act = _up_proj(f"up_proj{i}", h_mid, g_mlp[i], w_up, i, tm)
        h_next = _down_proj(f"down_proj{i}", act, h_mid, w_down, i, tm)
        saved.append((h, hn, u_cp, qkv, y_cp, y_at, lt, h_mid, hn2, m_pre, act))
        h = h_next

    dh, loss8, dgf8 = _loss_bwd("loss", h, g_final, target, lp)
    grads = {"g_final": dgf8.sum(0)}
    per_layer = {k: [] for k in ("g_mix", "w_in", "w_conv", "w_pool", "pool_scale", "w_out", "g_mlp", "w_up",
                                 "w_down")}
    for i in reversed(range(depth)):
        h_in, hn, u_cp, qkv, y_cp, y_at, lt, h_mid, hn2, m_pre, act = saved[i]
        dm = _down_proj_dx(f"down_proj_dx{i}", dh, m_pre, w_down, i, tm)
        per_layer["w_down"].append(_mm_tn(f"down_proj_dw{i}", act, dh, tt, 1024, 1024))
        per_layer["w_up"].append(_mm_tn(f"up_proj_dw{i}", hn2, dm, tt, 1024, 1024))
        dh_mid, dy, dg8 = _up_proj_dx(f"up_proj_dx{i}", dm, h_mid, dh, g_mlp[i], w_up, w_out, i, tm)
        per_layer["g_mlp"].append(dg8.sum(0))
        per_layer["w_out"].append(jnp.concatenate(
            [_mm_tn(f"out_proj_dw_cp{i}", y_cp, dh_mid, tt, 512, 1024),
             _mm_tn(f"out_proj_dw_at{i}", y_at, dh_mid, tt, 512, 1024)], axis=0))
        dq, dk, dv = _attn_bwd(f"attn_bwd{i}", qkv, lt, dy, bl, lp)
        du_cp, sm, dwbd = _convpool_bwd(f"convpool_bwd{i}", u_cp, dy, w_conv[i], wbd[i], pool_scale[i:i + 1], lp,
                                        rt)
        sm = sm.reshape(4, 8, cw).sum(1)
        per_layer["w_conv"].append(sm[0:3])
        per_layer["pool_scale"].append(sm[3])
        per_layer["w_pool"].append(_diag_blocks(dwbd, ngrp))
        dus = [du_cp, dq, dk, dv]
        per_layer["w_in"].append(jnp.concatenate(
            [_mm_tn(f"in_proj_dw{j}_{i}", hn, du, tt, 1024, du.shape[1]) for j, du in enumerate(dus)], axis=1))
        dh, dg8 = _in_proj_dx(f"in_proj_dx{i}", dus, h_in, dh_mid, g_mix[i], w_in, i, tm)
        per_layer["g_mix"].append(dg8.sum(0))
    for k, v in per_layer.items():
        grads[k] = jnp.stack(v[::-1])
    return loss8, dh, grads


def kernel(x, meta_tokens, g_mix, w_in, w_conv, w_pool, pool_scale, w_out, g_mlp, w_up, w_down, g_final, loss_target, m_meta_tokens, m_g_mix, m_w_in, m_w_conv, m_w_pool, m_pool_scale, m_w_out, m_g_mlp, m_w_up, m_w_down, m_g_final, v_meta_tokens, v_g_mix, v_w_in, v_w_conv, v_w_pool, v_pool_scale, v_w_out, v_g_mlp, v_w_up, v_w_down, v_g_final):
    bl, s, d = x.shape
    lp = PAD + N_META + s
    xi, yi, ci = _place()
    chip = (2 * xi + yi).astype(jnp.int32)
    cidx, kidx = ci.astype(jnp.int32).reshape(1), chip.reshape(1)
    kc_idx = jnp.stack([chip, ci.astype(jnp.int32)])

    big = {"w_in": (w_in, 1), "w_out": (w_out, 0), "w_up": (w_up, 1), "w_down": (w_down, 0)}
    f_in, f_out, f_up, f_down = _all_gather(
        "gather_weights", [_place_shard(f"place_{k}", big[k][0], big[k][1], kidx, 256) for k in big],
        [big[k][1] for k in big])
    cs = w_conv.shape[2]
    placed = jnp.zeros((32, d), F32)
    placed = lax.dynamic_update_slice(placed, meta_tokens, (0, chip * meta_tokens.shape[1]))
    placed = lax.dynamic_update_slice(placed, w_conv.reshape(-1, cs), (N_META, chip * cs))
    placed = jnp.where(ci == 0, placed, 0.0)
    whole, _ = _all_reduce_small("gather_small", placed, 8, 1)
    meta_full = whole[:N_META]
    conv_full = whole[N_META:N_META + 2 * 3, :4 * cs].reshape(2, 3, 4 * cs)

    loss8, dh0, grads = _local_step(x, meta_full, g_mix, f_in, conv_full, w_pool, pool_scale, f_out, g_mlp, f_up,
                                    f_down, g_final, loss_target)
    loss = lax.psum(jnp.sum(loss8), ("x", "y", "c"))
    dh0 = dh0.reshape(bl, lp, d)
    grad_x = dh0[:, PAD + N_META:]

    names = list(big)
    axes = [big[k][1] for k in names]
    from_sib = _swap_halves("grads_to_sibling", [grads[k] for k in names])
    chip_sums = [_add_half(f"chip_sum_{k}", grads[k], got, cidx, 256) for k, got in zip(names, from_sib)]
    from_chips = _scatter_chips("grads_to_chips", chip_sums, axes)
    reduced = [_add_chips(f"reduce_{k}", own, got, ax, kc_idx, 256)
               for k, own, got, ax in zip(names, chip_sums, from_chips, axes)]
    big_grads = dict(zip(names, _join_halves("grads_join", reduced)))

    cw = w_conv.shape[2] * 4
    pieces = [dh0[:, PAD:PAD + N_META].reshape(bl * N_META, d), grads["g_mix"], grads["g_mlp"],
              grads["g_final"].reshape(1, d),
              jnp.pad(grads["w_conv"].reshape(-1), (0, 2 * d - grads["w_conv"].size)).reshape(2, d),
              jnp.pad(grads["pool_scale"].reshape(-1), (0, d - grads["pool_scale"].size)).reshape(1, d),
              grads["w_pool"].reshape(-1, d)]
    pack = jnp.concatenate(pieces, axis=0)
    summed, meta_sum = _all_reduce_small("small_grads", pack, N_META, bl)
    o = bl * N_META
    g_small = {
        "meta_tokens": lax.dynamic_slice_in_dim(meta_sum, chip * meta_tokens.shape[1], meta_tokens.shape[1], 1),
        "g_mix": summed[o:o + 2], "g_mlp": summed[o + 2:o + 4], "g_final": summed[o + 4],
        "w_conv": lax.dynamic_slice_in_dim(summed[o + 5:o + 7].reshape(-1)[:2 * 3 * cw].reshape(2, 3, cw),
                                           chip * w_conv.shape[2], w_conv.shape[2], 2),
        "pool_scale": summed[o + 7].reshape(-1)[:pool_scale.size].reshape(pool_scale.shape),
        "w_pool": summed[o + 8:].reshape(w_pool.shape),
    }

    weights = dict(meta_tokens=meta_tokens, g_mix=g_mix, w_in=w_in, w_conv=w_conv, w_pool=w_pool,
                   pool_scale=pool_scale, w_out=w_out, g_mlp=g_mlp, w_up=w_up, w_down=w_down, g_final=g_final)
    ms = dict(meta_tokens=m_meta_tokens, g_mix=m_g_mix, w_in=m_w_in, w_conv=m_w_conv, w_pool=m_w_pool,
              pool_scale=m_pool_scale, w_out=m_w_out, g_mlp=m_g_mlp, w_up=m_w_up, w_down=m_w_down,
              g_final=m_g_final)
    vs = dict(meta_tokens=v_meta_tokens, g_mix=v_g_mix, w_in=v_w_in, w_conv=v_w_conv, w_pool=v_w_pool,
              pool_scale=v_pool_scale, w_out=v_w_out, g_mlp=v_g_mlp, w_up=v_w_up, w_down=v_w_down,
              g_final=v_g_final)
    order = list(weights)
    grad = {**g_small, **big_grads}
    upd = {}
    for k in names:
        upd[k] = _adamw(f"adamw_{k}", weights[k], grad[k], ms[k], vs[k], 256)
    small = [k for k in order if k not in big]
    for k, res in zip(small, _adamw_small("adamw_small", [(weights[k], grad[k].reshape(weights[k].shape), ms[k],
                                                           vs[k]) for k in small])):
        upd[k] = res
    grad = {k: grad[k].reshape(weights[k].shape) for k in order}
    return (loss, grad_x, *[grad[k] for k in order], *[upd[k][0] for k in order], *[upd[k][1] for k in order],
            *[upd[k][2] for k in order])
```

```python
import functools

import jax
import jax.numpy as jnp
from jax import lax
from jax.experimental import pallas as pl
from jax.experimental.pallas import tpu as pltpu

F32, BF16 = jnp.float32, jnp.bfloat16
MESH = pl.DeviceIdType.MESH
EPS = 1e-6
N_META = 16
QB = 128
PAD = QB - N_META
HALO = 16
POOL_WINDOWS = (2.0, 4.0, 8.0, 16.0)
HEAD_SCALE = 0.125
LR, B1, B2, ADAM_EPS, WD, STEP = 0.001, 0.9, 0.999, 1e-08, 0.01, 10
VMEM_LIMIT = 56 * 1024 * 1024


def _params(sem=None):
    return pltpu.CompilerParams(dimension_semantics=sem, vmem_limit_bytes=VMEM_LIMIT)


def _nt(a, b):
    return lax.dot_general(a, b, (((1,), (1,)), ((), ())), preferred_element_type=F32)


def _tn(a, b):
    return lax.dot_general(a, b, (((0,), (0,)), ((), ())), preferred_element_type=F32)


def _nn(a, b):
    return jnp.dot(a, b, preferred_element_type=F32)


def _fold8(v):
    r, c = v.shape
    return jnp.sum(v.reshape(r // 8, 8, c), axis=0)


NCH = 512


def _rows_call(name, body, tm, row_ins, consts, row_outs, accs=()):
    t = row_ins[0].shape[0]
    in_specs = [pl.BlockSpec((tm, a.shape[1]), lambda i: (i, 0)) for a in row_ins]
    for a, layer in consts:
        if layer is None:
            in_specs.append(pl.BlockSpec(a.shape, lambda i: (0, 0)))
        else:
            in_specs.append(pl.BlockSpec((None, *a.shape[1:]), lambda i, l=layer: (l, 0, 0)))
    return pl.pallas_call(
        body, name=name, grid=(t // tm,), in_specs=in_specs,
        out_specs=[pl.BlockSpec((tm, c), lambda i: (i, 0)) for c, _ in row_outs]
        + [pl.BlockSpec(s, lambda i: (0, 0)) for s in accs],
        out_shape=[jax.ShapeDtypeStruct((t, c), dt) for c, dt in row_outs]
        + [jax.ShapeDtypeStruct(s, F32) for s in accs],
        compiler_params=_params(("arbitrary",) if accs else ("parallel",)),
    )(*row_ins, *[a for a, _ in consts])


def _norm_parts(x):
    r = lax.rsqrt(jnp.mean(x * x, axis=-1, keepdims=True) + EPS)
    return r, x * r


def _norm_bwd(r, xh, dyn, g):
    w = dyn * g
    return r * (w - xh * jnp.mean(w * xh, axis=-1, keepdims=True))


def _in_proj(name, h, g, w, layer, tm, ncp):
    d, n = h.shape[1], w.shape[2]

    def body(h_ref, g_ref, w_ref, hn_ref, ucp_ref, qkv_ref):
        _, xh = _norm_parts(h_ref[...])
        hn = (xh * g_ref[...]).astype(BF16)
        hn_ref[...] = hn
        for n0 in range(0, n, NCH):
            acc = _nn(hn, w_ref[:, n0:n0 + NCH])
            if n0 < ncp:
                ucp_ref[:, n0:n0 + NCH] = acc
            else:
                qkv_ref[:, n0 - ncp:n0 - ncp + NCH] = acc.astype(BF16)

    return _rows_call(name, body, tm, [h], [(g.reshape(1, d), None), (w, layer)],
                      [(d, BF16), (ncp, F32), (n - ncp, BF16)])


def _out_proj(name, y_cp, y_at, h, w, layer, tm):
    d, k1 = h.shape[1], y_cp.shape[1]

    def body(ycp_ref, yat_ref, h_ref, w_ref, o_ref):
        for n0 in range(0, d, NCH):
            o_ref[:, n0:n0 + NCH] = (h_ref[:, n0:n0 + NCH] + _nn(ycp_ref[...], w_ref[0:k1, n0:n0 + NCH])
                                     + _nn(yat_ref[...], w_ref[k1:, n0:n0 + NCH]))

    return _rows_call(name, body, tm, [y_cp, y_at, h], [(w, layer)], [(d, F32)])[0]


def _up_proj(name, h_mid, g, w, layer, tm):
    d, n = h_mid.shape[1], w.shape[2]

    def body(h_ref, g_ref, w_ref, hn_ref, m_ref, act_ref):
        _, xh = _norm_parts(h_ref[...])
        hn = (xh * g_ref[...]).astype(BF16)
        hn_ref[...] = hn
        for n0 in range(0, n, NCH):
            acc = _nn(hn, w_ref[:, n0:n0 + NCH])
            m_ref[:, n0:n0 + NCH] = acc.astype(BF16)
            act_ref[:, n0:n0 + NCH] = jnp.square(jnp.maximum(acc, 0.0)).astype(BF16)

    return _rows_call(name, body, tm, [h_mid], [(g.reshape(1, d), None), (w, layer)],
                      [(d, BF16), (n, BF16), (n, BF16)])


def _down_proj(name, act, h_mid, w, layer, tm):
    d = h_mid.shape[1]

    def body(a_ref, h_ref, w_ref, o_ref):
        for n0 in range(0, d, NCH):
            o_ref[:, n0:n0 + NCH] = h_ref[:, n0:n0 + NCH] + _nn(a_ref[...], w_ref[:, n0:n0 + NCH])

    return _rows_call(name, body, tm, [act, h_mid], [(w, layer)], [(d, F32)])[0]


def _down_proj_dx(name, dh, m_pre, w, layer, tm):
    n = w.shape[1]

    def body(dh_ref, m_ref, w_ref, dm_ref):
        dhb = dh_ref[...].astype(BF16)
        for n0 in range(0, n, NCH):
            dm_ref[:, n0:n0 + NCH] = (_nt(dhb, w_ref[n0:n0 + NCH, :])
                                      * (2.0 * jnp.maximum(m_ref[:, n0:n0 + NCH].astype(F32), 0.0))).astype(BF16)

    return _rows_call(name, body, tm, [dh, m_pre], [(w, layer)], [(n, BF16)])[0]


def _up_proj_dx(name, dm, h_mid, dh, g, w_up, w_out, layer, tm):
    d = h_mid.shape[1]

    def body(dm_ref, h_ref, dh_ref, g_ref, wup_ref, wout_ref, dhm_ref, dy_ref, dg_ref):
        @pl.when(pl.program_id(0) == 0)
        def _():
            dg_ref[...] = jnp.zeros_like(dg_ref)
        dyn = _nt(dm_ref[...], wup_ref[...])
        r, xh = _norm_parts(h_ref[...])
        dhm = dh_ref[...] + _norm_bwd(r, xh, dyn, g_ref[...])
        dhm_ref[...] = dhm
        dg_ref[...] += _fold8(dyn * xh)
        dy_ref[...] = _nt(dhm.astype(BF16), wout_ref[...])

    return _rows_call(name, body, tm, [dm, h_mid, dh], [(g.reshape(1, d), None), (w_up, layer), (w_out, layer)],
                      [(d, F32), (w_out.shape[1], F32)], [(8, d)])


def _in_proj_dx(name, dus, h, dh_mid, g, w, layer, tm):
    d = h.shape[1]
    ns = [du.shape[1] for du in dus]
    nd = len(dus)

    def body(*refs):
        du_refs = refs[:nd]
        h_ref, dhm_ref, g_ref, w_ref, dh_ref, dg_ref = refs[nd:]

        @pl.when(pl.program_id(0) == 0)
        def _():
            dg_ref[...] = jnp.zeros_like(dg_ref)
        dyn, off = None, 0
        for du_ref, n in zip(du_refs, ns):
            part = _nt(du_ref[...], w_ref[:, off:off + n])
            dyn = part if dyn is None else dyn + part
            off += n
        r, xh = _norm_parts(h_ref[...])
        dh_ref[...] = dhm_ref[...] + _norm_bwd(r, xh, dyn, g_ref[...])
        dg_ref[...] += _fold8(dyn * xh)

    return _rows_call(name, body, tm, [*dus, h, dh_mid], [(g.reshape(1, d), None), (w, layer)], [(d, F32)],
                      [(8, d)])


def _mm_tn(name, a, b, tt, tka, tn, into, shape, layer, row_off, col_off):
    t, ka = a.shape
    n = b.shape[1]
    assert t % tt == 0 and ka % tka == 0 and n % tn == 0 and row_off % tka == 0 and col_off % tn == 0

    def body(a_ref, b_ref, *rest):
        o_ref = rest[-1]

        @pl.when(pl.program_id(2) == 0)
        def _():
            o_ref[...] = jnp.zeros_like(o_ref)
        o_ref[...] += _tn(a_ref[...].astype(BF16), b_ref[...].astype(BF16))

    in_specs = [pl.BlockSpec((tt, tka), lambda i, j, s: (s, i)), pl.BlockSpec((tt, tn), lambda i, j, s: (s, j))]
    args = [a, b]
    if into is not None:
        in_specs.append(pl.BlockSpec(memory_space=pl.ANY))
        args.append(into)
    return pl.pallas_call(
        body, name=name, grid=(ka // tka, n // tn, t // tt), in_specs=in_specs,
        out_specs=pl.BlockSpec((None, tka, tn), lambda i, j, s: (layer, row_off // tka + i, col_off // tn + j)),
        out_shape=jax.ShapeDtypeStruct(shape, F32),
        input_output_aliases={} if into is None else {2: 0},
        compiler_params=_params(("parallel", "parallel", "arbitrary")),
    )(*args)


def _loss_bwd(name, h, g, target, lp):
    t, d = h.shape
    bl = target.shape[0]
    nq = lp // QB

    def body(h_ref, g_ref, t_ref, dh_ref, ls_ref, dg_ref):
        b, j = pl.program_id(0), pl.program_id(1)

        @pl.when((b == 0) & (j == 0))
        def _():
            ls_ref[...] = jnp.zeros_like(ls_ref)
            dg_ref[...] = jnp.zeros_like(dg_ref)
        xv = h_ref[...]
        r = lax.rsqrt(jnp.mean(xv * xv, axis=-1, keepdims=True) + EPS)
        xh = xv * r
        gv = g_ref[...]
        err = jnp.where(j >= 1, xh * gv - t_ref[...], 0.0)
        ls_ref[...] += _fold8(err * err) * (0.5 / d)
        dy = err * (1.0 / d)
        w = dy * gv
        dh_ref[...] = r * (w - xh * jnp.mean(w * xh, axis=-1, keepdims=True))
        dg_ref[...] += _fold8(dy * xh)

    return pl.pallas_call(
        body, name=name, grid=(bl, nq),
        in_specs=[pl.BlockSpec((QB, d), lambda b, j: (b * nq + j, 0)), pl.BlockSpec((1, d), lambda b, j: (0, 0)),
                  pl.BlockSpec((None, QB, d), lambda b, j: (b, jnp.maximum(j - 1, 0), 0))],
        out_specs=[pl.BlockSpec((QB, d), lambda b, j: (b * nq + j, 0)), pl.BlockSpec((8, d), lambda b, j: (0, 0)),
                   pl.BlockSpec((8, d), lambda b, j: (0, 0))],
        out_shape=[jax.ShapeDtypeStruct((t, d), F32), jax.ShapeDtypeStruct((8, d), F32),
                   jax.ShapeDtypeStruct((8, d), F32)],
        compiler_params=_params(("arbitrary", "arbitrary")),
    )(h, g.reshape(1, d), target)


def _pool_select(grp, a2, a4, a8, a16):
    return jnp.where(grp == 0, a2, jnp.where(grp == 1, a4, jnp.where(grp == 2, a8, a16)))


def _trailing_sums(v):
    s2 = v + pltpu.roll(v, 1, 0)
    s4 = s2 + pltpu.roll(s2, 2, 0)
    s8 = s4 + pltpu.roll(s4, 4, 0)
    s16 = s8 + pltpu.roll(s8, 8, 0)
    return s2, s4, s8, s16


def _leading_sums(v):
    n = v.shape[0]
    s2 = v + pltpu.roll(v, n - 1, 0)
    s4 = s2 + pltpu.roll(s2, n - 2, 0)
    s8 = s4 + pltpu.roll(s4, n - 4, 0)
    s16 = s8 + pltpu.roll(s8, n - 8, 0)
    return s2, s4, s8, s16


def _convpool_fwd(name, u_cp, wconv, wbd, pscale, lp, r):
    t = u_cp.shape[0]
    cw = u_cp.shape[1] // 4
    tps, hb = lp // r, r // HALO

    def body(cb_ref, cc_ref, cx_ref, pi_ref, cch_ref, cxh_ref, pih_ref, wc_ref, wbd_ref, ps_ref, y_ref):
        i = pl.program_id(0)
        lrow = (i % tps) * r + lax.broadcasted_iota(jnp.int32, (r, 1), 0)
        valid = lrow >= PAD
        xx = jnp.concatenate([cch_ref[...] * cxh_ref[...], cc_ref[...] * cx_ref[...]], axis=0)
        conv = (wc_ref[0:1, :] * pltpu.roll(xx, 2, 0) + wc_ref[1:2, :] * pltpu.roll(xx, 1, 0)
                + wc_ref[2:3, :] * xx)
        y_ref[:, 0:cw] = (cb_ref[...] * conv[HALO:]).astype(y_ref.dtype)
        p = pi_ref[...]
        grp = lax.broadcasted_iota(jnp.int32, (1, cw), 1) // (cw // 4)
        sel = _pool_select(grp, *_trailing_sums(jnp.concatenate([pih_ref[...], p], axis=0)))[HALO:]
        cnt = jnp.maximum(jnp.minimum((lrow - (PAD - 1)).astype(F32), _pool_select(grp, *POOL_WINDOWS)), 1.0)
        pooled = jnp.where(valid, sel / cnt - p, 0.0)
        y_ref[:, cw:2 * cw] = (_nn(pooled.astype(BF16), wbd_ref[...]) * ps_ref[...]).astype(y_ref.dtype)

    def main(col):
        return pl.BlockSpec((r, cw), lambda i: (i, col))

    def prev(col):
        return pl.BlockSpec((HALO, cw), lambda i: (jnp.maximum(i * hb - 1, 0), col))

    def whole(a):
        return pl.BlockSpec(a.shape, lambda i: (0, 0))

    return pl.pallas_call(
        body, name=name, grid=(t // r,),
        in_specs=[main(0), main(1), main(2), main(3), prev(1), prev(2), prev(3), whole(wconv), whole(wbd),
                  whole(pscale)],
        out_specs=pl.BlockSpec((r, 2 * cw), lambda i: (i, 0)),
        out_shape=jax.ShapeDtypeStruct((t, 2 * cw), BF16),
        compiler_params=_params(("parallel",)),
    )(u_cp, u_cp, u_cp, u_cp, u_cp, u_cp, u_cp, wconv, wbd, pscale)


def _convpool_bwd(name, u_cp, dy, wconv, wbd, pscale, lp, r):
    t = u_cp.shape[0]
    cw = u_cp.shape[1] // 4
    tps, hb = lp // r, r // HALO
    e = r + HALO

    def body(cb_ref, cc_ref, cx_ref, pi_ref, cbn_ref, cch_ref, cxh_ref, pih_ref, dyc_ref, dyp_ref, dycn_ref,
             dypn_ref, wc_ref, wbd_ref, ps_ref, du_ref, sm_ref, dwbd_ref):
        i = pl.program_id(0)

        @pl.when(i == 0)
        def _():
            sm_ref[...] = jnp.zeros_like(sm_ref)
            dwbd_ref[...] = jnp.zeros_like(dwbd_ref)
        lrow_e = (i % tps) * r + lax.broadcasted_iota(jnp.int32, (e, 1), 0)
        valid_e = (lrow_e >= PAD) & (lrow_e < lp)
        lrow, valid = lrow_e[:r], lrow_e[:r] >= PAD
        w0, w1, w2 = wc_ref[0:1, :], wc_ref[1:2, :], wc_ref[2:3, :]
        cb, cc, cx = cb_ref[...], cc_ref[...], cx_ref[...]
        prod = cc * cx
        xx = jnp.concatenate([cch_ref[...] * cxh_ref[...], prod], axis=0)
        back1, back2 = pltpu.roll(xx, 1, 0)[HALO:], pltpu.roll(xx, 2, 0)[HALO:]
        dyc = dyc_ref[...]
        du_ref[:, 0:cw] = (dyc * (w0 * back2 + w1 * back1 + w2 * prod)).astype(du_ref.dtype)
        dconv_e = jnp.where(valid_e, jnp.concatenate([dyc * cb, dycn_ref[...] * cbn_ref[...]], axis=0), 0.0)
        dconv = dconv_e[:r]
        dprod = (w2 * dconv + w1 * pltpu.roll(dconv_e, e - 1, 0)[:r] + w0 * pltpu.roll(dconv_e, e - 2, 0)[:r])
        du_ref[:, cw:2 * cw] = (dprod * cx).astype(du_ref.dtype)
        du_ref[:, 2 * cw:3 * cw] = (dprod * cc).astype(du_ref.dtype)
        sm_ref[0:8, :] += _fold8(dconv * back2)
        sm_ref[8:16, :] += _fold8(dconv * back1)
        sm_ref[16:24, :] += _fold8(dconv * prod)
        p = pi_ref[...]
        grp = lax.broadcasted_iota(jnp.int32, (1, cw), 1) // (cw // 4)
        win = _pool_select(grp, *POOL_WINDOWS)
        sel = _pool_select(grp, *_trailing_sums(jnp.concatenate([pih_ref[...], p], axis=0)))[HALO:]
        cnt_e = jnp.maximum(jnp.minimum((lrow_e - (PAD - 1)).astype(F32), win), 1.0)
        pooled = jnp.where(valid, sel / cnt_e[:r] - p, 0.0).astype(BF16)
        dyp = dyp_ref[...]
        sm_ref[24:32, :] += _fold8(dyp * _nn(pooled, wbd_ref[...]))
        dpre_e = (jnp.concatenate([dyp, dypn_ref[...]], axis=0) * ps_ref[...]).astype(BF16)
        dwbd_ref[...] += _tn(pooled, dpre_e[:r])
        dpooled_e = jnp.where(valid_e, _nt(dpre_e, wbd_ref[...]), 0.0)
        ahead = _pool_select(grp, *_leading_sums(dpooled_e / cnt_e))[:r]
        du_ref[:, 3 * cw:4 * cw] = (ahead - dpooled_e[:r]).astype(du_ref.dtype)

    last_halo = t // HALO - 1

    def main(col):
        return pl.BlockSpec((r, cw), lambda i: (i, col))

    def prev(col):
        return pl.BlockSpec((HALO, cw), lambda i: (jnp.maximum(i * hb - 1, 0), col))

    def nxt(col):
        return pl.BlockSpec((HALO, cw), lambda i: (jnp.minimum((i + 1) * hb, last_halo), col))

    def whole(a):
        return pl.BlockSpec(a.shape, lambda i: (0, 0))

    return pl.pallas_call(
        body, name=name, grid=(t // r,),
        in_specs=[main(0), main(1), main(2), main(3), nxt(0), prev(1), prev(2), prev(3), main(0), main(1), nxt(0),
                  nxt(1), whole(wconv), whole(wbd), whole(pscale)],
        out_specs=[pl.BlockSpec((r, 4 * cw), lambda i: (i, 0)), pl.BlockSpec((32, cw), lambda i: (0, 0)),
                   pl.BlockSpec((cw, cw), lambda i: (0, 0))],
        out_shape=[jax.ShapeDtypeStruct((t, 4 * cw), BF16), jax.ShapeDtypeStruct((32, cw), F32),
                   jax.ShapeDtypeStruct((cw, cw), F32)],
        compiler_params=_params(("arbitrary",)),
    )(u_cp, u_cp, u_cp, u_cp, u_cp, u_cp, u_cp, u_cp, dy, dy, dy, dy, wconv, wbd, pscale)


KW = 2 * QB
HP = 4


def _cumsum_matrix(before):
    r = lax.broadcasted_iota(jnp.int32, (KW, KW), 0)
    c = lax.broadcasted_iota(jnp.int32, (KW, KW), 1)
    return ((r < c) if before else (r > c)).astype(BF16)


def _running(v, mat):
    m = v.shape[0]
    hi = v.astype(BF16)
    ext = _nn(jnp.concatenate([hi, (v - hi.astype(F32)).astype(BF16)], axis=0), mat)
    return ext[:m] + ext[m:]


def _log_sigmoid(z):
    neg_abs = lax.bitcast_convert_type(lax.bitcast_convert_type(z, jnp.int32) | jnp.int32(-2 ** 31), F32)
    return jnp.minimum(z, 0.0) - jnp.log(1.0 + jnp.exp(neg_abs))


def _stack_heads(v, head0):
    zero = jnp.zeros_like(v)
    return jnp.concatenate([jnp.where(head0, v, zero), jnp.where(head0, zero, v)], axis=0)


def _lanes(hp):
    return slice(hp * QB, (hp + 1) * QB)


def _attn_fwd(name, qkv, bl, lp, rider=None):
    t = qkv.shape[0]
    nq, nblk = lp // QB, qkv.shape[1] // (3 * HP * QB)
    ride = _Ride(rider, 3, 2, bl * nblk * nq)

    def body(*refs):
        q_ref, k_ref, v_ref, o_ref, lt_ref = ride.own(refs)
        qi = pl.program_id(2)
        step = (pl.program_id(0) * nblk + pl.program_id(1)) * nq + qi
        ride.before(refs, step)
        head0 = lax.broadcasted_iota(jnp.int32, (QB, QB), 1) < QB // 2
        q2 = [_stack_heads(q_ref[:, _lanes(hp)] * jnp.asarray(HEAD_SCALE, BF16), head0) for hp in range(HP)]
        later = _cumsum_matrix(before=False)
        q_pos = qi * QB + (lax.broadcasted_iota(jnp.int32, (2 * QB, KW), 0) & (QB - 1))
        col = lax.broadcasted_iota(jnp.int32, (2 * QB, KW), 1)
        ng = qi // 2 + 1

        def group(g, carry, masked):
            start = pl.multiple_of(jnp.minimum(g * KW, lp - KW) if masked else g * KW, QB)
            if masked:
                k_pos = start + col
                valid = (k_pos < q_pos) & (k_pos >= PAD) & (k_pos >= g * KW)
            z = [_nt(q2[hp], k_ref[pl.ds(start, KW), _lanes(hp)]) for hp in range(HP)]
            logp, after, rs = [], [], []
            for hp in range(HP):
                lp_ = _log_sigmoid(z[hp])
                lk = lp_ - z[hp]
                if masked:
                    lk = jnp.where(valid, lk, 0.0)
                logp.append(lp_)
                rs.append(jnp.sum(lk, axis=1, keepdims=True))
                after.append(_running(lk, later))
            out = []
            for hp in range(HP):
                run, acc = carry[2 * hp], carry[2 * hp + 1]
                a = jnp.exp(logp[hp] + after[hp] + run)
                if masked:
                    a = jnp.where(valid, a, 0.0)
                out += [run + rs[hp], acc + _nn(a.astype(BF16), v_ref[pl.ds(start, KW), _lanes(hp)])]
            return tuple(out)

        carry = (jnp.zeros((2 * QB, 1), F32), jnp.zeros((2 * QB, QB), F32)) * HP
        carry = group(ng - 1, carry, True)
        carry = lax.fori_loop(0, jnp.maximum(ng - 2, 0), lambda i, c: group(ng - 2 - i, c, False), carry)
        carry = lax.fori_loop(0, jnp.minimum(ng - 1, 1), lambda i, c: group(0, c, True), carry)
        for hp in range(HP):
            run, acc = carry[2 * hp], carry[2 * hp + 1]
            o_ref[:, _lanes(hp)] = jnp.where(head0, acc[:QB], acc[QB:]).astype(o_ref.dtype)
            lt_ref[:, _lanes(hp)] = jnp.where(head0, run[:QB], run[QB:])
        ride.after(refs, step)

    wb = HP * QB
    blk = pl.BlockSpec((QB, wb), lambda b, p, i: (b * nq + i, p))
    return ride.call(
        body, name, (bl, nblk, nq),
        [blk, pl.BlockSpec((lp, wb), lambda b, p, i: (b, nblk + p)),
         pl.BlockSpec((lp, wb), lambda b, p, i: (b, 2 * nblk + p))], [qkv, qkv, qkv], [blk, blk],
        [jax.ShapeDtypeStruct((t, nblk * wb), BF16), jax.ShapeDtypeStruct((t, nblk * wb), F32)], [])


def _attn_bwd(name, qkv, lt, dy, bl, lp, rider=None):
    t = qkv.shape[0]
    nq, nblk = lp // QB, qkv.shape[1] // (3 * HP * QB)
    ride = _Ride(rider, 5, 3, bl * nblk * nq)

    def body(*refs):
        q_ref, k_ref, v_ref, lt_ref, do_ref, dq_ref, dk_ref, dv_ref, dk_acc, dv_acc = ride.own(refs)
        qi = pl.program_id(2)
        step = (pl.program_id(0) * nblk + pl.program_id(1)) * nq + qi
        ride.before(refs, step)

        @pl.when(qi == 0)
        def _():
            dk_acc[...] = jnp.zeros_like(dk_acc)
            dv_acc[...] = jnp.zeros_like(dv_acc)
        lane = lax.broadcasted_iota(jnp.int32, (QB, QB), 1)
        head0 = lane < QB // 2
        q2, do2, total = [], [], []
        for hp in range(HP):
            q2.append(_stack_heads(q_ref[:, _lanes(hp)] * jnp.asarray(HEAD_SCALE, BF16), head0))
            do2.append(_stack_heads(do_ref[:, _lanes(hp)].astype(BF16), head0))
            ltv = lt_ref[:, _lanes(hp)]
            total.append(jnp.concatenate(
                [jnp.sum(jnp.where(lane == 0, ltv, 0.0), axis=1, keepdims=True),
                 jnp.sum(jnp.where(lane == QB // 2, ltv, 0.0), axis=1, keepdims=True)], axis=0))
        later, earlier = _cumsum_matrix(before=False), _cumsum_matrix(before=True)
        q_pos = qi * QB + (lax.broadcasted_iota(jnp.int32, (2 * QB, KW), 0) & (QB - 1))
        col = lax.broadcasted_iota(jnp.int32, (2 * QB, KW), 1)
        ng = qi // 2 + 1

        def group(g, carry, masked):
            start = pl.multiple_of(jnp.minimum(g * KW, lp - KW) if masked else g * KW, QB)
            if masked:
                k_pos = start + col
                valid = (k_pos < q_pos) & (k_pos >= PAD) & (k_pos >= g * KW)
            hps = range(HP)
            kg = [k_ref[pl.ds(start, KW), _lanes(hp)] for hp in hps]
            z = [_nt(q2[hp], kg[hp]) for hp in hps]
            da = [_nt(do2[hp], v_ref[pl.ds(start, KW), _lanes(hp)]) for hp in hps]
            logp, sig, after, rs = [], [], [], []
            for hp in hps:
                lp_ = _log_sigmoid(z[hp])
                lk = lp_ - z[hp]
                if masked:
                    lk = jnp.where(valid, lk, 0.0)
                logp.append(lp_)
                sig.append(jnp.exp(lp_))
                rs.append(jnp.sum(lk, axis=1, keepdims=True))
                after.append(_running(lk, later))
            a, gg, before = [], [], []
            for hp in hps:
                a_ = jnp.exp(logp[hp] + after[hp] + (total[hp] - carry[3 * hp] - rs[hp]))
                if masked:
                    a_ = jnp.where(valid, a_, 0.0)
                a.append(a_.astype(BF16))
                gg.append(a_ * da[hp])
                before.append(_nn(gg[hp].astype(BF16), earlier))
            out = []
            for hp in hps:
                seen, gsum, dq = carry[3 * hp], carry[3 * hp + 1], carry[3 * hp + 2]
                dz = gg[hp] - (gg[hp] + before[hp] + gsum) * sig[hp]
                if masked:
                    dz = jnp.where(valid, dz, 0.0)
                dz = dz.astype(BF16)
                dk_acc[pl.ds(start, KW), _lanes(hp)] += _tn(dz, q2[hp])
                dv_acc[pl.ds(start, KW), _lanes(hp)] += _tn(a[hp], do2[hp])
                out += [seen + rs[hp], gsum + jnp.sum(gg[hp], axis=1, keepdims=True), dq + _nn(dz, kg[hp])]
            return tuple(out)

        col0 = jnp.zeros((2 * QB, 1), F32)
        carry = (col0, col0, jnp.zeros((2 * QB, QB), F32)) * HP
        carry = group(0, carry, True)
        carry = lax.fori_loop(1, ng - 1, lambda g, c: group(g, c, False), carry)
        carry = lax.fori_loop(0, jnp.minimum(ng - 1, 1), lambda i, c: group(ng - 1, c, True), carry)
        for hp in range(HP):
            dq = carry[3 * hp + 2]
            dq_ref[:, _lanes(hp)] = (jnp.where(head0, dq[:QB], dq[QB:]) * HEAD_SCALE).astype(dq_ref.dtype)

        @pl.when(qi == nq - 1)
        def _():
            dk_ref[...] = dk_acc[...].astype(dk_ref.dtype)
            dv_ref[...] = dv_acc[...].astype(dv_ref.dtype)
        ride.after(refs, step)

    wb = HP * QB
    blk = pl.BlockSpec((QB, wb), lambda b, p, i: (b * nq + i, p))
    seq = pl.BlockSpec((lp, wb), lambda b, p, i: (b, p))
    out = jax.ShapeDtypeStruct((t, nblk * wb), BF16)
    return ride.call(
        body, name, (bl, nblk, nq),
        [blk, pl.BlockSpec((lp, wb), lambda b, p, i: (b, nblk + p)),
         pl.BlockSpec((lp, wb), lambda b, p, i: (b, 2 * nblk + p)), blk,
         pl.BlockSpec((QB, wb), lambda b, p, i: (b * nq + i, nblk + p))], [qkv, qkv, qkv, lt, dy],
        [blk, seq, seq], [out, out, out], [pltpu.VMEM((lp, wb), F32), pltpu.VMEM((lp, wb), F32)])


def _place():
    return lax.axis_index("x"), lax.axis_index("y"), lax.axis_index("c")


def _peers(chip):
    kx, ky = chip // 2, chip % 2
    return ((1 - kx, ky), (kx, 1 - ky), (1 - kx, 1 - ky))


def _hbm_specs(n):
    return [pl.BlockSpec(memory_space=pl.ANY) for _ in range(n)]


def _remote(src, dst, send, recv, k, to):
    return pltpu.make_async_remote_copy(src, dst, send.at[k], recv.at[k], device_id=to, device_id_type=MESH)


class _Rider:
    def __init__(self, ins, out_shapes, aliases, nsem, first, mid=None, last=None):
        self.ins, self.out_shapes, self.aliases, self.nsem = list(ins), list(out_shapes), dict(aliases), nsem
        self.first, self.mid, self.last = first, mid, last


def _by_chip(fn):
    def run(ins, outs, send, recv):
        x, y, c = _place()
        for me in range(4):
            pl.when(2 * x + y == me)(functools.partial(fn, ins, outs, send, recv, me, c, (x, y, 1 - c)))
    return run


def _run_rider(name, rider):
    ni, no = len(rider.ins), len(rider.out_shapes)

    def body(*refs):
        args = (refs[:ni], refs[ni:ni + no], refs[ni + no], refs[ni + no + 1])
        for hook in (rider.first, rider.mid, rider.last):
            if hook is not None:
                hook(*args)

    return pl.pallas_call(
        body, name=name, in_specs=_hbm_specs(ni), out_specs=_hbm_specs(no), out_shape=rider.out_shapes,
        input_output_aliases=rider.aliases,
        scratch_shapes=[pltpu.SemaphoreType.DMA((rider.nsem,)), pltpu.SemaphoreType.DMA((rider.nsem,))],
        compiler_params=pltpu.CompilerParams(has_side_effects=True),
    )(*rider.ins)


class _Ride:
    def __init__(self, rider, n_in, n_out, steps):
        self.rider, self.n_in, self.n_out, self.steps = rider, n_in, n_out, steps
        self.ri = len(rider.ins) if rider else 0
        self.ro = len(rider.out_shapes) if rider else 0

    def own(self, refs):
        refs = list(refs)
        a, b = self.n_in, self.n_in + self.ri + self.n_out
        tail = refs[b + self.ro:len(refs) - 2] if self.rider else refs[b + self.ro:]
        return refs[:a] + refs[a + self.ri:b] + tail

    def _args(self, refs):
        a, b = self.n_in, self.n_in + self.ri + self.n_out
        return refs[a:a + self.ri], refs[b:b + self.ro], refs[-2], refs[-1]

    def before(self, refs, step):
        if self.rider is None:
            return
        pl.when(step == 0)(functools.partial(self.rider.first, *self._args(refs)))
        if self.rider.mid is not None:
            pl.when(step == (3 * self.steps) // 4)(functools.partial(self.rider.mid, *self._args(refs)))

    def after(self, refs, step):
        if self.rider is not None and self.rider.last is not None:
            pl.when(step == self.steps - 1)(functools.partial(self.rider.last, *self._args(refs)))

    def call(self, body, name, grid, in_specs, args, out_specs, out_shape, scratch):
        r = self.rider
        if r is None:
            return pl.pallas_call(body, name=name, grid=grid, in_specs=in_specs, out_specs=out_specs,
                                  out_shape=out_shape, scratch_shapes=scratch,
                                  compiler_params=_params(("parallel", "parallel", "arbitrary")))(*args)
        return pl.pallas_call(
            body, name=name, grid=grid, in_specs=in_specs + _hbm_specs(self.ri),
            out_specs=out_specs + _hbm_specs(self.ro), out_shape=out_shape + r.out_shapes,
            input_output_aliases={self.n_in + i: self.n_out + o for i, o in r.aliases.items()},
            scratch_shapes=scratch + [pltpu.SemaphoreType.DMA((r.nsem,)), pltpu.SemaphoreType.DMA((r.nsem,))],
            compiler_params=pltpu.CompilerParams(dimension_semantics=("arbitrary",) * len(grid),
                                                 vmem_limit_bytes=VMEM_LIMIT, has_side_effects=True),
        )(*args, *r.ins)


def _core_view(a, axis):
    l, r, c = a.shape
    return a.reshape(l, 4, 2, r // 8, c) if axis == 0 else a.reshape(l, 2, r // 2, c)


def _shard_view(a):
    l, r, c = a.shape
    return a.reshape(l, 2, r // 2, c)


def _piece(ref, axis, layer, chip, core):
    if axis == 0:
        return ref.at[layer, chip, core]
    cs = ref.shape[-1] // 4
    return ref.at[layer, core, :, pl.ds(chip * cs, cs)]


def _place_shard(name, w, axis, kidx, tr):
    _, r, cdim = w.shape
    shp = [2, r, cdim]
    shp[1 + axis] *= 4
    nb = r // tr

    def body(k_ref, w_ref, o_ref):
        o_ref[...] = w_ref[...].astype(o_ref.dtype)

    if axis == 0:
        out_spec = pl.BlockSpec((None, tr, cdim), lambda l, i, k_ref: (l, k_ref[0] * nb + i, 0))
    else:
        out_spec = pl.BlockSpec((None, tr, cdim), lambda l, i, k_ref: (l, i, k_ref[0]))
    return pl.pallas_call(
        body, name=name,
        grid_spec=pltpu.PrefetchScalarGridSpec(
            num_scalar_prefetch=1, grid=(2, nb),
            in_specs=[pl.BlockSpec((None, tr, cdim), lambda l, i, k_ref: (l, i, 0))], out_specs=out_spec),
        out_shape=jax.ShapeDtypeStruct(tuple(shp), BF16),
        compiler_params=_params(("arbitrary", "arbitrary")),
    )(kidx, w)


def _gather_rider(views, axes, items):
    n = len(items)

    def first(ins, outs, send, recv, me, c, sib):
        for i, (w, l) in enumerate(items):
            for j, (px, py) in enumerate(_peers(me)):
                _remote(_piece(ins[w], axes[w], l, me, c), _piece(outs[w], axes[w], l, me, c), send, recv,
                        3 * i + j, (px, py, c)).start()

    def mid(ins, outs, send, recv, me, c, sib):
        for i, (w, l) in enumerate(items):
            for j, (px, py) in enumerate(_peers(me)):
                got = _piece(outs[w], axes[w], l, 2 * px + py, c)
                _remote(got, got, send, recv, 3 * i + j, (px, py, c)).wait_recv()
                _remote(got, got, send, recv, 3 * (n + i) + j, sib).start()

    def last(ins, outs, send, recv, me, c, sib):
        for i, (w, l) in enumerate(items):
            for j, (px, py) in enumerate(_peers(me)):
                mine, got = _piece(outs[w], axes[w], l, me, c), _piece(outs[w], axes[w], l, 2 * px + py, c)
                theirs = _piece(outs[w], axes[w], l, 2 * px + py, 1 - c)
                _remote(theirs, theirs, send, recv, 3 * (n + i) + j, sib).wait_recv()
                _remote(mine, mine, send, recv, 3 * i + j, (px, py, c)).wait_send()
                _remote(got, got, send, recv, 3 * (n + i) + j, sib).wait_send()

    return _Rider(views, [jax.ShapeDtypeStruct(v.shape, v.dtype) for v in views], {w: w for w in range(len(views))},
                  6 * n, _by_chip(first), _by_chip(mid), _by_chip(last))


def _swap_rider(views, axes, items):
    nv = len(views)

    def part(ref, w, l, core):
        return ref.at[l, :, core] if axes[w] == 0 else ref.at[l, core]

    def copies(ins, outs, send, recv):
        x, y, c = _place()
        return [_remote(part(ins[w], w, l, 1 - c), outs[nv + i], send, recv, i, (x, y, 1 - c))
                for i, (w, l) in enumerate(items)]

    def first(ins, outs, send, recv):
        for cp in copies(ins, outs, send, recv):
            cp.start()

    def last(ins, outs, send, recv):
        for cp in copies(ins, outs, send, recv):
            cp.wait()

    got = [jax.ShapeDtypeStruct(views[w].shape[1:2] + views[w].shape[3:] if axes[w] == 0 else views[w].shape[2:],
                                views[w].dtype) for w, _ in items]
    return _Rider(views, [jax.ShapeDtypeStruct(v.shape, v.dtype) for v in views] + got,
                  {w: w for w in range(nv)}, len(items), first, None, last)


def _add_core(name, view, got, axis, layer, cidx, tr):
    def body(c_ref, g_ref, r_ref, o_ref):
        o_ref[...] = (g_ref[...] + r_ref[...]).astype(o_ref.dtype)

    if axis == 0:
        _, nchip, _, pr, cdim = view.shape
        grid = (nchip, pr // tr)
        specs = [pl.BlockSpec((None, None, None, tr, cdim), lambda k, i, c_ref: (layer, k, c_ref[0], i, 0)),
                 pl.BlockSpec((None, tr, cdim), lambda k, i, c_ref: (k, i, 0))]
        out_spec = pl.BlockSpec((None, tr, cdim), lambda k, i, c_ref: (k, i, 0))
    else:
        _, _, pr, cdim = view.shape
        grid = (pr // tr,)
        specs = [pl.BlockSpec((None, None, tr, cdim), lambda i, c_ref: (layer, c_ref[0], i, 0)),
                 pl.BlockSpec((tr, cdim), lambda i, c_ref: (i, 0))]
        out_spec = pl.BlockSpec((tr, cdim), lambda i, c_ref: (i, 0))
    return pl.pallas_call(
        body, name=name,
        grid_spec=pltpu.PrefetchScalarGridSpec(num_scalar_prefetch=1, grid=grid, in_specs=specs,
                                               out_specs=out_spec),
        out_shape=jax.ShapeDtypeStruct(got.shape, BF16),
        compiler_params=_params(("arbitrary",) * len(grid)),
    )(cidx, view, got)


def _scatter_rider(sums, axes):
    def part(ref, i, chip):
        if axes[i] == 0:
            return ref.at[chip]
        cs = ref.shape[-1] // 4
        return ref.at[:, pl.ds(chip * cs, cs)]

    def copies(ins, outs, send, recv, me, c, sib):
        return [_remote(part(ins[i], i, 2 * px + py), outs[i].at[j], send, recv, 3 * i + j, (px, py, c))
                for i in range(len(sums)) for j, (px, py) in enumerate(_peers(me))]

    def first(*args):
        for cp in copies(*args):
            cp.start()

    def last(*args):
        for cp in copies(*args):
            cp.wait()

    shapes = [jax.ShapeDtypeStruct((3,) + (s.shape[1:] if ax == 0 else (s.shape[0], s.shape[1] // 4)), s.dtype)
              for s, ax in zip(sums, axes)]
    return _Rider(sums, shapes, {}, 3 * len(sums), _by_chip(first), None, _by_chip(last))


def _add_chips(name, own, got, axis, layer, kc_idx, tr, into, shard_shape):
    _, pr, pc = got.shape

    def body(k_ref, o_ref, g_ref, *rest):
        rest[-1][...] = (o_ref[...].astype(F32) + g_ref[0].astype(F32) + g_ref[1].astype(F32)
                         + g_ref[2].astype(F32))

    if axis == 0:
        own_spec = pl.BlockSpec((None, tr, pc), lambda i, k_ref: (k_ref[0], i, 0))
    else:
        own_spec = pl.BlockSpec((tr, pc), lambda i, k_ref: (i, k_ref[0]))
    specs = [own_spec, pl.BlockSpec((3, tr, pc), lambda i, k_ref: (0, i, 0))]
    args = [kc_idx, own, got]
    if into is not None:
        specs.append(pl.BlockSpec(memory_space=pl.ANY))
        args.append(into)
    return pl.pallas_call(
        body, name=name,
        grid_spec=pltpu.PrefetchScalarGridSpec(
            num_scalar_prefetch=1, grid=(pr // tr,), in_specs=specs,
            out_specs=pl.BlockSpec((None, None, tr, pc), lambda i, k_ref: (layer, k_ref[1], i, 0))),
        out_shape=jax.ShapeDtypeStruct(shard_shape, F32),
        input_output_aliases={} if into is None else {3: 0},
        compiler_params=_params(("arbitrary",)),
    )(*args)


def _join_rider(parts):
    def first(ins, outs, send, recv):
        x, y, c = _place()
        for w in range(len(parts)):
            _remote(ins[w].at[:, c], outs[w].at[:, c], send, recv, w, (x, y, 1 - c)).start()

    def last(ins, outs, send, recv):
        x, y, c = _place()
        for w in range(len(parts)):
            _remote(ins[w].at[:, c], outs[w].at[:, c], send, recv, w, (x, y, 1 - c)).wait_send()
            _remote(ins[w].at[:, c], outs[w].at[:, 1 - c], send, recv, w, (x, y, 1 - c)).wait_recv()

    return _Rider(parts, [jax.ShapeDtypeStruct(p.shape, p.dtype) for p in parts],
                  {w: w for w in range(len(parts))}, len(parts), first, None, last)


def _all_reduce_small(name, pack, fold_rows, groups):
    nr, d = pack.shape

    def body(in_ref, sum_ref, meta_ref, slots, send, recv):
        x, y, c = _place()
        me = 4 * x + 2 * y + c
        slots[me] = in_ref[...]
        cps = []
        for r in range(1, 8):
            rx, ry, rc = r // 4, (r // 2) % 2, r % 2
            peer = (x + rx - 2 * x * rx, y + ry - 2 * y * ry, c + rc - 2 * c * rc)
            cp = pltpu.make_async_remote_copy(in_ref, slots.at[me], send.at[r - 1], recv.at[r - 1],
                                              device_id=peer, device_id_type=MESH)
            cp.start()
            cps.append(cp)
        for cp in cps:
            cp.wait()
        acc = slots[0]
        for dev in range(1, 8):
            acc = acc + slots[dev]
        sum_ref[...] = acc
        fold = acc[0:fold_rows]
        for grp in range(1, groups):
            fold = fold + acc[grp * fold_rows:(grp + 1) * fold_rows]
        meta_ref[...] = fold

    vmem = pl.BlockSpec(memory_space=pltpu.VMEM)
    return pl.pallas_call(
        body, name=name, in_specs=[vmem], out_specs=[vmem, vmem],
        out_shape=[jax.ShapeDtypeStruct((nr, d), F32), jax.ShapeDtypeStruct((fold_rows, d), F32)],
        scratch_shapes=[pltpu.VMEM((8, nr, d), F32), pltpu.SemaphoreType.DMA((7,)), pltpu.SemaphoreType.DMA((7,))],
        compiler_params=pltpu.CompilerParams(has_side_effects=True, vmem_limit_bytes=VMEM_LIMIT),
    )(pack)


def _adamw_math(w, g, m, v):
    m = B1 * m + (1.0 - B1) * g
    v = B2 * v + (1.0 - B2) * (g * g)
    m_hat = m / (1.0 - B1 ** STEP)
    v_hat = v / (1.0 - B2 ** STEP)
    return -LR * (m_hat / (jnp.sqrt(v_hat) + ADAM_EPS) + WD * w), m, v


def _adamw(name, w, g, m, v, tr):
    shape = w.shape
    flat = [a.reshape(-1, shape[-1]) for a in (w, g, m, v)]
    r, cdim = flat[0].shape

    def body(w_ref, g_ref, m_ref, v_ref, d_ref, nm_ref, nv_ref):
        d_ref[...], nm_ref[...], nv_ref[...] = _adamw_math(w_ref[...], g_ref[...], m_ref[...], v_ref[...])

    spec = pl.BlockSpec((tr, cdim), lambda i: (i, 0))
    outs = pl.pallas_call(
        body, name=name, grid=(r // tr,), in_specs=[spec] * 4, out_specs=[spec] * 3,
        out_shape=[jax.ShapeDtypeStruct((r, cdim), F32)] * 3,
        compiler_params=_params(("parallel",)),
    )(*flat)
    return [o.reshape(shape) for o in outs]


def _adamw_small(name, groups):
    n = len(groups)
    shapes = [grp[0].shape for grp in groups]
    flat = [a.reshape(-1, a.shape[-1]) for grp in groups for a in grp]

    def body(*refs):
        ins, outs = refs[:4 * n], refs[4 * n:]
        for i in range(n):
            w_ref, g_ref, m_ref, v_ref = ins[4 * i:4 * i + 4]
            outs[3 * i][...], outs[3 * i + 1][...], outs[3 * i + 2][...] = _adamw_math(
                w_ref[...], g_ref[...], m_ref[...], v_ref[...])

    vmem = pl.BlockSpec(memory_space=pltpu.VMEM)
    out_shape = [jax.ShapeDtypeStruct(flat[4 * i].shape, F32) for i in range(n) for _ in range(3)]
    outs = pl.pallas_call(body, name=name, in_specs=[vmem] * (4 * n), out_specs=[vmem] * (3 * n),
                          out_shape=out_shape)(*flat)
    return [[outs[3 * i + j].reshape(shapes[i]) for j in range(3)] for i in range(n)]


def _block_diag(w_grp):
    g, pg, _ = w_grp.shape
    eye = jnp.eye(g, dtype=w_grp.dtype)
    return (eye[:, None, :, None] * w_grp[:, :, None, :]).reshape(g * pg, g * pg)


def _diag_blocks(m, g):
    pg = m.shape[0] // g
    return jnp.stack([m[i * pg:(i + 1) * pg, i * pg:(i + 1) * pg] for i in range(g)])


BIG = ("w_in", "w_out", "w_up", "w_down")
AXES = (1, 0, 1, 0)
W_IN, W_OUT, W_UP, W_DOWN = range(4)


def kernel(x, meta_tokens, g_mix, w_in, w_conv, w_pool, pool_scale, w_out, g_mlp, w_up, w_down, g_final, loss_target, m_meta_tokens, m_g_mix, m_w_in, m_w_conv, m_w_pool, m_pool_scale, m_w_out, m_g_mlp, m_w_up, m_w_down, m_g_final, v_meta_tokens, v_g_mix, v_w_in, v_w_conv, v_w_pool, v_pool_scale, v_w_out, v_g_mlp, v_w_up, v_w_down, v_g_final):
    bl, s, d = x.shape
    depth = g_mix.shape[0]
    assert depth == 2
    lp = PAD + N_META + s
    t = bl * lp
    tt = lp // 2
    tm = lp // 4
    cs = w_conv.shape[2]
    cw = 4 * cs
    ngrp = w_pool.shape[1]
    xi, yi, ci = _place()
    chip = (2 * xi + yi).astype(jnp.int32)
    cidx, kidx = ci.astype(jnp.int32).reshape(1), chip.reshape(1)
    kc_idx = jnp.stack([chip, ci.astype(jnp.int32)])
    shards = (w_in, w_out, w_up, w_down)

    views = [_core_view(_place_shard(f"place_{BIG[w]}", shards[w], AXES[w], kidx, 256), AXES[w]) for w in range(4)]
    views[W_IN] = _run_rider("gather_first", _gather_rider(views[:1], AXES[:1], [(W_IN, 0)]))[0]
    rest = [(w, l) for l in range(depth) for w in range(4) if (w, l) != (W_IN, 0)]

    def whole(w):
        return views[w].reshape(depth, -1, views[w].shape[-1])

    placed = jnp.zeros((32, d), F32)
    placed = lax.dynamic_update_slice(placed, meta_tokens, (0, chip * meta_tokens.shape[1]))
    placed = lax.dynamic_update_slice(placed, w_conv.reshape(-1, cs), (N_META, chip * cs))
    placed = jnp.where(ci == 0, placed, 0.0)
    small, _ = _all_reduce_small("gather_small", placed, 8, 1)
    meta_full = small[:N_META]
    conv_full = small[N_META:N_META + depth * 3, :cw].reshape(depth, 3, cw)

    h = jnp.concatenate([jnp.zeros((bl, PAD, d), F32), jnp.broadcast_to(meta_full[None], (bl, N_META, d)), x],
                        axis=1).reshape(t, d)
    wbd = [_block_diag(w_pool[i]).astype(BF16) for i in range(depth)]
    saved = []
    for i in range(depth):
        hn, u_cp, qkv = _in_proj(f"in_proj{i}", h, g_mix[i], whole(W_IN), i, tm, 4 * cw)
        y_cp = _convpool_fwd(f"convpool{i}", u_cp, conv_full[i], wbd[i], pool_scale[i:i + 1], lp, tm)
        if i == 0:
            y_at, lt, *views = _attn_fwd(f"attn{i}", qkv, bl, lp, _gather_rider(views, AXES, rest))
        else:
            y_at, lt = _attn_fwd(f"attn{i}", qkv, bl, lp)
        h_mid = _out_proj(f"out_proj{i}", y_cp, y_at, h, whole(W_OUT), i, tm)
        hn2, m_pre, act = _up_proj(f"up_proj{i}", h_mid, g_mlp[i], whole(W_UP), i, tm)
        h_next = _down_proj(f"down_proj{i}", act, h_mid, whole(W_DOWN), i, tm)
        saved.append((h, hn, u_cp, qkv, y_cp, y_at, lt, h_mid, hn2, m_pre, act))
        h = h_next

    dh, loss8, dgf8 = _loss_bwd("loss", h, g_final, loss_target, lp)
    loss = lax.psum(jnp.sum(loss8), ("x", "y", "c"))
    per_layer = {k: [None] * depth for k in ("g_mix", "w_conv", "w_pool", "pool_scale", "g_mlp")}

    gw = [None] * 4
    sums, arrived = {}, {}

    def dw(name, a, b, w, layer, tka, tn, row_off=0, col_off=0):
        shape = whole(w).shape
        into = None if gw[w] is None else gw[w].reshape(shape)
        gw[w] = _core_view(_mm_tn(name, a, b, tt, tka, tn, into, shape, layer, row_off, col_off), AXES[w])

    def swap_rider(ws):
        return _swap_rider([gw[w] for w, _ in ws], [AXES[w] for w, _ in ws],
                           [(j, layer) for j, (_, layer) in enumerate(ws)])

    def swapped(ws, outs):
        for j, (w, layer) in enumerate(ws):
            gw[w] = outs[j]
            sums[w, layer] = _add_core(f"chip_sum_{BIG[w]}{layer}", gw[w], outs[len(ws) + j], AXES[w], layer, cidx,
                                       128)

    def scatter_rider(items):
        return _scatter_rider([sums[it] for it in items], [AXES[w] for w, _ in items])

    def bwd_mlp(i, dh):
        _, _, _, _, y_cp, y_at, _, h_mid, hn2, m_pre, act = saved[i]
        dm = _down_proj_dx(f"down_proj_dx{i}", dh, m_pre, whole(W_DOWN), i, tm)
        dw(f"down_proj_dw{i}", act, dh, W_DOWN, i, 1024, 1024)
        dw(f"up_proj_dw{i}", hn2, dm, W_UP, i, 1024, 1024)
        dh_mid, dy, dg8 = _up_proj_dx(f"up_proj_dx{i}", dm, h_mid, dh, g_mlp[i], whole(W_UP), whole(W_OUT), i, tm)
        per_layer["g_mlp"][i] = dg8.sum(0)
        dw(f"out_proj_dw_cp{i}", y_cp, dh_mid, W_OUT, i, 512, 1024)
        dw(f"out_proj_dw_at{i}", y_at, dh_mid, W_OUT, i, 512, 1024, row_off=y_cp.shape[1])
        return dh_mid, dy

    def bwd_mix(i, dh_mid, dy, dus3):
        h_in, hn, u_cp = saved[i][:3]
        du_cp, sm, dwbd = _convpool_bwd(f"convpool_bwd{i}", u_cp, dy, conv_full[i], wbd[i], pool_scale[i:i + 1], lp,
                                        tm)
        sm = sm.reshape(4, 8, cw).sum(1)
        per_layer["w_conv"][i] = sm[0:3]
        per_layer["pool_scale"][i] = sm[3]
        per_layer["w_pool"][i] = _diag_blocks(dwbd, ngrp)
        dus, off = [du_cp, *dus3], 0
        for j, du in enumerate(dus):
            dw(f"in_proj_dw{j}_{i}", hn, du, W_IN, i, 1024, du.shape[1], col_off=off)
            off += du.shape[1]
        dh, dg8 = _in_proj_dx(f"in_proj_dx{i}", dus, h_in, dh_mid, g_mix[i], whole(W_IN), i, tm)
        per_layer["g_mix"][i] = dg8.sum(0)
        return dh

    def attn_bwd(i, dy, rider):
        qkv, lt = saved[i][3], saved[i][6]
        res = _attn_bwd(f"attn_bwd{i}", qkv, lt, dy, bl, lp, rider)
        return res[:3], res[3:]

    mlp_ws = [W_DOWN, W_UP, W_OUT]
    dh_mid, dy = bwd_mlp(1, dh)
    ws = [(w, 1) for w in mlp_ws]
    dus3, outs = attn_bwd(1, dy, swap_rider(ws))
    swapped(ws, outs)
    dh = bwd_mix(1, dh_mid, dy, dus3)
    swapped([(W_IN, 1)], _run_rider("grads_swap_in1", swap_rider([(W_IN, 1)])))

    dh_mid, dy = bwd_mlp(0, dh)
    ws = [(w, 0) for w in mlp_ws]
    swapped(ws, _run_rider("grads_swap0", swap_rider(ws)))
    items = list(sums)
    dus3, outs = attn_bwd(0, dy, scatter_rider(items))
    arrived.update(zip(items, outs))
    dh0 = bwd_mix(0, dh_mid, dy, dus3)
    swapped([(W_IN, 0)], _run_rider("grads_swap_in0", swap_rider([(W_IN, 0)])))
    arrived[W_IN, 0] = _run_rider("grads_scatter_in0", scatter_rider([(W_IN, 0)]))[0]

    finals = []
    for w in range(4):
        rs_, cs_ = shards[w].shape[1:]
        part = None
        for layer in reversed(range(depth)):
            part = _add_chips(f"reduce_{BIG[w]}{layer}", sums[w, layer], arrived[w, layer], AXES[w], layer, kc_idx,
                              128, part, (depth, 2, rs_ // 2, cs_))
        finals.append(part)
    finals = _run_rider("grads_join", _join_rider(finals))
    grad = {BIG[w]: finals[w].reshape(shards[w].shape) for w in range(4)}

    dh0 = dh0.reshape(bl, lp, d)
    grad_x = dh0[:, PAD + N_META:]
    local = {k: jnp.stack(v) for k, v in per_layer.items()}
    pieces = [dh0[:, PAD:PAD + N_META].reshape(bl * N_META, d), local["g_mix"], local["g_mlp"],
              dgf8.sum(0).reshape(1, d),
              jnp.pad(local["w_conv"].reshape(-1), (0, 2 * d - local["w_conv"].size)).reshape(2, d),
              jnp.pad(local["pool_scale"].reshape(-1), (0, d - local["pool_scale"].size)).reshape(1, d),
              local["w_pool"].reshape(-1, d)]
    summed, meta_sum = _all_reduce_small("small_grads", jnp.concatenate(pieces, axis=0), N_META, bl)
    o = bl * N_META
    grad.update({
        "meta_tokens": lax.dynamic_slice_in_dim(meta_sum, chip * meta_tokens.shape[1], meta_tokens.shape[1], 1),
        "g_mix": summed[o:o + 2], "g_mlp": summed[o + 2:o + 4], "g_final": summed[o + 4],
        "w_conv": lax.dynamic_slice_in_dim(summed[o + 5:o + 7].reshape(-1)[:2 * 3 * cw].reshape(2, 3, cw),
                                           chip * cs, cs, 2),
        "pool_scale": summed[o + 7].reshape(-1)[:pool_scale.size].reshape(pool_scale.shape),
        "w_pool": summed[o + 8:].reshape(w_pool.shape),
    })

    weights = dict(meta_tokens=meta_tokens, g_mix=g_mix, w_in=w_in, w_conv=w_conv, w_pool=w_pool,
                   pool_scale=pool_scale, w_out=w_out, g_mlp=g_mlp, w_up=w_up, w_down=w_down, g_final=g_final)
    ms = dict(meta_tokens=m_meta_tokens, g_mix=m_g_mix, w_in=m_w_in, w_conv=m_w_conv, w_pool=m_w_pool,
              pool_scale=m_pool_scale, w_out=m_w_out, g_mlp=m_g_mlp, w_up=m_w_up, w_down=m_w_down,
              g_final=m_g_final)
    vs = dict(meta_tokens=v_meta_tokens, g_mix=v_g_mix, w_in=v_w_in, w_conv=v_w_conv, w_pool=v_w_pool,
              pool_scale=v_pool_scale, w_out=v_w_out, g_mlp=v_g_mlp, w_up=v_w_up, w_down=v_w_down,
              g_final=v_g_final)
    order = list(weights)
    upd = {k: _adamw(f"adamw_{k}", weights[k], grad[k], ms[k], vs[k], 256) for k in BIG}
    little = [k for k in order if k not in BIG]
    for k, res in zip(little, _adamw_small("adamw_small", [(weights[k], grad[k].reshape(weights[k].shape), ms[k],
                                                            vs[k]) for k in little])):
        upd[k] = res
    grad = {k: grad[k].reshape(weights[k].shape) for k in order}
    return (loss, grad_x, *[grad[k] for k in order], *[upd[k][0] for k in order], *[upd[k][1] for k in order],
            *[upd[k][2] for k in order])
```

```python
import functools

import jax
import jax.numpy as jnp
from jax import lax
from jax.experimental import pallas as pl
from jax.experimental.pallas import tpu as pltpu

F32, BF16 = jnp.float32, jnp.bfloat16
MESH = pl.DeviceIdType.MESH
EPS = 1e-6
N_META = 16
QB = 128
PAD = QB - N_META
HALO = 16
POOL_WINDOWS = (2.0, 4.0, 8.0, 16.0)
HEAD_SCALE = 0.125
LR, B1, B2, ADAM_EPS, WD, STEP = 0.001, 0.9, 0.999, 1e-08, 0.01, 10
VMEM_LIMIT = 56 * 1024 * 1024


def _params(sem=None):
    return pltpu.CompilerParams(dimension_semantics=sem, vmem_limit_bytes=VMEM_LIMIT)


def _nt(a, b):
    return lax.dot_general(a, b, (((1,), (1,)), ((), ())), preferred_element_type=F32)


def _tn(a, b):
    return lax.dot_general(a, b, (((0,), (0,)), ((), ())), preferred_element_type=F32)


def _nn(a, b):
    return jnp.dot(a, b, preferred_element_type=F32)


def _fold8(v):
    r, c = v.shape
    return jnp.sum(v.reshape(r // 8, 8, c), axis=0)


NCH = 512


def _rows_call(name, body, tm, row_ins, consts, row_outs, accs=()):
    t = row_ins[0].shape[0]
    in_specs = [pl.BlockSpec((tm, a.shape[1]), lambda i: (i, 0)) for a in row_ins]
    for a, layer in consts:
        if layer is None:
            in_specs.append(pl.BlockSpec(a.shape, lambda i: (0, 0)))
        else:
            in_specs.append(pl.BlockSpec((None, *a.shape[1:]), lambda i, l=layer: (l, 0, 0)))
    return pl.pallas_call(
        body, name=name, grid=(t // tm,), in_specs=in_specs,
        out_specs=[pl.BlockSpec((tm, c), lambda i: (i, 0)) for c, _ in row_outs]
        + [pl.BlockSpec(s, lambda i: (0, 0)) for s in accs],
        out_shape=[jax.ShapeDtypeStruct((t, c), dt) for c, dt in row_outs]
        + [jax.ShapeDtypeStruct(s, F32) for s in accs],
        compiler_params=_params(("arbitrary",) if accs else ("parallel",)),
    )(*row_ins, *[a for a, _ in consts])


def _norm_parts(x):
    r = lax.rsqrt(jnp.mean(x * x, axis=-1, keepdims=True) + EPS)
    return r, x * r


def _norm_bwd(r, xh, dyn, g):
    w = dyn * g
    return r * (w - xh * jnp.mean(w * xh, axis=-1, keepdims=True))


def _in_proj(name, h, g, w, layer, tm, ncp):
    d, n = h.shape[1], w.shape[2]

    def body(h_ref, g_ref, w_ref, hn_ref, ucp_ref, qkv_ref):
        _, xh = _norm_parts(h_ref[...])
        hn = (xh * g_ref[...]).astype(BF16)
        hn_ref[...] = hn
        for n0 in range(0, n, NCH):
            acc = _nn(hn, w_ref[:, n0:n0 + NCH])
            if n0 < ncp:
                ucp_ref[:, n0:n0 + NCH] = acc
            else:
                qkv_ref[:, n0 - ncp:n0 - ncp + NCH] = acc.astype(BF16)

    return _rows_call(name, body, tm, [h], [(g.reshape(1, d), None), (w, layer)],
                      [(d, BF16), (ncp, F32), (n - ncp, BF16)])


def _out_proj(name, y_cp, y_at, h, w, layer, tm):
    d, k1 = h.shape[1], y_cp.shape[1]

    def body(ycp_ref, yat_ref, h_ref, w_ref, o_ref):
        for n0 in range(0, d, NCH):
            o_ref[:, n0:n0 + NCH] = (h_ref[:, n0:n0 + NCH] + _nn(ycp_ref[...], w_ref[0:k1, n0:n0 + NCH])
                                     + _nn(yat_ref[...], w_ref[k1:, n0:n0 + NCH]))

    return _rows_call(name, body, tm, [y_cp, y_at, h], [(w, layer)], [(d, F32)])[0]


def _up_proj(name, h_mid, g, w, layer, tm):
    d, n = h_mid.shape[1], w.shape[2]

    def body(h_ref, g_ref, w_ref, hn_ref, m_ref, act_ref):
        _, xh = _norm_parts(h_ref[...])
        hn = (xh * g_ref[...]).astype(BF16)
        hn_ref[...] = hn
        for n0 in range(0, n, NCH):
            acc = _nn(hn, w_ref[:, n0:n0 + NCH])
            m_ref[:, n0:n0 + NCH] = acc.astype(BF16)
            act_ref[:, n0:n0 + NCH] = jnp.square(jnp.maximum(acc, 0.0)).astype(BF16)

    return _rows_call(name, body, tm, [h_mid], [(g.reshape(1, d), None), (w, layer)],
                      [(d, BF16), (n, BF16), (n, BF16)])


def _down_proj(name, act, h_mid, w, layer, tm):
    d = h_mid.shape[1]

    def body(a_ref, h_ref, w_ref, o_ref):
        for n0 in range(0, d, NCH):
            o_ref[:, n0:n0 + NCH] = h_ref[:, n0:n0 + NCH] + _nn(a_ref[...], w_ref[:, n0:n0 + NCH])

    return _rows_call(name, body, tm, [act, h_mid], [(w, layer)], [(d, F32)])[0]


def _down_proj_dx(name, dh, m_pre, w, layer, tm):
    n = w.shape[1]

    def body(dh_ref, m_ref, w_ref, dm_ref):
        dhb = dh_ref[...].astype(BF16)
        for n0 in range(0, n, NCH):
            dm_ref[:, n0:n0 + NCH] = (_nt(dhb, w_ref[n0:n0 + NCH, :])
                                      * (2.0 * jnp.maximum(m_ref[:, n0:n0 + NCH].astype(F32), 0.0))).astype(BF16)

    return _rows_call(name, body, tm, [dh, m_pre], [(w, layer)], [(n, BF16)])[0]


def _up_proj_dx(name, dm, h_mid, dh, g, w_up, w_out, layer, tm):
    d = h_mid.shape[1]

    def body(dm_ref, h_ref, dh_ref, g_ref, wup_ref, wout_ref, dhm_ref, dy_ref, dg_ref):
        @pl.when(pl.program_id(0) == 0)
        def _():
            dg_ref[...] = jnp.zeros_like(dg_ref)
        dyn = _nt(dm_ref[...], wup_ref[...])
        r, xh = _norm_parts(h_ref[...])
        dhm = dh_ref[...] + _norm_bwd(r, xh, dyn, g_ref[...])
        dhm_ref[...] = dhm
        dg_ref[...] += _fold8(dyn * xh)
        dy_ref[...] = _nt(dhm.astype(BF16), wout_ref[...])

    return _rows_call(name, body, tm, [dm, h_mid, dh], [(g.reshape(1, d), None), (w_up, layer), (w_out, layer)],
                      [(d, F32), (w_out.shape[1], F32)], [(8, d)])


def _in_proj_dx(name, dus, h, dh_mid, g, w, layer, tm):
    d = h.shape[1]
    ns = [du.shape[1] for du in dus]
    nd = len(dus)

    def body(*refs):
        du_refs = refs[:nd]
        h_ref, dhm_ref, g_ref, w_ref, dh_ref, dg_ref = refs[nd:]

        @pl.when(pl.program_id(0) == 0)
        def _():
            dg_ref[...] = jnp.zeros_like(dg_ref)
        dyn, off = None, 0
        for du_ref, n in zip(du_refs, ns):
            part = _nt(du_ref[...], w_ref[:, off:off + n])
            dyn = part if dyn is None else dyn + part
            off += n
        r, xh = _norm_parts(h_ref[...])
        dh_ref[...] = dhm_ref[...] + _norm_bwd(r, xh, dyn, g_ref[...])
        dg_ref[...] += _fold8(dyn * xh)

    return _rows_call(name, body, tm, [*dus, h, dh_mid], [(g.reshape(1, d), None), (w, layer)], [(d, F32)],
                      [(8, d)])


def _mm_tn(name, a, b, tt, tka, tn, into, shape, layer, row_off, col_off):
    t, ka = a.shape
    n = b.shape[1]
    assert t % tt == 0 and ka % tka == 0 and n % tn == 0 and row_off % tka == 0 and col_off % tn == 0

    def body(a_ref, b_ref, *rest):
        o_ref = rest[-1]

        @pl.when(pl.program_id(2) == 0)
        def _():
            o_ref[...] = jnp.zeros_like(o_ref)
        o_ref[...] += _tn(a_ref[...].astype(BF16), b_ref[...].astype(BF16))

    in_specs = [pl.BlockSpec((tt, tka), lambda i, j, s: (s, i)), pl.BlockSpec((tt, tn), lambda i, j, s: (s, j))]
    args = [a, b]
    if into is not None:
        in_specs.append(pl.BlockSpec(memory_space=pl.ANY))
        args.append(into)
    return pl.pallas_call(
        body, name=name, grid=(ka // tka, n // tn, t // tt), in_specs=in_specs,
        out_specs=pl.BlockSpec((None, tka, tn), lambda i, j, s: (layer, row_off // tka + i, col_off // tn + j)),
        out_shape=jax.ShapeDtypeStruct(shape, F32),
        input_output_aliases={} if into is None else {2: 0},
        compiler_params=_params(("parallel", "parallel", "arbitrary")),
    )(*args)


def _loss_bwd(name, h, g, target, lp):
    t, d = h.shape
    bl = target.shape[0]
    nq = lp // QB

    def body(h_ref, g_ref, t_ref, dh_ref, ls_ref, dg_ref):
        b, j = pl.program_id(0), pl.program_id(1)

        @pl.when((b == 0) & (j == 0))
        def _():
            ls_ref[...] = jnp.zeros_like(ls_ref)
            dg_ref[...] = jnp.zeros_like(dg_ref)
        xv = h_ref[...]
        r = lax.rsqrt(jnp.mean(xv * xv, axis=-1, keepdims=True) + EPS)
        xh = xv * r
        gv = g_ref[...]
        err = jnp.where(j >= 1, xh * gv - t_ref[...], 0.0)
        ls_ref[...] += _fold8(err * err) * (0.5 / d)
        dy = err * (1.0 / d)
        w = dy * gv
        dh_ref[...] = r * (w - xh * jnp.mean(w * xh, axis=-1, keepdims=True))
        dg_ref[...] += _fold8(dy * xh)

    return pl.pallas_call(
        body, name=name, grid=(bl, nq),
        in_specs=[pl.BlockSpec((QB, d), lambda b, j: (b * nq + j, 0)), pl.BlockSpec((1, d), lambda b, j: (0, 0)),
                  pl.BlockSpec((None, QB, d), lambda b, j: (b, jnp.maximum(j - 1, 0), 0))],
        out_specs=[pl.BlockSpec((QB, d), lambda b, j: (b * nq + j, 0)), pl.BlockSpec((8, d), lambda b, j: (0, 0)),
                   pl.BlockSpec((8, d), lambda b, j: (0, 0))],
        out_shape=[jax.ShapeDtypeStruct((t, d), F32), jax.ShapeDtypeStruct((8, d), F32),
                   jax.ShapeDtypeStruct((8, d), F32)],
        compiler_params=_params(("arbitrary", "arbitrary")),
    )(h, g.reshape(1, d), target)


def _pool_select(grp, a2, a4, a8, a16):
    return jnp.where(grp == 0, a2, jnp.where(grp == 1, a4, jnp.where(grp == 2, a8, a16)))


def _trailing_sums(v):
    s2 = v + pltpu.roll(v, 1, 0)
    s4 = s2 + pltpu.roll(s2, 2, 0)
    s8 = s4 + pltpu.roll(s4, 4, 0)
    s16 = s8 + pltpu.roll(s8, 8, 0)
    return s2, s4, s8, s16


def _leading_sums(v):
    n = v.shape[0]
    s2 = v + pltpu.roll(v, n - 1, 0)
    s4 = s2 + pltpu.roll(s2, n - 2, 0)
    s8 = s4 + pltpu.roll(s4, n - 4, 0)
    s16 = s8 + pltpu.roll(s8, n - 8, 0)
    return s2, s4, s8, s16


def _convpool_fwd(name, u_cp, wconv, wbd, pscale, lp, r):
    t = u_cp.shape[0]
    cw = u_cp.shape[1] // 4
    tps, hb = lp // r, r // HALO

    def body(cb_ref, cc_ref, cx_ref, pi_ref, cch_ref, cxh_ref, pih_ref, wc_ref, wbd_ref, ps_ref, y_ref):
        i = pl.program_id(0)
        lrow = (i % tps) * r + lax.broadcasted_iota(jnp.int32, (r, 1), 0)
        valid = lrow >= PAD
        xx = jnp.concatenate([cch_ref[...] * cxh_ref[...], cc_ref[...] * cx_ref[...]], axis=0)
        conv = (wc_ref[0:1, :] * pltpu.roll(xx, 2, 0) + wc_ref[1:2, :] * pltpu.roll(xx, 1, 0)
                + wc_ref[2:3, :] * xx)
        y_ref[:, 0:cw] = (cb_ref[...] * conv[HALO:]).astype(y_ref.dtype)
        p = pi_ref[...]
        grp = lax.broadcasted_iota(jnp.int32, (1, cw), 1) // (cw // 4)
        sel = _pool_select(grp, *_trailing_sums(jnp.concatenate([pih_ref[...], p], axis=0)))[HALO:]
        cnt = jnp.maximum(jnp.minimum((lrow - (PAD - 1)).astype(F32), _pool_select(grp, *POOL_WINDOWS)), 1.0)
        pooled = jnp.where(valid, sel / cnt - p, 0.0)
        y_ref[:, cw:2 * cw] = (_nn(pooled.astype(BF16), wbd_ref[...]) * ps_ref[...]).astype(y_ref.dtype)

    def main(col):
        return pl.BlockSpec((r, cw), lambda i: (i, col))

    def prev(col):
        return pl.BlockSpec((HALO, cw), lambda i: (jnp.maximum(i * hb - 1, 0), col))

    def whole(a):
        return pl.BlockSpec(a.shape, lambda i: (0, 0))

    return pl.pallas_call(
        body, name=name, grid=(t // r,),
        in_specs=[main(0), main(1), main(2), main(3), prev(1), prev(2), prev(3), whole(wconv), whole(wbd),
                  whole(pscale)],
        out_specs=pl.BlockSpec((r, 2 * cw), lambda i: (i, 0)),
        out_shape=jax.ShapeDtypeStruct((t, 2 * cw), BF16),
        compiler_params=_params(("parallel",)),
    )(u_cp, u_cp, u_cp, u_cp, u_cp, u_cp, u_cp, wconv, wbd, pscale)


def _convpool_bwd(name, u_cp, dy, wconv, wbd, pscale, lp, r):
    t = u_cp.shape[0]
    cw = u_cp.shape[1] // 4
    tps, hb = lp // r, r // HALO
    e = r + HALO

    def body(cb_ref, cc_ref, cx_ref, pi_ref, cbn_ref, cch_ref, cxh_ref, pih_ref, dyc_ref, dyp_ref, dycn_ref,
             dypn_ref, wc_ref, wbd_ref, ps_ref, du_ref, sm_ref, dwbd_ref):
        i = pl.program_id(0)

        @pl.when(i == 0)
        def _():
            sm_ref[...] = jnp.zeros_like(sm_ref)
            dwbd_ref[...] = jnp.zeros_like(dwbd_ref)
        lrow_e = (i % tps) * r + lax.broadcasted_iota(jnp.int32, (e, 1), 0)
        valid_e = (lrow_e >= PAD) & (lrow_e < lp)
        lrow, valid = lrow_e[:r], lrow_e[:r] >= PAD
        w0, w1, w2 = wc_ref[0:1, :], wc_ref[1:2, :], wc_ref[2:3, :]
        cb, cc, cx = cb_ref[...], cc_ref[...], cx_ref[...]
        prod = cc * cx
        xx = jnp.concatenate([cch_ref[...] * cxh_ref[...], prod], axis=0)
        back1, back2 = pltpu.roll(xx, 1, 0)[HALO:], pltpu.roll(xx, 2, 0)[HALO:]
        dyc = dyc_ref[...]
        du_ref[:, 0:cw] = (dyc * (w0 * back2 + w1 * back1 + w2 * prod)).astype(du_ref.dtype)
        dconv_e = jnp.where(valid_e, jnp.concatenate([dyc * cb, dycn_ref[...] * cbn_ref[...]], axis=0), 0.0)
        dconv = dconv_e[:r]
        dprod = (w2 * dconv + w1 * pltpu.roll(dconv_e, e - 1, 0)[:r] + w0 * pltpu.roll(dconv_e, e - 2, 0)[:r])
        du_ref[:, cw:2 * cw] = (dprod * cx).astype(du_ref.dtype)
        du_ref[:, 2 * cw:3 * cw] = (dprod * cc).astype(du_ref.dtype)
        sm_ref[0:8, :] += _fold8(dconv * back2)
        sm_ref[8:16, :] += _fold8(dconv * back1)
        sm_ref[16:24, :] += _fold8(dconv * prod)
        p = pi_ref[...]
        grp = lax.broadcasted_iota(jnp.int32, (1, cw), 1) // (cw // 4)
        win = _pool_select(grp, *POOL_WINDOWS)
        sel = _pool_select(grp, *_trailing_sums(jnp.concatenate([pih_ref[...], p], axis=0)))[HALO:]
        cnt_e = jnp.maximum(jnp.minimum((lrow_e - (PAD - 1)).astype(F32), win), 1.0)
        pooled = jnp.where(valid, sel / cnt_e[:r] - p, 0.0).astype(BF16)
        dyp = dyp_ref[...]
        sm_ref[24:32, :] += _fold8(dyp * _nn(pooled, wbd_ref[...]))
        dpre_e = (jnp.concatenate([dyp, dypn_ref[...]], axis=0) * ps_ref[...]).astype(BF16)
        dwbd_ref[...] += _tn(pooled, dpre_e[:r])
        dpooled_e = jnp.where(valid_e, _nt(dpre_e, wbd_ref[...]), 0.0)
        ahead = _pool_select(grp, *_leading_sums(dpooled_e / cnt_e))[:r]
        du_ref[:, 3 * cw:4 * cw] = (ahead - dpooled_e[:r]).astype(du_ref.dtype)

    last_halo = t // HALO - 1

    def main(col):
        return pl.BlockSpec((r, cw), lambda i: (i, col))

    def prev(col):
        return pl.BlockSpec((HALO, cw), lambda i: (jnp.maximum(i * hb - 1, 0), col))

    def nxt(col):
        return pl.BlockSpec((HALO, cw), lambda i: (jnp.minimum((i + 1) * hb, last_halo), col))

    def whole(a):
        return pl.BlockSpec(a.shape, lambda i: (0, 0))

    return pl.pallas_call(
        body, name=name, grid=(t // r,),
        in_specs=[main(0), main(1), main(2), main(3), nxt(0), prev(1), prev(2), prev(3), main(0), main(1), nxt(0),
                  nxt(1), whole(wconv), whole(wbd), whole(pscale)],
        out_specs=[pl.BlockSpec((r, 4 * cw), lambda i: (i, 0)), pl.BlockSpec((32, cw), lambda i: (0, 0)),
                   pl.BlockSpec((cw, cw), lambda i: (0, 0))],
        out_shape=[jax.ShapeDtypeStruct((t, 4 * cw), BF16), jax.ShapeDtypeStruct((32, cw), F32),
                   jax.ShapeDtypeStruct((cw, cw), F32)],
        compiler_params=_params(("arbitrary",)),
    )(u_cp, u_cp, u_cp, u_cp, u_cp, u_cp, u_cp, u_cp, dy, dy, dy, dy, wconv, wbd, pscale)


KW = 2 * QB
HP = 4
DECAY = 64.0


def _cumsum_matrix(before):
    r = lax.broadcasted_iota(jnp.int32, (KW, KW), 0)
    c = lax.broadcasted_iota(jnp.int32, (KW, KW), 1)
    return ((r < c) if before else (r > c)).astype(BF16)


def _running(v, mat):
    m = v.shape[0]
    hi = v.astype(BF16)
    ext = _nn(jnp.concatenate([hi, (v - hi.astype(F32)).astype(BF16)], axis=0), mat)
    return ext[:m] + ext[m:]


def _log_sigmoid(z):
    neg_abs = lax.bitcast_convert_type(lax.bitcast_convert_type(z, jnp.int32) | jnp.int32(-2 ** 31), F32)
    return jnp.minimum(z, 0.0) - jnp.log(1.0 + jnp.exp(neg_abs))


def _stack_heads(v, head0):
    zero = jnp.zeros_like(v)
    return jnp.concatenate([jnp.where(head0, v, zero), jnp.where(head0, zero, v)], axis=0)


def _lanes(hp):
    return slice(hp * QB, (hp + 1) * QB)


def _attn_fwd(name, qkv, bl, lp, rider=None):
    t = qkv.shape[0]
    nq, nblk = lp // QB, qkv.shape[1] // (3 * HP * QB)
    assert nblk == 1
    ride = _Ride(rider, 3, 3, bl * nblk * nq)

    def body(*refs):
        q_ref, k_ref, v_ref, o_ref, lt_ref, g0_ref = ride.own(refs)
        qi = pl.program_id(2)
        step = (pl.program_id(0) * nblk + pl.program_id(1)) * nq + qi
        ride.before(refs, step)
        head0 = lax.broadcasted_iota(jnp.int32, (QB, QB), 1) < QB // 2
        q2 = [_stack_heads(q_ref[:, _lanes(hp)] * jnp.asarray(HEAD_SCALE, BF16), head0) for hp in range(HP)]
        later = _cumsum_matrix(before=False)
        q_pos = qi * QB + (lax.broadcasted_iota(jnp.int32, (2 * QB, KW), 0) & (QB - 1))
        col = lax.broadcasted_iota(jnp.int32, (2 * QB, KW), 1)
        ng = qi // 2 + 1

        def group(g, carry, masked):
            start = pl.multiple_of(jnp.minimum(g * KW, lp - KW) if masked else g * KW, QB)
            if masked:
                k_pos = start + col
                valid = (k_pos < q_pos) & (k_pos >= PAD) & (k_pos >= g * KW)
            z = [_nt(q2[hp], k_ref[pl.ds(start, KW), _lanes(hp)]) for hp in range(HP)]
            logp, after, rs = [], [], []
            for hp in range(HP):
                lp_ = _log_sigmoid(z[hp])
                lk = lp_ - z[hp]
                if masked:
                    lk = jnp.where(valid, lk, 0.0)
                logp.append(lp_)
                rs.append(jnp.sum(lk, axis=1, keepdims=True))
                after.append(_running(lk, later))
            out = []
            for hp in range(HP):
                run, acc = carry[2 * hp], carry[2 * hp + 1]
                a = jnp.exp(logp[hp] + after[hp] + run)
                if masked:
                    a = jnp.where(valid, a, 0.0)
                out += [run + rs[hp], acc + _nn(a.astype(BF16), v_ref[pl.ds(start, KW), _lanes(hp)])]
            return tuple(out)

        def alive(carry):
            most = carry[0]
            for hp in range(1, HP):
                most = jnp.maximum(most, carry[2 * hp])
            return jnp.max(most) > -DECAY

        carry = (jnp.zeros((2 * QB, 1), F32), jnp.zeros((2 * QB, QB), F32)) * HP
        carry = group(ng - 1, carry, True)
        g, *carry = lax.while_loop(lambda st: (st[0] >= 1) & alive(st[1:]),
                                   lambda st: (st[0] - 1, *group(st[0], tuple(st[1:]), False)), (ng - 2, *carry))
        oldest = (g == 0) & (ng >= 2) & alive(carry)
        carry = lax.fori_loop(0, oldest.astype(jnp.int32), lambda i, c: group(0, c, True), tuple(carry))
        g0_ref[pl.program_id(0), qi] = jnp.where(oldest, 0, g + 1).astype(F32)
        for hp in range(HP):
            run, acc = carry[2 * hp], carry[2 * hp + 1]
            o_ref[:, _lanes(hp)] = jnp.where(head0, acc[:QB], acc[QB:]).astype(o_ref.dtype)
            lt_ref[:, _lanes(hp)] = jnp.where(head0, run[:QB], run[QB:])
        ride.after(refs, step)

    wb = HP * QB
    blk = pl.BlockSpec((QB, wb), lambda b, p, i: (b * nq + i, p))
    return ride.call(
        body, name, (bl, nblk, nq),
        [blk, pl.BlockSpec((lp, wb), lambda b, p, i: (b, nblk + p)),
         pl.BlockSpec((lp, wb), lambda b, p, i: (b, 2 * nblk + p))], [qkv, qkv, qkv],
        [blk, blk, pl.BlockSpec(memory_space=pltpu.SMEM)],
        [jax.ShapeDtypeStruct((t, nblk * wb), BF16), jax.ShapeDtypeStruct((t, nblk * wb), F32),
         jax.ShapeDtypeStruct((bl, nq), F32)], [])


def _attn_bwd(name, qkv, lt, g0, dy, bl, lp, rider=None):
    t = qkv.shape[0]
    nq, nblk = lp // QB, qkv.shape[1] // (3 * HP * QB)
    ride = _Ride(rider, 6, 3, bl * nblk * nq)

    def body(*refs):
        q_ref, k_ref, v_ref, lt_ref, do_ref, g0_ref, dq_ref, dk_ref, dv_ref, dk_acc, dv_acc = ride.own(refs)
        qi = pl.program_id(2)
        step = (pl.program_id(0) * nblk + pl.program_id(1)) * nq + qi
        ride.before(refs, step)

        @pl.when(qi == 0)
        def _():
            dk_acc[...] = jnp.zeros_like(dk_acc)
            dv_acc[...] = jnp.zeros_like(dv_acc)
        lane = lax.broadcasted_iota(jnp.int32, (QB, QB), 1)
        head0 = lane < QB // 2
        q2, do2, total = [], [], []
        for hp in range(HP):
            q2.append(_stack_heads(q_ref[:, _lanes(hp)] * jnp.asarray(HEAD_SCALE, BF16), head0))
            do2.append(_stack_heads(do_ref[:, _lanes(hp)].astype(BF16), head0))
            ltv = lt_ref[:, _lanes(hp)]
            total.append(jnp.concatenate(
                [jnp.sum(jnp.where(lane == 0, ltv, 0.0), axis=1, keepdims=True),
                 jnp.sum(jnp.where(lane == QB // 2, ltv, 0.0), axis=1, keepdims=True)], axis=0))
        later, earlier = _cumsum_matrix(before=False), _cumsum_matrix(before=True)
        q_pos = qi * QB + (lax.broadcasted_iota(jnp.int32, (2 * QB, KW), 0) & (QB - 1))
        col = lax.broadcasted_iota(jnp.int32, (2 * QB, KW), 1)
        ng = qi // 2 + 1

        def group(g, carry, masked):
            start = pl.multiple_of(jnp.minimum(g * KW, lp - KW) if masked else g * KW, QB)
            if masked:
                k_pos = start + col
                valid = (k_pos < q_pos) & (k_pos >= PAD) & (k_pos >= g * KW)
            hps = range(HP)
            kg = [k_ref[pl.ds(start, KW), _lanes(hp)] for hp in hps]
            z = [_nt(q2[hp], kg[hp]) for hp in hps]
            da = [_nt(do2[hp], v_ref[pl.ds(start, KW), _lanes(hp)]) for hp in hps]
            logp, sig, after, rs = [], [], [], []
            for hp in hps:
                lp_ = _log_sigmoid(z[hp])
                lk = lp_ - z[hp]
                if masked:
                    lk = jnp.where(valid, lk, 0.0)
                logp.append(lp_)
                sig.append(jnp.exp(lp_))
                rs.append(jnp.sum(lk, axis=1, keepdims=True))
                after.append(_running(lk, later))
            a, gg, before = [], [], []
            for hp in hps:
                a_ = jnp.exp(logp[hp] + after[hp] + (total[hp] - carry[3 * hp] - rs[hp]))
                if masked:
                    a_ = jnp.where(valid, a_, 0.0)
                a.append(a_.astype(BF16))
                gg.append(a_ * da[hp])
                before.append(_nn(gg[hp].astype(BF16), earlier))
            out = []
            for hp in hps:
                seen, gsum, dq = carry[3 * hp], carry[3 * hp + 1], carry[3 * hp + 2]
                dz = gg[hp] - (gg[hp] + before[hp] + gsum) * sig[hp]
                if masked:
                    dz = jnp.where(valid, dz, 0.0)
                dz = dz.astype(BF16)
                dk_acc[pl.ds(start, KW), _lanes(hp)] += _tn(dz, q2[hp])
                dv_acc[pl.ds(start, KW), _lanes(hp)] += _tn(a[hp], do2[hp])
                out += [seen + rs[hp], gsum + jnp.sum(gg[hp], axis=1, keepdims=True), dq + _nn(dz, kg[hp])]
            return tuple(out)

        col0 = jnp.zeros((2 * QB, 1), F32)
        carry = (col0, col0, jnp.zeros((2 * QB, QB), F32)) * HP
        first = g0_ref[pl.program_id(0), qi].astype(jnp.int32)
        carry = lax.fori_loop(0, (first == 0).astype(jnp.int32), lambda i, c: group(0, c, True), carry)
        carry = lax.fori_loop(jnp.maximum(first, 1), ng - 1, lambda g, c: group(g, c, False), carry)
        carry = lax.fori_loop(0, jnp.minimum(ng - 1, 1), lambda i, c: group(ng - 1, c, True), carry)
        for hp in range(HP):
            dq = carry[3 * hp + 2]
            dq_ref[:, _lanes(hp)] = (jnp.where(head0, dq[:QB], dq[QB:]) * HEAD_SCALE).astype(dq_ref.dtype)

        @pl.when(qi == nq - 1)
        def _():
            dk_ref[...] = dk_acc[...].astype(dk_ref.dtype)
            dv_ref[...] = dv_acc[...].astype(dv_ref.dtype)
        ride.after(refs, step)

    wb = HP * QB
    blk = pl.BlockSpec((QB, wb), lambda b, p, i: (b * nq + i, p))
    seq = pl.BlockSpec((lp, wb), lambda b, p, i: (b, p))
    out = jax.ShapeDtypeStruct((t, nblk * wb), BF16)
    return ride.call(
        body, name, (bl, nblk, nq),
        [blk, pl.BlockSpec((lp, wb), lambda b, p, i: (b, nblk + p)),
         pl.BlockSpec((lp, wb), lambda b, p, i: (b, 2 * nblk + p)), blk,
         pl.BlockSpec((QB, wb), lambda b, p, i: (b * nq + i, nblk + p)), pl.BlockSpec(memory_space=pltpu.SMEM)],
        [qkv, qkv, qkv, lt, dy, g0],
        [blk, seq, seq], [out, out, out], [pltpu.VMEM((lp, wb), F32), pltpu.VMEM((lp, wb), F32)])


def _place():
    return lax.axis_index("x"), lax.axis_index("y"), lax.axis_index("c")


def _peers(chip):
    kx, ky = chip // 2, chip % 2
    return ((1 - kx, ky), (kx, 1 - ky), (1 - kx, 1 - ky))


def _hbm_specs(n):
    return [pl.BlockSpec(memory_space=pl.ANY) for _ in range(n)]


def _remote(src, dst, send, recv, k, to):
    return pltpu.make_async_remote_copy(src, dst, send.at[k], recv.at[k], device_id=to, device_id_type=MESH)


class _Rider:
    def __init__(self, ins, out_shapes, aliases, nsem, first, mid=None, last=None):
        self.ins, self.out_shapes, self.aliases, self.nsem = list(ins), list(out_shapes), dict(aliases), nsem
        self.first, self.mid, self.last = first, mid, last


def _by_chip(fn):
    def run(ins, outs, send, recv):
        x, y, c = _place()
        for me in range(4):
            pl.when(2 * x + y == me)(functools.partial(fn, ins, outs, send, recv, me, c, (x, y, 1 - c)))
    return run


def _run_rider(name, rider):
    ni, no = len(rider.ins), len(rider.out_shapes)

    def body(*refs):
        args = (refs[:ni], refs[ni:ni + no], refs[ni + no], refs[ni + no + 1])
        for hook in (rider.first, rider.mid, rider.last):
            if hook is not None:
                hook(*args)

    return pl.pallas_call(
        body, name=name, in_specs=_hbm_specs(ni), out_specs=_hbm_specs(no), out_shape=rider.out_shapes,
        input_output_aliases=rider.aliases,
        scratch_shapes=[pltpu.SemaphoreType.DMA((rider.nsem,)), pltpu.SemaphoreType.DMA((rider.nsem,))],
        compiler_params=pltpu.CompilerParams(has_side_effects=True),
    )(*rider.ins)


class _Ride:
    def __init__(self, rider, n_in, n_out, steps):
        self.rider, self.n_in, self.n_out, self.steps = rider, n_in, n_out, steps
        self.ri = len(rider.ins) if rider else 0
        self.ro = len(rider.out_shapes) if rider else 0

    def own(self, refs):
        refs = list(refs)
        a, b = self.n_in, self.n_in + self.ri + self.n_out
        tail = refs[b + self.ro:len(refs) - 2] if self.rider else refs[b + self.ro:]
        return refs[:a] + refs[a + self.ri:b] + tail

    def _args(self, refs):
        a, b = self.n_in, self.n_in + self.ri + self.n_out
        return refs[a:a + self.ri], refs[b:b + self.ro], refs[-2], refs[-1]

    def before(self, refs, step):
        if self.rider is None:
            return
        pl.when(step == 0)(functools.partial(self.rider.first, *self._args(refs)))
        if self.rider.mid is not None:
            pl.when(step == (3 * self.steps) // 4)(functools.partial(self.rider.mid, *self._args(refs)))

    def after(self, refs, step):
        if self.rider is not None and self.rider.last is not None:
            pl.when(step == self.steps - 1)(functools.partial(self.rider.last, *self._args(refs)))

    def call(self, body, name, grid, in_specs, args, out_specs, out_shape, scratch):
        r = self.rider
        if r is None:
            return pl.pallas_call(body, name=name, grid=grid, in_specs=in_specs, out_specs=out_specs,
                                  out_shape=out_shape, scratch_shapes=scratch,
                                  compiler_params=_params(("parallel", "parallel", "arbitrary")))(*args)
        return pl.pallas_call(
            body, name=name, grid=grid, in_specs=in_specs + _hbm_specs(self.ri),
            out_specs=out_specs + _hbm_specs(self.ro), out_shape=out_shape + r.out_shapes,
            input_output_aliases={self.n_in + i: self.n_out + o for i, o in r.aliases.items()},
            scratch_shapes=scratch + [pltpu.SemaphoreType.DMA((r.nsem,)), pltpu.SemaphoreType.DMA((r.nsem,))],
            compiler_params=pltpu.CompilerParams(dimension_semantics=("arbitrary",) * len(grid),
                                                 vmem_limit_bytes=VMEM_LIMIT, has_side_effects=True),
        )(*args, *r.ins)


def _core_view(a, axis):
    l, r, c = a.shape
    return a.reshape(l, 4, 2, r // 8, c) if axis == 0 else a.reshape(l, 2, r // 2, c)


def _shard_view(a):
    l, r, c = a.shape
    return a.reshape(l, 2, r // 2, c)


def _piece(ref, axis, layer, chip, core):
    if axis == 0:
        return ref.at[layer, chip, core]
    cs = ref.shape[-1] // 4
    return ref.at[layer, core, :, pl.ds(chip * cs, cs)]


def _place_shard(name, w, axis, kidx, tr):
    _, r, cdim = w.shape
    shp = [2, r, cdim]
    shp[1 + axis] *= 4
    nb = r // tr

    def body(k_ref, w_ref, o_ref):
        o_ref[...] = w_ref[...].astype(o_ref.dtype)

    if axis == 0:
        out_spec = pl.BlockSpec((None, tr, cdim), lambda l, i, k_ref: (l, k_ref[0] * nb + i, 0))
    else:
        out_spec = pl.BlockSpec((None, tr, cdim), lambda l, i, k_ref: (l, i, k_ref[0]))
    return pl.pallas_call(
        body, name=name,
        grid_spec=pltpu.PrefetchScalarGridSpec(
            num_scalar_prefetch=1, grid=(2, nb),
            in_specs=[pl.BlockSpec((None, tr, cdim), lambda l, i, k_ref: (l, i, 0))], out_specs=out_spec),
        out_shape=jax.ShapeDtypeStruct(tuple(shp), BF16),
        compiler_params=_params(("arbitrary", "arbitrary")),
    )(kidx, w)


def _gather_rider(views, axes, items):
    n = len(items)

    def first(ins, outs, send, recv, me, c, sib):
        for i, (w, l) in enumerate(items):
            for j, (px, py) in enumerate(_peers(me)):
                _remote(_piece(ins[w], axes[w], l, me, c), _piece(outs[w], axes[w], l, me, c), send, recv,
                        3 * i + j, (px, py, c)).start()

    def mid(ins, outs, send, recv, me, c, sib):
        for i, (w, l) in enumerate(items):
            for j, (px, py) in enumerate(_peers(me)):
                got = _piece(outs[w], axes[w], l, 2 * px + py, c)
                _remote(got, got, send, recv, 3 * i + j, (px, py, c)).wait_recv()
                _remote(got, got, send, recv, 3 * (n + i) + j, sib).start()

    def last(ins, outs, send, recv, me, c, sib):
        for i, (w, l) in enumerate(items):
            for j, (px, py) in enumerate(_peers(me)):
                mine, got = _piece(outs[w], axes[w], l, me, c), _piece(outs[w], axes[w], l, 2 * px + py, c)
                theirs = _piece(outs[w], axes[w], l, 2 * px + py, 1 - c)
                _remote(theirs, theirs, send, recv, 3 * (n + i) + j, sib).wait_recv()
                _remote(mine, mine, send, recv, 3 * i + j, (px, py, c)).wait_send()
                _remote(got, got, send, recv, 3 * (n + i) + j, sib).wait_send()

    return _Rider(views, [jax.ShapeDtypeStruct(v.shape, v.dtype) for v in views], {w: w for w in range(len(views))},
                  6 * n, _by_chip(first), _by_chip(mid), _by_chip(last))


def _swap_rider(views, axes, items):
    nv = len(views)

    def part(ref, w, l, core):
        return ref.at[l, :, core] if axes[w] == 0 else ref.at[l, core]

    def copies(ins, outs, send, recv):
        x, y, c = _place()
        return [_remote(part(ins[w], w, l, 1 - c), outs[nv + i], send, recv, i, (x, y, 1 - c))
                for i, (w, l) in enumerate(items)]

    def first(ins, outs, send, recv):
        for cp in copies(ins, outs, send, recv):
            cp.start()

    def last(ins, outs, send, recv):
        for cp in copies(ins, outs, send, recv):
            cp.wait()

    got = [jax.ShapeDtypeStruct(views[w].shape[1:2] + views[w].shape[3:] if axes[w] == 0 else views[w].shape[2:],
                                views[w].dtype) for w, _ in items]
    return _Rider(views, [jax.ShapeDtypeStruct(v.shape, v.dtype) for v in views] + got,
                  {w: w for w in range(nv)}, len(items), first, None, last)


def _add_core(name, view, got, axis, layer, cidx, tr):
    def body(c_ref, g_ref, r_ref, o_ref):
        o_ref[...] = (g_ref[...] + r_ref[...]).astype(o_ref.dtype)

    if axis == 0:
        _, nchip, _, pr, cdim = view.shape
        grid = (nchip, pr // tr)
        specs = [pl.BlockSpec((None, None, None, tr, cdim), lambda k, i, c_ref: (layer, k, c_ref[0], i, 0)),
                 pl.BlockSpec((None, tr, cdim), lambda k, i, c_ref: (k, i, 0))]
        out_spec = pl.BlockSpec((None, tr, cdim), lambda k, i, c_ref: (k, i, 0))
    else:
        _, _, pr, cdim = view.shape
        grid = (pr // tr,)
        specs = [pl.BlockSpec((None, None, tr, cdim), lambda i, c_ref: (layer, c_ref[0], i, 0)),
                 pl.BlockSpec((tr, cdim), lambda i, c_ref: (i, 0))]
        out_spec = pl.BlockSpec((tr, cdim), lambda i, c_ref: (i, 0))
    return pl.pallas_call(
        body, name=name,
        grid_spec=pltpu.PrefetchScalarGridSpec(num_scalar_prefetch=1, grid=grid, in_specs=specs,
                                               out_specs=out_spec),
        out_shape=jax.ShapeDtypeStruct(got.shape, BF16),
        compiler_params=_params(("arbitrary",) * len(grid)),
    )(cidx, view, got)


def _scatter_rider(sums, axes):
    def part(ref, i, chip):
        if axes[i] == 0:
            return ref.at[chip]
        cs = ref.shape[-1] // 4
        return ref.at[:, pl.ds(chip * cs, cs)]

    def copies(ins, outs, send, recv, me, c, sib):
        return [_remote(part(ins[i], i, 2 * px + py), outs[i].at[j], send, recv, 3 * i + j, (px, py, c))
                for i in range(len(sums)) for j, (px, py) in enumerate(_peers(me))]

    def first(*args):
        for cp in copies(*args):
            cp.start()

    def last(*args):
        for cp in copies(*args):
            cp.wait()

    shapes = [jax.ShapeDtypeStruct((3,) + (s.shape[1:] if ax == 0 else (s.shape[0], s.shape[1] // 4)), s.dtype)
              for s, ax in zip(sums, axes)]
    return _Rider(sums, shapes, {}, 3 * len(sums), _by_chip(first), None, _by_chip(last))


def _add_chips(name, own, got, axis, layer, kc_idx, tr, into, shard_shape):
    _, pr, pc = got.shape

    def body(k_ref, o_ref, g_ref, *rest):
        rest[-1][...] = (o_ref[...].astype(F32) + g_ref[0].astype(F32) + g_ref[1].astype(F32)
                         + g_ref[2].astype(F32))

    if axis == 0:
        own_spec = pl.BlockSpec((None, tr, pc), lambda i, k_ref: (k_ref[0], i, 0))
    else:
        own_spec = pl.BlockSpec((tr, pc), lambda i, k_ref: (i, k_ref[0]))
    specs = [own_spec, pl.BlockSpec((3, tr, pc), lambda i, k_ref: (0, i, 0))]
    args = [kc_idx, own, got]
    if into is not None:
        specs.append(pl.BlockSpec(memory_space=pl.ANY))
        args.append(into)
    return pl.pallas_call(
        body, name=name,
        grid_spec=pltpu.PrefetchScalarGridSpec(
            num_scalar_prefetch=1, grid=(pr // tr,), in_specs=specs,
            out_specs=pl.BlockSpec((None, None, tr, pc), lambda i, k_ref: (layer, k_ref[1], i, 0))),
        out_shape=jax.ShapeDtypeStruct(shard_shape, F32),
        input_output_aliases={} if into is None else {3: 0},
        compiler_params=_params(("arbitrary",)),
    )(*args)


def _join_rider(parts):
    def first(ins, outs, send, recv):
        x, y, c = _place()
        for w in range(len(parts)):
            _remote(ins[w].at[:, c], outs[w].at[:, c], send, recv, w, (x, y, 1 - c)).start()

    def last(ins, outs, send, recv):
        x, y, c = _place()
        for w in range(len(parts)):
            _remote(ins[w].at[:, c], outs[w].at[:, c], send, recv, w, (x, y, 1 - c)).wait_send()
            _remote(ins[w].at[:, c], outs[w].at[:, 1 - c], send, recv, w, (x, y, 1 - c)).wait_recv()

    return _Rider(parts, [jax.ShapeDtypeStruct(p.shape, p.dtype) for p in parts],
                  {w: w for w in range(len(parts))}, len(parts), first, None, last)


def _all_reduce_small(name, pack, fold_rows, groups):
    nr, d = pack.shape

    def body(in_ref, sum_ref, meta_ref, slots, send, recv):
        x, y, c = _place()
        me = 4 * x + 2 * y + c
        slots[me] = in_ref[...]
        cps = []
        for r in range(1, 8):
            rx, ry, rc = r // 4, (r // 2) % 2, r % 2
            peer = (x + rx - 2 * x * rx, y + ry - 2 * y * ry, c + rc - 2 * c * rc)
            cp = pltpu.make_async_remote_copy(in_ref, slots.at[me], send.at[r - 1], recv.at[r - 1],
                                              device_id=peer, device_id_type=MESH)
            cp.start()
            cps.append(cp)
        for cp in cps:
            cp.wait()
        acc = slots[0]
        for dev in range(1, 8):
            acc = acc + slots[dev]
        sum_ref[...] = acc
        fold = acc[0:fold_rows]
        for grp in range(1, groups):
            fold = fold + acc[grp * fold_rows:(grp + 1) * fold_rows]
        meta_ref[...] = fold

    vmem = pl.BlockSpec(memory_space=pltpu.VMEM)
    return pl.pallas_call(
        body, name=name, in_specs=[vmem], out_specs=[vmem, vmem],
        out_shape=[jax.ShapeDtypeStruct((nr, d), F32), jax.ShapeDtypeStruct((fold_rows, d), F32)],
        scratch_shapes=[pltpu.VMEM((8, nr, d), F32), pltpu.SemaphoreType.DMA((7,)), pltpu.SemaphoreType.DMA((7,))],
        compiler_params=pltpu.CompilerParams(has_side_effects=True, vmem_limit_bytes=VMEM_LIMIT),
    )(pack)


def _adamw_math(w, g, m, v):
    m = B1 * m + (1.0 - B1) * g
    v = B2 * v + (1.0 - B2) * (g * g)
    m_hat = m / (1.0 - B1 ** STEP)
    v_hat = v / (1.0 - B2 ** STEP)
    return -LR * (m_hat / (jnp.sqrt(v_hat) + ADAM_EPS) + WD * w), m, v


def _adamw(name, w, g, m, v, tr):
    shape = w.shape
    flat = [a.reshape(-1, shape[-1]) for a in (w, g, m, v)]
    r, cdim = flat[0].shape

    def body(w_ref, g_ref, m_ref, v_ref, d_ref, nm_ref, nv_ref):
        d_ref[...], nm_ref[...], nv_ref[...] = _adamw_math(w_ref[...], g_ref[...], m_ref[...], v_ref[...])

    spec = pl.BlockSpec((tr, cdim), lambda i: (i, 0))
    outs = pl.pallas_call(
        body, name=name, grid=(r // tr,), in_specs=[spec] * 4, out_specs=[spec] * 3,
        out_shape=[jax.ShapeDtypeStruct((r, cdim), F32)] * 3,
        compiler_params=_params(("parallel",)),
    )(*flat)
    return [o.reshape(shape) for o in outs]


def _adamw_small(name, groups):
    n = len(groups)
    shapes = [grp[0].shape for grp in groups]
    flat = [a.reshape(-1, a.shape[-1]) for grp in groups for a in grp]

    def body(*refs):
        ins, outs = refs[:4 * n], refs[4 * n:]
        for i in range(n):
            w_ref, g_ref, m_ref, v_ref = ins[4 * i:4 * i + 4]
            outs[3 * i][...], outs[3 * i + 1][...], outs[3 * i + 2][...] = _adamw_math(
                w_ref[...], g_ref[...], m_ref[...], v_ref[...])

    vmem = pl.BlockSpec(memory_space=pltpu.VMEM)
    out_shape = [jax.ShapeDtypeStruct(flat[4 * i].shape, F32) for i in range(n) for _ in range(3)]
    outs = pl.pallas_call(body, name=name, in_specs=[vmem] * (4 * n), out_specs=[vmem] * (3 * n),
                          out_shape=out_shape)(*flat)
    return [[outs[3 * i + j].reshape(shapes[i]) for j in range(3)] for i in range(n)]


def _block_diag(w_grp):
    g, pg, _ = w_grp.shape
    eye = jnp.eye(g, dtype=w_grp.dtype)
    return (eye[:, None, :, None] * w_grp[:, :, None, :]).reshape(g * pg, g * pg)


def _diag_blocks(m, g):
    pg = m.shape[0] // g
    return jnp.stack([m[i * pg:(i + 1) * pg, i * pg:(i + 1) * pg] for i in range(g)])


BIG = ("w_in", "w_out", "w_up", "w_down")
AXES = (1, 0, 1, 0)
W_IN, W_OUT, W_UP, W_DOWN = range(4)


def kernel(x, meta_tokens, g_mix, w_in, w_conv, w_pool, pool_scale, w_out, g_mlp, w_up, w_down, g_final, loss_target, m_meta_tokens, m_g_mix, m_w_in, m_w_conv, m_w_pool, m_pool_scale, m_w_out, m_g_mlp, m_w_up, m_w_down, m_g_final, v_meta_tokens, v_g_mix, v_w_in, v_w_conv, v_w_pool, v_pool_scale, v_w_out, v_g_mlp, v_w_up, v_w_down, v_g_final):
    bl, s, d = x.shape
    depth = g_mix.shape[0]
    assert depth == 2
    lp = PAD + N_META + s
    t = bl * lp
    tt = lp // 2
    tm = lp // 4
    cs = w_conv.shape[2]
    cw = 4 * cs
    ngrp = w_pool.shape[1]
    xi, yi, ci = _place()
    chip = (2 * xi + yi).astype(jnp.int32)
    cidx, kidx = ci.astype(jnp.int32).reshape(1), chip.reshape(1)
    kc_idx = jnp.stack([chip, ci.astype(jnp.int32)])
    shards = (w_in, w_out, w_up, w_down)

    views = [_core_view(_place_shard(f"place_{BIG[w]}", shards[w], AXES[w], kidx, 256), AXES[w]) for w in range(4)]
    views[W_IN] = _run_rider("gather_first", _gather_rider(views[:1], AXES[:1], [(W_IN, 0)]))[0]
    rest = [(w, l) for l in range(depth) for w in range(4) if (w, l) != (W_IN, 0)]

    def whole(w):
        return views[w].reshape(depth, -1, views[w].shape[-1])

    placed = jnp.zeros((32, d), F32)
    placed = lax.dynamic_update_slice(placed, meta_tokens, (0, chip * meta_tokens.shape[1]))
    placed = lax.dynamic_update_slice(placed, w_conv.reshape(-1, cs), (N_META, chip * cs))
    placed = jnp.where(ci == 0, placed, 0.0)
    small, _ = _all_reduce_small("gather_small", placed, 8, 1)
    meta_full = small[:N_META]
    conv_full = small[N_META:N_META + depth * 3, :cw].reshape(depth, 3, cw)

    h = jnp.concatenate([jnp.zeros((bl, PAD, d), F32), jnp.broadcast_to(meta_full[None], (bl, N_META, d)), x],
                        axis=1).reshape(t, d)
    wbd = [_block_diag(w_pool[i]).astype(BF16) for i in range(depth)]
    saved = []
    for i in range(depth):
        hn, u_cp, qkv = _in_proj(f"in_proj{i}", h, g_mix[i], whole(W_IN), i, tm, 4 * cw)
        y_cp = _convpool_fwd(f"convpool{i}", u_cp, conv_full[i], wbd[i], pool_scale[i:i + 1], lp, tm)
        if i == 0:
            y_at, lt, g0, *views = _attn_fwd(f"attn{i}", qkv, bl, lp, _gather_rider(views, AXES, rest))
        else:
            y_at, lt, g0 = _attn_fwd(f"attn{i}", qkv, bl, lp)
        h_mid = _out_proj(f"out_proj{i}", y_cp, y_at, h, whole(W_OUT), i, tm)
        hn2, m_pre, act = _up_proj(f"up_proj{i}", h_mid, g_mlp[i], whole(W_UP), i, tm)
        h_next = _down_proj(f"down_proj{i}", act, h_mid, whole(W_DOWN), i, tm)
        saved.append((h, hn, u_cp, qkv, y_cp, y_at, (lt, g0), h_mid, hn2, m_pre, act))
        h = h_next

    dh, loss8, dgf8 = _loss_bwd("loss", h, g_final, loss_target, lp)
    loss = lax.psum(jnp.sum(loss8), ("x", "y", "c"))
    per_layer = {k: [None] * depth for k in ("g_mix", "w_conv", "w_pool", "pool_scale", "g_mlp")}

    gw = [None] * 4
    sums, arrived = {}, {}

    def dw(name, a, b, w, layer, tka, tn, row_off=0, col_off=0):
        shape = whole(w).shape
        into = None if gw[w] is None else gw[w].reshape(shape)
        gw[w] = _core_view(_mm_tn(name, a, b, tt, tka, tn, into, shape, layer, row_off, col_off), AXES[w])

    def swap_rider(ws):
        return _swap_rider([gw[w] for w, _ in ws], [AXES[w] for w, _ in ws],
                           [(j, layer) for j, (_, layer) in enumerate(ws)])

    def swapped(ws, outs):
        for j, (w, layer) in enumerate(ws):
            gw[w] = outs[j]
            sums[w, layer] = _add_core(f"chip_sum_{BIG[w]}{layer}", gw[w], outs[len(ws) + j], AXES[w], layer, cidx,
                                       128)

    def scatter_rider(items):
        return _scatter_rider([sums[it] for it in items], [AXES[w] for w, _ in items])

    def bwd_mlp(i, dh):
        _, _, _, _, y_cp, y_at, _, h_mid, hn2, m_pre, act = saved[i]
        dm = _down_proj_dx(f"down_proj_dx{i}", dh, m_pre, whole(W_DOWN), i, tm)
        dw(f"down_proj_dw{i}", act, dh, W_DOWN, i, 1024, 1024)
        dw(f"up_proj_dw{i}", hn2, dm, W_UP, i, 1024, 1024)
        dh_mid, dy, dg8 = _up_proj_dx(f"up_proj_dx{i}", dm, h_mid, dh, g_mlp[i], whole(W_UP), whole(W_OUT), i, tm)
        per_layer["g_mlp"][i] = dg8.sum(0)
        dw(f"out_proj_dw_cp{i}", y_cp, dh_mid, W_OUT, i, 512, 1024)
        dw(f"out_proj_dw_at{i}", y_at, dh_mid, W_OUT, i, 512, 1024, row_off=y_cp.shape[1])
        return dh_mid, dy

    def bwd_mix(i, dh_mid, dy, dus3):
        h_in, hn, u_cp = saved[i][:3]
        du_cp, sm, dwbd = _convpool_bwd(f"convpool_bwd{i}", u_cp, dy, conv_full[i], wbd[i], pool_scale[i:i + 1], lp,
                                        tm)
        sm = sm.reshape(4, 8, cw).sum(1)
        per_layer["w_conv"][i] = sm[0:3]
        per_layer["pool_scale"][i] = sm[3]
        per_layer["w_pool"][i] = _diag_blocks(dwbd, ngrp)
        dus, off = [du_cp, *dus3], 0
        for j, du in enumerate(dus):
            dw(f"in_proj_dw{j}_{i}", hn, du, W_IN, i, 1024, du.shape[1], col_off=off)
            off += du.shape[1]
        dh, dg8 = _in_proj_dx(f"in_proj_dx{i}", dus, h_in, dh_mid, g_mix[i], whole(W_IN), i, tm)
        per_layer["g_mix"][i] = dg8.sum(0)
        return dh

    def attn_bwd(i, dy, rider):
        qkv, (lt, g0) = saved[i][3], saved[i][6]
        res = _attn_bwd(f"attn_bwd{i}", qkv, lt, g0, dy, bl, lp, rider)
        return res[:3], res[3:]

    mlp_ws = [W_DOWN, W_UP, W_OUT]
    dh_mid, dy = bwd_mlp(1, dh)
    ws = [(w, 1) for w in mlp_ws]
    dus3, outs = attn_bwd(1, dy, swap_rider(ws))
    swapped(ws, outs)
    dh = bwd_mix(1, dh_mid, dy, dus3)
    swapped([(W_IN, 1)], _run_rider("grads_swap_in1", swap_rider([(W_IN, 1)])))

    dh_mid, dy = bwd_mlp(0, dh)
    ws = [(w, 0) for w in mlp_ws]
    swapped(ws, _run_rider("grads_swap0", swap_rider(ws)))
    items = list(sums)
    dus3, outs = attn_bwd(0, dy, scatter_rider(items))
    arrived.update(zip(items, outs))
    dh0 = bwd_mix(0, dh_mid, dy, dus3)
    swapped([(W_IN, 0)], _run_rider("grads_swap_in0", swap_rider([(W_IN, 0)])))
    arrived[W_IN, 0] = _run_rider("grads_scatter_in0", scatter_rider([(W_IN, 0)]))[0]

    finals = []
    for w in range(4):
        rs_, cs_ = shards[w].shape[1:]
        part = None
        for layer in reversed(range(depth)):
            part = _add_chips(f"reduce_{BIG[w]}{layer}", sums[w, layer], arrived[w, layer], AXES[w], layer, kc_idx,
                              128, part, (depth, 2, rs_ // 2, cs_))
        finals.append(part)
    finals = _run_rider("grads_join", _join_rider(finals))
    grad = {BIG[w]: finals[w].reshape(shards[w].shape) for w in range(4)}

    dh0 = dh0.reshape(bl, lp, d)
    grad_x = dh0[:, PAD + N_META:]
    local = {k: jnp.stack(v) for k, v in per_layer.items()}
    pieces = [dh0[:, PAD:PAD + N_META].reshape(bl * N_META, d), local["g_mix"], local["g_mlp"],
              dgf8.sum(0).reshape(1, d),
              jnp.pad(local["w_conv"].reshape(-1), (0, 2 * d - local["w_conv"].size)).reshape(2, d),
              jnp.pad(local["pool_scale"].reshape(-1), (0, d - local["pool_scale"].size)).reshape(1, d),
              local["w_pool"].reshape(-1, d)]
    summed, meta_sum = _all_reduce_small("small_grads", jnp.concatenate(pieces, axis=0), N_META, bl)
    o = bl * N_META
    grad.update({
        "meta_tokens": lax.dynamic_slice_in_dim(meta_sum, chip * meta_tokens.shape[1], meta_tokens.shape[1], 1),
        "g_mix": summed[o:o + 2], "g_mlp": summed[o + 2:o + 4], "g_final": summed[o + 4],
        "w_conv": lax.dynamic_slice_in_dim(summed[o + 5:o + 7].reshape(-1)[:2 * 3 * cw].reshape(2, 3, cw),
                                           chip * cs, cs, 2),
        "pool_scale": summed[o + 7].reshape(-1)[:pool_scale.size].reshape(pool_scale.shape),
        "w_pool": summed[o + 8:].reshape(w_pool.shape),
    })

    weights = dict(meta_tokens=meta_tokens, g_mix=g_mix, w_in=w_in, w_conv=w_conv, w_pool=w_pool,
                   pool_scale=pool_scale, w_out=w_out, g_mlp=g_mlp, w_up=w_up, w_down=w_down, g_final=g_final)
    ms = dict(meta_tokens=m_meta_tokens, g_mix=m_g_mix, w_in=m_w_in, w_conv=m_w_conv, w_pool=m_w_pool,
              pool_scale=m_pool_scale, w_out=m_w_out, g_mlp=m_g_mlp, w_up=m_w_up, w_down=m_w_down,
              g_final=m_g_final)
    vs = dict(meta_tokens=v_meta_tokens, g_mix=v_g_mix, w_in=v_w_in, w_conv=v_w_conv, w_pool=v_w_pool,
              pool_scale=v_pool_scale, w_out=v_w_out, g_mlp=v_g_mlp, w_up=v_w_up, w_down=v_w_down,
              g_final=v_g_final)
    order = list(weights)
    upd = {k: _adamw(f"adamw_{k}", weights[k], grad[k], ms[k], vs[k], 256) for k in BIG}
    little = [k for k in order if k not in BIG]
    for k, res in zip(little, _adamw_small("adamw_small", [(weights[k], grad[k].reshape(weights[k].shape), ms[k],
                                                            vs[k]) for k in little])):
        upd[k] = res
    grad = {k: grad[k].reshape(weights[k].shape) for k in order}
    return (loss, grad_x, *[grad[k] for k in order], *[upd[k][0] for k in order], *[upd[k][1] for k in order],
            *[upd[k][2] for k in order])
```

```python
import functools

import jax
import jax.numpy as jnp
from jax import lax
from jax.experimental import pallas as pl
from jax.experimental.pallas import tpu as pltpu

F32, BF16 = jnp.float32, jnp.bfloat16
MESH = pl.DeviceIdType.MESH
EPS = 1e-6
N_META = 16
QB = 128
PAD = QB - N_META
HALO = 16
POOL_WINDOWS = (2.0, 4.0, 8.0, 16.0)
HEAD_SCALE = 0.125
LR, B1, B2, ADAM_EPS, WD, STEP = 0.001, 0.9, 0.999, 1e-08, 0.01, 10
VMEM_LIMIT = 56 * 1024 * 1024


def _params(sem=None):
    return pltpu.CompilerParams(dimension_semantics=sem, vmem_limit_bytes=VMEM_LIMIT)


def _nt(a, b):
    return lax.dot_general(a, b, (((1,), (1,)), ((), ())), preferred_element_type=F32)


def _tn(a, b):
    return lax.dot_general(a, b, (((0,), (0,)), ((), ())), preferred_element_type=F32)


def _nn(a, b):
    return jnp.dot(a, b, preferred_element_type=F32)


def _fold8(v):
    r, c = v.shape
    return jnp.sum(v.reshape(r // 8, 8, c), axis=0)


NCH = 512


def _rows_call(name, body, tm, row_ins, consts, row_outs, accs=(), rider=None):
    t = row_ins[0].shape[0]
    ride = _Ride(rider, len(row_ins) + len(consts), len(row_outs) + len(accs), t // tm)

    def stepped(*refs):
        step = pl.program_id(0)
        ride.before(refs, step)
        body(*ride.own(refs))
        ride.after(refs, step)

    in_specs = [pl.BlockSpec((tm, a.shape[1]), lambda i: (i, 0)) for a in row_ins]
    for a, layer in consts:
        if layer is None:
            in_specs.append(pl.BlockSpec(a.shape, lambda i: (0, 0)))
        else:
            in_specs.append(pl.BlockSpec((None, *a.shape[1:]), lambda i, l=layer: (l, 0, 0)))
    return ride.call(
        stepped, name, (t // tm,), in_specs, [*row_ins, *[a for a, _ in consts]],
        [pl.BlockSpec((tm, c), lambda i: (i, 0)) for c, _ in row_outs]
        + [pl.BlockSpec(s, lambda i: (0, 0)) for s in accs],
        [jax.ShapeDtypeStruct((t, c), dt) for c, dt in row_outs] + [jax.ShapeDtypeStruct(s, F32) for s in accs],
        [], ("arbitrary",) if accs else ("parallel",))


def _norm_parts(x):
    r = lax.rsqrt(jnp.mean(x * x, axis=-1, keepdims=True) + EPS)
    return r, x * r


def _norm_bwd(r, xh, dyn, g):
    w = dyn * g
    return r * (w - xh * jnp.mean(w * xh, axis=-1, keepdims=True))


def _in_proj(name, h, g, w, layer, tm, ncp):
    d, n = h.shape[1], w.shape[2]

    def body(h_ref, g_ref, w_ref, hn_ref, ucp_ref, qkv_ref):
        _, xh = _norm_parts(h_ref[...])
        hn = (xh * g_ref[...]).astype(BF16)
        hn_ref[...] = hn
        for n0 in range(0, n, NCH):
            acc = _nn(hn, w_ref[:, n0:n0 + NCH])
            if n0 < ncp:
                ucp_ref[:, n0:n0 + NCH] = acc
            else:
                qkv_ref[:, n0 - ncp:n0 - ncp + NCH] = acc.astype(BF16)

    return _rows_call(name, body, tm, [h], [(g.reshape(1, d), None), (w, layer)],
                      [(d, BF16), (ncp, F32), (n - ncp, BF16)])


def _out_proj(name, y_cp, y_at, h, w, layer, tm):
    d, k1 = h.shape[1], y_cp.shape[1]

    def body(ycp_ref, yat_ref, h_ref, w_ref, o_ref):
        for n0 in range(0, d, NCH):
            o_ref[:, n0:n0 + NCH] = (h_ref[:, n0:n0 + NCH] + _nn(ycp_ref[...], w_ref[0:k1, n0:n0 + NCH])
                                     + _nn(yat_ref[...], w_ref[k1:, n0:n0 + NCH]))

    return _rows_call(name, body, tm, [y_cp, y_at, h], [(w, layer)], [(d, F32)])[0]


def _up_proj(name, h_mid, g, w, layer, tm, rider=None):
    d, n = h_mid.shape[1], w.shape[2]

    def body(h_ref, g_ref, w_ref, hn_ref, m_ref, act_ref):
        _, xh = _norm_parts(h_ref[...])
        hn = (xh * g_ref[...]).astype(BF16)
        hn_ref[...] = hn
        for n0 in range(0, n, NCH):
            acc = _nn(hn, w_ref[:, n0:n0 + NCH])
            m_ref[:, n0:n0 + NCH] = acc.astype(BF16)
            act_ref[:, n0:n0 + NCH] = jnp.square(jnp.maximum(acc, 0.0)).astype(BF16)

    return _rows_call(name, body, tm, [h_mid], [(g.reshape(1, d), None), (w, layer)],
                      [(d, BF16), (n, BF16), (n, BF16)], rider=rider)


def _down_proj(name, act, h_mid, w, layer, tm, rider=None):
    d = h_mid.shape[1]

    def body(a_ref, h_ref, w_ref, o_ref):
        for n0 in range(0, d, NCH):
            o_ref[:, n0:n0 + NCH] = h_ref[:, n0:n0 + NCH] + _nn(a_ref[...], w_ref[:, n0:n0 + NCH])

    return _rows_call(name, body, tm, [act, h_mid], [(w, layer)], [(d, F32)], rider=rider)


def _down_proj_dx(name, dh, m_pre, w, layer, tm):
    n = w.shape[1]

    def body(dh_ref, m_ref, w_ref, dm_ref):
        dhb = dh_ref[...].astype(BF16)
        for n0 in range(0, n, NCH):
            dm_ref[:, n0:n0 + NCH] = (_nt(dhb, w_ref[n0:n0 + NCH, :])
                                      * (2.0 * jnp.maximum(m_ref[:, n0:n0 + NCH].astype(F32), 0.0))).astype(BF16)

    return _rows_call(name, body, tm, [dh, m_pre], [(w, layer)], [(n, BF16)])[0]


def _up_proj_dx(name, dm, h_mid, dh, g, w_up, w_out, layer, tm):
    d = h_mid.shape[1]

    def body(dm_ref, h_ref, dh_ref, g_ref, wup_ref, wout_ref, dhm_ref, dy_ref, dg_ref):
        @pl.when(pl.program_id(0) == 0)
        def _():
            dg_ref[...] = jnp.zeros_like(dg_ref)
        dyn = _nt(dm_ref[...], wup_ref[...])
        r, xh = _norm_parts(h_ref[...])
        dhm = dh_ref[...] + _norm_bwd(r, xh, dyn, g_ref[...])
        dhm_ref[...] = dhm
        dg_ref[...] += _fold8(dyn * xh)
        dy_ref[...] = _nt(dhm.astype(BF16), wout_ref[...])

    return _rows_call(name, body, tm, [dm, h_mid, dh], [(g.reshape(1, d), None), (w_up, layer), (w_out, layer)],
                      [(d, F32), (w_out.shape[1], F32)], [(8, d)])


def _in_proj_dx(name, dus, h, dh_mid, g, w, layer, tm):
    d = h.shape[1]
    ns = [du.shape[1] for du in dus]
    nd = len(dus)

    def body(*refs):
        du_refs = refs[:nd]
        h_ref, dhm_ref, g_ref, w_ref, dh_ref, dg_ref = refs[nd:]

        @pl.when(pl.program_id(0) == 0)
        def _():
            dg_ref[...] = jnp.zeros_like(dg_ref)
        dyn, off = None, 0
        for du_ref, n in zip(du_refs, ns):
            part = _nt(du_ref[...], w_ref[:, off:off + n])
            dyn = part if dyn is None else dyn + part
            off += n
        r, xh = _norm_parts(h_ref[...])
        dh_ref[...] = dhm_ref[...] + _norm_bwd(r, xh, dyn, g_ref[...])
        dg_ref[...] += _fold8(dyn * xh)

    return _rows_call(name, body, tm, [*dus, h, dh_mid], [(g.reshape(1, d), None), (w, layer)], [(d, F32)],
                      [(8, d)])


def _mm_tn(name, a, b, tt, tka, tn, into, shape, layer, row_off, col_off):
    t, ka = a.shape
    n = b.shape[1]
    assert t % tt == 0 and ka % tka == 0 and n % tn == 0 and row_off % tka == 0 and col_off % tn == 0

    def body(a_ref, b_ref, *rest):
        o_ref = rest[-1]

        @pl.when(pl.program_id(2) == 0)
        def _():
            o_ref[...] = jnp.zeros_like(o_ref)
        o_ref[...] += _tn(a_ref[...].astype(BF16), b_ref[...].astype(BF16))

    in_specs = [pl.BlockSpec((tt, tka), lambda i, j, s: (s, i)), pl.BlockSpec((tt, tn), lambda i, j, s: (s, j))]
    args = [a, b]
    if into is not None:
        in_specs.append(pl.BlockSpec(memory_space=pl.ANY))
        args.append(into)
    return pl.pallas_call(
        body, name=name, grid=(ka // tka, n // tn, t // tt), in_specs=in_specs,
        out_specs=pl.BlockSpec((None, tka, tn), lambda i, j, s: (layer, row_off // tka + i, col_off // tn + j)),
        out_shape=jax.ShapeDtypeStruct(shape, F32),
        input_output_aliases={} if into is None else {2: 0},
        compiler_params=_params(("parallel", "parallel", "arbitrary")),
    )(*args)


def _loss_bwd(name, h, g, target, lp):
    t, d = h.shape
    bl = target.shape[0]
    nq = lp // QB

    def body(h_ref, g_ref, t_ref, dh_ref, ls_ref, dg_ref):
        b, j = pl.program_id(0), pl.program_id(1)

        @pl.when((b == 0) & (j == 0))
        def _():
            ls_ref[...] = jnp.zeros_like(ls_ref)
            dg_ref[...] = jnp.zeros_like(dg_ref)
        xv = h_ref[...]
        r = lax.rsqrt(jnp.mean(xv * xv, axis=-1, keepdims=True) + EPS)
        xh = xv * r
        gv = g_ref[...]
        err = jnp.where(j >= 1, xh * gv - t_ref[...], 0.0)
        ls_ref[...] += _fold8(err * err) * (0.5 / d)
        dy = err * (1.0 / d)
        w = dy * gv
        dh_ref[...] = r * (w - xh * jnp.mean(w * xh, axis=-1, keepdims=True))
        dg_ref[...] += _fold8(dy * xh)

    return pl.pallas_call(
        body, name=name, grid=(bl, nq),
        in_specs=[pl.BlockSpec((QB, d), lambda b, j: (b * nq + j, 0)), pl.BlockSpec((1, d), lambda b, j: (0, 0)),
                  pl.BlockSpec((None, QB, d), lambda b, j: (b, jnp.maximum(j - 1, 0), 0))],
        out_specs=[pl.BlockSpec((QB, d), lambda b, j: (b * nq + j, 0)), pl.BlockSpec((8, d), lambda b, j: (0, 0)),
                   pl.BlockSpec((8, d), lambda b, j: (0, 0))],
        out_shape=[jax.ShapeDtypeStruct((t, d), F32), jax.ShapeDtypeStruct((8, d), F32),
                   jax.ShapeDtypeStruct((8, d), F32)],
        compiler_params=_params(("arbitrary", "arbitrary")),
    )(h, g.reshape(1, d), target)


def _pool_select(grp, a2, a4, a8, a16):
    return jnp.where(grp == 0, a2, jnp.where(grp == 1, a4, jnp.where(grp == 2, a8, a16)))


def _trailing_sums(v):
    s2 = v + pltpu.roll(v, 1, 0)
    s4 = s2 + pltpu.roll(s2, 2, 0)
    s8 = s4 + pltpu.roll(s4, 4, 0)
    s16 = s8 + pltpu.roll(s8, 8, 0)
    return s2, s4, s8, s16


def _leading_sums(v):
    n = v.shape[0]
    s2 = v + pltpu.roll(v, n - 1, 0)
    s4 = s2 + pltpu.roll(s2, n - 2, 0)
    s8 = s4 + pltpu.roll(s4, n - 4, 0)
    s16 = s8 + pltpu.roll(s8, n - 8, 0)
    return s2, s4, s8, s16


def _convpool_fwd(name, u_cp, wconv, wbd, pscale, lp, r):
    t = u_cp.shape[0]
    cw = u_cp.shape[1] // 4
    tps, hb = lp // r, r // HALO

    def body(cb_ref, cc_ref, cx_ref, pi_ref, cch_ref, cxh_ref, pih_ref, wc_ref, wbd_ref, ps_ref, y_ref):
        i = pl.program_id(0)
        lrow = (i % tps) * r + lax.broadcasted_iota(jnp.int32, (r, 1), 0)
        valid = lrow >= PAD
        xx = jnp.concatenate([cch_ref[...] * cxh_ref[...], cc_ref[...] * cx_ref[...]], axis=0)
        conv = (wc_ref[0:1, :] * pltpu.roll(xx, 2, 0) + wc_ref[1:2, :] * pltpu.roll(xx, 1, 0)
                + wc_ref[2:3, :] * xx)
        y_ref[:, 0:cw] = (cb_ref[...] * conv[HALO:]).astype(y_ref.dtype)
        p = pi_ref[...]
        grp = lax.broadcasted_iota(jnp.int32, (1, cw), 1) // (cw // 4)
        sel = _pool_select(grp, *_trailing_sums(jnp.concatenate([pih_ref[...], p], axis=0)))[HALO:]
        cnt = jnp.maximum(jnp.minimum((lrow - (PAD - 1)).astype(F32), _pool_select(grp, *POOL_WINDOWS)), 1.0)
        pooled = jnp.where(valid, sel / cnt - p, 0.0)
        y_ref[:, cw:2 * cw] = (_nn(pooled.astype(BF16), wbd_ref[...]) * ps_ref[...]).astype(y_ref.dtype)

    def main(col):
        return pl.BlockSpec((r, cw), lambda i: (i, col))

    def prev(col):
        return pl.BlockSpec((HALO, cw), lambda i: (jnp.maximum(i * hb - 1, 0), col))

    def whole(a):
        return pl.BlockSpec(a.shape, lambda i: (0, 0))

    return pl.pallas_call(
        body, name=name, grid=(t // r,),
        in_specs=[main(0), main(1), main(2), main(3), prev(1), prev(2), prev(3), whole(wconv), whole(wbd),
                  whole(pscale)],
        out_specs=pl.BlockSpec((r, 2 * cw), lambda i: (i, 0)),
        out_shape=jax.ShapeDtypeStruct((t, 2 * cw), BF16),
        compiler_params=_params(("parallel",)),
    )(u_cp, u_cp, u_cp, u_cp, u_cp, u_cp, u_cp, wconv, wbd, pscale)


def _convpool_bwd(name, u_cp, dy, wconv, wbd, pscale, lp, r):
    t = u_cp.shape[0]
    cw = u_cp.shape[1] // 4
    tps, hb = lp // r, r // HALO
    e = r + HALO

    def body(cb_ref, cc_ref, cx_ref, pi_ref, cbn_ref, cch_ref, cxh_ref, pih_ref, dyc_ref, dyp_ref, dycn_ref,
             dypn_ref, wc_ref, wbd_ref, ps_ref, du_ref, sm_ref, dwbd_ref):
        i = pl.program_id(0)

        @pl.when(i == 0)
        def _():
            sm_ref[...] = jnp.zeros_like(sm_ref)
            dwbd_ref[...] = jnp.zeros_like(dwbd_ref)
        lrow_e = (i % tps) * r + lax.broadcasted_iota(jnp.int32, (e, 1), 0)
        valid_e = (lrow_e >= PAD) & (lrow_e < lp)
        lrow, valid = lrow_e[:r], lrow_e[:r] >= PAD
        w0, w1, w2 = wc_ref[0:1, :], wc_ref[1:2, :], wc_ref[2:3, :]
        cb, cc, cx = cb_ref[...], cc_ref[...], cx_ref[...]
        prod = cc * cx
        xx = jnp.concatenate([cch_ref[...] * cxh_ref[...], prod], axis=0)
        back1, back2 = pltpu.roll(xx, 1, 0)[HALO:], pltpu.roll(xx, 2, 0)[HALO:]
        dyc = dyc_ref[...]
        du_ref[:, 0:cw] = (dyc * (w0 * back2 + w1 * back1 + w2 * prod)).astype(du_ref.dtype)
        dconv_e = jnp.where(valid_e, jnp.concatenate([dyc * cb, dycn_ref[...] * cbn_ref[...]], axis=0), 0.0)
        dconv = dconv_e[:r]
        dprod = (w2 * dconv + w1 * pltpu.roll(dconv_e, e - 1, 0)[:r] + w0 * pltpu.roll(dconv_e, e - 2, 0)[:r])
        du_ref[:, cw:2 * cw] = (dprod * cx).astype(du_ref.dtype)
        du_ref[:, 2 * cw:3 * cw] = (dprod * cc).astype(du_ref.dtype)
        sm_ref[0:8, :] += _fold8(dconv * back2)
        sm_ref[8:16, :] += _fold8(dconv * back1)
        sm_ref[16:24, :] += _fold8(dconv * prod)
        p = pi_ref[...]
        grp = lax.broadcasted_iota(jnp.int32, (1, cw), 1) // (cw // 4)
        win = _pool_select(grp, *POOL_WINDOWS)
        sel = _pool_select(grp, *_trailing_sums(jnp.concatenate([pih_ref[...], p], axis=0)))[HALO:]
        cnt_e = jnp.maximum(jnp.minimum((lrow_e - (PAD - 1)).astype(F32), win), 1.0)
        pooled = jnp.where(valid, sel / cnt_e[:r] - p, 0.0).astype(BF16)
        dyp = dyp_ref[...]
        sm_ref[24:32, :] += _fold8(dyp * _nn(pooled, wbd_ref[...]))
        dpre_e = (jnp.concatenate([dyp, dypn_ref[...]], axis=0) * ps_ref[...]).astype(BF16)
        dwbd_ref[...] += _tn(pooled, dpre_e[:r])
        dpooled_e = jnp.where(valid_e, _nt(dpre_e, wbd_ref[...]), 0.0)
        ahead = _pool_select(grp, *_leading_sums(dpooled_e / cnt_e))[:r]
        du_ref[:, 3 * cw:4 * cw] = (ahead - dpooled_e[:r]).astype(du_ref.dtype)

    last_halo = t // HALO - 1

    def main(col):
        return pl.BlockSpec((r, cw), lambda i: (i, col))

    def prev(col):
        return pl.BlockSpec((HALO, cw), lambda i: (jnp.maximum(i * hb - 1, 0), col))

    def nxt(col):
        return pl.BlockSpec((HALO, cw), lambda i: (jnp.minimum((i + 1) * hb, last_halo), col))

    def whole(a):
        return pl.BlockSpec(a.shape, lambda i: (0, 0))

    return pl.pallas_call(
        body, name=name, grid=(t // r,),
        in_specs=[main(0), main(1), main(2), main(3), nxt(0), prev(1), prev(2), prev(3), main(0), main(1), nxt(0),
                  nxt(1), whole(wconv), whole(wbd), whole(pscale)],
        out_specs=[pl.BlockSpec((r, 4 * cw), lambda i: (i, 0)), pl.BlockSpec((32, cw), lambda i: (0, 0)),
                   pl.BlockSpec((cw, cw), lambda i: (0, 0))],
        out_shape=[jax.ShapeDtypeStruct((t, 4 * cw), BF16), jax.ShapeDtypeStruct((32, cw), F32),
                   jax.ShapeDtypeStruct((cw, cw), F32)],
        compiler_params=_params(("arbitrary",)),
    )(u_cp, u_cp, u_cp, u_cp, u_cp, u_cp, u_cp, u_cp, dy, dy, dy, dy, wconv, wbd, pscale)


KW = 2 * QB
HP = 4
DECAY = 64.0


def _cumsum_matrix(before, kw):
    r = lax.broadcasted_iota(jnp.int32, (kw, kw), 0)
    c = lax.broadcasted_iota(jnp.int32, (kw, kw), 1)
    return ((r < c) if before else (r > c)).astype(BF16)


def _running(v, mat):
    m = v.shape[0]
    hi = v.astype(BF16)
    ext = _nn(jnp.concatenate([hi, (v - hi.astype(F32)).astype(BF16)], axis=0), mat)
    return ext[:m] + ext[m:]


def _log_sigmoid(z):
    neg_abs = lax.bitcast_convert_type(lax.bitcast_convert_type(z, jnp.int32) | jnp.int32(-2 ** 31), F32)
    return jnp.minimum(z, 0.0) - jnp.log(1.0 + jnp.exp(neg_abs))


def _stack_heads(v, head0):
    zero = jnp.zeros_like(v)
    return jnp.concatenate([jnp.where(head0, v, zero), jnp.where(head0, zero, v)], axis=0)


def _lanes(hp):
    return slice(hp * QB, (hp + 1) * QB)


def _attn_fwd(name, qkv, bl, lp, rider=None):
    t = qkv.shape[0]
    nq, nblk = lp // QB, qkv.shape[1] // (3 * HP * QB)
    assert nblk == 1
    ride = _Ride(rider, 3, 3, bl * nblk * nq)

    def body(*refs):
        q_ref, k_ref, v_ref, o_ref, lt_ref, g0_ref = ride.own(refs)
        qi = pl.program_id(2)
        step = (pl.program_id(0) * nblk + pl.program_id(1)) * nq + qi
        ride.before(refs, step)
        head0 = lax.broadcasted_iota(jnp.int32, (QB, QB), 1) < QB // 2
        q2 = [_stack_heads(q_ref[:, _lanes(hp)] * jnp.asarray(HEAD_SCALE, BF16), head0) for hp in range(HP)]
        later = {KW: _cumsum_matrix(False, KW), QB: _cumsum_matrix(False, QB)}
        q_pos = qi * QB + (lax.broadcasted_iota(jnp.int32, (2 * QB, KW), 0) & (QB - 1))
        col = lax.broadcasted_iota(jnp.int32, (2 * QB, KW), 1)
        rest = qi // 2

        def group(start, kw, carry, masked):
            start = pl.multiple_of(start, QB)
            if masked:
                k_pos = start + col[:, :kw]
                valid = (k_pos < q_pos[:, :kw]) & (k_pos >= PAD)
            z = [_nt(q2[hp], k_ref[pl.ds(start, kw), _lanes(hp)]) for hp in range(HP)]
            logp, after, rs = [], [], []
            for hp in range(HP):
                lp_ = _log_sigmoid(z[hp])
                lk = lp_ - z[hp]
                if masked:
                    lk = jnp.where(valid, lk, 0.0)
                logp.append(lp_)
                rs.append(jnp.sum(lk, axis=1, keepdims=True))
                after.append(_running(lk, later[kw]))
            out = []
            for hp in range(HP):
                run, acc = carry[2 * hp], carry[2 * hp + 1]
                a = jnp.exp(logp[hp] + after[hp] + run)
                if masked:
                    a = jnp.where(valid, a, 0.0)
                out += [run + rs[hp], acc + _nn(a.astype(BF16), v_ref[pl.ds(start, kw), _lanes(hp)])]
            return tuple(out)

        def alive(carry):
            most = carry[0]
            for hp in range(1, HP):
                most = jnp.maximum(most, carry[2 * hp])
            return jnp.max(most) > -DECAY

        carry = (jnp.zeros((2 * QB, 1), F32), jnp.zeros((2 * QB, QB), F32)) * HP
        odd = qi % 2
        carry = lax.fori_loop(0, 1 - odd, lambda i, c: group(qi * QB, QB, c, True), carry)
        carry = lax.fori_loop(0, odd, lambda i, c: group(rest * KW, KW, c, True), carry)
        g, *carry = lax.while_loop(lambda st: (st[0] >= 1) & alive(st[1:]),
                                   lambda st: (st[0] - 1, *group(st[0] * KW, KW, tuple(st[1:]), False)),
                                   (rest - 1, *carry))
        oldest = (g == 0) & (rest >= 1) & alive(carry)
        carry = lax.fori_loop(0, oldest.astype(jnp.int32), lambda i, c: group(0, KW, c, True), tuple(carry))
        g0_ref[pl.program_id(0), qi] = jnp.where(oldest, 0, g + 1).astype(F32)
        for hp in range(HP):
            run, acc = carry[2 * hp], carry[2 * hp + 1]
            o_ref[:, _lanes(hp)] = jnp.where(head0, acc[:QB], acc[QB:]).astype(o_ref.dtype)
            lt_ref[:, _lanes(hp)] = jnp.where(head0, run[:QB], run[QB:])
        ride.after(refs, step)

    wb = HP * QB
    blk = pl.BlockSpec((QB, wb), lambda b, p, i: (b * nq + i, p))
    return ride.call(
        body, name, (bl, nblk, nq),
        [blk, pl.BlockSpec((lp, wb), lambda b, p, i: (b, nblk + p)),
         pl.BlockSpec((lp, wb), lambda b, p, i: (b, 2 * nblk + p))], [qkv, qkv, qkv],
        [blk, blk, pl.BlockSpec(memory_space=pltpu.SMEM)],
        [jax.ShapeDtypeStruct((t, nblk * wb), BF16), jax.ShapeDtypeStruct((t, nblk * wb), F32),
         jax.ShapeDtypeStruct((bl, nq), F32)], [])


def _attn_bwd(name, qkv, lt, g0, dy, bl, lp, rider=None):
    t = qkv.shape[0]
    nq, nblk = lp // QB, qkv.shape[1] // (3 * HP * QB)
    ride = _Ride(rider, 6, 3, bl * nblk * nq)

    def body(*refs):
        q_ref, k_ref, v_ref, lt_ref, do_ref, g0_ref, dq_ref, dk_ref, dv_ref, dk_acc, dv_acc = ride.own(refs)
        qi = pl.program_id(2)
        step = (pl.program_id(0) * nblk + pl.program_id(1)) * nq + qi
        ride.before(refs, step)

        @pl.when(qi == 0)
        def _():
            dk_acc[...] = jnp.zeros_like(dk_acc)
            dv_acc[...] = jnp.zeros_like(dv_acc)
        lane = lax.broadcasted_iota(jnp.int32, (QB, QB), 1)
        head0 = lane < QB // 2
        q2, do2, total = [], [], []
        for hp in range(HP):
            q2.append(_stack_heads(q_ref[:, _lanes(hp)] * jnp.asarray(HEAD_SCALE, BF16), head0))
            do2.append(_stack_heads(do_ref[:, _lanes(hp)].astype(BF16), head0))
            ltv = lt_ref[:, _lanes(hp)]
            total.append(jnp.concatenate(
                [jnp.sum(jnp.where(lane == 0, ltv, 0.0), axis=1, keepdims=True),
                 jnp.sum(jnp.where(lane == QB // 2, ltv, 0.0), axis=1, keepdims=True)], axis=0))
        later = {KW: _cumsum_matrix(False, KW), QB: _cumsum_matrix(False, QB)}
        earlier = {KW: _cumsum_matrix(True, KW), QB: _cumsum_matrix(True, QB)}
        q_pos = qi * QB + (lax.broadcasted_iota(jnp.int32, (2 * QB, KW), 0) & (QB - 1))
        col = lax.broadcasted_iota(jnp.int32, (2 * QB, KW), 1)
        rest = qi // 2

        def group(start, kw, carry, masked):
            start = pl.multiple_of(start, QB)
            if masked:
                k_pos = start + col[:, :kw]
                valid = (k_pos < q_pos[:, :kw]) & (k_pos >= PAD)
            hps = range(HP)
            kg = [k_ref[pl.ds(start, kw), _lanes(hp)] for hp in hps]
            z = [_nt(q2[hp], kg[hp]) for hp in hps]
            da = [_nt(do2[hp], v_ref[pl.ds(start, kw), _lanes(hp)]) for hp in hps]
            logp, sig, after, rs = [], [], [], []
            for hp in hps:
                lp_ = _log_sigmoid(z[hp])
                lk = lp_ - z[hp]
                if masked:
                    lk = jnp.where(valid, lk, 0.0)
                logp.append(lp_)
                sig.append(jnp.exp(lp_))
                rs.append(jnp.sum(lk, axis=1, keepdims=True))
                after.append(_running(lk, later[kw]))
            a, gg, before = [], [], []
            for hp in hps:
                a_ = jnp.exp(logp[hp] + after[hp] + (total[hp] - carry[3 * hp] - rs[hp]))
                if masked:
                    a_ = jnp.where(valid, a_, 0.0)
                a.append(a_.astype(BF16))
                gg.append(a_ * da[hp])
                before.append(_nn(gg[hp].astype(BF16), earlier[kw]))
            out = []
            for hp in hps:
                seen, gsum, dq = carry[3 * hp], carry[3 * hp + 1], carry[3 * hp + 2]
                dz = gg[hp] - (gg[hp] + before[hp] + gsum) * sig[hp]
                if masked:
                    dz = jnp.where(valid, dz, 0.0)
                dz = dz.astype(BF16)
                dk_acc[pl.ds(start, kw), _lanes(hp)] += _tn(dz, q2[hp])
                dv_acc[pl.ds(start, kw), _lanes(hp)] += _tn(a[hp], do2[hp])
                out += [seen + rs[hp], gsum + jnp.sum(gg[hp], axis=1, keepdims=True), dq + _nn(dz, kg[hp])]
            return tuple(out)

        col0 = jnp.zeros((2 * QB, 1), F32)
        carry = (col0, col0, jnp.zeros((2 * QB, QB), F32)) * HP
        first = g0_ref[pl.program_id(0), qi].astype(jnp.int32)
        odd = qi % 2
        carry = lax.fori_loop(0, ((first == 0) & (rest >= 1)).astype(jnp.int32), lambda i, c: group(0, KW, c, True),
                              carry)
        carry = lax.fori_loop(jnp.maximum(first, 1), rest, lambda g, c: group(g * KW, KW, c, False), carry)
        carry = lax.fori_loop(0, odd, lambda i, c: group(rest * KW, KW, c, True), carry)
        carry = lax.fori_loop(0, 1 - odd, lambda i, c: group(qi * QB, QB, c, True), carry)
        for hp in range(HP):
            dq = carry[3 * hp + 2]
            dq_ref[:, _lanes(hp)] = (jnp.where(head0, dq[:QB], dq[QB:]) * HEAD_SCALE).astype(dq_ref.dtype)

        @pl.when(qi == nq - 1)
        def _():
            dk_ref[...] = dk_acc[...].astype(dk_ref.dtype)
            dv_ref[...] = dv_acc[...].astype(dv_ref.dtype)
        ride.after(refs, step)

    wb = HP * QB
    blk = pl.BlockSpec((QB, wb), lambda b, p, i: (b * nq + i, p))
    seq = pl.BlockSpec((lp, wb), lambda b, p, i: (b, p))
    out = jax.ShapeDtypeStruct((t, nblk * wb), BF16)
    return ride.call(
        body, name, (bl, nblk, nq),
        [blk, pl.BlockSpec((lp, wb), lambda b, p, i: (b, nblk + p)),
         pl.BlockSpec((lp, wb), lambda b, p, i: (b, 2 * nblk + p)), blk,
         pl.BlockSpec((QB, wb), lambda b, p, i: (b * nq + i, nblk + p)), pl.BlockSpec(memory_space=pltpu.SMEM)],
        [qkv, qkv, qkv, lt, dy, g0],
        [blk, seq, seq], [out, out, out], [pltpu.VMEM((lp, wb), F32), pltpu.VMEM((lp, wb), F32)])


def _place():
    return lax.axis_index("x"), lax.axis_index("y"), lax.axis_index("c")


def _peers(chip):
    kx, ky = chip // 2, chip % 2
    return ((1 - kx, ky), (kx, 1 - ky), (1 - kx, 1 - ky))


def _hbm_specs(n):
    return [pl.BlockSpec(memory_space=pl.ANY) for _ in range(n)]


def _remote(src, dst, send, recv, k, to):
    return pltpu.make_async_remote_copy(src, dst, send.at[k], recv.at[k], device_id=to, device_id_type=MESH)


class _Rider:
    def __init__(self, ins, out_shapes, aliases, nsem, first, mid=None, last=None):
        self.ins, self.out_shapes, self.aliases, self.nsem = list(ins), list(out_shapes), dict(aliases), nsem
        self.first, self.mid, self.last = first, mid, last


def _by_chip(fn):
    def run(ins, outs, send, recv):
        x, y, c = _place()
        for me in range(4):
            pl.when(2 * x + y == me)(functools.partial(fn, ins, outs, send, recv, me, c, (x, y, 1 - c)))
    return run


def _run_rider(name, rider):
    ni, no = len(rider.ins), len(rider.out_shapes)

    def body(*refs):
        args = (refs[:ni], refs[ni:ni + no], refs[ni + no], refs[ni + no + 1])
        for hook in (rider.first, rider.mid, rider.last):
            if hook is not None:
                hook(*args)

    return pl.pallas_call(
        body, name=name, in_specs=_hbm_specs(ni), out_specs=_hbm_specs(no), out_shape=rider.out_shapes,
        input_output_aliases=rider.aliases,
        scratch_shapes=[pltpu.SemaphoreType.DMA((rider.nsem,)), pltpu.SemaphoreType.DMA((rider.nsem,))],
        compiler_params=pltpu.CompilerParams(has_side_effects=True),
    )(*rider.ins)


class _Ride:
    def __init__(self, rider, n_in, n_out, steps):
        self.rider, self.n_in, self.n_out, self.steps = rider, n_in, n_out, steps
        self.ri = len(rider.ins) if rider else 0
        self.ro = len(rider.out_shapes) if rider else 0

    def own(self, refs):
        refs = list(refs)
        a, b = self.n_in, self.n_in + self.ri + self.n_out
        tail = refs[b + self.ro:len(refs) - 2] if self.rider else refs[b + self.ro:]
        return refs[:a] + refs[a + self.ri:b] + tail

    def _args(self, refs):
        a, b = self.n_in, self.n_in + self.ri + self.n_out
        return refs[a:a + self.ri], refs[b:b + self.ro], refs[-2], refs[-1]

    def before(self, refs, step):
        if self.rider is None:
            return
        pl.when(step == 0)(functools.partial(self.rider.first, *self._args(refs)))
        if self.rider.mid is not None:
            pl.when(step == (3 * self.steps) // 4)(functools.partial(self.rider.mid, *self._args(refs)))

    def after(self, refs, step):
        if self.rider is not None and self.rider.last is not None:
            pl.when(step == self.steps - 1)(functools.partial(self.rider.last, *self._args(refs)))

    def call(self, body, name, grid, in_specs, args, out_specs, out_shape, scratch,
             sem=("parallel", "parallel", "arbitrary")):
        r = self.rider
        if r is None:
            return pl.pallas_call(body, name=name, grid=grid, in_specs=in_specs, out_specs=out_specs,
                                  out_shape=out_shape, scratch_shapes=scratch, compiler_params=_params(sem))(*args)
        return pl.pallas_call(
            body, name=name, grid=grid, in_specs=in_specs + _hbm_specs(self.ri),
            out_specs=out_specs + _hbm_specs(self.ro), out_shape=out_shape + r.out_shapes,
            input_output_aliases={self.n_in + i: self.n_out + o for i, o in r.aliases.items()},
            scratch_shapes=scratch + [pltpu.SemaphoreType.DMA((r.nsem,)), pltpu.SemaphoreType.DMA((r.nsem,))],
            compiler_params=pltpu.CompilerParams(dimension_semantics=("arbitrary",) * len(grid),
                                                 vmem_limit_bytes=VMEM_LIMIT, has_side_effects=True),
        )(*args, *r.ins)


def _core_view(a, axis):
    l, r, c = a.shape
    return a.reshape(l, 4, 2, r // 8, c) if axis == 0 else a.reshape(l, 2, r // 2, c)


def _shard_view(a):
    l, r, c = a.shape
    return a.reshape(l, 2, r // 2, c)


def _piece(ref, axis, layer, chip, core):
    if axis == 0:
        return ref.at[layer, chip, core]
    cs = ref.shape[-1] // 4
    return ref.at[layer, core, :, pl.ds(chip * cs, cs)]


def _place_shard(name, w, axis, kidx, tr):
    _, r, cdim = w.shape
    shp = [2, r, cdim]
    shp[1 + axis] *= 4
    nb = r // tr

    def body(k_ref, w_ref, o_ref):
        o_ref[...] = w_ref[...].astype(o_ref.dtype)

    if axis == 0:
        out_spec = pl.BlockSpec((None, tr, cdim), lambda l, i, k_ref: (l, k_ref[0] * nb + i, 0))
    else:
        out_spec = pl.BlockSpec((None, tr, cdim), lambda l, i, k_ref: (l, i, k_ref[0]))
    return pl.pallas_call(
        body, name=name,
        grid_spec=pltpu.PrefetchScalarGridSpec(
            num_scalar_prefetch=1, grid=(2, nb),
            in_specs=[pl.BlockSpec((None, tr, cdim), lambda l, i, k_ref: (l, i, 0))], out_specs=out_spec),
        out_shape=jax.ShapeDtypeStruct(tuple(shp), BF16),
        compiler_params=_params(("arbitrary", "arbitrary")),
    )(kidx, w)


def _gather_rider(views, axes, items):
    n = len(items)

    def first(ins, outs, send, recv, me, c, sib):
        for i, (w, l) in enumerate(items):
            for j, (px, py) in enumerate(_peers(me)):
                _remote(_piece(ins[w], axes[w], l, me, c), _piece(outs[w], axes[w], l, me, c), send, recv,
                        3 * i + j, (px, py, c)).start()

    def mid(ins, outs, send, recv, me, c, sib):
        for i, (w, l) in enumerate(items):
            for j, (px, py) in enumerate(_peers(me)):
                got = _piece(outs[w], axes[w], l, 2 * px + py, c)
                _remote(got, got, send, recv, 3 * i + j, (px, py, c)).wait_recv()
                _remote(got, got, send, recv, 3 * (n + i) + j, sib).start()

    def last(ins, outs, send, recv, me, c, sib):
        for i, (w, l) in enumerate(items):
            for j, (px, py) in enumerate(_peers(me)):
                mine, got = _piece(outs[w], axes[w], l, me, c), _piece(outs[w], axes[w], l, 2 * px + py, c)
                theirs = _piece(outs[w], axes[w], l, 2 * px + py, 1 - c)
                _remote(theirs, theirs, send, recv, 3 * (n + i) + j, sib).wait_recv()
                _remote(mine, mine, send, recv, 3 * i + j, (px, py, c)).wait_send()
                _remote(got, got, send, recv, 3 * (n + i) + j, sib).wait_send()

    return _Rider(views, [jax.ShapeDtypeStruct(v.shape, v.dtype) for v in views], {w: w for w in range(len(views))},
                  6 * n, _by_chip(first), _by_chip(mid), _by_chip(last))


def _swap_rider(views, axes, items):
    nv = len(views)

    def part(ref, w, l, core):
        return ref.at[l, :, core] if axes[w] == 0 else ref.at[l, core]

    def copies(ins, outs, send, recv):
        x, y, c = _place()
        return [_remote(part(ins[w], w, l, 1 - c), outs[nv + i], send, recv, i, (x, y, 1 - c))
                for i, (w, l) in enumerate(items)]

    def first(ins, outs, send, recv):
        for cp in copies(ins, outs, send, recv):
            cp.start()

    def last(ins, outs, send, recv):
        for cp in copies(ins, outs, send, recv):
            cp.wait()

    got = [jax.ShapeDtypeStruct(views[w].shape[1:2] + views[w].shape[3:] if axes[w] == 0 else views[w].shape[2:],
                                views[w].dtype) for w, _ in items]
    return _Rider(views, [jax.ShapeDtypeStruct(v.shape, v.dtype) for v in views] + got,
                  {w: w for w in range(nv)}, len(items), first, None, last)


def _add_core(name, view, got, axis, layer, cidx, tr):
    def body(c_ref, g_ref, r_ref, o_ref):
        o_ref[...] = (g_ref[...] + r_ref[...]).astype(o_ref.dtype)

    if axis == 0:
        _, nchip, _, pr, cdim = view.shape
        grid = (nchip, pr // tr)
        specs = [pl.BlockSpec((None, None, None, tr, cdim), lambda k, i, c_ref: (layer, k, c_ref[0], i, 0)),
                 pl.BlockSpec((None, tr, cdim), lambda k, i, c_ref: (k, i, 0))]
        out_spec = pl.BlockSpec((None, tr, cdim), lambda k, i, c_ref: (k, i, 0))
    else:
        _, _, pr, cdim = view.shape
        grid = (pr // tr,)
        specs = [pl.BlockSpec((None, None, tr, cdim), lambda i, c_ref: (layer, c_ref[0], i, 0)),
                 pl.BlockSpec((tr, cdim), lambda i, c_ref: (i, 0))]
        out_spec = pl.BlockSpec((tr, cdim), lambda i, c_ref: (i, 0))
    return pl.pallas_call(
        body, name=name,
        grid_spec=pltpu.PrefetchScalarGridSpec(num_scalar_prefetch=1, grid=grid, in_specs=specs,
                                               out_specs=out_spec),
        out_shape=jax.ShapeDtypeStruct(got.shape, BF16),
        compiler_params=_params(("arbitrary",) * len(grid)),
    )(cidx, view, got)


def _scatter_rider(sums, axes):
    def part(ref, i, chip):
        if axes[i] == 0:
            return ref.at[chip]
        cs = ref.shape[-1] // 4
        return ref.at[:, pl.ds(chip * cs, cs)]

    def copies(ins, outs, send, recv, me, c, sib):
        return [_remote(part(ins[i], i, 2 * px + py), outs[i].at[j], send, recv, 3 * i + j, (px, py, c))
                for i in range(len(sums)) for j, (px, py) in enumerate(_peers(me))]

    def first(*args):
        for cp in copies(*args):
            cp.start()

    def last(*args):
        for cp in copies(*args):
            cp.wait()

    shapes = [jax.ShapeDtypeStruct((3,) + (s.shape[1:] if ax == 0 else (s.shape[0], s.shape[1] // 4)), s.dtype)
              for s, ax in zip(sums, axes)]
    return _Rider(sums, shapes, {}, 3 * len(sums), _by_chip(first), None, _by_chip(last))


def _add_chips(name, own, got, axis, layer, kc_idx, tr, into, shard_shape):
    _, pr, pc = got.shape

    def body(k_ref, o_ref, g_ref, *rest):
        rest[-1][...] = (o_ref[...].astype(F32) + g_ref[0].astype(F32) + g_ref[1].astype(F32)
                         + g_ref[2].astype(F32))

    if axis == 0:
        own_spec = pl.BlockSpec((None, tr, pc), lambda i, k_ref: (k_ref[0], i, 0))
    else:
        own_spec = pl.BlockSpec((tr, pc), lambda i, k_ref: (i, k_ref[0]))
    specs = [own_spec, pl.BlockSpec((3, tr, pc), lambda i, k_ref: (0, i, 0))]
    args = [kc_idx, own, got]
    if into is not None:
        specs.append(pl.BlockSpec(memory_space=pl.ANY))
        args.append(into)
    return pl.pallas_call(
        body, name=name,
        grid_spec=pltpu.PrefetchScalarGridSpec(
            num_scalar_prefetch=1, grid=(pr // tr,), in_specs=specs,
            out_specs=pl.BlockSpec((None, None, tr, pc), lambda i, k_ref: (layer, k_ref[1], i, 0))),
        out_shape=jax.ShapeDtypeStruct(shard_shape, F32),
        input_output_aliases={} if into is None else {3: 0},
        compiler_params=_params(("arbitrary",)),
    )(*args)


def _join_rider(parts):
    def first(ins, outs, send, recv):
        x, y, c = _place()
        for w in range(len(parts)):
            _remote(ins[w].at[:, c], outs[w].at[:, c], send, recv, w, (x, y, 1 - c)).start()

    def last(ins, outs, send, recv):
        x, y, c = _place()
        for w in range(len(parts)):
            _remote(ins[w].at[:, c], outs[w].at[:, c], send, recv, w, (x, y, 1 - c)).wait_send()
            _remote(ins[w].at[:, c], outs[w].at[:, 1 - c], send, recv, w, (x, y, 1 - c)).wait_recv()

    return _Rider(parts, [jax.ShapeDtypeStruct(p.shape, p.dtype) for p in parts],
                  {w: w for w in range(len(parts))}, len(parts), first, None, last)


def _all_reduce_small(name, pack, lead, groups):
    nr, d = pack.shape
    nout = nr - (groups - 1) * lead

    def body(in_ref, sum_ref, mine, slots, send, recv):
        x, y, c = _place()
        me = 4 * x + 2 * y + c
        fold = in_ref[0:lead]
        for grp in range(1, groups):
            fold = fold + in_ref[grp * lead:(grp + 1) * lead]
        mine[0:lead] = fold
        mine[lead:] = in_ref[groups * lead:]
        slots[me] = mine[...]
        cps = []
        for r in range(1, 8):
            rx, ry, rc = r // 4, (r // 2) % 2, r % 2
            peer = (x + rx - 2 * x * rx, y + ry - 2 * y * ry, c + rc - 2 * c * rc)
            cp = pltpu.make_async_remote_copy(mine, slots.at[me], send.at[r - 1], recv.at[r - 1],
                                              device_id=peer, device_id_type=MESH)
            cp.start()
            cps.append(cp)
        for cp in cps:
            cp.wait()
        acc = slots[0]
        for dev in range(1, 8):
            acc = acc + slots[dev]
        sum_ref[...] = acc

    vmem = pl.BlockSpec(memory_space=pltpu.VMEM)
    return pl.pallas_call(
        body, name=name, in_specs=[vmem], out_specs=vmem, out_shape=jax.ShapeDtypeStruct((nout, d), F32),
        scratch_shapes=[pltpu.VMEM((nout, d), F32), pltpu.VMEM((8, nout, d), F32), pltpu.SemaphoreType.DMA((7,)),
                        pltpu.SemaphoreType.DMA((7,))],
        compiler_params=pltpu.CompilerParams(has_side_effects=True, vmem_limit_bytes=VMEM_LIMIT),
    )(pack)


def _adamw_math(w, g, m, v):
    m = B1 * m + (1.0 - B1) * g
    v = B2 * v + (1.0 - B2) * (g * g)
    m_hat = m / (1.0 - B1 ** STEP)
    v_hat = v / (1.0 - B2 ** STEP)
    return -LR * (m_hat / (jnp.sqrt(v_hat) + ADAM_EPS) + WD * w), m, v


def _adamw(name, w, g, m, v, tr):
    shape = w.shape
    flat = [a.reshape(-1, shape[-1]) for a in (w, g, m, v)]
    r, cdim = flat[0].shape

    def body(w_ref, g_ref, m_ref, v_ref, d_ref, nm_ref, nv_ref):
        d_ref[...], nm_ref[...], nv_ref[...] = _adamw_math(w_ref[...], g_ref[...], m_ref[...], v_ref[...])

    spec = pl.BlockSpec((tr, cdim), lambda i: (i, 0))
    outs = pl.pallas_call(
        body, name=name, grid=(r // tr,), in_specs=[spec] * 4, out_specs=[spec] * 3,
        out_shape=[jax.ShapeDtypeStruct((r, cdim), F32)] * 3,
        compiler_params=_params(("parallel",)),
    )(*flat)
    return [o.reshape(shape) for o in outs]


def _adamw_small(name, groups):
    n = len(groups)
    shapes = [grp[0].shape for grp in groups]
    flat = [a.reshape(-1, a.shape[-1]) for grp in groups for a in grp]

    def body(*refs):
        ins, outs = refs[:4 * n], refs[4 * n:]
        for i in range(n):
            w_ref, g_ref, m_ref, v_ref = ins[4 * i:4 * i + 4]
            outs[3 * i][...], outs[3 * i + 1][...], outs[3 * i + 2][...] = _adamw_math(
                w_ref[...], g_ref[...], m_ref[...], v_ref[...])

    vmem = pl.BlockSpec(memory_space=pltpu.VMEM)
    out_shape = [jax.ShapeDtypeStruct(flat[4 * i].shape, F32) for i in range(n) for _ in range(3)]
    outs = pl.pallas_call(body, name=name, in_specs=[vmem] * (4 * n), out_specs=[vmem] * (3 * n),
                          out_shape=out_shape)(*flat)
    return [[outs[3 * i + j].reshape(shapes[i]) for j in range(3)] for i in range(n)]


def _block_diag(w_grp):
    g, pg, _ = w_grp.shape
    eye = jnp.eye(g, dtype=w_grp.dtype)
    return (eye[:, None, :, None] * w_grp[:, :, None, :]).reshape(g * pg, g * pg)


def _diag_blocks(m, g):
    pg = m.shape[0] // g
    return jnp.stack([m[i * pg:(i + 1) * pg, i * pg:(i + 1) * pg] for i in range(g)])


BIG = ("w_in", "w_out", "w_up", "w_down")
AXES = (1, 0, 1, 0)
W_IN, W_OUT, W_UP, W_DOWN = range(4)


def kernel(x, meta_tokens, g_mix, w_in, w_conv, w_pool, pool_scale, w_out, g_mlp, w_up, w_down, g_final, loss_target, m_meta_tokens, m_g_mix, m_w_in, m_w_conv, m_w_pool, m_pool_scale, m_w_out, m_g_mlp, m_w_up, m_w_down, m_g_final, v_meta_tokens, v_g_mix, v_w_in, v_w_conv, v_w_pool, v_pool_scale, v_w_out, v_g_mlp, v_w_up, v_w_down, v_g_final):
    bl, s, d = x.shape
    depth = g_mix.shape[0]
    assert depth == 2
    lp = PAD + N_META + s
    t = bl * lp
    tt = 512 if t % 512 == 0 else lp // 2
    tm = lp // 4
    cs = w_conv.shape[2]
    cw = 4 * cs
    ngrp = w_pool.shape[1]
    xi, yi, ci = _place()
    chip = (2 * xi + yi).astype(jnp.int32)
    cidx, kidx = ci.astype(jnp.int32).reshape(1), chip.reshape(1)
    kc_idx = jnp.stack([chip, ci.astype(jnp.int32)])
    shards = (w_in, w_out, w_up, w_down)

    views = [_core_view(_place_shard(f"place_{BIG[w]}", shards[w], AXES[w], kidx, 256), AXES[w]) for w in range(4)]

    def whole(w):
        return views[w].reshape(depth, -1, views[w].shape[-1])

    def gather_on(call, items):
        ws = sorted({w for w, _ in items})
        res = call(_gather_rider([views[w] for w in ws], [AXES[w] for w in ws],
                                 [(ws.index(w), layer) for w, layer in items]))
        for j, w in enumerate(ws):
            views[w] = res[len(res) - len(ws) + j]
        return res[:len(res) - len(ws)]

    gather_on(lambda rider: _run_rider("gather_first", rider), [(W_IN, 0)])

    placed = jnp.zeros((32, d), F32)
    placed = lax.dynamic_update_slice(placed, meta_tokens, (0, chip * meta_tokens.shape[1]))
    placed = lax.dynamic_update_slice(placed, w_conv.reshape(-1, cs), (N_META, chip * cs))
    placed = jnp.where(ci == 0, placed, 0.0)
    small = _all_reduce_small("gather_small", placed, 8, 1)
    meta_full = small[:N_META]
    conv_full = small[N_META:N_META + depth * 3, :cw].reshape(depth, 3, cw)

    h = jnp.concatenate([jnp.zeros((bl, PAD, d), F32), jnp.broadcast_to(meta_full[None], (bl, N_META, d)), x],
                        axis=1).reshape(t, d)
    wbd = [_block_diag(w_pool[i]).astype(BF16) for i in range(depth)]
    saved = []
    for i in range(depth):
        hn, u_cp, qkv = _in_proj(f"in_proj{i}", h, g_mix[i], whole(W_IN), i, tm, 4 * cw)
        y_cp = _convpool_fwd(f"convpool{i}", u_cp, conv_full[i], wbd[i], pool_scale[i:i + 1], lp, tm)
        if i == 0:
            y_at, lt, g0 = gather_on(lambda rider: _attn_fwd(f"attn{i}", qkv, bl, lp, rider),
                                     [(W_OUT, 0), (W_UP, 0), (W_DOWN, 0), (W_IN, 1)])
            h_mid = _out_proj(f"out_proj{i}", y_cp, y_at, h, whole(W_OUT), i, tm)
            w_up0 = whole(W_UP)
            hn2, m_pre, act = gather_on(lambda rider: _up_proj(f"up_proj{i}", h_mid, g_mlp[i], w_up0, i, tm, rider),
                                        [(W_OUT, 1), (W_DOWN, 1)])
            w_down0 = whole(W_DOWN)
            (h_next,) = gather_on(lambda rider: _down_proj(f"down_proj{i}", act, h_mid, w_down0, i, tm, rider),
                                  [(W_UP, 1)])
        else:
            y_at, lt, g0 = _attn_fwd(f"attn{i}", qkv, bl, lp)
            h_mid = _out_proj(f"out_proj{i}", y_cp, y_at, h, whole(W_OUT), i, tm)
            hn2, m_pre, act = _up_proj(f"up_proj{i}", h_mid, g_mlp[i], whole(W_UP), i, tm)
            (h_next,) = _down_proj(f"down_proj{i}", act, h_mid, whole(W_DOWN), i, tm)
        saved.append((h, hn, u_cp, qkv, y_cp, y_at, (lt, g0), h_mid, hn2, m_pre, act))
        h = h_next

    dh, loss8, dgf8 = _loss_bwd("loss", h, g_final, loss_target, lp)
    loss = lax.psum(jnp.sum(loss8), ("x", "y", "c"))
    per_layer = {k: [None] * depth for k in ("g_mix", "w_conv", "w_pool", "pool_scale", "g_mlp")}

    gw = [None] * 4
    sums, arrived = {}, {}

    def dw(name, a, b, w, layer, tka, tn, row_off=0, col_off=0):
        shape = whole(w).shape
        into = None if gw[w] is None else gw[w].reshape(shape)
        gw[w] = _core_view(_mm_tn(name, a, b, tt, tka, tn, into, shape, layer, row_off, col_off), AXES[w])

    def swap_rider(ws):
        return _swap_rider([gw[w] for w, _ in ws], [AXES[w] for w, _ in ws],
                           [(j, layer) for j, (_, layer) in enumerate(ws)])

    def swapped(ws, outs):
        for j, (w, layer) in enumerate(ws):
            gw[w] = outs[j]
            sums[w, layer] = _add_core(f"chip_sum_{BIG[w]}{layer}", gw[w], outs[len(ws) + j], AXES[w], layer, cidx,
                                       128)

    def scatter_rider(items):
        return _scatter_rider([sums[it] for it in items], [AXES[w] for w, _ in items])

    def bwd_mlp(i, dh):
        _, _, _, _, y_cp, y_at, _, h_mid, hn2, m_pre, act = saved[i]
        dm = _down_proj_dx(f"down_proj_dx{i}", dh, m_pre, whole(W_DOWN), i, tm)
        dw(f"down_proj_dw{i}", act, dh, W_DOWN, i, 1024, 1024)
        dw(f"up_proj_dw{i}", hn2, dm, W_UP, i, 1024, 1024)
        dh_mid, dy, dg8 = _up_proj_dx(f"up_proj_dx{i}", dm, h_mid, dh, g_mlp[i], whole(W_UP), whole(W_OUT), i, tm)
        per_layer["g_mlp"][i] = dg8.sum(0)
        dw(f"out_proj_dw_cp{i}", y_cp, dh_mid, W_OUT, i, 512, 1024)
        dw(f"out_proj_dw_at{i}", y_at, dh_mid, W_OUT, i, 512, 1024, row_off=y_cp.shape[1])
        return dh_mid, dy

    def bwd_mix(i, dh_mid, dy, dus3):
        h_in, hn, u_cp = saved[i][:3]
        du_cp, sm, dwbd = _convpool_bwd(f"convpool_bwd{i}", u_cp, dy, conv_full[i], wbd[i], pool_scale[i:i + 1], lp,
                                        tm)
        sm = sm.reshape(4, 8, cw).sum(1)
        per_layer["w_conv"][i] = sm[0:3]
        per_layer["pool_scale"][i] = sm[3]
        per_layer["w_pool"][i] = _diag_blocks(dwbd, ngrp)
        dus, off = [du_cp, *dus3], 0
        for j, du in enumerate(dus):
            dw(f"in_proj_dw{j}_{i}", hn, du, W_IN, i, 1024, du.shape[1], col_off=off)
            off += du.shape[1]
        dh, dg8 = _in_proj_dx(f"in_proj_dx{i}", dus, h_in, dh_mid, g_mix[i], whole(W_IN), i, tm)
        per_layer["g_mix"][i] = dg8.sum(0)
        return dh

    def attn_bwd(i, dy, rider):
        qkv, (lt, g0) = saved[i][3], saved[i][6]
        res = _attn_bwd(f"attn_bwd{i}", qkv, lt, g0, dy, bl, lp, rider)
        return res[:3], res[3:]

    mlp_ws = [W_DOWN, W_UP, W_OUT]
    dh_mid, dy = bwd_mlp(1, dh)
    ws = [(w, 1) for w in mlp_ws]
    dus3, outs = attn_bwd(1, dy, swap_rider(ws))
    swapped(ws, outs)
    dh = bwd_mix(1, dh_mid, dy, dus3)
    swapped([(W_IN, 1)], _run_rider("grads_swap_in1", swap_rider([(W_IN, 1)])))

    dh_mid, dy = bwd_mlp(0, dh)
    ws = [(w, 0) for w in mlp_ws]
    swapped(ws, _run_rider("grads_swap0", swap_rider(ws)))
    items = list(sums)
    dus3, outs = attn_bwd(0, dy, scatter_rider(items))
    arrived.update(zip(items, outs))
    dh0 = bwd_mix(0, dh_mid, dy, dus3)
    swapped([(W_IN, 0)], _run_rider("grads_swap_in0", swap_rider([(W_IN, 0)])))
    arrived[W_IN, 0] = _run_rider("grads_scatter_in0", scatter_rider([(W_IN, 0)]))[0]

    finals = []
    for w in range(4):
        rs_, cs_ = shards[w].shape[1:]
        part = None
        for layer in reversed(range(depth)):
            part = _add_chips(f"reduce_{BIG[w]}{layer}", sums[w, layer], arrived[w, layer], AXES[w], layer, kc_idx,
                              128, part, (depth, 2, rs_ // 2, cs_))
        finals.append(part)
    finals = _run_rider("grads_join", _join_rider(finals))
    grad = {BIG[w]: finals[w].reshape(shards[w].shape) for w in range(4)}

    dh0 = dh0.reshape(bl, lp, d)
    grad_x = dh0[:, PAD + N_META:]
    local = {k: jnp.stack(v) for k, v in per_layer.items()}
    pieces = [dh0[:, PAD:PAD + N_META].reshape(bl * N_META, d), local["g_mix"], local["g_mlp"],
              dgf8.sum(0).reshape(1, d),
              jnp.pad(local["w_conv"].reshape(-1), (0, 2 * d - local["w_conv"].size)).reshape(2, d),
              jnp.pad(local["pool_scale"].reshape(-1), (0, d - local["pool_scale"].size)).reshape(1, d),
              local["w_pool"].reshape(-1, d)]
    summed = _all_reduce_small("small_grads", jnp.concatenate(pieces, axis=0), N_META, bl)
    o = N_META
    grad.update({
        "meta_tokens": lax.dynamic_slice_in_dim(summed[:o], chip * meta_tokens.shape[1], meta_tokens.shape[1], 1),
        "g_mix": summed[o:o + 2], "g_mlp": summed[o + 2:o + 4], "g_final": summed[o + 4],
        "w_conv": lax.dynamic_slice_in_dim(summed[o + 5:o + 7].reshape(-1)[:2 * 3 * cw].reshape(2, 3, cw),
                                           chip * cs, cs, 2),
        "pool_scale": summed[o + 7].reshape(-1)[:pool_scale.size].reshape(pool_scale.shape),
        "w_pool": summed[o + 8:].reshape(w_pool.shape),
    })

    weights = dict(meta_tokens=meta_tokens, g_mix=g_mix, w_in=w_in, w_conv=w_conv, w_pool=w_pool,
                   pool_scale=pool_scale, w_out=w_out, g_mlp=g_mlp, w_up=w_up, w_down=w_down, g_final=g_final)
    ms = dict(meta_tokens=m_meta_tokens, g_mix=m_g_mix, w_in=m_w_in, w_conv=m_w_conv, w_pool=m_w_pool,
              pool_scale=m_pool_scale, w_out=m_w_out, g_mlp=m_g_mlp, w_up=m_w_up, w_down=m_w_down,
              g_final=m_g_final)
    vs = dict(meta_tokens=v_meta_tokens, g_mix=v_g_mix, w_in=v_w_in, w_conv=v_w_conv, w_pool=v_w_pool,
              pool_scale=v_pool_scale, w_out=v_w_out, g_mlp=v_g_mlp, w_up=v_w_up, w_down=v_w_down,
              g_final=v_g_final)
    order = list(weights)
    upd = {k: _adamw(f"adamw_{k}", weights[k], grad[k], ms[k], vs[k], 256) for k in BIG}
    little = [k for k in order if k not in BIG]
    for k, res in zip(little, _adamw_small("adamw_small", [(weights[k], grad[k].reshape(weights[k].shape), ms[k],
                                                            vs[k]) for k in little])):
        upd[k] = res
    grad = {k: grad[k].reshape(weights[k].shape) for k in order}
    return (loss, grad_x, *[grad[k] for k in order], *[upd[k][0] for k in order], *[upd[k][1] for k in order],
            *[upd[k][2] for k in order])
```

```python
import functools

import jax
import jax.numpy as jnp
from jax import lax
from jax.experimental import pallas as pl
from jax.experimental.pallas import tpu as pltpu

F32, BF16 = jnp.float32, jnp.bfloat16
MESH = pl.DeviceIdType.MESH
EPS = 1e-6
N_META = 16
QB = 128
PAD = QB - N_META
HALO = 16
POOL_WINDOWS = (2.0, 4.0, 8.0, 16.0)
HEAD_SCALE = 0.125
LR, B1, B2, ADAM_EPS, WD, STEP = 0.001, 0.9, 0.999, 1e-08, 0.01, 10
VMEM_LIMIT = 56 * 1024 * 1024


def _params(sem=None):
    return pltpu.CompilerParams(dimension_semantics=sem, vmem_limit_bytes=VMEM_LIMIT)


def _nt(a, b):
    return lax.dot_general(a, b, (((1,), (1,)), ((), ())), preferred_element_type=F32)


def _tn(a, b):
    return lax.dot_general(a, b, (((0,), (0,)), ((), ())), preferred_element_type=F32)


def _nn(a, b):
    return jnp.dot(a, b, preferred_element_type=F32)


def _fold8(v):
    r, c = v.shape
    return jnp.sum(v.reshape(r // 8, 8, c), axis=0)


NCH = 512


def _rows_call(name, body, tm, row_ins, consts, row_outs, accs=(), rider=None):
    t = row_ins[0].shape[0]
    ride = _Ride(rider, len(row_ins) + len(consts), len(row_outs) + len(accs), t // tm)

    def stepped(*refs):
        step = pl.program_id(0)
        ride.before(refs, step)
        body(*ride.own(refs))
        ride.after(refs, step)

    in_specs = [pl.BlockSpec((tm, a.shape[1]), lambda i: (i, 0)) for a in row_ins]
    for a, layer in consts:
        if layer is None:
            in_specs.append(pl.BlockSpec(a.shape, lambda i: (0, 0)))
        else:
            in_specs.append(pl.BlockSpec((None, *a.shape[1:]), lambda i, l=layer: (l, 0, 0)))
    return ride.call(
        stepped, name, (t // tm,), in_specs, [*row_ins, *[a for a, _ in consts]],
        [pl.BlockSpec((tm, c), lambda i: (i, 0)) for c, _ in row_outs]
        + [pl.BlockSpec(s, lambda i: (0, 0)) for s in accs],
        [jax.ShapeDtypeStruct((t, c), dt) for c, dt in row_outs] + [jax.ShapeDtypeStruct(s, F32) for s in accs],
        [], ("arbitrary",) if accs else ("parallel",))


def _norm_parts(x):
    r = lax.rsqrt(jnp.mean(x * x, axis=-1, keepdims=True) + EPS)
    return r, x * r


def _norm_bwd(r, xh, dyn, g):
    w = dyn * g
    return r * (w - xh * jnp.mean(w * xh, axis=-1, keepdims=True))


def _in_proj(name, h, g, w, layer, tm, ncp):
    d, n = h.shape[1], w.shape[2]

    def body(h_ref, g_ref, w_ref, hn_ref, ucp_ref, qkv_ref):
        _, xh = _norm_parts(h_ref[...])
        hn = (xh * g_ref[...]).astype(BF16)
        hn_ref[...] = hn
        for n0 in range(0, n, NCH):
            acc = _nn(hn, w_ref[:, n0:n0 + NCH])
            if n0 < ncp:
                ucp_ref[:, n0:n0 + NCH] = acc
            else:
                qkv_ref[:, n0 - ncp:n0 - ncp + NCH] = acc.astype(BF16)

    return _rows_call(name, body, tm, [h], [(g.reshape(1, d), None), (w, layer)],
                      [(d, BF16), (ncp, F32), (n - ncp, BF16)])


def _out_proj(name, y_cp, y_at, h, w, layer, tm):
    d, k1 = h.shape[1], y_cp.shape[1]

    def body(ycp_ref, yat_ref, h_ref, w_ref, o_ref):
        for n0 in range(0, d, NCH):
            o_ref[:, n0:n0 + NCH] = (h_ref[:, n0:n0 + NCH] + _nn(ycp_ref[...], w_ref[0:k1, n0:n0 + NCH])
                                     + _nn(yat_ref[...], w_ref[k1:, n0:n0 + NCH]))

    return _rows_call(name, body, tm, [y_cp, y_at, h], [(w, layer)], [(d, F32)])[0]


def _up_proj(name, h_mid, g, w, layer, tm, rider=None):
    d, n = h_mid.shape[1], w.shape[2]

    def body(h_ref, g_ref, w_ref, hn_ref, m_ref, act_ref):
        _, xh = _norm_parts(h_ref[...])
        hn = (xh * g_ref[...]).astype(BF16)
        hn_ref[...] = hn
        for n0 in range(0, n, NCH):
            acc = _nn(hn, w_ref[:, n0:n0 + NCH])
            m_ref[:, n0:n0 + NCH] = acc.astype(BF16)
            act_ref[:, n0:n0 + NCH] = jnp.square(jnp.maximum(acc, 0.0)).astype(BF16)

    return _rows_call(name, body, tm, [h_mid], [(g.reshape(1, d), None), (w, layer)],
                      [(d, BF16), (n, BF16), (n, BF16)], rider=rider)


def _down_proj(name, act, h_mid, w, layer, tm, rider=None):
    d = h_mid.shape[1]

    def body(a_ref, h_ref, w_ref, o_ref):
        for n0 in range(0, d, NCH):
            o_ref[:, n0:n0 + NCH] = h_ref[:, n0:n0 + NCH] + _nn(a_ref[...], w_ref[:, n0:n0 + NCH])

    return _rows_call(name, body, tm, [act, h_mid], [(w, layer)], [(d, F32)], rider=rider)


def _down_proj_dx(name, dh, m_pre, w, layer, tm):
    n = w.shape[1]

    def body(dh_ref, m_ref, w_ref, dm_ref):
        dhb = dh_ref[...].astype(BF16)
        for n0 in range(0, n, NCH):
            dm_ref[:, n0:n0 + NCH] = (_nt(dhb, w_ref[n0:n0 + NCH, :])
                                      * (2.0 * jnp.maximum(m_ref[:, n0:n0 + NCH].astype(F32), 0.0))).astype(BF16)

    return _rows_call(name, body, tm, [dh, m_pre], [(w, layer)], [(n, BF16)])[0]


def _up_proj_dx(name, dm, h_mid, dh, g, w_up, w_out, layer, tm):
    d = h_mid.shape[1]

    def body(dm_ref, h_ref, dh_ref, g_ref, wup_ref, wout_ref, dhm_ref, dy_ref, dg_ref):
        @pl.when(pl.program_id(0) == 0)
        def _():
            dg_ref[...] = jnp.zeros_like(dg_ref)
        dyn = _nt(dm_ref[...], wup_ref[...])
        r, xh = _norm_parts(h_ref[...])
        dhm = dh_ref[...] + _norm_bwd(r, xh, dyn, g_ref[...])
        dhm_ref[...] = dhm
        dg_ref[...] += _fold8(dyn * xh)
        dy_ref[...] = _nt(dhm.astype(BF16), wout_ref[...])

    return _rows_call(name, body, tm, [dm, h_mid, dh], [(g.reshape(1, d), None), (w_up, layer), (w_out, layer)],
                      [(d, F32), (w_out.shape[1], F32)], [(8, d)])


def _in_proj_dx(name, dus, h, dh_mid, g, w, layer, tm):
    d = h.shape[1]
    ns = [du.shape[1] for du in dus]
    nd = len(dus)

    def body(*refs):
        du_refs = refs[:nd]
        h_ref, dhm_ref, g_ref, w_ref, dh_ref, dg_ref = refs[nd:]

        @pl.when(pl.program_id(0) == 0)
        def _():
            dg_ref[...] = jnp.zeros_like(dg_ref)
        dyn, off = None, 0
        for du_ref, n in zip(du_refs, ns):
            part = _nt(du_ref[...], w_ref[:, off:off + n])
            dyn = part if dyn is None else dyn + part
            off += n
        r, xh = _norm_parts(h_ref[...])
        dh_ref[...] = dhm_ref[...] + _norm_bwd(r, xh, dyn, g_ref[...])
        dg_ref[...] += _fold8(dyn * xh)

    return _rows_call(name, body, tm, [*dus, h, dh_mid], [(g.reshape(1, d), None), (w, layer)], [(d, F32)],
                      [(8, d)])


def _mm_tn(name, a, b, tt, tka, tn, into, shape, layer, row_off, col_off):
    t, ka = a.shape
    n = b.shape[1]
    assert t % tt == 0 and ka % tka == 0 and n % tn == 0 and row_off % tka == 0 and col_off % tn == 0

    def body(a_ref, b_ref, *rest):
        o_ref = rest[-1]

        @pl.when(pl.program_id(2) == 0)
        def _():
            o_ref[...] = jnp.zeros_like(o_ref)
        o_ref[...] += _tn(a_ref[...].astype(BF16), b_ref[...].astype(BF16))

    in_specs = [pl.BlockSpec((tt, tka), lambda i, j, s: (s, i)), pl.BlockSpec((tt, tn), lambda i, j, s: (s, j))]
    args = [a, b]
    if into is not None:
        in_specs.append(pl.BlockSpec(memory_space=pl.ANY))
        args.append(into)
    return pl.pallas_call(
        body, name=name, grid=(ka // tka, n // tn, t // tt), in_specs=in_specs,
        out_specs=pl.BlockSpec((None, tka, tn), lambda i, j, s: (layer, row_off // tka + i, col_off // tn + j)),
        out_shape=jax.ShapeDtypeStruct(shape, F32),
        input_output_aliases={} if into is None else {2: 0},
        compiler_params=_params(("parallel", "parallel", "arbitrary")),
    )(*args)


def _loss_bwd(name, h, g, target, lp):
    t, d = h.shape
    bl = target.shape[0]
    nq = lp // QB

    def body(h_ref, g_ref, t_ref, dh_ref, ls_ref, dg_ref):
        b, j = pl.program_id(0), pl.program_id(1)

        @pl.when((b == 0) & (j == 0))
        def _():
            ls_ref[...] = jnp.zeros_like(ls_ref)
            dg_ref[...] = jnp.zeros_like(dg_ref)
        xv = h_ref[...]
        r = lax.rsqrt(jnp.mean(xv * xv, axis=-1, keepdims=True) + EPS)
        xh = xv * r
        gv = g_ref[...]
        err = jnp.where(j >= 1, xh * gv - t_ref[...], 0.0)
        ls_ref[...] += _fold8(err * err) * (0.5 / d)
        dy = err * (1.0 / d)
        w = dy * gv
        dh_ref[...] = r * (w - xh * jnp.mean(w * xh, axis=-1, keepdims=True))
        dg_ref[...] += _fold8(dy * xh)

    return pl.pallas_call(
        body, name=name, grid=(bl, nq),
        in_specs=[pl.BlockSpec((QB, d), lambda b, j: (b * nq + j, 0)), pl.BlockSpec((1, d), lambda b, j: (0, 0)),
                  pl.BlockSpec((None, QB, d), lambda b, j: (b, jnp.maximum(j - 1, 0), 0))],
        out_specs=[pl.BlockSpec((QB, d), lambda b, j: (b * nq + j, 0)), pl.BlockSpec((8, d), lambda b, j: (0, 0)),
                   pl.BlockSpec((8, d), lambda b, j: (0, 0))],
        out_shape=[jax.ShapeDtypeStruct((t, d), F32), jax.ShapeDtypeStruct((8, d), F32),
                   jax.ShapeDtypeStruct((8, d), F32)],
        compiler_params=_params(("arbitrary", "arbitrary")),
    )(h, g.reshape(1, d), target)


def _pool_select(grp, a2, a4, a8, a16):
    return jnp.where(grp == 0, a2, jnp.where(grp == 1, a4, jnp.where(grp == 2, a8, a16)))


def _trailing_sums(v):
    s2 = v + pltpu.roll(v, 1, 0)
    s4 = s2 + pltpu.roll(s2, 2, 0)
    s8 = s4 + pltpu.roll(s4, 4, 0)
    s16 = s8 + pltpu.roll(s8, 8, 0)
    return s2, s4, s8, s16


def _leading_sums(v):
    n = v.shape[0]
    s2 = v + pltpu.roll(v, n - 1, 0)
    s4 = s2 + pltpu.roll(s2, n - 2, 0)
    s8 = s4 + pltpu.roll(s4, n - 4, 0)
    s16 = s8 + pltpu.roll(s8, n - 8, 0)
    return s2, s4, s8, s16


def _convpool_fwd(name, u_cp, wconv, wbd, pscale, lp, r):
    t = u_cp.shape[0]
    cw = u_cp.shape[1] // 4
    tps, hb = lp // r, r // HALO

    def body(cb_ref, cc_ref, cx_ref, pi_ref, cch_ref, cxh_ref, pih_ref, wc_ref, wbd_ref, ps_ref, y_ref):
        i = pl.program_id(0)
        lrow = (i % tps) * r + lax.broadcasted_iota(jnp.int32, (r, 1), 0)
        valid = lrow >= PAD
        xx = jnp.concatenate([cch_ref[...] * cxh_ref[...], cc_ref[...] * cx_ref[...]], axis=0)
        conv = (wc_ref[0:1, :] * pltpu.roll(xx, 2, 0) + wc_ref[1:2, :] * pltpu.roll(xx, 1, 0)
                + wc_ref[2:3, :] * xx)
        y_ref[:, 0:cw] = (cb_ref[...] * conv[HALO:]).astype(y_ref.dtype)
        p = pi_ref[...]
        grp = lax.broadcasted_iota(jnp.int32, (1, cw), 1) // (cw // 4)
        sel = _pool_select(grp, *_trailing_sums(jnp.concatenate([pih_ref[...], p], axis=0)))[HALO:]
        cnt = jnp.maximum(jnp.minimum((lrow - (PAD - 1)).astype(F32), _pool_select(grp, *POOL_WINDOWS)), 1.0)
        pooled = jnp.where(valid, sel / cnt - p, 0.0)
        y_ref[:, cw:2 * cw] = (_nn(pooled.astype(BF16), wbd_ref[...]) * ps_ref[...]).astype(y_ref.dtype)

    def main(col):
        return pl.BlockSpec((r, cw), lambda i: (i, col))

    def prev(col):
        return pl.BlockSpec((HALO, cw), lambda i: (jnp.maximum(i * hb - 1, 0), col))

    def whole(a):
        return pl.BlockSpec(a.shape, lambda i: (0, 0))

    return pl.pallas_call(
        body, name=name, grid=(t // r,),
        in_specs=[main(0), main(1), main(2), main(3), prev(1), prev(2), prev(3), whole(wconv), whole(wbd),
                  whole(pscale)],
        out_specs=pl.BlockSpec((r, 2 * cw), lambda i: (i, 0)),
        out_shape=jax.ShapeDtypeStruct((t, 2 * cw), BF16),
        compiler_params=_params(("parallel",)),
    )(u_cp, u_cp, u_cp, u_cp, u_cp, u_cp, u_cp, wconv, wbd, pscale)


def _convpool_bwd(name, u_cp, dy, wconv, wbd, pscale, lp, r):
    t = u_cp.shape[0]
    cw = u_cp.shape[1] // 4
    tps, hb = lp // r, r // HALO
    e = r + HALO

    def body(cb_ref, cc_ref, cx_ref, pi_ref, cbn_ref, cch_ref, cxh_ref, pih_ref, dyc_ref, dyp_ref, dycn_ref,
             dypn_ref, wc_ref, wbd_ref, ps_ref, du_ref, sm_ref, dwbd_ref):
        i = pl.program_id(0)

        @pl.when(i == 0)
        def _():
            sm_ref[...] = jnp.zeros_like(sm_ref)
            dwbd_ref[...] = jnp.zeros_like(dwbd_ref)
        lrow_e = (i % tps) * r + lax.broadcasted_iota(jnp.int32, (e, 1), 0)
        valid_e = (lrow_e >= PAD) & (lrow_e < lp)
        lrow, valid = lrow_e[:r], lrow_e[:r] >= PAD
        w0, w1, w2 = wc_ref[0:1, :], wc_ref[1:2, :], wc_ref[2:3, :]
        cb, cc, cx = cb_ref[...], cc_ref[...], cx_ref[...]
        prod = cc * cx
        xx = jnp.concatenate([cch_ref[...] * cxh_ref[...], prod], axis=0)
        back1, back2 = pltpu.roll(xx, 1, 0)[HALO:], pltpu.roll(xx, 2, 0)[HALO:]
        dyc = dyc_ref[...]
        du_ref[:, 0:cw] = (dyc * (w0 * back2 + w1 * back1 + w2 * prod)).astype(du_ref.dtype)
        dconv_e = jnp.where(valid_e, jnp.concatenate([dyc * cb, dycn_ref[...] * cbn_ref[...]], axis=0), 0.0)
        dconv = dconv_e[:r]
        dprod = (w2 * dconv + w1 * pltpu.roll(dconv_e, e - 1, 0)[:r] + w0 * pltpu.roll(dconv_e, e - 2, 0)[:r])
        du_ref[:, cw:2 * cw] = (dprod * cx).astype(du_ref.dtype)
        du_ref[:, 2 * cw:3 * cw] = (dprod * cc).astype(du_ref.dtype)
        sm_ref[0:8, :] += _fold8(dconv * back2)
        sm_ref[8:16, :] += _fold8(dconv * back1)
        sm_ref[16:24, :] += _fold8(dconv * prod)
        p = pi_ref[...]
        grp = lax.broadcasted_iota(jnp.int32, (1, cw), 1) // (cw // 4)
        win = _pool_select(grp, *POOL_WINDOWS)
        sel = _pool_select(grp, *_trailing_sums(jnp.concatenate([pih_ref[...], p], axis=0)))[HALO:]
        cnt_e = jnp.maximum(jnp.minimum((lrow_e - (PAD - 1)).astype(F32), win), 1.0)
        pooled = jnp.where(valid, sel / cnt_e[:r] - p, 0.0).astype(BF16)
        dyp = dyp_ref[...]
        sm_ref[24:32, :] += _fold8(dyp * _nn(pooled, wbd_ref[...]))
        dpre_e = (jnp.concatenate([dyp, dypn_ref[...]], axis=0) * ps_ref[...]).astype(BF16)
        dwbd_ref[...] += _tn(pooled, dpre_e[:r])
        dpooled_e = jnp.where(valid_e, _nt(dpre_e, wbd_ref[...]), 0.0)
        ahead = _pool_select(grp, *_leading_sums(dpooled_e / cnt_e))[:r]
        du_ref[:, 3 * cw:4 * cw] = (ahead - dpooled_e[:r]).astype(du_ref.dtype)

    last_halo = t // HALO - 1

    def main(col):
        return pl.BlockSpec((r, cw), lambda i: (i, col))

    def prev(col):
        return pl.BlockSpec((HALO, cw), lambda i: (jnp.maximum(i * hb - 1, 0), col))

    def nxt(col):
        return pl.BlockSpec((HALO, cw), lambda i: (jnp.minimum((i + 1) * hb, last_halo), col))

    def whole(a):
        return pl.BlockSpec(a.shape, lambda i: (0, 0))

    return pl.pallas_call(
        body, name=name, grid=(t // r,),
        in_specs=[main(0), main(1), main(2), main(3), nxt(0), prev(1), prev(2), prev(3), main(0), main(1), nxt(0),
                  nxt(1), whole(wconv), whole(wbd), whole(pscale)],
        out_specs=[pl.BlockSpec((r, 4 * cw), lambda i: (i, 0)), pl.BlockSpec((32, cw), lambda i: (0, 0)),
                   pl.BlockSpec((cw, cw), lambda i: (0, 0))],
        out_shape=[jax.ShapeDtypeStruct((t, 4 * cw), BF16), jax.ShapeDtypeStruct((32, cw), F32),
                   jax.ShapeDtypeStruct((cw, cw), F32)],
        compiler_params=_params(("arbitrary",)),
    )(u_cp, u_cp, u_cp, u_cp, u_cp, u_cp, u_cp, u_cp, dy, dy, dy, dy, wconv, wbd, pscale)


KW = 2 * QB
HP = 4
DECAY = 64.0


def _cumsum_matrix(before, kw):
    r = lax.broadcasted_iota(jnp.int32, (kw, kw), 0)
    c = lax.broadcasted_iota(jnp.int32, (kw, kw), 1)
    return ((r < c) if before else (r > c)).astype(BF16)


def _running(v, mat):
    m = v.shape[0]
    hi = v.astype(BF16)
    ext = _nn(jnp.concatenate([hi, (v - hi.astype(F32)).astype(BF16)], axis=0), mat)
    return ext[:m] + ext[m:]


def _log_sigmoid(z):
    neg_abs = lax.bitcast_convert_type(lax.bitcast_convert_type(z, jnp.int32) | jnp.int32(-2 ** 31), F32)
    return jnp.minimum(z, 0.0) - jnp.log(1.0 + jnp.exp(neg_abs))


def _stack_heads(v, head0):
    zero = jnp.zeros_like(v)
    return jnp.concatenate([jnp.where(head0, v, zero), jnp.where(head0, zero, v)], axis=0)


def _lanes(hp):
    return slice(hp * QB, (hp + 1) * QB)


def _attn_fwd(name, qkv, bl, lp, rider=None):
    t = qkv.shape[0]
    nq, nblk = lp // QB, qkv.shape[1] // (3 * HP * QB)
    assert nblk == 1
    ride = _Ride(rider, 3, 3, bl * nblk * nq)

    def body(*refs):
        q_ref, k_ref, v_ref, o_ref, lt_ref, g0_ref = ride.own(refs)
        qi = pl.program_id(2)
        step = (pl.program_id(0) * nblk + pl.program_id(1)) * nq + qi
        ride.before(refs, step)
        head0 = lax.broadcasted_iota(jnp.int32, (QB, QB), 1) < QB // 2
        q2 = [_stack_heads(q_ref[:, _lanes(hp)] * jnp.asarray(HEAD_SCALE, BF16), head0) for hp in range(HP)]
        later = {KW: _cumsum_matrix(False, KW)}
        q_pos = qi * QB + (lax.broadcasted_iota(jnp.int32, (2 * QB, KW), 0) & (QB - 1))
        col = lax.broadcasted_iota(jnp.int32, (2 * QB, KW), 1)
        rest = qi // 2

        def group(start, kw, carry, masked, lo=0):
            start = pl.multiple_of(start, QB)
            if masked:
                k_pos = start + col[:, :kw]
                valid = (k_pos < q_pos[:, :kw]) & (k_pos >= jnp.maximum(lo, PAD))
            z = [_nt(q2[hp], k_ref[pl.ds(start, kw), _lanes(hp)]) for hp in range(HP)]
            logp, after, rs = [], [], []
            for hp in range(HP):
                lp_ = _log_sigmoid(z[hp])
                lk = lp_ - z[hp]
                if masked:
                    lk = jnp.where(valid, lk, 0.0)
                logp.append(lp_)
                rs.append(jnp.sum(lk, axis=1, keepdims=True))
                after.append(_running(lk, later[kw]))
            out = []
            for hp in range(HP):
                run, acc = carry[2 * hp], carry[2 * hp + 1]
                a = jnp.exp(logp[hp] + after[hp] + run)
                if masked:
                    a = jnp.where(valid, a, 0.0)
                out += [run + rs[hp], acc + _nn(a.astype(BF16), v_ref[pl.ds(start, kw), _lanes(hp)])]
            return tuple(out)

        def alive(carry):
            most = carry[0]
            for hp in range(1, HP):
                most = jnp.maximum(most, carry[2 * hp])
            return jnp.max(most) > -DECAY

        carry = (jnp.zeros((2 * QB, 1), F32), jnp.zeros((2 * QB, QB), F32)) * HP
        carry = group(jnp.minimum(rest * KW, lp - KW), KW, carry, True, lo=rest * KW)
        g, *carry = lax.while_loop(lambda st: (st[0] >= 1) & alive(st[1:]),
                                   lambda st: (st[0] - 1, *group(st[0] * KW, KW, tuple(st[1:]), False)),
                                   (rest - 1, *carry))
        oldest = (g == 0) & (rest >= 1) & alive(carry)
        carry = lax.fori_loop(0, oldest.astype(jnp.int32), lambda i, c: group(0, KW, c, True), tuple(carry))
        g0_ref[pl.program_id(0), qi] = jnp.where(oldest, 0, g + 1).astype(F32)
        for hp in range(HP):
            run, acc = carry[2 * hp], carry[2 * hp + 1]
            o_ref[:, _lanes(hp)] = jnp.where(head0, acc[:QB], acc[QB:]).astype(o_ref.dtype)
            lt_ref[:, _lanes(hp)] = jnp.where(head0, run[:QB], run[QB:])
        ride.after(refs, step)

    wb = HP * QB
    blk = pl.BlockSpec((QB, wb), lambda b, p, i: (b * nq + i, p))
    return ride.call(
        body, name, (bl, nblk, nq),
        [blk, pl.BlockSpec((lp, wb), lambda b, p, i: (b, nblk + p)),
         pl.BlockSpec((lp, wb), lambda b, p, i: (b, 2 * nblk + p))], [qkv, qkv, qkv],
        [blk, blk, pl.BlockSpec(memory_space=pltpu.SMEM)],
        [jax.ShapeDtypeStruct((t, nblk * wb), BF16), jax.ShapeDtypeStruct((t, nblk * wb), F32),
         jax.ShapeDtypeStruct((bl, nq), F32)], [])


def _attn_bwd(name, qkv, lt, g0, dy, bl, lp, rider=None):
    t = qkv.shape[0]
    nq, nblk = lp // QB, qkv.shape[1] // (3 * HP * QB)
    ride = _Ride(rider, 6, 3, bl * nblk * nq)

    def body(*refs):
        q_ref, k_ref, v_ref, lt_ref, do_ref, g0_ref, dq_ref, dk_ref, dv_ref, dk_acc, dv_acc = ride.own(refs)
        qi = pl.program_id(2)
        step = (pl.program_id(0) * nblk + pl.program_id(1)) * nq + qi
        ride.before(refs, step)

        @pl.when(qi == 0)
        def _():
            dk_acc[...] = jnp.zeros_like(dk_acc)
            dv_acc[...] = jnp.zeros_like(dv_acc)
        lane = lax.broadcasted_iota(jnp.int32, (QB, QB), 1)
        head0 = lane < QB // 2
        q2, do2, total = [], [], []
        for hp in range(HP):
            q2.append(_stack_heads(q_ref[:, _lanes(hp)] * jnp.asarray(HEAD_SCALE, BF16), head0))
            do2.append(_stack_heads(do_ref[:, _lanes(hp)].astype(BF16), head0))
            ltv = lt_ref[:, _lanes(hp)]
            total.append(jnp.concatenate(
                [jnp.sum(jnp.where(lane == 0, ltv, 0.0), axis=1, keepdims=True),
                 jnp.sum(jnp.where(lane == QB // 2, ltv, 0.0), axis=1, keepdims=True)], axis=0))
        later = {KW: _cumsum_matrix(False, KW), QB: _cumsum_matrix(False, QB)}
        earlier = {KW: _cumsum_matrix(True, KW), QB: _cumsum_matrix(True, QB)}
        q_pos = qi * QB + (lax.broadcasted_iota(jnp.int32, (2 * QB, KW), 0) & (QB - 1))
        col = lax.broadcasted_iota(jnp.int32, (2 * QB, KW), 1)
        rest = qi // 2

        def group(start, kw, carry, masked):
            start = pl.multiple_of(start, QB)
            if masked:
                k_pos = start + col[:, :kw]
                valid = (k_pos < q_pos[:, :kw]) & (k_pos >= PAD)
            hps = range(HP)
            kg = [k_ref[pl.ds(start, kw), _lanes(hp)] for hp in hps]
            z = [_nt(q2[hp], kg[hp]) for hp in hps]
            da = [_nt(do2[hp], v_ref[pl.ds(start, kw), _lanes(hp)]) for hp in hps]
            logp, sig, after, rs = [], [], [], []
            for hp in hps:
                lp_ = _log_sigmoid(z[hp])
                lk = lp_ - z[hp]
                if masked:
                    lk = jnp.where(valid, lk, 0.0)
                logp.append(lp_)
                sig.append(jnp.exp(lp_))
                rs.append(jnp.sum(lk, axis=1, keepdims=True))
                after.append(_running(lk, later[kw]))
            a, gg, before = [], [], []
            for hp in hps:
                a_ = jnp.exp(logp[hp] + after[hp] + (total[hp] - carry[3 * hp] - rs[hp]))
                if masked:
                    a_ = jnp.where(valid, a_, 0.0)
                a.append(a_.astype(BF16))
                gg.append(a_ * da[hp])
                before.append(_nn(gg[hp].astype(BF16), earlier[kw]))
            out = []
            for hp in hps:
                seen, gsum, dq = carry[3 * hp], carry[3 * hp + 1], carry[3 * hp + 2]
                dz = gg[hp] - (gg[hp] + before[hp] + gsum) * sig[hp]
                if masked:
                    dz = jnp.where(valid, dz, 0.0)
                dz = dz.astype(BF16)
                dk_acc[pl.ds(start, kw), _lanes(hp)] += _tn(dz, q2[hp])
                dv_acc[pl.ds(start, kw), _lanes(hp)] += _tn(a[hp], do2[hp])
                out += [seen + rs[hp], gsum + jnp.sum(gg[hp], axis=1, keepdims=True), dq + _nn(dz, kg[hp])]
            return tuple(out)

        col0 = jnp.zeros((2 * QB, 1), F32)
        carry = (col0, col0, jnp.zeros((2 * QB, QB), F32)) * HP
        first = g0_ref[pl.program_id(0), qi].astype(jnp.int32)
        odd = qi % 2
        carry = lax.fori_loop(0, ((first == 0) & (rest >= 1)).astype(jnp.int32), lambda i, c: group(0, KW, c, True),
                              carry)
        carry = lax.fori_loop(jnp.maximum(first, 1), rest, lambda g, c: group(g * KW, KW, c, False), carry)
        carry = lax.fori_loop(0, odd, lambda i, c: group(rest * KW, KW, c, True), carry)
        carry = lax.fori_loop(0, 1 - odd, lambda i, c: group(qi * QB, QB, c, True), carry)
        for hp in range(HP):
            dq = carry[3 * hp + 2]
            dq_ref[:, _lanes(hp)] = (jnp.where(head0, dq[:QB], dq[QB:]) * HEAD_SCALE).astype(dq_ref.dtype)

        @pl.when(qi == nq - 1)
        def _():
            dk_ref[...] = dk_acc[...].astype(dk_ref.dtype)
            dv_ref[...] = dv_acc[...].astype(dv_ref.dtype)
        ride.after(refs, step)

    wb = HP * QB
    blk = pl.BlockSpec((QB, wb), lambda b, p, i: (b * nq + i, p))
    seq = pl.BlockSpec((lp, wb), lambda b, p, i: (b, p))
    out = jax.ShapeDtypeStruct((t, nblk * wb), BF16)
    return ride.call(
        body, name, (bl, nblk, nq),
        [blk, pl.BlockSpec((lp, wb), lambda b, p, i: (b, nblk + p)),
         pl.BlockSpec((lp, wb), lambda b, p, i: (b, 2 * nblk + p)), blk,
         pl.BlockSpec((QB, wb), lambda b, p, i: (b * nq + i, nblk + p)), pl.BlockSpec(memory_space=pltpu.SMEM)],
        [qkv, qkv, qkv, lt, dy, g0],
        [blk, seq, seq], [out, out, out], [pltpu.VMEM((lp, wb), F32), pltpu.VMEM((lp, wb), F32)])


def _place():
    return lax.axis_index("x"), lax.axis_index("y"), lax.axis_index("c")


def _peers(chip):
    kx, ky = chip // 2, chip % 2
    return ((1 - kx, ky), (kx, 1 - ky), (1 - kx, 1 - ky))


def _hbm_specs(n):
    return [pl.BlockSpec(memory_space=pl.ANY) for _ in range(n)]


def _remote(src, dst, send, recv, k, to):
    return pltpu.make_async_remote_copy(src, dst, send.at[k], recv.at[k], device_id=to, device_id_type=MESH)


class _Rider:
    def __init__(self, ins, out_shapes, aliases, nsem, first, mid=None, last=None):
        self.ins, self.out_shapes, self.aliases, self.nsem = list(ins), list(out_shapes), dict(aliases), nsem
        self.first, self.mid, self.last = first, mid, last


def _by_chip(fn):
    def run(ins, outs, send, recv):
        x, y, c = _place()
        for me in range(4):
            pl.when(2 * x + y == me)(functools.partial(fn, ins, outs, send, recv, me, c, (x, y, 1 - c)))
    return run


def _run_rider(name, rider):
    ni, no = len(rider.ins), len(rider.out_shapes)

    def body(*refs):
        args = (refs[:ni], refs[ni:ni + no], refs[ni + no], refs[ni + no + 1])
        for hook in (rider.first, rider.mid, rider.last):
            if hook is not None:
                hook(*args)

    return pl.pallas_call(
        body, name=name, in_specs=_hbm_specs(ni), out_specs=_hbm_specs(no), out_shape=rider.out_shapes,
        input_output_aliases=rider.aliases,
        scratch_shapes=[pltpu.SemaphoreType.DMA((rider.nsem,)), pltpu.SemaphoreType.DMA((rider.nsem,))],
        compiler_params=pltpu.CompilerParams(has_side_effects=True),
    )(*rider.ins)


class _Ride:
    def __init__(self, rider, n_in, n_out, steps):
        self.rider, self.n_in, self.n_out, self.steps = rider, n_in, n_out, steps
        self.ri = len(rider.ins) if rider else 0
        self.ro = len(rider.out_shapes) if rider else 0

    def own(self, refs):
        refs = list(refs)
        a, b = self.n_in, self.n_in + self.ri + self.n_out
        tail = refs[b + self.ro:len(refs) - 2] if self.rider else refs[b + self.ro:]
        return refs[:a] + refs[a + self.ri:b] + tail

    def _args(self, refs):
        a, b = self.n_in, self.n_in + self.ri + self.n_out
        return refs[a:a + self.ri], refs[b:b + self.ro], refs[-2], refs[-1]

    def before(self, refs, step):
        if self.rider is None:
            return
        pl.when(step == 0)(functools.partial(self.rider.first, *self._args(refs)))
        if self.rider.mid is not None:
            pl.when(step == (3 * self.steps) // 4)(functools.partial(self.rider.mid, *self._args(refs)))

    def after(self, refs, step):
        if self.rider is not None and self.rider.last is not None:
            pl.when(step == self.steps - 1)(functools.partial(self.rider.last, *self._args(refs)))

    def call(self, body, name, grid, in_specs, args, out_specs, out_shape, scratch,
             sem=("parallel", "parallel", "arbitrary")):
        r = self.rider
        if r is None:
            return pl.pallas_call(body, name=name, grid=grid, in_specs=in_specs, out_specs=out_specs,
                                  out_shape=out_shape, scratch_shapes=scratch, compiler_params=_params(sem))(*args)
        return pl.pallas_call(
            body, name=name, grid=grid, in_specs=in_specs + _hbm_specs(self.ri),
            out_specs=out_specs + _hbm_specs(self.ro), out_shape=out_shape + r.out_shapes,
            input_output_aliases={self.n_in + i: self.n_out + o for i, o in r.aliases.items()},
            scratch_shapes=scratch + [pltpu.SemaphoreType.DMA((r.nsem,)), pltpu.SemaphoreType.DMA((r.nsem,))],
            compiler_params=pltpu.CompilerParams(dimension_semantics=("arbitrary",) * len(grid),
                                                 vmem_limit_bytes=VMEM_LIMIT, has_side_effects=True),
        )(*args, *r.ins)


def _core_view(a, axis):
    l, r, c = a.shape
    return a.reshape(l, 4, 2, r // 8, c) if axis == 0 else a.reshape(l, 2, r // 2, c)


def _shard_view(a):
    l, r, c = a.shape
    return a.reshape(l, 2, r // 2, c)


def _piece(ref, axis, layer, chip, core):
    if axis == 0:
        return ref.at[layer, chip, core]
    cs = ref.shape[-1] // 4
    return ref.at[layer, core, :, pl.ds(chip * cs, cs)]


def _place_shard(name, w, axis, kidx, tr):
    _, r, cdim = w.shape
    shp = [2, r, cdim]
    shp[1 + axis] *= 4
    nb = r // tr

    def body(k_ref, w_ref, o_ref):
        o_ref[...] = w_ref[...].astype(o_ref.dtype)

    if axis == 0:
        out_spec = pl.BlockSpec((None, tr, cdim), lambda l, i, k_ref: (l, k_ref[0] * nb + i, 0))
    else:
        out_spec = pl.BlockSpec((None, tr, cdim), lambda l, i, k_ref: (l, i, k_ref[0]))
    return pl.pallas_call(
        body, name=name,
        grid_spec=pltpu.PrefetchScalarGridSpec(
            num_scalar_prefetch=1, grid=(2, nb),
            in_specs=[pl.BlockSpec((None, tr, cdim), lambda l, i, k_ref: (l, i, 0))], out_specs=out_spec),
        out_shape=jax.ShapeDtypeStruct(tuple(shp), BF16),
        compiler_params=_params(("arbitrary", "arbitrary")),
    )(kidx, w)


def _gather_rider(views, axes, items):
    n = len(items)

    def first(ins, outs, send, recv, me, c, sib):
        for i, (w, l) in enumerate(items):
            for j, (px, py) in enumerate(_peers(me)):
                _remote(_piece(ins[w], axes[w], l, me, c), _piece(outs[w], axes[w], l, me, c), send, recv,
                        3 * i + j, (px, py, c)).start()

    def mid(ins, outs, send, recv, me, c, sib):
        for i, (w, l) in enumerate(items):
            for j, (px, py) in enumerate(_peers(me)):
                got = _piece(outs[w], axes[w], l, 2 * px + py, c)
                _remote(got, got, send, recv, 3 * i + j, (px, py, c)).wait_recv()
                _remote(got, got, send, recv, 3 * (n + i) + j, sib).start()

    def last(ins, outs, send, recv, me, c, sib):
        for i, (w, l) in enumerate(items):
            for j, (px, py) in enumerate(_peers(me)):
                mine, got = _piece(outs[w], axes[w], l, me, c), _piece(outs[w], axes[w], l, 2 * px + py, c)
                theirs = _piece(outs[w], axes[w], l, 2 * px + py, 1 - c)
                _remote(theirs, theirs, send, recv, 3 * (n + i) + j, sib).wait_recv()
                _remote(mine, mine, send, recv, 3 * i + j, (px, py, c)).wait_send()
                _remote(got, got, send, recv, 3 * (n + i) + j, sib).wait_send()

    return _Rider(views, [jax.ShapeDtypeStruct(v.shape, v.dtype) for v in views], {w: w for w in range(len(views))},
                  6 * n, _by_chip(first), _by_chip(mid), _by_chip(last))


def _swap_rider(views, axes, items):
    nv = len(views)

    def part(ref, w, l, core):
        return ref.at[l, :, core] if axes[w] == 0 else ref.at[l, core]

    def copies(ins, outs, send, recv):
        x, y, c = _place()
        return [_remote(part(ins[w], w, l, 1 - c), outs[nv + i], send, recv, i, (x, y, 1 - c))
                for i, (w, l) in enumerate(items)]

    def first(ins, outs, send, recv):
        for cp in copies(ins, outs, send, recv):
            cp.start()

    def last(ins, outs, send, recv):
        for cp in copies(ins, outs, send, recv):
            cp.wait()

    got = [jax.ShapeDtypeStruct(views[w].shape[1:2] + views[w].shape[3:] if axes[w] == 0 else views[w].shape[2:],
                                views[w].dtype) for w, _ in items]
    return _Rider(views, [jax.ShapeDtypeStruct(v.shape, v.dtype) for v in views] + got,
                  {w: w for w in range(nv)}, len(items), first, None, last)


def _add_core(name, view, got, axis, layer, cidx, tr):
    def body(c_ref, g_ref, r_ref, o_ref):
        o_ref[...] = (g_ref[...] + r_ref[...]).astype(o_ref.dtype)

    if axis == 0:
        _, nchip, _, pr, cdim = view.shape
        grid = (nchip, pr // tr)
        specs = [pl.BlockSpec((None, None, None, tr, cdim), lambda k, i, c_ref: (layer, k, c_ref[0], i, 0)),
                 pl.BlockSpec((None, tr, cdim), lambda k, i, c_ref: (k, i, 0))]
        out_spec = pl.BlockSpec((None, tr, cdim), lambda k, i, c_ref: (k, i, 0))
    else:
        _, _, pr, cdim = view.shape
        grid = (pr // tr,)
        specs = [pl.BlockSpec((None, None, tr, cdim), lambda i, c_ref: (layer, c_ref[0], i, 0)),
                 pl.BlockSpec((tr, cdim), lambda i, c_ref: (i, 0))]
        out_spec = pl.BlockSpec((tr, cdim), lambda i, c_ref: (i, 0))
    return pl.pallas_call(
        body, name=name,
        grid_spec=pltpu.PrefetchScalarGridSpec(num_scalar_prefetch=1, grid=grid, in_specs=specs,
                                               out_specs=out_spec),
        out_shape=jax.ShapeDtypeStruct(got.shape, BF16),
        compiler_params=_params(("arbitrary",) * len(grid)),
    )(cidx, view, got)


def _scatter_rider(sums, axes):
    def part(ref, i, chip):
        if axes[i] == 0:
            return ref.at[chip]
        cs = ref.shape[-1] // 4
        return ref.at[:, pl.ds(chip * cs, cs)]

    def copies(ins, outs, send, recv, me, c, sib):
        return [_remote(part(ins[i], i, 2 * px + py), outs[i].at[j], send, recv, 3 * i + j, (px, py, c))
                for i in range(len(sums)) for j, (px, py) in enumerate(_peers(me))]

    def first(*args):
        for cp in copies(*args):
            cp.start()

    def last(*args):
        for cp in copies(*args):
            cp.wait()

    shapes = [jax.ShapeDtypeStruct((3,) + (s.shape[1:] if ax == 0 else (s.shape[0], s.shape[1] // 4)), s.dtype)
              for s, ax in zip(sums, axes)]
    return _Rider(sums, shapes, {}, 3 * len(sums), _by_chip(first), None, _by_chip(last))


def _add_chips(name, own, got, axis, layer, kc_idx, tr, into, shard_shape):
    _, pr, pc = got.shape

    def body(k_ref, o_ref, g_ref, *rest):
        rest[-1][...] = (o_ref[...].astype(F32) + g_ref[0].astype(F32) + g_ref[1].astype(F32)
                         + g_ref[2].astype(F32))

    if axis == 0:
        own_spec = pl.BlockSpec((None, tr, pc), lambda i, k_ref: (k_ref[0], i, 0))
    else:
        own_spec = pl.BlockSpec((tr, pc), lambda i, k_ref: (i, k_ref[0]))
    specs = [own_spec, pl.BlockSpec((3, tr, pc), lambda i, k_ref: (0, i, 0))]
    args = [kc_idx, own, got]
    if into is not None:
        specs.append(pl.BlockSpec(memory_space=pl.ANY))
        args.append(into)
    return pl.pallas_call(
        body, name=name,
        grid_spec=pltpu.PrefetchScalarGridSpec(
            num_scalar_prefetch=1, grid=(pr // tr,), in_specs=specs,
            out_specs=pl.BlockSpec((None, None, tr, pc), lambda i, k_ref: (layer, k_ref[1], i, 0))),
        out_shape=jax.ShapeDtypeStruct(shard_shape, F32),
        input_output_aliases={} if into is None else {3: 0},
        compiler_params=_params(("arbitrary",)),
    )(*args)


def _join_rider(parts):
    def first(ins, outs, send, recv):
        x, y, c = _place()
        for w in range(len(parts)):
            _remote(ins[w].at[:, c], outs[w].at[:, c], send, recv, w, (x, y, 1 - c)).start()

    def last(ins, outs, send, recv):
        x, y, c = _place()
        for w in range(len(parts)):
            _remote(ins[w].at[:, c], outs[w].at[:, c], send, recv, w, (x, y, 1 - c)).wait_send()
            _remote(ins[w].at[:, c], outs[w].at[:, 1 - c], send, recv, w, (x, y, 1 - c)).wait_recv()

    return _Rider(parts, [jax.ShapeDtypeStruct(p.shape, p.dtype) for p in parts],
                  {w: w for w in range(len(parts))}, len(parts), first, None, last)


def _all_reduce_small(name, pack, lead, groups):
    nr, d = pack.shape
    nout = nr - (groups - 1) * lead

    def body(in_ref, sum_ref, mine, slots, send, recv):
        x, y, c = _place()
        me = 4 * x + 2 * y + c
        fold = in_ref[0:lead]
        for grp in range(1, groups):
            fold = fold + in_ref[grp * lead:(grp + 1) * lead]
        mine[0:lead] = fold
        mine[lead:] = in_ref[groups * lead:]
        slots[me] = mine[...]
        cps = []
        for r in range(1, 8):
            rx, ry, rc = r // 4, (r // 2) % 2, r % 2
            peer = (x + rx - 2 * x * rx, y + ry - 2 * y * ry, c + rc - 2 * c * rc)
            cp = pltpu.make_async_remote_copy(mine, slots.at[me], send.at[r - 1], recv.at[r - 1],
                                              device_id=peer, device_id_type=MESH)
            cp.start()
            cps.append(cp)
        for cp in cps:
            cp.wait()
        acc = slots[0]
        for dev in range(1, 8):
            acc = acc + slots[dev]
        sum_ref[...] = acc

    vmem = pl.BlockSpec(memory_space=pltpu.VMEM)
    return pl.pallas_call(
        body, name=name, in_specs=[vmem], out_specs=vmem, out_shape=jax.ShapeDtypeStruct((nout, d), F32),
        scratch_shapes=[pltpu.VMEM((nout, d), F32), pltpu.VMEM((8, nout, d), F32), pltpu.SemaphoreType.DMA((7,)),
                        pltpu.SemaphoreType.DMA((7,))],
        compiler_params=pltpu.CompilerParams(has_side_effects=True, vmem_limit_bytes=VMEM_LIMIT),
    )(pack)


def _adamw_math(w, g, m, v):
    m = B1 * m + (1.0 - B1) * g
    v = B2 * v + (1.0 - B2) * (g * g)
    m_hat = m / (1.0 - B1 ** STEP)
    v_hat = v / (1.0 - B2 ** STEP)
    return -LR * (m_hat / (jnp.sqrt(v_hat) + ADAM_EPS) + WD * w), m, v


def _adamw(name, w, g, m, v, tr):
    shape = w.shape
    flat = [a.reshape(-1, shape[-1]) for a in (w, g, m, v)]
    r, cdim = flat[0].shape

    def body(w_ref, g_ref, m_ref, v_ref, d_ref, nm_ref, nv_ref):
        d_ref[...], nm_ref[...], nv_ref[...] = _adamw_math(w_ref[...], g_ref[...], m_ref[...], v_ref[...])

    spec = pl.BlockSpec((tr, cdim), lambda i: (i, 0))
    outs = pl.pallas_call(
        body, name=name, grid=(r // tr,), in_specs=[spec] * 4, out_specs=[spec] * 3,
        out_shape=[jax.ShapeDtypeStruct((r, cdim), F32)] * 3,
        compiler_params=_params(("parallel",)),
    )(*flat)
    return [o.reshape(shape) for o in outs]


def _adamw_small(name, groups):
    n = len(groups)
    shapes = [grp[0].shape for grp in groups]
    flat = [a.reshape(-1, a.shape[-1]) for grp in groups for a in grp]

    def body(*refs):
        ins, outs = refs[:4 * n], refs[4 * n:]
        for i in range(n):
            w_ref, g_ref, m_ref, v_ref = ins[4 * i:4 * i + 4]
            outs[3 * i][...], outs[3 * i + 1][...], outs[3 * i + 2][...] = _adamw_math(
                w_ref[...], g_ref[...], m_ref[...], v_ref[...])

    vmem = pl.BlockSpec(memory_space=pltpu.VMEM)
    out_shape = [jax.ShapeDtypeStruct(flat[4 * i].shape, F32) for i in range(n) for _ in range(3)]
    outs = pl.pallas_call(body, name=name, in_specs=[vmem] * (4 * n), out_specs=[vmem] * (3 * n),
                          out_shape=out_shape)(*flat)
    return [[outs[3 * i + j].reshape(shapes[i]) for j in range(3)] for i in range(n)]


def _block_diag(w_grp):
    g, pg, _ = w_grp.shape
    eye = jnp.eye(g, dtype=w_grp.dtype)
    return (eye[:, None, :, None] * w_grp[:, :, None, :]).reshape(g * pg, g * pg)


def _diag_blocks(m, g):
    pg = m.shape[0] // g
    return jnp.stack([m[i * pg:(i + 1) * pg, i * pg:(i + 1) * pg] for i in range(g)])


BIG = ("w_in", "w_out", "w_up", "w_down")
AXES = (1, 0, 1, 0)
W_IN, W_OUT, W_UP, W_DOWN = range(4)


def kernel(x, meta_tokens, g_mix, w_in, w_conv, w_pool, pool_scale, w_out, g_mlp, w_up, w_down, g_final, loss_target, m_meta_tokens, m_g_mix, m_w_in, m_w_conv, m_w_pool, m_pool_scale, m_w_out, m_g_mlp, m_w_up, m_w_down, m_g_final, v_meta_tokens, v_g_mix, v_w_in, v_w_conv, v_w_pool, v_pool_scale, v_w_out, v_g_mlp, v_w_up, v_w_down, v_g_final):
    bl, s, d = x.shape
    depth = g_mix.shape[0]
    assert depth == 2
    lp = PAD + N_META + s
    t = bl * lp
    tt = lp
    tm = lp // 4
    cs = w_conv.shape[2]
    cw = 4 * cs
    ngrp = w_pool.shape[1]
    xi, yi, ci = _place()
    chip = (2 * xi + yi).astype(jnp.int32)
    cidx, kidx = ci.astype(jnp.int32).reshape(1), chip.reshape(1)
    kc_idx = jnp.stack([chip, ci.astype(jnp.int32)])
    shards = (w_in, w_out, w_up, w_down)

    views = [_core_view(_place_shard(f"place_{BIG[w]}", shards[w], AXES[w], kidx, 256), AXES[w]) for w in range(4)]

    def whole(w):
        return views[w].reshape(depth, -1, views[w].shape[-1])

    def gather_on(call, items):
        ws = sorted({w for w, _ in items})
        res = call(_gather_rider([views[w] for w in ws], [AXES[w] for w in ws],
                                 [(ws.index(w), layer) for w, layer in items]))
        for j, w in enumerate(ws):
            views[w] = res[len(res) - len(ws) + j]
        return res[:len(res) - len(ws)]

    gather_on(lambda rider: _run_rider("gather_first", rider), [(W_IN, 0)])

    placed = jnp.zeros((32, d), F32)
    placed = lax.dynamic_update_slice(placed, meta_tokens, (0, chip * meta_tokens.shape[1]))
    placed = lax.dynamic_update_slice(placed, w_conv.reshape(-1, cs), (N_META, chip * cs))
    placed = jnp.where(ci == 0, placed, 0.0)
    small = _all_reduce_small("gather_small", placed, 8, 1)
    meta_full = small[:N_META]
    conv_full = small[N_META:N_META + depth * 3, :cw].reshape(depth, 3, cw)

    h = jnp.concatenate([jnp.zeros((bl, PAD, d), F32), jnp.broadcast_to(meta_full[None], (bl, N_META, d)), x],
                        axis=1).reshape(t, d)
    wbd = [_block_diag(w_pool[i]).astype(BF16) for i in range(depth)]
    saved = []
    for i in range(depth):
        hn, u_cp, qkv = _in_proj(f"in_proj{i}", h, g_mix[i], whole(W_IN), i, tm, 4 * cw)
        y_cp = _convpool_fwd(f"convpool{i}", u_cp, conv_full[i], wbd[i], pool_scale[i:i + 1], lp, tm)
        if i == 0:
            y_at, lt, g0 = gather_on(lambda rider: _attn_fwd(f"attn{i}", qkv, bl, lp, rider),
                                     [(W_OUT, 0), (W_UP, 0), (W_DOWN, 0), (W_IN, 1)])
            h_mid = _out_proj(f"out_proj{i}", y_cp, y_at, h, whole(W_OUT), i, tm)
            w_up0 = whole(W_UP)
            hn2, m_pre, act = gather_on(lambda rider: _up_proj(f"up_proj{i}", h_mid, g_mlp[i], w_up0, i, tm, rider),
                                        [(W_OUT, 1), (W_DOWN, 1)])
            w_down0 = whole(W_DOWN)
            (h_next,) = gather_on(lambda rider: _down_proj(f"down_proj{i}", act, h_mid, w_down0, i, tm, rider),
                                  [(W_UP, 1)])
        else:
            y_at, lt, g0 = _attn_fwd(f"attn{i}", qkv, bl, lp)
            h_mid = _out_proj(f"out_proj{i}", y_cp, y_at, h, whole(W_OUT), i, tm)
            hn2, m_pre, act = _up_proj(f"up_proj{i}", h_mid, g_mlp[i], whole(W_UP), i, tm)
            (h_next,) = _down_proj(f"down_proj{i}", act, h_mid, whole(W_DOWN), i, tm)
        saved.append((h, hn, u_cp, qkv, y_cp, y_at, (lt, g0), h_mid, hn2, m_pre, act))
        h = h_next

    dh, loss8, dgf8 = _loss_bwd("loss", h, g_final, loss_target, lp)
    loss = lax.psum(jnp.sum(loss8), ("x", "y", "c"))
    per_layer = {k: [None] * depth for k in ("g_mix", "w_conv", "w_pool", "pool_scale", "g_mlp")}

    gw = [None] * 4
    sums, arrived = {}, {}

    def dw(name, a, b, w, layer, tka, tn, row_off=0, col_off=0):
        shape = whole(w).shape
        into = None if gw[w] is None else gw[w].reshape(shape)
        gw[w] = _core_view(_mm_tn(name, a, b, tt, tka, tn, into, shape, layer, row_off, col_off), AXES[w])

    def swap_rider(ws):
        return _swap_rider([gw[w] for w, _ in ws], [AXES[w] for w, _ in ws],
                           [(j, layer) for j, (_, layer) in enumerate(ws)])

    def swapped(ws, outs):
        for j, (w, layer) in enumerate(ws):
            gw[w] = outs[j]
            sums[w, layer] = _add_core(f"chip_sum_{BIG[w]}{layer}", gw[w], outs[len(ws) + j], AXES[w], layer, cidx,
                                       128)

    def scatter_rider(items):
        return _scatter_rider([sums[it] for it in items], [AXES[w] for w, _ in items])

    def bwd_mlp(i, dh):
        _, _, _, _, y_cp, y_at, _, h_mid, hn2, m_pre, act = saved[i]
        dm = _down_proj_dx(f"down_proj_dx{i}", dh, m_pre, whole(W_DOWN), i, tm)
        dw(f"down_proj_dw{i}", act, dh, W_DOWN, i, 1024, 1024)
        dw(f"up_proj_dw{i}", hn2, dm, W_UP, i, 1024, 1024)
        dh_mid, dy, dg8 = _up_proj_dx(f"up_proj_dx{i}", dm, h_mid, dh, g_mlp[i], whole(W_UP), whole(W_OUT), i, tm)
        per_layer["g_mlp"][i] = dg8.sum(0)
        dw(f"out_proj_dw_cp{i}", y_cp, dh_mid, W_OUT, i, 512, 1024)
        dw(f"out_proj_dw_at{i}", y_at, dh_mid, W_OUT, i, 512, 1024, row_off=y_cp.shape[1])
        return dh_mid, dy

    def bwd_mix(i, dh_mid, dy, dus3):
        h_in, hn, u_cp = saved[i][:3]
        du_cp, sm, dwbd = _convpool_bwd(f"convpool_bwd{i}", u_cp, dy, conv_full[i], wbd[i], pool_scale[i:i + 1], lp,
                                        tm)
        sm = sm.reshape(4, 8, cw).sum(1)
        per_layer["w_conv"][i] = sm[0:3]
        per_layer["pool_scale"][i] = sm[3]
        per_layer["w_pool"][i] = _diag_blocks(dwbd, ngrp)
        dus, off = [du_cp, *dus3], 0
        for j, du in enumerate(dus):
            dw(f"in_proj_dw{j}_{i}", hn, du, W_IN, i, 1024, du.shape[1], col_off=off)
            off += du.shape[1]
        dh, dg8 = _in_proj_dx(f"in_proj_dx{i}", dus, h_in, dh_mid, g_mix[i], whole(W_IN), i, tm)
        per_layer["g_mix"][i] = dg8.sum(0)
        return dh

    def attn_bwd(i, dy, rider):
        qkv, (lt, g0) = saved[i][3], saved[i][6]
        res = _attn_bwd(f"attn_bwd{i}", qkv, lt, g0, dy, bl, lp, rider)
        return res[:3], res[3:]

    mlp_ws = [W_DOWN, W_UP, W_OUT]
    dh_mid, dy = bwd_mlp(1, dh)
    ws = [(w, 1) for w in mlp_ws]
    dus3, outs = attn_bwd(1, dy, swap_rider(ws))
    swapped(ws, outs)
    dh = bwd_mix(1, dh_mid, dy, dus3)

    dh_mid, dy = bwd_mlp(0, dh)
    ws = [(W_IN, 1)] + [(w, 0) for w in mlp_ws]
    swapped(ws, _run_rider("grads_swap0", swap_rider(ws)))
    items = list(sums)
    dus3, outs = attn_bwd(0, dy, scatter_rider(items))
    arrived.update(zip(items, outs))
    dh0 = bwd_mix(0, dh_mid, dy, dus3)
    swapped([(W_IN, 0)], _run_rider("grads_swap_in0", swap_rider([(W_IN, 0)])))
    arrived[W_IN, 0] = _run_rider("grads_scatter_in0", scatter_rider([(W_IN, 0)]))[0]

    finals = []
    for w in range(4):
        rs_, cs_ = shards[w].shape[1:]
        part = None
        for layer in reversed(range(depth)):
            part = _add_chips(f"reduce_{BIG[w]}{layer}", sums[w, layer], arrived[w, layer], AXES[w], layer, kc_idx,
                              128, part, (depth, 2, rs_ // 2, cs_))
        finals.append(part)
    finals = _run_rider("grads_join", _join_rider(finals))
    grad = {BIG[w]: finals[w].reshape(shards[w].shape) for w in range(4)}

    dh0 = dh0.reshape(bl, lp, d)
    grad_x = dh0[:, PAD + N_META:]
    local = {k: jnp.stack(v) for k, v in per_layer.items()}
    pieces = [dh0[:, PAD:PAD + N_META].reshape(bl * N_META, d), local["g_mix"], local["g_mlp"],
              dgf8.sum(0).reshape(1, d),
              jnp.pad(local["w_conv"].reshape(-1), (0, 2 * d - local["w_conv"].size)).reshape(2, d),
              jnp.pad(local["pool_scale"].reshape(-1), (0, d - local["pool_scale"].size)).reshape(1, d),
              local["w_pool"].reshape(-1, d)]
    summed = _all_reduce_small("small_grads", jnp.concatenate(pieces, axis=0), N_META, bl)
    o = N_META
    grad.update({
        "meta_tokens": lax.dynamic_slice_in_dim(summed[:o], chip * meta_tokens.shape[1], meta_tokens.shape[1], 1),
        "g_mix": summed[o:o + 2], "g_mlp": summed[o + 2:o + 4], "g_final": summed[o + 4],
        "w_conv": lax.dynamic_slice_in_dim(summed[o + 5:o + 7].reshape(-1)[:2 * 3 * cw].reshape(2, 3, cw),
                                           chip * cs, cs, 2),
        "pool_scale": summed[o + 7].reshape(-1)[:pool_scale.size].reshape(pool_scale.shape),
        "w_pool": summed[o + 8:].reshape(w_pool.shape),
    })

    weights = dict(meta_tokens=meta_tokens, g_mix=g_mix, w_in=w_in, w_conv=w_conv, w_pool=w_pool,
                   pool_scale=pool_scale, w_out=w_out, g_mlp=g_mlp, w_up=w_up, w_down=w_down, g_final=g_final)
    ms = dict(meta_tokens=m_meta_tokens, g_mix=m_g_mix, w_in=m_w_in, w_conv=m_w_conv, w_pool=m_w_pool,
              pool_scale=m_pool_scale, w_out=m_w_out, g_mlp=m_g_mlp, w_up=m_w_up, w_down=m_w_down,
              g_final=m_g_final)
    vs = dict(meta_tokens=v_meta_tokens, g_mix=v_g_mix, w_in=v_w_in, w_conv=v_w_conv, w_pool=v_w_pool,
              pool_scale=v_pool_scale, w_out=v_w_out, g_mlp=v_g_mlp, w_up=v_w_up, w_down=v_w_down,
              g_final=v_g_final)
    order = list(weights)
    upd = {k: _adamw(f"adamw_{k}", weights[k], grad[k], ms[k], vs[k], 256) for k in BIG}
    little = [k for k in order if k not in BIG]
    for k, res in zip(little, _adamw_small("adamw_small", [(weights[k], grad[k].reshape(weights[k].shape), ms[k],
                                                            vs[k]) for k in little])):
        upd[k] = res
    grad = {k: grad[k].reshape(weights[k].shape) for k in order}
    return (loss, grad_x, *[grad[k] for k in order], *[upd[k][0] for k in order], *[upd[k][1] for k in order],
            *[upd[k][2] for k in order])
```

```python
import functools

import jax
import jax.numpy as jnp
from jax import lax
from jax.experimental import pallas as pl
from jax.experimental.pallas import tpu as pltpu

F32, BF16 = jnp.float32, jnp.bfloat16
MESH = pl.DeviceIdType.MESH
EPS = 1e-6
N_META = 16
QB = 128
PAD = QB - N_META
HALO = 16
POOL_WINDOWS = (2.0, 4.0, 8.0, 16.0)
HEAD_SCALE = 0.125
LR, B1, B2, ADAM_EPS, WD, STEP = 0.001, 0.9, 0.999, 1e-08, 0.01, 10
VMEM_LIMIT = 56 * 1024 * 1024


def _params(sem=None):
    return pltpu.CompilerParams(dimension_semantics=sem, vmem_limit_bytes=VMEM_LIMIT)


def _nt(a, b):
    return lax.dot_general(a, b, (((1,), (1,)), ((), ())), preferred_element_type=F32)


def _tn(a, b):
    return lax.dot_general(a, b, (((0,), (0,)), ((), ())), preferred_element_type=F32)


def _nn(a, b):
    return jnp.dot(a, b, preferred_element_type=F32)


def _fold8(v):
    r, c = v.shape
    return jnp.sum(v.reshape(r // 8, 8, c), axis=0)


NCH = 512


def _rows_call(name, body, tm, row_ins, consts, row_outs, accs=(), rider=None):
    t = row_ins[0].shape[0]
    ride = _Ride(rider, len(row_ins) + len(consts), len(row_outs) + len(accs), t // tm)

    def stepped(*refs):
        step = pl.program_id(0)
        ride.before(refs, step)
        body(*ride.own(refs))
        ride.after(refs, step)

    in_specs = [pl.BlockSpec((tm, a.shape[1]), lambda i: (i, 0)) for a in row_ins]
    for a, layer in consts:
        if layer is None:
            in_specs.append(pl.BlockSpec(a.shape, lambda i: (0, 0)))
        else:
            in_specs.append(pl.BlockSpec((None, *a.shape[1:]), lambda i, l=layer: (l, 0, 0)))
    return ride.call(
        stepped, name, (t // tm,), in_specs, [*row_ins, *[a for a, _ in consts]],
        [pl.BlockSpec((tm, c), lambda i: (i, 0)) for c, _ in row_outs]
        + [pl.BlockSpec(s, lambda i: (0, 0)) for s in accs],
        [jax.ShapeDtypeStruct((t, c), dt) for c, dt in row_outs] + [jax.ShapeDtypeStruct(s, F32) for s in accs],
        [], ("arbitrary",) if accs else ("parallel",))


def _norm_parts(x):
    r = lax.rsqrt(jnp.mean(x * x, axis=-1, keepdims=True) + EPS)
    return r, x * r


def _norm_bwd(r, xh, dyn, g):
    w = dyn * g
    return r * (w - xh * jnp.mean(w * xh, axis=-1, keepdims=True))


def _in_proj(name, h, g, w, layer, tm, ncp):
    d, n = h.shape[1], w.shape[2]

    def body(h_ref, g_ref, w_ref, hn_ref, ucp_ref, qkv_ref):
        _, xh = _norm_parts(h_ref[...])
        hn = (xh * g_ref[...]).astype(BF16)
        hn_ref[...] = hn
        for n0 in range(0, n, NCH):
            acc = _nn(hn, w_ref[:, n0:n0 + NCH])
            if n0 < ncp:
                ucp_ref[:, n0:n0 + NCH] = acc
            else:
                qkv_ref[:, n0 - ncp:n0 - ncp + NCH] = acc.astype(BF16)

    return _rows_call(name, body, tm, [h], [(g.reshape(1, d), None), (w, layer)],
                      [(d, BF16), (ncp, F32), (n - ncp, BF16)])


def _out_proj(name, y_cp, y_at, h, w, layer, tm):
    d, k1 = h.shape[1], y_cp.shape[1]

    def body(ycp_ref, yat_ref, h_ref, w_ref, o_ref):
        for n0 in range(0, d, NCH):
            o_ref[:, n0:n0 + NCH] = (h_ref[:, n0:n0 + NCH] + _nn(ycp_ref[...], w_ref[0:k1, n0:n0 + NCH])
                                     + _nn(yat_ref[...], w_ref[k1:, n0:n0 + NCH]))

    return _rows_call(name, body, tm, [y_cp, y_at, h], [(w, layer)], [(d, F32)])[0]


def _up_proj(name, h_mid, g, w, layer, tm, rider=None):
    d, n = h_mid.shape[1], w.shape[2]

    def body(h_ref, g_ref, w_ref, hn_ref, m_ref, act_ref):
        _, xh = _norm_parts(h_ref[...])
        hn = (xh * g_ref[...]).astype(BF16)
        hn_ref[...] = hn
        for n0 in range(0, n, NCH):
            acc = _nn(hn, w_ref[:, n0:n0 + NCH])
            m_ref[:, n0:n0 + NCH] = acc.astype(BF16)
            act_ref[:, n0:n0 + NCH] = jnp.square(jnp.maximum(acc, 0.0)).astype(BF16)

    return _rows_call(name, body, tm, [h_mid], [(g.reshape(1, d), None), (w, layer)],
                      [(d, BF16), (n, BF16), (n, BF16)], rider=rider)


def _down_proj(name, act, h_mid, w, layer, tm, rider=None):
    d = h_mid.shape[1]

    def body(a_ref, h_ref, w_ref, o_ref):
        for n0 in range(0, d, NCH):
            o_ref[:, n0:n0 + NCH] = h_ref[:, n0:n0 + NCH] + _nn(a_ref[...], w_ref[:, n0:n0 + NCH])

    return _rows_call(name, body, tm, [act, h_mid], [(w, layer)], [(d, F32)], rider=rider)


def _down_proj_dx(name, dh, m_pre, w, layer, tm):
    n = w.shape[1]

    def body(dh_ref, m_ref, w_ref, dm_ref):
        dhb = dh_ref[...].astype(BF16)
        for n0 in range(0, n, NCH):
            dm_ref[:, n0:n0 + NCH] = (_nt(dhb, w_ref[n0:n0 + NCH, :])
                                      * (2.0 * jnp.maximum(m_ref[:, n0:n0 + NCH].astype(F32), 0.0))).astype(BF16)

    return _rows_call(name, body, tm, [dh, m_pre], [(w, layer)], [(n, BF16)])[0]


def _up_proj_dx(name, dm, h_mid, dh, g, w_up, w_out, layer, tm, rider=None):
    d = h_mid.shape[1]

    def body(dm_ref, h_ref, dh_ref, g_ref, wup_ref, wout_ref, dhm_ref, dy_ref, dg_ref):
        @pl.when(pl.program_id(0) == 0)
        def _():
            dg_ref[...] = jnp.zeros_like(dg_ref)
        dyn = _nt(dm_ref[...], wup_ref[...])
        r, xh = _norm_parts(h_ref[...])
        dhm = dh_ref[...] + _norm_bwd(r, xh, dyn, g_ref[...])
        dhm_ref[...] = dhm
        dg_ref[...] += _fold8(dyn * xh)
        dy_ref[...] = _nt(dhm.astype(BF16), wout_ref[...])

    return _rows_call(name, body, tm, [dm, h_mid, dh], [(g.reshape(1, d), None), (w_up, layer), (w_out, layer)],
                      [(d, F32), (w_out.shape[1], F32)], [(8, d)], rider)


def _in_proj_dx(name, dus, h, dh_mid, g, w, layer, tm, rider=None):
    d = h.shape[1]
    ns = [du.shape[1] for du in dus]
    nd = len(dus)

    def body(*refs):
        du_refs = refs[:nd]
        h_ref, dhm_ref, g_ref, w_ref, dh_ref, dg_ref = refs[nd:]

        @pl.when(pl.program_id(0) == 0)
        def _():
            dg_ref[...] = jnp.zeros_like(dg_ref)
        dyn, off = None, 0
        for du_ref, n in zip(du_refs, ns):
            part = _nt(du_ref[...], w_ref[:, off:off + n])
            dyn = part if dyn is None else dyn + part
            off += n
        r, xh = _norm_parts(h_ref[...])
        dh_ref[...] = dhm_ref[...] + _norm_bwd(r, xh, dyn, g_ref[...])
        dg_ref[...] += _fold8(dyn * xh)

    return _rows_call(name, body, tm, [*dus, h, dh_mid], [(g.reshape(1, d), None), (w, layer)], [(d, F32)],
                      [(8, d)], rider)


def _mm_tn(name, a, b, tt, tka, tn, into, shape, layer, row_off, col_off):
    t, ka = a.shape
    n = b.shape[1]
    assert t % tt == 0 and ka % tka == 0 and n % tn == 0 and row_off % tka == 0 and col_off % tn == 0

    def body(a_ref, b_ref, *rest):
        o_ref = rest[-1]

        @pl.when(pl.program_id(2) == 0)
        def _():
            o_ref[...] = jnp.zeros_like(o_ref)
        o_ref[...] += _tn(a_ref[...].astype(BF16), b_ref[...].astype(BF16))

    in_specs = [pl.BlockSpec((tt, tka), lambda i, j, s: (s, i)), pl.BlockSpec((tt, tn), lambda i, j, s: (s, j))]
    args = [a, b]
    if into is not None:
        in_specs.append(pl.BlockSpec(memory_space=pl.ANY))
        args.append(into)
    return pl.pallas_call(
        body, name=name, grid=(ka // tka, n // tn, t // tt), in_specs=in_specs,
        out_specs=pl.BlockSpec((None, tka, tn), lambda i, j, s: (layer, row_off // tka + i, col_off // tn + j)),
        out_shape=jax.ShapeDtypeStruct(shape, F32),
        input_output_aliases={} if into is None else {2: 0},
        compiler_params=_params(("parallel", "parallel", "arbitrary")),
    )(*args)


def _mm_tn_slab(name, a_list, b_list, tt, into, shape, layer):
    t = a_list[0].shape[0]
    kas, ns = [a.shape[1] for a in a_list], [b.shape[1] for b in b_list]
    assert t % tt == 0 and (sum(kas), sum(ns)) == tuple(shape[1:])
    na, nb = len(a_list), len(b_list)

    def body(*refs):
        o_ref = refs[-1]

        @pl.when(pl.program_id(0) == 0)
        def _():
            o_ref[...] = jnp.zeros_like(o_ref)
        r0 = 0
        for a_ref, ka in zip(refs[:na], kas):
            a = a_ref[...].astype(BF16)
            c0 = 0
            for b_ref, n in zip(refs[na:na + nb], ns):
                o_ref[r0:r0 + ka, c0:c0 + n] += _tn(a, b_ref[...].astype(BF16))
                c0 += n
            r0 += ka

    in_specs = [pl.BlockSpec((tt, c), lambda s: (s, 0)) for c in kas + ns]
    args = [*a_list, *b_list]
    if into is not None:
        in_specs.append(pl.BlockSpec(memory_space=pl.ANY))
        args.append(into)
    return pl.pallas_call(
        body, name=name, grid=(t // tt,), in_specs=in_specs,
        out_specs=pl.BlockSpec((None, *shape[1:]), lambda s: (layer, 0, 0)),
        out_shape=jax.ShapeDtypeStruct(shape, F32),
        input_output_aliases={} if into is None else {na + nb: 0},
        compiler_params=_params(("arbitrary",)),
    )(*args)


def _loss_bwd(name, h, g, target, lp):
    t, d = h.shape
    bl = target.shape[0]
    nq = lp // QB

    def body(h_ref, g_ref, t_ref, dh_ref, ls_ref, dg_ref):
        b, j = pl.program_id(0), pl.program_id(1)

        @pl.when((b == 0) & (j == 0))
        def _():
            ls_ref[...] = jnp.zeros_like(ls_ref)
            dg_ref[...] = jnp.zeros_like(dg_ref)
        xv = h_ref[...]
        r = lax.rsqrt(jnp.mean(xv * xv, axis=-1, keepdims=True) + EPS)
        xh = xv * r
        gv = g_ref[...]
        err = jnp.where(j >= 1, xh * gv - t_ref[...], 0.0)
        ls_ref[...] += _fold8(err * err) * (0.5 / d)
        dy = err * (1.0 / d)
        w = dy * gv
        dh_ref[...] = r * (w - xh * jnp.mean(w * xh, axis=-1, keepdims=True))
        dg_ref[...] += _fold8(dy * xh)

    return pl.pallas_call(
        body, name=name, grid=(bl, nq),
        in_specs=[pl.BlockSpec((QB, d), lambda b, j: (b * nq + j, 0)), pl.BlockSpec((1, d), lambda b, j: (0, 0)),
                  pl.BlockSpec((None, QB, d), lambda b, j: (b, jnp.maximum(j - 1, 0), 0))],
        out_specs=[pl.BlockSpec((QB, d), lambda b, j: (b * nq + j, 0)), pl.BlockSpec((8, d), lambda b, j: (0, 0)),
                   pl.BlockSpec((8, d), lambda b, j: (0, 0))],
        out_shape=[jax.ShapeDtypeStruct((t, d), F32), jax.ShapeDtypeStruct((8, d), F32),
                   jax.ShapeDtypeStruct((8, d), F32)],
        compiler_params=_params(("arbitrary", "arbitrary")),
    )(h, g.reshape(1, d), target)


def _pool_select(grp, a2, a4, a8, a16):
    return jnp.where(grp == 0, a2, jnp.where(grp == 1, a4, jnp.where(grp == 2, a8, a16)))


def _trailing_sums(v):
    s2 = v + pltpu.roll(v, 1, 0)
    s4 = s2 + pltpu.roll(s2, 2, 0)
    s8 = s4 + pltpu.roll(s4, 4, 0)
    s16 = s8 + pltpu.roll(s8, 8, 0)
    return s2, s4, s8, s16


def _leading_sums(v):
    n = v.shape[0]
    s2 = v + pltpu.roll(v, n - 1, 0)
    s4 = s2 + pltpu.roll(s2, n - 2, 0)
    s8 = s4 + pltpu.roll(s4, n - 4, 0)
    s16 = s8 + pltpu.roll(s8, n - 8, 0)
    return s2, s4, s8, s16


def _convpool_fwd(name, u_cp, wconv, wbd, pscale, lp, r):
    t = u_cp.shape[0]
    cw = u_cp.shape[1] // 4
    tps, hb = lp // r, r // HALO

    def body(cb_ref, cc_ref, cx_ref, pi_ref, cch_ref, cxh_ref, pih_ref, wc_ref, wbd_ref, ps_ref, y_ref):
        i = pl.program_id(0)
        lrow = (i % tps) * r + lax.broadcasted_iota(jnp.int32, (r, 1), 0)
        valid = lrow >= PAD
        xx = jnp.concatenate([cch_ref[...] * cxh_ref[...], cc_ref[...] * cx_ref[...]], axis=0)
        conv = (wc_ref[0:1, :] * pltpu.roll(xx, 2, 0) + wc_ref[1:2, :] * pltpu.roll(xx, 1, 0)
                + wc_ref[2:3, :] * xx)
        y_ref[:, 0:cw] = (cb_ref[...] * conv[HALO:]).astype(y_ref.dtype)
        p = pi_ref[...]
        grp = lax.broadcasted_iota(jnp.int32, (1, cw), 1) // (cw // 4)
        sel = _pool_select(grp, *_trailing_sums(jnp.concatenate([pih_ref[...], p], axis=0)))[HALO:]
        cnt = jnp.maximum(jnp.minimum((lrow - (PAD - 1)).astype(F32), _pool_select(grp, *POOL_WINDOWS)), 1.0)
        pooled = jnp.where(valid, sel / cnt - p, 0.0)
        y_ref[:, cw:2 * cw] = (_nn(pooled.astype(BF16), wbd_ref[...]) * ps_ref[...]).astype(y_ref.dtype)

    def main(col):
        return pl.BlockSpec((r, cw), lambda i: (i, col))

    def prev(col):
        return pl.BlockSpec((HALO, cw), lambda i: (jnp.maximum(i * hb - 1, 0), col))

    def whole(a):
        return pl.BlockSpec(a.shape, lambda i: (0, 0))

    return pl.pallas_call(
        body, name=name, grid=(t // r,),
        in_specs=[main(0), main(1), main(2), main(3), prev(1), prev(2), prev(3), whole(wconv), whole(wbd),
                  whole(pscale)],
        out_specs=pl.BlockSpec((r, 2 * cw), lambda i: (i, 0)),
        out_shape=jax.ShapeDtypeStruct((t, 2 * cw), BF16),
        compiler_params=_params(("parallel",)),
    )(u_cp, u_cp, u_cp, u_cp, u_cp, u_cp, u_cp, wconv, wbd, pscale)


def _convpool_bwd(name, u_cp, dy, wconv, wbd, pscale, lp, r):
    t = u_cp.shape[0]
    cw = u_cp.shape[1] // 4
    tps, hb = lp // r, r // HALO
    e = r + HALO

    def body(cb_ref, cc_ref, cx_ref, pi_ref, cbn_ref, cch_ref, cxh_ref, pih_ref, dyc_ref, dyp_ref, dycn_ref,
             dypn_ref, wc_ref, wbd_ref, ps_ref, du_ref, sm_ref, dwbd_ref):
        i = pl.program_id(0)

        @pl.when(i == 0)
        def _():
            sm_ref[...] = jnp.zeros_like(sm_ref)
            dwbd_ref[...] = jnp.zeros_like(dwbd_ref)
        lrow_e = (i % tps) * r + lax.broadcasted_iota(jnp.int32, (e, 1), 0)
        valid_e = (lrow_e >= PAD) & (lrow_e < lp)
        lrow, valid = lrow_e[:r], lrow_e[:r] >= PAD
        w0, w1, w2 = wc_ref[0:1, :], wc_ref[1:2, :], wc_ref[2:3, :]
        cb, cc, cx = cb_ref[...], cc_ref[...], cx_ref[...]
        prod = cc * cx
        xx = jnp.concatenate([cch_ref[...] * cxh_ref[...], prod], axis=0)
        back1, back2 = pltpu.roll(xx, 1, 0)[HALO:], pltpu.roll(xx, 2, 0)[HALO:]
        dyc = dyc_ref[...]
        du_ref[:, 0:cw] = (dyc * (w0 * back2 + w1 * back1 + w2 * prod)).astype(du_ref.dtype)
        dconv_e = jnp.where(valid_e, jnp.concatenate([dyc * cb, dycn_ref[...] * cbn_ref[...]], axis=0), 0.0)
        dconv = dconv_e[:r]
        dprod = (w2 * dconv + w1 * pltpu.roll(dconv_e, e - 1, 0)[:r] + w0 * pltpu.roll(dconv_e, e - 2, 0)[:r])
        du_ref[:, cw:2 * cw] = (dprod * cx).astype(du_ref.dtype)
        du_ref[:, 2 * cw:3 * cw] = (dprod * cc).astype(du_ref.dtype)
        sm_ref[0:8, :] += _fold8(dconv * back2)
        sm_ref[8:16, :] += _fold8(dconv * back1)
        sm_ref[16:24, :] += _fold8(dconv * prod)
        p = pi_ref[...]
        grp = lax.broadcasted_iota(jnp.int32, (1, cw), 1) // (cw // 4)
        win = _pool_select(grp, *POOL_WINDOWS)
        sel = _pool_select(grp, *_trailing_sums(jnp.concatenate([pih_ref[...], p], axis=0)))[HALO:]
        cnt_e = jnp.maximum(jnp.minimum((lrow_e - (PAD - 1)).astype(F32), win), 1.0)
        pooled = jnp.where(valid, sel / cnt_e[:r] - p, 0.0).astype(BF16)
        dyp = dyp_ref[...]
        sm_ref[24:32, :] += _fold8(dyp * _nn(pooled, wbd_ref[...]))
        dpre_e = (jnp.concatenate([dyp, dypn_ref[...]], axis=0) * ps_ref[...]).astype(BF16)
        dwbd_ref[...] += _tn(pooled, dpre_e[:r])
        dpooled_e = jnp.where(valid_e, _nt(dpre_e, wbd_ref[...]), 0.0)
        ahead = _pool_select(grp, *_leading_sums(dpooled_e / cnt_e))[:r]
        du_ref[:, 3 * cw:4 * cw] = (ahead - dpooled_e[:r]).astype(du_ref.dtype)

    last_halo = t // HALO - 1

    def main(col):
        return pl.BlockSpec((r, cw), lambda i: (i, col))

    def prev(col):
        return pl.BlockSpec((HALO, cw), lambda i: (jnp.maximum(i * hb - 1, 0), col))

    def nxt(col):
        return pl.BlockSpec((HALO, cw), lambda i: (jnp.minimum((i + 1) * hb, last_halo), col))

    def whole(a):
        return pl.BlockSpec(a.shape, lambda i: (0, 0))

    return pl.pallas_call(
        body, name=name, grid=(t // r,),
        in_specs=[main(0), main(1), main(2), main(3), nxt(0), prev(1), prev(2), prev(3), main(0), main(1), nxt(0),
                  nxt(1), whole(wconv), whole(wbd), whole(pscale)],
        out_specs=[pl.BlockSpec((r, 4 * cw), lambda i: (i, 0)), pl.BlockSpec((32, cw), lambda i: (0, 0)),
                   pl.BlockSpec((cw, cw), lambda i: (0, 0))],
        out_shape=[jax.ShapeDtypeStruct((t, 4 * cw), BF16), jax.ShapeDtypeStruct((32, cw), F32),
                   jax.ShapeDtypeStruct((cw, cw), F32)],
        compiler_params=_params(("arbitrary",)),
    )(u_cp, u_cp, u_cp, u_cp, u_cp, u_cp, u_cp, u_cp, dy, dy, dy, dy, wconv, wbd, pscale)


KW = 2 * QB
HP = 4
DECAY = 64.0


def _cumsum_matrix(before, kw):
    r = lax.broadcasted_iota(jnp.int32, (kw, kw), 0)
    c = lax.broadcasted_iota(jnp.int32, (kw, kw), 1)
    return ((r < c) if before else (r > c)).astype(BF16)


def _running(v, mat):
    m = v.shape[0]
    hi = v.astype(BF16)
    ext = _nn(jnp.concatenate([hi, (v - hi.astype(F32)).astype(BF16)], axis=0), mat)
    return ext[:m] + ext[m:]


def _log_sigmoid(z):
    neg_abs = lax.bitcast_convert_type(lax.bitcast_convert_type(z, jnp.int32) | jnp.int32(-2 ** 31), F32)
    return jnp.minimum(z, 0.0) - jnp.log(1.0 + jnp.exp(neg_abs))


def _stack_heads(v, head0):
    zero = jnp.zeros_like(v)
    return jnp.concatenate([jnp.where(head0, v, zero), jnp.where(head0, zero, v)], axis=0)


def _lanes(hp):
    return slice(hp * QB, (hp + 1) * QB)


def _attn_fwd(name, qkv, bl, lp, rider=None):
    t = qkv.shape[0]
    nq, nblk = lp // QB, qkv.shape[1] // (3 * HP * QB)
    assert nblk == 1
    ride = _Ride(rider, 3, 3, bl * nblk * nq)

    def body(*refs):
        q_ref, k_ref, v_ref, o_ref, lt_ref, g0_ref = ride.own(refs)
        qi = pl.program_id(2)
        step = (pl.program_id(0) * nblk + pl.program_id(1)) * nq + qi
        ride.before(refs, step)
        head0 = lax.broadcasted_iota(jnp.int32, (QB, QB), 1) < QB // 2
        q2 = [_stack_heads(q_ref[:, _lanes(hp)] * jnp.asarray(HEAD_SCALE, BF16), head0) for hp in range(HP)]
        later = {KW: _cumsum_matrix(False, KW)}
        q_pos = qi * QB + (lax.broadcasted_iota(jnp.int32, (2 * QB, KW), 0) & (QB - 1))
        col = lax.broadcasted_iota(jnp.int32, (2 * QB, KW), 1)
        rest = qi // 2

        def group(start, kw, carry, masked, lo=0):
            start = pl.multiple_of(start, QB)
            if masked:
                k_pos = start + col[:, :kw]
                valid = (k_pos < q_pos[:, :kw]) & (k_pos >= jnp.maximum(lo, PAD))
            z = [_nt(q2[hp], k_ref[pl.ds(start, kw), _lanes(hp)]) for hp in range(HP)]
            logp, after, rs = [], [], []
            for hp in range(HP):
                lp_ = _log_sigmoid(z[hp])
                lk = lp_ - z[hp]
                if masked:
                    lk = jnp.where(valid, lk, 0.0)
                logp.append(lp_)
                rs.append(jnp.sum(lk, axis=1, keepdims=True))
                after.append(_running(lk, later[kw]))
            out = []
            for hp in range(HP):
                run, acc = carry[2 * hp], carry[2 * hp + 1]
                a = jnp.exp(logp[hp] + after[hp] + run)
                if masked:
                    a = jnp.where(valid, a, 0.0)
                out += [run + rs[hp], acc + _nn(a.astype(BF16), v_ref[pl.ds(start, kw), _lanes(hp)])]
            return tuple(out)

        def alive(carry):
            most = carry[0]
            for hp in range(1, HP):
                most = jnp.maximum(most, carry[2 * hp])
            return jnp.max(most) > -DECAY

        carry = (jnp.zeros((2 * QB, 1), F32), jnp.zeros((2 * QB, QB), F32)) * HP
        carry = group(jnp.minimum(rest * KW, lp - KW), KW, carry, True, lo=rest * KW)
        g, *carry = lax.while_loop(lambda st: (st[0] >= 1) & alive(st[1:]),
                                   lambda st: (st[0] - 1, *group(st[0] * KW, KW, tuple(st[1:]), False)),
                                   (rest - 1, *carry))
        oldest = (g == 0) & (rest >= 1) & alive(carry)
        carry = lax.fori_loop(0, oldest.astype(jnp.int32), lambda i, c: group(0, KW, c, True), tuple(carry))
        g0_ref[pl.program_id(0), qi] = jnp.where(oldest, 0, g + 1).astype(F32)
        for hp in range(HP):
            run, acc = carry[2 * hp], carry[2 * hp + 1]
            o_ref[:, _lanes(hp)] = jnp.where(head0, acc[:QB], acc[QB:]).astype(o_ref.dtype)
            lt_ref[:, _lanes(hp)] = jnp.where(head0, run[:QB], run[QB:])
        ride.after(refs, step)

    wb = HP * QB
    blk = pl.BlockSpec((QB, wb), lambda b, p, i: (b * nq + i, p))
    return ride.call(
        body, name, (bl, nblk, nq),
        [blk, pl.BlockSpec((lp, wb), lambda b, p, i: (b, nblk + p)),
         pl.BlockSpec((lp, wb), lambda b, p, i: (b, 2 * nblk + p))], [qkv, qkv, qkv],
        [blk, blk, pl.BlockSpec(memory_space=pltpu.SMEM)],
        [jax.ShapeDtypeStruct((t, nblk * wb), BF16), jax.ShapeDtypeStruct((t, nblk * wb), F32),
         jax.ShapeDtypeStruct((bl, nq), F32)], [])


def _attn_bwd(name, qkv, lt, g0, dy, bl, lp, rider=None):
    t = qkv.shape[0]
    nq, nblk = lp // QB, qkv.shape[1] // (3 * HP * QB)
    ride = _Ride(rider, 6, 3, bl * nblk * nq)

    def body(*refs):
        q_ref, k_ref, v_ref, lt_ref, do_ref, g0_ref, dq_ref, dk_ref, dv_ref, dk_acc, dv_acc = ride.own(refs)
        qi = pl.program_id(2)
        step = (pl.program_id(0) * nblk + pl.program_id(1)) * nq + qi
        ride.before(refs, step)

        @pl.when(qi == 0)
        def _():
            dk_acc[...] = jnp.zeros_like(dk_acc)
            dv_acc[...] = jnp.zeros_like(dv_acc)
        lane = lax.broadcasted_iota(jnp.int32, (QB, QB), 1)
        head0 = lane < QB // 2
        q2, do2, total = [], [], []
        for hp in range(HP):
            q2.append(_stack_heads(q_ref[:, _lanes(hp)] * jnp.asarray(HEAD_SCALE, BF16), head0))
            do2.append(_stack_heads(do_ref[:, _lanes(hp)].astype(BF16), head0))
            ltv = lt_ref[:, _lanes(hp)]
            total.append(jnp.concatenate(
                [jnp.sum(jnp.where(lane == 0, ltv, 0.0), axis=1, keepdims=True),
                 jnp.sum(jnp.where(lane == QB // 2, ltv, 0.0), axis=1, keepdims=True)], axis=0))
        later = {KW: _cumsum_matrix(False, KW), QB: _cumsum_matrix(False, QB)}
        earlier = {KW: _cumsum_matrix(True, KW), QB: _cumsum_matrix(True, QB)}
        q_pos = qi * QB + (lax.broadcasted_iota(jnp.int32, (2 * QB, KW), 0) & (QB - 1))
        col = lax.broadcasted_iota(jnp.int32, (2 * QB, KW), 1)
        rest = qi // 2

        def group(start, kw, carry, masked):
            start = pl.multiple_of(start, QB)
            if masked:
                k_pos = start + col[:, :kw]
                valid = (k_pos < q_pos[:, :kw]) & (k_pos >= PAD)
            hps = range(HP)
            kg = [k_ref[pl.ds(start, kw), _lanes(hp)] for hp in hps]
            z = [_nt(q2[hp], kg[hp]) for hp in hps]
            da = [_nt(do2[hp], v_ref[pl.ds(start, kw), _lanes(hp)]) for hp in hps]
            logp, sig, after, rs = [], [], [], []
            for hp in hps:
                lp_ = _log_sigmoid(z[hp])
                lk = lp_ - z[hp]
                if masked:
                    lk = jnp.where(valid, lk, 0.0)
                logp.append(lp_)
                sig.append(jnp.exp(lp_))
                rs.append(jnp.sum(lk, axis=1, keepdims=True))
                after.append(_running(lk, later[kw]))
            a, gg, before = [], [], []
            for hp in hps:
                a_ = jnp.exp(logp[hp] + after[hp] + (total[hp] - carry[3 * hp] - rs[hp]))
                if masked:
                    a_ = jnp.where(valid, a_, 0.0)
                a.append(a_.astype(BF16))
                gg.append(a_ * da[hp])
                before.append(_nn(gg[hp].astype(BF16), earlier[kw]))
            out = []
            for hp in hps:
                seen, gsum, dq = carry[3 * hp], carry[3 * hp + 1], carry[3 * hp + 2]
                dz = gg[hp] - (gg[hp] + before[hp] + gsum) * sig[hp]
                if masked:
                    dz = jnp.where(valid, dz, 0.0)
                dz = dz.astype(BF16)
                dk_acc[pl.ds(start, kw), _lanes(hp)] += _tn(dz, q2[hp])
                dv_acc[pl.ds(start, kw), _lanes(hp)] += _tn(a[hp], do2[hp])
                out += [seen + rs[hp], gsum + jnp.sum(gg[hp], axis=1, keepdims=True), dq + _nn(dz, kg[hp])]
            return tuple(out)

        col0 = jnp.zeros((2 * QB, 1), F32)
        carry = (col0, col0, jnp.zeros((2 * QB, QB), F32)) * HP
        first = g0_ref[pl.program_id(0), qi].astype(jnp.int32)
        odd = qi % 2
        carry = lax.fori_loop(0, ((first == 0) & (rest >= 1)).astype(jnp.int32), lambda i, c: group(0, KW, c, True),
                              carry)
        carry = lax.fori_loop(jnp.maximum(first, 1), rest, lambda g, c: group(g * KW, KW, c, False), carry)
        carry = lax.fori_loop(0, odd, lambda i, c: group(rest * KW, KW, c, True), carry)
        carry = lax.fori_loop(0, 1 - odd, lambda i, c: group(qi * QB, QB, c, True), carry)
        for hp in range(HP):
            dq = carry[3 * hp + 2]
            dq_ref[:, _lanes(hp)] = (jnp.where(head0, dq[:QB], dq[QB:]) * HEAD_SCALE).astype(dq_ref.dtype)

        @pl.when(qi == nq - 1)
        def _():
            dk_ref[...] = dk_acc[...].astype(dk_ref.dtype)
            dv_ref[...] = dv_acc[...].astype(dv_ref.dtype)
        ride.after(refs, step)

    wb = HP * QB
    blk = pl.BlockSpec((QB, wb), lambda b, p, i: (b * nq + i, p))
    seq = pl.BlockSpec((lp, wb), lambda b, p, i: (b, p))
    out = jax.ShapeDtypeStruct((t, nblk * wb), BF16)
    return ride.call(
        body, name, (bl, nblk, nq),
        [blk, pl.BlockSpec((lp, wb), lambda b, p, i: (b, nblk + p)),
         pl.BlockSpec((lp, wb), lambda b, p, i: (b, 2 * nblk + p)), blk,
         pl.BlockSpec((QB, wb), lambda b, p, i: (b * nq + i, nblk + p)), pl.BlockSpec(memory_space=pltpu.SMEM)],
        [qkv, qkv, qkv, lt, dy, g0],
        [blk, seq, seq], [out, out, out], [pltpu.VMEM((lp, wb), F32), pltpu.VMEM((lp, wb), F32)])


def _place():
    return lax.axis_index("x"), lax.axis_index("y"), lax.axis_index("c")


def _peers(chip):
    kx, ky = chip // 2, chip % 2
    return ((1 - kx, ky), (kx, 1 - ky), (1 - kx, 1 - ky))


def _hbm_specs(n):
    return [pl.BlockSpec(memory_space=pl.ANY) for _ in range(n)]


def _remote(src, dst, send, recv, k, to):
    return pltpu.make_async_remote_copy(src, dst, send.at[k], recv.at[k], device_id=to, device_id_type=MESH)


class _Rider:
    def __init__(self, ins, out_shapes, aliases, nsem, first, mid=None, last=None):
        self.ins, self.out_shapes, self.aliases, self.nsem = list(ins), list(out_shapes), dict(aliases), nsem
        self.first, self.mid, self.last = first, mid, last


def _by_chip(fn):
    def run(ins, outs, send, recv):
        x, y, c = _place()
        for me in range(4):
            pl.when(2 * x + y == me)(functools.partial(fn, ins, outs, send, recv, me, c, (x, y, 1 - c)))
    return run


def _run_rider(name, rider):
    ni, no = len(rider.ins), len(rider.out_shapes)

    def body(*refs):
        args = (refs[:ni], refs[ni:ni + no], refs[ni + no], refs[ni + no + 1])
        for hook in (rider.first, rider.mid, rider.last):
            if hook is not None:
                hook(*args)

    return pl.pallas_call(
        body, name=name, in_specs=_hbm_specs(ni), out_specs=_hbm_specs(no), out_shape=rider.out_shapes,
        input_output_aliases=rider.aliases,
        scratch_shapes=[pltpu.SemaphoreType.DMA((rider.nsem,)), pltpu.SemaphoreType.DMA((rider.nsem,))],
        compiler_params=pltpu.CompilerParams(has_side_effects=True),
    )(*rider.ins)


class _Ride:
    def __init__(self, rider, n_in, n_out, steps):
        self.rider, self.n_in, self.n_out, self.steps = rider, n_in, n_out, steps
        self.ri = len(rider.ins) if rider else 0
        self.ro = len(rider.out_shapes) if rider else 0

    def own(self, refs):
        refs = list(refs)
        a, b = self.n_in, self.n_in + self.ri + self.n_out
        tail = refs[b + self.ro:len(refs) - 2] if self.rider else refs[b + self.ro:]
        return refs[:a] + refs[a + self.ri:b] + tail

    def _args(self, refs):
        a, b = self.n_in, self.n_in + self.ri + self.n_out
        return refs[a:a + self.ri], refs[b:b + self.ro], refs[-2], refs[-1]

    def before(self, refs, step):
        if self.rider is None:
            return
        pl.when(step == 0)(functools.partial(self.rider.first, *self._args(refs)))
        if self.rider.mid is not None:
            pl.when(step == (3 * self.steps) // 4)(functools.partial(self.rider.mid, *self._args(refs)))

    def after(self, refs, step):
        if self.rider is not None and self.rider.last is not None:
            pl.when(step == self.steps - 1)(functools.partial(self.rider.last, *self._args(refs)))

    def call(self, body, name, grid, in_specs, args, out_specs, out_shape, scratch,
             sem=("parallel", "parallel", "arbitrary")):
        r = self.rider
        if r is None:
            return pl.pallas_call(body, name=name, grid=grid, in_specs=in_specs, out_specs=out_specs,
                                  out_shape=out_shape, scratch_shapes=scratch, compiler_params=_params(sem))(*args)
        return pl.pallas_call(
            body, name=name, grid=grid, in_specs=in_specs + _hbm_specs(self.ri),
            out_specs=out_specs + _hbm_specs(self.ro), out_shape=out_shape + r.out_shapes,
            input_output_aliases={self.n_in + i: self.n_out + o for i, o in r.aliases.items()},
            scratch_shapes=scratch + [pltpu.SemaphoreType.DMA((r.nsem,)), pltpu.SemaphoreType.DMA((r.nsem,))],
            compiler_params=pltpu.CompilerParams(dimension_semantics=("arbitrary",) * len(grid),
                                                 vmem_limit_bytes=VMEM_LIMIT, has_side_effects=True),
        )(*args, *r.ins)


def _core_view(a, axis):
    l, r, c = a.shape
    return a.reshape(l, 4, 2, r // 8, c) if axis == 0 else a.reshape(l, 2, r // 2, c)


def _shard_view(a):
    l, r, c = a.shape
    return a.reshape(l, 2, r // 2, c)


def _piece(ref, axis, layer, chip, core):
    if axis == 0:
        return ref.at[layer, chip, core]
    cs = ref.shape[-1] // 4
    return ref.at[layer, core, :, pl.ds(chip * cs, cs)]


def _place_shard(name, w, axis, kidx, tr):
    _, r, cdim = w.shape
    shp = [2, r, cdim]
    shp[1 + axis] *= 4
    nb = r // tr

    def body(k_ref, w_ref, o_ref):
        o_ref[...] = w_ref[...].astype(o_ref.dtype)

    if axis == 0:
        out_spec = pl.BlockSpec((None, tr, cdim), lambda l, i, k_ref: (l, k_ref[0] * nb + i, 0))
    else:
        out_spec = pl.BlockSpec((None, tr, cdim), lambda l, i, k_ref: (l, i, k_ref[0]))
    return pl.pallas_call(
        body, name=name,
        grid_spec=pltpu.PrefetchScalarGridSpec(
            num_scalar_prefetch=1, grid=(2, nb),
            in_specs=[pl.BlockSpec((None, tr, cdim), lambda l, i, k_ref: (l, i, 0))], out_specs=out_spec),
        out_shape=jax.ShapeDtypeStruct(tuple(shp), BF16),
        compiler_params=_params(("arbitrary", "arbitrary")),
    )(kidx, w)


def _gather_rider(views, axes, items):
    n = len(items)

    def first(ins, outs, send, recv, me, c, sib):
        for i, (w, l) in enumerate(items):
            for j, (px, py) in enumerate(_peers(me)):
                _remote(_piece(ins[w], axes[w], l, me, c), _piece(outs[w], axes[w], l, me, c), send, recv,
                        3 * i + j, (px, py, c)).start()

    def mid(ins, outs, send, recv, me, c, sib):
        for i, (w, l) in enumerate(items):
            for j, (px, py) in enumerate(_peers(me)):
                got = _piece(outs[w], axes[w], l, 2 * px + py, c)
                _remote(got, got, send, recv, 3 * i + j, (px, py, c)).wait_recv()
                _remote(got, got, send, recv, 3 * (n + i) + j, sib).start()

    def last(ins, outs, send, recv, me, c, sib):
        for i, (w, l) in enumerate(items):
            for j, (px, py) in enumerate(_peers(me)):
                mine, got = _piece(outs[w], axes[w], l, me, c), _piece(outs[w], axes[w], l, 2 * px + py, c)
                theirs = _piece(outs[w], axes[w], l, 2 * px + py, 1 - c)
                _remote(theirs, theirs, send, recv, 3 * (n + i) + j, sib).wait_recv()
                _remote(mine, mine, send, recv, 3 * i + j, (px, py, c)).wait_send()
                _remote(got, got, send, recv, 3 * (n + i) + j, sib).wait_send()

    return _Rider(views, [jax.ShapeDtypeStruct(v.shape, v.dtype) for v in views], {w: w for w in range(len(views))},
                  6 * n, _by_chip(first), _by_chip(mid), _by_chip(last))


def _swap_rider(views, axes, items):
    nv = len(views)

    def part(ref, w, l, core):
        return ref.at[l, :, core] if axes[w] == 0 else ref.at[l, core]

    def copies(ins, outs, send, recv):
        x, y, c = _place()
        return [_remote(part(ins[w], w, l, 1 - c), outs[nv + i], send, recv, i, (x, y, 1 - c))
                for i, (w, l) in enumerate(items)]

    def first(ins, outs, send, recv):
        for cp in copies(ins, outs, send, recv):
            cp.start()

    def last(ins, outs, send, recv):
        for cp in copies(ins, outs, send, recv):
            cp.wait()

    got = [jax.ShapeDtypeStruct(views[w].shape[1:2] + views[w].shape[3:] if axes[w] == 0 else views[w].shape[2:],
                                views[w].dtype) for w, _ in items]
    return _Rider(views, [jax.ShapeDtypeStruct(v.shape, v.dtype) for v in views] + got,
                  {w: w for w in range(nv)}, len(items), first, None, last)


def _add_core(name, view, got, axis, layer, cidx, tr):
    def body(c_ref, g_ref, r_ref, o_ref):
        o_ref[...] = (g_ref[...] + r_ref[...]).astype(o_ref.dtype)

    if axis == 0:
        _, nchip, _, pr, cdim = view.shape
        grid = (nchip, pr // tr)
        specs = [pl.BlockSpec((None, None, None, tr, cdim), lambda k, i, c_ref: (layer, k, c_ref[0], i, 0)),
                 pl.BlockSpec((None, tr, cdim), lambda k, i, c_ref: (k, i, 0))]
        out_spec = pl.BlockSpec((None, tr, cdim), lambda k, i, c_ref: (k, i, 0))
    else:
        _, _, pr, cdim = view.shape
        grid = (pr // tr,)
        specs = [pl.BlockSpec((None, None, tr, cdim), lambda i, c_ref: (layer, c_ref[0], i, 0)),
                 pl.BlockSpec((tr, cdim), lambda i, c_ref: (i, 0))]
        out_spec = pl.BlockSpec((tr, cdim), lambda i, c_ref: (i, 0))
    return pl.pallas_call(
        body, name=name,
        grid_spec=pltpu.PrefetchScalarGridSpec(num_scalar_prefetch=1, grid=grid, in_specs=specs,
                                               out_specs=out_spec),
        out_shape=jax.ShapeDtypeStruct(got.shape, BF16),
        compiler_params=_params(("arbitrary",) * len(grid)),
    )(cidx, view, got)


def _scatter_rider(sums, axes):
    def part(ref, i, chip):
        if axes[i] == 0:
            return ref.at[chip]
        cs = ref.shape[-1] // 4
        return ref.at[:, pl.ds(chip * cs, cs)]

    def copies(ins, outs, send, recv, me, c, sib):
        return [_remote(part(ins[i], i, 2 * px + py), outs[i].at[j], send, recv, 3 * i + j, (px, py, c))
                for i in range(len(sums)) for j, (px, py) in enumerate(_peers(me))]

    def first(*args):
        for cp in copies(*args):
            cp.start()

    def last(*args):
        for cp in copies(*args):
            cp.wait()

    shapes = [jax.ShapeDtypeStruct((3,) + (s.shape[1:] if ax == 0 else (s.shape[0], s.shape[1] // 4)), s.dtype)
              for s, ax in zip(sums, axes)]
    return _Rider(sums, shapes, {}, 3 * len(sums), _by_chip(first), None, _by_chip(last))


def _add_chips(name, own, got, axis, layer, kc_idx, tr, into, shard_shape):
    _, pr, pc = got.shape

    def body(k_ref, o_ref, g_ref, *rest):
        rest[-1][...] = (o_ref[...].astype(F32) + g_ref[0].astype(F32) + g_ref[1].astype(F32)
                         + g_ref[2].astype(F32))

    if axis == 0:
        own_spec = pl.BlockSpec((None, tr, pc), lambda i, k_ref: (k_ref[0], i, 0))
    else:
        own_spec = pl.BlockSpec((tr, pc), lambda i, k_ref: (i, k_ref[0]))
    specs = [own_spec, pl.BlockSpec((3, tr, pc), lambda i, k_ref: (0, i, 0))]
    args = [kc_idx, own, got]
    if into is not None:
        specs.append(pl.BlockSpec(memory_space=pl.ANY))
        args.append(into)
    return pl.pallas_call(
        body, name=name,
        grid_spec=pltpu.PrefetchScalarGridSpec(
            num_scalar_prefetch=1, grid=(pr // tr,), in_specs=specs,
            out_specs=pl.BlockSpec((None, None, tr, pc), lambda i, k_ref: (layer, k_ref[1], i, 0))),
        out_shape=jax.ShapeDtypeStruct(shard_shape, F32),
        input_output_aliases={} if into is None else {3: 0},
        compiler_params=_params(("arbitrary",)),
    )(*args)


def _join_rider(parts):
    def first(ins, outs, send, recv):
        x, y, c = _place()
        for w in range(len(parts)):
            _remote(ins[w].at[:, c], outs[w].at[:, c], send, recv, w, (x, y, 1 - c)).start()

    def last(ins, outs, send, recv):
        x, y, c = _place()
        for w in range(len(parts)):
            _remote(ins[w].at[:, c], outs[w].at[:, c], send, recv, w, (x, y, 1 - c)).wait_send()
            _remote(ins[w].at[:, c], outs[w].at[:, 1 - c], send, recv, w, (x, y, 1 - c)).wait_recv()

    return _Rider(parts, [jax.ShapeDtypeStruct(p.shape, p.dtype) for p in parts],
                  {w: w for w in range(len(parts))}, len(parts), first, None, last)


def _all_reduce_small(name, pack, lead, groups):
    nr, d = pack.shape
    nout = nr - (groups - 1) * lead

    def body(in_ref, sum_ref, mine, slots, send, recv):
        x, y, c = _place()
        me = 4 * x + 2 * y + c
        fold = in_ref[0:lead]
        for grp in range(1, groups):
            fold = fold + in_ref[grp * lead:(grp + 1) * lead]
        mine[0:lead] = fold
        mine[lead:] = in_ref[groups * lead:]
        slots[me] = mine[...]
        cps = []
        for r in range(1, 8):
            rx, ry, rc = r // 4, (r // 2) % 2, r % 2
            peer = (x + rx - 2 * x * rx, y + ry - 2 * y * ry, c + rc - 2 * c * rc)
            cp = pltpu.make_async_remote_copy(mine, slots.at[me], send.at[r - 1], recv.at[r - 1],
                                              device_id=peer, device_id_type=MESH)
            cp.start()
            cps.append(cp)
        for cp in cps:
            cp.wait()
        acc = slots[0]
        for dev in range(1, 8):
            acc = acc + slots[dev]
        sum_ref[...] = acc

    vmem = pl.BlockSpec(memory_space=pltpu.VMEM)
    return pl.pallas_call(
        body, name=name, in_specs=[vmem], out_specs=vmem, out_shape=jax.ShapeDtypeStruct((nout, d), F32),
        scratch_shapes=[pltpu.VMEM((nout, d), F32), pltpu.VMEM((8, nout, d), F32), pltpu.SemaphoreType.DMA((7,)),
                        pltpu.SemaphoreType.DMA((7,))],
        compiler_params=pltpu.CompilerParams(has_side_effects=True, vmem_limit_bytes=VMEM_LIMIT),
    )(pack)


def _adamw_math(w, g, m, v):
    m = B1 * m + (1.0 - B1) * g
    v = B2 * v + (1.0 - B2) * (g * g)
    m_hat = m / (1.0 - B1 ** STEP)
    v_hat = v / (1.0 - B2 ** STEP)
    return -LR * (m_hat / (jnp.sqrt(v_hat) + ADAM_EPS) + WD * w), m, v


def _adamw(name, w, g, m, v, tr):
    shape = w.shape
    flat = [a.reshape(-1, shape[-1]) for a in (w, g, m, v)]
    r, cdim = flat[0].shape

    def body(w_ref, g_ref, m_ref, v_ref, d_ref, nm_ref, nv_ref):
        d_ref[...], nm_ref[...], nv_ref[...] = _adamw_math(w_ref[...], g_ref[...], m_ref[...], v_ref[...])

    spec = pl.BlockSpec((tr, cdim), lambda i: (i, 0))
    outs = pl.pallas_call(
        body, name=name, grid=(r // tr,), in_specs=[spec] * 4, out_specs=[spec] * 3,
        out_shape=[jax.ShapeDtypeStruct((r, cdim), F32)] * 3,
        compiler_params=_params(("parallel",)),
    )(*flat)
    return [o.reshape(shape) for o in outs]


def _adamw_small(name, groups):
    n = len(groups)
    shapes = [grp[0].shape for grp in groups]
    flat = [a.reshape(-1, a.shape[-1]) for grp in groups for a in grp]

    def body(*refs):
        ins, outs = refs[:4 * n], refs[4 * n:]
        for i in range(n):
            w_ref, g_ref, m_ref, v_ref = ins[4 * i:4 * i + 4]
            outs[3 * i][...], outs[3 * i + 1][...], outs[3 * i + 2][...] = _adamw_math(
                w_ref[...], g_ref[...], m_ref[...], v_ref[...])

    vmem = pl.BlockSpec(memory_space=pltpu.VMEM)
    out_shape = [jax.ShapeDtypeStruct(flat[4 * i].shape, F32) for i in range(n) for _ in range(3)]
    outs = pl.pallas_call(body, name=name, in_specs=[vmem] * (4 * n), out_specs=[vmem] * (3 * n),
                          out_shape=out_shape)(*flat)
    return [[outs[3 * i + j].reshape(shapes[i]) for j in range(3)] for i in range(n)]


def _block_diag(w_grp):
    g, pg, _ = w_grp.shape
    eye = jnp.eye(g, dtype=w_grp.dtype)
    return (eye[:, None, :, None] * w_grp[:, :, None, :]).reshape(g * pg, g * pg)


def _diag_blocks(m, g):
    pg = m.shape[0] // g
    return jnp.stack([m[i * pg:(i + 1) * pg, i * pg:(i + 1) * pg] for i in range(g)])


BIG = ("w_in", "w_out", "w_up", "w_down")
AXES = (1, 0, 1, 0)
W_IN, W_OUT, W_UP, W_DOWN = range(4)


def kernel(x, meta_tokens, g_mix, w_in, w_conv, w_pool, pool_scale, w_out, g_mlp, w_up, w_down, g_final, loss_target, m_meta_tokens, m_g_mix, m_w_in, m_w_conv, m_w_pool, m_pool_scale, m_w_out, m_g_mlp, m_w_up, m_w_down, m_g_final, v_meta_tokens, v_g_mix, v_w_in, v_w_conv, v_w_pool, v_pool_scale, v_w_out, v_g_mlp, v_w_up, v_w_down, v_g_final):
    bl, s, d = x.shape
    depth = g_mix.shape[0]
    assert depth == 2
    lp = PAD + N_META + s
    t = bl * lp
    tt = lp
    tm = lp // 4
    cs = w_conv.shape[2]
    cw = 4 * cs
    ngrp = w_pool.shape[1]
    xi, yi, ci = _place()
    chip = (2 * xi + yi).astype(jnp.int32)
    cidx, kidx = ci.astype(jnp.int32).reshape(1), chip.reshape(1)
    kc_idx = jnp.stack([chip, ci.astype(jnp.int32)])
    shards = (w_in, w_out, w_up, w_down)

    views = [_core_view(_place_shard(f"place_{BIG[w]}", shards[w], AXES[w], kidx, 256), AXES[w]) for w in range(4)]

    def whole(w):
        return views[w].reshape(depth, -1, views[w].shape[-1])

    def gather_on(call, items):
        ws = sorted({w for w, _ in items})
        res = call(_gather_rider([views[w] for w in ws], [AXES[w] for w in ws],
                                 [(ws.index(w), layer) for w, layer in items]))
        for j, w in enumerate(ws):
            views[w] = res[len(res) - len(ws) + j]
        return res[:len(res) - len(ws)]

    gather_on(lambda rider: _run_rider("gather_first", rider), [(W_IN, 0)])

    placed = jnp.zeros((32, d), F32)
    placed = lax.dynamic_update_slice(placed, meta_tokens, (0, chip * meta_tokens.shape[1]))
    placed = lax.dynamic_update_slice(placed, w_conv.reshape(-1, cs), (N_META, chip * cs))
    placed = jnp.where(ci == 0, placed, 0.0)
    small = _all_reduce_small("gather_small", placed, 8, 1)
    meta_full = small[:N_META]
    conv_full = small[N_META:N_META + depth * 3, :cw].reshape(depth, 3, cw)

    h = jnp.concatenate([jnp.zeros((bl, PAD, d), F32), jnp.broadcast_to(meta_full[None], (bl, N_META, d)), x],
                        axis=1).reshape(t, d)
    wbd = [_block_diag(w_pool[i]).astype(BF16) for i in range(depth)]
    saved = []
    for i in range(depth):
        hn, u_cp, qkv = _in_proj(f"in_proj{i}", h, g_mix[i], whole(W_IN), i, tm, 4 * cw)
        y_cp = _convpool_fwd(f"convpool{i}", u_cp, conv_full[i], wbd[i], pool_scale[i:i + 1], lp, tm)
        if i == 0:
            y_at, lt, g0 = gather_on(lambda rider: _attn_fwd(f"attn{i}", qkv, bl, lp, rider),
                                     [(W_OUT, 0), (W_UP, 0), (W_DOWN, 0), (W_IN, 1)])
            h_mid = _out_proj(f"out_proj{i}", y_cp, y_at, h, whole(W_OUT), i, tm)
            w_up0 = whole(W_UP)
            hn2, m_pre, act = gather_on(lambda rider: _up_proj(f"up_proj{i}", h_mid, g_mlp[i], w_up0, i, tm, rider),
                                        [(W_OUT, 1), (W_DOWN, 1)])
            w_down0 = whole(W_DOWN)
            (h_next,) = gather_on(lambda rider: _down_proj(f"down_proj{i}", act, h_mid, w_down0, i, tm, rider),
                                  [(W_UP, 1)])
        else:
            y_at, lt, g0 = _attn_fwd(f"attn{i}", qkv, bl, lp)
            h_mid = _out_proj(f"out_proj{i}", y_cp, y_at, h, whole(W_OUT), i, tm)
            hn2, m_pre, act = _up_proj(f"up_proj{i}", h_mid, g_mlp[i], whole(W_UP), i, tm)
            (h_next,) = _down_proj(f"down_proj{i}", act, h_mid, whole(W_DOWN), i, tm)
        saved.append((h, hn, u_cp, qkv, y_cp, y_at, (lt, g0), h_mid, hn2, m_pre, act))
        h = h_next

    dh, loss8, dgf8 = _loss_bwd("loss", h, g_final, loss_target, lp)
    loss = lax.psum(jnp.sum(loss8), ("x", "y", "c"))
    per_layer = {k: [None] * depth for k in ("g_mix", "w_conv", "w_pool", "pool_scale", "g_mlp")}

    gw = [None] * 4
    sums, arrived = {}, {}

    def dw(name, a, b, w, layer):
        shape = whole(w).shape
        into = None if gw[w] is None else gw[w].reshape(shape)
        if isinstance(a, list):
            res = _mm_tn_slab(name, a, b, tt // 2, into, shape, layer)
        else:
            res = _mm_tn(name, a, b, tt, 1024, 1024, into, shape, layer, 0, 0)
        gw[w] = _core_view(res, AXES[w])

    def swap_rider(ws):
        return _swap_rider([gw[w] for w, _ in ws], [AXES[w] for w, _ in ws],
                           [(j, layer) for j, (_, layer) in enumerate(ws)])

    def swapped(ws, outs):
        for j, (w, layer) in enumerate(ws):
            gw[w] = outs[j]
            sums[w, layer] = _add_core(f"chip_sum_{BIG[w]}{layer}", gw[w], outs[len(ws) + j], AXES[w], layer, cidx,
                                       128)

    def scatter_rider(items):
        return _scatter_rider([sums[it] for it in items], [AXES[w] for w, _ in items])

    def bwd_mlp(i, dh, swap_early):
        _, _, _, _, y_cp, y_at, _, h_mid, hn2, m_pre, act = saved[i]
        dm = _down_proj_dx(f"down_proj_dx{i}", dh, m_pre, whole(W_DOWN), i, tm)
        dw(f"down_proj_dw{i}", act, dh, W_DOWN, i)
        dw(f"up_proj_dw{i}", hn2, dm, W_UP, i)
        ws = [(W_DOWN, i), (W_UP, i)] if swap_early else []
        dh_mid, dy, dg8, *outs = _up_proj_dx(f"up_proj_dx{i}", dm, h_mid, dh, g_mlp[i], whole(W_UP), whole(W_OUT), i,
                                             tm, swap_rider(ws) if ws else None)
        swapped(ws, outs)
        per_layer["g_mlp"][i] = dg8.sum(0)
        dw(f"out_proj_dw{i}", [y_cp, y_at], [dh_mid], W_OUT, i)
        return dh_mid, dy

    def bwd_mix(i, dh_mid, dy, dus3, scatter_late):
        h_in, hn, u_cp = saved[i][:3]
        du_cp, sm, dwbd = _convpool_bwd(f"convpool_bwd{i}", u_cp, dy, conv_full[i], wbd[i], pool_scale[i:i + 1], lp,
                                        tm)
        sm = sm.reshape(4, 8, cw).sum(1)
        per_layer["w_conv"][i] = sm[0:3]
        per_layer["pool_scale"][i] = sm[3]
        per_layer["w_pool"][i] = _diag_blocks(dwbd, ngrp)
        dus = [du_cp, *dus3]
        dw(f"in_proj_dw{i}", [hn], dus, W_IN, i)
        rider = None
        if scatter_late:
            swapped([(W_IN, i)], _run_rider(f"grads_swap_in{i}", swap_rider([(W_IN, i)])))
            rider = scatter_rider([(W_IN, i)])
        dh, dg8, *outs = _in_proj_dx(f"in_proj_dx{i}", dus, h_in, dh_mid, g_mix[i], whole(W_IN), i, tm, rider)
        arrived.update(zip([(W_IN, i)], outs))
        per_layer["g_mix"][i] = dg8.sum(0)
        return dh

    def attn_bwd(i, dy, rider):
        qkv, (lt, g0) = saved[i][3], saved[i][6]
        res = _attn_bwd(f"attn_bwd{i}", qkv, lt, g0, dy, bl, lp, rider)
        return res[:3], res[3:]

    dh_mid, dy = bwd_mlp(1, dh, False)
    ws = [(W_DOWN, 1), (W_UP, 1), (W_OUT, 1)]
    dus3, outs = attn_bwd(1, dy, swap_rider(ws))
    swapped(ws, outs)
    dh = bwd_mix(1, dh_mid, dy, dus3, False)

    dh_mid, dy = bwd_mlp(0, dh, True)
    ws = [(W_IN, 1), (W_OUT, 0)]
    swapped(ws, _run_rider("grads_swap0", swap_rider(ws)))
    items = list(sums)
    dus3, outs = attn_bwd(0, dy, scatter_rider(items))
    arrived.update(zip(items, outs))
    dh0 = bwd_mix(0, dh_mid, dy, dus3, True)

    finals = []
    for w in range(4):
        rs_, cs_ = shards[w].shape[1:]
        part = None
        for layer in reversed(range(depth)):
            part = _add_chips(f"reduce_{BIG[w]}{layer}", sums[w, layer], arrived[w, layer], AXES[w], layer, kc_idx,
                              128, part, (depth, 2, rs_ // 2, cs_))
        finals.append(part)
    finals = _run_rider("grads_join", _join_rider(finals))
    grad = {BIG[w]: finals[w].reshape(shards[w].shape) for w in range(4)}

    dh0 = dh0.reshape(bl, lp, d)
    grad_x = dh0[:, PAD + N_META:]
    local = {k: jnp.stack(v) for k, v in per_layer.items()}
    pieces = [dh0[:, PAD:PAD + N_META].reshape(bl * N_META, d), local["g_mix"], local["g_mlp"],
              dgf8.sum(0).reshape(1, d),
              jnp.pad(local["w_conv"].reshape(-1), (0, 2 * d - local["w_conv"].size)).reshape(2, d),
              jnp.pad(local["pool_scale"].reshape(-1), (0, d - local["pool_scale"].size)).reshape(1, d),
              local["w_pool"].reshape(-1, d)]
    summed = _all_reduce_small("small_grads", jnp.concatenate(pieces, axis=0), N_META, bl)
    o = N_META
    grad.update({
        "meta_tokens": lax.dynamic_slice_in_dim(summed[:o], chip * meta_tokens.shape[1], meta_tokens.shape[1], 1),
        "g_mix": summed[o:o + 2], "g_mlp": summed[o + 2:o + 4], "g_final": summed[o + 4],
        "w_conv": lax.dynamic_slice_in_dim(summed[o + 5:o + 7].reshape(-1)[:2 * 3 * cw].reshape(2, 3, cw),
                                           chip * cs, cs, 2),
        "pool_scale": summed[o + 7].reshape(-1)[:pool_scale.size].reshape(pool_scale.shape),
        "w_pool": summed[o + 8:].reshape(w_pool.shape),
    })

    weights = dict(meta_tokens=meta_tokens, g_mix=g_mix, w_in=w_in, w_conv=w_conv, w_pool=w_pool,
                   pool_scale=pool_scale, w_out=w_out, g_mlp=g_mlp, w_up=w_up, w_down=w_down, g_final=g_final)
    ms = dict(meta_tokens=m_meta_tokens, g_mix=m_g_mix, w_in=m_w_in, w_conv=m_w_conv, w_pool=m_w_pool,
              pool_scale=m_pool_scale, w_out=m_w_out, g_mlp=m_g_mlp, w_up=m_w_up, w_down=m_w_down,
              g_final=m_g_final)
    vs = dict(meta_tokens=v_meta_tokens, g_mix=v_g_mix, w_in=v_w_in, w_conv=v_w_conv, w_pool=v_w_pool,
              pool_scale=v_pool_scale, w_out=v_w_out, g_mlp=v_g_mlp, w_up=v_w_up, w_down=v_w_down,
              g_final=v_g_final)
    order = list(weights)
    upd = {k: _adamw(f"adamw_{k}", weights[k], grad[k], ms[k], vs[k], 256) for k in BIG}
    little = [k for k in order if k not in BIG]
    for k, res in zip(little, _adamw_small("adamw_small", [(weights[k], grad[k].reshape(weights[k].shape), ms[k],
                                                            vs[k]) for k in little])):
        upd[k] = res
    grad = {k: grad[k].reshape(weights[k].shape) for k in order}
    return (loss, grad_x, *[grad[k] for k in order], *[upd[k][0] for k in order], *[upd[k][1] for k in order],
            *[upd[k][2] for k in order])
```

```python
import functools

import jax
import jax.numpy as jnp
from jax import lax
from jax.experimental import pallas as pl
from jax.experimental.pallas import tpu as pltpu

F32, BF16 = jnp.float32, jnp.bfloat16
MESH = pl.DeviceIdType.MESH
EPS = 1e-6
N_META = 16
QB = 128
PAD = QB - N_META
HALO = 16
POOL_WINDOWS = (2.0, 4.0, 8.0, 16.0)
HEAD_SCALE = 0.125
LR, B1, B2, ADAM_EPS, WD, STEP = 0.001, 0.9, 0.999, 1e-08, 0.01, 10
VMEM_LIMIT = 56 * 1024 * 1024


def _params(sem=None):
    return pltpu.CompilerParams(dimension_semantics=sem, vmem_limit_bytes=VMEM_LIMIT)


def _nt(a, b):
    return lax.dot_general(a, b, (((1,), (1,)), ((), ())), preferred_element_type=F32)


def _tn(a, b):
    return lax.dot_general(a, b, (((0,), (0,)), ((), ())), preferred_element_type=F32)


def _nn(a, b):
    return jnp.dot(a, b, preferred_element_type=F32)


def _fold8(v):
    r, c = v.shape
    return jnp.sum(v.reshape(r // 8, 8, c), axis=0)


NCH = 512


def _rows_call(name, body, tm, row_ins, consts, row_outs, accs=(), rider=None):
    t = row_ins[0].shape[0]
    ride = _Ride(rider, len(row_ins) + len(consts), len(row_outs) + len(accs), t // tm)

    def stepped(*refs):
        step = pl.program_id(0)
        ride.before(refs, step)
        body(*ride.own(refs))
        ride.after(refs, step)

    in_specs = [pl.BlockSpec((tm, a.shape[1]), lambda i: (i, 0)) for a in row_ins]
    for a, layer in consts:
        if layer is None:
            in_specs.append(pl.BlockSpec(a.shape, lambda i: (0, 0)))
        else:
            in_specs.append(pl.BlockSpec((None, *a.shape[1:]), lambda i, l=layer: (l, 0, 0)))
    return ride.call(
        stepped, name, (t // tm,), in_specs, [*row_ins, *[a for a, _ in consts]],
        [pl.BlockSpec((tm, c), lambda i: (i, 0)) for c, _ in row_outs]
        + [pl.BlockSpec(s, lambda i: (0, 0)) for s in accs],
        [jax.ShapeDtypeStruct((t, c), dt) for c, dt in row_outs] + [jax.ShapeDtypeStruct(s, F32) for s in accs],
        [], ("arbitrary",) if accs else ("parallel",))


def _norm_parts(x):
    r = lax.rsqrt(jnp.mean(x * x, axis=-1, keepdims=True) + EPS)
    return r, x * r


def _norm_bwd(r, xh, dyn, g):
    w = dyn * g
    return r * (w - xh * jnp.mean(w * xh, axis=-1, keepdims=True))


def _in_proj(name, h, g, w, layer, tm, ncp):
    d, n = h.shape[1], w.shape[2]

    def body(h_ref, g_ref, w_ref, hn_ref, ucp_ref, qkv_ref):
        _, xh = _norm_parts(h_ref[...])
        hn = (xh * g_ref[...]).astype(BF16)
        hn_ref[...] = hn
        for n0 in range(0, n, NCH):
            acc = _nn(hn, w_ref[:, n0:n0 + NCH])
            if n0 < ncp:
                ucp_ref[:, n0:n0 + NCH] = acc
            else:
                qkv_ref[:, n0 - ncp:n0 - ncp + NCH] = acc.astype(BF16)

    return _rows_call(name, body, tm, [h], [(g.reshape(1, d), None), (w, layer)],
                      [(d, BF16), (ncp, F32), (n - ncp, BF16)])


def _out_proj(name, y_cp, y_at, h, w, layer, tm):
    d, k1 = h.shape[1], y_cp.shape[1]

    def body(ycp_ref, yat_ref, h_ref, w_ref, o_ref):
        for n0 in range(0, d, NCH):
            o_ref[:, n0:n0 + NCH] = (h_ref[:, n0:n0 + NCH] + _nn(ycp_ref[...], w_ref[0:k1, n0:n0 + NCH])
                                     + _nn(yat_ref[...], w_ref[k1:, n0:n0 + NCH]))

    return _rows_call(name, body, tm, [y_cp, y_at, h], [(w, layer)], [(d, F32)])[0]


def _up_proj(name, h_mid, g, w, layer, tm, rider=None):
    d, n = h_mid.shape[1], w.shape[2]

    def body(h_ref, g_ref, w_ref, hn_ref, m_ref, act_ref):
        _, xh = _norm_parts(h_ref[...])
        hn = (xh * g_ref[...]).astype(BF16)
        hn_ref[...] = hn
        for n0 in range(0, n, NCH):
            acc = _nn(hn, w_ref[:, n0:n0 + NCH])
            m_ref[:, n0:n0 + NCH] = acc.astype(BF16)
            act_ref[:, n0:n0 + NCH] = jnp.square(jnp.maximum(acc, 0.0)).astype(BF16)

    return _rows_call(name, body, tm, [h_mid], [(g.reshape(1, d), None), (w, layer)],
                      [(d, BF16), (n, BF16), (n, BF16)], rider=rider)


def _down_proj(name, act, h_mid, w, layer, tm, rider=None):
    d = h_mid.shape[1]

    def body(a_ref, h_ref, w_ref, o_ref):
        for n0 in range(0, d, NCH):
            o_ref[:, n0:n0 + NCH] = h_ref[:, n0:n0 + NCH] + _nn(a_ref[...], w_ref[:, n0:n0 + NCH])

    return _rows_call(name, body, tm, [act, h_mid], [(w, layer)], [(d, F32)], rider=rider)


def _down_proj_dx(name, dh, m_pre, w, layer, tm):
    n = w.shape[1]

    def body(dh_ref, m_ref, w_ref, dm_ref):
        dhb = dh_ref[...].astype(BF16)
        for n0 in range(0, n, NCH):
            dm_ref[:, n0:n0 + NCH] = (_nt(dhb, w_ref[n0:n0 + NCH, :])
                                      * (2.0 * jnp.maximum(m_ref[:, n0:n0 + NCH].astype(F32), 0.0))).astype(BF16)

    return _rows_call(name, body, tm, [dh, m_pre], [(w, layer)], [(n, BF16)])[0]


def _up_proj_dx(name, dm, h_mid, dh, g, w_up, w_out, layer, tm, rider=None):
    d = h_mid.shape[1]

    def body(dm_ref, h_ref, dh_ref, g_ref, wup_ref, wout_ref, dhm_ref, dy_ref, dg_ref):
        @pl.when(pl.program_id(0) == 0)
        def _():
            dg_ref[...] = jnp.zeros_like(dg_ref)
        dyn = _nt(dm_ref[...], wup_ref[...])
        r, xh = _norm_parts(h_ref[...])
        dhm = dh_ref[...] + _norm_bwd(r, xh, dyn, g_ref[...])
        dhm_ref[...] = dhm
        dg_ref[...] += _fold8(dyn * xh)
        dy_ref[...] = _nt(dhm.astype(BF16), wout_ref[...])

    return _rows_call(name, body, tm, [dm, h_mid, dh], [(g.reshape(1, d), None), (w_up, layer), (w_out, layer)],
                      [(d, F32), (w_out.shape[1], F32)], [(8, d)], rider)


def _in_proj_dx(name, dus, h, dh_mid, g, w, layer, tm, rider=None):
    d = h.shape[1]
    ns = [du.shape[1] for du in dus]
    nd = len(dus)

    def body(*refs):
        du_refs = refs[:nd]
        h_ref, dhm_ref, g_ref, w_ref, dh_ref, dg_ref = refs[nd:]

        @pl.when(pl.program_id(0) == 0)
        def _():
            dg_ref[...] = jnp.zeros_like(dg_ref)
        dyn, off = None, 0
        for du_ref, n in zip(du_refs, ns):
            part = _nt(du_ref[...], w_ref[:, off:off + n])
            dyn = part if dyn is None else dyn + part
            off += n
        r, xh = _norm_parts(h_ref[...])
        dh_ref[...] = dhm_ref[...] + _norm_bwd(r, xh, dyn, g_ref[...])
        dg_ref[...] += _fold8(dyn * xh)

    return _rows_call(name, body, tm, [*dus, h, dh_mid], [(g.reshape(1, d), None), (w, layer)], [(d, F32)],
                      [(8, d)], rider)


def _mm_tn(name, a, b, tt, tka, tn, into, shape, layer, row_off, col_off):
    t, ka = a.shape
    n = b.shape[1]
    assert t % tt == 0 and ka % tka == 0 and n % tn == 0 and row_off % tka == 0 and col_off % tn == 0

    def body(a_ref, b_ref, *rest):
        o_ref = rest[-1]

        @pl.when(pl.program_id(2) == 0)
        def _():
            o_ref[...] = jnp.zeros_like(o_ref)
        o_ref[...] += _tn(a_ref[...].astype(BF16), b_ref[...].astype(BF16))

    in_specs = [pl.BlockSpec((tt, tka), lambda i, j, s: (s, i)), pl.BlockSpec((tt, tn), lambda i, j, s: (s, j))]
    args = [a, b]
    if into is not None:
        in_specs.append(pl.BlockSpec(memory_space=pl.ANY))
        args.append(into)
    return pl.pallas_call(
        body, name=name, grid=(ka // tka, n // tn, t // tt), in_specs=in_specs,
        out_specs=pl.BlockSpec((None, tka, tn), lambda i, j, s: (layer, row_off // tka + i, col_off // tn + j)),
        out_shape=jax.ShapeDtypeStruct(shape, F32),
        input_output_aliases={} if into is None else {2: 0},
        compiler_params=_params(("parallel", "parallel", "arbitrary")),
    )(*args)


def _mm_tn_slab(name, a_list, b_list, tt, into, shape, layer):
    t = a_list[0].shape[0]
    kas, ns = [a.shape[1] for a in a_list], [b.shape[1] for b in b_list]
    assert t % tt == 0 and (sum(kas), sum(ns)) == tuple(shape[1:])
    na, nb = len(a_list), len(b_list)

    def body(*refs):
        o_ref = refs[-1]

        @pl.when(pl.program_id(0) == 0)
        def _():
            o_ref[...] = jnp.zeros_like(o_ref)
        r0 = 0
        for a_ref, ka in zip(refs[:na], kas):
            a = a_ref[...].astype(BF16)
            c0 = 0
            for b_ref, n in zip(refs[na:na + nb], ns):
                o_ref[r0:r0 + ka, c0:c0 + n] += _tn(a, b_ref[...].astype(BF16))
                c0 += n
            r0 += ka

    in_specs = [pl.BlockSpec((tt, c), lambda s: (s, 0)) for c in kas + ns]
    args = [*a_list, *b_list]
    if into is not None:
        in_specs.append(pl.BlockSpec(memory_space=pl.ANY))
        args.append(into)
    return pl.pallas_call(
        body, name=name, grid=(t // tt,), in_specs=in_specs,
        out_specs=pl.BlockSpec((None, *shape[1:]), lambda s: (layer, 0, 0)),
        out_shape=jax.ShapeDtypeStruct(shape, F32),
        input_output_aliases={} if into is None else {na + nb: 0},
        compiler_params=_params(("arbitrary",)),
    )(*args)


def _build_h(name, x, meta, lp, rider=None):
    bl, s, d = x.shape
    nq = lp // QB
    ride = _Ride(rider, 2, 1, bl * nq)

    def body(*refs):
        x_ref, m_ref, o_ref = ride.own(refs)
        j = pl.program_id(1)
        step = pl.program_id(0) * nq + j
        ride.before(refs, step)
        head = jnp.concatenate([jnp.zeros((PAD, d), F32), m_ref[...]], axis=0)
        o_ref[...] = jnp.where(j == 0, head, x_ref[...])
        ride.after(refs, step)

    return ride.call(
        body, name, (bl, nq),
        [pl.BlockSpec((None, QB, d), lambda b, j: (b, jnp.maximum(j - 1, 0), 0)),
         pl.BlockSpec(meta.shape, lambda b, j: (0, 0))], [x, meta],
        [pl.BlockSpec((QB, d), lambda b, j: (b * nq + j, 0))], [jax.ShapeDtypeStruct((bl * lp, d), F32)], [],
        ("parallel", "arbitrary"))


def _loss_bwd(name, h, g, target, lp):
    t, d = h.shape
    bl = target.shape[0]
    nq = lp // QB

    def body(h_ref, g_ref, t_ref, dh_ref, ls_ref, dg_ref):
        b, j = pl.program_id(0), pl.program_id(1)

        @pl.when((b == 0) & (j == 0))
        def _():
            ls_ref[...] = jnp.zeros_like(ls_ref)
            dg_ref[...] = jnp.zeros_like(dg_ref)
        xv = h_ref[...]
        r = lax.rsqrt(jnp.mean(xv * xv, axis=-1, keepdims=True) + EPS)
        xh = xv * r
        gv = g_ref[...]
        err = jnp.where(j >= 1, xh * gv - t_ref[...], 0.0)
        ls_ref[...] += _fold8(err * err) * (0.5 / d)
        dy = err * (1.0 / d)
        w = dy * gv
        dh_ref[...] = r * (w - xh * jnp.mean(w * xh, axis=-1, keepdims=True))
        dg_ref[...] += _fold8(dy * xh)

    return pl.pallas_call(
        body, name=name, grid=(bl, nq),
        in_specs=[pl.BlockSpec((QB, d), lambda b, j: (b * nq + j, 0)), pl.BlockSpec((1, d), lambda b, j: (0, 0)),
                  pl.BlockSpec((None, QB, d), lambda b, j: (b, jnp.maximum(j - 1, 0), 0))],
        out_specs=[pl.BlockSpec((QB, d), lambda b, j: (b * nq + j, 0)), pl.BlockSpec((8, d), lambda b, j: (0, 0)),
                   pl.BlockSpec((8, d), lambda b, j: (0, 0))],
        out_shape=[jax.ShapeDtypeStruct((t, d), F32), jax.ShapeDtypeStruct((8, d), F32),
                   jax.ShapeDtypeStruct((8, d), F32)],
        compiler_params=_params(("arbitrary", "arbitrary")),
    )(h, g.reshape(1, d), target)


def _pool_select(grp, a2, a4, a8, a16):
    return jnp.where(grp == 0, a2, jnp.where(grp == 1, a4, jnp.where(grp == 2, a8, a16)))


def _trailing_sums(v):
    s2 = v + pltpu.roll(v, 1, 0)
    s4 = s2 + pltpu.roll(s2, 2, 0)
    s8 = s4 + pltpu.roll(s4, 4, 0)
    s16 = s8 + pltpu.roll(s8, 8, 0)
    return s2, s4, s8, s16


def _leading_sums(v):
    n = v.shape[0]
    s2 = v + pltpu.roll(v, n - 1, 0)
    s4 = s2 + pltpu.roll(s2, n - 2, 0)
    s8 = s4 + pltpu.roll(s4, n - 4, 0)
    s16 = s8 + pltpu.roll(s8, n - 8, 0)
    return s2, s4, s8, s16


def _convpool_fwd(name, u_cp, wconv, wbd, pscale, lp, r):
    t = u_cp.shape[0]
    cw = u_cp.shape[1] // 4
    tps, hb = lp // r, r // HALO

    def body(cb_ref, cc_ref, cx_ref, pi_ref, cch_ref, cxh_ref, pih_ref, wc_ref, wbd_ref, ps_ref, y_ref):
        i = pl.program_id(0)
        lrow = (i % tps) * r + lax.broadcasted_iota(jnp.int32, (r, 1), 0)
        valid = lrow >= PAD
        xx = jnp.concatenate([cch_ref[...] * cxh_ref[...], cc_ref[...] * cx_ref[...]], axis=0)
        conv = (wc_ref[0:1, :] * pltpu.roll(xx, 2, 0) + wc_ref[1:2, :] * pltpu.roll(xx, 1, 0)
                + wc_ref[2:3, :] * xx)
        y_ref[:, 0:cw] = (cb_ref[...] * conv[HALO:]).astype(y_ref.dtype)
        p = pi_ref[...]
        grp = lax.broadcasted_iota(jnp.int32, (1, cw), 1) // (cw // 4)
        sel = _pool_select(grp, *_trailing_sums(jnp.concatenate([pih_ref[...], p], axis=0)))[HALO:]
        cnt = jnp.maximum(jnp.minimum((lrow - (PAD - 1)).astype(F32), _pool_select(grp, *POOL_WINDOWS)), 1.0)
        pooled = jnp.where(valid, sel / cnt - p, 0.0)
        y_ref[:, cw:2 * cw] = (_nn(pooled.astype(BF16), wbd_ref[...]) * ps_ref[...]).astype(y_ref.dtype)

    def main(col):
        return pl.BlockSpec((r, cw), lambda i: (i, col))

    def prev(col):
        return pl.BlockSpec((HALO, cw), lambda i: (jnp.maximum(i * hb - 1, 0), col))

    def whole(a):
        return pl.BlockSpec(a.shape, lambda i: (0, 0))

    return pl.pallas_call(
        body, name=name, grid=(t // r,),
        in_specs=[main(0), main(1), main(2), main(3), prev(1), prev(2), prev(3), whole(wconv), whole(wbd),
                  whole(pscale)],
        out_specs=pl.BlockSpec((r, 2 * cw), lambda i: (i, 0)),
        out_shape=jax.ShapeDtypeStruct((t, 2 * cw), BF16),
        compiler_params=_params(("parallel",)),
    )(u_cp, u_cp, u_cp, u_cp, u_cp, u_cp, u_cp, wconv, wbd, pscale)


def _convpool_bwd(name, u_cp, dy, wconv, wbd, pscale, lp, r):
    t = u_cp.shape[0]
    cw = u_cp.shape[1] // 4
    tps, hb = lp // r, r // HALO
    e = r + HALO

    def body(cb_ref, cc_ref, cx_ref, pi_ref, cbn_ref, cch_ref, cxh_ref, pih_ref, dyc_ref, dyp_ref, dycn_ref,
             dypn_ref, wc_ref, wbd_ref, ps_ref, du_ref, sm_ref, dwbd_ref):
        i = pl.program_id(0)

        @pl.when(i == 0)
        def _():
            sm_ref[...] = jnp.zeros_like(sm_ref)
            dwbd_ref[...] = jnp.zeros_like(dwbd_ref)
        lrow_e = (i % tps) * r + lax.broadcasted_iota(jnp.int32, (e, 1), 0)
        valid_e = (lrow_e >= PAD) & (lrow_e < lp)
        lrow, valid = lrow_e[:r], lrow_e[:r] >= PAD
        w0, w1, w2 = wc_ref[0:1, :], wc_ref[1:2, :], wc_ref[2:3, :]
        cb, cc, cx = cb_ref[...], cc_ref[...], cx_ref[...]
        prod = cc * cx
        xx = jnp.concatenate([cch_ref[...] * cxh_ref[...], prod], axis=0)
        back1, back2 = pltpu.roll(xx, 1, 0)[HALO:], pltpu.roll(xx, 2, 0)[HALO:]
        dyc = dyc_ref[...]
        du_ref[:, 0:cw] = (dyc * (w0 * back2 + w1 * back1 + w2 * prod)).astype(du_ref.dtype)
        dconv_e = jnp.where(valid_e, jnp.concatenate([dyc * cb, dycn_ref[...] * cbn_ref[...]], axis=0), 0.0)
        dconv = dconv_e[:r]
        dprod = (w2 * dconv + w1 * pltpu.roll(dconv_e, e - 1, 0)[:r] + w0 * pltpu.roll(dconv_e, e - 2, 0)[:r])
        du_ref[:, cw:2 * cw] = (dprod * cx).astype(du_ref.dtype)
        du_ref[:, 2 * cw:3 * cw] = (dprod * cc).astype(du_ref.dtype)
        sm_ref[0:8, :] += _fold8(dconv * back2)
        sm_ref[8:16, :] += _fold8(dconv * back1)
        sm_ref[16:24, :] += _fold8(dconv * prod)
        p = pi_ref[...]
        grp = lax.broadcasted_iota(jnp.int32, (1, cw), 1) // (cw // 4)
        win = _pool_select(grp, *POOL_WINDOWS)
        sel = _pool_select(grp, *_trailing_sums(jnp.concatenate([pih_ref[...], p], axis=0)))[HALO:]
        cnt_e = jnp.maximum(jnp.minimum((lrow_e - (PAD - 1)).astype(F32), win), 1.0)
        pooled = jnp.where(valid, sel / cnt_e[:r] - p, 0.0).astype(BF16)
        dyp = dyp_ref[...]
        sm_ref[24:32, :] += _fold8(dyp * _nn(pooled, wbd_ref[...]))
        dpre_e = (jnp.concatenate([dyp, dypn_ref[...]], axis=0) * ps_ref[...]).astype(BF16)
        dwbd_ref[...] += _tn(pooled, dpre_e[:r])
        dpooled_e = jnp.where(valid_e, _nt(dpre_e, wbd_ref[...]), 0.0)
        ahead = _pool_select(grp, *_leading_sums(dpooled_e / cnt_e))[:r]
        du_ref[:, 3 * cw:4 * cw] = (ahead - dpooled_e[:r]).astype(du_ref.dtype)

    last_halo = t // HALO - 1

    def main(col):
        return pl.BlockSpec((r, cw), lambda i: (i, col))

    def prev(col):
        return pl.BlockSpec((HALO, cw), lambda i: (jnp.maximum(i * hb - 1, 0), col))

    def nxt(col):
        return pl.BlockSpec((HALO, cw), lambda i: (jnp.minimum((i + 1) * hb, last_halo), col))

    def whole(a):
        return pl.BlockSpec(a.shape, lambda i: (0, 0))

    return pl.pallas_call(
        body, name=name, grid=(t // r,),
        in_specs=[main(0), main(1), main(2), main(3), nxt(0), prev(1), prev(2), prev(3), main(0), main(1), nxt(0),
                  nxt(1), whole(wconv), whole(wbd), whole(pscale)],
        out_specs=[pl.BlockSpec((r, 4 * cw), lambda i: (i, 0)), pl.BlockSpec((32, cw), lambda i: (0, 0)),
                   pl.BlockSpec((cw, cw), lambda i: (0, 0))],
        out_shape=[jax.ShapeDtypeStruct((t, 4 * cw), BF16), jax.ShapeDtypeStruct((32, cw), F32),
                   jax.ShapeDtypeStruct((cw, cw), F32)],
        compiler_params=_params(("arbitrary",)),
    )(u_cp, u_cp, u_cp, u_cp, u_cp, u_cp, u_cp, u_cp, dy, dy, dy, dy, wconv, wbd, pscale)


KW = 2 * QB
HP = 4
DECAY = 64.0


def _cumsum_matrix(before, kw):
    r = lax.broadcasted_iota(jnp.int32, (kw, kw), 0)
    c = lax.broadcasted_iota(jnp.int32, (kw, kw), 1)
    return ((r < c) if before else (r > c)).astype(BF16)


def _running(v, mat):
    m = v.shape[0]
    hi = v.astype(BF16)
    ext = _nn(jnp.concatenate([hi, (v - hi.astype(F32)).astype(BF16)], axis=0), mat)
    return ext[:m] + ext[m:]


def _log_sigmoid(z):
    neg_abs = lax.bitcast_convert_type(lax.bitcast_convert_type(z, jnp.int32) | jnp.int32(-2 ** 31), F32)
    return jnp.minimum(z, 0.0) - jnp.log(1.0 + jnp.exp(neg_abs))


def _stack_heads(v, head0):
    zero = jnp.zeros_like(v)
    return jnp.concatenate([jnp.where(head0, v, zero), jnp.where(head0, zero, v)], axis=0)


def _lanes(hp):
    return slice(hp * QB, (hp + 1) * QB)


def _attn_fwd(name, qkv, bl, lp, rider=None):
    t = qkv.shape[0]
    nq, nblk = lp // QB, qkv.shape[1] // (3 * HP * QB)
    assert nblk == 1
    ride = _Ride(rider, 3, 3, bl * nblk * nq)

    def body(*refs):
        q_ref, k_ref, v_ref, o_ref, lt_ref, g0_ref = ride.own(refs)
        qi = pl.program_id(2)
        step = (pl.program_id(0) * nblk + pl.program_id(1)) * nq + qi
        ride.before(refs, step)
        head0 = lax.broadcasted_iota(jnp.int32, (QB, QB), 1) < QB // 2
        q2 = [_stack_heads(q_ref[:, _lanes(hp)] * jnp.asarray(HEAD_SCALE, BF16), head0) for hp in range(HP)]
        later = {KW: _cumsum_matrix(False, KW)}
        q_pos = qi * QB + (lax.broadcasted_iota(jnp.int32, (2 * QB, KW), 0) & (QB - 1))
        col = lax.broadcasted_iota(jnp.int32, (2 * QB, KW), 1)
        rest = qi // 2

        def group(start, kw, carry, masked, lo=0):
            start = pl.multiple_of(start, QB)
            if masked:
                k_pos = start + col[:, :kw]
                valid = (k_pos < q_pos[:, :kw]) & (k_pos >= jnp.maximum(lo, PAD))
            z = [_nt(q2[hp], k_ref[pl.ds(start, kw), _lanes(hp)]) for hp in range(HP)]
            logp, after, rs = [], [], []
            for hp in range(HP):
                lp_ = _log_sigmoid(z[hp])
                lk = lp_ - z[hp]
                if masked:
                    lk = jnp.where(valid, lk, 0.0)
                logp.append(lp_)
                rs.append(jnp.sum(lk, axis=1, keepdims=True))
                after.append(_running(lk, later[kw]))
            out = []
            for hp in range(HP):
                run, acc = carry[2 * hp], carry[2 * hp + 1]
                a = jnp.exp(logp[hp] + after[hp] + run)
                if masked:
                    a = jnp.where(valid, a, 0.0)
                out += [run + rs[hp], acc + _nn(a.astype(BF16), v_ref[pl.ds(start, kw), _lanes(hp)])]
            return tuple(out)

        def alive(carry):
            most = carry[0]
            for hp in range(1, HP):
                most = jnp.maximum(most, carry[2 * hp])
            return jnp.max(most) > -DECAY

        carry = (jnp.zeros((2 * QB, 1), F32), jnp.zeros((2 * QB, QB), F32)) * HP
        carry = group(jnp.minimum(rest * KW, lp - KW), KW, carry, True, lo=rest * KW)
        g, *carry = lax.while_loop(lambda st: (st[0] >= 1) & alive(st[1:]),
                                   lambda st: (st[0] - 1, *group(st[0] * KW, KW, tuple(st[1:]), False)),
                                   (rest - 1, *carry))
        oldest = (g == 0) & (rest >= 1) & alive(carry)
        carry = lax.fori_loop(0, oldest.astype(jnp.int32), lambda i, c: group(0, KW, c, True), tuple(carry))
        g0_ref[pl.program_id(0), qi] = jnp.where(oldest, 0, g + 1).astype(F32)
        for hp in range(HP):
            run, acc = carry[2 * hp], carry[2 * hp + 1]
            o_ref[:, _lanes(hp)] = jnp.where(head0, acc[:QB], acc[QB:]).astype(o_ref.dtype)
            lt_ref[:, _lanes(hp)] = jnp.where(head0, run[:QB], run[QB:])
        ride.after(refs, step)

    wb = HP * QB
    blk = pl.BlockSpec((QB, wb), lambda b, p, i: (b * nq + i, p))
    return ride.call(
        body, name, (bl, nblk, nq),
        [blk, pl.BlockSpec((lp, wb), lambda b, p, i: (b, nblk + p)),
         pl.BlockSpec((lp, wb), lambda b, p, i: (b, 2 * nblk + p))], [qkv, qkv, qkv],
        [blk, blk, pl.BlockSpec(memory_space=pltpu.SMEM)],
        [jax.ShapeDtypeStruct((t, nblk * wb), BF16), jax.ShapeDtypeStruct((t, nblk * wb), F32),
         jax.ShapeDtypeStruct((bl, nq), F32)], [])


def _attn_bwd(name, qkv, lt, g0, dy, bl, lp, rider=None):
    t = qkv.shape[0]
    nq, nblk = lp // QB, qkv.shape[1] // (3 * HP * QB)
    ride = _Ride(rider, 6, 3, bl * nblk * nq)

    def body(*refs):
        q_ref, k_ref, v_ref, lt_ref, do_ref, g0_ref, dq_ref, dk_ref, dv_ref, dk_acc, dv_acc = ride.own(refs)
        qi = pl.program_id(2)
        step = (pl.program_id(0) * nblk + pl.program_id(1)) * nq + qi
        ride.before(refs, step)

        @pl.when(qi == 0)
        def _():
            dk_acc[...] = jnp.zeros_like(dk_acc)
            dv_acc[...] = jnp.zeros_like(dv_acc)
        lane = lax.broadcasted_iota(jnp.int32, (QB, QB), 1)
        head0 = lane < QB // 2
        q2, do2, total = [], [], []
        for hp in range(HP):
            q2.append(_stack_heads(q_ref[:, _lanes(hp)] * jnp.asarray(HEAD_SCALE, BF16), head0))
            do2.append(_stack_heads(do_ref[:, _lanes(hp)].astype(BF16), head0))
            ltv = lt_ref[:, _lanes(hp)]
            total.append(jnp.concatenate(
                [jnp.sum(jnp.where(lane == 0, ltv, 0.0), axis=1, keepdims=True),
                 jnp.sum(jnp.where(lane == QB // 2, ltv, 0.0), axis=1, keepdims=True)], axis=0))
        later = {KW: _cumsum_matrix(False, KW), QB: _cumsum_matrix(False, QB)}
        earlier = {KW: _cumsum_matrix(True, KW), QB: _cumsum_matrix(True, QB)}
        q_pos = qi * QB + (lax.broadcasted_iota(jnp.int32, (2 * QB, KW), 0) & (QB - 1))
        col = lax.broadcasted_iota(jnp.int32, (2 * QB, KW), 1)
        rest = qi // 2

        def group(start, kw, carry, masked):
            start = pl.multiple_of(start, QB)
            if masked:
                k_pos = start + col[:, :kw]
                valid = (k_pos < q_pos[:, :kw]) & (k_pos >= PAD)
            hps = range(HP)
            kg = [k_ref[pl.ds(start, kw), _lanes(hp)] for hp in hps]
            z = [_nt(q2[hp], kg[hp]) for hp in hps]
            da = [_nt(do2[hp], v_ref[pl.ds(start, kw), _lanes(hp)]) for hp in hps]
            logp, sig, after, rs = [], [], [], []
            for hp in hps:
                lp_ = _log_sigmoid(z[hp])
                lk = lp_ - z[hp]
                if masked:
                    lk = jnp.where(valid, lk, 0.0)
                logp.append(lp_)
                sig.append(jnp.exp(lp_))
                rs.append(jnp.sum(lk, axis=1, keepdims=True))
                after.append(_running(lk, later[kw]))
            a, gg, before = [], [], []
            for hp in hps:
                a_ = jnp.exp(logp[hp] + after[hp] + (total[hp] - carry[3 * hp] - rs[hp]))
                if masked:
                    a_ = jnp.where(valid, a_, 0.0)
                a.append(a_.astype(BF16))
                gg.append(a_ * da[hp])
                before.append(_nn(gg[hp].astype(BF16), earlier[kw]))
            out = []
            for hp in hps:
                seen, gsum, dq = carry[3 * hp], carry[3 * hp + 1], carry[3 * hp + 2]
                dz = gg[hp] - (gg[hp] + before[hp] + gsum) * sig[hp]
                if masked:
                    dz = jnp.where(valid, dz, 0.0)
                dz = dz.astype(BF16)
                dk_acc[pl.ds(start, kw), _lanes(hp)] += _tn(dz, q2[hp])
                dv_acc[pl.ds(start, kw), _lanes(hp)] += _tn(a[hp], do2[hp])
                out += [seen + rs[hp], gsum + jnp.sum(gg[hp], axis=1, keepdims=True), dq + _nn(dz, kg[hp])]
            return tuple(out)

        col0 = jnp.zeros((2 * QB, 1), F32)
        carry = (col0, col0, jnp.zeros((2 * QB, QB), F32)) * HP
        first = g0_ref[pl.program_id(0), qi].astype(jnp.int32)
        odd = qi % 2
        carry = lax.fori_loop(0, ((first == 0) & (rest >= 1)).astype(jnp.int32), lambda i, c: group(0, KW, c, True),
                              carry)
        carry = lax.fori_loop(jnp.maximum(first, 1), rest, lambda g, c: group(g * KW, KW, c, False), carry)
        carry = lax.fori_loop(0, odd, lambda i, c: group(rest * KW, KW, c, True), carry)
        carry = lax.fori_loop(0, 1 - odd, lambda i, c: group(qi * QB, QB, c, True), carry)
        for hp in range(HP):
            dq = carry[3 * hp + 2]
            dq_ref[:, _lanes(hp)] = (jnp.where(head0, dq[:QB], dq[QB:]) * HEAD_SCALE).astype(dq_ref.dtype)

        @pl.when(qi == nq - 1)
        def _():
            dk_ref[...] = dk_acc[...].astype(dk_ref.dtype)
            dv_ref[...] = dv_acc[...].astype(dv_ref.dtype)
        ride.after(refs, step)

    wb = HP * QB
    blk = pl.BlockSpec((QB, wb), lambda b, p, i: (b * nq + i, p))
    seq = pl.BlockSpec((lp, wb), lambda b, p, i: (b, p))
    out = jax.ShapeDtypeStruct((t, nblk * wb), BF16)
    return ride.call(
        body, name, (bl, nblk, nq),
        [blk, pl.BlockSpec((lp, wb), lambda b, p, i: (b, nblk + p)),
         pl.BlockSpec((lp, wb), lambda b, p, i: (b, 2 * nblk + p)), blk,
         pl.BlockSpec((QB, wb), lambda b, p, i: (b * nq + i, nblk + p)), pl.BlockSpec(memory_space=pltpu.SMEM)],
        [qkv, qkv, qkv, lt, dy, g0],
        [blk, seq, seq], [out, out, out], [pltpu.VMEM((lp, wb), F32), pltpu.VMEM((lp, wb), F32)])


def _place():
    return lax.axis_index("x"), lax.axis_index("y"), lax.axis_index("c")


def _peers(chip):
    kx, ky = chip // 2, chip % 2
    return ((1 - kx, ky), (kx, 1 - ky), (1 - kx, 1 - ky))


def _hbm_specs(n):
    return [pl.BlockSpec(memory_space=pl.ANY) for _ in range(n)]


def _remote(src, dst, send, recv, k, to):
    return pltpu.make_async_remote_copy(src, dst, send.at[k], recv.at[k], device_id=to, device_id_type=MESH)


class _Rider:
    def __init__(self, ins, out_shapes, aliases, nsem, first, mid=None, last=None):
        self.ins, self.out_shapes, self.aliases, self.nsem = list(ins), list(out_shapes), dict(aliases), nsem
        self.first, self.mid, self.last = first, mid, last


def _by_chip(fn):
    def run(ins, outs, send, recv):
        x, y, c = _place()
        for me in range(4):
            pl.when(2 * x + y == me)(functools.partial(fn, ins, outs, send, recv, me, c, (x, y, 1 - c)))
    return run


def _run_rider(name, rider):
    ni, no = len(rider.ins), len(rider.out_shapes)

    def body(*refs):
        args = (refs[:ni], refs[ni:ni + no], refs[ni + no], refs[ni + no + 1])
        for hook in (rider.first, rider.mid, rider.last):
            if hook is not None:
                hook(*args)

    return pl.pallas_call(
        body, name=name, in_specs=_hbm_specs(ni), out_specs=_hbm_specs(no), out_shape=rider.out_shapes,
        input_output_aliases=rider.aliases,
        scratch_shapes=[pltpu.SemaphoreType.DMA((rider.nsem,)), pltpu.SemaphoreType.DMA((rider.nsem,))],
        compiler_params=pltpu.CompilerParams(has_side_effects=True),
    )(*rider.ins)


class _Ride:
    def __init__(self, rider, n_in, n_out, steps):
        self.rider, self.n_in, self.n_out, self.steps = rider, n_in, n_out, steps
        self.ri = len(rider.ins) if rider else 0
        self.ro = len(rider.out_shapes) if rider else 0

    def own(self, refs):
        refs = list(refs)
        a, b = self.n_in, self.n_in + self.ri + self.n_out
        tail = refs[b + self.ro:len(refs) - 2] if self.rider else refs[b + self.ro:]
        return refs[:a] + refs[a + self.ri:b] + tail

    def _args(self, refs):
        a, b = self.n_in, self.n_in + self.ri + self.n_out
        return refs[a:a + self.ri], refs[b:b + self.ro], refs[-2], refs[-1]

    def before(self, refs, step):
        if self.rider is None:
            return
        pl.when(step == 0)(functools.partial(self.rider.first, *self._args(refs)))
        if self.rider.mid is not None:
            pl.when(step == (3 * self.steps) // 4)(functools.partial(self.rider.mid, *self._args(refs)))

    def after(self, refs, step):
        if self.rider is not None and self.rider.last is not None:
            pl.when(step == self.steps - 1)(functools.partial(self.rider.last, *self._args(refs)))

    def call(self, body, name, grid, in_specs, args, out_specs, out_shape, scratch,
             sem=("parallel", "parallel", "arbitrary")):
        r = self.rider
        if r is None:
            return pl.pallas_call(body, name=name, grid=grid, in_specs=in_specs, out_specs=out_specs,
                                  out_shape=out_shape, scratch_shapes=scratch, compiler_params=_params(sem))(*args)
        return pl.pallas_call(
            body, name=name, grid=grid, in_specs=in_specs + _hbm_specs(self.ri),
            out_specs=out_specs + _hbm_specs(self.ro), out_shape=out_shape + r.out_shapes,
            input_output_aliases={self.n_in + i: self.n_out + o for i, o in r.aliases.items()},
            scratch_shapes=scratch + [pltpu.SemaphoreType.DMA((r.nsem,)), pltpu.SemaphoreType.DMA((r.nsem,))],
            compiler_params=pltpu.CompilerParams(dimension_semantics=("arbitrary",) * len(grid),
                                                 vmem_limit_bytes=VMEM_LIMIT, has_side_effects=True),
        )(*args, *r.ins)


def _core_view(a, axis):
    l, r, c = a.shape
    return a.reshape(l, 4, 2, r // 8, c) if axis == 0 else a.reshape(l, 2, r // 2, c)


def _shard_view(a):
    l, r, c = a.shape
    return a.reshape(l, 2, r // 2, c)


def _piece(ref, axis, layer, chip, core):
    if axis == 0:
        return ref.at[layer, chip, core]
    cs = ref.shape[-1] // 4
    return ref.at[layer, core, :, pl.ds(chip * cs, cs)]


def _place_shard(name, w, axis, kidx, tr):
    _, r, cdim = w.shape
    shp = [2, r, cdim]
    shp[1 + axis] *= 4
    nb = r // tr

    def body(k_ref, w_ref, o_ref):
        o_ref[...] = w_ref[...].astype(o_ref.dtype)

    if axis == 0:
        out_spec = pl.BlockSpec((None, tr, cdim), lambda l, i, k_ref: (l, k_ref[0] * nb + i, 0))
    else:
        out_spec = pl.BlockSpec((None, tr, cdim), lambda l, i, k_ref: (l, i, k_ref[0]))
    return pl.pallas_call(
        body, name=name,
        grid_spec=pltpu.PrefetchScalarGridSpec(
            num_scalar_prefetch=1, grid=(2, nb),
            in_specs=[pl.BlockSpec((None, tr, cdim), lambda l, i, k_ref: (l, i, 0))], out_specs=out_spec),
        out_shape=jax.ShapeDtypeStruct(tuple(shp), BF16),
        compiler_params=_params(("arbitrary", "arbitrary")),
    )(kidx, w)


def _gather_rider(views, axes, items):
    n = len(items)

    def first(ins, outs, send, recv, me, c, sib):
        for i, (w, l) in enumerate(items):
            for j, (px, py) in enumerate(_peers(me)):
                _remote(_piece(ins[w], axes[w], l, me, c), _piece(outs[w], axes[w], l, me, c), send, recv,
                        3 * i + j, (px, py, c)).start()

    def mid(ins, outs, send, recv, me, c, sib):
        for i, (w, l) in enumerate(items):
            for j, (px, py) in enumerate(_peers(me)):
                got = _piece(outs[w], axes[w], l, 2 * px + py, c)
                _remote(got, got, send, recv, 3 * i + j, (px, py, c)).wait_recv()
                _remote(got, got, send, recv, 3 * (n + i) + j, sib).start()

    def last(ins, outs, send, recv, me, c, sib):
        for i, (w, l) in enumerate(items):
            for j, (px, py) in enumerate(_peers(me)):
                mine, got = _piece(outs[w], axes[w], l, me, c), _piece(outs[w], axes[w], l, 2 * px + py, c)
                theirs = _piece(outs[w], axes[w], l, 2 * px + py, 1 - c)
                _remote(theirs, theirs, send, recv, 3 * (n + i) + j, sib).wait_recv()
                _remote(mine, mine, send, recv, 3 * i + j, (px, py, c)).wait_send()
                _remote(got, got, send, recv, 3 * (n + i) + j, sib).wait_send()

    return _Rider(views, [jax.ShapeDtypeStruct(v.shape, v.dtype) for v in views], {w: w for w in range(len(views))},
                  6 * n, _by_chip(first), _by_chip(mid), _by_chip(last))


def _swap_rider(views, axes, items):
    nv = len(views)

    def part(ref, w, l, core):
        return ref.at[l, :, core] if axes[w] == 0 else ref.at[l, core]

    def copies(ins, outs, send, recv):
        x, y, c = _place()
        return [_remote(part(ins[w], w, l, 1 - c), outs[nv + i], send, recv, i, (x, y, 1 - c))
                for i, (w, l) in enumerate(items)]

    def first(ins, outs, send, recv):
        for cp in copies(ins, outs, send, recv):
            cp.start()

    def last(ins, outs, send, recv):
        for cp in copies(ins, outs, send, recv):
            cp.wait()

    got = [jax.ShapeDtypeStruct(views[w].shape[1:2] + views[w].shape[3:] if axes[w] == 0 else views[w].shape[2:],
                                views[w].dtype) for w, _ in items]
    return _Rider(views, [jax.ShapeDtypeStruct(v.shape, v.dtype) for v in views] + got,
                  {w: w for w in range(nv)}, len(items), first, None, last)


def _add_core(name, view, got, axis, layer, cidx, tr):
    def body(c_ref, g_ref, r_ref, o_ref):
        o_ref[...] = (g_ref[...] + r_ref[...]).astype(o_ref.dtype)

    if axis == 0:
        _, nchip, _, pr, cdim = view.shape
        grid = (nchip, pr // tr)
        specs = [pl.BlockSpec((None, None, None, tr, cdim), lambda k, i, c_ref: (layer, k, c_ref[0], i, 0)),
                 pl.BlockSpec((None, tr, cdim), lambda k, i, c_ref: (k, i, 0))]
        out_spec = pl.BlockSpec((None, tr, cdim), lambda k, i, c_ref: (k, i, 0))
    else:
        _, _, pr, cdim = view.shape
        grid = (pr // tr,)
        specs = [pl.BlockSpec((None, None, tr, cdim), lambda i, c_ref: (layer, c_ref[0], i, 0)),
                 pl.BlockSpec((tr, cdim), lambda i, c_ref: (i, 0))]
        out_spec = pl.BlockSpec((tr, cdim), lambda i, c_ref: (i, 0))
    return pl.pallas_call(
        body, name=name,
        grid_spec=pltpu.PrefetchScalarGridSpec(num_scalar_prefetch=1, grid=grid, in_specs=specs,
                                               out_specs=out_spec),
        out_shape=jax.ShapeDtypeStruct(got.shape, BF16),
        compiler_params=_params(("arbitrary",) * len(grid)),
    )(cidx, view, got)


def _scatter_rider(sums, axes):
    def part(ref, i, chip):
        if axes[i] == 0:
            return ref.at[chip]
        cs = ref.shape[-1] // 4
        return ref.at[:, pl.ds(chip * cs, cs)]

    def copies(ins, outs, send, recv, me, c, sib):
        return [_remote(part(ins[i], i, 2 * px + py), outs[i].at[j], send, recv, 3 * i + j, (px, py, c))
                for i in range(len(sums)) for j, (px, py) in enumerate(_peers(me))]

    def first(*args):
        for cp in copies(*args):
            cp.start()

    def last(*args):
        for cp in copies(*args):
            cp.wait()

    shapes = [jax.ShapeDtypeStruct((3,) + (s.shape[1:] if ax == 0 else (s.shape[0], s.shape[1] // 4)), s.dtype)
              for s, ax in zip(sums, axes)]
    return _Rider(sums, shapes, {}, 3 * len(sums), _by_chip(first), None, _by_chip(last))


def _add_chips(name, own, got, axis, layer, kc_idx, tr, into, shard_shape):
    _, pr, pc = got.shape

    def body(k_ref, o_ref, g_ref, *rest):
        rest[-1][...] = (o_ref[...].astype(F32) + g_ref[0].astype(F32) + g_ref[1].astype(F32)
                         + g_ref[2].astype(F32))

    if axis == 0:
        own_spec = pl.BlockSpec((None, tr, pc), lambda i, k_ref: (k_ref[0], i, 0))
    else:
        own_spec = pl.BlockSpec((tr, pc), lambda i, k_ref: (i, k_ref[0]))
    specs = [own_spec, pl.BlockSpec((3, tr, pc), lambda i, k_ref: (0, i, 0))]
    args = [kc_idx, own, got]
    if into is not None:
        specs.append(pl.BlockSpec(memory_space=pl.ANY))
        args.append(into)
    return pl.pallas_call(
        body, name=name,
        grid_spec=pltpu.PrefetchScalarGridSpec(
            num_scalar_prefetch=1, grid=(pr // tr,), in_specs=specs,
            out_specs=pl.BlockSpec((None, None, tr, pc), lambda i, k_ref: (layer, k_ref[1], i, 0))),
        out_shape=jax.ShapeDtypeStruct(shard_shape, F32),
        input_output_aliases={} if into is None else {3: 0},
        compiler_params=_params(("arbitrary",)),
    )(*args)


def _join_rider(parts):
    def first(ins, outs, send, recv):
        x, y, c = _place()
        for w in range(len(parts)):
            _remote(ins[w].at[:, c], outs[w].at[:, c], send, recv, w, (x, y, 1 - c)).start()

    def last(ins, outs, send, recv):
        x, y, c = _place()
        for w in range(len(parts)):
            _remote(ins[w].at[:, c], outs[w].at[:, c], send, recv, w, (x, y, 1 - c)).wait_send()
            _remote(ins[w].at[:, c], outs[w].at[:, 1 - c], send, recv, w, (x, y, 1 - c)).wait_recv()

    return _Rider(parts, [jax.ShapeDtypeStruct(p.shape, p.dtype) for p in parts],
                  {w: w for w in range(len(parts))}, len(parts), first, None, last)


def _all_reduce_small(name, pack, lead, groups):
    nr, d = pack.shape
    nout = nr - (groups - 1) * lead

    def body(in_ref, sum_ref, mine, slots, send, recv):
        x, y, c = _place()
        me = 4 * x + 2 * y + c
        fold = in_ref[0:lead]
        for grp in range(1, groups):
            fold = fold + in_ref[grp * lead:(grp + 1) * lead]
        mine[0:lead] = fold
        mine[lead:] = in_ref[groups * lead:]
        slots[me] = mine[...]
        cps = []
        for r in range(1, 8):
            rx, ry, rc = r // 4, (r // 2) % 2, r % 2
            peer = (x + rx - 2 * x * rx, y + ry - 2 * y * ry, c + rc - 2 * c * rc)
            cp = pltpu.make_async_remote_copy(mine, slots.at[me], send.at[r - 1], recv.at[r - 1],
                                              device_id=peer, device_id_type=MESH)
            cp.start()
            cps.append(cp)
        for cp in cps:
            cp.wait()
        acc = slots[0]
        for dev in range(1, 8):
            acc = acc + slots[dev]
        sum_ref[...] = acc

    vmem = pl.BlockSpec(memory_space=pltpu.VMEM)
    return pl.pallas_call(
        body, name=name, in_specs=[vmem], out_specs=vmem, out_shape=jax.ShapeDtypeStruct((nout, d), F32),
        scratch_shapes=[pltpu.VMEM((nout, d), F32), pltpu.VMEM((8, nout, d), F32), pltpu.SemaphoreType.DMA((7,)),
                        pltpu.SemaphoreType.DMA((7,))],
        compiler_params=pltpu.CompilerParams(has_side_effects=True, vmem_limit_bytes=VMEM_LIMIT),
    )(pack)


def _adamw_math(w, g, m, v):
    m = B1 * m + (1.0 - B1) * g
    v = B2 * v + (1.0 - B2) * (g * g)
    m_hat = m / (1.0 - B1 ** STEP)
    v_hat = v / (1.0 - B2 ** STEP)
    return -LR * (m_hat / (jnp.sqrt(v_hat) + ADAM_EPS) + WD * w), m, v


def _adamw(name, w, g, m, v, tr):
    shape = w.shape
    flat = [a.reshape(-1, shape[-1]) for a in (w, g, m, v)]
    r, cdim = flat[0].shape

    def body(w_ref, g_ref, m_ref, v_ref, d_ref, nm_ref, nv_ref):
        d_ref[...], nm_ref[...], nv_ref[...] = _adamw_math(w_ref[...], g_ref[...], m_ref[...], v_ref[...])

    spec = pl.BlockSpec((tr, cdim), lambda i: (i, 0))
    outs = pl.pallas_call(
        body, name=name, grid=(r // tr,), in_specs=[spec] * 4, out_specs=[spec] * 3,
        out_shape=[jax.ShapeDtypeStruct((r, cdim), F32)] * 3,
        compiler_params=_params(("parallel",)),
    )(*flat)
    return [o.reshape(shape) for o in outs]


def _adamw_small(name, groups):
    n = len(groups)
    shapes = [grp[0].shape for grp in groups]
    flat = [a.reshape(-1, a.shape[-1]) for grp in groups for a in grp]

    def body(*refs):
        ins, outs = refs[:4 * n], refs[4 * n:]
        for i in range(n):
            w_ref, g_ref, m_ref, v_ref = ins[4 * i:4 * i + 4]
            outs[3 * i][...], outs[3 * i + 1][...], outs[3 * i + 2][...] = _adamw_math(
                w_ref[...], g_ref[...], m_ref[...], v_ref[...])

    vmem = pl.BlockSpec(memory_space=pltpu.VMEM)
    out_shape = [jax.ShapeDtypeStruct(flat[4 * i].shape, F32) for i in range(n) for _ in range(3)]
    outs = pl.pallas_call(body, name=name, in_specs=[vmem] * (4 * n), out_specs=[vmem] * (3 * n),
                          out_shape=out_shape)(*flat)
    return [[outs[3 * i + j].reshape(shapes[i]) for j in range(3)] for i in range(n)]


def _block_diag(w_grp):
    g, pg, _ = w_grp.shape
    eye = jnp.eye(g, dtype=w_grp.dtype)
    return (eye[:, None, :, None] * w_grp[:, :, None, :]).reshape(g * pg, g * pg)


def _diag_blocks(m, g):
    pg = m.shape[0] // g
    return jnp.stack([m[i * pg:(i + 1) * pg, i * pg:(i + 1) * pg] for i in range(g)])


BIG = ("w_in", "w_out", "w_up", "w_down")
AXES = (1, 0, 1, 0)
W_IN, W_OUT, W_UP, W_DOWN = range(4)


def kernel(x, meta_tokens, g_mix, w_in, w_conv, w_pool, pool_scale, w_out, g_mlp, w_up, w_down, g_final, loss_target, m_meta_tokens, m_g_mix, m_w_in, m_w_conv, m_w_pool, m_pool_scale, m_w_out, m_g_mlp, m_w_up, m_w_down, m_g_final, v_meta_tokens, v_g_mix, v_w_in, v_w_conv, v_w_pool, v_pool_scale, v_w_out, v_g_mlp, v_w_up, v_w_down, v_g_final):
    bl, s, d = x.shape
    depth = g_mix.shape[0]
    assert depth == 2
    lp = PAD + N_META + s
    tt = lp
    tm = lp // 4
    copy_rows, sum_rows, dw_tile = 256, 128, 1024
    cs = w_conv.shape[2]
    cw = 4 * cs
    ngrp = w_pool.shape[1]
    xi, yi, ci = _place()
    chip = (2 * xi + yi).astype(jnp.int32)
    cidx, kidx = ci.astype(jnp.int32).reshape(1), chip.reshape(1)
    kc_idx = jnp.stack([chip, ci.astype(jnp.int32)])
    shards = (w_in, w_out, w_up, w_down)

    views = [_core_view(_place_shard(f"place_{BIG[w]}", shards[w], AXES[w], kidx, copy_rows), AXES[w])
             for w in range(4)]

    def whole(w):
        return views[w].reshape(depth, -1, views[w].shape[-1])

    def gather_on(call, items):
        ws = sorted({w for w, _ in items})
        res = call(_gather_rider([views[w] for w in ws], [AXES[w] for w in ws],
                                 [(ws.index(w), layer) for w, layer in items]))
        for j, w in enumerate(ws):
            views[w] = res[len(res) - len(ws) + j]
        return res[:len(res) - len(ws)]

    placed = jnp.zeros((32, d), F32)
    placed = lax.dynamic_update_slice(placed, meta_tokens, (0, chip * meta_tokens.shape[1]))
    placed = lax.dynamic_update_slice(placed, w_conv.reshape(-1, cs), (N_META, chip * cs))
    placed = jnp.where(ci == 0, placed, 0.0)
    small = _all_reduce_small("gather_small", placed, 8, 1)
    meta_full = small[:N_META]
    conv_full = small[N_META:N_META + depth * 3, :cw].reshape(depth, 3, cw)

    (h,) = gather_on(lambda rider: _build_h("build_h", x, meta_full, lp, rider), [(W_IN, 0)])
    wbd = [_block_diag(w_pool[i]).astype(BF16) for i in range(depth)]
    saved = []
    for i in range(depth):
        hn, u_cp, qkv = _in_proj(f"in_proj{i}", h, g_mix[i], whole(W_IN), i, tm, 4 * cw)
        y_cp = _convpool_fwd(f"convpool{i}", u_cp, conv_full[i], wbd[i], pool_scale[i:i + 1], lp, tm)
        if i == 0:
            y_at, lt, g0 = gather_on(lambda rider: _attn_fwd(f"attn{i}", qkv, bl, lp, rider),
                                     [(W_OUT, 0), (W_UP, 0), (W_DOWN, 0), (W_IN, 1)])
            h_mid = _out_proj(f"out_proj{i}", y_cp, y_at, h, whole(W_OUT), i, tm)
            w_up0 = whole(W_UP)
            hn2, m_pre, act = gather_on(lambda rider: _up_proj(f"up_proj{i}", h_mid, g_mlp[i], w_up0, i, tm, rider),
                                        [(W_OUT, 1), (W_DOWN, 1)])
            w_down0 = whole(W_DOWN)
            (h_next,) = gather_on(lambda rider: _down_proj(f"down_proj{i}", act, h_mid, w_down0, i, tm, rider),
                                  [(W_UP, 1)])
        else:
            y_at, lt, g0 = _attn_fwd(f"attn{i}", qkv, bl, lp)
            h_mid = _out_proj(f"out_proj{i}", y_cp, y_at, h, whole(W_OUT), i, tm)
            hn2, m_pre, act = _up_proj(f"up_proj{i}", h_mid, g_mlp[i], whole(W_UP), i, tm)
            (h_next,) = _down_proj(f"down_proj{i}", act, h_mid, whole(W_DOWN), i, tm)
        saved.append((h, hn, u_cp, qkv, y_cp, y_at, (lt, g0), h_mid, hn2, m_pre, act))
        h = h_next

    dh, loss8, dgf8 = _loss_bwd("loss", h, g_final, loss_target, lp)
    per_layer = {k: [None] * depth for k in ("g_mix", "w_conv", "w_pool", "pool_scale", "g_mlp")}

    gw = [None] * 4
    sums, arrived = {}, {}

    def dw(name, a, b, w, layer):
        shape = whole(w).shape
        into = None if gw[w] is None else gw[w].reshape(shape)
        if isinstance(a, list):
            res = _mm_tn_slab(name, a, b, tt // 2, into, shape, layer)
        else:
            res = _mm_tn(name, a, b, tt, dw_tile, dw_tile, into, shape, layer, 0, 0)
        gw[w] = _core_view(res, AXES[w])

    def swap_rider(ws):
        return _swap_rider([gw[w] for w, _ in ws], [AXES[w] for w, _ in ws],
                           [(j, layer) for j, (_, layer) in enumerate(ws)])

    def swapped(ws, outs):
        for j, (w, layer) in enumerate(ws):
            gw[w] = outs[j]
            sums[w, layer] = _add_core(f"chip_sum_{BIG[w]}{layer}", gw[w], outs[len(ws) + j], AXES[w], layer, cidx,
                                       sum_rows)

    def scatter_rider(items):
        return _scatter_rider([sums[it] for it in items], [AXES[w] for w, _ in items])

    def bwd_mlp(i, dh, swap_early):
        _, _, _, _, y_cp, y_at, _, h_mid, hn2, m_pre, act = saved[i]
        dm = _down_proj_dx(f"down_proj_dx{i}", dh, m_pre, whole(W_DOWN), i, tm)
        dw(f"down_proj_dw{i}", act, dh, W_DOWN, i)
        dw(f"up_proj_dw{i}", hn2, dm, W_UP, i)
        ws = [(W_DOWN, i), (W_UP, i)] if swap_early else []
        dh_mid, dy, dg8, *outs = _up_proj_dx(f"up_proj_dx{i}", dm, h_mid, dh, g_mlp[i], whole(W_UP), whole(W_OUT), i,
                                             tm, swap_rider(ws) if ws else None)
        swapped(ws, outs)
        per_layer["g_mlp"][i] = dg8.sum(0)
        dw(f"out_proj_dw{i}", [y_cp, y_at], [dh_mid], W_OUT, i)
        return dh_mid, dy

    def bwd_mix(i, dh_mid, dy, dus3, scatter_late):
        h_in, hn, u_cp = saved[i][:3]
        du_cp, sm, dwbd = _convpool_bwd(f"convpool_bwd{i}", u_cp, dy, conv_full[i], wbd[i], pool_scale[i:i + 1], lp,
                                        tm)
        sm = sm.reshape(4, 8, cw).sum(1)
        per_layer["w_conv"][i] = sm[0:3]
        per_layer["pool_scale"][i] = sm[3]
        per_layer["w_pool"][i] = _diag_blocks(dwbd, ngrp)
        dus = [du_cp, *dus3]
        dw(f"in_proj_dw{i}", [hn], dus, W_IN, i)
        rider = None
        if scatter_late:
            swapped([(W_IN, i)], _run_rider(f"grads_swap_in{i}", swap_rider([(W_IN, i)])))
            rider = scatter_rider([(W_IN, i)])
        dh, dg8, *outs = _in_proj_dx(f"in_proj_dx{i}", dus, h_in, dh_mid, g_mix[i], whole(W_IN), i, tm, rider)
        arrived.update(zip([(W_IN, i)], outs))
        per_layer["g_mix"][i] = dg8.sum(0)
        return dh

    def attn_bwd(i, dy, rider):
        qkv, (lt, g0) = saved[i][3], saved[i][6]
        res = _attn_bwd(f"attn_bwd{i}", qkv, lt, g0, dy, bl, lp, rider)
        return res[:3], res[3:]

    dh_mid, dy = bwd_mlp(1, dh, False)
    ws = [(W_DOWN, 1), (W_UP, 1), (W_OUT, 1)]
    dus3, outs = attn_bwd(1, dy, swap_rider(ws))
    swapped(ws, outs)
    dh = bwd_mix(1, dh_mid, dy, dus3, False)

    dh_mid, dy = bwd_mlp(0, dh, True)
    ws = [(W_IN, 1), (W_OUT, 0)]
    swapped(ws, _run_rider("grads_swap0", swap_rider(ws)))
    items = list(sums)
    dus3, outs = attn_bwd(0, dy, scatter_rider(items))
    arrived.update(zip(items, outs))
    dh0 = bwd_mix(0, dh_mid, dy, dus3, True)

    finals = []
    for w in range(4):
        rs_, cs_ = shards[w].shape[1:]
        part = None
        for layer in reversed(range(depth)):
            part = _add_chips(f"reduce_{BIG[w]}{layer}", sums[w, layer], arrived[w, layer], AXES[w], layer, kc_idx,
                              sum_rows, part, (depth, 2, rs_ // 2, cs_))
        finals.append(part)
    finals = _run_rider("grads_join", _join_rider(finals))
    grad = {BIG[w]: finals[w].reshape(shards[w].shape) for w in range(4)}

    dh0 = dh0.reshape(bl, lp, d)
    grad_x = dh0[:, PAD + N_META:]
    local = {k: jnp.stack(v) for k, v in per_layer.items()}
    pieces = [dh0[:, PAD:PAD + N_META].reshape(bl * N_META, d), local["g_mix"], local["g_mlp"],
              dgf8.sum(0).reshape(1, d),
              jnp.pad(local["w_conv"].reshape(-1), (0, 2 * d - local["w_conv"].size)).reshape(2, d),
              jnp.pad(local["pool_scale"].reshape(-1), (0, d - local["pool_scale"].size)).reshape(1, d),
              jnp.pad(loss8.sum(0, keepdims=True), ((0, 7), (0, 0))), local["w_pool"].reshape(-1, d)]
    summed = _all_reduce_small("small_grads", jnp.concatenate(pieces, axis=0), N_META, bl)
    o = N_META
    grad.update({
        "meta_tokens": lax.dynamic_slice_in_dim(summed[:o], chip * meta_tokens.shape[1], meta_tokens.shape[1], 1),
        "g_mix": summed[o:o + 2], "g_mlp": summed[o + 2:o + 4], "g_final": summed[o + 4],
        "w_conv": lax.dynamic_slice_in_dim(summed[o + 5:o + 7].reshape(-1)[:2 * 3 * cw].reshape(2, 3, cw),
                                           chip * cs, cs, 2),
        "pool_scale": summed[o + 7].reshape(-1)[:pool_scale.size].reshape(pool_scale.shape),
        "w_pool": summed[o + 16:].reshape(w_pool.shape),
    })
    loss = jnp.sum(summed[o + 8])

    weights = dict(meta_tokens=meta_tokens, g_mix=g_mix, w_in=w_in, w_conv=w_conv, w_pool=w_pool,
                   pool_scale=pool_scale, w_out=w_out, g_mlp=g_mlp, w_up=w_up, w_down=w_down, g_final=g_final)
    ms = dict(meta_tokens=m_meta_tokens, g_mix=m_g_mix, w_in=m_w_in, w_conv=m_w_conv, w_pool=m_w_pool,
              pool_scale=m_pool_scale, w_out=m_w_out, g_mlp=m_g_mlp, w_up=m_w_up, w_down=m_w_down,
              g_final=m_g_final)
    vs = dict(meta_tokens=v_meta_tokens, g_mix=v_g_mix, w_in=v_w_in, w_conv=v_w_conv, w_pool=v_w_pool,
              pool_scale=v_pool_scale, w_out=v_w_out, g_mlp=v_g_mlp, w_up=v_w_up, w_down=v_w_down,
              g_final=v_g_final)
    order = list(weights)
    upd = {k: _adamw(f"adamw_{k}", weights[k], grad[k], ms[k], vs[k], copy_rows) for k in BIG}
    little = [k for k in order if k not in BIG]
    for k, res in zip(little, _adamw_small("adamw_small", [(weights[k], grad[k].reshape(weights[k].shape), ms[k],
                                                            vs[k]) for k in little])):
        upd[k] = res
    grad = {k: grad[k].reshape(weights[k].shape) for k in order}
    return (loss, grad_x, *[grad[k] for k in order], *[upd[k][0] for k in order], *[upd[k][1] for k in order],
            *[upd[k][2] for k in order])
```

```python
import functools

import jax
import jax.numpy as jnp
from jax import lax
from jax.experimental import pallas as pl
from jax.experimental.pallas import tpu as pltpu

F32, BF16 = jnp.float32, jnp.bfloat16
MESH = pl.DeviceIdType.MESH
EPS = 1e-6
N_META = 16
QB = 128
PAD = QB - N_META
HALO = 16
POOL_WINDOWS = (2.0, 4.0, 8.0, 16.0)
HEAD_SCALE = 0.125
LR, B1, B2, ADAM_EPS, WD, STEP = 0.001, 0.9, 0.999, 1e-08, 0.01, 10
VMEM_LIMIT = 56 * 1024 * 1024


def _params(sem=None):
    return pltpu.CompilerParams(dimension_semantics=sem, vmem_limit_bytes=VMEM_LIMIT)


def _nt(a, b):
    return lax.dot_general(a, b, (((1,), (1,)), ((), ())), preferred_element_type=F32)


def _tn(a, b):
    return lax.dot_general(a, b, (((0,), (0,)), ((), ())), preferred_element_type=F32)


def _nn(a, b):
    return jnp.dot(a, b, preferred_element_type=F32)


def _fold8(v):
    r, c = v.shape
    return jnp.sum(v.reshape(r // 8, 8, c), axis=0)


NCH = 512


def _rows_call(name, body, tm, row_ins, consts, row_outs, accs=(), rider=None):
    t = row_ins[0].shape[0]
    ride = _Ride(rider, len(row_ins) + len(consts), len(row_outs) + len(accs), t // tm)

    def stepped(*refs):
        step = pl.program_id(0)
        ride.before(refs, step)
        body(*ride.own(refs))
        ride.after(refs, step)

    in_specs = [pl.BlockSpec((tm, a.shape[1]), lambda i: (i, 0)) for a in row_ins]
    for a, layer in consts:
        if layer is None:
            in_specs.append(pl.BlockSpec(a.shape, lambda i: (0, 0)))
        else:
            in_specs.append(pl.BlockSpec((None, *a.shape[1:]), lambda i, l=layer: (l, 0, 0)))
    return ride.call(
        stepped, name, (t // tm,), in_specs, [*row_ins, *[a for a, _ in consts]],
        [pl.BlockSpec((tm, c), lambda i: (i, 0)) for c, _ in row_outs]
        + [pl.BlockSpec(s, lambda i: (0, 0)) for s in accs],
        [jax.ShapeDtypeStruct((t, c), dt) for c, dt in row_outs] + [jax.ShapeDtypeStruct(s, F32) for s in accs],
        [], ("arbitrary",) if accs else ("parallel",))


def _norm_parts(x):
    r = lax.rsqrt(jnp.mean(x * x, axis=-1, keepdims=True) + EPS)
    return r, x * r


def _norm_bwd(r, xh, dyn, g):
    w = dyn * g
    return r * (w - xh * jnp.mean(w * xh, axis=-1, keepdims=True))


def _in_proj(name, h, g, w, layer, tm, ncp):
    d, n = h.shape[1], w.shape[2]

    def body(h_ref, g_ref, w_ref, hn_ref, ucp_ref, qkv_ref):
        _, xh = _norm_parts(h_ref[...])
        hn = (xh * g_ref[...]).astype(BF16)
        hn_ref[...] = hn
        for n0 in range(0, n, NCH):
            acc = _nn(hn, w_ref[:, n0:n0 + NCH])
            if n0 < ncp:
                ucp_ref[:, n0:n0 + NCH] = acc
            else:
                qkv_ref[:, n0 - ncp:n0 - ncp + NCH] = acc.astype(BF16)

    return _rows_call(name, body, tm, [h], [(g.reshape(1, d), None), (w, layer)],
                      [(d, BF16), (ncp, F32), (n - ncp, BF16)])


def _out_proj(name, y_cp, y_at, h, w, layer, tm):
    d, k1 = h.shape[1], y_cp.shape[1]

    def body(ycp_ref, yat_ref, h_ref, w_ref, o_ref):
        for n0 in range(0, d, NCH):
            o_ref[:, n0:n0 + NCH] = (h_ref[:, n0:n0 + NCH] + _nn(ycp_ref[...], w_ref[0:k1, n0:n0 + NCH])
                                     + _nn(yat_ref[...], w_ref[k1:, n0:n0 + NCH]))

    return _rows_call(name, body, tm, [y_cp, y_at, h], [(w, layer)], [(d, F32)])[0]


def _up_proj(name, h_mid, g, w, layer, tm, rider=None):
    d, n = h_mid.shape[1], w.shape[2]

    def body(h_ref, g_ref, w_ref, hn_ref, m_ref, act_ref):
        _, xh = _norm_parts(h_ref[...])
        hn = (xh * g_ref[...]).astype(BF16)
        hn_ref[...] = hn
        for n0 in range(0, n, NCH):
            acc = _nn(hn, w_ref[:, n0:n0 + NCH])
            m_ref[:, n0:n0 + NCH] = acc.astype(BF16)
            act_ref[:, n0:n0 + NCH] = jnp.square(jnp.maximum(acc, 0.0)).astype(BF16)

    return _rows_call(name, body, tm, [h_mid], [(g.reshape(1, d), None), (w, layer)],
                      [(d, BF16), (n, BF16), (n, BF16)], rider=rider)


def _down_proj(name, act, h_mid, w, layer, tm, rider=None):
    d = h_mid.shape[1]

    def body(a_ref, h_ref, w_ref, o_ref):
        for n0 in range(0, d, NCH):
            o_ref[:, n0:n0 + NCH] = h_ref[:, n0:n0 + NCH] + _nn(a_ref[...], w_ref[:, n0:n0 + NCH])

    return _rows_call(name, body, tm, [act, h_mid], [(w, layer)], [(d, F32)], rider=rider)


def _down_proj_dx(name, dh, m_pre, w, layer, tm):
    n = w.shape[1]

    def body(dh_ref, m_ref, w_ref, dm_ref):
        dhb = dh_ref[...].astype(BF16)
        for n0 in range(0, n, NCH):
            dm_ref[:, n0:n0 + NCH] = (_nt(dhb, w_ref[n0:n0 + NCH, :])
                                      * (2.0 * jnp.maximum(m_ref[:, n0:n0 + NCH].astype(F32), 0.0))).astype(BF16)

    return _rows_call(name, body, tm, [dh, m_pre], [(w, layer)], [(n, BF16)])[0]


def _up_proj_dx(name, dm, h_mid, dh, g, w_up, w_out, layer, tm, rider=None):
    d = h_mid.shape[1]

    def body(dm_ref, h_ref, dh_ref, g_ref, wup_ref, wout_ref, dhm_ref, dy_ref, dg_ref):
        @pl.when(pl.program_id(0) == 0)
        def _():
            dg_ref[...] = jnp.zeros_like(dg_ref)
        dyn = _nt(dm_ref[...], wup_ref[...])
        r, xh = _norm_parts(h_ref[...])
        dhm = dh_ref[...] + _norm_bwd(r, xh, dyn, g_ref[...])
        dhm_ref[...] = dhm
        dg_ref[...] += _fold8(dyn * xh)
        dy_ref[...] = _nt(dhm.astype(BF16), wout_ref[...])

    return _rows_call(name, body, tm, [dm, h_mid, dh], [(g.reshape(1, d), None), (w_up, layer), (w_out, layer)],
                      [(d, F32), (w_out.shape[1], F32)], [(8, d)], rider)


def _in_proj_dx(name, dus, h, dh_mid, g, w, layer, tm, rider=None):
    d = h.shape[1]
    ns = [du.shape[1] for du in dus]
    nd = len(dus)

    def body(*refs):
        du_refs = refs[:nd]
        h_ref, dhm_ref, g_ref, w_ref, dh_ref, dg_ref = refs[nd:]

        @pl.when(pl.program_id(0) == 0)
        def _():
            dg_ref[...] = jnp.zeros_like(dg_ref)
        dyn, off = None, 0
        for du_ref, n in zip(du_refs, ns):
            part = _nt(du_ref[...], w_ref[:, off:off + n])
            dyn = part if dyn is None else dyn + part
            off += n
        r, xh = _norm_parts(h_ref[...])
        dh_ref[...] = dhm_ref[...] + _norm_bwd(r, xh, dyn, g_ref[...])
        dg_ref[...] += _fold8(dyn * xh)

    return _rows_call(name, body, tm, [*dus, h, dh_mid], [(g.reshape(1, d), None), (w, layer)], [(d, F32)],
                      [(8, d)], rider)


def _mm_tn(name, a, b, tt, tka, tn, into, shape, layer, row_off, col_off):
    t, ka = a.shape
    n = b.shape[1]
    assert t % tt == 0 and ka % tka == 0 and n % tn == 0 and row_off % tka == 0 and col_off % tn == 0

    def body(a_ref, b_ref, *rest):
        o_ref = rest[-1]

        @pl.when(pl.program_id(2) == 0)
        def _():
            o_ref[...] = jnp.zeros_like(o_ref)
        o_ref[...] += _tn(a_ref[...].astype(BF16), b_ref[...].astype(BF16))

    in_specs = [pl.BlockSpec((tt, tka), lambda i, j, s: (s, i)), pl.BlockSpec((tt, tn), lambda i, j, s: (s, j))]
    args = [a, b]
    if into is not None:
        in_specs.append(pl.BlockSpec(memory_space=pl.ANY))
        args.append(into)
    return pl.pallas_call(
        body, name=name, grid=(ka // tka, n // tn, t // tt), in_specs=in_specs,
        out_specs=pl.BlockSpec((None, tka, tn), lambda i, j, s: (layer, row_off // tka + i, col_off // tn + j)),
        out_shape=jax.ShapeDtypeStruct(shape, F32),
        input_output_aliases={} if into is None else {2: 0},
        compiler_params=_params(("parallel", "parallel", "arbitrary")),
    )(*args)


def _mm_tn_slab(name, a_list, b_list, tt, into, shape, layer):
    t = a_list[0].shape[0]
    kas, ns = [a.shape[1] for a in a_list], [b.shape[1] for b in b_list]
    assert t % tt == 0 and (sum(kas), sum(ns)) == tuple(shape[1:])
    na, nb = len(a_list), len(b_list)

    def body(*refs):
        o_ref = refs[-1]

        @pl.when(pl.program_id(0) == 0)
        def _():
            o_ref[...] = jnp.zeros_like(o_ref)
        r0 = 0
        for a_ref, ka in zip(refs[:na], kas):
            a = a_ref[...].astype(BF16)
            c0 = 0
            for b_ref, n in zip(refs[na:na + nb], ns):
                o_ref[r0:r0 + ka, c0:c0 + n] += _tn(a, b_ref[...].astype(BF16))
                c0 += n
            r0 += ka

    in_specs = [pl.BlockSpec((tt, c), lambda s: (s, 0)) for c in kas + ns]
    args = [*a_list, *b_list]
    if into is not None:
        in_specs.append(pl.BlockSpec(memory_space=pl.ANY))
        args.append(into)
    return pl.pallas_call(
        body, name=name, grid=(t // tt,), in_specs=in_specs,
        out_specs=pl.BlockSpec((None, *shape[1:]), lambda s: (layer, 0, 0)),
        out_shape=jax.ShapeDtypeStruct(shape, F32),
        input_output_aliases={} if into is None else {na + nb: 0},
        compiler_params=_params(("arbitrary",)),
    )(*args)


def _build_h(name, x, meta, lp, rider=None):
    bl, s, d = x.shape
    nq = lp // QB
    ride = _Ride(rider, 2, 1, bl * nq)

    def body(*refs):
        x_ref, m_ref, o_ref = ride.own(refs)
        j = pl.program_id(1)
        step = pl.program_id(0) * nq + j
        ride.before(refs, step)
        head = jnp.concatenate([jnp.zeros((PAD, d), F32), m_ref[...]], axis=0)
        o_ref[...] = jnp.where(j == 0, head, x_ref[...])
        ride.after(refs, step)

    return ride.call(
        body, name, (bl, nq),
        [pl.BlockSpec((None, QB, d), lambda b, j: (b, jnp.maximum(j - 1, 0), 0)),
         pl.BlockSpec(meta.shape, lambda b, j: (0, 0))], [x, meta],
        [pl.BlockSpec((QB, d), lambda b, j: (b * nq + j, 0))], [jax.ShapeDtypeStruct((bl * lp, d), F32)], [],
        ("parallel", "arbitrary"))


def _loss_bwd(name, h, g, target, lp):
    t, d = h.shape
    bl = target.shape[0]
    nq = lp // QB

    def body(h_ref, g_ref, t_ref, dh_ref, ls_ref, dg_ref):
        b, j = pl.program_id(0), pl.program_id(1)

        @pl.when((b == 0) & (j == 0))
        def _():
            ls_ref[...] = jnp.zeros_like(ls_ref)
            dg_ref[...] = jnp.zeros_like(dg_ref)
        xv = h_ref[...]
        r = lax.rsqrt(jnp.mean(xv * xv, axis=-1, keepdims=True) + EPS)
        xh = xv * r
        gv = g_ref[...]
        err = jnp.where(j >= 1, xh * gv - t_ref[...], 0.0)
        ls_ref[...] += _fold8(err * err) * (0.5 / d)
        dy = err * (1.0 / d)
        w = dy * gv
        dh_ref[...] = r * (w - xh * jnp.mean(w * xh, axis=-1, keepdims=True))
        dg_ref[...] += _fold8(dy * xh)

    return pl.pallas_call(
        body, name=name, grid=(bl, nq),
        in_specs=[pl.BlockSpec((QB, d), lambda b, j: (b * nq + j, 0)), pl.BlockSpec((1, d), lambda b, j: (0, 0)),
                  pl.BlockSpec((None, QB, d), lambda b, j: (b, jnp.maximum(j - 1, 0), 0))],
        out_specs=[pl.BlockSpec((QB, d), lambda b, j: (b * nq + j, 0)), pl.BlockSpec((8, d), lambda b, j: (0, 0)),
                   pl.BlockSpec((8, d), lambda b, j: (0, 0))],
        out_shape=[jax.ShapeDtypeStruct((t, d), F32), jax.ShapeDtypeStruct((8, d), F32),
                   jax.ShapeDtypeStruct((8, d), F32)],
        compiler_params=_params(("arbitrary", "arbitrary")),
    )(h, g.reshape(1, d), target)


def _pool_select(grp, a2, a4, a8, a16):
    return jnp.where(grp == 0, a2, jnp.where(grp == 1, a4, jnp.where(grp == 2, a8, a16)))


def _trailing_sums(v):
    s2 = v + pltpu.roll(v, 1, 0)
    s4 = s2 + pltpu.roll(s2, 2, 0)
    s8 = s4 + pltpu.roll(s4, 4, 0)
    s16 = s8 + pltpu.roll(s8, 8, 0)
    return s2, s4, s8, s16


def _leading_sums(v):
    n = v.shape[0]
    s2 = v + pltpu.roll(v, n - 1, 0)
    s4 = s2 + pltpu.roll(s2, n - 2, 0)
    s8 = s4 + pltpu.roll(s4, n - 4, 0)
    s16 = s8 + pltpu.roll(s8, n - 8, 0)
    return s2, s4, s8, s16


def _convpool_fwd(name, u_cp, wconv, wbd, pscale, lp, r):
    t = u_cp.shape[0]
    cw = u_cp.shape[1] // 4
    tps, hb = lp // r, r // HALO

    def body(cb_ref, cc_ref, cx_ref, pi_ref, cch_ref, cxh_ref, pih_ref, wc_ref, wbd_ref, ps_ref, y_ref):
        i = pl.program_id(0)
        lrow = (i % tps) * r + lax.broadcasted_iota(jnp.int32, (r, 1), 0)
        valid = lrow >= PAD
        xx = jnp.concatenate([cch_ref[...] * cxh_ref[...], cc_ref[...] * cx_ref[...]], axis=0)
        conv = (wc_ref[0:1, :] * pltpu.roll(xx, 2, 0) + wc_ref[1:2, :] * pltpu.roll(xx, 1, 0)
                + wc_ref[2:3, :] * xx)
        y_ref[:, 0:cw] = (cb_ref[...] * conv[HALO:]).astype(y_ref.dtype)
        p = pi_ref[...]
        grp = lax.broadcasted_iota(jnp.int32, (1, cw), 1) // (cw // 4)
        sel = _pool_select(grp, *_trailing_sums(jnp.concatenate([pih_ref[...], p], axis=0)))[HALO:]
        cnt = jnp.maximum(jnp.minimum((lrow - (PAD - 1)).astype(F32), _pool_select(grp, *POOL_WINDOWS)), 1.0)
        pooled = jnp.where(valid, sel / cnt - p, 0.0)
        y_ref[:, cw:2 * cw] = (_nn(pooled.astype(BF16), wbd_ref[...]) * ps_ref[...]).astype(y_ref.dtype)

    def main(col):
        return pl.BlockSpec((r, cw), lambda i: (i, col))

    def prev(col):
        return pl.BlockSpec((HALO, cw), lambda i: (jnp.maximum(i * hb - 1, 0), col))

    def whole(a):
        return pl.BlockSpec(a.shape, lambda i: (0, 0))

    return pl.pallas_call(
        body, name=name, grid=(t // r,),
        in_specs=[main(0), main(1), main(2), main(3), prev(1), prev(2), prev(3), whole(wconv), whole(wbd),
                  whole(pscale)],
        out_specs=pl.BlockSpec((r, 2 * cw), lambda i: (i, 0)),
        out_shape=jax.ShapeDtypeStruct((t, 2 * cw), BF16),
        compiler_params=_params(("parallel",)),
    )(u_cp, u_cp, u_cp, u_cp, u_cp, u_cp, u_cp, wconv, wbd, pscale)


def _convpool_bwd(name, u_cp, dy, wconv, wbd, pscale, lp, r):
    t = u_cp.shape[0]
    cw = u_cp.shape[1] // 4
    tps, hb = lp // r, r // HALO
    e = r + HALO

    def body(cb_ref, cc_ref, cx_ref, pi_ref, cbn_ref, cch_ref, cxh_ref, pih_ref, dyc_ref, dyp_ref, dycn_ref,
             dypn_ref, wc_ref, wbd_ref, ps_ref, du_ref, sm_ref, dwbd_ref):
        i = pl.program_id(0)

        @pl.when(i == 0)
        def _():
            sm_ref[...] = jnp.zeros_like(sm_ref)
            dwbd_ref[...] = jnp.zeros_like(dwbd_ref)
        lrow_e = (i % tps) * r + lax.broadcasted_iota(jnp.int32, (e, 1), 0)
        valid_e = (lrow_e >= PAD) & (lrow_e < lp)
        lrow, valid = lrow_e[:r], lrow_e[:r] >= PAD
        w0, w1, w2 = wc_ref[0:1, :], wc_ref[1:2, :], wc_ref[2:3, :]
        cb, cc, cx = cb_ref[...], cc_ref[...], cx_ref[...]
        prod = cc * cx
        xx = jnp.concatenate([cch_ref[...] * cxh_ref[...], prod], axis=0)
        back1, back2 = pltpu.roll(xx, 1, 0)[HALO:], pltpu.roll(xx, 2, 0)[HALO:]
        dyc = dyc_ref[...]
        du_ref[:, 0:cw] = (dyc * (w0 * back2 + w1 * back1 + w2 * prod)).astype(du_ref.dtype)
        dconv_e = jnp.where(valid_e, jnp.concatenate([dyc * cb, dycn_ref[...] * cbn_ref[...]], axis=0), 0.0)
        dconv = dconv_e[:r]
        dprod = (w2 * dconv + w1 * pltpu.roll(dconv_e, e - 1, 0)[:r] + w0 * pltpu.roll(dconv_e, e - 2, 0)[:r])
        du_ref[:, cw:2 * cw] = (dprod * cx).astype(du_ref.dtype)
        du_ref[:, 2 * cw:3 * cw] = (dprod * cc).astype(du_ref.dtype)
        sm_ref[0:8, :] += _fold8(dconv * back2)
        sm_ref[8:16, :] += _fold8(dconv * back1)
        sm_ref[16:24, :] += _fold8(dconv * prod)
        p = pi_ref[...]
        grp = lax.broadcasted_iota(jnp.int32, (1, cw), 1) // (cw // 4)
        win = _pool_select(grp, *POOL_WINDOWS)
        sel = _pool_select(grp, *_trailing_sums(jnp.concatenate([pih_ref[...], p], axis=0)))[HALO:]
        cnt_e = jnp.maximum(jnp.minimum((lrow_e - (PAD - 1)).astype(F32), win), 1.0)
        pooled = jnp.where(valid, sel / cnt_e[:r] - p, 0.0).astype(BF16)
        dyp = dyp_ref[...]
        sm_ref[24:32, :] += _fold8(dyp * _nn(pooled, wbd_ref[...]))
        dpre_e = (jnp.concatenate([dyp, dypn_ref[...]], axis=0) * ps_ref[...]).astype(BF16)
        dwbd_ref[...] += _tn(pooled, dpre_e[:r])
        dpooled_e = jnp.where(valid_e, _nt(dpre_e, wbd_ref[...]), 0.0)
        ahead = _pool_select(grp, *_leading_sums(dpooled_e / cnt_e))[:r]
        du_ref[:, 3 * cw:4 * cw] = (ahead - dpooled_e[:r]).astype(du_ref.dtype)

    last_halo = t // HALO - 1

    def main(col):
        return pl.BlockSpec((r, cw), lambda i: (i, col))

    def prev(col):
        return pl.BlockSpec((HALO, cw), lambda i: (jnp.maximum(i * hb - 1, 0), col))

    def nxt(col):
        return pl.BlockSpec((HALO, cw), lambda i: (jnp.minimum((i + 1) * hb, last_halo), col))

    def whole(a):
        return pl.BlockSpec(a.shape, lambda i: (0, 0))

    return pl.pallas_call(
        body, name=name, grid=(t // r,),
        in_specs=[main(0), main(1), main(2), main(3), nxt(0), prev(1), prev(2), prev(3), main(0), main(1), nxt(0),
                  nxt(1), whole(wconv), whole(wbd), whole(pscale)],
        out_specs=[pl.BlockSpec((r, 4 * cw), lambda i: (i, 0)), pl.BlockSpec((32, cw), lambda i: (0, 0)),
                   pl.BlockSpec((cw, cw), lambda i: (0, 0))],
        out_shape=[jax.ShapeDtypeStruct((t, 4 * cw), BF16), jax.ShapeDtypeStruct((32, cw), F32),
                   jax.ShapeDtypeStruct((cw, cw), F32)],
        compiler_params=_params(("arbitrary",)),
    )(u_cp, u_cp, u_cp, u_cp, u_cp, u_cp, u_cp, u_cp, dy, dy, dy, dy, wconv, wbd, pscale)


KW = 2 * QB
HP = 4
DECAY = 64.0


def _cumsum_matrix(before, kw):
    r = lax.broadcasted_iota(jnp.int32, (kw, kw), 0)
    c = lax.broadcasted_iota(jnp.int32, (kw, kw), 1)
    return ((r < c) if before else (r > c)).astype(BF16)


def _running(v, mat):
    m = v.shape[0]
    hi = v.astype(BF16)
    ext = _nn(jnp.concatenate([hi, (v - hi.astype(F32)).astype(BF16)], axis=0), mat)
    return ext[:m] + ext[m:]


def _log_sigmoid(z):
    neg_abs = lax.bitcast_convert_type(lax.bitcast_convert_type(z, jnp.int32) | jnp.int32(-2 ** 31), F32)
    return jnp.minimum(z, 0.0) - jnp.log(1.0 + jnp.exp(neg_abs))


def _stack_heads(v, head0):
    zero = jnp.zeros_like(v)
    return jnp.concatenate([jnp.where(head0, v, zero), jnp.where(head0, zero, v)], axis=0)


def _lanes(hp):
    return slice(hp * QB, (hp + 1) * QB)


def _attn_fwd(name, qkv, bl, lp, rider=None):
    t = qkv.shape[0]
    nq, nblk = lp // QB, qkv.shape[1] // (3 * HP * QB)
    assert nblk == 1
    ride = _Ride(rider, 3, 3, bl * nblk * nq)

    def body(*refs):
        q_ref, k_ref, v_ref, o_ref, lt_ref, g0_ref, run_s, acc_s = ride.own(refs)
        qi = pl.program_id(2)
        step = (pl.program_id(0) * nblk + pl.program_id(1)) * nq + qi
        ride.before(refs, step)
        head0 = lax.broadcasted_iota(jnp.int32, (QB, QB), 1) < QB // 2
        q2 = [_stack_heads(q_ref[:, _lanes(hp)] * jnp.asarray(HEAD_SCALE, BF16), head0) for hp in range(HP)]
        later = {KW: _cumsum_matrix(False, KW)}
        q_pos = qi * QB + (lax.broadcasted_iota(jnp.int32, (2 * QB, KW), 0) & (QB - 1))
        col = lax.broadcasted_iota(jnp.int32, (2 * QB, KW), 1)
        rest = qi // 2

        def group(start, kw, masked, lo=0):
            start = pl.multiple_of(start, QB)
            if masked:
                k_pos = start + col[:, :kw]
                valid = (k_pos < q_pos[:, :kw]) & (k_pos >= jnp.maximum(lo, PAD))
            z = [_nt(q2[hp], k_ref[pl.ds(start, kw), _lanes(hp)]) for hp in range(HP)]
            logp, after, rs = [], [], []
            for hp in range(HP):
                lp_ = _log_sigmoid(z[hp])
                lk = lp_ - z[hp]
                if masked:
                    lk = jnp.where(valid, lk, 0.0)
                logp.append(lp_)
                rs.append(jnp.sum(lk, axis=1, keepdims=True))
                after.append(_running(lk, later[kw]))
            for hp in range(HP):
                run = run_s[hp]
                a = jnp.exp(logp[hp] + after[hp] + run)
                if masked:
                    a = jnp.where(valid, a, 0.0)
                run_s[hp] = run + rs[hp]
                acc_s[hp] += _nn(a.astype(BF16), v_ref[pl.ds(start, kw), _lanes(hp)])

        def alive():
            most = run_s[0]
            for hp in range(1, HP):
                most = jnp.maximum(most, run_s[hp])
            return jnp.max(most) > -DECAY

        def older(st):
            group(st[0] * KW, KW, False)
            return st[0] - 1, alive()

        run_s[...] = jnp.zeros_like(run_s)
        acc_s[...] = jnp.zeros_like(acc_s)
        group(jnp.minimum(rest * KW, lp - KW), KW, True, lo=rest * KW)
        g, live = lax.while_loop(lambda st: (st[0] >= 1) & st[1], older, (rest - 1, alive()))
        oldest = (g == 0) & (rest >= 1) & live
        pl.when(oldest)(lambda: group(0, KW, True))
        g0_ref[pl.program_id(0), qi] = jnp.where(oldest, 0, g + 1).astype(F32)
        for hp in range(HP):
            o_ref[:, _lanes(hp)] = jnp.where(head0, acc_s[hp, :QB], acc_s[hp, QB:]).astype(o_ref.dtype)
            lt_ref[:, _lanes(hp)] = jnp.where(head0, run_s[hp, :QB], run_s[hp, QB:])
        ride.after(refs, step)

    wb = HP * QB
    blk = pl.BlockSpec((QB, wb), lambda b, p, i: (b * nq + i, p))
    return ride.call(
        body, name, (bl, nblk, nq),
        [blk, pl.BlockSpec((lp, wb), lambda b, p, i: (b, nblk + p)),
         pl.BlockSpec((lp, wb), lambda b, p, i: (b, 2 * nblk + p))], [qkv, qkv, qkv],
        [blk, blk, pl.BlockSpec(memory_space=pltpu.SMEM)],
        [jax.ShapeDtypeStruct((t, nblk * wb), BF16), jax.ShapeDtypeStruct((t, nblk * wb), F32),
         jax.ShapeDtypeStruct((bl, nq), F32)],
        [pltpu.VMEM((HP, 2 * QB, 1), F32), pltpu.VMEM((HP, 2 * QB, QB), F32)])


def _attn_bwd(name, qkv, lt, g0, dy, bl, lp, rider=None):
    t = qkv.shape[0]
    nq, nblk = lp // QB, qkv.shape[1] // (3 * HP * QB)
    ride = _Ride(rider, 6, 3, bl * nblk * nq)

    def body(*refs):
        (q_ref, k_ref, v_ref, lt_ref, do_ref, g0_ref, dq_ref, dk_ref, dv_ref, dk_acc, dv_acc, seen_s, gsum_s,
         dq_s) = ride.own(refs)
        qi = pl.program_id(2)
        step = (pl.program_id(0) * nblk + pl.program_id(1)) * nq + qi
        ride.before(refs, step)

        @pl.when(qi == 0)
        def _():
            dk_acc[...] = jnp.zeros_like(dk_acc)
            dv_acc[...] = jnp.zeros_like(dv_acc)
        lane = lax.broadcasted_iota(jnp.int32, (QB, QB), 1)
        head0 = lane < QB // 2
        q2, do2, total = [], [], []
        for hp in range(HP):
            q2.append(_stack_heads(q_ref[:, _lanes(hp)] * jnp.asarray(HEAD_SCALE, BF16), head0))
            do2.append(_stack_heads(do_ref[:, _lanes(hp)].astype(BF16), head0))
            ltv = lt_ref[:, _lanes(hp)]
            total.append(jnp.concatenate(
                [jnp.sum(jnp.where(lane == 0, ltv, 0.0), axis=1, keepdims=True),
                 jnp.sum(jnp.where(lane == QB // 2, ltv, 0.0), axis=1, keepdims=True)], axis=0))
        later = {KW: _cumsum_matrix(False, KW), QB: _cumsum_matrix(False, QB)}
        earlier = {KW: _cumsum_matrix(True, KW), QB: _cumsum_matrix(True, QB)}
        q_pos = qi * QB + (lax.broadcasted_iota(jnp.int32, (2 * QB, KW), 0) & (QB - 1))
        col = lax.broadcasted_iota(jnp.int32, (2 * QB, KW), 1)
        rest = qi // 2

        def group(start, kw, masked):
            start = pl.multiple_of(start, QB)
            if masked:
                k_pos = start + col[:, :kw]
                valid = (k_pos < q_pos[:, :kw]) & (k_pos >= PAD)
            hps = range(HP)
            kg = [k_ref[pl.ds(start, kw), _lanes(hp)] for hp in hps]
            z = [_nt(q2[hp], kg[hp]) for hp in hps]
            da = [_nt(do2[hp], v_ref[pl.ds(start, kw), _lanes(hp)]) for hp in hps]
            logp, sig, after, rs = [], [], [], []
            for hp in hps:
                lp_ = _log_sigmoid(z[hp])
                lk = lp_ - z[hp]
                if masked:
                    lk = jnp.where(valid, lk, 0.0)
                logp.append(lp_)
                sig.append(jnp.exp(lp_))
                rs.append(jnp.sum(lk, axis=1, keepdims=True))
                after.append(_running(lk, later[kw]))
            a, gg, before = [], [], []
            for hp in hps:
                a_ = jnp.exp(logp[hp] + after[hp] + (total[hp] - seen_s[hp] - rs[hp]))
                if masked:
                    a_ = jnp.where(valid, a_, 0.0)
                a.append(a_.astype(BF16))
                gg.append(a_ * da[hp])
                before.append(_nn(gg[hp].astype(BF16), earlier[kw]))
            for hp in hps:
                gsum = gsum_s[hp]
                dz = gg[hp] - (gg[hp] + before[hp] + gsum) * sig[hp]
                if masked:
                    dz = jnp.where(valid, dz, 0.0)
                dz = dz.astype(BF16)
                dk_acc[pl.ds(start, kw), _lanes(hp)] += _tn(dz, q2[hp])
                dv_acc[pl.ds(start, kw), _lanes(hp)] += _tn(a[hp], do2[hp])
                seen_s[hp] += rs[hp]
                gsum_s[hp] = gsum + jnp.sum(gg[hp], axis=1, keepdims=True)
                dq_s[hp] += _nn(dz, kg[hp])

        def inner(g, _):
            group(g * KW, KW, False)
            return 0

        seen_s[...] = jnp.zeros_like(seen_s)
        gsum_s[...] = jnp.zeros_like(gsum_s)
        dq_s[...] = jnp.zeros_like(dq_s)
        first = g0_ref[pl.program_id(0), qi].astype(jnp.int32)
        odd = qi % 2
        pl.when((first == 0) & (rest >= 1))(lambda: group(0, KW, True))
        lax.fori_loop(jnp.maximum(first, 1), rest, inner, 0)
        pl.when(odd == 1)(lambda: group(rest * KW, KW, True))
        pl.when(odd == 0)(lambda: group(qi * QB, QB, True))
        for hp in range(HP):
            dq_ref[:, _lanes(hp)] = (jnp.where(head0, dq_s[hp, :QB], dq_s[hp, QB:])
                                     * HEAD_SCALE).astype(dq_ref.dtype)

        @pl.when(qi == nq - 1)
        def _():
            dk_ref[...] = dk_acc[...].astype(dk_ref.dtype)
            dv_ref[...] = dv_acc[...].astype(dv_ref.dtype)
        ride.after(refs, step)

    wb = HP * QB
    blk = pl.BlockSpec((QB, wb), lambda b, p, i: (b * nq + i, p))
    seq = pl.BlockSpec((lp, wb), lambda b, p, i: (b, p))
    out = jax.ShapeDtypeStruct((t, nblk * wb), BF16)
    return ride.call(
        body, name, (bl, nblk, nq),
        [blk, pl.BlockSpec((lp, wb), lambda b, p, i: (b, nblk + p)),
         pl.BlockSpec((lp, wb), lambda b, p, i: (b, 2 * nblk + p)), blk,
         pl.BlockSpec((QB, wb), lambda b, p, i: (b * nq + i, nblk + p)), pl.BlockSpec(memory_space=pltpu.SMEM)],
        [qkv, qkv, qkv, lt, dy, g0],
        [blk, seq, seq], [out, out, out],
        [pltpu.VMEM((lp, wb), F32), pltpu.VMEM((lp, wb), F32), pltpu.VMEM((HP, 2 * QB, 1), F32),
         pltpu.VMEM((HP, 2 * QB, 1), F32), pltpu.VMEM((HP, 2 * QB, QB), F32)])


def _place():
    return lax.axis_index("x"), lax.axis_index("y"), lax.axis_index("c")


def _peers(chip):
    kx, ky = chip // 2, chip % 2
    return ((1 - kx, ky), (kx, 1 - ky), (1 - kx, 1 - ky))


def _hbm_specs(n):
    return [pl.BlockSpec(memory_space=pl.ANY) for _ in range(n)]


def _remote(src, dst, send, recv, k, to):
    return pltpu.make_async_remote_copy(src, dst, send.at[k], recv.at[k], device_id=to, device_id_type=MESH)


class _Rider:
    def __init__(self, ins, out_shapes, aliases, nsem, first, mid=None, last=None):
        self.ins, self.out_shapes, self.aliases, self.nsem = list(ins), list(out_shapes), dict(aliases), nsem
        self.first, self.mid, self.last = first, mid, last


def _by_chip(fn):
    def run(ins, outs, send, recv):
        x, y, c = _place()
        for me in range(4):
            pl.when(2 * x + y == me)(functools.partial(fn, ins, outs, send, recv, me, c, (x, y, 1 - c)))
    return run


def _run_rider(name, rider):
    ni, no = len(rider.ins), len(rider.out_shapes)

    def body(*refs):
        args = (refs[:ni], refs[ni:ni + no], refs[ni + no], refs[ni + no + 1])
        for hook in (rider.first, rider.mid, rider.last):
            if hook is not None:
                hook(*args)

    return pl.pallas_call(
        body, name=name, in_specs=_hbm_specs(ni), out_specs=_hbm_specs(no), out_shape=rider.out_shapes,
        input_output_aliases=rider.aliases,
        scratch_shapes=[pltpu.SemaphoreType.DMA((rider.nsem,)), pltpu.SemaphoreType.DMA((rider.nsem,))],
        compiler_params=pltpu.CompilerParams(has_side_effects=True),
    )(*rider.ins)


class _Ride:
    def __init__(self, rider, n_in, n_out, steps):
        self.rider, self.n_in, self.n_out, self.steps = rider, n_in, n_out, steps
        self.ri = len(rider.ins) if rider else 0
        self.ro = len(rider.out_shapes) if rider else 0

    def own(self, refs):
        refs = list(refs)
        a, b = self.n_in, self.n_in + self.ri + self.n_out
        tail = refs[b + self.ro:len(refs) - 2] if self.rider else refs[b + self.ro:]
        return refs[:a] + refs[a + self.ri:b] + tail

    def _args(self, refs):
        a, b = self.n_in, self.n_in + self.ri + self.n_out
        return refs[a:a + self.ri], refs[b:b + self.ro], refs[-2], refs[-1]

    def before(self, refs, step):
        if self.rider is None:
            return
        pl.when(step == 0)(functools.partial(self.rider.first, *self._args(refs)))
        if self.rider.mid is not None:
            pl.when(step == (3 * self.steps) // 4)(functools.partial(self.rider.mid, *self._args(refs)))

    def after(self, refs, step):
        if self.rider is not None and self.rider.last is not None:
            pl.when(step == self.steps - 1)(functools.partial(self.rider.last, *self._args(refs)))

    def call(self, body, name, grid, in_specs, args, out_specs, out_shape, scratch,
             sem=("parallel", "parallel", "arbitrary")):
        r = self.rider
        if r is None:
            return pl.pallas_call(body, name=name, grid=grid, in_specs=in_specs, out_specs=out_specs,
                                  out_shape=out_shape, scratch_shapes=scratch, compiler_params=_params(sem))(*args)
        return pl.pallas_call(
            body, name=name, grid=grid, in_specs=in_specs + _hbm_specs(self.ri),
            out_specs=out_specs + _hbm_specs(self.ro), out_shape=out_shape + r.out_shapes,
            input_output_aliases={self.n_in + i: self.n_out + o for i, o in r.aliases.items()},
            scratch_shapes=scratch + [pltpu.SemaphoreType.DMA((r.nsem,)), pltpu.SemaphoreType.DMA((r.nsem,))],
            compiler_params=pltpu.CompilerParams(dimension_semantics=("arbitrary",) * len(grid),
                                                 vmem_limit_bytes=VMEM_LIMIT, has_side_effects=True),
        )(*args, *r.ins)


def _core_view(a, axis):
    l, r, c = a.shape
    return a.reshape(l, 4, 2, r // 8, c) if axis == 0 else a.reshape(l, 2, r // 2, c)


def _shard_view(a):
    l, r, c = a.shape
    return a.reshape(l, 2, r // 2, c)


def _piece(ref, axis, layer, chip, core):
    if axis == 0:
        return ref.at[layer, chip, core]
    cs = ref.shape[-1] // 4
    return ref.at[layer, core, :, pl.ds(chip * cs, cs)]


def _place_shard(name, w, axis, kidx, tr):
    _, r, cdim = w.shape
    shp = [2, r, cdim]
    shp[1 + axis] *= 4
    nb = r // tr

    def body(k_ref, w_ref, o_ref):
        o_ref[...] = w_ref[...].astype(o_ref.dtype)

    if axis == 0:
        out_spec = pl.BlockSpec((None, tr, cdim), lambda l, i, k_ref: (l, k_ref[0] * nb + i, 0))
    else:
        out_spec = pl.BlockSpec((None, tr, cdim), lambda l, i, k_ref: (l, i, k_ref[0]))
    return pl.pallas_call(
        body, name=name,
        grid_spec=pltpu.PrefetchScalarGridSpec(
            num_scalar_prefetch=1, grid=(2, nb),
            in_specs=[pl.BlockSpec((None, tr, cdim), lambda l, i, k_ref: (l, i, 0))], out_specs=out_spec),
        out_shape=jax.ShapeDtypeStruct(tuple(shp), BF16),
        compiler_params=_params(("arbitrary", "arbitrary")),
    )(kidx, w)


def _gather_rider(views, axes, items):
    n = len(items)

    def first(ins, outs, send, recv, me, c, sib):
        for i, (w, l) in enumerate(items):
            for j, (px, py) in enumerate(_peers(me)):
                _remote(_piece(ins[w], axes[w], l, me, c), _piece(outs[w], axes[w], l, me, c), send, recv,
                        3 * i + j, (px, py, c)).start()

    def mid(ins, outs, send, recv, me, c, sib):
        for i, (w, l) in enumerate(items):
            for j, (px, py) in enumerate(_peers(me)):
                got = _piece(outs[w], axes[w], l, 2 * px + py, c)
                _remote(got, got, send, recv, 3 * i + j, (px, py, c)).wait_recv()
                _remote(got, got, send, recv, 3 * (n + i) + j, sib).start()

    def last(ins, outs, send, recv, me, c, sib):
        for i, (w, l) in enumerate(items):
            for j, (px, py) in enumerate(_peers(me)):
                mine, got = _piece(outs[w], axes[w], l, me, c), _piece(outs[w], axes[w], l, 2 * px + py, c)
                theirs = _piece(outs[w], axes[w], l, 2 * px + py, 1 - c)
                _remote(theirs, theirs, send, recv, 3 * (n + i) + j, sib).wait_recv()
                _remote(mine, mine, send, recv, 3 * i + j, (px, py, c)).wait_send()
                _remote(got, got, send, recv, 3 * (n + i) + j, sib).wait_send()

    return _Rider(views, [jax.ShapeDtypeStruct(v.shape, v.dtype) for v in views], {w: w for w in range(len(views))},
                  6 * n, _by_chip(first), _by_chip(mid), _by_chip(last))


def _swap_rider(views, axes, items):
    nv = len(views)

    def part(ref, w, l, core):
        return ref.at[l, :, core] if axes[w] == 0 else ref.at[l, core]

    def copies(ins, outs, send, recv):
        x, y, c = _place()
        return [_remote(part(ins[w], w, l, 1 - c), outs[nv + i], send, recv, i, (x, y, 1 - c))
                for i, (w, l) in enumerate(items)]

    def first(ins, outs, send, recv):
        for cp in copies(ins, outs, send, recv):
            cp.start()

    def last(ins, outs, send, recv):
        for cp in copies(ins, outs, send, recv):
            cp.wait()

    got = [jax.ShapeDtypeStruct(views[w].shape[1:2] + views[w].shape[3:] if axes[w] == 0 else views[w].shape[2:],
                                views[w].dtype) for w, _ in items]
    return _Rider(views, [jax.ShapeDtypeStruct(v.shape, v.dtype) for v in views] + got,
                  {w: w for w in range(nv)}, len(items), first, None, last)


def _add_core(name, view, got, axis, layer, cidx, tr):
    def body(c_ref, g_ref, r_ref, o_ref):
        o_ref[...] = (g_ref[...] + r_ref[...]).astype(o_ref.dtype)

    if axis == 0:
        _, nchip, _, pr, cdim = view.shape
        grid = (nchip, pr // tr)
        specs = [pl.BlockSpec((None, None, None, tr, cdim), lambda k, i, c_ref: (layer, k, c_ref[0], i, 0)),
                 pl.BlockSpec((None, tr, cdim), lambda k, i, c_ref: (k, i, 0))]
        out_spec = pl.BlockSpec((None, tr, cdim), lambda k, i, c_ref: (k, i, 0))
    else:
        _, _, pr, cdim = view.shape
        grid = (pr // tr,)
        specs = [pl.BlockSpec((None, None, tr, cdim), lambda i, c_ref: (layer, c_ref[0], i, 0)),
                 pl.BlockSpec((tr, cdim), lambda i, c_ref: (i, 0))]
        out_spec = pl.BlockSpec((tr, cdim), lambda i, c_ref: (i, 0))
    return pl.pallas_call(
        body, name=name,
        grid_spec=pltpu.PrefetchScalarGridSpec(num_scalar_prefetch=1, grid=grid, in_specs=specs,
                                               out_specs=out_spec),
        out_shape=jax.ShapeDtypeStruct(got.shape, BF16),
        compiler_params=_params(("arbitrary",) * len(grid)),
    )(cidx, view, got)


def _scatter_rider(sums, axes):
    def part(ref, i, chip):
        if axes[i] == 0:
            return ref.at[chip]
        cs = ref.shape[-1] // 4
        return ref.at[:, pl.ds(chip * cs, cs)]

    def copies(ins, outs, send, recv, me, c, sib):
        return [_remote(part(ins[i], i, 2 * px + py), outs[i].at[j], send, recv, 3 * i + j, (px, py, c))
                for i in range(len(sums)) for j, (px, py) in enumerate(_peers(me))]

    def first(*args):
        for cp in copies(*args):
            cp.start()

    def last(*args):
        for cp in copies(*args):
            cp.wait()

    shapes = [jax.ShapeDtypeStruct((3,) + (s.shape[1:] if ax == 0 else (s.shape[0], s.shape[1] // 4)), s.dtype)
              for s, ax in zip(sums, axes)]
    return _Rider(sums, shapes, {}, 3 * len(sums), _by_chip(first), None, _by_chip(last))


def _add_chips(name, own, got, axis, layer, kc_idx, tr, into, shard_shape):
    _, pr, pc = got.shape

    def body(k_ref, o_ref, g_ref, *rest):
        rest[-1][...] = (o_ref[...].astype(F32) + g_ref[0].astype(F32) + g_ref[1].astype(F32)
                         + g_ref[2].astype(F32))

    if axis == 0:
        own_spec = pl.BlockSpec((None, tr, pc), lambda i, k_ref: (k_ref[0], i, 0))
    else:
        own_spec = pl.BlockSpec((tr, pc), lambda i, k_ref: (i, k_ref[0]))
    specs = [own_spec, pl.BlockSpec((3, tr, pc), lambda i, k_ref: (0, i, 0))]
    args = [kc_idx, own, got]
    if into is not None:
        specs.append(pl.BlockSpec(memory_space=pl.ANY))
        args.append(into)
    return pl.pallas_call(
        body, name=name,
        grid_spec=pltpu.PrefetchScalarGridSpec(
            num_scalar_prefetch=1, grid=(pr // tr,), in_specs=specs,
            out_specs=pl.BlockSpec((None, None, tr, pc), lambda i, k_ref: (layer, k_ref[1], i, 0))),
        out_shape=jax.ShapeDtypeStruct(shard_shape, F32),
        input_output_aliases={} if into is None else {3: 0},
        compiler_params=_params(("arbitrary",)),
    )(*args)


def _join_rider(parts):
    def first(ins, outs, send, recv):
        x, y, c = _place()
        for w in range(len(parts)):
            _remote(ins[w].at[:, c], outs[w].at[:, c], send, recv, w, (x, y, 1 - c)).start()

    def last(ins, outs, send, recv):
        x, y, c = _place()
        for w in range(len(parts)):
            _remote(ins[w].at[:, c], outs[w].at[:, c], send, recv, w, (x, y, 1 - c)).wait_send()
            _remote(ins[w].at[:, c], outs[w].at[:, 1 - c], send, recv, w, (x, y, 1 - c)).wait_recv()

    return _Rider(parts, [jax.ShapeDtypeStruct(p.shape, p.dtype) for p in parts],
                  {w: w for w in range(len(parts))}, len(parts), first, None, last)


def _all_reduce_small(name, pack, lead, groups):
    nr, d = pack.shape
    nout = nr - (groups - 1) * lead

    def body(in_ref, sum_ref, mine, slots, send, recv):
        x, y, c = _place()
        me = 4 * x + 2 * y + c
        fold = in_ref[0:lead]
        for grp in range(1, groups):
            fold = fold + in_ref[grp * lead:(grp + 1) * lead]
        mine[0:lead] = fold
        mine[lead:] = in_ref[groups * lead:]
        slots[me] = mine[...]
        cps = []
        for r in range(1, 8):
            rx, ry, rc = r // 4, (r // 2) % 2, r % 2
            peer = (x + rx - 2 * x * rx, y + ry - 2 * y * ry, c + rc - 2 * c * rc)
            cp = pltpu.make_async_remote_copy(mine, slots.at[me], send.at[r - 1], recv.at[r - 1],
                                              device_id=peer, device_id_type=MESH)
            cp.start()
            cps.append(cp)
        for cp in cps:
            cp.wait()
        acc = slots[0]
        for dev in range(1, 8):
            acc = acc + slots[dev]
        sum_ref[...] = acc

    vmem = pl.BlockSpec(memory_space=pltpu.VMEM)
    return pl.pallas_call(
        body, name=name, in_specs=[vmem], out_specs=vmem, out_shape=jax.ShapeDtypeStruct((nout, d), F32),
        scratch_shapes=[pltpu.VMEM((nout, d), F32), pltpu.VMEM((8, nout, d), F32), pltpu.SemaphoreType.DMA((7,)),
                        pltpu.SemaphoreType.DMA((7,))],
        compiler_params=pltpu.CompilerParams(has_side_effects=True, vmem_limit_bytes=VMEM_LIMIT),
    )(pack)


def _adamw_math(w, g, m, v):
    m = B1 * m + (1.0 - B1) * g
    v = B2 * v + (1.0 - B2) * (g * g)
    m_hat = m / (1.0 - B1 ** STEP)
    v_hat = v / (1.0 - B2 ** STEP)
    return -LR * (m_hat / (jnp.sqrt(v_hat) + ADAM_EPS) + WD * w), m, v


def _adamw(name, w, g, m, v, tr):
    shape = w.shape
    flat = [a.reshape(-1, shape[-1]) for a in (w, g, m, v)]
    r, cdim = flat[0].shape

    def body(w_ref, g_ref, m_ref, v_ref, d_ref, nm_ref, nv_ref):
        d_ref[...], nm_ref[...], nv_ref[...] = _adamw_math(w_ref[...], g_ref[...], m_ref[...], v_ref[...])

    spec = pl.BlockSpec((tr, cdim), lambda i: (i, 0))
    outs = pl.pallas_call(
        body, name=name, grid=(r // tr,), in_specs=[spec] * 4, out_specs=[spec] * 3,
        out_shape=[jax.ShapeDtypeStruct((r, cdim), F32)] * 3,
        compiler_params=_params(("parallel",)),
    )(*flat)
    return [o.reshape(shape) for o in outs]


def _adamw_small(name, groups):
    n = len(groups)
    shapes = [grp[0].shape for grp in groups]
    flat = [a.reshape(-1, a.shape[-1]) for grp in groups for a in grp]

    def body(*refs):
        ins, outs = refs[:4 * n], refs[4 * n:]
        for i in range(n):
            w_ref, g_ref, m_ref, v_ref = ins[4 * i:4 * i + 4]
            outs[3 * i][...], outs[3 * i + 1][...], outs[3 * i + 2][...] = _adamw_math(
                w_ref[...], g_ref[...], m_ref[...], v_ref[...])

    vmem = pl.BlockSpec(memory_space=pltpu.VMEM)
    out_shape = [jax.ShapeDtypeStruct(flat[4 * i].shape, F32) for i in range(n) for _ in range(3)]
    outs = pl.pallas_call(body, name=name, in_specs=[vmem] * (4 * n), out_specs=[vmem] * (3 * n),
                          out_shape=out_shape)(*flat)
    return [[outs[3 * i + j].reshape(shapes[i]) for j in range(3)] for i in range(n)]


def _block_diag(w_grp):
    g, pg, _ = w_grp.shape
    eye = jnp.eye(g, dtype=w_grp.dtype)
    return (eye[:, None, :, None] * w_grp[:, :, None, :]).reshape(g * pg, g * pg)


def _diag_blocks(m, g):
    pg = m.shape[0] // g
    return jnp.stack([m[i * pg:(i + 1) * pg, i * pg:(i + 1) * pg] for i in range(g)])


BIG = ("w_in", "w_out", "w_up", "w_down")
AXES = (1, 0, 1, 0)
W_IN, W_OUT, W_UP, W_DOWN = range(4)


def kernel(x, meta_tokens, g_mix, w_in, w_conv, w_pool, pool_scale, w_out, g_mlp, w_up, w_down, g_final, loss_target, m_meta_tokens, m_g_mix, m_w_in, m_w_conv, m_w_pool, m_pool_scale, m_w_out, m_g_mlp, m_w_up, m_w_down, m_g_final, v_meta_tokens, v_g_mix, v_w_in, v_w_conv, v_w_pool, v_pool_scale, v_w_out, v_g_mlp, v_w_up, v_w_down, v_g_final):
    bl, s, d = x.shape
    depth = g_mix.shape[0]
    assert depth == 2
    lp = PAD + N_META + s
    tt = lp
    tm = lp // 4
    copy_rows, sum_rows, dw_tile = 256, 128, 1024
    cs = w_conv.shape[2]
    cw = 4 * cs
    ngrp = w_pool.shape[1]
    xi, yi, ci = _place()
    chip = (2 * xi + yi).astype(jnp.int32)
    cidx, kidx = ci.astype(jnp.int32).reshape(1), chip.reshape(1)
    kc_idx = jnp.stack([chip, ci.astype(jnp.int32)])
    shards = (w_in, w_out, w_up, w_down)

    views = [_core_view(_place_shard(f"place_{BIG[w]}", shards[w], AXES[w], kidx, copy_rows), AXES[w])
             for w in range(4)]

    def whole(w):
        return views[w].reshape(depth, -1, views[w].shape[-1])

    def gather_on(call, items):
        ws = sorted({w for w, _ in items})
        res = call(_gather_rider([views[w] for w in ws], [AXES[w] for w in ws],
                                 [(ws.index(w), layer) for w, layer in items]))
        for j, w in enumerate(ws):
            views[w] = res[len(res) - len(ws) + j]
        return res[:len(res) - len(ws)]

    placed = jnp.zeros((32, d), F32)
    placed = lax.dynamic_update_slice(placed, meta_tokens, (0, chip * meta_tokens.shape[1]))
    placed = lax.dynamic_update_slice(placed, w_conv.reshape(-1, cs), (N_META, chip * cs))
    placed = jnp.where(ci == 0, placed, 0.0)
    small = _all_reduce_small("gather_small", placed, 8, 1)
    meta_full = small[:N_META]
    conv_full = small[N_META:N_META + depth * 3, :cw].reshape(depth, 3, cw)

    (h,) = gather_on(lambda rider: _build_h("build_h", x, meta_full, lp, rider), [(W_IN, 0)])
    wbd = [_block_diag(w_pool[i]).astype(BF16) for i in range(depth)]
    saved = []
    for i in range(depth):
        hn, u_cp, qkv = _in_proj(f"in_proj{i}", h, g_mix[i], whole(W_IN), i, tm, 4 * cw)
        y_cp = _convpool_fwd(f"convpool{i}", u_cp, conv_full[i], wbd[i], pool_scale[i:i + 1], lp, tm)
        if i == 0:
            y_at, lt, g0 = gather_on(lambda rider: _attn_fwd(f"attn{i}", qkv, bl, lp, rider),
                                     [(W_OUT, 0), (W_UP, 0), (W_DOWN, 0), (W_IN, 1)])
            h_mid = _out_proj(f"out_proj{i}", y_cp, y_at, h, whole(W_OUT), i, tm)
            w_up0 = whole(W_UP)
            hn2, m_pre, act = gather_on(lambda rider: _up_proj(f"up_proj{i}", h_mid, g_mlp[i], w_up0, i, tm, rider),
                                        [(W_OUT, 1), (W_DOWN, 1)])
            w_down0 = whole(W_DOWN)
            (h_next,) = gather_on(lambda rider: _down_proj(f"down_proj{i}", act, h_mid, w_down0, i, tm, rider),
                                  [(W_UP, 1)])
        else:
            y_at, lt, g0 = _attn_fwd(f"attn{i}", qkv, bl, lp)
            h_mid = _out_proj(f"out_proj{i}", y_cp, y_at, h, whole(W_OUT), i, tm)
            hn2, m_pre, act = _up_proj(f"up_proj{i}", h_mid, g_mlp[i], whole(W_UP), i, tm)
            (h_next,) = _down_proj(f"down_proj{i}", act, h_mid, whole(W_DOWN), i, tm)
        saved.append((h, hn, u_cp, qkv, y_cp, y_at, (lt, g0), h_mid, hn2, m_pre, act))
        h = h_next

    dh, loss8, dgf8 = _loss_bwd("loss", h, g_final, loss_target, lp)
    per_layer = {k: [None] * depth for k in ("g_mix", "w_conv", "w_pool", "pool_scale", "g_mlp")}

    gw = [None] * 4
    sums, arrived = {}, {}

    def dw(name, a, b, w, layer):
        shape = whole(w).shape
        into = None if gw[w] is None else gw[w].reshape(shape)
        if isinstance(a, list):
            res = _mm_tn_slab(name, a, b, tt // 2, into, shape, layer)
        else:
            res = _mm_tn(name, a, b, tt, dw_tile, dw_tile, into, shape, layer, 0, 0)
        gw[w] = _core_view(res, AXES[w])

    def swap_rider(ws):
        return _swap_rider([gw[w] for w, _ in ws], [AXES[w] for w, _ in ws],
                           [(j, layer) for j, (_, layer) in enumerate(ws)])

    def swapped(ws, outs):
        for j, (w, layer) in enumerate(ws):
            gw[w] = outs[j]
            sums[w, layer] = _add_core(f"chip_sum_{BIG[w]}{layer}", gw[w], outs[len(ws) + j], AXES[w], layer, cidx,
                                       sum_rows)

    def scatter_rider(items):
        return _scatter_rider([sums[it] for it in items], [AXES[w] for w, _ in items])

    def bwd_mlp(i, dh, swap_early):
        _, _, _, _, y_cp, y_at, _, h_mid, hn2, m_pre, act = saved[i]
        dm = _down_proj_dx(f"down_proj_dx{i}", dh, m_pre, whole(W_DOWN), i, tm)
        dw(f"down_proj_dw{i}", act, dh, W_DOWN, i)
        dw(f"up_proj_dw{i}", hn2, dm, W_UP, i)
        ws = [(W_DOWN, i), (W_UP, i)] if swap_early else []
        dh_mid, dy, dg8, *outs = _up_proj_dx(f"up_proj_dx{i}", dm, h_mid, dh, g_mlp[i], whole(W_UP), whole(W_OUT), i,
                                             tm, swap_rider(ws) if ws else None)
        swapped(ws, outs)
        per_layer["g_mlp"][i] = dg8.sum(0)
        dw(f"out_proj_dw{i}", [y_cp, y_at], [dh_mid], W_OUT, i)
        return dh_mid, dy

    def bwd_mix(i, dh_mid, dy, dus3, scatter_late):
        h_in, hn, u_cp = saved[i][:3]
        du_cp, sm, dwbd = _convpool_bwd(f"convpool_bwd{i}", u_cp, dy, conv_full[i], wbd[i], pool_scale[i:i + 1], lp,
                                        tm)
        sm = sm.reshape(4, 8, cw).sum(1)
        per_layer["w_conv"][i] = sm[0:3]
        per_layer["pool_scale"][i] = sm[3]
        per_layer["w_pool"][i] = _diag_blocks(dwbd, ngrp)
        dus = [du_cp, *dus3]
        dw(f"in_proj_dw{i}", [hn], dus, W_IN, i)
        rider = None
        if scatter_late:
            swapped([(W_IN, i)], _run_rider(f"grads_swap_in{i}", swap_rider([(W_IN, i)])))
            rider = scatter_rider([(W_IN, i)])
        dh, dg8, *outs = _in_proj_dx(f"in_proj_dx{i}", dus, h_in, dh_mid, g_mix[i], whole(W_IN), i, tm, rider)
        arrived.update(zip([(W_IN, i)], outs))
        per_layer["g_mix"][i] = dg8.sum(0)
        return dh

    def attn_bwd(i, dy, rider):
        qkv, (lt, g0) = saved[i][3], saved[i][6]
        res = _attn_bwd(f"attn_bwd{i}", qkv, lt, g0, dy, bl, lp, rider)
        return res[:3], res[3:]

    dh_mid, dy = bwd_mlp(1, dh, False)
    ws = [(W_DOWN, 1), (W_UP, 1), (W_OUT, 1)]
    dus3, outs = attn_bwd(1, dy, swap_rider(ws))
    swapped(ws, outs)
    dh = bwd_mix(1, dh_mid, dy, dus3, False)

    dh_mid, dy = bwd_mlp(0, dh, True)
    ws = [(W_IN, 1), (W_OUT, 0)]
    swapped(ws, _run_rider("grads_swap0", swap_rider(ws)))
    items = list(sums)
    dus3, outs = attn_bwd(0, dy, scatter_rider(items))
    arrived.update(zip(items, outs))
    dh0 = bwd_mix(0, dh_mid, dy, dus3, True)

    finals = []
    for w in range(4):
        rs_, cs_ = shards[w].shape[1:]
        part = None
        for layer in reversed(range(depth)):
            part = _add_chips(f"reduce_{BIG[w]}{layer}", sums[w, layer], arrived[w, layer], AXES[w], layer, kc_idx,
                              sum_rows, part, (depth, 2, rs_ // 2, cs_))
        finals.append(part)
    finals = _run_rider("grads_join", _join_rider(finals))
    grad = {BIG[w]: finals[w].reshape(shards[w].shape) for w in range(4)}

    dh0 = dh0.reshape(bl, lp, d)
    grad_x = dh0[:, PAD + N_META:]
    local = {k: jnp.stack(v) for k, v in per_layer.items()}
    pieces = [dh0[:, PAD:PAD + N_META].reshape(bl * N_META, d), local["g_mix"], local["g_mlp"],
              dgf8.sum(0).reshape(1, d),
              jnp.pad(local["w_conv"].reshape(-1), (0, 2 * d - local["w_conv"].size)).reshape(2, d),
              jnp.pad(local["pool_scale"].reshape(-1), (0, d - local["pool_scale"].size)).reshape(1, d),
              jnp.pad(loss8.sum(0, keepdims=True), ((0, 7), (0, 0))), local["w_pool"].reshape(-1, d)]
    summed = _all_reduce_small("small_grads", jnp.concatenate(pieces, axis=0), N_META, bl)
    o = N_META
    grad.update({
        "meta_tokens": lax.dynamic_slice_in_dim(summed[:o], chip * meta_tokens.shape[1], meta_tokens.shape[1], 1),
        "g_mix": summed[o:o + 2], "g_mlp": summed[o + 2:o + 4], "g_final": summed[o + 4],
        "w_conv": lax.dynamic_slice_in_dim(summed[o + 5:o + 7].reshape(-1)[:2 * 3 * cw].reshape(2, 3, cw),
                                           chip * cs, cs, 2),
        "pool_scale": summed[o + 7].reshape(-1)[:pool_scale.size].reshape(pool_scale.shape),
        "w_pool": summed[o + 16:].reshape(w_pool.shape),
    })
    loss = jnp.sum(summed[o + 8])

    weights = dict(meta_tokens=meta_tokens, g_mix=g_mix, w_in=w_in, w_conv=w_conv, w_pool=w_pool,
                   pool_scale=pool_scale, w_out=w_out, g_mlp=g_mlp, w_up=w_up, w_down=w_down, g_final=g_final)
    ms = dict(meta_tokens=m_meta_tokens, g_mix=m_g_mix, w_in=m_w_in, w_conv=m_w_conv, w_pool=m_w_pool,
              pool_scale=m_pool_scale, w_out=m_w_out, g_mlp=m_g_mlp, w_up=m_w_up, w_down=m_w_down,
              g_final=m_g_final)
    vs = dict(meta_tokens=v_meta_tokens, g_mix=v_g_mix, w_in=v_w_in, w_conv=v_w_conv, w_pool=v_w_pool,
              pool_scale=v_pool_scale, w_out=v_w_out, g_mlp=v_g_mlp, w_up=v_w_up, w_down=v_w_down,
              g_final=v_g_final)
    order = list(weights)
    upd = {k: _adamw(f"adamw_{k}", weights[k], grad[k], ms[k], vs[k], copy_rows) for k in BIG}
    little = [k for k in order if k not in BIG]
    for k, res in zip(little, _adamw_small("adamw_small", [(weights[k], grad[k].reshape(weights[k].shape), ms[k],
                                                            vs[k]) for k in little])):
        upd[k] = res
    grad = {k: grad[k].reshape(weights[k].shape) for k in order}
    return (loss, grad_x, *[grad[k] for k in order], *[upd[k][0] for k in order], *[upd[k][1] for k in order],
            *[upd[k][2] for k in order])
```

```python
import functools

import jax
import jax.numpy as jnp
from jax import lax
from jax.experimental import pallas as pl
from jax.experimental.pallas import tpu as pltpu

F32, BF16 = jnp.float32, jnp.bfloat16
MESH = pl.DeviceIdType.MESH
EPS = 1e-6
N_META = 16
QB = 128
PAD = QB - N_META
HALO = 16
POOL_WINDOWS = (2.0, 4.0, 8.0, 16.0)
HEAD_SCALE = 0.125
LR, B1, B2, ADAM_EPS, WD, STEP = 0.001, 0.9, 0.999, 1e-08, 0.01, 10
VMEM_LIMIT = 56 * 1024 * 1024


def _params(sem=None):
    return pltpu.CompilerParams(dimension_semantics=sem, vmem_limit_bytes=VMEM_LIMIT)


def _nt(a, b):
    return lax.dot_general(a, b, (((1,), (1,)), ((), ())), preferred_element_type=F32)


def _tn(a, b):
    return lax.dot_general(a, b, (((0,), (0,)), ((), ())), preferred_element_type=F32)


def _nn(a, b):
    return jnp.dot(a, b, preferred_element_type=F32)


def _fold8(v):
    r, c = v.shape
    return jnp.sum(v.reshape(r // 8, 8, c), axis=0)


NCH = 512


def _rows_call(name, body, tm, row_ins, consts, row_outs, accs=(), rider=None):
    t = row_ins[0].shape[0]
    ride = _Ride(rider, len(row_ins) + len(consts), len(row_outs) + len(accs), t // tm)

    def stepped(*refs):
        step = pl.program_id(0)
        ride.before(refs, step)
        body(*ride.own(refs))
        ride.after(refs, step)

    in_specs = [pl.BlockSpec((tm, a.shape[1]), lambda i: (i, 0)) for a in row_ins]
    for a, layer in consts:
        if layer is None:
            in_specs.append(pl.BlockSpec(a.shape, lambda i: (0, 0)))
        else:
            in_specs.append(pl.BlockSpec((None, *a.shape[1:]), lambda i, l=layer: (l, 0, 0)))
    return ride.call(
        stepped, name, (t // tm,), in_specs, [*row_ins, *[a for a, _ in consts]],
        [pl.BlockSpec((tm, c), lambda i: (i, 0)) for c, _ in row_outs]
        + [pl.BlockSpec(s, lambda i: (0, 0)) for s in accs],
        [jax.ShapeDtypeStruct((t, c), dt) for c, dt in row_outs] + [jax.ShapeDtypeStruct(s, F32) for s in accs],
        [], ("arbitrary",) if accs else ("parallel",))


def _norm_parts(x):
    r = lax.rsqrt(jnp.mean(x * x, axis=-1, keepdims=True) + EPS)
    return r, x * r


def _norm_bwd(r, xh, dyn, g):
    w = dyn * g
    return r * (w - xh * jnp.mean(w * xh, axis=-1, keepdims=True))


def _in_proj(name, h, g, w, layer, tm, ncp):
    d, n = h.shape[1], w.shape[2]

    def body(h_ref, g_ref, w_ref, hn_ref, ucp_ref, qkv_ref):
        _, xh = _norm_parts(h_ref[...])
        hn = (xh * g_ref[...]).astype(BF16)
        hn_ref[...] = hn
        for n0 in range(0, n, NCH):
            acc = _nn(hn, w_ref[:, n0:n0 + NCH])
            if n0 < ncp:
                ucp_ref[:, n0:n0 + NCH] = acc
            else:
                qkv_ref[:, n0 - ncp:n0 - ncp + NCH] = acc.astype(BF16)

    return _rows_call(name, body, tm, [h], [(g.reshape(1, d), None), (w, layer)],
                      [(d, BF16), (ncp, F32), (n - ncp, BF16)])


def _out_proj(name, y_cp, y_at, h, w, layer, tm):
    d, k1 = h.shape[1], y_cp.shape[1]

    def body(ycp_ref, yat_ref, h_ref, w_ref, o_ref):
        for n0 in range(0, d, NCH):
            o_ref[:, n0:n0 + NCH] = (h_ref[:, n0:n0 + NCH] + _nn(ycp_ref[...], w_ref[0:k1, n0:n0 + NCH])
                                     + _nn(yat_ref[...], w_ref[k1:, n0:n0 + NCH]))

    return _rows_call(name, body, tm, [y_cp, y_at, h], [(w, layer)], [(d, F32)])[0]


def _up_proj(name, h_mid, g, w, layer, tm, rider=None):
    d, n = h_mid.shape[1], w.shape[2]

    def body(h_ref, g_ref, w_ref, hn_ref, m_ref, act_ref):
        _, xh = _norm_parts(h_ref[...])
        hn = (xh * g_ref[...]).astype(BF16)
        hn_ref[...] = hn
        for n0 in range(0, n, NCH):
            acc = _nn(hn, w_ref[:, n0:n0 + NCH])
            m_ref[:, n0:n0 + NCH] = acc.astype(BF16)
            act_ref[:, n0:n0 + NCH] = jnp.square(jnp.maximum(acc, 0.0)).astype(BF16)

    return _rows_call(name, body, tm, [h_mid], [(g.reshape(1, d), None), (w, layer)],
                      [(d, BF16), (n, BF16), (n, BF16)], rider=rider)


def _down_proj(name, act, h_mid, w, layer, tm, rider=None):
    d = h_mid.shape[1]

    def body(a_ref, h_ref, w_ref, o_ref):
        for n0 in range(0, d, NCH):
            o_ref[:, n0:n0 + NCH] = h_ref[:, n0:n0 + NCH] + _nn(a_ref[...], w_ref[:, n0:n0 + NCH])

    return _rows_call(name, body, tm, [act, h_mid], [(w, layer)], [(d, F32)], rider=rider)


def _down_proj_dx(name, dh, m_pre, w, layer, tm):
    n = w.shape[1]

    def body(dh_ref, m_ref, w_ref, dm_ref):
        dhb = dh_ref[...].astype(BF16)
        for n0 in range(0, n, NCH):
            dm_ref[:, n0:n0 + NCH] = (_nt(dhb, w_ref[n0:n0 + NCH, :])
                                      * (2.0 * jnp.maximum(m_ref[:, n0:n0 + NCH].astype(F32), 0.0))).astype(BF16)

    return _rows_call(name, body, tm, [dh, m_pre], [(w, layer)], [(n, BF16)])[0]


def _up_proj_dx(name, dm, h_mid, dh, g, w_up, w_out, layer, tm, rider=None):
    d = h_mid.shape[1]

    def body(dm_ref, h_ref, dh_ref, g_ref, wup_ref, wout_ref, dhm_ref, dy_ref, dg_ref):
        @pl.when(pl.program_id(0) == 0)
        def _():
            dg_ref[...] = jnp.zeros_like(dg_ref)
        dyn = _nt(dm_ref[...], wup_ref[...])
        r, xh = _norm_parts(h_ref[...])
        dhm = dh_ref[...] + _norm_bwd(r, xh, dyn, g_ref[...])
        dhm_ref[...] = dhm
        dg_ref[...] += _fold8(dyn * xh)
        dy_ref[...] = _nt(dhm.astype(BF16), wout_ref[...])

    return _rows_call(name, body, tm, [dm, h_mid, dh], [(g.reshape(1, d), None), (w_up, layer), (w_out, layer)],
                      [(d, F32), (w_out.shape[1], F32)], [(8, d)], rider)


def _in_proj_dx(name, dus, h, dh_mid, g, w, layer, tm, rider=None):
    d = h.shape[1]
    ns = [du.shape[1] for du in dus]
    nd = len(dus)

    def body(*refs):
        du_refs = refs[:nd]
        h_ref, dhm_ref, g_ref, w_ref, dh_ref, dg_ref = refs[nd:]

        @pl.when(pl.program_id(0) == 0)
        def _():
            dg_ref[...] = jnp.zeros_like(dg_ref)
        dyn, off = None, 0
        for du_ref, n in zip(du_refs, ns):
            part = _nt(du_ref[...], w_ref[:, off:off + n])
            dyn = part if dyn is None else dyn + part
            off += n
        r, xh = _norm_parts(h_ref[...])
        dh_ref[...] = dhm_ref[...] + _norm_bwd(r, xh, dyn, g_ref[...])
        dg_ref[...] += _fold8(dyn * xh)

    return _rows_call(name, body, tm, [*dus, h, dh_mid], [(g.reshape(1, d), None), (w, layer)], [(d, F32)],
                      [(8, d)], rider)


def _mm_tn(name, a, b, tt, tka, tn, into, shape, layer, row_off, col_off):
    t, ka = a.shape
    n = b.shape[1]
    assert t % tt == 0 and ka % tka == 0 and n % tn == 0 and row_off % tka == 0 and col_off % tn == 0

    def body(a_ref, b_ref, *rest):
        o_ref = rest[-1]

        @pl.when(pl.program_id(2) == 0)
        def _():
            o_ref[...] = jnp.zeros_like(o_ref)
        o_ref[...] += _tn(a_ref[...].astype(BF16), b_ref[...].astype(BF16))

    in_specs = [pl.BlockSpec((tt, tka), lambda i, j, s: (s, i)), pl.BlockSpec((tt, tn), lambda i, j, s: (s, j))]
    args = [a, b]
    if into is not None:
        in_specs.append(pl.BlockSpec(memory_space=pl.ANY))
        args.append(into)
    return pl.pallas_call(
        body, name=name, grid=(ka // tka, n // tn, t // tt), in_specs=in_specs,
        out_specs=pl.BlockSpec((None, tka, tn), lambda i, j, s: (layer, row_off // tka + i, col_off // tn + j)),
        out_shape=jax.ShapeDtypeStruct(shape, F32),
        input_output_aliases={} if into is None else {2: 0},
        compiler_params=_params(("parallel", "parallel", "arbitrary")),
    )(*args)


def _mm_tn_slab(name, a_list, b_list, tt, into, shape, layer):
    t = a_list[0].shape[0]
    kas, ns = [a.shape[1] for a in a_list], [b.shape[1] for b in b_list]
    assert t % tt == 0 and (sum(kas), sum(ns)) == tuple(shape[1:])
    na, nb = len(a_list), len(b_list)

    def body(*refs):
        o_ref = refs[-1]

        @pl.when(pl.program_id(0) == 0)
        def _():
            o_ref[...] = jnp.zeros_like(o_ref)
        r0 = 0
        for a_ref, ka in zip(refs[:na], kas):
            a = a_ref[...].astype(BF16)
            c0 = 0
            for b_ref, n in zip(refs[na:na + nb], ns):
                o_ref[r0:r0 + ka, c0:c0 + n] += _tn(a, b_ref[...].astype(BF16))
                c0 += n
            r0 += ka

    in_specs = [pl.BlockSpec((tt, c), lambda s: (s, 0)) for c in kas + ns]
    args = [*a_list, *b_list]
    if into is not None:
        in_specs.append(pl.BlockSpec(memory_space=pl.ANY))
        args.append(into)
    return pl.pallas_call(
        body, name=name, grid=(t // tt,), in_specs=in_specs,
        out_specs=pl.BlockSpec((None, *shape[1:]), lambda s: (layer, 0, 0)),
        out_shape=jax.ShapeDtypeStruct(shape, F32),
        input_output_aliases={} if into is None else {na + nb: 0},
        compiler_params=_params(("arbitrary",)),
    )(*args)


def _build_h(name, x, meta, lp, rider=None):
    bl, s, d = x.shape
    nq = lp // QB
    ride = _Ride(rider, 2, 1, bl * nq)

    def body(*refs):
        x_ref, m_ref, o_ref = ride.own(refs)
        j = pl.program_id(1)
        step = pl.program_id(0) * nq + j
        ride.before(refs, step)
        head = jnp.concatenate([jnp.zeros((PAD, d), F32), m_ref[...]], axis=0)
        o_ref[...] = jnp.where(j == 0, head, x_ref[...])
        ride.after(refs, step)

    return ride.call(
        body, name, (bl, nq),
        [pl.BlockSpec((None, QB, d), lambda b, j: (b, jnp.maximum(j - 1, 0), 0)),
         pl.BlockSpec(meta.shape, lambda b, j: (0, 0))], [x, meta],
        [pl.BlockSpec((QB, d), lambda b, j: (b * nq + j, 0))], [jax.ShapeDtypeStruct((bl * lp, d), F32)], [],
        ("parallel", "arbitrary"))


def _loss_bwd(name, h, g, target, lp):
    t, d = h.shape
    bl = target.shape[0]
    nq = lp // QB

    def body(h_ref, g_ref, t_ref, dh_ref, ls_ref, dg_ref):
        b, j = pl.program_id(0), pl.program_id(1)

        @pl.when((b == 0) & (j == 0))
        def _():
            ls_ref[...] = jnp.zeros_like(ls_ref)
            dg_ref[...] = jnp.zeros_like(dg_ref)
        xv = h_ref[...]
        r = lax.rsqrt(jnp.mean(xv * xv, axis=-1, keepdims=True) + EPS)
        xh = xv * r
        gv = g_ref[...]
        err = jnp.where(j >= 1, xh * gv - t_ref[...], 0.0)
        ls_ref[...] += _fold8(err * err) * (0.5 / d)
        dy = err * (1.0 / d)
        w = dy * gv
        dh_ref[...] = r * (w - xh * jnp.mean(w * xh, axis=-1, keepdims=True))
        dg_ref[...] += _fold8(dy * xh)

    return pl.pallas_call(
        body, name=name, grid=(bl, nq),
        in_specs=[pl.BlockSpec((QB, d), lambda b, j: (b * nq + j, 0)), pl.BlockSpec((1, d), lambda b, j: (0, 0)),
                  pl.BlockSpec((None, QB, d), lambda b, j: (b, jnp.maximum(j - 1, 0), 0))],
        out_specs=[pl.BlockSpec((QB, d), lambda b, j: (b * nq + j, 0)), pl.BlockSpec((8, d), lambda b, j: (0, 0)),
                   pl.BlockSpec((8, d), lambda b, j: (0, 0))],
        out_shape=[jax.ShapeDtypeStruct((t, d), F32), jax.ShapeDtypeStruct((8, d), F32),
                   jax.ShapeDtypeStruct((8, d), F32)],
        compiler_params=_params(("arbitrary", "arbitrary")),
    )(h, g.reshape(1, d), target)


def _pool_select(grp, a2, a4, a8, a16):
    return jnp.where(grp == 0, a2, jnp.where(grp == 1, a4, jnp.where(grp == 2, a8, a16)))


def _trailing_sums(v):
    s2 = v + pltpu.roll(v, 1, 0)
    s4 = s2 + pltpu.roll(s2, 2, 0)
    s8 = s4 + pltpu.roll(s4, 4, 0)
    s16 = s8 + pltpu.roll(s8, 8, 0)
    return s2, s4, s8, s16


def _leading_sums(v):
    n = v.shape[0]
    s2 = v + pltpu.roll(v, n - 1, 0)
    s4 = s2 + pltpu.roll(s2, n - 2, 0)
    s8 = s4 + pltpu.roll(s4, n - 4, 0)
    s16 = s8 + pltpu.roll(s8, n - 8, 0)
    return s2, s4, s8, s16


def _convpool_fwd(name, u_cp, wconv, wbd, pscale, lp, r):
    t = u_cp.shape[0]
    cw = u_cp.shape[1] // 4
    tps, hb = lp // r, r // HALO

    def body(cb_ref, cc_ref, cx_ref, pi_ref, cch_ref, cxh_ref, pih_ref, wc_ref, wbd_ref, ps_ref, y_ref):
        i = pl.program_id(0)
        lrow = (i % tps) * r + lax.broadcasted_iota(jnp.int32, (r, 1), 0)
        valid = lrow >= PAD
        xx = jnp.concatenate([cch_ref[...] * cxh_ref[...], cc_ref[...] * cx_ref[...]], axis=0)
        conv = (wc_ref[0:1, :] * pltpu.roll(xx, 2, 0) + wc_ref[1:2, :] * pltpu.roll(xx, 1, 0)
                + wc_ref[2:3, :] * xx)
        y_ref[:, 0:cw] = (cb_ref[...] * conv[HALO:]).astype(y_ref.dtype)
        p = pi_ref[...]
        grp = lax.broadcasted_iota(jnp.int32, (1, cw), 1) // (cw // 4)
        sel = _pool_select(grp, *_trailing_sums(jnp.concatenate([pih_ref[...], p], axis=0)))[HALO:]
        cnt = jnp.maximum(jnp.minimum((lrow - (PAD - 1)).astype(F32), _pool_select(grp, *POOL_WINDOWS)), 1.0)
        pooled = jnp.where(valid, sel / cnt - p, 0.0)
        y_ref[:, cw:2 * cw] = (_nn(pooled.astype(BF16), wbd_ref[...]) * ps_ref[...]).astype(y_ref.dtype)

    def main(col):
        return pl.BlockSpec((r, cw), lambda i: (i, col))

    def prev(col):
        return pl.BlockSpec((HALO, cw), lambda i: (jnp.maximum(i * hb - 1, 0), col))

    def whole(a):
        return pl.BlockSpec(a.shape, lambda i: (0, 0))

    return pl.pallas_call(
        body, name=name, grid=(t // r,),
        in_specs=[main(0), main(1), main(2), main(3), prev(1), prev(2), prev(3), whole(wconv), whole(wbd),
                  whole(pscale)],
        out_specs=pl.BlockSpec((r, 2 * cw), lambda i: (i, 0)),
        out_shape=jax.ShapeDtypeStruct((t, 2 * cw), BF16),
        compiler_params=_params(("parallel",)),
    )(u_cp, u_cp, u_cp, u_cp, u_cp, u_cp, u_cp, wconv, wbd, pscale)


def _convpool_bwd(name, u_cp, dy, wconv, wbd, pscale, lp, r):
    t = u_cp.shape[0]
    cw = u_cp.shape[1] // 4
    tps, hb = lp // r, r // HALO
    e = r + HALO

    def body(cb_ref, cc_ref, cx_ref, pi_ref, cbn_ref, cch_ref, cxh_ref, pih_ref, dyc_ref, dyp_ref, dycn_ref,
             dypn_ref, wc_ref, wbd_ref, ps_ref, du_ref, sm_ref, dwbd_ref):
        i = pl.program_id(0)

        @pl.when(i == 0)
        def _():
            sm_ref[...] = jnp.zeros_like(sm_ref)
            dwbd_ref[...] = jnp.zeros_like(dwbd_ref)
        lrow_e = (i % tps) * r + lax.broadcasted_iota(jnp.int32, (e, 1), 0)
        valid_e = (lrow_e >= PAD) & (lrow_e < lp)
        lrow, valid = lrow_e[:r], lrow_e[:r] >= PAD
        w0, w1, w2 = wc_ref[0:1, :], wc_ref[1:2, :], wc_ref[2:3, :]
        cb, cc, cx = cb_ref[...], cc_ref[...], cx_ref[...]
        prod = cc * cx
        xx = jnp.concatenate([cch_ref[...] * cxh_ref[...], prod], axis=0)
        back1, back2 = pltpu.roll(xx, 1, 0)[HALO:], pltpu.roll(xx, 2, 0)[HALO:]
        dyc = dyc_ref[...]
        du_ref[:, 0:cw] = (dyc * (w0 * back2 + w1 * back1 + w2 * prod)).astype(du_ref.dtype)
        dconv_e = jnp.where(valid_e, jnp.concatenate([dyc * cb, dycn_ref[...] * cbn_ref[...]], axis=0), 0.0)
        dconv = dconv_e[:r]
        dprod = (w2 * dconv + w1 * pltpu.roll(dconv_e, e - 1, 0)[:r] + w0 * pltpu.roll(dconv_e, e - 2, 0)[:r])
        du_ref[:, cw:2 * cw] = (dprod * cx).astype(du_ref.dtype)
        du_ref[:, 2 * cw:3 * cw] = (dprod * cc).astype(du_ref.dtype)
        sm_ref[0:8, :] += _fold8(dconv * back2)
        sm_ref[8:16, :] += _fold8(dconv * back1)
        sm_ref[16:24, :] += _fold8(dconv * prod)
        p = pi_ref[...]
        grp = lax.broadcasted_iota(jnp.int32, (1, cw), 1) // (cw // 4)
        win = _pool_select(grp, *POOL_WINDOWS)
        sel = _pool_select(grp, *_trailing_sums(jnp.concatenate([pih_ref[...], p], axis=0)))[HALO:]
        cnt_e = jnp.maximum(jnp.minimum((lrow_e - (PAD - 1)).astype(F32), win), 1.0)
        pooled = jnp.where(valid, sel / cnt_e[:r] - p, 0.0).astype(BF16)
        dyp = dyp_ref[...]
        sm_ref[24:32, :] += _fold8(dyp * _nn(pooled, wbd_ref[...]))
        dpre_e = (jnp.concatenate([dyp, dypn_ref[...]], axis=0) * ps_ref[...]).astype(BF16)
        dwbd_ref[...] += _tn(pooled, dpre_e[:r])
        dpooled_e = jnp.where(valid_e, _nt(dpre_e, wbd_ref[...]), 0.0)
        ahead = _pool_select(grp, *_leading_sums(dpooled_e / cnt_e))[:r]
        du_ref[:, 3 * cw:4 * cw] = (ahead - dpooled_e[:r]).astype(du_ref.dtype)

    last_halo = t // HALO - 1

    def main(col):
        return pl.BlockSpec((r, cw), lambda i: (i, col))

    def prev(col):
        return pl.BlockSpec((HALO, cw), lambda i: (jnp.maximum(i * hb - 1, 0), col))

    def nxt(col):
        return pl.BlockSpec((HALO, cw), lambda i: (jnp.minimum((i + 1) * hb, last_halo), col))

    def whole(a):
        return pl.BlockSpec(a.shape, lambda i: (0, 0))

    return pl.pallas_call(
        body, name=name, grid=(t // r,),
        in_specs=[main(0), main(1), main(2), main(3), nxt(0), prev(1), prev(2), prev(3), main(0), main(1), nxt(0),
                  nxt(1), whole(wconv), whole(wbd), whole(pscale)],
        out_specs=[pl.BlockSpec((r, 4 * cw), lambda i: (i, 0)), pl.BlockSpec((32, cw), lambda i: (0, 0)),
                   pl.BlockSpec((cw, cw), lambda i: (0, 0))],
        out_shape=[jax.ShapeDtypeStruct((t, 4 * cw), BF16), jax.ShapeDtypeStruct((32, cw), F32),
                   jax.ShapeDtypeStruct((cw, cw), F32)],
        compiler_params=_params(("arbitrary",)),
    )(u_cp, u_cp, u_cp, u_cp, u_cp, u_cp, u_cp, u_cp, dy, dy, dy, dy, wconv, wbd, pscale)


KW = 2 * QB
HP = 4
DECAY = 64.0


def _cumsum_matrix(before, kw):
    r = lax.broadcasted_iota(jnp.int32, (kw, kw), 0)
    c = lax.broadcasted_iota(jnp.int32, (kw, kw), 1)
    return ((r < c) if before else (r > c)).astype(BF16)


def _running(v, mat):
    m = v.shape[0]
    hi = v.astype(BF16)
    ext = _nn(jnp.concatenate([hi, (v - hi.astype(F32)).astype(BF16)], axis=0), mat)
    return ext[:m] + ext[m:]


def _log_sigmoid(z):
    neg_abs = lax.bitcast_convert_type(lax.bitcast_convert_type(z, jnp.int32) | jnp.int32(-2 ** 31), F32)
    return jnp.minimum(z, 0.0) - jnp.log(1.0 + jnp.exp(neg_abs))


def _stack_heads(v, head0):
    zero = jnp.zeros_like(v)
    return jnp.concatenate([jnp.where(head0, v, zero), jnp.where(head0, zero, v)], axis=0)


def _lanes(hp):
    return slice(hp * QB, (hp + 1) * QB)


def _key_walk(qi):
    prev = jnp.maximum(qi - 1, 0)
    return prev, (prev % 2) * QB, prev // 2, 1 - prev % 2


def _attn_fwd(name, qkv, bl, lp, rider=None):
    t = qkv.shape[0]
    nq, nblk = lp // QB, qkv.shape[1] // (3 * HP * QB)
    assert nblk == 1
    ride = _Ride(rider, 3, 3, bl * nblk * nq)

    def body(*refs):
        q_ref, k_ref, v_ref, o_ref, lt_ref, g0_ref, run_s, acc_s = ride.own(refs)
        qi = pl.program_id(2)
        step = (pl.program_id(0) * nblk + pl.program_id(1)) * nq + qi
        ride.before(refs, step)
        head0 = lax.broadcasted_iota(jnp.int32, (QB, QB), 1) < QB // 2
        q2 = [_stack_heads(q_ref[:, _lanes(hp)] * jnp.asarray(HEAD_SCALE, BF16), head0) for hp in range(HP)]
        later = {KW: _cumsum_matrix(False, KW), QB: _cumsum_matrix(False, QB)}
        q_pos = qi * QB + (lax.broadcasted_iota(jnp.int32, (2 * QB, KW), 0) & (QB - 1))
        col = lax.broadcasted_iota(jnp.int32, (2 * QB, KW), 1)
        prev, off, ngrp, lo_g = _key_walk(qi)

        def group(start, kw, masked):
            start = pl.multiple_of(start, QB)
            if masked:
                k_pos = start + col[:, :kw]
                valid = (k_pos < q_pos[:, :kw]) & (k_pos >= PAD)
            z = [_nt(q2[hp], k_ref[pl.ds(start, kw), _lanes(hp)]) for hp in range(HP)]
            logp, after, rs = [], [], []
            for hp in range(HP):
                lp_ = _log_sigmoid(z[hp])
                lk = lp_ - z[hp]
                if masked:
                    lk = jnp.where(valid, lk, 0.0)
                logp.append(lp_)
                rs.append(jnp.sum(lk, axis=1, keepdims=True))
                after.append(_running(lk, later[kw]))
            for hp in range(HP):
                run = run_s[hp]
                a = jnp.exp(logp[hp] + after[hp] + run)
                if masked:
                    a = jnp.where(valid, a, 0.0)
                run_s[hp] = run + rs[hp]
                acc_s[hp] += _nn(a.astype(BF16), v_ref[pl.ds(start, kw), _lanes(hp)])

        def alive():
            most = run_s[0]
            for hp in range(1, HP):
                most = jnp.maximum(most, run_s[hp])
            return jnp.max(most) > -DECAY

        def older(st):
            group(off + st[0] * KW, KW, False)
            return st[0] - 1, alive()

        run_s[...] = jnp.zeros_like(run_s)
        acc_s[...] = jnp.zeros_like(acc_s)
        group(prev * QB, KW, True)
        g, live = lax.while_loop(lambda st: (st[0] >= lo_g) & st[1], older, (ngrp - 1, alive()))
        bottom = (g < lo_g) & live & ((off > 0) | (ngrp >= 1))
        pl.when(bottom & (off > 0))(lambda: group(0, QB, True))
        pl.when(bottom & (off == 0))(lambda: group(0, KW, True))
        g0_ref[pl.program_id(0), qi] = jnp.where(bottom, -1, g + 1).astype(F32)
        for hp in range(HP):
            o_ref[:, _lanes(hp)] = jnp.where(head0, acc_s[hp, :QB], acc_s[hp, QB:]).astype(o_ref.dtype)
            lt_ref[:, _lanes(hp)] = jnp.where(head0, run_s[hp, :QB], run_s[hp, QB:])
        ride.after(refs, step)

    wb = HP * QB
    blk = pl.BlockSpec((QB, wb), lambda b, p, i: (b * nq + i, p))
    return ride.call(
        body, name, (bl, nblk, nq),
        [blk, pl.BlockSpec((lp, wb), lambda b, p, i: (b, nblk + p)),
         pl.BlockSpec((lp, wb), lambda b, p, i: (b, 2 * nblk + p))], [qkv, qkv, qkv],
        [blk, blk, pl.BlockSpec(memory_space=pltpu.SMEM)],
        [jax.ShapeDtypeStruct((t, nblk * wb), BF16), jax.ShapeDtypeStruct((t, nblk * wb), F32),
         jax.ShapeDtypeStruct((bl, nq), F32)],
        [pltpu.VMEM((HP, 2 * QB, 1), F32), pltpu.VMEM((HP, 2 * QB, QB), F32)])


def _attn_bwd(name, qkv, lt, g0, dy, bl, lp, rider=None):
    t = qkv.shape[0]
    nq, nblk = lp // QB, qkv.shape[1] // (3 * HP * QB)
    ride = _Ride(rider, 6, 3, bl * nblk * nq)

    def body(*refs):
        (q_ref, k_ref, v_ref, lt_ref, do_ref, g0_ref, dq_ref, dk_ref, dv_ref, dk_acc, dv_acc, seen_s, gsum_s,
         dq_s) = ride.own(refs)
        qi = pl.program_id(2)
        step = (pl.program_id(0) * nblk + pl.program_id(1)) * nq + qi
        ride.before(refs, step)

        @pl.when(qi == 0)
        def _():
            dk_acc[...] = jnp.zeros_like(dk_acc)
            dv_acc[...] = jnp.zeros_like(dv_acc)
        lane = lax.broadcasted_iota(jnp.int32, (QB, QB), 1)
        head0 = lane < QB // 2
        q2, do2, total = [], [], []
        for hp in range(HP):
            q2.append(_stack_heads(q_ref[:, _lanes(hp)] * jnp.asarray(HEAD_SCALE, BF16), head0))
            do2.append(_stack_heads(do_ref[:, _lanes(hp)].astype(BF16), head0))
            ltv = lt_ref[:, _lanes(hp)]
            total.append(jnp.concatenate(
                [jnp.sum(jnp.where(lane == 0, ltv, 0.0), axis=1, keepdims=True),
                 jnp.sum(jnp.where(lane == QB // 2, ltv, 0.0), axis=1, keepdims=True)], axis=0))
        later = {KW: _cumsum_matrix(False, KW), QB: _cumsum_matrix(False, QB)}
        earlier = {KW: _cumsum_matrix(True, KW), QB: _cumsum_matrix(True, QB)}
        q_pos = qi * QB + (lax.broadcasted_iota(jnp.int32, (2 * QB, KW), 0) & (QB - 1))
        col = lax.broadcasted_iota(jnp.int32, (2 * QB, KW), 1)
        prev, off, ngrp, lo_g = _key_walk(qi)

        def group(start, kw, masked):
            start = pl.multiple_of(start, QB)
            if masked:
                k_pos = start + col[:, :kw]
                valid = (k_pos < q_pos[:, :kw]) & (k_pos >= PAD)
            hps = range(HP)
            kg = [k_ref[pl.ds(start, kw), _lanes(hp)] for hp in hps]
            z = [_nt(q2[hp], kg[hp]) for hp in hps]
            da = [_nt(do2[hp], v_ref[pl.ds(start, kw), _lanes(hp)]) for hp in hps]
            logp, sig, after, rs = [], [], [], []
            for hp in hps:
                lp_ = _log_sigmoid(z[hp])
                lk = lp_ - z[hp]
                if masked:
                    lk = jnp.where(valid, lk, 0.0)
                logp.append(lp_)
                sig.append(jnp.exp(lp_))
                rs.append(jnp.sum(lk, axis=1, keepdims=True))
                after.append(_running(lk, later[kw]))
            a, gg, before = [], [], []
            for hp in hps:
                a_ = jnp.exp(logp[hp] + after[hp] + (total[hp] - seen_s[hp] - rs[hp]))
                if masked:
                    a_ = jnp.where(valid, a_, 0.0)
                a.append(a_.astype(BF16))
                gg.append(a_ * da[hp])
                before.append(_nn(gg[hp].astype(BF16), earlier[kw]))
            for hp in hps:
                gsum = gsum_s[hp]
                dz = gg[hp] - (gg[hp] + before[hp] + gsum) * sig[hp]
                if masked:
                    dz = jnp.where(valid, dz, 0.0)
                dz = dz.astype(BF16)
                dk_acc[pl.ds(start, kw), _lanes(hp)] += _tn(dz, q2[hp])
                dv_acc[pl.ds(start, kw), _lanes(hp)] += _tn(a[hp], do2[hp])
                seen_s[hp] += rs[hp]
                gsum_s[hp] = gsum + jnp.sum(gg[hp], axis=1, keepdims=True)
                dq_s[hp] += _nn(dz, kg[hp])

        def inner(g, _):
            group(off + g * KW, KW, False)
            return 0

        seen_s[...] = jnp.zeros_like(seen_s)
        gsum_s[...] = jnp.zeros_like(gsum_s)
        dq_s[...] = jnp.zeros_like(dq_s)
        first = g0_ref[pl.program_id(0), qi].astype(jnp.int32)
        pl.when((first < 0) & (off > 0))(lambda: group(0, QB, True))
        pl.when((first < 0) & (off == 0))(lambda: group(0, KW, True))
        lax.fori_loop(jnp.where(first < 0, lo_g, first), ngrp, inner, 0)
        group(prev * QB, KW, True)
        for hp in range(HP):
            dq_ref[:, _lanes(hp)] = (jnp.where(head0, dq_s[hp, :QB], dq_s[hp, QB:])
                                     * HEAD_SCALE).astype(dq_ref.dtype)

        @pl.when(qi == nq - 1)
        def _():
            dk_ref[...] = dk_acc[...].astype(dk_ref.dtype)
            dv_ref[...] = dv_acc[...].astype(dv_ref.dtype)
        ride.after(refs, step)

    wb = HP * QB
    blk = pl.BlockSpec((QB, wb), lambda b, p, i: (b * nq + i, p))
    seq = pl.BlockSpec((lp, wb), lambda b, p, i: (b, p))
    out = jax.ShapeDtypeStruct((t, nblk * wb), BF16)
    return ride.call(
        body, name, (bl, nblk, nq),
        [blk, pl.BlockSpec((lp, wb), lambda b, p, i: (b, nblk + p)),
         pl.BlockSpec((lp, wb), lambda b, p, i: (b, 2 * nblk + p)), blk,
         pl.BlockSpec((QB, wb), lambda b, p, i: (b * nq + i, nblk + p)), pl.BlockSpec(memory_space=pltpu.SMEM)],
        [qkv, qkv, qkv, lt, dy, g0],
        [blk, seq, seq], [out, out, out],
        [pltpu.VMEM((lp, wb), F32), pltpu.VMEM((lp, wb), F32), pltpu.VMEM((HP, 2 * QB, 1), F32),
         pltpu.VMEM((HP, 2 * QB, 1), F32), pltpu.VMEM((HP, 2 * QB, QB), F32)])


def _place():
    return lax.axis_index("x"), lax.axis_index("y"), lax.axis_index("c")


def _peers(chip):
    kx, ky = chip // 2, chip % 2
    return ((1 - kx, ky), (kx, 1 - ky), (1 - kx, 1 - ky))


def _hbm_specs(n):
    return [pl.BlockSpec(memory_space=pl.ANY) for _ in range(n)]


def _remote(src, dst, send, recv, k, to):
    return pltpu.make_async_remote_copy(src, dst, send.at[k], recv.at[k], device_id=to, device_id_type=MESH)


class _Rider:
    def __init__(self, ins, out_shapes, aliases, nsem, first, mid=None, last=None):
        self.ins, self.out_shapes, self.aliases, self.nsem = list(ins), list(out_shapes), dict(aliases), nsem
        self.first, self.mid, self.last = first, mid, last


def _by_chip(fn):
    def run(ins, outs, send, recv):
        x, y, c = _place()
        for me in range(4):
            pl.when(2 * x + y == me)(functools.partial(fn, ins, outs, send, recv, me, c, (x, y, 1 - c)))
    return run


def _run_rider(name, rider):
    ni, no = len(rider.ins), len(rider.out_shapes)

    def body(*refs):
        args = (refs[:ni], refs[ni:ni + no], refs[ni + no], refs[ni + no + 1])
        for hook in (rider.first, rider.mid, rider.last):
            if hook is not None:
                hook(*args)

    return pl.pallas_call(
        body, name=name, in_specs=_hbm_specs(ni), out_specs=_hbm_specs(no), out_shape=rider.out_shapes,
        input_output_aliases=rider.aliases,
        scratch_shapes=[pltpu.SemaphoreType.DMA((rider.nsem,)), pltpu.SemaphoreType.DMA((rider.nsem,))],
        compiler_params=pltpu.CompilerParams(has_side_effects=True),
    )(*rider.ins)


class _Ride:
    def __init__(self, rider, n_in, n_out, steps):
        self.rider, self.n_in, self.n_out, self.steps = rider, n_in, n_out, steps
        self.ri = len(rider.ins) if rider else 0
        self.ro = len(rider.out_shapes) if rider else 0

    def own(self, refs):
        refs = list(refs)
        a, b = self.n_in, self.n_in + self.ri + self.n_out
        tail = refs[b + self.ro:len(refs) - 2] if self.rider else refs[b + self.ro:]
        return refs[:a] + refs[a + self.ri:b] + tail

    def _args(self, refs):
        a, b = self.n_in, self.n_in + self.ri + self.n_out
        return refs[a:a + self.ri], refs[b:b + self.ro], refs[-2], refs[-1]

    def before(self, refs, step):
        if self.rider is None:
            return
        pl.when(step == 0)(functools.partial(self.rider.first, *self._args(refs)))
        if self.rider.mid is not None:
            pl.when(step == (3 * self.steps) // 4)(functools.partial(self.rider.mid, *self._args(refs)))

    def after(self, refs, step):
        if self.rider is not None and self.rider.last is not None:
            pl.when(step == self.steps - 1)(functools.partial(self.rider.last, *self._args(refs)))

    def call(self, body, name, grid, in_specs, args, out_specs, out_shape, scratch,
             sem=("parallel", "parallel", "arbitrary")):
        r = self.rider
        if r is None:
            return pl.pallas_call(body, name=name, grid=grid, in_specs=in_specs, out_specs=out_specs,
                                  out_shape=out_shape, scratch_shapes=scratch, compiler_params=_params(sem))(*args)
        return pl.pallas_call(
            body, name=name, grid=grid, in_specs=in_specs + _hbm_specs(self.ri),
            out_specs=out_specs + _hbm_specs(self.ro), out_shape=out_shape + r.out_shapes,
            input_output_aliases={self.n_in + i: self.n_out + o for i, o in r.aliases.items()},
            scratch_shapes=scratch + [pltpu.SemaphoreType.DMA((r.nsem,)), pltpu.SemaphoreType.DMA((r.nsem,))],
            compiler_params=pltpu.CompilerParams(dimension_semantics=("arbitrary",) * len(grid),
                                                 vmem_limit_bytes=VMEM_LIMIT, has_side_effects=True),
        )(*args, *r.ins)


def _core_view(a, axis):
    l, r, c = a.shape
    return a.reshape(l, 4, 2, r // 8, c) if axis == 0 else a.reshape(l, 2, r // 2, c)


def _shard_view(a):
    l, r, c = a.shape
    return a.reshape(l, 2, r // 2, c)


def _piece(ref, axis, layer, chip, core):
    if axis == 0:
        return ref.at[layer, chip, core]
    cs = ref.shape[-1] // 4
    return ref.at[layer, core, :, pl.ds(chip * cs, cs)]


def _place_shard(name, w, axis, kidx, tr):
    _, r, cdim = w.shape
    shp = [2, r, cdim]
    shp[1 + axis] *= 4
    nb = r // tr

    def body(k_ref, w_ref, o_ref):
        o_ref[...] = w_ref[...].astype(o_ref.dtype)

    if axis == 0:
        out_spec = pl.BlockSpec((None, tr, cdim), lambda l, i, k_ref: (l, k_ref[0] * nb + i, 0))
    else:
        out_spec = pl.BlockSpec((None, tr, cdim), lambda l, i, k_ref: (l, i, k_ref[0]))
    return pl.pallas_call(
        body, name=name,
        grid_spec=pltpu.PrefetchScalarGridSpec(
            num_scalar_prefetch=1, grid=(2, nb),
            in_specs=[pl.BlockSpec((None, tr, cdim), lambda l, i, k_ref: (l, i, 0))], out_specs=out_spec),
        out_shape=jax.ShapeDtypeStruct(tuple(shp), BF16),
        compiler_params=_params(("arbitrary", "arbitrary")),
    )(kidx, w)


def _gather_rider(views, axes, items):
    n = len(items)

    def first(ins, outs, send, recv, me, c, sib):
        for i, (w, l) in enumerate(items):
            for j, (px, py) in enumerate(_peers(me)):
                _remote(_piece(ins[w], axes[w], l, me, c), _piece(outs[w], axes[w], l, me, c), send, recv,
                        3 * i + j, (px, py, c)).start()

    def mid(ins, outs, send, recv, me, c, sib):
        for i, (w, l) in enumerate(items):
            for j, (px, py) in enumerate(_peers(me)):
                got = _piece(outs[w], axes[w], l, 2 * px + py, c)
                _remote(got, got, send, recv, 3 * i + j, (px, py, c)).wait_recv()
                _remote(got, got, send, recv, 3 * (n + i) + j, sib).start()

    def last(ins, outs, send, recv, me, c, sib):
        for i, (w, l) in enumerate(items):
            for j, (px, py) in enumerate(_peers(me)):
                mine, got = _piece(outs[w], axes[w], l, me, c), _piece(outs[w], axes[w], l, 2 * px + py, c)
                theirs = _piece(outs[w], axes[w], l, 2 * px + py, 1 - c)
                _remote(theirs, theirs, send, recv, 3 * (n + i) + j, sib).wait_recv()
                _remote(mine, mine, send, recv, 3 * i + j, (px, py, c)).wait_send()
                _remote(got, got, send, recv, 3 * (n + i) + j, sib).wait_send()

    return _Rider(views, [jax.ShapeDtypeStruct(v.shape, v.dtype) for v in views], {w: w for w in range(len(views))},
                  6 * n, _by_chip(first), _by_chip(mid), _by_chip(last))


def _swap_rider(views, axes, items):
    nv = len(views)

    def part(ref, w, l, core):
        return ref.at[l, :, core] if axes[w] == 0 else ref.at[l, core]

    def copies(ins, outs, send, recv):
        x, y, c = _place()
        return [_remote(part(ins[w], w, l, 1 - c), outs[nv + i], send, recv, i, (x, y, 1 - c))
                for i, (w, l) in enumerate(items)]

    def first(ins, outs, send, recv):
        for cp in copies(ins, outs, send, recv):
            cp.start()

    def last(ins, outs, send, recv):
        for cp in copies(ins, outs, send, recv):
            cp.wait()

    got = [jax.ShapeDtypeStruct(views[w].shape[1:2] + views[w].shape[3:] if axes[w] == 0 else views[w].shape[2:],
                                views[w].dtype) for w, _ in items]
    return _Rider(views, [jax.ShapeDtypeStruct(v.shape, v.dtype) for v in views] + got,
                  {w: w for w in range(nv)}, len(items), first, None, last)


def _add_core(name, view, got, axis, layer, cidx, tr):
    def body(c_ref, g_ref, r_ref, o_ref):
        o_ref[...] = (g_ref[...] + r_ref[...]).astype(o_ref.dtype)

    if axis == 0:
        _, nchip, _, pr, cdim = view.shape
        grid = (nchip, pr // tr)
        specs = [pl.BlockSpec((None, None, None, tr, cdim), lambda k, i, c_ref: (layer, k, c_ref[0], i, 0)),
                 pl.BlockSpec((None, tr, cdim), lambda k, i, c_ref: (k, i, 0))]
        out_spec = pl.BlockSpec((None, tr, cdim), lambda k, i, c_ref: (k, i, 0))
    else:
        _, _, pr, cdim = view.shape
        grid = (pr // tr,)
        specs = [pl.BlockSpec((None, None, tr, cdim), lambda i, c_ref: (layer, c_ref[0], i, 0)),
                 pl.BlockSpec((tr, cdim), lambda i, c_ref: (i, 0))]
        out_spec = pl.BlockSpec((tr, cdim), lambda i, c_ref: (i, 0))
    return pl.pallas_call(
        body, name=name,
        grid_spec=pltpu.PrefetchScalarGridSpec(num_scalar_prefetch=1, grid=grid, in_specs=specs,
                                               out_specs=out_spec),
        out_shape=jax.ShapeDtypeStruct(got.shape, BF16),
        compiler_params=_params(("arbitrary",) * len(grid)),
    )(cidx, view, got)


def _scatter_rider(sums, axes):
    def part(ref, i, chip):
        if axes[i] == 0:
            return ref.at[chip]
        cs = ref.shape[-1] // 4
        return ref.at[:, pl.ds(chip * cs, cs)]

    def copies(ins, outs, send, recv, me, c, sib):
        return [_remote(part(ins[i], i, 2 * px + py), outs[i].at[j], send, recv, 3 * i + j, (px, py, c))
                for i in range(len(sums)) for j, (px, py) in enumerate(_peers(me))]

    def first(*args):
        for cp in copies(*args):
            cp.start()

    def last(*args):
        for cp in copies(*args):
            cp.wait()

    shapes = [jax.ShapeDtypeStruct((3,) + (s.shape[1:] if ax == 0 else (s.shape[0], s.shape[1] // 4)), s.dtype)
              for s, ax in zip(sums, axes)]
    return _Rider(sums, shapes, {}, 3 * len(sums), _by_chip(first), None, _by_chip(last))


def _add_chips(name, own, got, axis, layer, kc_idx, tr, into, shard_shape):
    _, pr, pc = got.shape

    def body(k_ref, o_ref, g_ref, *rest):
        rest[-1][...] = (o_ref[...].astype(F32) + g_ref[0].astype(F32) + g_ref[1].astype(F32)
                         + g_ref[2].astype(F32))

    if axis == 0:
        own_spec = pl.BlockSpec((None, tr, pc), lambda i, k_ref: (k_ref[0], i, 0))
    else:
        own_spec = pl.BlockSpec((tr, pc), lambda i, k_ref: (i, k_ref[0]))
    specs = [own_spec, pl.BlockSpec((3, tr, pc), lambda i, k_ref: (0, i, 0))]
    args = [kc_idx, own, got]
    if into is not None:
        specs.append(pl.BlockSpec(memory_space=pl.ANY))
        args.append(into)
    return pl.pallas_call(
        body, name=name,
        grid_spec=pltpu.PrefetchScalarGridSpec(
            num_scalar_prefetch=1, grid=(pr // tr,), in_specs=specs,
            out_specs=pl.BlockSpec((None, None, tr, pc), lambda i, k_ref: (layer, k_ref[1], i, 0))),
        out_shape=jax.ShapeDtypeStruct(shard_shape, F32),
        input_output_aliases={} if into is None else {3: 0},
        compiler_params=_params(("arbitrary",)),
    )(*args)


def _join_rider(parts):
    def first(ins, outs, send, recv):
        x, y, c = _place()
        for w in range(len(parts)):
            _remote(ins[w].at[:, c], outs[w].at[:, c], send, recv, w, (x, y, 1 - c)).start()

    def last(ins, outs, send, recv):
        x, y, c = _place()
        for w in range(len(parts)):
            _remote(ins[w].at[:, c], outs[w].at[:, c], send, recv, w, (x, y, 1 - c)).wait_send()
            _remote(ins[w].at[:, c], outs[w].at[:, 1 - c], send, recv, w, (x, y, 1 - c)).wait_recv()

    return _Rider(parts, [jax.ShapeDtypeStruct(p.shape, p.dtype) for p in parts],
                  {w: w for w in range(len(parts))}, len(parts), first, None, last)


def _all_reduce_small(name, pack, lead, groups):
    nr, d = pack.shape
    nout = nr - (groups - 1) * lead

    def body(in_ref, sum_ref, mine, slots, send, recv):
        x, y, c = _place()
        me = 4 * x + 2 * y + c
        fold = in_ref[0:lead]
        for grp in range(1, groups):
            fold = fold + in_ref[grp * lead:(grp + 1) * lead]
        mine[0:lead] = fold
        mine[lead:] = in_ref[groups * lead:]
        slots[me] = mine[...]
        cps = []
        for r in range(1, 8):
            rx, ry, rc = r // 4, (r // 2) % 2, r % 2
            peer = (x + rx - 2 * x * rx, y + ry - 2 * y * ry, c + rc - 2 * c * rc)
            cp = pltpu.make_async_remote_copy(mine, slots.at[me], send.at[r - 1], recv.at[r - 1],
                                              device_id=peer, device_id_type=MESH)
            cp.start()
            cps.append(cp)
        for cp in cps:
            cp.wait()
        acc = slots[0]
        for dev in range(1, 8):
            acc = acc + slots[dev]
        sum_ref[...] = acc

    vmem = pl.BlockSpec(memory_space=pltpu.VMEM)
    return pl.pallas_call(
        body, name=name, in_specs=[vmem], out_specs=vmem, out_shape=jax.ShapeDtypeStruct((nout, d), F32),
        scratch_shapes=[pltpu.VMEM((nout, d), F32), pltpu.VMEM((8, nout, d), F32), pltpu.SemaphoreType.DMA((7,)),
                        pltpu.SemaphoreType.DMA((7,))],
        compiler_params=pltpu.CompilerParams(has_side_effects=True, vmem_limit_bytes=VMEM_LIMIT),
    )(pack)


def _adamw_math(w, g, m, v):
    m = B1 * m + (1.0 - B1) * g
    v = B2 * v + (1.0 - B2) * (g * g)
    m_hat = m / (1.0 - B1 ** STEP)
    v_hat = v / (1.0 - B2 ** STEP)
    return -LR * (m_hat / (jnp.sqrt(v_hat) + ADAM_EPS) + WD * w), m, v


def _adamw(name, w, g, m, v, tr):
    shape = w.shape
    flat = [a.reshape(-1, shape[-1]) for a in (w, g, m, v)]
    r, cdim = flat[0].shape

    def body(w_ref, g_ref, m_ref, v_ref, d_ref, nm_ref, nv_ref):
        d_ref[...], nm_ref[...], nv_ref[...] = _adamw_math(w_ref[...], g_ref[...], m_ref[...], v_ref[...])

    spec = pl.BlockSpec((tr, cdim), lambda i: (i, 0))
    outs = pl.pallas_call(
        body, name=name, grid=(r // tr,), in_specs=[spec] * 4, out_specs=[spec] * 3,
        out_shape=[jax.ShapeDtypeStruct((r, cdim), F32)] * 3,
        compiler_params=_params(("parallel",)),
    )(*flat)
    return [o.reshape(shape) for o in outs]


def _adamw_small(name, groups):
    n = len(groups)
    shapes = [grp[0].shape for grp in groups]
    flat = [a.reshape(-1, a.shape[-1]) for grp in groups for a in grp]

    def body(*refs):
        ins, outs = refs[:4 * n], refs[4 * n:]
        for i in range(n):
            w_ref, g_ref, m_ref, v_ref = ins[4 * i:4 * i + 4]
            outs[3 * i][...], outs[3 * i + 1][...], outs[3 * i + 2][...] = _adamw_math(
                w_ref[...], g_ref[...], m_ref[...], v_ref[...])

    vmem = pl.BlockSpec(memory_space=pltpu.VMEM)
    out_shape = [jax.ShapeDtypeStruct(flat[4 * i].shape, F32) for i in range(n) for _ in range(3)]
    outs = pl.pallas_call(body, name=name, in_specs=[vmem] * (4 * n), out_specs=[vmem] * (3 * n),
                          out_shape=out_shape)(*flat)
    return [[outs[3 * i + j].reshape(shapes[i]) for j in range(3)] for i in range(n)]


def _block_diag(w_grp):
    g, pg, _ = w_grp.shape
    eye = jnp.eye(g, dtype=w_grp.dtype)
    return (eye[:, None, :, None] * w_grp[:, :, None, :]).reshape(g * pg, g * pg)


def _diag_blocks(m, g):
    pg = m.shape[0] // g
    return jnp.stack([m[i * pg:(i + 1) * pg, i * pg:(i + 1) * pg] for i in range(g)])


BIG = ("w_in", "w_out", "w_up", "w_down")
AXES = (1, 0, 1, 0)
W_IN, W_OUT, W_UP, W_DOWN = range(4)


def kernel(x, meta_tokens, g_mix, w_in, w_conv, w_pool, pool_scale, w_out, g_mlp, w_up, w_down, g_final, loss_target, m_meta_tokens, m_g_mix, m_w_in, m_w_conv, m_w_pool, m_pool_scale, m_w_out, m_g_mlp, m_w_up, m_w_down, m_g_final, v_meta_tokens, v_g_mix, v_w_in, v_w_conv, v_w_pool, v_pool_scale, v_w_out, v_g_mlp, v_w_up, v_w_down, v_g_final):
    bl, s, d = x.shape
    depth = g_mix.shape[0]
    assert depth == 2
    lp = PAD + N_META + s
    tt = lp
    tm = lp // 4
    copy_rows, sum_rows, dw_tile = 256, 128, 1024
    cs = w_conv.shape[2]
    cw = 4 * cs
    ngrp = w_pool.shape[1]
    xi, yi, ci = _place()
    chip = (2 * xi + yi).astype(jnp.int32)
    cidx, kidx = ci.astype(jnp.int32).reshape(1), chip.reshape(1)
    kc_idx = jnp.stack([chip, ci.astype(jnp.int32)])
    shards = (w_in, w_out, w_up, w_down)

    views = [_core_view(_place_shard(f"place_{BIG[w]}", shards[w], AXES[w], kidx, copy_rows), AXES[w])
             for w in range(4)]

    def whole(w):
        return views[w].reshape(depth, -1, views[w].shape[-1])

    def gather_on(call, items):
        ws = sorted({w for w, _ in items})
        res = call(_gather_rider([views[w] for w in ws], [AXES[w] for w in ws],
                                 [(ws.index(w), layer) for w, layer in items]))
        for j, w in enumerate(ws):
            views[w] = res[len(res) - len(ws) + j]
        return res[:len(res) - len(ws)]

    placed = jnp.zeros((32, d), F32)
    placed = lax.dynamic_update_slice(placed, meta_tokens, (0, chip * meta_tokens.shape[1]))
    placed = lax.dynamic_update_slice(placed, w_conv.reshape(-1, cs), (N_META, chip * cs))
    placed = jnp.where(ci == 0, placed, 0.0)
    small = _all_reduce_small("gather_small", placed, 8, 1)
    meta_full = small[:N_META]
    conv_full = small[N_META:N_META + depth * 3, :cw].reshape(depth, 3, cw)

    (h,) = gather_on(lambda rider: _build_h("build_h", x, meta_full, lp, rider), [(W_IN, 0)])
    wbd = [_block_diag(w_pool[i]).astype(BF16) for i in range(depth)]
    saved = []
    for i in range(depth):
        hn, u_cp, qkv = _in_proj(f"in_proj{i}", h, g_mix[i], whole(W_IN), i, tm, 4 * cw)
        y_cp = _convpool_fwd(f"convpool{i}", u_cp, conv_full[i], wbd[i], pool_scale[i:i + 1], lp, tm)
        if i == 0:
            y_at, lt, g0 = gather_on(lambda rider: _attn_fwd(f"attn{i}", qkv, bl, lp, rider),
                                     [(W_OUT, 0), (W_UP, 0), (W_DOWN, 0)])
            h_mid = _out_proj(f"out_proj{i}", y_cp, y_at, h, whole(W_OUT), i, tm)
            w_up0 = whole(W_UP)
            hn2, m_pre, act = gather_on(lambda rider: _up_proj(f"up_proj{i}", h_mid, g_mlp[i], w_up0, i, tm, rider),
                                        [(W_OUT, 1), (W_DOWN, 1)])
            w_down0 = whole(W_DOWN)
            (h_next,) = gather_on(lambda rider: _down_proj(f"down_proj{i}", act, h_mid, w_down0, i, tm, rider),
                                  [(W_IN, 1), (W_UP, 1)])
        else:
            y_at, lt, g0 = _attn_fwd(f"attn{i}", qkv, bl, lp)
            h_mid = _out_proj(f"out_proj{i}", y_cp, y_at, h, whole(W_OUT), i, tm)
            hn2, m_pre, act = _up_proj(f"up_proj{i}", h_mid, g_mlp[i], whole(W_UP), i, tm)
            (h_next,) = _down_proj(f"down_proj{i}", act, h_mid, whole(W_DOWN), i, tm)
        saved.append((h, hn, u_cp, qkv, y_cp, y_at, (lt, g0), h_mid, hn2, m_pre, act))
        h = h_next

    dh, loss8, dgf8 = _loss_bwd("loss", h, g_final, loss_target, lp)
    per_layer = {k: [None] * depth for k in ("g_mix", "w_conv", "w_pool", "pool_scale", "g_mlp")}

    gw = [None] * 4
    sums, arrived = {}, {}

    def dw(name, a, b, w, layer):
        shape = whole(w).shape
        into = None if gw[w] is None else gw[w].reshape(shape)
        if isinstance(a, list):
            res = _mm_tn_slab(name, a, b, tt // 2, into, shape, layer)
        else:
            res = _mm_tn(name, a, b, tt, dw_tile, dw_tile, into, shape, layer, 0, 0)
        gw[w] = _core_view(res, AXES[w])

    def swap_rider(ws):
        return _swap_rider([gw[w] for w, _ in ws], [AXES[w] for w, _ in ws],
                           [(j, layer) for j, (_, layer) in enumerate(ws)])

    def swapped(ws, outs):
        for j, (w, layer) in enumerate(ws):
            gw[w] = outs[j]
            sums[w, layer] = _add_core(f"chip_sum_{BIG[w]}{layer}", gw[w], outs[len(ws) + j], AXES[w], layer, cidx,
                                       sum_rows)

    def scatter_rider(items):
        return _scatter_rider([sums[it] for it in items], [AXES[w] for w, _ in items])

    def bwd_mlp(i, dh, swap_early):
        _, _, _, _, y_cp, y_at, _, h_mid, hn2, m_pre, act = saved[i]
        dm = _down_proj_dx(f"down_proj_dx{i}", dh, m_pre, whole(W_DOWN), i, tm)
        dw(f"down_proj_dw{i}", act, dh, W_DOWN, i)
        dw(f"up_proj_dw{i}", hn2, dm, W_UP, i)
        ws = [(W_DOWN, i), (W_UP, i)] if swap_early else []
        dh_mid, dy, dg8, *outs = _up_proj_dx(f"up_proj_dx{i}", dm, h_mid, dh, g_mlp[i], whole(W_UP), whole(W_OUT), i,
                                             tm, swap_rider(ws) if ws else None)
        swapped(ws, outs)
        per_layer["g_mlp"][i] = dg8.sum(0)
        dw(f"out_proj_dw{i}", [y_cp, y_at], [dh_mid], W_OUT, i)
        return dh_mid, dy

    def bwd_mix(i, dh_mid, dy, dus3, scatter_late):
        h_in, hn, u_cp = saved[i][:3]
        du_cp, sm, dwbd = _convpool_bwd(f"convpool_bwd{i}", u_cp, dy, conv_full[i], wbd[i], pool_scale[i:i + 1], lp,
                                        tm)
        sm = sm.reshape(4, 8, cw).sum(1)
        per_layer["w_conv"][i] = sm[0:3]
        per_layer["pool_scale"][i] = sm[3]
        per_layer["w_pool"][i] = _diag_blocks(dwbd, ngrp)
        dus = [du_cp, *dus3]
        dw(f"in_proj_dw{i}", [hn], dus, W_IN, i)
        rider = None
        if scatter_late:
            swapped([(W_IN, i)], _run_rider(f"grads_swap_in{i}", swap_rider([(W_IN, i)])))
            rider = scatter_rider([(W_IN, i)])
        dh, dg8, *outs = _in_proj_dx(f"in_proj_dx{i}", dus, h_in, dh_mid, g_mix[i], whole(W_IN), i, tm, rider)
        arrived.update(zip([(W_IN, i)], outs))
        per_layer["g_mix"][i] = dg8.sum(0)
        return dh

    def attn_bwd(i, dy, rider):
        qkv, (lt, g0) = saved[i][3], saved[i][6]
        res = _attn_bwd(f"attn_bwd{i}", qkv, lt, g0, dy, bl, lp, rider)
        return res[:3], res[3:]

    dh_mid, dy = bwd_mlp(1, dh, False)
    ws = [(W_DOWN, 1), (W_UP, 1), (W_OUT, 1)]
    dus3, outs = attn_bwd(1, dy, swap_rider(ws))
    swapped(ws, outs)
    dh = bwd_mix(1, dh_mid, dy, dus3, False)

    dh_mid, dy = bwd_mlp(0, dh, True)
    ws = [(W_IN, 1), (W_OUT, 0)]
    swapped(ws, _run_rider("grads_swap0", swap_rider(ws)))
    items = list(sums)
    dus3, outs = attn_bwd(0, dy, scatter_rider(items))
    arrived.update(zip(items, outs))
    dh0 = bwd_mix(0, dh_mid, dy, dus3, True)

    finals = []
    for w in range(4):
        rs_, cs_ = shards[w].shape[1:]
        part = None
        for layer in reversed(range(depth)):
            part = _add_chips(f"reduce_{BIG[w]}{layer}", sums[w, layer], arrived[w, layer], AXES[w], layer, kc_idx,
                              sum_rows, part, (depth, 2, rs_ // 2, cs_))
        finals.append(part)
    finals = _run_rider("grads_join", _join_rider(finals))
    grad = {BIG[w]: finals[w].reshape(shards[w].shape) for w in range(4)}

    dh0 = dh0.reshape(bl, lp, d)
    grad_x = dh0[:, PAD + N_META:]
    local = {k: jnp.stack(v) for k, v in per_layer.items()}
    pieces = [dh0[:, PAD:PAD + N_META].reshape(bl * N_META, d), local["g_mix"], local["g_mlp"],
              dgf8.sum(0).reshape(1, d),
              jnp.pad(local["w_conv"].reshape(-1), (0, 2 * d - local["w_conv"].size)).reshape(2, d),
              jnp.pad(local["pool_scale"].reshape(-1), (0, d - local["pool_scale"].size)).reshape(1, d),
              jnp.pad(loss8.sum(0, keepdims=True), ((0, 7), (0, 0))), local["w_pool"].reshape(-1, d)]
    summed = _all_reduce_small("small_grads", jnp.concatenate(pieces, axis=0), N_META, bl)
    o = N_META
    grad.update({
        "meta_tokens": lax.dynamic_slice_in_dim(summed[:o], chip * meta_tokens.shape[1], meta_tokens.shape[1], 1),
        "g_mix": summed[o:o + 2], "g_mlp": summed[o + 2:o + 4], "g_final": summed[o + 4],
        "w_conv": lax.dynamic_slice_in_dim(summed[o + 5:o + 7].reshape(-1)[:2 * 3 * cw].reshape(2, 3, cw),
                                           chip * cs, cs, 2),
        "pool_scale": summed[o + 7].reshape(-1)[:pool_scale.size].reshape(pool_scale.shape),
        "w_pool": summed[o + 16:].reshape(w_pool.shape),
    })
    loss = jnp.sum(summed[o + 8])

    weights = dict(meta_tokens=meta_tokens, g_mix=g_mix, w_in=w_in, w_conv=w_conv, w_pool=w_pool,
                   pool_scale=pool_scale, w_out=w_out, g_mlp=g_mlp, w_up=w_up, w_down=w_down, g_final=g_final)
    ms = dict(meta_tokens=m_meta_tokens, g_mix=m_g_mix, w_in=m_w_in, w_conv=m_w_conv, w_pool=m_w_pool,
              pool_scale=m_pool_scale, w_out=m_w_out, g_mlp=m_g_mlp, w_up=m_w_up, w_down=m_w_down,
              g_final=m_g_final)
    vs = dict(meta_tokens=v_meta_tokens, g_mix=v_g_mix, w_in=v_w_in, w_conv=v_w_conv, w_pool=v_w_pool,
              pool_scale=v_pool_scale, w_out=v_w_out, g_mlp=v_g_mlp, w_up=v_w_up, w_down=v_w_down,
              g_final=v_g_final)
    order = list(weights)
    upd = {k: _adamw(f"adamw_{k}", weights[k], grad[k], ms[k], vs[k], copy_rows) for k in BIG}
    little = [k for k in order if k not in BIG]
    for k, res in zip(little, _adamw_small("adamw_small", [(weights[k], grad[k].reshape(weights[k].shape), ms[k],
                                                            vs[k]) for k in little])):
        upd[k] = res
    grad = {k: grad[k].reshape(weights[k].shape) for k in order}
    return (loss, grad_x, *[grad[k] for k in order], *[upd[k][0] for k in order], *[upd[k][1] for k in order],
            *[upd[k][2] for k in order])
```

```python
import functools

import jax
import jax.numpy as jnp
from jax import lax
from jax.experimental import pallas as pl
from jax.experimental.pallas import tpu as pltpu

F32, BF16 = jnp.float32, jnp.bfloat16
MESH = pl.DeviceIdType.MESH
EPS = 1e-6
N_META = 16
QB = 128
PAD = QB - N_META
HALO = 16
POOL_WINDOWS = (2.0, 4.0, 8.0, 16.0)
HEAD_SCALE = 0.125
LR, B1, B2, ADAM_EPS, WD, STEP = 0.001, 0.9, 0.999, 1e-08, 0.01, 10
VMEM_LIMIT = 56 * 1024 * 1024


def _params(sem=None):
    return pltpu.CompilerParams(dimension_semantics=sem, vmem_limit_bytes=VMEM_LIMIT)


def _nt(a, b):
    return lax.dot_general(a, b, (((1,), (1,)), ((), ())), preferred_element_type=F32)


def _tn(a, b):
    return lax.dot_general(a, b, (((0,), (0,)), ((), ())), preferred_element_type=F32)


def _nn(a, b):
    return jnp.dot(a, b, preferred_element_type=F32)


def _fold8(v):
    r, c = v.shape
    return jnp.sum(v.reshape(r // 8, 8, c), axis=0)


NCH = 512


def _rows_call(name, body, tm, row_ins, consts, row_outs, accs=(), rider=None):
    t = row_ins[0].shape[0]
    ride = _Ride(rider, len(row_ins) + len(consts), len(row_outs) + len(accs), t // tm)

    def stepped(*refs):
        step = pl.program_id(0)
        ride.before(refs, step)
        body(*ride.own(refs))
        ride.after(refs, step)

    in_specs = [pl.BlockSpec((tm, a.shape[1]), lambda i: (i, 0)) for a in row_ins]
    for a, layer in consts:
        if layer is None:
            in_specs.append(pl.BlockSpec(a.shape, lambda i: (0, 0)))
        else:
            in_specs.append(pl.BlockSpec((None, *a.shape[1:]), lambda i, l=layer: (l, 0, 0)))
    return ride.call(
        stepped, name, (t // tm,), in_specs, [*row_ins, *[a for a, _ in consts]],
        [pl.BlockSpec((tm, c), lambda i: (i, 0)) for c, _ in row_outs]
        + [pl.BlockSpec(s, lambda i: (0, 0)) for s in accs],
        [jax.ShapeDtypeStruct((t, c), dt) for c, dt in row_outs] + [jax.ShapeDtypeStruct(s, F32) for s in accs],
        [], ("arbitrary",) if accs else ("parallel",))


def _norm_parts(x):
    r = lax.rsqrt(jnp.mean(x * x, axis=-1, keepdims=True) + EPS)
    return r, x * r


def _norm_bwd(r, xh, dyn, g):
    w = dyn * g
    return r * (w - xh * jnp.mean(w * xh, axis=-1, keepdims=True))


def _in_proj(name, h, g, w, layer, tm, ncp):
    d, n = h.shape[1], w.shape[2]

    def body(h_ref, g_ref, w_ref, hn_ref, ucp_ref, qkv_ref):
        _, xh = _norm_parts(h_ref[...])
        hn = (xh * g_ref[...]).astype(BF16)
        hn_ref[...] = hn
        for n0 in range(0, n, NCH):
            acc = _nn(hn, w_ref[:, n0:n0 + NCH])
            if n0 < ncp:
                ucp_ref[:, n0:n0 + NCH] = acc
            else:
                qkv_ref[:, n0 - ncp:n0 - ncp + NCH] = acc.astype(BF16)

    return _rows_call(name, body, tm, [h], [(g.reshape(1, d), None), (w, layer)],
                      [(d, BF16), (ncp, F32), (n - ncp, BF16)])


def _out_proj(name, y_cp, y_at, h, w, layer, tm):
    d, k1 = h.shape[1], y_cp.shape[1]

    def body(ycp_ref, yat_ref, h_ref, w_ref, o_ref):
        for n0 in range(0, d, NCH):
            o_ref[:, n0:n0 + NCH] = (h_ref[:, n0:n0 + NCH] + _nn(ycp_ref[...], w_ref[0:k1, n0:n0 + NCH])
                                     + _nn(yat_ref[...], w_ref[k1:, n0:n0 + NCH]))

    return _rows_call(name, body, tm, [y_cp, y_at, h], [(w, layer)], [(d, F32)])[0]


def _up_proj(name, h_mid, g, w, layer, tm, rider=None):
    d, n = h_mid.shape[1], w.shape[2]

    def body(h_ref, g_ref, w_ref, hn_ref, m_ref, act_ref):
        _, xh = _norm_parts(h_ref[...])
        hn = (xh * g_ref[...]).astype(BF16)
        hn_ref[...] = hn
        for n0 in range(0, n, NCH):
            acc = _nn(hn, w_ref[:, n0:n0 + NCH])
            m_ref[:, n0:n0 + NCH] = acc.astype(BF16)
            act_ref[:, n0:n0 + NCH] = jnp.square(jnp.maximum(acc, 0.0)).astype(BF16)

    return _rows_call(name, body, tm, [h_mid], [(g.reshape(1, d), None), (w, layer)],
                      [(d, BF16), (n, BF16), (n, BF16)], rider=rider)


def _down_proj(name, act, h_mid, w, layer, tm, rider=None):
    d = h_mid.shape[1]

    def body(a_ref, h_ref, w_ref, o_ref):
        for n0 in range(0, d, NCH):
            o_ref[:, n0:n0 + NCH] = h_ref[:, n0:n0 + NCH] + _nn(a_ref[...], w_ref[:, n0:n0 + NCH])

    return _rows_call(name, body, tm, [act, h_mid], [(w, layer)], [(d, F32)], rider=rider)


def _down_proj_dx(name, dh, m_pre, w, layer, tm, rider=None):
    n = w.shape[1]

    def body(dh_ref, m_ref, w_ref, dm_ref):
        dhb = dh_ref[...].astype(BF16)
        for n0 in range(0, n, NCH):
            dm_ref[:, n0:n0 + NCH] = (_nt(dhb, w_ref[n0:n0 + NCH, :])
                                      * (2.0 * jnp.maximum(m_ref[:, n0:n0 + NCH].astype(F32), 0.0))).astype(BF16)

    return _rows_call(name, body, tm, [dh, m_pre], [(w, layer)], [(n, BF16)], rider=rider)


def _up_proj_dx(name, dm, h_mid, dh, g, w_up, w_out, layer, tm, rider=None):
    d = h_mid.shape[1]

    def body(dm_ref, h_ref, dh_ref, g_ref, wup_ref, wout_ref, dhm_ref, dy_ref, dg_ref):
        @pl.when(pl.program_id(0) == 0)
        def _():
            dg_ref[...] = jnp.zeros_like(dg_ref)
        dyn = _nt(dm_ref[...], wup_ref[...])
        r, xh = _norm_parts(h_ref[...])
        dhm = dh_ref[...] + _norm_bwd(r, xh, dyn, g_ref[...])
        dhm_ref[...] = dhm
        dg_ref[...] += _fold8(dyn * xh)
        dy_ref[...] = _nt(dhm.astype(BF16), wout_ref[...])

    return _rows_call(name, body, tm, [dm, h_mid, dh], [(g.reshape(1, d), None), (w_up, layer), (w_out, layer)],
                      [(d, F32), (w_out.shape[1], F32)], [(8, d)], rider)


def _in_proj_dx(name, dus, h, dh_mid, g, w, layer, tm, rider=None):
    d = h.shape[1]
    ns = [du.shape[1] for du in dus]
    nd = len(dus)

    def body(*refs):
        du_refs = refs[:nd]
        h_ref, dhm_ref, g_ref, w_ref, dh_ref, dg_ref = refs[nd:]

        @pl.when(pl.program_id(0) == 0)
        def _():
            dg_ref[...] = jnp.zeros_like(dg_ref)
        dyn, off = None, 0
        for du_ref, n in zip(du_refs, ns):
            part = _nt(du_ref[...], w_ref[:, off:off + n])
            dyn = part if dyn is None else dyn + part
            off += n
        r, xh = _norm_parts(h_ref[...])
        dh_ref[...] = dhm_ref[...] + _norm_bwd(r, xh, dyn, g_ref[...])
        dg_ref[...] += _fold8(dyn * xh)

    return _rows_call(name, body, tm, [*dus, h, dh_mid], [(g.reshape(1, d), None), (w, layer)], [(d, F32)],
                      [(8, d)], rider)


def _mm_tn(name, a, b, tt, tka, tn, into, shape, layer, row_off, col_off):
    t, ka = a.shape
    n = b.shape[1]
    assert t % tt == 0 and ka % tka == 0 and n % tn == 0 and row_off % tka == 0 and col_off % tn == 0

    def body(a_ref, b_ref, *rest):
        o_ref = rest[-1]

        @pl.when(pl.program_id(2) == 0)
        def _():
            o_ref[...] = jnp.zeros_like(o_ref)
        o_ref[...] += _tn(a_ref[...].astype(BF16), b_ref[...].astype(BF16))

    in_specs = [pl.BlockSpec((tt, tka), lambda i, j, s: (s, i)), pl.BlockSpec((tt, tn), lambda i, j, s: (s, j))]
    args = [a, b]
    if into is not None:
        in_specs.append(pl.BlockSpec(memory_space=pl.ANY))
        args.append(into)
    return pl.pallas_call(
        body, name=name, grid=(ka // tka, n // tn, t // tt), in_specs=in_specs,
        out_specs=pl.BlockSpec((None, tka, tn), lambda i, j, s: (layer, row_off // tka + i, col_off // tn + j)),
        out_shape=jax.ShapeDtypeStruct(shape, F32),
        input_output_aliases={} if into is None else {2: 0},
        compiler_params=_params(("parallel", "parallel", "arbitrary")),
    )(*args)


def _mm_tn_slab(name, a_list, b_list, tt, into, shape, layer):
    t = a_list[0].shape[0]
    kas, ns = [a.shape[1] for a in a_list], [b.shape[1] for b in b_list]
    assert t % tt == 0 and (sum(kas), sum(ns)) == tuple(shape[1:])
    na, nb = len(a_list), len(b_list)

    def body(*refs):
        o_ref = refs[-1]

        @pl.when(pl.program_id(0) == 0)
        def _():
            o_ref[...] = jnp.zeros_like(o_ref)
        r0 = 0
        for a_ref, ka in zip(refs[:na], kas):
            a = a_ref[...].astype(BF16)
            c0 = 0
            for b_ref, n in zip(refs[na:na + nb], ns):
                o_ref[r0:r0 + ka, c0:c0 + n] += _tn(a, b_ref[...].astype(BF16))
                c0 += n
            r0 += ka

    in_specs = [pl.BlockSpec((tt, c), lambda s: (s, 0)) for c in kas + ns]
    args = [*a_list, *b_list]
    if into is not None:
        in_specs.append(pl.BlockSpec(memory_space=pl.ANY))
        args.append(into)
    return pl.pallas_call(
        body, name=name, grid=(t // tt,), in_specs=in_specs,
        out_specs=pl.BlockSpec((None, *shape[1:]), lambda s: (layer, 0, 0)),
        out_shape=jax.ShapeDtypeStruct(shape, F32),
        input_output_aliases={} if into is None else {na + nb: 0},
        compiler_params=_params(("arbitrary",)),
    )(*args)


def _build_h(name, x, meta, lp, rider=None):
    bl, s, d = x.shape
    nq = lp // QB
    ride = _Ride(rider, 2, 1, bl * nq)

    def body(*refs):
        x_ref, m_ref, o_ref = ride.own(refs)
        j = pl.program_id(1)
        step = pl.program_id(0) * nq + j
        ride.before(refs, step)
        head = jnp.concatenate([jnp.zeros((PAD, d), F32), m_ref[...]], axis=0)
        o_ref[...] = jnp.where(j == 0, head, x_ref[...])
        ride.after(refs, step)

    return ride.call(
        body, name, (bl, nq),
        [pl.BlockSpec((None, QB, d), lambda b, j: (b, jnp.maximum(j - 1, 0), 0)),
         pl.BlockSpec(meta.shape, lambda b, j: (0, 0))], [x, meta],
        [pl.BlockSpec((QB, d), lambda b, j: (b * nq + j, 0))], [jax.ShapeDtypeStruct((bl * lp, d), F32)], [],
        ("parallel", "arbitrary"))


def _loss_bwd(name, h, g, target, lp):
    t, d = h.shape
    bl = target.shape[0]
    nq = lp // QB

    def body(h_ref, g_ref, t_ref, dh_ref, ls_ref, dg_ref):
        b, j = pl.program_id(0), pl.program_id(1)

        @pl.when((b == 0) & (j == 0))
        def _():
            ls_ref[...] = jnp.zeros_like(ls_ref)
            dg_ref[...] = jnp.zeros_like(dg_ref)
        xv = h_ref[...]
        r = lax.rsqrt(jnp.mean(xv * xv, axis=-1, keepdims=True) + EPS)
        xh = xv * r
        gv = g_ref[...]
        err = jnp.where(j >= 1, xh * gv - t_ref[...], 0.0)
        ls_ref[...] += _fold8(err * err) * (0.5 / d)
        dy = err * (1.0 / d)
        w = dy * gv
        dh_ref[...] = r * (w - xh * jnp.mean(w * xh, axis=-1, keepdims=True))
        dg_ref[...] += _fold8(dy * xh)

    return pl.pallas_call(
        body, name=name, grid=(bl, nq),
        in_specs=[pl.BlockSpec((QB, d), lambda b, j: (b * nq + j, 0)), pl.BlockSpec((1, d), lambda b, j: (0, 0)),
                  pl.BlockSpec((None, QB, d), lambda b, j: (b, jnp.maximum(j - 1, 0), 0))],
        out_specs=[pl.BlockSpec((QB, d), lambda b, j: (b * nq + j, 0)), pl.BlockSpec((8, d), lambda b, j: (0, 0)),
                   pl.BlockSpec((8, d), lambda b, j: (0, 0))],
        out_shape=[jax.ShapeDtypeStruct((t, d), F32), jax.ShapeDtypeStruct((8, d), F32),
                   jax.ShapeDtypeStruct((8, d), F32)],
        compiler_params=_params(("arbitrary", "arbitrary")),
    )(h, g.reshape(1, d), target)


def _pool_select(grp, a2, a4, a8, a16):
    return jnp.where(grp == 0, a2, jnp.where(grp == 1, a4, jnp.where(grp == 2, a8, a16)))


def _trailing_sums(v):
    s2 = v + pltpu.roll(v, 1, 0)
    s4 = s2 + pltpu.roll(s2, 2, 0)
    s8 = s4 + pltpu.roll(s4, 4, 0)
    s16 = s8 + pltpu.roll(s8, 8, 0)
    return s2, s4, s8, s16


def _leading_sums(v):
    n = v.shape[0]
    s2 = v + pltpu.roll(v, n - 1, 0)
    s4 = s2 + pltpu.roll(s2, n - 2, 0)
    s8 = s4 + pltpu.roll(s4, n - 4, 0)
    s16 = s8 + pltpu.roll(s8, n - 8, 0)
    return s2, s4, s8, s16


def _convpool_fwd(name, u_cp, wconv, wbd, pscale, lp, r):
    t = u_cp.shape[0]
    cw = u_cp.shape[1] // 4
    tps, hb = lp // r, r // HALO

    def body(cb_ref, cc_ref, cx_ref, pi_ref, cch_ref, cxh_ref, pih_ref, wc_ref, wbd_ref, ps_ref, y_ref):
        i = pl.program_id(0)
        lrow = (i % tps) * r + lax.broadcasted_iota(jnp.int32, (r, 1), 0)
        valid = lrow >= PAD
        xx = jnp.concatenate([cch_ref[...] * cxh_ref[...], cc_ref[...] * cx_ref[...]], axis=0)
        conv = (wc_ref[0:1, :] * pltpu.roll(xx, 2, 0) + wc_ref[1:2, :] * pltpu.roll(xx, 1, 0)
                + wc_ref[2:3, :] * xx)
        y_ref[:, 0:cw] = (cb_ref[...] * conv[HALO:]).astype(y_ref.dtype)
        p = pi_ref[...]
        grp = lax.broadcasted_iota(jnp.int32, (1, cw), 1) // (cw // 4)
        sel = _pool_select(grp, *_trailing_sums(jnp.concatenate([pih_ref[...], p], axis=0)))[HALO:]
        cnt = jnp.maximum(jnp.minimum((lrow - (PAD - 1)).astype(F32), _pool_select(grp, *POOL_WINDOWS)), 1.0)
        pooled = jnp.where(valid, sel / cnt - p, 0.0)
        y_ref[:, cw:2 * cw] = (_nn(pooled.astype(BF16), wbd_ref[...]) * ps_ref[...]).astype(y_ref.dtype)

    def main(col):
        return pl.BlockSpec((r, cw), lambda i: (i, col))

    def prev(col):
        return pl.BlockSpec((HALO, cw), lambda i: (jnp.maximum(i * hb - 1, 0), col))

    def whole(a):
        return pl.BlockSpec(a.shape, lambda i: (0, 0))

    return pl.pallas_call(
        body, name=name, grid=(t // r,),
        in_specs=[main(0), main(1), main(2), main(3), prev(1), prev(2), prev(3), whole(wconv), whole(wbd),
                  whole(pscale)],
        out_specs=pl.BlockSpec((r, 2 * cw), lambda i: (i, 0)),
        out_shape=jax.ShapeDtypeStruct((t, 2 * cw), BF16),
        compiler_params=_params(("parallel",)),
    )(u_cp, u_cp, u_cp, u_cp, u_cp, u_cp, u_cp, wconv, wbd, pscale)


def _convpool_bwd(name, u_cp, dy, wconv, wbd, pscale, lp, r):
    t = u_cp.shape[0]
    cw = u_cp.shape[1] // 4
    tps, hb = lp // r, r // HALO
    e = r + HALO

    def body(cb_ref, cc_ref, cx_ref, pi_ref, cbn_ref, cch_ref, cxh_ref, pih_ref, dyc_ref, dyp_ref, dycn_ref,
             dypn_ref, wc_ref, wbd_ref, ps_ref, du_ref, sm_ref, dwbd_ref):
        i = pl.program_id(0)

        @pl.when(i == 0)
        def _():
            sm_ref[...] = jnp.zeros_like(sm_ref)
            dwbd_ref[...] = jnp.zeros_like(dwbd_ref)
        lrow_e = (i % tps) * r + lax.broadcasted_iota(jnp.int32, (e, 1), 0)
        valid_e = (lrow_e >= PAD) & (lrow_e < lp)
        lrow, valid = lrow_e[:r], lrow_e[:r] >= PAD
        w0, w1, w2 = wc_ref[0:1, :], wc_ref[1:2, :], wc_ref[2:3, :]
        cb, cc, cx = cb_ref[...], cc_ref[...], cx_ref[...]
        prod = cc * cx
        xx = jnp.concatenate([cch_ref[...] * cxh_ref[...], prod], axis=0)
        back1, back2 = pltpu.roll(xx, 1, 0)[HALO:], pltpu.roll(xx, 2, 0)[HALO:]
        dyc = dyc_ref[...]
        du_ref[:, 0:cw] = (dyc * (w0 * back2 + w1 * back1 + w2 * prod)).astype(du_ref.dtype)
        dconv_e = jnp.where(valid_e, jnp.concatenate([dyc * cb, dycn_ref[...] * cbn_ref[...]], axis=0), 0.0)
        dconv = dconv_e[:r]
        dprod = (w2 * dconv + w1 * pltpu.roll(dconv_e, e - 1, 0)[:r] + w0 * pltpu.roll(dconv_e, e - 2, 0)[:r])
        du_ref[:, cw:2 * cw] = (dprod * cx).astype(du_ref.dtype)
        du_ref[:, 2 * cw:3 * cw] = (dprod * cc).astype(du_ref.dtype)
        sm_ref[0:8, :] += _fold8(dconv * back2)
        sm_ref[8:16, :] += _fold8(dconv * back1)
        sm_ref[16:24, :] += _fold8(dconv * prod)
        p = pi_ref[...]
        grp = lax.broadcasted_iota(jnp.int32, (1, cw), 1) // (cw // 4)
        win = _pool_select(grp, *POOL_WINDOWS)
        sel = _pool_select(grp, *_trailing_sums(jnp.concatenate([pih_ref[...], p], axis=0)))[HALO:]
        cnt_e = jnp.maximum(jnp.minimum((lrow_e - (PAD - 1)).astype(F32), win), 1.0)
        pooled = jnp.where(valid, sel / cnt_e[:r] - p, 0.0).astype(BF16)
        dyp = dyp_ref[...]
        sm_ref[24:32, :] += _fold8(dyp * _nn(pooled, wbd_ref[...]))
        dpre_e = (jnp.concatenate([dyp, dypn_ref[...]], axis=0) * ps_ref[...]).astype(BF16)
        dwbd_ref[...] += _tn(pooled, dpre_e[:r])
        dpooled_e = jnp.where(valid_e, _nt(dpre_e, wbd_ref[...]), 0.0)
        ahead = _pool_select(grp, *_leading_sums(dpooled_e / cnt_e))[:r]
        du_ref[:, 3 * cw:4 * cw] = (ahead - dpooled_e[:r]).astype(du_ref.dtype)

    last_halo = t // HALO - 1

    def main(col):
        return pl.BlockSpec((r, cw), lambda i: (i, col))

    def prev(col):
        return pl.BlockSpec((HALO, cw), lambda i: (jnp.maximum(i * hb - 1, 0), col))

    def nxt(col):
        return pl.BlockSpec((HALO, cw), lambda i: (jnp.minimum((i + 1) * hb, last_halo), col))

    def whole(a):
        return pl.BlockSpec(a.shape, lambda i: (0, 0))

    return pl.pallas_call(
        body, name=name, grid=(t // r,),
        in_specs=[main(0), main(1), main(2), main(3), nxt(0), prev(1), prev(2), prev(3), main(0), main(1), nxt(0),
                  nxt(1), whole(wconv), whole(wbd), whole(pscale)],
        out_specs=[pl.BlockSpec((r, 4 * cw), lambda i: (i, 0)), pl.BlockSpec((32, cw), lambda i: (0, 0)),
                   pl.BlockSpec((cw, cw), lambda i: (0, 0))],
        out_shape=[jax.ShapeDtypeStruct((t, 4 * cw), BF16), jax.ShapeDtypeStruct((32, cw), F32),
                   jax.ShapeDtypeStruct((cw, cw), F32)],
        compiler_params=_params(("arbitrary",)),
    )(u_cp, u_cp, u_cp, u_cp, u_cp, u_cp, u_cp, u_cp, dy, dy, dy, dy, wconv, wbd, pscale)


KW = 2 * QB
HP = 4
DECAY = 64.0


def _cumsum_matrix(before, kw):
    r = lax.broadcasted_iota(jnp.int32, (kw, kw), 0)
    c = lax.broadcasted_iota(jnp.int32, (kw, kw), 1)
    return ((r < c) if before else (r > c)).astype(BF16)


def _running(v, mat):
    m = v.shape[0]
    hi = v.astype(BF16)
    ext = _nn(jnp.concatenate([hi, (v - hi.astype(F32)).astype(BF16)], axis=0), mat)
    return ext[:m] + ext[m:]


def _log_sigmoid(z):
    neg_abs = lax.bitcast_convert_type(lax.bitcast_convert_type(z, jnp.int32) | jnp.int32(-2 ** 31), F32)
    return jnp.minimum(z, 0.0) - jnp.log(1.0 + jnp.exp(neg_abs))


def _stack_heads(v, head0):
    zero = jnp.zeros_like(v)
    return jnp.concatenate([jnp.where(head0, v, zero), jnp.where(head0, zero, v)], axis=0)


def _lanes(hp):
    return slice(hp * QB, (hp + 1) * QB)


def _key_walk(qi):
    prev = jnp.maximum(qi - 1, 0)
    return prev, (prev % 2) * QB, prev // 2, 1 - prev % 2


def _attn_fwd(name, qkv, bl, lp, rider=None):
    t = qkv.shape[0]
    nq, nblk = lp // QB, qkv.shape[1] // (3 * HP * QB)
    assert nblk == 1
    ride = _Ride(rider, 3, 3, bl * nblk * nq)

    def body(*refs):
        q_ref, k_ref, v_ref, o_ref, lt_ref, g0_ref, run_s, acc_s = ride.own(refs)
        qi = pl.program_id(2)
        step = (pl.program_id(0) * nblk + pl.program_id(1)) * nq + qi
        ride.before(refs, step)
        head0 = lax.broadcasted_iota(jnp.int32, (QB, QB), 1) < QB // 2
        q2 = [_stack_heads(q_ref[:, _lanes(hp)] * jnp.asarray(HEAD_SCALE, BF16), head0) for hp in range(HP)]
        later = {KW: _cumsum_matrix(False, KW), QB: _cumsum_matrix(False, QB)}
        q_pos = qi * QB + (lax.broadcasted_iota(jnp.int32, (2 * QB, KW), 0) & (QB - 1))
        col = lax.broadcasted_iota(jnp.int32, (2 * QB, KW), 1)
        prev, off, ngrp, lo_g = _key_walk(qi)

        def group(start, kw, masked):
            start = pl.multiple_of(start, QB)
            if masked:
                k_pos = start + col[:, :kw]
                valid = (k_pos < q_pos[:, :kw]) & (k_pos >= PAD)
            z = [_nt(q2[hp], k_ref[pl.ds(start, kw), _lanes(hp)]) for hp in range(HP)]
            logp, after, rs = [], [], []
            for hp in range(HP):
                lp_ = _log_sigmoid(z[hp])
                lk = lp_ - z[hp]
                if masked:
                    lk = jnp.where(valid, lk, 0.0)
                logp.append(lp_)
                rs.append(jnp.sum(lk, axis=1, keepdims=True))
                after.append(_running(lk, later[kw]))
            for hp in range(HP):
                run = run_s[hp]
                a = jnp.exp(logp[hp] + after[hp] + run)
                if masked:
                    a = jnp.where(valid, a, 0.0)
                run_s[hp] = run + rs[hp]
                acc_s[hp] += _nn(a.astype(BF16), v_ref[pl.ds(start, kw), _lanes(hp)])

        def alive():
            most = run_s[0]
            for hp in range(1, HP):
                most = jnp.maximum(most, run_s[hp])
            return jnp.max(most) > -DECAY

        def older(st):
            group(off + st[0] * KW, KW, False)
            return st[0] - 1, alive()

        run_s[...] = jnp.zeros_like(run_s)
        acc_s[...] = jnp.zeros_like(acc_s)
        group(prev * QB, KW, True)
        g, live = lax.while_loop(lambda st: (st[0] >= lo_g) & st[1], older, (ngrp - 1, alive()))
        bottom = (g < lo_g) & live & ((off > 0) | (ngrp >= 1))
        pl.when(bottom & (off > 0))(lambda: group(0, QB, True))
        pl.when(bottom & (off == 0))(lambda: group(0, KW, True))
        g0_ref[pl.program_id(0), qi] = jnp.where(bottom, -1, g + 1).astype(F32)
        for hp in range(HP):
            o_ref[:, _lanes(hp)] = jnp.where(head0, acc_s[hp, :QB], acc_s[hp, QB:]).astype(o_ref.dtype)
            lt_ref[:, _lanes(hp)] = jnp.where(head0, run_s[hp, :QB], run_s[hp, QB:])
        ride.after(refs, step)

    wb = HP * QB
    blk = pl.BlockSpec((QB, wb), lambda b, p, i: (b * nq + i, p))
    return ride.call(
        body, name, (bl, nblk, nq),
        [blk, pl.BlockSpec((lp, wb), lambda b, p, i: (b, nblk + p)),
         pl.BlockSpec((lp, wb), lambda b, p, i: (b, 2 * nblk + p))], [qkv, qkv, qkv],
        [blk, blk, pl.BlockSpec(memory_space=pltpu.SMEM)],
        [jax.ShapeDtypeStruct((t, nblk * wb), BF16), jax.ShapeDtypeStruct((t, nblk * wb), F32),
         jax.ShapeDtypeStruct((bl, nq), F32)],
        [pltpu.VMEM((HP, 2 * QB, 1), F32), pltpu.VMEM((HP, 2 * QB, QB), F32)])


def _attn_bwd(name, qkv, lt, g0, dy, bl, lp, rider=None):
    t = qkv.shape[0]
    nq, nblk = lp // QB, qkv.shape[1] // (3 * HP * QB)
    ride = _Ride(rider, 6, 3, bl * nblk * nq)

    def body(*refs):
        (q_ref, k_ref, v_ref, lt_ref, do_ref, g0_ref, dq_ref, dk_ref, dv_ref, dk_acc, dv_acc, seen_s, gsum_s,
         dq_s) = ride.own(refs)
        qi = pl.program_id(2)
        step = (pl.program_id(0) * nblk + pl.program_id(1)) * nq + qi
        ride.before(refs, step)

        @pl.when(qi == 0)
        def _():
            dk_acc[...] = jnp.zeros_like(dk_acc)
            dv_acc[...] = jnp.zeros_like(dv_acc)
        lane = lax.broadcasted_iota(jnp.int32, (QB, QB), 1)
        head0 = lane < QB // 2
        q2, do2, total = [], [], []
        for hp in range(HP):
            q2.append(_stack_heads(q_ref[:, _lanes(hp)] * jnp.asarray(HEAD_SCALE, BF16), head0))
            do2.append(_stack_heads(do_ref[:, _lanes(hp)].astype(BF16), head0))
            ltv = lt_ref[:, _lanes(hp)]
            total.append(jnp.concatenate(
                [jnp.sum(jnp.where(lane == 0, ltv, 0.0), axis=1, keepdims=True),
                 jnp.sum(jnp.where(lane == QB // 2, ltv, 0.0), axis=1, keepdims=True)], axis=0))
        later = {KW: _cumsum_matrix(False, KW), QB: _cumsum_matrix(False, QB)}
        earlier = {KW: _cumsum_matrix(True, KW), QB: _cumsum_matrix(True, QB)}
        q_pos = qi * QB + (lax.broadcasted_iota(jnp.int32, (2 * QB, KW), 0) & (QB - 1))
        col = lax.broadcasted_iota(jnp.int32, (2 * QB, KW), 1)
        prev, off, ngrp, lo_g = _key_walk(qi)

        def group(start, kw, masked):
            start = pl.multiple_of(start, QB)
            if masked:
                k_pos = start + col[:, :kw]
                valid = (k_pos < q_pos[:, :kw]) & (k_pos >= PAD)
            hps = range(HP)
            kg = [k_ref[pl.ds(start, kw), _lanes(hp)] for hp in hps]
            z = [_nt(q2[hp], kg[hp]) for hp in hps]
            da = [_nt(do2[hp], v_ref[pl.ds(start, kw), _lanes(hp)]) for hp in hps]
            logp, sig, after, rs = [], [], [], []
            for hp in hps:
                lp_ = _log_sigmoid(z[hp])
                lk = lp_ - z[hp]
                if masked:
                    lk = jnp.where(valid, lk, 0.0)
                logp.append(lp_)
                sig.append(jnp.exp(lp_))
                rs.append(jnp.sum(lk, axis=1, keepdims=True))
                after.append(_running(lk, later[kw]))
            a, gg, before = [], [], []
            for hp in hps:
                a_ = jnp.exp(logp[hp] + after[hp] + (total[hp] - seen_s[hp] - rs[hp]))
                if masked:
                    a_ = jnp.where(valid, a_, 0.0)
                a.append(a_.astype(BF16))
                gg.append(a_ * da[hp])
                before.append(_nn(gg[hp].astype(BF16), earlier[kw]))
            for hp in hps:
                gsum = gsum_s[hp]
                dz = gg[hp] - (gg[hp] + before[hp] + gsum) * sig[hp]
                if masked:
                    dz = jnp.where(valid, dz, 0.0)
                dz = dz.astype(BF16)
                dk_acc[pl.ds(start, kw), _lanes(hp)] += _tn(dz, q2[hp])
                dv_acc[pl.ds(start, kw), _lanes(hp)] += _tn(a[hp], do2[hp])
                seen_s[hp] += rs[hp]
                gsum_s[hp] = gsum + jnp.sum(gg[hp], axis=1, keepdims=True)
                dq_s[hp] += _nn(dz, kg[hp])

        def inner(g, _):
            group(off + g * KW, KW, False)
            return 0

        seen_s[...] = jnp.zeros_like(seen_s)
        gsum_s[...] = jnp.zeros_like(gsum_s)
        dq_s[...] = jnp.zeros_like(dq_s)
        first = g0_ref[pl.program_id(0), qi].astype(jnp.int32)
        pl.when((first < 0) & (off > 0))(lambda: group(0, QB, True))
        pl.when((first < 0) & (off == 0))(lambda: group(0, KW, True))
        lax.fori_loop(jnp.where(first < 0, lo_g, first), ngrp, inner, 0)
        group(prev * QB, KW, True)
        for hp in range(HP):
            dq_ref[:, _lanes(hp)] = (jnp.where(head0, dq_s[hp, :QB], dq_s[hp, QB:])
                                     * HEAD_SCALE).astype(dq_ref.dtype)

        @pl.when(qi == nq - 1)
        def _():
            dk_ref[...] = dk_acc[...].astype(dk_ref.dtype)
            dv_ref[...] = dv_acc[...].astype(dv_ref.dtype)
        ride.after(refs, step)

    wb = HP * QB
    blk = pl.BlockSpec((QB, wb), lambda b, p, i: (b * nq + i, p))
    seq = pl.BlockSpec((lp, wb), lambda b, p, i: (b, p))
    out = jax.ShapeDtypeStruct((t, nblk * wb), BF16)
    return ride.call(
        body, name, (bl, nblk, nq),
        [blk, pl.BlockSpec((lp, wb), lambda b, p, i: (b, nblk + p)),
         pl.BlockSpec((lp, wb), lambda b, p, i: (b, 2 * nblk + p)), blk,
         pl.BlockSpec((QB, wb), lambda b, p, i: (b * nq + i, nblk + p)), pl.BlockSpec(memory_space=pltpu.SMEM)],
        [qkv, qkv, qkv, lt, dy, g0],
        [blk, seq, seq], [out, out, out],
        [pltpu.VMEM((lp, wb), F32), pltpu.VMEM((lp, wb), F32), pltpu.VMEM((HP, 2 * QB, 1), F32),
         pltpu.VMEM((HP, 2 * QB, 1), F32), pltpu.VMEM((HP, 2 * QB, QB), F32)])


def _place():
    return lax.axis_index("x"), lax.axis_index("y"), lax.axis_index("c")


def _peers(chip):
    kx, ky = chip // 2, chip % 2
    return ((1 - kx, ky), (kx, 1 - ky), (1 - kx, 1 - ky))


def _hbm_specs(n):
    return [pl.BlockSpec(memory_space=pl.ANY) for _ in range(n)]


def _remote(src, dst, send, recv, k, to):
    return pltpu.make_async_remote_copy(src, dst, send.at[k], recv.at[k], device_id=to, device_id_type=MESH)


class _Rider:
    def __init__(self, ins, out_shapes, aliases, nsem, first, mid=None, last=None):
        self.ins, self.out_shapes, self.aliases, self.nsem = list(ins), list(out_shapes), dict(aliases), nsem
        self.first, self.mid, self.last = first, mid, last


def _by_chip(fn):
    def run(ins, outs, send, recv):
        x, y, c = _place()
        for me in range(4):
            pl.when(2 * x + y == me)(functools.partial(fn, ins, outs, send, recv, me, c, (x, y, 1 - c)))
    return run


def _run_rider(name, rider):
    ni, no = len(rider.ins), len(rider.out_shapes)

    def body(*refs):
        args = (refs[:ni], refs[ni:ni + no], refs[ni + no], refs[ni + no + 1])
        for hook in (rider.first, rider.mid, rider.last):
            if hook is not None:
                hook(*args)

    return pl.pallas_call(
        body, name=name, in_specs=_hbm_specs(ni), out_specs=_hbm_specs(no), out_shape=rider.out_shapes,
        input_output_aliases=rider.aliases,
        scratch_shapes=[pltpu.SemaphoreType.DMA((rider.nsem,)), pltpu.SemaphoreType.DMA((rider.nsem,))],
        compiler_params=pltpu.CompilerParams(has_side_effects=True),
    )(*rider.ins)


class _Ride:
    def __init__(self, rider, n_in, n_out, steps):
        self.rider, self.n_in, self.n_out, self.steps = rider, n_in, n_out, steps
        self.ri = len(rider.ins) if rider else 0
        self.ro = len(rider.out_shapes) if rider else 0

    def own(self, refs):
        refs = list(refs)
        a, b = self.n_in, self.n_in + self.ri + self.n_out
        tail = refs[b + self.ro:len(refs) - 2] if self.rider else refs[b + self.ro:]
        return refs[:a] + refs[a + self.ri:b] + tail

    def _args(self, refs):
        a, b = self.n_in, self.n_in + self.ri + self.n_out
        return refs[a:a + self.ri], refs[b:b + self.ro], refs[-2], refs[-1]

    def before(self, refs, step):
        if self.rider is None:
            return
        pl.when(step == 0)(functools.partial(self.rider.first, *self._args(refs)))
        if self.rider.mid is not None:
            pl.when(step == (3 * self.steps) // 4)(functools.partial(self.rider.mid, *self._args(refs)))

    def after(self, refs, step):
        if self.rider is not None and self.rider.last is not None:
            pl.when(step == self.steps - 1)(functools.partial(self.rider.last, *self._args(refs)))

    def call(self, body, name, grid, in_specs, args, out_specs, out_shape, scratch,
             sem=("parallel", "parallel", "arbitrary")):
        r = self.rider
        if r is None:
            return pl.pallas_call(body, name=name, grid=grid, in_specs=in_specs, out_specs=out_specs,
                                  out_shape=out_shape, scratch_shapes=scratch, compiler_params=_params(sem))(*args)
        return pl.pallas_call(
            body, name=name, grid=grid, in_specs=in_specs + _hbm_specs(self.ri),
            out_specs=out_specs + _hbm_specs(self.ro), out_shape=out_shape + r.out_shapes,
            input_output_aliases={self.n_in + i: self.n_out + o for i, o in r.aliases.items()},
            scratch_shapes=scratch + [pltpu.SemaphoreType.DMA((r.nsem,)), pltpu.SemaphoreType.DMA((r.nsem,))],
            compiler_params=pltpu.CompilerParams(dimension_semantics=("arbitrary",) * len(grid),
                                                 vmem_limit_bytes=VMEM_LIMIT, has_side_effects=True),
        )(*args, *r.ins)


def _core_view(a, axis):
    l, r, c = a.shape
    return a.reshape(l, 4, 2, r // 8, c) if axis == 0 else a.reshape(l, 2, r // 2, c)


def _shard_view(a):
    l, r, c = a.shape
    return a.reshape(l, 2, r // 2, c)


def _piece(ref, axis, layer, chip, core):
    if axis == 0:
        return ref.at[layer, chip, core]
    cs = ref.shape[-1] // 4
    return ref.at[layer, core, :, pl.ds(chip * cs, cs)]


def _place_shard(name, w, axis, kidx, tr):
    _, r, cdim = w.shape
    shp = [2, r, cdim]
    shp[1 + axis] *= 4
    nb = r // tr

    def body(k_ref, w_ref, o_ref):
        o_ref[...] = w_ref[...].astype(o_ref.dtype)

    if axis == 0:
        out_spec = pl.BlockSpec((None, tr, cdim), lambda l, i, k_ref: (l, k_ref[0] * nb + i, 0))
    else:
        out_spec = pl.BlockSpec((None, tr, cdim), lambda l, i, k_ref: (l, i, k_ref[0]))
    return pl.pallas_call(
        body, name=name,
        grid_spec=pltpu.PrefetchScalarGridSpec(
            num_scalar_prefetch=1, grid=(2, nb),
            in_specs=[pl.BlockSpec((None, tr, cdim), lambda l, i, k_ref: (l, i, 0))], out_specs=out_spec),
        out_shape=jax.ShapeDtypeStruct(tuple(shp), BF16),
        compiler_params=_params(("arbitrary", "arbitrary")),
    )(kidx, w)


def _gather_rider(views, axes, items):
    n = len(items)

    def first(ins, outs, send, recv, me, c, sib):
        for i, (w, l) in enumerate(items):
            for j, (px, py) in enumerate(_peers(me)):
                _remote(_piece(ins[w], axes[w], l, me, c), _piece(outs[w], axes[w], l, me, c), send, recv,
                        3 * i + j, (px, py, c)).start()

    def mid(ins, outs, send, recv, me, c, sib):
        for i, (w, l) in enumerate(items):
            for j, (px, py) in enumerate(_peers(me)):
                got = _piece(outs[w], axes[w], l, 2 * px + py, c)
                _remote(got, got, send, recv, 3 * i + j, (px, py, c)).wait_recv()
                _remote(got, got, send, recv, 3 * (n + i) + j, sib).start()

    def last(ins, outs, send, recv, me, c, sib):
        for i, (w, l) in enumerate(items):
            for j, (px, py) in enumerate(_peers(me)):
                mine, got = _piece(outs[w], axes[w], l, me, c), _piece(outs[w], axes[w], l, 2 * px + py, c)
                theirs = _piece(outs[w], axes[w], l, 2 * px + py, 1 - c)
                _remote(theirs, theirs, send, recv, 3 * (n + i) + j, sib).wait_recv()
                _remote(mine, mine, send, recv, 3 * i + j, (px, py, c)).wait_send()
                _remote(got, got, send, recv, 3 * (n + i) + j, sib).wait_send()

    return _Rider(views, [jax.ShapeDtypeStruct(v.shape, v.dtype) for v in views], {w: w for w in range(len(views))},
                  6 * n, _by_chip(first), _by_chip(mid), _by_chip(last))


def _swap_rider(views, axes, items):
    nv = len(views)

    def part(ref, w, l, core):
        return ref.at[l, :, core] if axes[w] == 0 else ref.at[l, core]

    def copies(ins, outs, send, recv):
        x, y, c = _place()
        return [_remote(part(ins[w], w, l, 1 - c), outs[nv + i], send, recv, i, (x, y, 1 - c))
                for i, (w, l) in enumerate(items)]

    def first(ins, outs, send, recv):
        for cp in copies(ins, outs, send, recv):
            cp.start()

    def last(ins, outs, send, recv):
        for cp in copies(ins, outs, send, recv):
            cp.wait()

    got = [jax.ShapeDtypeStruct(views[w].shape[1:2] + views[w].shape[3:] if axes[w] == 0 else views[w].shape[2:],
                                views[w].dtype) for w, _ in items]
    return _Rider(views, [jax.ShapeDtypeStruct(v.shape, v.dtype) for v in views] + got,
                  {w: w for w in range(nv)}, len(items), first, None, last)


def _add_core(name, view, got, axis, layer, cidx, tr):
    def body(c_ref, g_ref, r_ref, o_ref):
        o_ref[...] = (g_ref[...] + r_ref[...]).astype(o_ref.dtype)

    if axis == 0:
        _, nchip, _, pr, cdim = view.shape
        grid = (nchip, pr // tr)
        specs = [pl.BlockSpec((None, None, None, tr, cdim), lambda k, i, c_ref: (layer, k, c_ref[0], i, 0)),
                 pl.BlockSpec((None, tr, cdim), lambda k, i, c_ref: (k, i, 0))]
        out_spec = pl.BlockSpec((None, tr, cdim), lambda k, i, c_ref: (k, i, 0))
    else:
        _, _, pr, cdim = view.shape
        grid = (pr // tr,)
        specs = [pl.BlockSpec((None, None, tr, cdim), lambda i, c_ref: (layer, c_ref[0], i, 0)),
                 pl.BlockSpec((tr, cdim), lambda i, c_ref: (i, 0))]
        out_spec = pl.BlockSpec((tr, cdim), lambda i, c_ref: (i, 0))
    return pl.pallas_call(
        body, name=name,
        grid_spec=pltpu.PrefetchScalarGridSpec(num_scalar_prefetch=1, grid=grid, in_specs=specs,
                                               out_specs=out_spec),
        out_shape=jax.ShapeDtypeStruct(got.shape, BF16),
        compiler_params=_params(("arbitrary",) * len(grid)),
    )(cidx, view, got)


def _scatter_rider(sums, axes):
    def part(ref, i, chip):
        if axes[i] == 0:
            return ref.at[chip]
        cs = ref.shape[-1] // 4
        return ref.at[:, pl.ds(chip * cs, cs)]

    def copies(ins, outs, send, recv, me, c, sib):
        return [_remote(part(ins[i], i, 2 * px + py), outs[i].at[j], send, recv, 3 * i + j, (px, py, c))
                for i in range(len(sums)) for j, (px, py) in enumerate(_peers(me))]

    def first(*args):
        for cp in copies(*args):
            cp.start()

    def last(*args):
        for cp in copies(*args):
            cp.wait()

    shapes = [jax.ShapeDtypeStruct((3,) + (s.shape[1:] if ax == 0 else (s.shape[0], s.shape[1] // 4)), s.dtype)
              for s, ax in zip(sums, axes)]
    return _Rider(sums, shapes, {}, 3 * len(sums), _by_chip(first), None, _by_chip(last))


def _add_chips(name, own, got, axis, layer, kc_idx, tr, into, shard_shape):
    _, pr, pc = got.shape

    def body(k_ref, o_ref, g_ref, *rest):
        rest[-1][...] = (o_ref[...].astype(F32) + g_ref[0].astype(F32) + g_ref[1].astype(F32)
                         + g_ref[2].astype(F32))

    if axis == 0:
        own_spec = pl.BlockSpec((None, tr, pc), lambda i, k_ref: (k_ref[0], i, 0))
    else:
        own_spec = pl.BlockSpec((tr, pc), lambda i, k_ref: (i, k_ref[0]))
    specs = [own_spec, pl.BlockSpec((3, tr, pc), lambda i, k_ref: (0, i, 0))]
    args = [kc_idx, own, got]
    if into is not None:
        specs.append(pl.BlockSpec(memory_space=pl.ANY))
        args.append(into)
    return pl.pallas_call(
        body, name=name,
        grid_spec=pltpu.PrefetchScalarGridSpec(
            num_scalar_prefetch=1, grid=(pr // tr,), in_specs=specs,
            out_specs=pl.BlockSpec((None, None, tr, pc), lambda i, k_ref: (layer, k_ref[1], i, 0))),
        out_shape=jax.ShapeDtypeStruct(shard_shape, F32),
        input_output_aliases={} if into is None else {3: 0},
        compiler_params=_params(("arbitrary",)),
    )(*args)


def _join_rider(parts):
    def first(ins, outs, send, recv):
        x, y, c = _place()
        for w in range(len(parts)):
            _remote(ins[w].at[:, c], outs[w].at[:, c], send, recv, w, (x, y, 1 - c)).start()

    def last(ins, outs, send, recv):
        x, y, c = _place()
        for w in range(len(parts)):
            _remote(ins[w].at[:, c], outs[w].at[:, c], send, recv, w, (x, y, 1 - c)).wait_send()
            _remote(ins[w].at[:, c], outs[w].at[:, 1 - c], send, recv, w, (x, y, 1 - c)).wait_recv()

    return _Rider(parts, [jax.ShapeDtypeStruct(p.shape, p.dtype) for p in parts],
                  {w: w for w in range(len(parts))}, len(parts), first, None, last)


def _all_reduce_small(name, pack, lead, groups):
    nr, d = pack.shape
    nout = nr - (groups - 1) * lead

    def body(in_ref, sum_ref, mine, slots, send, recv):
        x, y, c = _place()
        me = 4 * x + 2 * y + c
        fold = in_ref[0:lead]
        for grp in range(1, groups):
            fold = fold + in_ref[grp * lead:(grp + 1) * lead]
        mine[0:lead] = fold
        mine[lead:] = in_ref[groups * lead:]
        slots[me] = mine[...]
        cps = []
        for r in range(1, 8):
            rx, ry, rc = r // 4, (r // 2) % 2, r % 2
            peer = (x + rx - 2 * x * rx, y + ry - 2 * y * ry, c + rc - 2 * c * rc)
            cp = pltpu.make_async_remote_copy(mine, slots.at[me], send.at[r - 1], recv.at[r - 1],
                                              device_id=peer, device_id_type=MESH)
            cp.start()
            cps.append(cp)
        for cp in cps:
            cp.wait()
        acc = slots[0]
        for dev in range(1, 8):
            acc = acc + slots[dev]
        sum_ref[...] = acc

    vmem = pl.BlockSpec(memory_space=pltpu.VMEM)
    return pl.pallas_call(
        body, name=name, in_specs=[vmem], out_specs=vmem, out_shape=jax.ShapeDtypeStruct((nout, d), F32),
        scratch_shapes=[pltpu.VMEM((nout, d), F32), pltpu.VMEM((8, nout, d), F32), pltpu.SemaphoreType.DMA((7,)),
                        pltpu.SemaphoreType.DMA((7,))],
        compiler_params=pltpu.CompilerParams(has_side_effects=True, vmem_limit_bytes=VMEM_LIMIT),
    )(pack)


def _adamw_math(w, g, m, v):
    m = B1 * m + (1.0 - B1) * g
    v = B2 * v + (1.0 - B2) * (g * g)
    m_hat = m / (1.0 - B1 ** STEP)
    v_hat = v / (1.0 - B2 ** STEP)
    return -LR * (m_hat / (jnp.sqrt(v_hat) + ADAM_EPS) + WD * w), m, v


def _adamw(name, w, g, m, v, tr):
    shape = w.shape
    flat = [a.reshape(-1, shape[-1]) for a in (w, g, m, v)]
    r, cdim = flat[0].shape

    def body(w_ref, g_ref, m_ref, v_ref, d_ref, nm_ref, nv_ref):
        d_ref[...], nm_ref[...], nv_ref[...] = _adamw_math(w_ref[...], g_ref[...], m_ref[...], v_ref[...])

    spec = pl.BlockSpec((tr, cdim), lambda i: (i, 0))
    outs = pl.pallas_call(
        body, name=name, grid=(r // tr,), in_specs=[spec] * 4, out_specs=[spec] * 3,
        out_shape=[jax.ShapeDtypeStruct((r, cdim), F32)] * 3,
        compiler_params=_params(("parallel",)),
    )(*flat)
    return [o.reshape(shape) for o in outs]


def _adamw_small(name, groups):
    n = len(groups)
    shapes = [grp[0].shape for grp in groups]
    flat = [a.reshape(-1, a.shape[-1]) for grp in groups for a in grp]

    def body(*refs):
        ins, outs = refs[:4 * n], refs[4 * n:]
        for i in range(n):
            w_ref, g_ref, m_ref, v_ref = ins[4 * i:4 * i + 4]
            outs[3 * i][...], outs[3 * i + 1][...], outs[3 * i + 2][...] = _adamw_math(
                w_ref[...], g_ref[...], m_ref[...], v_ref[...])

    vmem = pl.BlockSpec(memory_space=pltpu.VMEM)
    out_shape = [jax.ShapeDtypeStruct(flat[4 * i].shape, F32) for i in range(n) for _ in range(3)]
    outs = pl.pallas_call(body, name=name, in_specs=[vmem] * (4 * n), out_specs=[vmem] * (3 * n),
                          out_shape=out_shape)(*flat)
    return [[outs[3 * i + j].reshape(shapes[i]) for j in range(3)] for i in range(n)]


def _block_diag(w_grp):
    g, pg, _ = w_grp.shape
    eye = jnp.eye(g, dtype=w_grp.dtype)
    return (eye[:, None, :, None] * w_grp[:, :, None, :]).reshape(g * pg, g * pg)


def _diag_blocks(m, g):
    pg = m.shape[0] // g
    return jnp.stack([m[i * pg:(i + 1) * pg, i * pg:(i + 1) * pg] for i in range(g)])


BIG = ("w_in", "w_out", "w_up", "w_down")
AXES = (1, 0, 1, 0)
W_IN, W_OUT, W_UP, W_DOWN = range(4)


def kernel(x, meta_tokens, g_mix, w_in, w_conv, w_pool, pool_scale, w_out, g_mlp, w_up, w_down, g_final, loss_target, m_meta_tokens, m_g_mix, m_w_in, m_w_conv, m_w_pool, m_pool_scale, m_w_out, m_g_mlp, m_w_up, m_w_down, m_g_final, v_meta_tokens, v_g_mix, v_w_in, v_w_conv, v_w_pool, v_pool_scale, v_w_out, v_g_mlp, v_w_up, v_w_down, v_g_final):
    bl, s, d = x.shape
    depth = g_mix.shape[0]
    assert depth == 2
    lp = PAD + N_META + s
    tt = lp
    tm = lp // 4
    copy_rows, sum_rows, dw_tile = 256, 128, 1024
    cs = w_conv.shape[2]
    cw = 4 * cs
    ngrp = w_pool.shape[1]
    xi, yi, ci = _place()
    chip = (2 * xi + yi).astype(jnp.int32)
    cidx, kidx = ci.astype(jnp.int32).reshape(1), chip.reshape(1)
    kc_idx = jnp.stack([chip, ci.astype(jnp.int32)])
    shards = (w_in, w_out, w_up, w_down)

    views = [_core_view(_place_shard(f"place_{BIG[w]}", shards[w], AXES[w], kidx, copy_rows), AXES[w])
             for w in range(4)]

    def whole(w):
        return views[w].reshape(depth, -1, views[w].shape[-1])

    def gather_on(call, items):
        ws = sorted({w for w, _ in items})
        res = call(_gather_rider([views[w] for w in ws], [AXES[w] for w in ws],
                                 [(ws.index(w), layer) for w, layer in items]))
        for j, w in enumerate(ws):
            views[w] = res[len(res) - len(ws) + j]
        return res[:len(res) - len(ws)]

    placed = jnp.zeros((32, d), F32)
    placed = lax.dynamic_update_slice(placed, meta_tokens, (0, chip * meta_tokens.shape[1]))
    placed = lax.dynamic_update_slice(placed, w_conv.reshape(-1, cs), (N_META, chip * cs))
    placed = jnp.where(ci == 0, placed, 0.0)
    small = _all_reduce_small("gather_small", placed, 8, 1)
    meta_full = small[:N_META]
    conv_full = small[N_META:N_META + depth * 3, :cw].reshape(depth, 3, cw)

    (h,) = gather_on(lambda rider: _build_h("build_h", x, meta_full, lp, rider), [(W_IN, 0)])
    wbd = [_block_diag(w_pool[i]).astype(BF16) for i in range(depth)]
    saved = []
    for i in range(depth):
        hn, u_cp, qkv = _in_proj(f"in_proj{i}", h, g_mix[i], whole(W_IN), i, tm, 4 * cw)
        y_cp = _convpool_fwd(f"convpool{i}", u_cp, conv_full[i], wbd[i], pool_scale[i:i + 1], lp, tm)
        if i == 0:
            y_at, lt, g0 = gather_on(lambda rider: _attn_fwd(f"attn{i}", qkv, bl, lp, rider),
                                     [(W_OUT, 0), (W_UP, 0), (W_DOWN, 0)])
            h_mid = _out_proj(f"out_proj{i}", y_cp, y_at, h, whole(W_OUT), i, tm)
            w_up0 = whole(W_UP)
            hn2, m_pre, act = gather_on(lambda rider: _up_proj(f"up_proj{i}", h_mid, g_mlp[i], w_up0, i, tm, rider),
                                        [(W_OUT, 1), (W_DOWN, 1)])
            w_down0 = whole(W_DOWN)
            (h_next,) = gather_on(lambda rider: _down_proj(f"down_proj{i}", act, h_mid, w_down0, i, tm, rider),
                                  [(W_IN, 1), (W_UP, 1)])
        else:
            y_at, lt, g0 = _attn_fwd(f"attn{i}", qkv, bl, lp)
            h_mid = _out_proj(f"out_proj{i}", y_cp, y_at, h, whole(W_OUT), i, tm)
            hn2, m_pre, act = _up_proj(f"up_proj{i}", h_mid, g_mlp[i], whole(W_UP), i, tm)
            (h_next,) = _down_proj(f"down_proj{i}", act, h_mid, whole(W_DOWN), i, tm)
        saved.append((h, hn, u_cp, qkv, y_cp, y_at, (lt, g0), h_mid, hn2, m_pre, act))
        h = h_next

    dh, loss8, dgf8 = _loss_bwd("loss", h, g_final, loss_target, lp)
    per_layer = {k: [None] * depth for k in ("g_mix", "w_conv", "w_pool", "pool_scale", "g_mlp")}

    gw = [None] * 4
    sums, arrived = {}, {}

    def dw(name, a, b, w, layer):
        shape = whole(w).shape
        into = None if gw[w] is None else gw[w].reshape(shape)
        if isinstance(a, list):
            res = _mm_tn_slab(name, a, b, tt // 2, into, shape, layer)
        else:
            res = _mm_tn(name, a, b, tt, dw_tile, dw_tile, into, shape, layer, 0, 0)
        gw[w] = _core_view(res, AXES[w])

    def swap_rider(ws):
        return _swap_rider([gw[w] for w, _ in ws], [AXES[w] for w, _ in ws],
                           [(j, layer) for j, (_, layer) in enumerate(ws)])

    def swapped(ws, outs):
        for j, (w, layer) in enumerate(ws):
            gw[w] = outs[j]
            sums[w, layer] = _add_core(f"chip_sum_{BIG[w]}{layer}", gw[w], outs[len(ws) + j], AXES[w], layer, cidx,
                                       sum_rows)

    def scatter_rider(items):
        return _scatter_rider([sums[it] for it in items], [AXES[w] for w, _ in items])

    def bwd_mlp(i, dh, swap_early, scatter_early=()):
        _, _, _, _, y_cp, y_at, _, h_mid, hn2, m_pre, act = saved[i]
        items = list(scatter_early)
        dm, *outs = _down_proj_dx(f"down_proj_dx{i}", dh, m_pre, whole(W_DOWN), i, tm,
                                  scatter_rider(items) if items else None)
        arrived.update(zip(items, outs))
        dw(f"down_proj_dw{i}", act, dh, W_DOWN, i)
        dw(f"up_proj_dw{i}", hn2, dm, W_UP, i)
        ws = [(W_DOWN, i), (W_UP, i)] if swap_early else []
        dh_mid, dy, dg8, *outs = _up_proj_dx(f"up_proj_dx{i}", dm, h_mid, dh, g_mlp[i], whole(W_UP), whole(W_OUT), i,
                                             tm, swap_rider(ws) if ws else None)
        swapped(ws, outs)
        per_layer["g_mlp"][i] = dg8.sum(0)
        dw(f"out_proj_dw{i}", [y_cp, y_at], [dh_mid], W_OUT, i)
        return dh_mid, dy

    def bwd_mix(i, dh_mid, dy, dus3, scatter_late):
        h_in, hn, u_cp = saved[i][:3]
        du_cp, sm, dwbd = _convpool_bwd(f"convpool_bwd{i}", u_cp, dy, conv_full[i], wbd[i], pool_scale[i:i + 1], lp,
                                        tm)
        sm = sm.reshape(4, 8, cw).sum(1)
        per_layer["w_conv"][i] = sm[0:3]
        per_layer["pool_scale"][i] = sm[3]
        per_layer["w_pool"][i] = _diag_blocks(dwbd, ngrp)
        dus = [du_cp, *dus3]
        dw(f"in_proj_dw{i}", [hn], dus, W_IN, i)
        rider = None
        if scatter_late:
            swapped([(W_IN, i)], _run_rider(f"grads_swap_in{i}", swap_rider([(W_IN, i)])))
            rider = scatter_rider([(W_IN, i)])
        dh, dg8, *outs = _in_proj_dx(f"in_proj_dx{i}", dus, h_in, dh_mid, g_mix[i], whole(W_IN), i, tm, rider)
        arrived.update(zip([(W_IN, i)], outs))
        per_layer["g_mix"][i] = dg8.sum(0)
        return dh

    def attn_bwd(i, dy, rider):
        qkv, (lt, g0) = saved[i][3], saved[i][6]
        res = _attn_bwd(f"attn_bwd{i}", qkv, lt, g0, dy, bl, lp, rider)
        return res[:3], res[3:]

    dh_mid, dy = bwd_mlp(1, dh, False)
    ws = [(W_DOWN, 1), (W_UP, 1), (W_OUT, 1)]
    dus3, outs = attn_bwd(1, dy, swap_rider(ws))
    swapped(ws, outs)
    dh = bwd_mix(1, dh_mid, dy, dus3, False)

    dh_mid, dy = bwd_mlp(0, dh, True, [(W_DOWN, 1), (W_OUT, 1)])
    ws = [(W_IN, 1), (W_OUT, 0)]
    swapped(ws, _run_rider("grads_swap0", swap_rider(ws)))
    items = [it for it in sums if it not in arrived]
    dus3, outs = attn_bwd(0, dy, scatter_rider(items))
    arrived.update(zip(items, outs))
    dh0 = bwd_mix(0, dh_mid, dy, dus3, True)

    finals = []
    for w in range(4):
        rs_, cs_ = shards[w].shape[1:]
        part = None
        for layer in reversed(range(depth)):
            part = _add_chips(f"reduce_{BIG[w]}{layer}", sums[w, layer], arrived[w, layer], AXES[w], layer, kc_idx,
                              sum_rows, part, (depth, 2, rs_ // 2, cs_))
        finals.append(part)
    finals = _run_rider("grads_join", _join_rider(finals))
    grad = {BIG[w]: finals[w].reshape(shards[w].shape) for w in range(4)}

    dh0 = dh0.reshape(bl, lp, d)
    grad_x = dh0[:, PAD + N_META:]
    local = {k: jnp.stack(v) for k, v in per_layer.items()}
    pieces = [dh0[:, PAD:PAD + N_META].reshape(bl * N_META, d), local["g_mix"], local["g_mlp"],
              dgf8.sum(0).reshape(1, d),
              jnp.pad(local["w_conv"].reshape(-1), (0, 2 * d - local["w_conv"].size)).reshape(2, d),
              jnp.pad(local["pool_scale"].reshape(-1), (0, d - local["pool_scale"].size)).reshape(1, d),
              jnp.pad(loss8.sum(0, keepdims=True), ((0, 7), (0, 0))), local["w_pool"].reshape(-1, d)]
    summed = _all_reduce_small("small_grads", jnp.concatenate(pieces, axis=0), N_META, bl)
    o = N_META
    grad.update({
        "meta_tokens": lax.dynamic_slice_in_dim(summed[:o], chip * meta_tokens.shape[1], meta_tokens.shape[1], 1),
        "g_mix": summed[o:o + 2], "g_mlp": summed[o + 2:o + 4], "g_final": summed[o + 4],
        "w_conv": lax.dynamic_slice_in_dim(summed[o + 5:o + 7].reshape(-1)[:2 * 3 * cw].reshape(2, 3, cw),
                                           chip * cs, cs, 2),
        "pool_scale": summed[o + 7].reshape(-1)[:pool_scale.size].reshape(pool_scale.shape),
        "w_pool": summed[o + 16:].reshape(w_pool.shape),
    })
    loss = jnp.sum(summed[o + 8])

    weights = dict(meta_tokens=meta_tokens, g_mix=g_mix, w_in=w_in, w_conv=w_conv, w_pool=w_pool,
                   pool_scale=pool_scale, w_out=w_out, g_mlp=g_mlp, w_up=w_up, w_down=w_down, g_final=g_final)
    ms = dict(meta_tokens=m_meta_tokens, g_mix=m_g_mix, w_in=m_w_in, w_conv=m_w_conv, w_pool=m_w_pool,
              pool_scale=m_pool_scale, w_out=m_w_out, g_mlp=m_g_mlp, w_up=m_w_up, w_down=m_w_down,
              g_final=m_g_final)
    vs = dict(meta_tokens=v_meta_tokens, g_mix=v_g_mix, w_in=v_w_in, w_conv=v_w_conv, w_pool=v_w_pool,
              pool_scale=v_pool_scale, w_out=v_w_out, g_mlp=v_g_mlp, w_up=v_w_up, w_down=v_w_down,
              g_final=v_g_final)
    order = list(weights)
    upd = {k: _adamw(f"adamw_{k}", weights[k], grad[k], ms[k], vs[k], copy_rows) for k in BIG}
    little = [k for k in order if k not in BIG]
    for k, res in zip(little, _adamw_small("adamw_small", [(weights[k], grad[k].reshape(weights[k].shape), ms[k],
                                                            vs[k]) for k in little])):
        upd[k] = res
    grad = {k: grad[k].reshape(weights[k].shape) for k in order}
    return (loss, grad_x, *[grad[k] for k in order], *[upd[k][0] for k in order], *[upd[k][1] for k in order],
            *[upd[k][2] for k in order])
```

```python
import functools

import jax
import jax.numpy as jnp
from jax import lax
from jax.experimental import pallas as pl
from jax.experimental.pallas import tpu as pltpu

F32, BF16 = jnp.float32, jnp.bfloat16
MESH = pl.DeviceIdType.MESH
EPS = 1e-6
N_META = 16
QB = 128
PAD = QB - N_META
HALO = 16
POOL_WINDOWS = (2.0, 4.0, 8.0, 16.0)
HEAD_SCALE = 0.125
LR, B1, B2, ADAM_EPS, WD, STEP = 0.001, 0.9, 0.999, 1e-08, 0.01, 10
VMEM_LIMIT = 56 * 1024 * 1024


def _params(sem=None):
    return pltpu.CompilerParams(dimension_semantics=sem, vmem_limit_bytes=VMEM_LIMIT)


def _nt(a, b):
    return lax.dot_general(a, b, (((1,), (1,)), ((), ())), preferred_element_type=F32)


def _tn(a, b):
    return lax.dot_general(a, b, (((0,), (0,)), ((), ())), preferred_element_type=F32)


def _nn(a, b):
    return jnp.dot(a, b, preferred_element_type=F32)


def _fold8(v):
    r, c = v.shape
    return jnp.sum(v.reshape(r // 8, 8, c), axis=0)


NCH = 512


def _rows_call(name, body, tm, row_ins, consts, row_outs, accs=(), rider=None):
    t = row_ins[0].shape[0]
    ride = _Ride(rider, len(row_ins) + len(consts), len(row_outs) + len(accs), t // tm)

    def stepped(*refs):
        step = pl.program_id(0)
        ride.before(refs, step)
        body(*ride.own(refs))
        ride.after(refs, step)

    in_specs = [pl.BlockSpec((tm, a.shape[1]), lambda i: (i, 0)) for a in row_ins]
    for a, layer in consts:
        if layer is None:
            in_specs.append(pl.BlockSpec(a.shape, lambda i: (0, 0)))
        else:
            in_specs.append(pl.BlockSpec((None, *a.shape[1:]), lambda i, l=layer: (l, 0, 0)))
    return ride.call(
        stepped, name, (t // tm,), in_specs, [*row_ins, *[a for a, _ in consts]],
        [pl.BlockSpec((tm, c), lambda i: (i, 0)) for c, _ in row_outs]
        + [pl.BlockSpec(s, lambda i: (0, 0)) for s in accs],
        [jax.ShapeDtypeStruct((t, c), dt) for c, dt in row_outs] + [jax.ShapeDtypeStruct(s, F32) for s in accs],
        [], ("arbitrary",) if accs else ("parallel",))


def _norm_parts(x):
    r = lax.rsqrt(jnp.mean(x * x, axis=-1, keepdims=True) + EPS)
    return r, x * r


def _norm_bwd(r, xh, dyn, g):
    w = dyn * g
    return r * (w - xh * jnp.mean(w * xh, axis=-1, keepdims=True))


def _in_proj(name, h, g, w, layer, tm, ncp):
    d, n = h.shape[1], w.shape[2]

    def body(h_ref, g_ref, w_ref, hn_ref, ucp_ref, qkv_ref):
        _, xh = _norm_parts(h_ref[...])
        hn = (xh * g_ref[...]).astype(BF16)
        hn_ref[...] = hn
        for n0 in range(0, n, NCH):
            acc = _nn(hn, w_ref[:, n0:n0 + NCH])
            if n0 < ncp:
                ucp_ref[:, n0:n0 + NCH] = acc
            else:
                qkv_ref[:, n0 - ncp:n0 - ncp + NCH] = acc.astype(BF16)

    return _rows_call(name, body, tm, [h], [(g.reshape(1, d), None), (w, layer)],
                      [(d, BF16), (ncp, F32), (n - ncp, BF16)])


def _out_proj(name, y_cp, y_at, h, w, layer, tm):
    d, k1 = h.shape[1], y_cp.shape[1]

    def body(ycp_ref, yat_ref, h_ref, w_ref, o_ref):
        for n0 in range(0, d, NCH):
            o_ref[:, n0:n0 + NCH] = (h_ref[:, n0:n0 + NCH] + _nn(ycp_ref[...], w_ref[0:k1, n0:n0 + NCH])
                                     + _nn(yat_ref[...], w_ref[k1:, n0:n0 + NCH]))

    return _rows_call(name, body, tm, [y_cp, y_at, h], [(w, layer)], [(d, F32)])[0]


def _up_proj(name, h_mid, g, w, layer, tm, rider=None):
    d, n = h_mid.shape[1], w.shape[2]

    def body(h_ref, g_ref, w_ref, hn_ref, m_ref, act_ref):
        _, xh = _norm_parts(h_ref[...])
        hn = (xh * g_ref[...]).astype(BF16)
        hn_ref[...] = hn
        for n0 in range(0, n, NCH):
            acc = _nn(hn, w_ref[:, n0:n0 + NCH])
            m_ref[:, n0:n0 + NCH] = acc.astype(BF16)
            act_ref[:, n0:n0 + NCH] = jnp.square(jnp.maximum(acc, 0.0)).astype(BF16)

    return _rows_call(name, body, tm, [h_mid], [(g.reshape(1, d), None), (w, layer)],
                      [(d, BF16), (n, BF16), (n, BF16)], rider=rider)


def _down_proj(name, act, h_mid, w, layer, tm, rider=None):
    d = h_mid.shape[1]

    def body(a_ref, h_ref, w_ref, o_ref):
        for n0 in range(0, d, NCH):
            o_ref[:, n0:n0 + NCH] = h_ref[:, n0:n0 + NCH] + _nn(a_ref[...], w_ref[:, n0:n0 + NCH])

    return _rows_call(name, body, tm, [act, h_mid], [(w, layer)], [(d, F32)], rider=rider)


def _down_proj_dx(name, dh, m_pre, w, layer, tm, rider=None):
    n = w.shape[1]

    def body(dh_ref, m_ref, w_ref, dm_ref):
        dhb = dh_ref[...].astype(BF16)
        for n0 in range(0, n, NCH):
            dm_ref[:, n0:n0 + NCH] = (_nt(dhb, w_ref[n0:n0 + NCH, :])
                                      * (2.0 * jnp.maximum(m_ref[:, n0:n0 + NCH].astype(F32), 0.0))).astype(BF16)

    return _rows_call(name, body, tm, [dh, m_pre], [(w, layer)], [(n, BF16)], rider=rider)


def _up_proj_dx(name, dm, h_mid, dh, g, w_up, w_out, layer, tm, rider=None):
    d = h_mid.shape[1]

    def body(dm_ref, h_ref, dh_ref, g_ref, wup_ref, wout_ref, dhm_ref, dy_ref, dg_ref):
        @pl.when(pl.program_id(0) == 0)
        def _():
            dg_ref[...] = jnp.zeros_like(dg_ref)
        dyn = _nt(dm_ref[...], wup_ref[...])
        r, xh = _norm_parts(h_ref[...])
        dhm = dh_ref[...] + _norm_bwd(r, xh, dyn, g_ref[...])
        dhm_ref[...] = dhm
        dg_ref[...] += _fold8(dyn * xh)
        dy_ref[...] = _nt(dhm.astype(BF16), wout_ref[...])

    return _rows_call(name, body, tm, [dm, h_mid, dh], [(g.reshape(1, d), None), (w_up, layer), (w_out, layer)],
                      [(d, F32), (w_out.shape[1], F32)], [(8, d)], rider)


def _in_proj_dx(name, dus, h, dh_mid, g, w, layer, tm, rider=None):
    d = h.shape[1]
    ns = [du.shape[1] for du in dus]
    nd = len(dus)

    def body(*refs):
        du_refs = refs[:nd]
        h_ref, dhm_ref, g_ref, w_ref, dh_ref, dg_ref = refs[nd:]

        @pl.when(pl.program_id(0) == 0)
        def _():
            dg_ref[...] = jnp.zeros_like(dg_ref)
        dyn, off = None, 0
        for du_ref, n in zip(du_refs, ns):
            part = _nt(du_ref[...], w_ref[:, off:off + n])
            dyn = part if dyn is None else dyn + part
            off += n
        r, xh = _norm_parts(h_ref[...])
        dh_ref[...] = dhm_ref[...] + _norm_bwd(r, xh, dyn, g_ref[...])
        dg_ref[...] += _fold8(dyn * xh)

    return _rows_call(name, body, tm, [*dus, h, dh_mid], [(g.reshape(1, d), None), (w, layer)], [(d, F32)],
                      [(8, d)], rider)


def _mm_tn(name, a, b, tt, tka, tn, into, shape, layer, row_off, col_off):
    t, ka = a.shape
    n = b.shape[1]
    assert t % tt == 0 and ka % tka == 0 and n % tn == 0 and row_off % tka == 0 and col_off % tn == 0

    def body(a_ref, b_ref, *rest):
        o_ref = rest[-1]

        @pl.when(pl.program_id(2) == 0)
        def _():
            o_ref[...] = jnp.zeros_like(o_ref)
        o_ref[...] += _tn(a_ref[...].astype(BF16), b_ref[...].astype(BF16))

    in_specs = [pl.BlockSpec((tt, tka), lambda i, j, s: (s, i)), pl.BlockSpec((tt, tn), lambda i, j, s: (s, j))]
    args = [a, b]
    if into is not None:
        in_specs.append(pl.BlockSpec(memory_space=pl.ANY))
        args.append(into)
    return pl.pallas_call(
        body, name=name, grid=(ka // tka, n // tn, t // tt), in_specs=in_specs,
        out_specs=pl.BlockSpec((None, tka, tn), lambda i, j, s: (layer, row_off // tka + i, col_off // tn + j)),
        out_shape=jax.ShapeDtypeStruct(shape, F32),
        input_output_aliases={} if into is None else {2: 0},
        compiler_params=_params(("parallel", "parallel", "arbitrary")),
    )(*args)


def _mm_tn_slab(name, a_list, b_list, tt, into, shape, layer):
    t = a_list[0].shape[0]
    kas, ns = [a.shape[1] for a in a_list], [b.shape[1] for b in b_list]
    assert t % tt == 0 and (sum(kas), sum(ns)) == tuple(shape[1:])
    na, nb = len(a_list), len(b_list)

    def body(*refs):
        o_ref = refs[-1]

        @pl.when(pl.program_id(0) == 0)
        def _():
            o_ref[...] = jnp.zeros_like(o_ref)
        r0 = 0
        for a_ref, ka in zip(refs[:na], kas):
            a = a_ref[...].astype(BF16)
            c0 = 0
            for b_ref, n in zip(refs[na:na + nb], ns):
                o_ref[r0:r0 + ka, c0:c0 + n] += _tn(a, b_ref[...].astype(BF16))
                c0 += n
            r0 += ka

    in_specs = [pl.BlockSpec((tt, c), lambda s: (s, 0)) for c in kas + ns]
    args = [*a_list, *b_list]
    if into is not None:
        in_specs.append(pl.BlockSpec(memory_space=pl.ANY))
        args.append(into)
    return pl.pallas_call(
        body, name=name, grid=(t // tt,), in_specs=in_specs,
        out_specs=pl.BlockSpec((None, *shape[1:]), lambda s: (layer, 0, 0)),
        out_shape=jax.ShapeDtypeStruct(shape, F32),
        input_output_aliases={} if into is None else {na + nb: 0},
        compiler_params=_params(("arbitrary",)),
    )(*args)


def _build_h(name, x, meta, lp, rider=None):
    bl, s, d = x.shape
    nq = lp // QB
    ride = _Ride(rider, 2, 1, bl * nq)

    def body(*refs):
        x_ref, m_ref, o_ref = ride.own(refs)
        j = pl.program_id(1)
        step = pl.program_id(0) * nq + j
        ride.before(refs, step)
        head = jnp.concatenate([jnp.zeros((PAD, d), F32), m_ref[...]], axis=0)
        o_ref[...] = jnp.where(j == 0, head, x_ref[...])
        ride.after(refs, step)

    return ride.call(
        body, name, (bl, nq),
        [pl.BlockSpec((None, QB, d), lambda b, j: (b, jnp.maximum(j - 1, 0), 0)),
         pl.BlockSpec(meta.shape, lambda b, j: (0, 0))], [x, meta],
        [pl.BlockSpec((QB, d), lambda b, j: (b * nq + j, 0))], [jax.ShapeDtypeStruct((bl * lp, d), F32)], [],
        ("parallel", "arbitrary"))


def _loss_bwd(name, h, g, target, lp):
    t, d = h.shape
    bl = target.shape[0]
    nq = lp // QB

    def body(h_ref, g_ref, t_ref, dh_ref, ls_ref, dg_ref):
        b, j = pl.program_id(0), pl.program_id(1)

        @pl.when((b == 0) & (j == 0))
        def _():
            ls_ref[...] = jnp.zeros_like(ls_ref)
            dg_ref[...] = jnp.zeros_like(dg_ref)
        xv = h_ref[...]
        r = lax.rsqrt(jnp.mean(xv * xv, axis=-1, keepdims=True) + EPS)
        xh = xv * r
        gv = g_ref[...]
        err = jnp.where(j >= 1, xh * gv - t_ref[...], 0.0)
        ls_ref[...] += _fold8(err * err) * (0.5 / d)
        dy = err * (1.0 / d)
        w = dy * gv
        dh_ref[...] = r * (w - xh * jnp.mean(w * xh, axis=-1, keepdims=True))
        dg_ref[...] += _fold8(dy * xh)

    return pl.pallas_call(
        body, name=name, grid=(bl, nq),
        in_specs=[pl.BlockSpec((QB, d), lambda b, j: (b * nq + j, 0)), pl.BlockSpec((1, d), lambda b, j: (0, 0)),
                  pl.BlockSpec((None, QB, d), lambda b, j: (b, jnp.maximum(j - 1, 0), 0))],
        out_specs=[pl.BlockSpec((QB, d), lambda b, j: (b * nq + j, 0)), pl.BlockSpec((8, d), lambda b, j: (0, 0)),
                   pl.BlockSpec((8, d), lambda b, j: (0, 0))],
        out_shape=[jax.ShapeDtypeStruct((t, d), F32), jax.ShapeDtypeStruct((8, d), F32),
                   jax.ShapeDtypeStruct((8, d), F32)],
        compiler_params=_params(("arbitrary", "arbitrary")),
    )(h, g.reshape(1, d), target)


def _pool_select(grp, a2, a4, a8, a16):
    return jnp.where(grp == 0, a2, jnp.where(grp == 1, a4, jnp.where(grp == 2, a8, a16)))


def _trailing_sums(v):
    s2 = v + pltpu.roll(v, 1, 0)
    s4 = s2 + pltpu.roll(s2, 2, 0)
    s8 = s4 + pltpu.roll(s4, 4, 0)
    s16 = s8 + pltpu.roll(s8, 8, 0)
    return s2, s4, s8, s16


def _leading_sums(v):
    n = v.shape[0]
    s2 = v + pltpu.roll(v, n - 1, 0)
    s4 = s2 + pltpu.roll(s2, n - 2, 0)
    s8 = s4 + pltpu.roll(s4, n - 4, 0)
    s16 = s8 + pltpu.roll(s8, n - 8, 0)
    return s2, s4, s8, s16


def _convpool_fwd(name, u_cp, wconv, wbd, pscale, lp, r):
    t = u_cp.shape[0]
    cw = u_cp.shape[1] // 4
    tps, hb = lp // r, r // HALO

    def body(cb_ref, cc_ref, cx_ref, pi_ref, cch_ref, cxh_ref, pih_ref, wc_ref, wbd_ref, ps_ref, y_ref):
        i = pl.program_id(0)
        lrow = (i % tps) * r + lax.broadcasted_iota(jnp.int32, (r, 1), 0)
        valid = lrow >= PAD
        xx = jnp.concatenate([cch_ref[...] * cxh_ref[...], cc_ref[...] * cx_ref[...]], axis=0)
        conv = (wc_ref[0:1, :] * pltpu.roll(xx, 2, 0) + wc_ref[1:2, :] * pltpu.roll(xx, 1, 0)
                + wc_ref[2:3, :] * xx)
        y_ref[:, 0:cw] = (cb_ref[...] * conv[HALO:]).astype(y_ref.dtype)
        p = pi_ref[...]
        grp = lax.broadcasted_iota(jnp.int32, (1, cw), 1) // (cw // 4)
        sel = _pool_select(grp, *_trailing_sums(jnp.concatenate([pih_ref[...], p], axis=0)))[HALO:]
        cnt = jnp.maximum(jnp.minimum((lrow - (PAD - 1)).astype(F32), _pool_select(grp, *POOL_WINDOWS)), 1.0)
        pooled = jnp.where(valid, sel / cnt - p, 0.0)
        y_ref[:, cw:2 * cw] = (_nn(pooled.astype(BF16), wbd_ref[...]) * ps_ref[...]).astype(y_ref.dtype)

    def main(col):
        return pl.BlockSpec((r, cw), lambda i: (i, col))

    def prev(col):
        return pl.BlockSpec((HALO, cw), lambda i: (jnp.maximum(i * hb - 1, 0), col))

    def whole(a):
        return pl.BlockSpec(a.shape, lambda i: (0, 0))

    return pl.pallas_call(
        body, name=name, grid=(t // r,),
        in_specs=[main(0), main(1), main(2), main(3), prev(1), prev(2), prev(3), whole(wconv), whole(wbd),
                  whole(pscale)],
        out_specs=pl.BlockSpec((r, 2 * cw), lambda i: (i, 0)),
        out_shape=jax.ShapeDtypeStruct((t, 2 * cw), BF16),
        compiler_params=_params(("parallel",)),
    )(u_cp, u_cp, u_cp, u_cp, u_cp, u_cp, u_cp, wconv, wbd, pscale)


def _convpool_bwd(name, u_cp, dy, wconv, wbd, pscale, lp, r):
    t = u_cp.shape[0]
    cw = u_cp.shape[1] // 4
    tps, hb = lp // r, r // HALO
    e = r + HALO

    def body(cb_ref, cc_ref, cx_ref, pi_ref, cbn_ref, cch_ref, cxh_ref, pih_ref, dyc_ref, dyp_ref, dycn_ref,
             dypn_ref, wc_ref, wbd_ref, ps_ref, du_ref, sm_ref, dwbd_ref):
        i = pl.program_id(0)

        @pl.when(i == 0)
        def _():
            sm_ref[...] = jnp.zeros_like(sm_ref)
            dwbd_ref[...] = jnp.zeros_like(dwbd_ref)
        lrow_e = (i % tps) * r + lax.broadcasted_iota(jnp.int32, (e, 1), 0)
        valid_e = (lrow_e >= PAD) & (lrow_e < lp)
        lrow, valid = lrow_e[:r], lrow_e[:r] >= PAD
        w0, w1, w2 = wc_ref[0:1, :], wc_ref[1:2, :], wc_ref[2:3, :]
        cb, cc, cx = cb_ref[...], cc_ref[...], cx_ref[...]
        prod = cc * cx
        xx = jnp.concatenate([cch_ref[...] * cxh_ref[...], prod], axis=0)
        back1, back2 = pltpu.roll(xx, 1, 0)[HALO:], pltpu.roll(xx, 2, 0)[HALO:]
        dyc = dyc_ref[...]
        du_ref[:, 0:cw] = (dyc * (w0 * back2 + w1 * back1 + w2 * prod)).astype(du_ref.dtype)
        dconv_e = jnp.where(valid_e, jnp.concatenate([dyc * cb, dycn_ref[...] * cbn_ref[...]], axis=0), 0.0)
        dconv = dconv_e[:r]
        dprod = (w2 * dconv + w1 * pltpu.roll(dconv_e, e - 1, 0)[:r] + w0 * pltpu.roll(dconv_e, e - 2, 0)[:r])
        du_ref[:, cw:2 * cw] = (dprod * cx).astype(du_ref.dtype)
        du_ref[:, 2 * cw:3 * cw] = (dprod * cc).astype(du_ref.dtype)
        sm_ref[0:8, :] += _fold8(dconv * back2)
        sm_ref[8:16, :] += _fold8(dconv * back1)
        sm_ref[16:24, :] += _fold8(dconv * prod)
        p = pi_ref[...]
        grp = lax.broadcasted_iota(jnp.int32, (1, cw), 1) // (cw // 4)
        win = _pool_select(grp, *POOL_WINDOWS)
        sel = _pool_select(grp, *_trailing_sums(jnp.concatenate([pih_ref[...], p], axis=0)))[HALO:]
        cnt_e = jnp.maximum(jnp.minimum((lrow_e - (PAD - 1)).astype(F32), win), 1.0)
        pooled = jnp.where(valid, sel / cnt_e[:r] - p, 0.0).astype(BF16)
        dyp = dyp_ref[...]
        sm_ref[24:32, :] += _fold8(dyp * _nn(pooled, wbd_ref[...]))
        dpre_e = (jnp.concatenate([dyp, dypn_ref[...]], axis=0) * ps_ref[...]).astype(BF16)
        dwbd_ref[...] += _tn(pooled, dpre_e[:r])
        dpooled_e = jnp.where(valid_e, _nt(dpre_e, wbd_ref[...]), 0.0)
        ahead = _pool_select(grp, *_leading_sums(dpooled_e / cnt_e))[:r]
        du_ref[:, 3 * cw:4 * cw] = (ahead - dpooled_e[:r]).astype(du_ref.dtype)

    last_halo = t // HALO - 1

    def main(col):
        return pl.BlockSpec((r, cw), lambda i: (i, col))

    def prev(col):
        return pl.BlockSpec((HALO, cw), lambda i: (jnp.maximum(i * hb - 1, 0), col))

    def nxt(col):
        return pl.BlockSpec((HALO, cw), lambda i: (jnp.minimum((i + 1) * hb, last_halo), col))

    def whole(a):
        return pl.BlockSpec(a.shape, lambda i: (0, 0))

    return pl.pallas_call(
        body, name=name, grid=(t // r,),
        in_specs=[main(0), main(1), main(2), main(3), nxt(0), prev(1), prev(2), prev(3), main(0), main(1), nxt(0),
                  nxt(1), whole(wconv), whole(wbd), whole(pscale)],
        out_specs=[pl.BlockSpec((r, 4 * cw), lambda i: (i, 0)), pl.BlockSpec((32, cw), lambda i: (0, 0)),
                   pl.BlockSpec((cw, cw), lambda i: (0, 0))],
        out_shape=[jax.ShapeDtypeStruct((t, 4 * cw), BF16), jax.ShapeDtypeStruct((32, cw), F32),
                   jax.ShapeDtypeStruct((cw, cw), F32)],
        compiler_params=_params(("arbitrary",)),
    )(u_cp, u_cp, u_cp, u_cp, u_cp, u_cp, u_cp, u_cp, dy, dy, dy, dy, wconv, wbd, pscale)


KW = 2 * QB
HP = 4
DECAY = 64.0


def _cumsum_matrix(before, kw):
    r = lax.broadcasted_iota(jnp.int32, (kw, kw), 0)
    c = lax.broadcasted_iota(jnp.int32, (kw, kw), 1)
    return ((r < c) if before else (r > c)).astype(BF16)


def _running(v, mat):
    m = v.shape[0]
    hi = v.astype(BF16)
    ext = _nn(jnp.concatenate([hi, (v - hi.astype(F32)).astype(BF16)], axis=0), mat)
    return ext[:m] + ext[m:]


def _log_sigmoid(z):
    neg_abs = lax.bitcast_convert_type(lax.bitcast_convert_type(z, jnp.int32) | jnp.int32(-2 ** 31), F32)
    return jnp.minimum(z, 0.0) - jnp.log(1.0 + jnp.exp(neg_abs))


def _stack_heads(v, head0):
    zero = jnp.zeros_like(v)
    return jnp.concatenate([jnp.where(head0, v, zero), jnp.where(head0, zero, v)], axis=0)


def _lanes(hp):
    return slice(hp * QB, (hp + 1) * QB)


def _key_walk(qi):
    prev = jnp.maximum(qi - 1, 0)
    return prev, (prev % 2) * QB, prev // 2, 1 - prev % 2


def _attn_fwd(name, qkv, bl, lp, rider=None):
    t = qkv.shape[0]
    nq, nblk = lp // QB, qkv.shape[1] // (3 * HP * QB)
    assert nblk == 1
    ride = _Ride(rider, 3, 3, bl * nblk * nq)

    def body(*refs):
        q_ref, k_ref, v_ref, o_ref, lt_ref, g0_ref, run_s, acc_s = ride.own(refs)
        qi = pl.program_id(2)
        step = (pl.program_id(0) * nblk + pl.program_id(1)) * nq + qi
        ride.before(refs, step)
        head0 = lax.broadcasted_iota(jnp.int32, (QB, QB), 1) < QB // 2
        q2 = [_stack_heads(q_ref[:, _lanes(hp)] * jnp.asarray(HEAD_SCALE, BF16), head0) for hp in range(HP)]
        later = {KW: _cumsum_matrix(False, KW), QB: _cumsum_matrix(False, QB)}
        q_pos = qi * QB + (lax.broadcasted_iota(jnp.int32, (2 * QB, KW), 0) & (QB - 1))
        col = lax.broadcasted_iota(jnp.int32, (2 * QB, KW), 1)
        prev, off, ngrp, lo_g = _key_walk(qi)

        def group(start, kw, masked):
            start = pl.multiple_of(start, QB)
            if masked:
                k_pos = start + col[:, :kw]
                valid = (k_pos < q_pos[:, :kw]) & (k_pos >= PAD)
            z = [_nt(q2[hp], k_ref[pl.ds(start, kw), _lanes(hp)]) for hp in range(HP)]
            logp, after, rs = [], [], []
            for hp in range(HP):
                lp_ = _log_sigmoid(z[hp])
                lk = lp_ - z[hp]
                if masked:
                    lk = jnp.where(valid, lk, 0.0)
                logp.append(lp_)
                rs.append(jnp.sum(lk, axis=1, keepdims=True))
                after.append(_running(lk, later[kw]))
            for hp in range(HP):
                run = run_s[hp]
                a = jnp.exp(logp[hp] + after[hp] + run)
                if masked:
                    a = jnp.where(valid, a, 0.0)
                run_s[hp] = run + rs[hp]
                acc_s[hp] += _nn(a.astype(BF16), v_ref[pl.ds(start, kw), _lanes(hp)])

        def alive():
            most = run_s[0]
            for hp in range(1, HP):
                most = jnp.maximum(most, run_s[hp])
            return jnp.max(most) > -DECAY

        def older(st):
            group(off + st[0] * KW, KW, False)
            return st[0] - 1, alive()

        run_s[...] = jnp.zeros_like(run_s)
        acc_s[...] = jnp.zeros_like(acc_s)
        group(prev * QB, KW, True)
        g, live = lax.while_loop(lambda st: (st[0] >= lo_g) & st[1], older, (ngrp - 1, alive()))
        bottom = (g < lo_g) & live & ((off > 0) | (ngrp >= 1))
        pl.when(bottom & (off > 0))(lambda: group(0, QB, True))
        pl.when(bottom & (off == 0))(lambda: group(0, KW, True))
        g0_ref[pl.program_id(0), qi] = jnp.where(bottom, -1, g + 1).astype(F32)
        for hp in range(HP):
            o_ref[:, _lanes(hp)] = jnp.where(head0, acc_s[hp, :QB], acc_s[hp, QB:]).astype(o_ref.dtype)
            lt_ref[:, _lanes(hp)] = jnp.where(head0, run_s[hp, :QB], run_s[hp, QB:])
        ride.after(refs, step)

    wb = HP * QB
    blk = pl.BlockSpec((QB, wb), lambda b, p, i: (b * nq + i, p))
    return ride.call(
        body, name, (bl, nblk, nq),
        [blk, pl.BlockSpec((lp, wb), lambda b, p, i: (b, nblk + p)),
         pl.BlockSpec((lp, wb), lambda b, p, i: (b, 2 * nblk + p))], [qkv, qkv, qkv],
        [blk, blk, pl.BlockSpec(memory_space=pltpu.SMEM)],
        [jax.ShapeDtypeStruct((t, nblk * wb), BF16), jax.ShapeDtypeStruct((t, nblk * wb), F32),
         jax.ShapeDtypeStruct((bl, nq), F32)],
        [pltpu.VMEM((HP, 2 * QB, 1), F32), pltpu.VMEM((HP, 2 * QB, QB), F32)])


def _attn_bwd(name, qkv, lt, g0, dy, bl, lp, rider=None):
    t = qkv.shape[0]
    nq, nblk = lp // QB, qkv.shape[1] // (3 * HP * QB)
    ride = _Ride(rider, 6, 3, bl * nblk * nq)

    def body(*refs):
        (q_ref, k_ref, v_ref, lt_ref, do_ref, g0_ref, dq_ref, dk_ref, dv_ref, dk_acc, dv_acc, seen_s, gsum_s,
         dq_s) = ride.own(refs)
        qi = pl.program_id(2)
        step = (pl.program_id(0) * nblk + pl.program_id(1)) * nq + qi
        ride.before(refs, step)

        @pl.when(qi == 0)
        def _():
            dk_acc[...] = jnp.zeros_like(dk_acc)
            dv_acc[...] = jnp.zeros_like(dv_acc)
        lane = lax.broadcasted_iota(jnp.int32, (QB, QB), 1)
        head0 = lane < QB // 2
        q2, do2, total = [], [], []
        for hp in range(HP):
            q2.append(_stack_heads(q_ref[:, _lanes(hp)] * jnp.asarray(HEAD_SCALE, BF16), head0))
            do2.append(_stack_heads(do_ref[:, _lanes(hp)].astype(BF16), head0))
            ltv = lt_ref[:, _lanes(hp)]
            total.append(jnp.concatenate(
                [jnp.sum(jnp.where(lane == 0, ltv, 0.0), axis=1, keepdims=True),
                 jnp.sum(jnp.where(lane == QB // 2, ltv, 0.0), axis=1, keepdims=True)], axis=0))
        later = {KW: _cumsum_matrix(False, KW), QB: _cumsum_matrix(False, QB)}
        earlier = {KW: _cumsum_matrix(True, KW), QB: _cumsum_matrix(True, QB)}
        q_pos = qi * QB + (lax.broadcasted_iota(jnp.int32, (2 * QB, KW), 0) & (QB - 1))
        col = lax.broadcasted_iota(jnp.int32, (2 * QB, KW), 1)
        prev, off, ngrp, lo_g = _key_walk(qi)

        def group(start, kw, masked):
            start = pl.multiple_of(start, QB)
            if masked:
                k_pos = start + col[:, :kw]
                valid = (k_pos < q_pos[:, :kw]) & (k_pos >= PAD)
            hps = range(HP)
            kg = [k_ref[pl.ds(start, kw), _lanes(hp)] for hp in hps]
            z = [_nt(q2[hp], kg[hp]) for hp in hps]
            da = [_nt(do2[hp], v_ref[pl.ds(start, kw), _lanes(hp)]) for hp in hps]
            logp, sig, after, rs = [], [], [], []
            for hp in hps:
                lp_ = _log_sigmoid(z[hp])
                lk = lp_ - z[hp]
                if masked:
                    lk = jnp.where(valid, lk, 0.0)
                logp.append(lp_)
                sig.append(jnp.exp(lp_))
                rs.append(jnp.sum(lk, axis=1, keepdims=True))
                after.append(_running(lk, later[kw]))
            a, gg, before = [], [], []
            for hp in hps:
                a_ = jnp.exp(logp[hp] + after[hp] + (total[hp] - seen_s[hp] - rs[hp]))
                if masked:
                    a_ = jnp.where(valid, a_, 0.0)
                a.append(a_.astype(BF16))
                gg.append(a_ * da[hp])
                before.append(_nn(gg[hp].astype(BF16), earlier[kw]))
            for hp in hps:
                gsum = gsum_s[hp]
                dz = gg[hp] - (gg[hp] + before[hp] + gsum) * sig[hp]
                if masked:
                    dz = jnp.where(valid, dz, 0.0)
                dz = dz.astype(BF16)
                dk_acc[pl.ds(start, kw), _lanes(hp)] += _tn(dz, q2[hp])
                dv_acc[pl.ds(start, kw), _lanes(hp)] += _tn(a[hp], do2[hp])
                seen_s[hp] += rs[hp]
                gsum_s[hp] = gsum + jnp.sum(gg[hp], axis=1, keepdims=True)
                dq_s[hp] += _nn(dz, kg[hp])

        def inner(g, _):
            group(off + g * KW, KW, False)
            return 0

        seen_s[...] = jnp.zeros_like(seen_s)
        gsum_s[...] = jnp.zeros_like(gsum_s)
        dq_s[...] = jnp.zeros_like(dq_s)
        first = g0_ref[pl.program_id(0), qi].astype(jnp.int32)
        pl.when((first < 0) & (off > 0))(lambda: group(0, QB, True))
        pl.when((first < 0) & (off == 0))(lambda: group(0, KW, True))
        lax.fori_loop(jnp.where(first < 0, lo_g, first), ngrp, inner, 0)
        group(prev * QB, KW, True)
        for hp in range(HP):
            dq_ref[:, _lanes(hp)] = (jnp.where(head0, dq_s[hp, :QB], dq_s[hp, QB:])
                                     * HEAD_SCALE).astype(dq_ref.dtype)

        @pl.when(qi == nq - 1)
        def _():
            dk_ref[...] = dk_acc[...].astype(dk_ref.dtype)
            dv_ref[...] = dv_acc[...].astype(dv_ref.dtype)
        ride.after(refs, step)

    wb = HP * QB
    blk = pl.BlockSpec((QB, wb), lambda b, p, i: (b * nq + i, p))
    seq = pl.BlockSpec((lp, wb), lambda b, p, i: (b, p))
    out = jax.ShapeDtypeStruct((t, nblk * wb), BF16)
    return ride.call(
        body, name, (bl, nblk, nq),
        [blk, pl.BlockSpec((lp, wb), lambda b, p, i: (b, nblk + p)),
         pl.BlockSpec((lp, wb), lambda b, p, i: (b, 2 * nblk + p)), blk,
         pl.BlockSpec((QB, wb), lambda b, p, i: (b * nq + i, nblk + p)), pl.BlockSpec(memory_space=pltpu.SMEM)],
        [qkv, qkv, qkv, lt, dy, g0],
        [blk, seq, seq], [out, out, out],
        [pltpu.VMEM((lp, wb), F32), pltpu.VMEM((lp, wb), F32), pltpu.VMEM((HP, 2 * QB, 1), F32),
         pltpu.VMEM((HP, 2 * QB, 1), F32), pltpu.VMEM((HP, 2 * QB, QB), F32)])


def _place():
    return lax.axis_index("x"), lax.axis_index("y"), lax.axis_index("c")


def _peers(chip):
    kx, ky = chip // 2, chip % 2
    return ((1 - kx, ky), (kx, 1 - ky), (1 - kx, 1 - ky))


def _hbm_specs(n):
    return [pl.BlockSpec(memory_space=pl.ANY) for _ in range(n)]


def _remote(src, dst, send, recv, k, to):
    return pltpu.make_async_remote_copy(src, dst, send.at[k], recv.at[k], device_id=to, device_id_type=MESH)


class _Rider:
    def __init__(self, ins, out_shapes, aliases, nsem, first, mid=None, last=None):
        self.ins, self.out_shapes, self.aliases, self.nsem = list(ins), list(out_shapes), dict(aliases), nsem
        self.first, self.mid, self.last = first, mid, last


def _by_chip(fn):
    def run(ins, outs, send, recv):
        x, y, c = _place()
        for me in range(4):
            pl.when(2 * x + y == me)(functools.partial(fn, ins, outs, send, recv, me, c, (x, y, 1 - c)))
    return run


def _run_rider(name, rider):
    ni, no = len(rider.ins), len(rider.out_shapes)

    def body(*refs):
        args = (refs[:ni], refs[ni:ni + no], refs[ni + no], refs[ni + no + 1])
        for hook in (rider.first, rider.mid, rider.last):
            if hook is not None:
                hook(*args)

    return pl.pallas_call(
        body, name=name, in_specs=_hbm_specs(ni), out_specs=_hbm_specs(no), out_shape=rider.out_shapes,
        input_output_aliases=rider.aliases,
        scratch_shapes=[pltpu.SemaphoreType.DMA((rider.nsem,)), pltpu.SemaphoreType.DMA((rider.nsem,))],
        compiler_params=pltpu.CompilerParams(has_side_effects=True),
    )(*rider.ins)


class _Ride:
    def __init__(self, rider, n_in, n_out, steps):
        self.rider, self.n_in, self.n_out, self.steps = rider, n_in, n_out, steps
        self.ri = len(rider.ins) if rider else 0
        self.ro = len(rider.out_shapes) if rider else 0

    def own(self, refs):
        refs = list(refs)
        a, b = self.n_in, self.n_in + self.ri + self.n_out
        tail = refs[b + self.ro:len(refs) - 2] if self.rider else refs[b + self.ro:]
        return refs[:a] + refs[a + self.ri:b] + tail

    def _args(self, refs):
        a, b = self.n_in, self.n_in + self.ri + self.n_out
        return refs[a:a + self.ri], refs[b:b + self.ro], refs[-2], refs[-1]

    def before(self, refs, step):
        if self.rider is None:
            return
        pl.when(step == 0)(functools.partial(self.rider.first, *self._args(refs)))
        if self.rider.mid is not None:
            pl.when(step == (7 * self.steps) // 8)(functools.partial(self.rider.mid, *self._args(refs)))

    def after(self, refs, step):
        if self.rider is not None and self.rider.last is not None:
            pl.when(step == self.steps - 1)(functools.partial(self.rider.last, *self._args(refs)))

    def call(self, body, name, grid, in_specs, args, out_specs, out_shape, scratch,
             sem=("parallel", "parallel", "arbitrary")):
        r = self.rider
        if r is None:
            return pl.pallas_call(body, name=name, grid=grid, in_specs=in_specs, out_specs=out_specs,
                                  out_shape=out_shape, scratch_shapes=scratch, compiler_params=_params(sem))(*args)
        return pl.pallas_call(
            body, name=name, grid=grid, in_specs=in_specs + _hbm_specs(self.ri),
            out_specs=out_specs + _hbm_specs(self.ro), out_shape=out_shape + r.out_shapes,
            input_output_aliases={self.n_in + i: self.n_out + o for i, o in r.aliases.items()},
            scratch_shapes=scratch + [pltpu.SemaphoreType.DMA((r.nsem,)), pltpu.SemaphoreType.DMA((r.nsem,))],
            compiler_params=pltpu.CompilerParams(dimension_semantics=("arbitrary",) * len(grid),
                                                 vmem_limit_bytes=VMEM_LIMIT, has_side_effects=True),
        )(*args, *r.ins)


def _core_view(a, axis):
    l, r, c = a.shape
    return a.reshape(l, 4, 2, r // 8, c) if axis == 0 else a.reshape(l, 2, r // 2, c)


def _shard_view(a):
    l, r, c = a.shape
    return a.reshape(l, 2, r // 2, c)


def _piece(ref, axis, layer, chip, core):
    if axis == 0:
        return ref.at[layer, chip, core]
    cs = ref.shape[-1] // 4
    return ref.at[layer, core, :, pl.ds(chip * cs, cs)]


def _place_shard(name, w, axis, kidx, tr):
    _, r, cdim = w.shape
    shp = [2, r, cdim]
    shp[1 + axis] *= 4
    nb = r // tr

    def body(k_ref, w_ref, o_ref):
        o_ref[...] = w_ref[...].astype(o_ref.dtype)

    if axis == 0:
        out_spec = pl.BlockSpec((None, tr, cdim), lambda l, i, k_ref: (l, k_ref[0] * nb + i, 0))
    else:
        out_spec = pl.BlockSpec((None, tr, cdim), lambda l, i, k_ref: (l, i, k_ref[0]))
    return pl.pallas_call(
        body, name=name,
        grid_spec=pltpu.PrefetchScalarGridSpec(
            num_scalar_prefetch=1, grid=(2, nb),
            in_specs=[pl.BlockSpec((None, tr, cdim), lambda l, i, k_ref: (l, i, 0))], out_specs=out_spec),
        out_shape=jax.ShapeDtypeStruct(tuple(shp), BF16),
        compiler_params=_params(("arbitrary", "arbitrary")),
    )(kidx, w)


def _gather_rider(views, axes, items):
    n = len(items)

    def first(ins, outs, send, recv, me, c, sib):
        for i, (w, l) in enumerate(items):
            for j, (px, py) in enumerate(_peers(me)):
                _remote(_piece(ins[w], axes[w], l, me, c), _piece(outs[w], axes[w], l, me, c), send, recv,
                        3 * i + j, (px, py, c)).start()

    def mid(ins, outs, send, recv, me, c, sib):
        for i, (w, l) in enumerate(items):
            for j, (px, py) in enumerate(_peers(me)):
                got = _piece(outs[w], axes[w], l, 2 * px + py, c)
                _remote(got, got, send, recv, 3 * i + j, (px, py, c)).wait_recv()
                _remote(got, got, send, recv, 3 * (n + i) + j, sib).start()

    def last(ins, outs, send, recv, me, c, sib):
        for i, (w, l) in enumerate(items):
            for j, (px, py) in enumerate(_peers(me)):
                mine, got = _piece(outs[w], axes[w], l, me, c), _piece(outs[w], axes[w], l, 2 * px + py, c)
                theirs = _piece(outs[w], axes[w], l, 2 * px + py, 1 - c)
                _remote(theirs, theirs, send, recv, 3 * (n + i) + j, sib).wait_recv()
                _remote(mine, mine, send, recv, 3 * i + j, (px, py, c)).wait_send()
                _remote(got, got, send, recv, 3 * (n + i) + j, sib).wait_send()

    return _Rider(views, [jax.ShapeDtypeStruct(v.shape, v.dtype) for v in views], {w: w for w in range(len(views))},
                  6 * n, _by_chip(first), _by_chip(mid), _by_chip(last))


def _swap_rider(views, axes, items):
    nv = len(views)

    def part(ref, w, l, core):
        return ref.at[l, :, core] if axes[w] == 0 else ref.at[l, core]

    def copies(ins, outs, send, recv):
        x, y, c = _place()
        return [_remote(part(ins[w], w, l, 1 - c), outs[nv + i], send, recv, i, (x, y, 1 - c))
                for i, (w, l) in enumerate(items)]

    def first(ins, outs, send, recv):
        for cp in copies(ins, outs, send, recv):
            cp.start()

    def last(ins, outs, send, recv):
        for cp in copies(ins, outs, send, recv):
            cp.wait()

    got = [jax.ShapeDtypeStruct(views[w].shape[1:2] + views[w].shape[3:] if axes[w] == 0 else views[w].shape[2:],
                                views[w].dtype) for w, _ in items]
    return _Rider(views, [jax.ShapeDtypeStruct(v.shape, v.dtype) for v in views] + got,
                  {w: w for w in range(nv)}, len(items), first, None, last)


def _add_core(name, view, got, axis, layer, cidx, tr):
    def body(c_ref, g_ref, r_ref, o_ref):
        o_ref[...] = (g_ref[...] + r_ref[...]).astype(o_ref.dtype)

    if axis == 0:
        _, nchip, _, pr, cdim = view.shape
        grid = (nchip, pr // tr)
        specs = [pl.BlockSpec((None, None, None, tr, cdim), lambda k, i, c_ref: (layer, k, c_ref[0], i, 0)),
                 pl.BlockSpec((None, tr, cdim), lambda k, i, c_ref: (k, i, 0))]
        out_spec = pl.BlockSpec((None, tr, cdim), lambda k, i, c_ref: (k, i, 0))
    else:
        _, _, pr, cdim = view.shape
        grid = (pr // tr,)
        specs = [pl.BlockSpec((None, None, tr, cdim), lambda i, c_ref: (layer, c_ref[0], i, 0)),
                 pl.BlockSpec((tr, cdim), lambda i, c_ref: (i, 0))]
        out_spec = pl.BlockSpec((tr, cdim), lambda i, c_ref: (i, 0))
    return pl.pallas_call(
        body, name=name,
        grid_spec=pltpu.PrefetchScalarGridSpec(num_scalar_prefetch=1, grid=grid, in_specs=specs,
                                               out_specs=out_spec),
        out_shape=jax.ShapeDtypeStruct(got.shape, BF16),
        compiler_params=_params(("arbitrary",) * len(grid)),
    )(cidx, view, got)


def _scatter_rider(sums, axes):
    def part(ref, i, chip):
        if axes[i] == 0:
            return ref.at[chip]
        cs = ref.shape[-1] // 4
        return ref.at[:, pl.ds(chip * cs, cs)]

    def copies(ins, outs, send, recv, me, c, sib):
        return [_remote(part(ins[i], i, 2 * px + py), outs[i].at[j], send, recv, 3 * i + j, (px, py, c))
                for i in range(len(sums)) for j, (px, py) in enumerate(_peers(me))]

    def first(*args):
        for cp in copies(*args):
            cp.start()

    def last(*args):
        for cp in copies(*args):
            cp.wait()

    shapes = [jax.ShapeDtypeStruct((3,) + (s.shape[1:] if ax == 0 else (s.shape[0], s.shape[1] // 4)), s.dtype)
              for s, ax in zip(sums, axes)]
    return _Rider(sums, shapes, {}, 3 * len(sums), _by_chip(first), None, _by_chip(last))


def _add_chips(name, own, got, axis, layer, kc_idx, tr, into, shard_shape):
    _, pr, pc = got.shape

    def body(k_ref, o_ref, g_ref, *rest):
        rest[-1][...] = (o_ref[...].astype(F32) + g_ref[0].astype(F32) + g_ref[1].astype(F32)
                         + g_ref[2].astype(F32))

    if axis == 0:
        own_spec = pl.BlockSpec((None, tr, pc), lambda i, k_ref: (k_ref[0], i, 0))
    else:
        own_spec = pl.BlockSpec((tr, pc), lambda i, k_ref: (i, k_ref[0]))
    specs = [own_spec, pl.BlockSpec((3, tr, pc), lambda i, k_ref: (0, i, 0))]
    args = [kc_idx, own, got]
    if into is not None:
        specs.append(pl.BlockSpec(memory_space=pl.ANY))
        args.append(into)
    return pl.pallas_call(
        body, name=name,
        grid_spec=pltpu.PrefetchScalarGridSpec(
            num_scalar_prefetch=1, grid=(pr // tr,), in_specs=specs,
            out_specs=pl.BlockSpec((None, None, tr, pc), lambda i, k_ref: (layer, k_ref[1], i, 0))),
        out_shape=jax.ShapeDtypeStruct(shard_shape, F32),
        input_output_aliases={} if into is None else {3: 0},
        compiler_params=_params(("arbitrary",)),
    )(*args)


def _join_rider(parts):
    def first(ins, outs, send, recv):
        x, y, c = _place()
        for w in range(len(parts)):
            _remote(ins[w].at[:, c], outs[w].at[:, c], send, recv, w, (x, y, 1 - c)).start()

    def last(ins, outs, send, recv):
        x, y, c = _place()
        for w in range(len(parts)):
            _remote(ins[w].at[:, c], outs[w].at[:, c], send, recv, w, (x, y, 1 - c)).wait_send()
            _remote(ins[w].at[:, c], outs[w].at[:, 1 - c], send, recv, w, (x, y, 1 - c)).wait_recv()

    return _Rider(parts, [jax.ShapeDtypeStruct(p.shape, p.dtype) for p in parts],
                  {w: w for w in range(len(parts))}, len(parts), first, None, last)


def _all_reduce_small(name, pack, lead, groups):
    nr, d = pack.shape
    nout = nr - (groups - 1) * lead

    def body(in_ref, sum_ref, mine, slots, send, recv):
        x, y, c = _place()
        me = 4 * x + 2 * y + c
        fold = in_ref[0:lead]
        for grp in range(1, groups):
            fold = fold + in_ref[grp * lead:(grp + 1) * lead]
        mine[0:lead] = fold
        mine[lead:] = in_ref[groups * lead:]
        slots[me] = mine[...]
        cps = []
        for r in range(1, 8):
            rx, ry, rc = r // 4, (r // 2) % 2, r % 2
            peer = (x + rx - 2 * x * rx, y + ry - 2 * y * ry, c + rc - 2 * c * rc)
            cp = pltpu.make_async_remote_copy(mine, slots.at[me], send.at[r - 1], recv.at[r - 1],
                                              device_id=peer, device_id_type=MESH)
            cp.start()
            cps.append(cp)
        for cp in cps:
            cp.wait()
        acc = slots[0]
        for dev in range(1, 8):
            acc = acc + slots[dev]
        sum_ref[...] = acc

    vmem = pl.BlockSpec(memory_space=pltpu.VMEM)
    return pl.pallas_call(
        body, name=name, in_specs=[vmem], out_specs=vmem, out_shape=jax.ShapeDtypeStruct((nout, d), F32),
        scratch_shapes=[pltpu.VMEM((nout, d), F32), pltpu.VMEM((8, nout, d), F32), pltpu.SemaphoreType.DMA((7,)),
                        pltpu.SemaphoreType.DMA((7,))],
        compiler_params=pltpu.CompilerParams(has_side_effects=True, vmem_limit_bytes=VMEM_LIMIT),
    )(pack)


def _adamw_math(w, g, m, v):
    m = B1 * m + (1.0 - B1) * g
    v = B2 * v + (1.0 - B2) * (g * g)
    m_hat = m / (1.0 - B1 ** STEP)
    v_hat = v / (1.0 - B2 ** STEP)
    return -LR * (m_hat / (jnp.sqrt(v_hat) + ADAM_EPS) + WD * w), m, v


def _adamw(name, w, g, m, v, tr):
    shape = w.shape
    flat = [a.reshape(-1, shape[-1]) for a in (w, g, m, v)]
    r, cdim = flat[0].shape

    def body(w_ref, g_ref, m_ref, v_ref, d_ref, nm_ref, nv_ref):
        d_ref[...], nm_ref[...], nv_ref[...] = _adamw_math(w_ref[...], g_ref[...], m_ref[...], v_ref[...])

    spec = pl.BlockSpec((tr, cdim), lambda i: (i, 0))
    outs = pl.pallas_call(
        body, name=name, grid=(r // tr,), in_specs=[spec] * 4, out_specs=[spec] * 3,
        out_shape=[jax.ShapeDtypeStruct((r, cdim), F32)] * 3,
        compiler_params=_params(("parallel",)),
    )(*flat)
    return [o.reshape(shape) for o in outs]


def _adamw_small(name, groups):
    n = len(groups)
    shapes = [grp[0].shape for grp in groups]
    flat = [a.reshape(-1, a.shape[-1]) for grp in groups for a in grp]

    def body(*refs):
        ins, outs = refs[:4 * n], refs[4 * n:]
        for i in range(n):
            w_ref, g_ref, m_ref, v_ref = ins[4 * i:4 * i + 4]
            outs[3 * i][...], outs[3 * i + 1][...], outs[3 * i + 2][...] = _adamw_math(
                w_ref[...], g_ref[...], m_ref[...], v_ref[...])

    vmem = pl.BlockSpec(memory_space=pltpu.VMEM)
    out_shape = [jax.ShapeDtypeStruct(flat[4 * i].shape, F32) for i in range(n) for _ in range(3)]
    outs = pl.pallas_call(body, name=name, in_specs=[vmem] * (4 * n), out_specs=[vmem] * (3 * n),
                          out_shape=out_shape)(*flat)
    return [[outs[3 * i + j].reshape(shapes[i]) for j in range(3)] for i in range(n)]


def _block_diag(w_grp):
    g, pg, _ = w_grp.shape
    eye = jnp.eye(g, dtype=w_grp.dtype)
    return (eye[:, None, :, None] * w_grp[:, :, None, :]).reshape(g * pg, g * pg)


def _diag_blocks(m, g):
    pg = m.shape[0] // g
    return jnp.stack([m[i * pg:(i + 1) * pg, i * pg:(i + 1) * pg] for i in range(g)])


BIG = ("w_in", "w_out", "w_up", "w_down")
AXES = (1, 0, 1, 0)
W_IN, W_OUT, W_UP, W_DOWN = range(4)


def kernel(x, meta_tokens, g_mix, w_in, w_conv, w_pool, pool_scale, w_out, g_mlp, w_up, w_down, g_final, loss_target, m_meta_tokens, m_g_mix, m_w_in, m_w_conv, m_w_pool, m_pool_scale, m_w_out, m_g_mlp, m_w_up, m_w_down, m_g_final, v_meta_tokens, v_g_mix, v_w_in, v_w_conv, v_w_pool, v_pool_scale, v_w_out, v_g_mlp, v_w_up, v_w_down, v_g_final):
    bl, s, d = x.shape
    depth = g_mix.shape[0]
    assert depth == 2
    lp = PAD + N_META + s
    tt = lp
    tm = lp // 4
    copy_rows, sum_rows, dw_tile = 256, 128, 1024
    cs = w_conv.shape[2]
    cw = 4 * cs
    ngrp = w_pool.shape[1]
    xi, yi, ci = _place()
    chip = (2 * xi + yi).astype(jnp.int32)
    cidx, kidx = ci.astype(jnp.int32).reshape(1), chip.reshape(1)
    kc_idx = jnp.stack([chip, ci.astype(jnp.int32)])
    shards = (w_in, w_out, w_up, w_down)

    views = [_core_view(_place_shard(f"place_{BIG[w]}", shards[w], AXES[w], kidx, copy_rows), AXES[w])
             for w in range(4)]

    def whole(w):
        return views[w].reshape(depth, -1, views[w].shape[-1])

    def gather_on(call, items):
        ws = sorted({w for w, _ in items})
        res = call(_gather_rider([views[w] for w in ws], [AXES[w] for w in ws],
                                 [(ws.index(w), layer) for w, layer in items]))
        for j, w in enumerate(ws):
            views[w] = res[len(res) - len(ws) + j]
        return res[:len(res) - len(ws)]

    placed = jnp.zeros((32, d), F32)
    placed = lax.dynamic_update_slice(placed, meta_tokens, (0, chip * meta_tokens.shape[1]))
    placed = lax.dynamic_update_slice(placed, w_conv.reshape(-1, cs), (N_META, chip * cs))
    placed = jnp.where(ci == 0, placed, 0.0)
    small = _all_reduce_small("gather_small", placed, 8, 1)
    meta_full = small[:N_META]
    conv_full = small[N_META:N_META + depth * 3, :cw].reshape(depth, 3, cw)

    (h,) = gather_on(lambda rider: _build_h("build_h", x, meta_full, lp, rider), [(W_IN, 0)])
    wbd = [_block_diag(w_pool[i]).astype(BF16) for i in range(depth)]
    saved = []
    for i in range(depth):
        hn, u_cp, qkv = _in_proj(f"in_proj{i}", h, g_mix[i], whole(W_IN), i, tm, 4 * cw)
        y_cp = _convpool_fwd(f"convpool{i}", u_cp, conv_full[i], wbd[i], pool_scale[i:i + 1], lp, tm)
        if i == 0:
            y_at, lt, g0 = gather_on(lambda rider: _attn_fwd(f"attn{i}", qkv, bl, lp, rider),
                                     [(W_OUT, 0), (W_UP, 0), (W_DOWN, 0)])
            h_mid = _out_proj(f"out_proj{i}", y_cp, y_at, h, whole(W_OUT), i, tm)
            w_up0 = whole(W_UP)
            hn2, m_pre, act = gather_on(lambda rider: _up_proj(f"up_proj{i}", h_mid, g_mlp[i], w_up0, i, tm, rider),
                                        [(W_OUT, 1), (W_DOWN, 1)])
            w_down0 = whole(W_DOWN)
            (h_next,) = gather_on(lambda rider: _down_proj(f"down_proj{i}", act, h_mid, w_down0, i, tm, rider),
                                  [(W_IN, 1), (W_UP, 1)])
        else:
            y_at, lt, g0 = _attn_fwd(f"attn{i}", qkv, bl, lp)
            h_mid = _out_proj(f"out_proj{i}", y_cp, y_at, h, whole(W_OUT), i, tm)
            hn2, m_pre, act = _up_proj(f"up_proj{i}", h_mid, g_mlp[i], whole(W_UP), i, tm)
            (h_next,) = _down_proj(f"down_proj{i}", act, h_mid, whole(W_DOWN), i, tm)
        saved.append((h, hn, u_cp, qkv, y_cp, y_at, (lt, g0), h_mid, hn2, m_pre, act))
        h = h_next

    dh, loss8, dgf8 = _loss_bwd("loss", h, g_final, loss_target, lp)
    per_layer = {k: [None] * depth for k in ("g_mix", "w_conv", "w_pool", "pool_scale", "g_mlp")}

    gw = [None] * 4
    sums, arrived = {}, {}

    def dw(name, a, b, w, layer):
        shape = whole(w).shape
        into = None if gw[w] is None else gw[w].reshape(shape)
        if isinstance(a, list):
            res = _mm_tn_slab(name, a, b, tt // 2, into, shape, layer)
        else:
            res = _mm_tn(name, a, b, tt, dw_tile, dw_tile, into, shape, layer, 0, 0)
        gw[w] = _core_view(res, AXES[w])

    def swap_rider(ws):
        return _swap_rider([gw[w] for w, _ in ws], [AXES[w] for w, _ in ws],
                           [(j, layer) for j, (_, layer) in enumerate(ws)])

    def swapped(ws, outs):
        for j, (w, layer) in enumerate(ws):
            gw[w] = outs[j]
            sums[w, layer] = _add_core(f"chip_sum_{BIG[w]}{layer}", gw[w], outs[len(ws) + j], AXES[w], layer, cidx,
                                       sum_rows)

    def scatter_rider(items):
        return _scatter_rider([sums[it] for it in items], [AXES[w] for w, _ in items])

    def bwd_mlp(i, dh, swap_early, scatter_early=()):
        _, _, _, _, y_cp, y_at, _, h_mid, hn2, m_pre, act = saved[i]
        items = list(scatter_early)
        dm, *outs = _down_proj_dx(f"down_proj_dx{i}", dh, m_pre, whole(W_DOWN), i, tm,
                                  scatter_rider(items) if items else None)
        arrived.update(zip(items, outs))
        dw(f"down_proj_dw{i}", act, dh, W_DOWN, i)
        dw(f"up_proj_dw{i}", hn2, dm, W_UP, i)
        ws = [(W_DOWN, i), (W_UP, i)] if swap_early else []
        dh_mid, dy, dg8, *outs = _up_proj_dx(f"up_proj_dx{i}", dm, h_mid, dh, g_mlp[i], whole(W_UP), whole(W_OUT), i,
                                             tm, swap_rider(ws) if ws else None)
        swapped(ws, outs)
        per_layer["g_mlp"][i] = dg8.sum(0)
        dw(f"out_proj_dw{i}", [y_cp, y_at], [dh_mid], W_OUT, i)
        return dh_mid, dy

    def bwd_mix(i, dh_mid, dy, dus3, scatter_late):
        h_in, hn, u_cp = saved[i][:3]
        du_cp, sm, dwbd = _convpool_bwd(f"convpool_bwd{i}", u_cp, dy, conv_full[i], wbd[i], pool_scale[i:i + 1], lp,
                                        tm)
        sm = sm.reshape(4, 8, cw).sum(1)
        per_layer["w_conv"][i] = sm[0:3]
        per_layer["pool_scale"][i] = sm[3]
        per_layer["w_pool"][i] = _diag_blocks(dwbd, ngrp)
        dus = [du_cp, *dus3]
        dw(f"in_proj_dw{i}", [hn], dus, W_IN, i)
        rider = None
        if scatter_late:
            swapped([(W_IN, i)], _run_rider(f"grads_swap_in{i}", swap_rider([(W_IN, i)])))
            rider = scatter_rider([(W_IN, i)])
        dh, dg8, *outs = _in_proj_dx(f"in_proj_dx{i}", dus, h_in, dh_mid, g_mix[i], whole(W_IN), i, tm, rider)
        arrived.update(zip([(W_IN, i)], outs))
        per_layer["g_mix"][i] = dg8.sum(0)
        return dh

    def attn_bwd(i, dy, rider):
        qkv, (lt, g0) = saved[i][3], saved[i][6]
        res = _attn_bwd(f"attn_bwd{i}", qkv, lt, g0, dy, bl, lp, rider)
        return res[:3], res[3:]

    dh_mid, dy = bwd_mlp(1, dh, False)
    ws = [(W_DOWN, 1), (W_UP, 1), (W_OUT, 1)]
    dus3, outs = attn_bwd(1, dy, swap_rider(ws))
    swapped(ws, outs)
    dh = bwd_mix(1, dh_mid, dy, dus3, False)

    dh_mid, dy = bwd_mlp(0, dh, True, [(W_DOWN, 1), (W_OUT, 1)])
    ws = [(W_IN, 1), (W_OUT, 0)]
    swapped(ws, _run_rider("grads_swap0", swap_rider(ws)))
    items = [it for it in sums if it not in arrived]
    dus3, outs = attn_bwd(0, dy, scatter_rider(items))
    arrived.update(zip(items, outs))
    dh0 = bwd_mix(0, dh_mid, dy, dus3, True)

    finals = []
    for w in range(4):
        rs_, cs_ = shards[w].shape[1:]
        part = None
        for layer in reversed(range(depth)):
            part = _add_chips(f"reduce_{BIG[w]}{layer}", sums[w, layer], arrived[w, layer], AXES[w], layer, kc_idx,
                              sum_rows, part, (depth, 2, rs_ // 2, cs_))
        finals.append(part)
    finals = _run_rider("grads_join", _join_rider(finals))
    grad = {BIG[w]: finals[w].reshape(shards[w].shape) for w in range(4)}

    dh0 = dh0.reshape(bl, lp, d)
    grad_x = dh0[:, PAD + N_META:]
    local = {k: jnp.stack(v) for k, v in per_layer.items()}
    pieces = [dh0[:, PAD:PAD + N_META].reshape(bl * N_META, d), local["g_mix"], local["g_mlp"],
              dgf8.sum(0).reshape(1, d),
              jnp.pad(local["w_conv"].reshape(-1), (0, 2 * d - local["w_conv"].size)).reshape(2, d),
              jnp.pad(local["pool_scale"].reshape(-1), (0, d - local["pool_scale"].size)).reshape(1, d),
              jnp.pad(loss8.sum(0, keepdims=True), ((0, 7), (0, 0))), local["w_pool"].reshape(-1, d)]
    summed = _all_reduce_small("small_grads", jnp.concatenate(pieces, axis=0), N_META, bl)
    o = N_META
    grad.update({
        "meta_tokens": lax.dynamic_slice_in_dim(summed[:o], chip * meta_tokens.shape[1], meta_tokens.shape[1], 1),
        "g_mix": summed[o:o + 2], "g_mlp": summed[o + 2:o + 4], "g_final": summed[o + 4],
        "w_conv": lax.dynamic_slice_in_dim(summed[o + 5:o + 7].reshape(-1)[:2 * 3 * cw].reshape(2, 3, cw),
                                           chip * cs, cs, 2),
        "pool_scale": summed[o + 7].reshape(-1)[:pool_scale.size].reshape(pool_scale.shape),
        "w_pool": summed[o + 16:].reshape(w_pool.shape),
    })
    loss = jnp.sum(summed[o + 8])

    weights = dict(meta_tokens=meta_tokens, g_mix=g_mix, w_in=w_in, w_conv=w_conv, w_pool=w_pool,
                   pool_scale=pool_scale, w_out=w_out, g_mlp=g_mlp, w_up=w_up, w_down=w_down, g_final=g_final)
    ms = dict(meta_tokens=m_meta_tokens, g_mix=m_g_mix, w_in=m_w_in, w_conv=m_w_conv, w_pool=m_w_pool,
              pool_scale=m_pool_scale, w_out=m_w_out, g_mlp=m_g_mlp, w_up=m_w_up, w_down=m_w_down,
              g_final=m_g_final)
    vs = dict(meta_tokens=v_meta_tokens, g_mix=v_g_mix, w_in=v_w_in, w_conv=v_w_conv, w_pool=v_w_pool,
              pool_scale=v_pool_scale, w_out=v_w_out, g_mlp=v_g_mlp, w_up=v_w_up, w_down=v_w_down,
              g_final=v_g_final)
    order = list(weights)
    upd = {k: _adamw(f"adamw_{k}", weights[k], grad[k], ms[k], vs[k], copy_rows) for k in BIG}
    little = [k for k in order if k not in BIG]
    for k, res in zip(little, _adamw_small("adamw_small", [(weights[k], grad[k].reshape(weights[k].shape), ms[k],
                                                            vs[k]) for k in little])):
        upd[k] = res
    grad = {k: grad[k].reshape(weights[k].shape) for k in order}
    return (loss, grad_x, *[grad[k] for k in order], *[upd[k][0] for k in order], *[upd[k][1] for k in order],
            *[upd[k][2] for k in order])
```

```python
import functools

import jax
import jax.numpy as jnp
from jax import lax
from jax.experimental import pallas as pl
from jax.experimental.pallas import tpu as pltpu

F32, BF16 = jnp.float32, jnp.bfloat16
MESH = pl.DeviceIdType.MESH
EPS = 1e-6
N_META = 16
QB = 128
PAD = QB - N_META
HALO = 16
POOL_WINDOWS = (2.0, 4.0, 8.0, 16.0)
HEAD_SCALE = 0.125
LR, B1, B2, ADAM_EPS, WD, STEP = 0.001, 0.9, 0.999, 1e-08, 0.01, 10
VMEM_LIMIT = 56 * 1024 * 1024


def _params(sem=None):
    return pltpu.CompilerParams(dimension_semantics=sem, vmem_limit_bytes=VMEM_LIMIT)


def _nt(a, b):
    return lax.dot_general(a, b, (((1,), (1,)), ((), ())), preferred_element_type=F32)


def _tn(a, b):
    return lax.dot_general(a, b, (((0,), (0,)), ((), ())), preferred_element_type=F32)


def _nn(a, b):
    return jnp.dot(a, b, preferred_element_type=F32)


def _fold8(v):
    r, c = v.shape
    return jnp.sum(v.reshape(r // 8, 8, c), axis=0)


NCH = 512


def _rows_call(name, body, tm, row_ins, consts, row_outs, accs=(), rider=None):
    t = row_ins[0].shape[0]
    ride = _Ride(rider, len(row_ins) + len(consts), len(row_outs) + len(accs), t // tm)

    def stepped(*refs):
        step = pl.program_id(0)
        ride.before(refs, step)
        body(*ride.own(refs))
        ride.after(refs, step)

    in_specs = [pl.BlockSpec((tm, a.shape[1]), lambda i: (i, 0)) for a in row_ins]
    for a, layer in consts:
        if layer is None:
            in_specs.append(pl.BlockSpec(a.shape, lambda i: (0, 0)))
        else:
            in_specs.append(pl.BlockSpec((None, *a.shape[1:]), lambda i, l=layer: (l, 0, 0)))
    return ride.call(
        stepped, name, (t // tm,), in_specs, [*row_ins, *[a for a, _ in consts]],
        [pl.BlockSpec((tm, c), lambda i: (i, 0)) for c, _ in row_outs]
        + [pl.BlockSpec(s, lambda i: (0, 0)) for s in accs],
        [jax.ShapeDtypeStruct((t, c), dt) for c, dt in row_outs] + [jax.ShapeDtypeStruct(s, F32) for s in accs],
        [], ("arbitrary",) if accs else ("parallel",))


def _norm_parts(x):
    r = lax.rsqrt(jnp.mean(x * x, axis=-1, keepdims=True) + EPS)
    return r, x * r


def _norm_bwd(r, xh, dyn, g):
    w = dyn * g
    return r * (w - xh * jnp.mean(w * xh, axis=-1, keepdims=True))


def _in_proj(name, h, g, w, layer, tm, ncp):
    d, n = h.shape[1], w.shape[2]

    def body(h_ref, g_ref, w_ref, hn_ref, ucp_ref, qkv_ref):
        _, xh = _norm_parts(h_ref[...])
        hn = (xh * g_ref[...]).astype(BF16)
        hn_ref[...] = hn
        for n0 in range(0, n, NCH):
            acc = _nn(hn, w_ref[:, n0:n0 + NCH])
            if n0 < ncp:
                ucp_ref[:, n0:n0 + NCH] = acc
            else:
                qkv_ref[:, n0 - ncp:n0 - ncp + NCH] = acc.astype(BF16)

    return _rows_call(name, body, tm, [h], [(g.reshape(1, d), None), (w, layer)],
                      [(d, BF16), (ncp, F32), (n - ncp, BF16)])


def _out_proj(name, y_cp, y_at, h, w, layer, tm):
    d, k1 = h.shape[1], y_cp.shape[1]

    def body(ycp_ref, yat_ref, h_ref, w_ref, o_ref):
        for n0 in range(0, d, NCH):
            o_ref[:, n0:n0 + NCH] = (h_ref[:, n0:n0 + NCH] + _nn(ycp_ref[...], w_ref[0:k1, n0:n0 + NCH])
                                     + _nn(yat_ref[...], w_ref[k1:, n0:n0 + NCH]))

    return _rows_call(name, body, tm, [y_cp, y_at, h], [(w, layer)], [(d, F32)])[0]


def _up_proj(name, h_mid, g, w, layer, tm, rider=None):
    d, n = h_mid.shape[1], w.shape[2]

    def body(h_ref, g_ref, w_ref, hn_ref, m_ref, act_ref):
        _, xh = _norm_parts(h_ref[...])
        hn = (xh * g_ref[...]).astype(BF16)
        hn_ref[...] = hn
        for n0 in range(0, n, NCH):
            acc = _nn(hn, w_ref[:, n0:n0 + NCH])
            m_ref[:, n0:n0 + NCH] = acc.astype(BF16)
            act_ref[:, n0:n0 + NCH] = jnp.square(jnp.maximum(acc, 0.0)).astype(BF16)

    return _rows_call(name, body, tm, [h_mid], [(g.reshape(1, d), None), (w, layer)],
                      [(d, BF16), (n, BF16), (n, BF16)], rider=rider)


def _down_proj(name, act, h_mid, w, layer, tm, rider=None):
    d = h_mid.shape[1]

    def body(a_ref, h_ref, w_ref, o_ref):
        for n0 in range(0, d, NCH):
            o_ref[:, n0:n0 + NCH] = h_ref[:, n0:n0 + NCH] + _nn(a_ref[...], w_ref[:, n0:n0 + NCH])

    return _rows_call(name, body, tm, [act, h_mid], [(w, layer)], [(d, F32)], rider=rider)


def _down_proj_dx(name, dh, m_pre, w, layer, tm, rider=None):
    n = w.shape[1]

    def body(dh_ref, m_ref, w_ref, dm_ref):
        dhb = dh_ref[...].astype(BF16)
        for n0 in range(0, n, NCH):
            dm_ref[:, n0:n0 + NCH] = (_nt(dhb, w_ref[n0:n0 + NCH, :])
                                      * (2.0 * jnp.maximum(m_ref[:, n0:n0 + NCH].astype(F32), 0.0))).astype(BF16)

    return _rows_call(name, body, tm, [dh, m_pre], [(w, layer)], [(n, BF16)], rider=rider)


def _up_proj_dx(name, dm, h_mid, dh, g, w_up, w_out, layer, tm, rider=None):
    d = h_mid.shape[1]

    def body(dm_ref, h_ref, dh_ref, g_ref, wup_ref, wout_ref, dhm_ref, dy_ref, dg_ref):
        @pl.when(pl.program_id(0) == 0)
        def _():
            dg_ref[...] = jnp.zeros_like(dg_ref)
        dyn = _nt(dm_ref[...], wup_ref[...])
        r, xh = _norm_parts(h_ref[...])
        dhm = dh_ref[...] + _norm_bwd(r, xh, dyn, g_ref[...])
        dhm_ref[...] = dhm
        dg_ref[...] += _fold8(dyn * xh)
        dy_ref[...] = _nt(dhm.astype(BF16), wout_ref[...])

    return _rows_call(name, body, tm, [dm, h_mid, dh], [(g.reshape(1, d), None), (w_up, layer), (w_out, layer)],
                      [(d, F32), (w_out.shape[1], F32)], [(8, d)], rider)


def _in_proj_dx(name, dus, h, dh_mid, g, w, layer, tm, rider=None):
    d = h.shape[1]
    ns = [du.shape[1] for du in dus]
    nd = len(dus)

    def body(*refs):
        du_refs = refs[:nd]
        h_ref, dhm_ref, g_ref, w_ref, dh_ref, dg_ref = refs[nd:]

        @pl.when(pl.program_id(0) == 0)
        def _():
            dg_ref[...] = jnp.zeros_like(dg_ref)
        dyn, off = None, 0
        for du_ref, n in zip(du_refs, ns):
            part = _nt(du_ref[...], w_ref[:, off:off + n])
            dyn = part if dyn is None else dyn + part
            off += n
        r, xh = _norm_parts(h_ref[...])
        dh_ref[...] = dhm_ref[...] + _norm_bwd(r, xh, dyn, g_ref[...])
        dg_ref[...] += _fold8(dyn * xh)

    return _rows_call(name, body, tm, [*dus, h, dh_mid], [(g.reshape(1, d), None), (w, layer)], [(d, F32)],
                      [(8, d)], rider)


def _mm_tn(name, a, b, tt, tka, tn, into, shape, layer, row_off, col_off):
    t, ka = a.shape
    n = b.shape[1]
    assert t % tt == 0 and ka % tka == 0 and n % tn == 0 and row_off % tka == 0 and col_off % tn == 0

    def body(a_ref, b_ref, *rest):
        o_ref = rest[-1]

        @pl.when(pl.program_id(2) == 0)
        def _():
            o_ref[...] = jnp.zeros_like(o_ref)
        o_ref[...] += _tn(a_ref[...].astype(BF16), b_ref[...].astype(BF16))

    in_specs = [pl.BlockSpec((tt, tka), lambda i, j, s: (s, i)), pl.BlockSpec((tt, tn), lambda i, j, s: (s, j))]
    args = [a, b]
    if into is not None:
        in_specs.append(pl.BlockSpec(memory_space=pl.ANY))
        args.append(into)
    return pl.pallas_call(
        body, name=name, grid=(ka // tka, n // tn, t // tt), in_specs=in_specs,
        out_specs=pl.BlockSpec((None, tka, tn), lambda i, j, s: (layer, row_off // tka + i, col_off // tn + j)),
        out_shape=jax.ShapeDtypeStruct(shape, F32),
        input_output_aliases={} if into is None else {2: 0},
        compiler_params=_params(("parallel", "parallel", "arbitrary")),
    )(*args)


def _mm_tn_slab(name, a_list, b_list, tt, into, shape, layer):
    t = a_list[0].shape[0]
    kas, ns = [a.shape[1] for a in a_list], [b.shape[1] for b in b_list]
    assert t % tt == 0 and (sum(kas), sum(ns)) == tuple(shape[1:])
    na, nb = len(a_list), len(b_list)

    def body(*refs):
        o_ref = refs[-1]

        @pl.when(pl.program_id(0) == 0)
        def _():
            o_ref[...] = jnp.zeros_like(o_ref)
        r0 = 0
        for a_ref, ka in zip(refs[:na], kas):
            a = a_ref[...].astype(BF16)
            c0 = 0
            for b_ref, n in zip(refs[na:na + nb], ns):
                o_ref[r0:r0 + ka, c0:c0 + n] += _tn(a, b_ref[...].astype(BF16))
                c0 += n
            r0 += ka

    in_specs = [pl.BlockSpec((tt, c), lambda s: (s, 0)) for c in kas + ns]
    args = [*a_list, *b_list]
    if into is not None:
        in_specs.append(pl.BlockSpec(memory_space=pl.ANY))
        args.append(into)
    return pl.pallas_call(
        body, name=name, grid=(t // tt,), in_specs=in_specs,
        out_specs=pl.BlockSpec((None, *shape[1:]), lambda s: (layer, 0, 0)),
        out_shape=jax.ShapeDtypeStruct(shape, F32),
        input_output_aliases={} if into is None else {na + nb: 0},
        compiler_params=_params(("arbitrary",)),
    )(*args)


def _build_h(name, x, meta, lp, rider=None):
    bl, s, d = x.shape
    nq = lp // QB
    ride = _Ride(rider, 2, 1, bl * nq)

    def body(*refs):
        x_ref, m_ref, o_ref = ride.own(refs)
        j = pl.program_id(1)
        step = pl.program_id(0) * nq + j
        ride.before(refs, step)
        head = jnp.concatenate([jnp.zeros((PAD, d), F32), m_ref[...]], axis=0)
        o_ref[...] = jnp.where(j == 0, head, x_ref[...])
        ride.after(refs, step)

    return ride.call(
        body, name, (bl, nq),
        [pl.BlockSpec((None, QB, d), lambda b, j: (b, jnp.maximum(j - 1, 0), 0)),
         pl.BlockSpec(meta.shape, lambda b, j: (0, 0))], [x, meta],
        [pl.BlockSpec((QB, d), lambda b, j: (b * nq + j, 0))], [jax.ShapeDtypeStruct((bl * lp, d), F32)], [],
        ("parallel", "arbitrary"))


def _loss_bwd(name, h, g, target, lp):
    t, d = h.shape
    bl = target.shape[0]
    nq = lp // QB

    def body(h_ref, g_ref, t_ref, dh_ref, ls_ref, dg_ref):
        b, j = pl.program_id(0), pl.program_id(1)

        @pl.when((b == 0) & (j == 0))
        def _():
            ls_ref[...] = jnp.zeros_like(ls_ref)
            dg_ref[...] = jnp.zeros_like(dg_ref)
        xv = h_ref[...]
        r = lax.rsqrt(jnp.mean(xv * xv, axis=-1, keepdims=True) + EPS)
        xh = xv * r
        gv = g_ref[...]
        err = jnp.where(j >= 1, xh * gv - t_ref[...], 0.0)
        ls_ref[...] += _fold8(err * err) * (0.5 / d)
        dy = err * (1.0 / d)
        w = dy * gv
        dh_ref[...] = r * (w - xh * jnp.mean(w * xh, axis=-1, keepdims=True))
        dg_ref[...] += _fold8(dy * xh)

    return pl.pallas_call(
        body, name=name, grid=(bl, nq),
        in_specs=[pl.BlockSpec((QB, d), lambda b, j: (b * nq + j, 0)), pl.BlockSpec((1, d), lambda b, j: (0, 0)),
                  pl.BlockSpec((None, QB, d), lambda b, j: (b, jnp.maximum(j - 1, 0), 0))],
        out_specs=[pl.BlockSpec((QB, d), lambda b, j: (b * nq + j, 0)), pl.BlockSpec((8, d), lambda b, j: (0, 0)),
                   pl.BlockSpec((8, d), lambda b, j: (0, 0))],
        out_shape=[jax.ShapeDtypeStruct((t, d), F32), jax.ShapeDtypeStruct((8, d), F32),
                   jax.ShapeDtypeStruct((8, d), F32)],
        compiler_params=_params(("arbitrary", "arbitrary")),
    )(h, g.reshape(1, d), target)


def _pool_select(grp, a2, a4, a8, a16):
    return jnp.where(grp == 0, a2, jnp.where(grp == 1, a4, jnp.where(grp == 2, a8, a16)))


def _trailing_sums(v):
    s2 = v + pltpu.roll(v, 1, 0)
    s4 = s2 + pltpu.roll(s2, 2, 0)
    s8 = s4 + pltpu.roll(s4, 4, 0)
    s16 = s8 + pltpu.roll(s8, 8, 0)
    return s2, s4, s8, s16


def _leading_sums(v):
    n = v.shape[0]
    s2 = v + pltpu.roll(v, n - 1, 0)
    s4 = s2 + pltpu.roll(s2, n - 2, 0)
    s8 = s4 + pltpu.roll(s4, n - 4, 0)
    s16 = s8 + pltpu.roll(s8, n - 8, 0)
    return s2, s4, s8, s16


def _convpool_fwd(name, u_cp, wconv, wbd, pscale, lp, r):
    t = u_cp.shape[0]
    cw = u_cp.shape[1] // 4
    tps, hb = lp // r, r // HALO

    def body(cb_ref, cc_ref, cx_ref, pi_ref, cch_ref, cxh_ref, pih_ref, wc_ref, wbd_ref, ps_ref, y_ref):
        i = pl.program_id(0)
        lrow = (i % tps) * r + lax.broadcasted_iota(jnp.int32, (r, 1), 0)
        valid = lrow >= PAD
        xx = jnp.concatenate([cch_ref[...] * cxh_ref[...], cc_ref[...] * cx_ref[...]], axis=0)
        conv = (wc_ref[0:1, :] * pltpu.roll(xx, 2, 0) + wc_ref[1:2, :] * pltpu.roll(xx, 1, 0)
                + wc_ref[2:3, :] * xx)
        y_ref[:, 0:cw] = (cb_ref[...] * conv[HALO:]).astype(y_ref.dtype)
        p = pi_ref[...]
        grp = lax.broadcasted_iota(jnp.int32, (1, cw), 1) // (cw // 4)
        sel = _pool_select(grp, *_trailing_sums(jnp.concatenate([pih_ref[...], p], axis=0)))[HALO:]
        cnt = jnp.maximum(jnp.minimum((lrow - (PAD - 1)).astype(F32), _pool_select(grp, *POOL_WINDOWS)), 1.0)
        pooled = jnp.where(valid, sel / cnt - p, 0.0)
        y_ref[:, cw:2 * cw] = (_nn(pooled.astype(BF16), wbd_ref[...]) * ps_ref[...]).astype(y_ref.dtype)

    def main(col):
        return pl.BlockSpec((r, cw), lambda i: (i, col))

    def prev(col):
        return pl.BlockSpec((HALO, cw), lambda i: (jnp.maximum(i * hb - 1, 0), col))

    def whole(a):
        return pl.BlockSpec(a.shape, lambda i: (0, 0))

    return pl.pallas_call(
        body, name=name, grid=(t // r,),
        in_specs=[main(0), main(1), main(2), main(3), prev(1), prev(2), prev(3), whole(wconv), whole(wbd),
                  whole(pscale)],
        out_specs=pl.BlockSpec((r, 2 * cw), lambda i: (i, 0)),
        out_shape=jax.ShapeDtypeStruct((t, 2 * cw), BF16),
        compiler_params=_params(("parallel",)),
    )(u_cp, u_cp, u_cp, u_cp, u_cp, u_cp, u_cp, wconv, wbd, pscale)


def _convpool_bwd(name, u_cp, dy, wconv, wbd, pscale, lp, r):
    t = u_cp.shape[0]
    cw = u_cp.shape[1] // 4
    tps, hb = lp // r, r // HALO
    e = r + HALO

    def body(cb_ref, cc_ref, cx_ref, pi_ref, cbn_ref, cch_ref, cxh_ref, pih_ref, dyc_ref, dyp_ref, dycn_ref,
             dypn_ref, wc_ref, wbd_ref, ps_ref, du_ref, sm_ref, dwbd_ref):
        i = pl.program_id(0)

        @pl.when(i == 0)
        def _():
            sm_ref[...] = jnp.zeros_like(sm_ref)
            dwbd_ref[...] = jnp.zeros_like(dwbd_ref)
        lrow_e = (i % tps) * r + lax.broadcasted_iota(jnp.int32, (e, 1), 0)
        valid_e = (lrow_e >= PAD) & (lrow_e < lp)
        lrow, valid = lrow_e[:r], lrow_e[:r] >= PAD
        w0, w1, w2 = wc_ref[0:1, :], wc_ref[1:2, :], wc_ref[2:3, :]
        cb, cc, cx = cb_ref[...], cc_ref[...], cx_ref[...]
        prod = cc * cx
        xx = jnp.concatenate([cch_ref[...] * cxh_ref[...], prod], axis=0)
        back1, back2 = pltpu.roll(xx, 1, 0)[HALO:], pltpu.roll(xx, 2, 0)[HALO:]
        dyc = dyc_ref[...]
        du_ref[:, 0:cw] = (dyc * (w0 * back2 + w1 * back1 + w2 * prod)).astype(du_ref.dtype)
        dconv_e = jnp.where(valid_e, jnp.concatenate([dyc * cb, dycn_ref[...] * cbn_ref[...]], axis=0), 0.0)
        dconv = dconv_e[:r]
        dprod = (w2 * dconv + w1 * pltpu.roll(dconv_e, e - 1, 0)[:r] + w0 * pltpu.roll(dconv_e, e - 2, 0)[:r])
        du_ref[:, cw:2 * cw] = (dprod * cx).astype(du_ref.dtype)
        du_ref[:, 2 * cw:3 * cw] = (dprod * cc).astype(du_ref.dtype)
        sm_ref[0:8, :] += _fold8(dconv * back2)
        sm_ref[8:16, :] += _fold8(dconv * back1)
        sm_ref[16:24, :] += _fold8(dconv * prod)
        p = pi_ref[...]
        grp = lax.broadcasted_iota(jnp.int32, (1, cw), 1) // (cw // 4)
        win = _pool_select(grp, *POOL_WINDOWS)
        sel = _pool_select(grp, *_trailing_sums(jnp.concatenate([pih_ref[...], p], axis=0)))[HALO:]
        cnt_e = jnp.maximum(jnp.minimum((lrow_e - (PAD - 1)).astype(F32), win), 1.0)
        pooled = jnp.where(valid, sel / cnt_e[:r] - p, 0.0).astype(BF16)
        dyp = dyp_ref[...]
        sm_ref[24:32, :] += _fold8(dyp * _nn(pooled, wbd_ref[...]))
        dpre_e = (jnp.concatenate([dyp, dypn_ref[...]], axis=0) * ps_ref[...]).astype(BF16)
        dwbd_ref[...] += _tn(pooled, dpre_e[:r])
        dpooled_e = jnp.where(valid_e, _nt(dpre_e, wbd_ref[...]), 0.0)
        ahead = _pool_select(grp, *_leading_sums(dpooled_e / cnt_e))[:r]
        du_ref[:, 3 * cw:4 * cw] = (ahead - dpooled_e[:r]).astype(du_ref.dtype)

    last_halo = t // HALO - 1

    def main(col):
        return pl.BlockSpec((r, cw), lambda i: (i, col))

    def prev(col):
        return pl.BlockSpec((HALO, cw), lambda i: (jnp.maximum(i * hb - 1, 0), col))

    def nxt(col):
        return pl.BlockSpec((HALO, cw), lambda i: (jnp.minimum((i + 1) * hb, last_halo), col))

    def whole(a):
        return pl.BlockSpec(a.shape, lambda i: (0, 0))

    return pl.pallas_call(
        body, name=name, grid=(t // r,),
        in_specs=[main(0), main(1), main(2), main(3), nxt(0), prev(1), prev(2), prev(3), main(0), main(1), nxt(0),
                  nxt(1), whole(wconv), whole(wbd), whole(pscale)],
        out_specs=[pl.BlockSpec((r, 4 * cw), lambda i: (i, 0)), pl.BlockSpec((32, cw), lambda i: (0, 0)),
                   pl.BlockSpec((cw, cw), lambda i: (0, 0))],
        out_shape=[jax.ShapeDtypeStruct((t, 4 * cw), BF16), jax.ShapeDtypeStruct((32, cw), F32),
                   jax.ShapeDtypeStruct((cw, cw), F32)],
        compiler_params=_params(("arbitrary",)),
    )(u_cp, u_cp, u_cp, u_cp, u_cp, u_cp, u_cp, u_cp, dy, dy, dy, dy, wconv, wbd, pscale)


KW = 2 * QB
HP = 4
DECAY = 64.0


def _cumsum_matrix(before, kw):
    r = lax.broadcasted_iota(jnp.int32, (kw, kw), 0)
    c = lax.broadcasted_iota(jnp.int32, (kw, kw), 1)
    return ((r < c) if before else (r > c)).astype(BF16)


def _running(v, mat):
    m = v.shape[0]
    hi = v.astype(BF16)
    ext = _nn(jnp.concatenate([hi, (v - hi.astype(F32)).astype(BF16)], axis=0), mat)
    return ext[:m] + ext[m:]


def _log_sigmoid(z):
    neg_abs = lax.bitcast_convert_type(lax.bitcast_convert_type(z, jnp.int32) | jnp.int32(-2 ** 31), F32)
    return jnp.minimum(z, 0.0) - jnp.log(1.0 + jnp.exp(neg_abs))


def _stack_heads(v, head0):
    zero = jnp.zeros_like(v)
    return jnp.concatenate([jnp.where(head0, v, zero), jnp.where(head0, zero, v)], axis=0)


def _lanes(hp):
    return slice(hp * QB, (hp + 1) * QB)


def _key_walk(qi):
    prev = jnp.maximum(qi - 1, 0)
    return prev, (prev % 2) * QB, prev // 2, 1 - prev % 2


def _attn_fwd(name, qkv, bl, lp, rider=None):
    t = qkv.shape[0]
    nq, nblk = lp // QB, qkv.shape[1] // (3 * HP * QB)
    assert nblk == 1
    ride = _Ride(rider, 3, 3, bl * nblk * nq)

    def body(*refs):
        q_ref, k_ref, v_ref, o_ref, lt_ref, g0_ref, run_s, acc_s = ride.own(refs)
        qi = pl.program_id(2)
        step = (pl.program_id(0) * nblk + pl.program_id(1)) * nq + qi
        ride.before(refs, step)
        head0 = lax.broadcasted_iota(jnp.int32, (QB, QB), 1) < QB // 2
        q2 = [_stack_heads(q_ref[:, _lanes(hp)] * jnp.asarray(HEAD_SCALE, BF16), head0) for hp in range(HP)]
        later = {KW: _cumsum_matrix(False, KW), QB: _cumsum_matrix(False, QB)}
        q_pos = qi * QB + (lax.broadcasted_iota(jnp.int32, (2 * QB, KW), 0) & (QB - 1))
        col = lax.broadcasted_iota(jnp.int32, (2 * QB, KW), 1)
        prev, off, ngrp, lo_g = _key_walk(qi)

        def group(start, kw, masked):
            start = pl.multiple_of(start, QB)
            if masked:
                k_pos = start + col[:, :kw]
                valid = (k_pos < q_pos[:, :kw]) & (k_pos >= PAD)
            z = [_nt(q2[hp], k_ref[pl.ds(start, kw), _lanes(hp)]) for hp in range(HP)]
            logp, after, rs = [], [], []
            for hp in range(HP):
                lp_ = _log_sigmoid(z[hp])
                lk = lp_ - z[hp]
                if masked:
                    lk = jnp.where(valid, lk, 0.0)
                logp.append(lp_)
                rs.append(jnp.sum(lk, axis=1, keepdims=True))
                after.append(_running(lk, later[kw]))
            for hp in range(HP):
                run = run_s[hp]
                a = jnp.exp(logp[hp] + after[hp] + run)
                if masked:
                    a = jnp.where(valid, a, 0.0)
                run_s[hp] = run + rs[hp]
                acc_s[hp] += _nn(a.astype(BF16), v_ref[pl.ds(start, kw), _lanes(hp)])

        def alive():
            most = run_s[0]
            for hp in range(1, HP):
                most = jnp.maximum(most, run_s[hp])
            return jnp.max(most) > -DECAY

        def older(st):
            group(off + st[0] * KW, KW, False)
            return st[0] - 1, alive()

        run_s[...] = jnp.zeros_like(run_s)
        acc_s[...] = jnp.zeros_like(acc_s)
        group(prev * QB, KW, True)
        g, live = lax.while_loop(lambda st: (st[0] >= lo_g) & st[1], older, (ngrp - 1, alive()))
        bottom = (g < lo_g) & live & ((off > 0) | (ngrp >= 1))
        pl.when(bottom & (off > 0))(lambda: group(0, QB, True))
        pl.when(bottom & (off == 0))(lambda: group(0, KW, True))
        g0_ref[pl.program_id(0), qi] = jnp.where(bottom, -1, g + 1).astype(F32)
        for hp in range(HP):
            o_ref[:, _lanes(hp)] = jnp.where(head0, acc_s[hp, :QB], acc_s[hp, QB:]).astype(o_ref.dtype)
            lt_ref[:, _lanes(hp)] = jnp.where(head0, run_s[hp, :QB], run_s[hp, QB:])
        ride.after(refs, step)

    wb = HP * QB
    blk = pl.BlockSpec((QB, wb), lambda b, p, i: (b * nq + i, p))
    return ride.call(
        body, name, (bl, nblk, nq),
        [blk, pl.BlockSpec((lp, wb), lambda b, p, i: (b, nblk + p)),
         pl.BlockSpec((lp, wb), lambda b, p, i: (b, 2 * nblk + p))], [qkv, qkv, qkv],
        [blk, blk, pl.BlockSpec(memory_space=pltpu.SMEM)],
        [jax.ShapeDtypeStruct((t, nblk * wb), BF16), jax.ShapeDtypeStruct((t, nblk * wb), F32),
         jax.ShapeDtypeStruct((bl, nq), F32)],
        [pltpu.VMEM((HP, 2 * QB, 1), F32), pltpu.VMEM((HP, 2 * QB, QB), F32)])


def _attn_bwd(name, qkv, lt, g0, dy, bl, lp, rider=None):
    t = qkv.shape[0]
    nq, nblk = lp // QB, qkv.shape[1] // (3 * HP * QB)
    ride = _Ride(rider, 6, 3, bl * nblk * nq)

    def body(*refs):
        (q_ref, k_ref, v_ref, lt_ref, do_ref, g0_ref, dq_ref, dk_ref, dv_ref, dk_acc, dv_acc, seen_s, gsum_s,
         dq_s) = ride.own(refs)
        qi = pl.program_id(2)
        step = (pl.program_id(0) * nblk + pl.program_id(1)) * nq + qi
        ride.before(refs, step)

        @pl.when(qi == 0)
        def _():
            dk_acc[...] = jnp.zeros_like(dk_acc)
            dv_acc[...] = jnp.zeros_like(dv_acc)
        lane = lax.broadcasted_iota(jnp.int32, (QB, QB), 1)
        head0 = lane < QB // 2
        q2, do2, total = [], [], []
        for hp in range(HP):
            q2.append(_stack_heads(q_ref[:, _lanes(hp)] * jnp.asarray(HEAD_SCALE, BF16), head0))
            do2.append(_stack_heads(do_ref[:, _lanes(hp)].astype(BF16), head0))
            ltv = lt_ref[:, _lanes(hp)]
            total.append(jnp.concatenate(
                [jnp.sum(jnp.where(lane == 0, ltv, 0.0), axis=1, keepdims=True),
                 jnp.sum(jnp.where(lane == QB // 2, ltv, 0.0), axis=1, keepdims=True)], axis=0))
        later = {KW: _cumsum_matrix(False, KW), QB: _cumsum_matrix(False, QB)}
        earlier = {KW: _cumsum_matrix(True, KW), QB: _cumsum_matrix(True, QB)}
        q_pos = qi * QB + (lax.broadcasted_iota(jnp.int32, (2 * QB, KW), 0) & (QB - 1))
        col = lax.broadcasted_iota(jnp.int32, (2 * QB, KW), 1)
        prev, off, ngrp, lo_g = _key_walk(qi)

        def group(start, kw, masked):
            start = pl.multiple_of(start, QB)
            if masked:
                k_pos = start + col[:, :kw]
                valid = (k_pos < q_pos[:, :kw]) & (k_pos >= PAD)
            hps = range(HP)
            kg = [k_ref[pl.ds(start, kw), _lanes(hp)] for hp in hps]
            z = [_nt(q2[hp], kg[hp]) for hp in hps]
            da = [_nt(do2[hp], v_ref[pl.ds(start, kw), _lanes(hp)]) for hp in hps]
            logp, sig, after, rs = [], [], [], []
            for hp in hps:
                lp_ = _log_sigmoid(z[hp])
                lk = lp_ - z[hp]
                if masked:
                    lk = jnp.where(valid, lk, 0.0)
                logp.append(lp_)
                sig.append(jnp.exp(lp_))
                rs.append(jnp.sum(lk, axis=1, keepdims=True))
                after.append(_running(lk, later[kw]))
            a, gg, before = [], [], []
            for hp in hps:
                a_ = jnp.exp(logp[hp] + after[hp] + (total[hp] - seen_s[hp] - rs[hp]))
                if masked:
                    a_ = jnp.where(valid, a_, 0.0)
                a.append(a_.astype(BF16))
                gg.append(a_ * da[hp])
                before.append(_nn(gg[hp].astype(BF16), earlier[kw]))
            for hp in hps:
                gsum = gsum_s[hp]
                dz = gg[hp] - (gg[hp] + before[hp] + gsum) * sig[hp]
                if masked:
                    dz = jnp.where(valid, dz, 0.0)
                dz = dz.astype(BF16)
                dk_acc[pl.ds(start, kw), _lanes(hp)] += _tn(dz, q2[hp])
                dv_acc[pl.ds(start, kw), _lanes(hp)] += _tn(a[hp], do2[hp])
                seen_s[hp] += rs[hp]
                gsum_s[hp] = gsum + jnp.sum(gg[hp], axis=1, keepdims=True)
                dq_s[hp] += _nn(dz, kg[hp])

        def inner(g, _):
            group(off + g * KW, KW, False)
            return 0

        seen_s[...] = jnp.zeros_like(seen_s)
        gsum_s[...] = jnp.zeros_like(gsum_s)
        dq_s[...] = jnp.zeros_like(dq_s)
        first = g0_ref[pl.program_id(0), qi].astype(jnp.int32)
        pl.when((first < 0) & (off > 0))(lambda: group(0, QB, True))
        pl.when((first < 0) & (off == 0))(lambda: group(0, KW, True))
        lax.fori_loop(jnp.where(first < 0, lo_g, first), ngrp, inner, 0)
        group(prev * QB, KW, True)
        for hp in range(HP):
            dq_ref[:, _lanes(hp)] = (jnp.where(head0, dq_s[hp, :QB], dq_s[hp, QB:])
                                     * HEAD_SCALE).astype(dq_ref.dtype)

        @pl.when(qi == nq - 1)
        def _():
            dk_ref[...] = dk_acc[...].astype(dk_ref.dtype)
            dv_ref[...] = dv_acc[...].astype(dv_ref.dtype)
        ride.after(refs, step)

    wb = HP * QB
    blk = pl.BlockSpec((QB, wb), lambda b, p, i: (b * nq + i, p))
    seq = pl.BlockSpec((lp, wb), lambda b, p, i: (b, p))
    out = jax.ShapeDtypeStruct((t, nblk * wb), BF16)
    return ride.call(
        body, name, (bl, nblk, nq),
        [blk, pl.BlockSpec((lp, wb), lambda b, p, i: (b, nblk + p)),
         pl.BlockSpec((lp, wb), lambda b, p, i: (b, 2 * nblk + p)), blk,
         pl.BlockSpec((QB, wb), lambda b, p, i: (b * nq + i, nblk + p)), pl.BlockSpec(memory_space=pltpu.SMEM)],
        [qkv, qkv, qkv, lt, dy, g0],
        [blk, seq, seq], [out, out, out],
        [pltpu.VMEM((lp, wb), F32), pltpu.VMEM((lp, wb), F32), pltpu.VMEM((HP, 2 * QB, 1), F32),
         pltpu.VMEM((HP, 2 * QB, 1), F32), pltpu.VMEM((HP, 2 * QB, QB), F32)])


def _place():
    return lax.axis_index("x"), lax.axis_index("y"), lax.axis_index("c")


def _peers(chip):
    kx, ky = chip // 2, chip % 2
    return ((1 - kx, ky), (kx, 1 - ky), (1 - kx, 1 - ky))


def _hbm_specs(n):
    return [pl.BlockSpec(memory_space=pl.ANY) for _ in range(n)]


def _remote(src, dst, send, recv, k, to):
    return pltpu.make_async_remote_copy(src, dst, send.at[k], recv.at[k], device_id=to, device_id_type=MESH)


class _Rider:
    def __init__(self, ins, out_shapes, aliases, nsem, first, mid=None, last=None):
        self.ins, self.out_shapes, self.aliases, self.nsem = list(ins), list(out_shapes), dict(aliases), nsem
        self.first, self.mid, self.last = first, mid, last


def _by_chip(fn):
    def run(ins, outs, send, recv):
        x, y, c = _place()
        for me in range(4):
            pl.when(2 * x + y == me)(functools.partial(fn, ins, outs, send, recv, me, c, (x, y, 1 - c)))
    return run


def _run_rider(name, rider):
    ni, no = len(rider.ins), len(rider.out_shapes)

    def body(*refs):
        args = (refs[:ni], refs[ni:ni + no], refs[ni + no], refs[ni + no + 1])
        for hook in (rider.first, rider.mid, rider.last):
            if hook is not None:
                hook(*args)

    return pl.pallas_call(
        body, name=name, in_specs=_hbm_specs(ni), out_specs=_hbm_specs(no), out_shape=rider.out_shapes,
        input_output_aliases=rider.aliases,
        scratch_shapes=[pltpu.SemaphoreType.DMA((rider.nsem,)), pltpu.SemaphoreType.DMA((rider.nsem,))],
        compiler_params=pltpu.CompilerParams(has_side_effects=True),
    )(*rider.ins)


class _Ride:
    def __init__(self, rider, n_in, n_out, steps):
        self.rider, self.n_in, self.n_out, self.steps = rider, n_in, n_out, steps
        self.ri = len(rider.ins) if rider else 0
        self.ro = len(rider.out_shapes) if rider else 0

    def own(self, refs):
        refs = list(refs)
        a, b = self.n_in, self.n_in + self.ri + self.n_out
        tail = refs[b + self.ro:len(refs) - 2] if self.rider else refs[b + self.ro:]
        return refs[:a] + refs[a + self.ri:b] + tail

    def _args(self, refs):
        a, b = self.n_in, self.n_in + self.ri + self.n_out
        return refs[a:a + self.ri], refs[b:b + self.ro], refs[-2], refs[-1]

    def before(self, refs, step):
        if self.rider is None:
            return
        pl.when(step == 0)(functools.partial(self.rider.first, *self._args(refs)))
        if self.rider.mid is not None:
            pl.when(step == (7 * self.steps) // 8)(functools.partial(self.rider.mid, *self._args(refs)))

    def after(self, refs, step):
        if self.rider is not None and self.rider.last is not None:
            pl.when(step == self.steps - 1)(functools.partial(self.rider.last, *self._args(refs)))

    def call(self, body, name, grid, in_specs, args, out_specs, out_shape, scratch,
             sem=("parallel", "parallel", "arbitrary")):
        r = self.rider
        if r is None:
            return pl.pallas_call(body, name=name, grid=grid, in_specs=in_specs, out_specs=out_specs,
                                  out_shape=out_shape, scratch_shapes=scratch, compiler_params=_params(sem))(*args)
        return pl.pallas_call(
            body, name=name, grid=grid, in_specs=in_specs + _hbm_specs(self.ri),
            out_specs=out_specs + _hbm_specs(self.ro), out_shape=out_shape + r.out_shapes,
            input_output_aliases={self.n_in + i: self.n_out + o for i, o in r.aliases.items()},
            scratch_shapes=scratch + [pltpu.SemaphoreType.DMA((r.nsem,)), pltpu.SemaphoreType.DMA((r.nsem,))],
            compiler_params=pltpu.CompilerParams(dimension_semantics=("arbitrary",) * len(grid),
                                                 vmem_limit_bytes=VMEM_LIMIT, has_side_effects=True),
        )(*args, *r.ins)


def _core_view(a, axis):
    l, r, c = a.shape
    return a.reshape(l, 4, 2, r // 8, c) if axis == 0 else a.reshape(l, 2, r // 2, c)


def _shard_view(a):
    l, r, c = a.shape
    return a.reshape(l, 2, r // 2, c)


def _piece(ref, axis, layer, chip, core):
    if axis == 0:
        return ref.at[layer, chip, core]
    cs = ref.shape[-1] // 4
    return ref.at[layer, core, :, pl.ds(chip * cs, cs)]


def _place_shard(name, w, axis, kidx, tr):
    _, r, cdim = w.shape
    shp = [2, r, cdim]
    shp[1 + axis] *= 4
    nb = r // tr

    def body(k_ref, w_ref, o_ref):
        o_ref[...] = w_ref[...].astype(o_ref.dtype)

    if axis == 0:
        out_spec = pl.BlockSpec((None, tr, cdim), lambda l, i, k_ref: (l, k_ref[0] * nb + i, 0))
    else:
        out_spec = pl.BlockSpec((None, tr, cdim), lambda l, i, k_ref: (l, i, k_ref[0]))
    return pl.pallas_call(
        body, name=name,
        grid_spec=pltpu.PrefetchScalarGridSpec(
            num_scalar_prefetch=1, grid=(2, nb),
            in_specs=[pl.BlockSpec((None, tr, cdim), lambda l, i, k_ref: (l, i, 0))], out_specs=out_spec),
        out_shape=jax.ShapeDtypeStruct(tuple(shp), BF16),
        compiler_params=_params(("arbitrary", "arbitrary")),
    )(kidx, w)


def _gather_rider(views, axes, items):
    n = len(items)

    def first(ins, outs, send, recv, me, c, sib):
        for i, (w, l) in enumerate(items):
            for j, (px, py) in enumerate(_peers(me)):
                _remote(_piece(ins[w], axes[w], l, me, c), _piece(outs[w], axes[w], l, me, c), send, recv,
                        3 * i + j, (px, py, c)).start()

    def mid(ins, outs, send, recv, me, c, sib):
        for i, (w, l) in enumerate(items):
            for j, (px, py) in enumerate(_peers(me)):
                got = _piece(outs[w], axes[w], l, 2 * px + py, c)
                _remote(got, got, send, recv, 3 * i + j, (px, py, c)).wait_recv()
                _remote(got, got, send, recv, 3 * (n + i) + j, sib).start()

    def last(ins, outs, send, recv, me, c, sib):
        for i, (w, l) in enumerate(items):
            for j, (px, py) in enumerate(_peers(me)):
                mine, got = _piece(outs[w], axes[w], l, me, c), _piece(outs[w], axes[w], l, 2 * px + py, c)
                theirs = _piece(outs[w], axes[w], l, 2 * px + py, 1 - c)
                _remote(theirs, theirs, send, recv, 3 * (n + i) + j, sib).wait_recv()
                _remote(mine, mine, send, recv, 3 * i + j, (px, py, c)).wait_send()
                _remote(got, got, send, recv, 3 * (n + i) + j, sib).wait_send()

    return _Rider(views, [jax.ShapeDtypeStruct(v.shape, v.dtype) for v in views], {w: w for w in range(len(views))},
                  6 * n, _by_chip(first), _by_chip(mid), _by_chip(last))


def _swap_rider(views, axes, items):
    nv = len(views)

    def part(ref, w, l, core):
        return ref.at[l, :, core] if axes[w] == 0 else ref.at[l, core]

    def copies(ins, outs, send, recv):
        x, y, c = _place()
        return [_remote(part(ins[w], w, l, 1 - c), outs[nv + i], send, recv, i, (x, y, 1 - c))
                for i, (w, l) in enumerate(items)]

    def first(ins, outs, send, recv):
        for cp in copies(ins, outs, send, recv):
            cp.start()

    def last(ins, outs, send, recv):
        for cp in copies(ins, outs, send, recv):
            cp.wait()

    got = [jax.ShapeDtypeStruct(views[w].shape[1:2] + views[w].shape[3:] if axes[w] == 0 else views[w].shape[2:],
                                views[w].dtype) for w, _ in items]
    return _Rider(views, [jax.ShapeDtypeStruct(v.shape, v.dtype) for v in views] + got,
                  {w: w for w in range(nv)}, len(items), first, None, last)


def _add_core(name, view, got, axis, layer, cidx, tr):
    def body(c_ref, g_ref, r_ref, o_ref):
        o_ref[...] = (g_ref[...] + r_ref[...]).astype(o_ref.dtype)

    if axis == 0:
        _, nchip, _, pr, cdim = view.shape
        grid = (nchip, pr // tr)
        specs = [pl.BlockSpec((None, None, None, tr, cdim), lambda k, i, c_ref: (layer, k, c_ref[0], i, 0)),
                 pl.BlockSpec((None, tr, cdim), lambda k, i, c_ref: (k, i, 0))]
        out_spec = pl.BlockSpec((None, tr, cdim), lambda k, i, c_ref: (k, i, 0))
    else:
        _, _, pr, cdim = view.shape
        grid = (pr // tr,)
        specs = [pl.BlockSpec((None, None, tr, cdim), lambda i, c_ref: (layer, c_ref[0], i, 0)),
                 pl.BlockSpec((tr, cdim), lambda i, c_ref: (i, 0))]
        out_spec = pl.BlockSpec((tr, cdim), lambda i, c_ref: (i, 0))
    return pl.pallas_call(
        body, name=name,
        grid_spec=pltpu.PrefetchScalarGridSpec(num_scalar_prefetch=1, grid=grid, in_specs=specs,
                                               out_specs=out_spec),
        out_shape=jax.ShapeDtypeStruct(got.shape, BF16),
        compiler_params=_params(("arbitrary",) * len(grid)),
    )(cidx, view, got)


def _scatter_rider(sums, axes):
    def part(ref, i, chip):
        if axes[i] == 0:
            return ref.at[chip]
        cs = ref.shape[-1] // 4
        return ref.at[:, pl.ds(chip * cs, cs)]

    def copies(ins, outs, send, recv, me, c, sib):
        return [_remote(part(ins[i], i, 2 * px + py), outs[i].at[j], send, recv, 3 * i + j, (px, py, c))
                for i in range(len(sums)) for j, (px, py) in enumerate(_peers(me))]

    def first(*args):
        for cp in copies(*args):
            cp.start()

    def last(*args):
        for cp in copies(*args):
            cp.wait()

    shapes = [jax.ShapeDtypeStruct((3,) + (s.shape[1:] if ax == 0 else (s.shape[0], s.shape[1] // 4)), s.dtype)
              for s, ax in zip(sums, axes)]
    return _Rider(sums, shapes, {}, 3 * len(sums), _by_chip(first), None, _by_chip(last))


def _add_chips(name, own, got, axis, layer, kc_idx, tr, into, shard_shape):
    _, pr, pc = got.shape

    def body(k_ref, o_ref, g_ref, *rest):
        rest[-1][...] = (o_ref[...].astype(F32) + g_ref[0].astype(F32) + g_ref[1].astype(F32)
                         + g_ref[2].astype(F32))

    if axis == 0:
        own_spec = pl.BlockSpec((None, tr, pc), lambda i, k_ref: (k_ref[0], i, 0))
    else:
        own_spec = pl.BlockSpec((tr, pc), lambda i, k_ref: (i, k_ref[0]))
    specs = [own_spec, pl.BlockSpec((3, tr, pc), lambda i, k_ref: (0, i, 0))]
    args = [kc_idx, own, got]
    if into is not None:
        specs.append(pl.BlockSpec(memory_space=pl.ANY))
        args.append(into)
    return pl.pallas_call(
        body, name=name,
        grid_spec=pltpu.PrefetchScalarGridSpec(
            num_scalar_prefetch=1, grid=(pr // tr,), in_specs=specs,
            out_specs=pl.BlockSpec((None, None, tr, pc), lambda i, k_ref: (layer, k_ref[1], i, 0))),
        out_shape=jax.ShapeDtypeStruct(shard_shape, F32),
        input_output_aliases={} if into is None else {3: 0},
        compiler_params=_params(("arbitrary",)),
    )(*args)


def _join_rider(parts):
    def first(ins, outs, send, recv):
        x, y, c = _place()
        for w in range(len(parts)):
            _remote(ins[w].at[:, c], outs[w].at[:, c], send, recv, w, (x, y, 1 - c)).start()

    def last(ins, outs, send, recv):
        x, y, c = _place()
        for w in range(len(parts)):
            _remote(ins[w].at[:, c], outs[w].at[:, c], send, recv, w, (x, y, 1 - c)).wait_send()
            _remote(ins[w].at[:, c], outs[w].at[:, 1 - c], send, recv, w, (x, y, 1 - c)).wait_recv()

    return _Rider(parts, [jax.ShapeDtypeStruct(p.shape, p.dtype) for p in parts],
                  {w: w for w in range(len(parts))}, len(parts), first, None, last)


def _all_reduce_small(name, pack, lead, groups):
    nr, d = pack.shape
    nout = nr - (groups - 1) * lead

    def body(in_ref, sum_ref, mine, slots, send, recv):
        x, y, c = _place()
        me = 4 * x + 2 * y + c
        fold = in_ref[0:lead]
        for grp in range(1, groups):
            fold = fold + in_ref[grp * lead:(grp + 1) * lead]
        mine[0:lead] = fold
        mine[lead:] = in_ref[groups * lead:]
        slots[me] = mine[...]
        cps = []
        for r in range(1, 8):
            rx, ry, rc = r // 4, (r // 2) % 2, r % 2
            peer = (x + rx - 2 * x * rx, y + ry - 2 * y * ry, c + rc - 2 * c * rc)
            cp = pltpu.make_async_remote_copy(mine, slots.at[me], send.at[r - 1], recv.at[r - 1],
                                              device_id=peer, device_id_type=MESH)
            cp.start()
            cps.append(cp)
        for cp in cps:
            cp.wait()
        acc = slots[0]
        for dev in range(1, 8):
            acc = acc + slots[dev]
        sum_ref[...] = acc

    vmem = pl.BlockSpec(memory_space=pltpu.VMEM)
    return pl.pallas_call(
        body, name=name, in_specs=[vmem], out_specs=vmem, out_shape=jax.ShapeDtypeStruct((nout, d), F32),
        scratch_shapes=[pltpu.VMEM((nout, d), F32), pltpu.VMEM((8, nout, d), F32), pltpu.SemaphoreType.DMA((7,)),
                        pltpu.SemaphoreType.DMA((7,))],
        compiler_params=pltpu.CompilerParams(has_side_effects=True, vmem_limit_bytes=VMEM_LIMIT),
    )(pack)


def _adamw_math(w, g, m, v):
    m = B1 * m + (1.0 - B1) * g
    v = B2 * v + (1.0 - B2) * (g * g)
    m_hat = m / (1.0 - B1 ** STEP)
    v_hat = v / (1.0 - B2 ** STEP)
    return -LR * (m_hat / (jnp.sqrt(v_hat) + ADAM_EPS) + WD * w), m, v


def _adamw(name, w, g, m, v, tr):
    shape = w.shape
    flat = [a.reshape(-1, shape[-1]) for a in (w, g, m, v)]
    r, cdim = flat[0].shape

    def body(w_ref, g_ref, m_ref, v_ref, d_ref, nm_ref, nv_ref):
        d_ref[...], nm_ref[...], nv_ref[...] = _adamw_math(w_ref[...], g_ref[...], m_ref[...], v_ref[...])

    spec = pl.BlockSpec((tr, cdim), lambda i: (i, 0))
    outs = pl.pallas_call(
        body, name=name, grid=(r // tr,), in_specs=[spec] * 4, out_specs=[spec] * 3,
        out_shape=[jax.ShapeDtypeStruct((r, cdim), F32)] * 3,
        compiler_params=_params(("parallel",)),
    )(*flat)
    return [o.reshape(shape) for o in outs]


def _adamw_small(name, groups):
    n = len(groups)
    shapes = [grp[0].shape for grp in groups]
    flat = [a.reshape(-1, a.shape[-1]) for grp in groups for a in grp]

    def body(*refs):
        ins, outs = refs[:4 * n], refs[4 * n:]
        for i in range(n):
            w_ref, g_ref, m_ref, v_ref = ins[4 * i:4 * i + 4]
            outs[3 * i][...], outs[3 * i + 1][...], outs[3 * i + 2][...] = _adamw_math(
                w_ref[...], g_ref[...], m_ref[...], v_ref[...])

    vmem = pl.BlockSpec(memory_space=pltpu.VMEM)
    out_shape = [jax.ShapeDtypeStruct(flat[4 * i].shape, F32) for i in range(n) for _ in range(3)]
    outs = pl.pallas_call(body, name=name, in_specs=[vmem] * (4 * n), out_specs=[vmem] * (3 * n),
                          out_shape=out_shape)(*flat)
    return [[outs[3 * i + j].reshape(shapes[i]) for j in range(3)] for i in range(n)]


def _block_diag(w_grp):
    g, pg, _ = w_grp.shape
    eye = jnp.eye(g, dtype=w_grp.dtype)
    return (eye[:, None, :, None] * w_grp[:, :, None, :]).reshape(g * pg, g * pg)


def _diag_blocks(m, g):
    pg = m.shape[0] // g
    return jnp.stack([m[i * pg:(i + 1) * pg, i * pg:(i + 1) * pg] for i in range(g)])


BIG = ("w_in", "w_out", "w_up", "w_down")
AXES = (1, 0, 1, 0)
W_IN, W_OUT, W_UP, W_DOWN = range(4)


def kernel(x, meta_tokens, g_mix, w_in, w_conv, w_pool, pool_scale, w_out, g_mlp, w_up, w_down, g_final, loss_target, m_meta_tokens, m_g_mix, m_w_in, m_w_conv, m_w_pool, m_pool_scale, m_w_out, m_g_mlp, m_w_up, m_w_down, m_g_final, v_meta_tokens, v_g_mix, v_w_in, v_w_conv, v_w_pool, v_pool_scale, v_w_out, v_g_mlp, v_w_up, v_w_down, v_g_final):
    bl, s, d = x.shape
    depth = g_mix.shape[0]
    assert depth == 2
    lp = PAD + N_META + s
    tt = lp
    tm = lp // 4
    copy_rows, sum_rows, dw_tile = 256, 128, 1024
    cs = w_conv.shape[2]
    cw = 4 * cs
    ngrp = w_pool.shape[1]
    xi, yi, ci = _place()
    chip = (2 * xi + yi).astype(jnp.int32)
    cidx, kidx = ci.astype(jnp.int32).reshape(1), chip.reshape(1)
    kc_idx = jnp.stack([chip, ci.astype(jnp.int32)])
    shards = (w_in, w_out, w_up, w_down)

    views = [_core_view(_place_shard(f"place_{BIG[w]}", shards[w], AXES[w], kidx, copy_rows), AXES[w])
             for w in range(4)]

    def whole(w):
        return views[w].reshape(depth, -1, views[w].shape[-1])

    def gather_on(call, items):
        ws = sorted({w for w, _ in items})
        res = call(_gather_rider([views[w] for w in ws], [AXES[w] for w in ws],
                                 [(ws.index(w), layer) for w, layer in items]))
        for j, w in enumerate(ws):
            views[w] = res[len(res) - len(ws) + j]
        return res[:len(res) - len(ws)]

    placed = jnp.zeros((32, d), F32)
    placed = lax.dynamic_update_slice(placed, meta_tokens, (0, chip * meta_tokens.shape[1]))
    placed = lax.dynamic_update_slice(placed, w_conv.reshape(-1, cs), (N_META, chip * cs))
    placed = jnp.where(ci == 0, placed, 0.0)
    small = _all_reduce_small("gather_small", placed, 8, 1)
    meta_full = small[:N_META]
    conv_full = small[N_META:N_META + depth * 3, :cw].reshape(depth, 3, cw)

    (h,) = gather_on(lambda rider: _build_h("build_h", x, meta_full, lp, rider), [(W_IN, 0)])
    wbd = [_block_diag(w_pool[i]).astype(BF16) for i in range(depth)]
    saved = []
    for i in range(depth):
        hn, u_cp, qkv = _in_proj(f"in_proj{i}", h, g_mix[i], whole(W_IN), i, tm, 4 * cw)
        y_cp = _convpool_fwd(f"convpool{i}", u_cp, conv_full[i], wbd[i], pool_scale[i:i + 1], lp, tm)
        if i == 0:
            y_at, lt, g0 = gather_on(lambda rider: _attn_fwd(f"attn{i}", qkv, bl, lp, rider),
                                     [(W_OUT, 0), (W_UP, 0), (W_DOWN, 0)])
            h_mid = _out_proj(f"out_proj{i}", y_cp, y_at, h, whole(W_OUT), i, tm)
            w_up0 = whole(W_UP)
            hn2, m_pre, act = gather_on(lambda rider: _up_proj(f"up_proj{i}", h_mid, g_mlp[i], w_up0, i, tm, rider),
                                        [(W_OUT, 1), (W_DOWN, 1)])
            w_down0 = whole(W_DOWN)
            (h_next,) = gather_on(lambda rider: _down_proj(f"down_proj{i}", act, h_mid, w_down0, i, tm, rider),
                                  [(W_IN, 1), (W_UP, 1)])
        else:
            y_at, lt, g0 = _attn_fwd(f"attn{i}", qkv, bl, lp)
            h_mid = _out_proj(f"out_proj{i}", y_cp, y_at, h, whole(W_OUT), i, tm)
            hn2, m_pre, act = _up_proj(f"up_proj{i}", h_mid, g_mlp[i], whole(W_UP), i, tm)
            (h_next,) = _down_proj(f"down_proj{i}", act, h_mid, whole(W_DOWN), i, tm)
        saved.append((h, hn, u_cp, qkv, y_cp, y_at, (lt, g0), h_mid, hn2, m_pre, act))
        h = h_next

    dh, loss8, dgf8 = _loss_bwd("loss", h, g_final, loss_target, lp)
    per_layer = {k: [None] * depth for k in ("g_mix", "w_conv", "w_pool", "pool_scale", "g_mlp")}

    gw = [None] * 4
    sums, arrived = {}, {}

    def dw(name, a, b, w, layer):
        shape = whole(w).shape
        into = None if gw[w] is None else gw[w].reshape(shape)
        if isinstance(a, list):
            res = _mm_tn_slab(name, a, b, tt // 2, into, shape, layer)
        else:
            res = _mm_tn(name, a, b, tt, dw_tile, dw_tile, into, shape, layer, 0, 0)
        gw[w] = _core_view(res, AXES[w])

    def swap_rider(ws):
        return _swap_rider([gw[w] for w, _ in ws], [AXES[w] for w, _ in ws],
                           [(j, layer) for j, (_, layer) in enumerate(ws)])

    def swapped(ws, outs):
        for j, (w, layer) in enumerate(ws):
            gw[w] = outs[j]
            sums[w, layer] = _add_core(f"chip_sum_{BIG[w]}{layer}", gw[w], outs[len(ws) + j], AXES[w], layer, cidx,
                                       sum_rows)

    def scatter_rider(items):
        return _scatter_rider([sums[it] for it in items], [AXES[w] for w, _ in items])

    def bwd_mlp(i, dh, swap_early, scatter_early=()):
        _, _, _, _, y_cp, y_at, _, h_mid, hn2, m_pre, act = saved[i]
        items = list(scatter_early)
        dm, *outs = _down_proj_dx(f"down_proj_dx{i}", dh, m_pre, whole(W_DOWN), i, tm,
                                  scatter_rider(items) if items else None)
        arrived.update(zip(items, outs))
        dw(f"down_proj_dw{i}", act, dh, W_DOWN, i)
        dw(f"up_proj_dw{i}", hn2, dm, W_UP, i)
        ws = [(W_DOWN, i), (W_UP, i)] if swap_early else []
        dh_mid, dy, dg8, *outs = _up_proj_dx(f"up_proj_dx{i}", dm, h_mid, dh, g_mlp[i], whole(W_UP), whole(W_OUT), i,
                                             tm, swap_rider(ws) if ws else None)
        swapped(ws, outs)
        per_layer["g_mlp"][i] = dg8.sum(0)
        dw(f"out_proj_dw{i}", [y_cp, y_at], [dh_mid], W_OUT, i)
        return dh_mid, dy

    def bwd_mix(i, dh_mid, dy, dus3, scatter_late):
        h_in, hn, u_cp = saved[i][:3]
        du_cp, sm, dwbd = _convpool_bwd(f"convpool_bwd{i}", u_cp, dy, conv_full[i], wbd[i], pool_scale[i:i + 1], lp,
                                        tm)
        sm = sm.reshape(4, 8, cw).sum(1)
        per_layer["w_conv"][i] = sm[0:3]
        per_layer["pool_scale"][i] = sm[3]
        per_layer["w_pool"][i] = _diag_blocks(dwbd, ngrp)
        dus = [du_cp, *dus3]
        dw(f"in_proj_dw{i}", [hn], dus, W_IN, i)
        rider = None
        if scatter_late:
            swapped([(W_IN, i)], _run_rider(f"grads_swap_in{i}", swap_rider([(W_IN, i)])))
            rider = scatter_rider([(W_IN, i)])
        dh, dg8, *outs = _in_proj_dx(f"in_proj_dx{i}", dus, h_in, dh_mid, g_mix[i], whole(W_IN), i, tm, rider)
        arrived.update(zip([(W_IN, i)], outs))
        per_layer["g_mix"][i] = dg8.sum(0)
        return dh

    def attn_bwd(i, dy, rider):
        qkv, (lt, g0) = saved[i][3], saved[i][6]
        res = _attn_bwd(f"attn_bwd{i}", qkv, lt, g0, dy, bl, lp, rider)
        return res[:3], res[3:]

    dh_mid, dy = bwd_mlp(1, dh, False)
    ws = [(W_DOWN, 1), (W_UP, 1), (W_OUT, 1)]
    dus3, outs = attn_bwd(1, dy, swap_rider(ws))
    swapped(ws, outs)
    dh = bwd_mix(1, dh_mid, dy, dus3, False)

    dh_mid, dy = bwd_mlp(0, dh, True, [(W_DOWN, 1)])
    ws = [(W_IN, 1), (W_OUT, 0)]
    swapped(ws, _run_rider("grads_swap0", swap_rider(ws)))
    items = [it for it in sums if it not in arrived]
    dus3, outs = attn_bwd(0, dy, scatter_rider(items))
    arrived.update(zip(items, outs))
    dh0 = bwd_mix(0, dh_mid, dy, dus3, True)

    finals = []
    for w in range(4):
        rs_, cs_ = shards[w].shape[1:]
        part = None
        for layer in reversed(range(depth)):
            part = _add_chips(f"reduce_{BIG[w]}{layer}", sums[w, layer], arrived[w, layer], AXES[w], layer, kc_idx,
                              sum_rows, part, (depth, 2, rs_ // 2, cs_))
        finals.append(part)
    finals = _run_rider("grads_join", _join_rider(finals))
    grad = {BIG[w]: finals[w].reshape(shards[w].shape) for w in range(4)}

    dh0 = dh0.reshape(bl, lp, d)
    grad_x = dh0[:, PAD + N_META:]
    local = {k: jnp.stack(v) for k, v in per_layer.items()}
    pieces = [dh0[:, PAD:PAD + N_META].reshape(bl * N_META, d), local["g_mix"], local["g_mlp"],
              dgf8.sum(0).reshape(1, d),
              jnp.pad(local["w_conv"].reshape(-1), (0, 2 * d - local["w_conv"].size)).reshape(2, d),
              jnp.pad(local["pool_scale"].reshape(-1), (0, d - local["pool_scale"].size)).reshape(1, d),
              jnp.pad(loss8.sum(0, keepdims=True), ((0, 7), (0, 0))), local["w_pool"].reshape(-1, d)]
    summed = _all_reduce_small("small_grads", jnp.concatenate(pieces, axis=0), N_META, bl)
    o = N_META
    grad.update({
        "meta_tokens": lax.dynamic_slice_in_dim(summed[:o], chip * meta_tokens.shape[1], meta_tokens.shape[1], 1),
        "g_mix": summed[o:o + 2], "g_mlp": summed[o + 2:o + 4], "g_final": summed[o + 4],
        "w_conv": lax.dynamic_slice_in_dim(summed[o + 5:o + 7].reshape(-1)[:2 * 3 * cw].reshape(2, 3, cw),
                                           chip * cs, cs, 2),
        "pool_scale": summed[o + 7].reshape(-1)[:pool_scale.size].reshape(pool_scale.shape),
        "w_pool": summed[o + 16:].reshape(w_pool.shape),
    })
    loss = jnp.sum(summed[o + 8])

    weights = dict(meta_tokens=meta_tokens, g_mix=g_mix, w_in=w_in, w_conv=w_conv, w_pool=w_pool,
                   pool_scale=pool_scale, w_out=w_out, g_mlp=g_mlp, w_up=w_up, w_down=w_down, g_final=g_final)
    ms = dict(meta_tokens=m_meta_tokens, g_mix=m_g_mix, w_in=m_w_in, w_conv=m_w_conv, w_pool=m_w_pool,
              pool_scale=m_pool_scale, w_out=m_w_out, g_mlp=m_g_mlp, w_up=m_w_up, w_down=m_w_down,
              g_final=m_g_final)
    vs = dict(meta_tokens=v_meta_tokens, g_mix=v_g_mix, w_in=v_w_in, w_conv=v_w_conv, w_pool=v_w_pool,
              pool_scale=v_pool_scale, w_out=v_w_out, g_mlp=v_g_mlp, w_up=v_w_up, w_down=v_w_down,
              g_final=v_g_final)
    order = list(weights)
    upd = {k: _adamw(f"adamw_{k}", weights[k], grad[k], ms[k], vs[k], copy_rows) for k in BIG}
    little = [k for k in order if k not in BIG]
    for k, res in zip(little, _adamw_small("adamw_small", [(weights[k], grad[k].reshape(weights[k].shape), ms[k],
                                                            vs[k]) for k in little])):
        upd[k] = res
    grad = {k: grad[k].reshape(weights[k].shape) for k in order}
    return (loss, grad_x, *[grad[k] for k in order], *[upd[k][0] for k in order], *[upd[k][1] for k in order],
            *[upd[k][2] for k in order])
```

```python
import functools

import jax
import jax.numpy as jnp
from jax import lax
from jax.experimental import pallas as pl
from jax.experimental.pallas import tpu as pltpu

F32, BF16 = jnp.float32, jnp.bfloat16
MESH = pl.DeviceIdType.MESH
EPS = 1e-6
N_META = 16
QB = 128
PAD = QB - N_META
HALO = 16
POOL_WINDOWS = (2.0, 4.0, 8.0, 16.0)
HEAD_SCALE = 0.125
LR, B1, B2, ADAM_EPS, WD, STEP = 0.001, 0.9, 0.999, 1e-08, 0.01, 10
VMEM_LIMIT = 56 * 1024 * 1024


def _params(sem=None):
    return pltpu.CompilerParams(dimension_semantics=sem, vmem_limit_bytes=VMEM_LIMIT)


def _nt(a, b):
    return lax.dot_general(a, b, (((1,), (1,)), ((), ())), preferred_element_type=F32)


def _tn(a, b):
    return lax.dot_general(a, b, (((0,), (0,)), ((), ())), preferred_element_type=F32)


def _nn(a, b):
    return jnp.dot(a, b, preferred_element_type=F32)


def _fold8(v):
    r, c = v.shape
    return jnp.sum(v.reshape(r // 8, 8, c), axis=0)


NCH = 512


def _rows_call(name, body, tm, row_ins, consts, row_outs, accs=(), rider=None):
    t = row_ins[0].shape[0]
    ride = _Ride(rider, len(row_ins) + len(consts), len(row_outs) + len(accs), t // tm)

    def stepped(*refs):
        step = pl.program_id(0)
        ride.before(refs, step)
        body(*ride.own(refs))
        ride.after(refs, step)

    in_specs = [pl.BlockSpec((tm, a.shape[1]), lambda i: (i, 0)) for a in row_ins]
    for a, layer in consts:
        if layer is None:
            in_specs.append(pl.BlockSpec(a.shape, lambda i: (0, 0)))
        else:
            in_specs.append(pl.BlockSpec((None, *a.shape[1:]), lambda i, l=layer: (l, 0, 0)))
    return ride.call(
        stepped, name, (t // tm,), in_specs, [*row_ins, *[a for a, _ in consts]],
        [pl.BlockSpec((tm, c), lambda i: (i, 0)) for c, _ in row_outs]
        + [pl.BlockSpec(s, lambda i: (0, 0)) for s in accs],
        [jax.ShapeDtypeStruct((t, c), dt) for c, dt in row_outs] + [jax.ShapeDtypeStruct(s, F32) for s in accs],
        [], ("arbitrary",) if accs else ("parallel",))


def _norm_parts(x):
    r = lax.rsqrt(jnp.mean(x * x, axis=-1, keepdims=True) + EPS)
    return r, x * r


def _norm_bwd(r, xh, dyn, g):
    w = dyn * g
    return r * (w - xh * jnp.mean(w * xh, axis=-1, keepdims=True))


def _in_proj(name, h, g, w, layer, tm, ncp):
    d, n = h.shape[1], w.shape[2]

    def body(h_ref, g_ref, w_ref, hn_ref, ucp_ref, qkv_ref):
        _, xh = _norm_parts(h_ref[...])
        hn = (xh * g_ref[...]).astype(BF16)
        hn_ref[...] = hn
        for n0 in range(0, n, NCH):
            acc = _nn(hn, w_ref[:, n0:n0 + NCH])
            if n0 < ncp:
                ucp_ref[:, n0:n0 + NCH] = acc
            else:
                qkv_ref[:, n0 - ncp:n0 - ncp + NCH] = acc.astype(BF16)

    return _rows_call(name, body, tm, [h], [(g.reshape(1, d), None), (w, layer)],
                      [(d, BF16), (ncp, F32), (n - ncp, BF16)])


def _out_proj(name, y_cp, y_at, h, w, layer, tm):
    d, k1 = h.shape[1], y_cp.shape[1]

    def body(ycp_ref, yat_ref, h_ref, w_ref, o_ref):
        for n0 in range(0, d, NCH):
            o_ref[:, n0:n0 + NCH] = (h_ref[:, n0:n0 + NCH] + _nn(ycp_ref[...], w_ref[0:k1, n0:n0 + NCH])
                                     + _nn(yat_ref[...], w_ref[k1:, n0:n0 + NCH]))

    return _rows_call(name, body, tm, [y_cp, y_at, h], [(w, layer)], [(d, F32)])[0]


def _up_proj(name, h_mid, g, w, layer, tm, rider=None):
    d, n = h_mid.shape[1], w.shape[2]

    def body(h_ref, g_ref, w_ref, hn_ref, m_ref, act_ref):
        _, xh = _norm_parts(h_ref[...])
        hn = (xh * g_ref[...]).astype(BF16)
        hn_ref[...] = hn
        for n0 in range(0, n, NCH):
            acc = _nn(hn, w_ref[:, n0:n0 + NCH])
            m_ref[:, n0:n0 + NCH] = acc.astype(BF16)
            act_ref[:, n0:n0 + NCH] = jnp.square(jnp.maximum(acc, 0.0)).astype(BF16)

    return _rows_call(name, body, tm, [h_mid], [(g.reshape(1, d), None), (w, layer)],
                      [(d, BF16), (n, BF16), (n, BF16)], rider=rider)


def _down_proj(name, act, h_mid, w, layer, tm, rider=None):
    d = h_mid.shape[1]

    def body(a_ref, h_ref, w_ref, o_ref):
        for n0 in range(0, d, NCH):
            o_ref[:, n0:n0 + NCH] = h_ref[:, n0:n0 + NCH] + _nn(a_ref[...], w_ref[:, n0:n0 + NCH])

    return _rows_call(name, body, tm, [act, h_mid], [(w, layer)], [(d, F32)], rider=rider)


def _down_proj_loss(name, act, h_mid, w, layer, tm, g, target, lp):
    t, d = h_mid.shape
    tps = lp // tm

    def body(a_ref, h_ref, w_ref, g_ref, t_hbm, dh_ref, ls_ref, dg_ref, tbuf, hbuf, sem):
        i = pl.program_id(0)
        b, k = i // tps, i % tps

        @pl.when(i == 0)
        def _():
            ls_ref[...] = jnp.zeros_like(ls_ref)
            dg_ref[...] = jnp.zeros_like(dg_ref)

        def first_tile():
            return pltpu.make_async_copy(t_hbm.at[b, pl.ds(0, tm - QB), :], tbuf.at[pl.ds(QB, tm - QB), :], sem)

        def later_tile():
            return pltpu.make_async_copy(t_hbm.at[b, pl.ds(pl.multiple_of(k * tm - QB, 8), tm), :], tbuf, sem)

        @pl.when(k == 0)
        def _():
            tbuf[0:QB, :] = jnp.zeros((QB, d), F32)
            first_tile().start()

        @pl.when(k > 0)
        def _():
            later_tile().start()
        for n0 in range(0, d, NCH):
            hbuf[:, n0:n0 + NCH] = h_ref[:, n0:n0 + NCH] + _nn(a_ref[...], w_ref[:, n0:n0 + NCH])
        pl.when(k == 0)(lambda: first_tile().wait())
        pl.when(k > 0)(lambda: later_tile().wait())
        r, xh = _norm_parts(hbuf[...])
        gv = g_ref[...]
        token = k * tm + lax.broadcasted_iota(jnp.int32, (tm, 1), 0) >= QB
        err = jnp.where(token, xh * gv - tbuf[...], 0.0)
        ls_ref[...] += _fold8(err * err) * (0.5 / d)
        dy = err * (1.0 / d)
        dh_ref[...] = _norm_bwd(r, xh, dy, gv)
        dg_ref[...] += _fold8(dy * xh)

    row = lambda c: pl.BlockSpec((tm, c), lambda i: (i, 0))
    acc = pl.BlockSpec((8, d), lambda i: (0, 0))
    return pl.pallas_call(
        body, name=name, grid=(t // tm,),
        in_specs=[row(act.shape[1]), row(d), pl.BlockSpec((None, *w.shape[1:]), lambda i: (layer, 0, 0)),
                  pl.BlockSpec((1, d), lambda i: (0, 0)), pl.BlockSpec(memory_space=pl.ANY)],
        out_specs=[row(d), acc, acc],
        out_shape=[jax.ShapeDtypeStruct((t, d), F32), jax.ShapeDtypeStruct((8, d), F32),
                   jax.ShapeDtypeStruct((8, d), F32)],
        scratch_shapes=[pltpu.VMEM((tm, d), F32), pltpu.VMEM((tm, d), F32), pltpu.SemaphoreType.DMA(())],
        compiler_params=_params(("arbitrary",)),
    )(act, h_mid, w, g.reshape(1, d), target)


def _down_proj_dx(name, dh, m_pre, w, layer, tm, rider=None):
    n = w.shape[1]

    def body(dh_ref, m_ref, w_ref, dm_ref):
        dhb = dh_ref[...].astype(BF16)
        for n0 in range(0, n, NCH):
            dm_ref[:, n0:n0 + NCH] = (_nt(dhb, w_ref[n0:n0 + NCH, :])
                                      * (2.0 * jnp.maximum(m_ref[:, n0:n0 + NCH].astype(F32), 0.0))).astype(BF16)

    return _rows_call(name, body, tm, [dh, m_pre], [(w, layer)], [(n, BF16)], rider=rider)


def _up_proj_dx(name, dm, h_mid, dh, g, w_up, w_out, layer, tm, rider=None):
    d = h_mid.shape[1]

    def body(dm_ref, h_ref, dh_ref, g_ref, wup_ref, wout_ref, dhm_ref, dy_ref, dg_ref):
        @pl.when(pl.program_id(0) == 0)
        def _():
            dg_ref[...] = jnp.zeros_like(dg_ref)
        dyn = _nt(dm_ref[...], wup_ref[...])
        r, xh = _norm_parts(h_ref[...])
        dhm = dh_ref[...] + _norm_bwd(r, xh, dyn, g_ref[...])
        dhm_ref[...] = dhm
        dg_ref[...] += _fold8(dyn * xh)
        dy_ref[...] = _nt(dhm.astype(BF16), wout_ref[...])

    return _rows_call(name, body, tm, [dm, h_mid, dh], [(g.reshape(1, d), None), (w_up, layer), (w_out, layer)],
                      [(d, F32), (w_out.shape[1], F32)], [(8, d)], rider)


def _in_proj_dx(name, dus, h, dh_mid, g, w, layer, tm, rider=None):
    d = h.shape[1]
    ns = [du.shape[1] for du in dus]
    nd = len(dus)

    def body(*refs):
        du_refs = refs[:nd]
        h_ref, dhm_ref, g_ref, w_ref, dh_ref, dg_ref = refs[nd:]

        @pl.when(pl.program_id(0) == 0)
        def _():
            dg_ref[...] = jnp.zeros_like(dg_ref)
        dyn, off = None, 0
        for du_ref, n in zip(du_refs, ns):
            part = _nt(du_ref[...], w_ref[:, off:off + n])
            dyn = part if dyn is None else dyn + part
            off += n
        r, xh = _norm_parts(h_ref[...])
        dh_ref[...] = dhm_ref[...] + _norm_bwd(r, xh, dyn, g_ref[...])
        dg_ref[...] += _fold8(dyn * xh)

    return _rows_call(name, body, tm, [*dus, h, dh_mid], [(g.reshape(1, d), None), (w, layer)], [(d, F32)],
                      [(8, d)], rider)


def _mm_tn(name, a, b, tt, tka, tn, into, shape, layer, row_off, col_off):
    t, ka = a.shape
    n = b.shape[1]
    assert t % tt == 0 and ka % tka == 0 and n % tn == 0 and row_off % tka == 0 and col_off % tn == 0

    def body(a_ref, b_ref, *rest):
        o_ref = rest[-1]

        @pl.when(pl.program_id(2) == 0)
        def _():
            o_ref[...] = jnp.zeros_like(o_ref)
        o_ref[...] += _tn(a_ref[...].astype(BF16), b_ref[...].astype(BF16))

    in_specs = [pl.BlockSpec((tt, tka), lambda i, j, s: (s, i)), pl.BlockSpec((tt, tn), lambda i, j, s: (s, j))]
    args = [a, b]
    if into is not None:
        in_specs.append(pl.BlockSpec(memory_space=pl.ANY))
        args.append(into)
    return pl.pallas_call(
        body, name=name, grid=(ka // tka, n // tn, t // tt), in_specs=in_specs,
        out_specs=pl.BlockSpec((None, tka, tn), lambda i, j, s: (layer, row_off // tka + i, col_off // tn + j)),
        out_shape=jax.ShapeDtypeStruct(shape, F32),
        input_output_aliases={} if into is None else {2: 0},
        compiler_params=_params(("parallel", "parallel", "arbitrary")),
    )(*args)


def _mm_tn_slab(name, a_list, b_list, tt, into, shape, layer):
    t = a_list[0].shape[0]
    kas, ns = [a.shape[1] for a in a_list], [b.shape[1] for b in b_list]
    assert t % tt == 0 and (sum(kas), sum(ns)) == tuple(shape[1:])
    na, nb = len(a_list), len(b_list)

    def body(*refs):
        o_ref = refs[-1]

        @pl.when(pl.program_id(0) == 0)
        def _():
            o_ref[...] = jnp.zeros_like(o_ref)
        r0 = 0
        for a_ref, ka in zip(refs[:na], kas):
            a = a_ref[...].astype(BF16)
            c0 = 0
            for b_ref, n in zip(refs[na:na + nb], ns):
                o_ref[r0:r0 + ka, c0:c0 + n] += _tn(a, b_ref[...].astype(BF16))
                c0 += n
            r0 += ka

    in_specs = [pl.BlockSpec((tt, c), lambda s: (s, 0)) for c in kas + ns]
    args = [*a_list, *b_list]
    if into is not None:
        in_specs.append(pl.BlockSpec(memory_space=pl.ANY))
        args.append(into)
    return pl.pallas_call(
        body, name=name, grid=(t // tt,), in_specs=in_specs,
        out_specs=pl.BlockSpec((None, *shape[1:]), lambda s: (layer, 0, 0)),
        out_shape=jax.ShapeDtypeStruct(shape, F32),
        input_output_aliases={} if into is None else {na + nb: 0},
        compiler_params=_params(("arbitrary",)),
    )(*args)


def _build_h(name, x, meta, lp, rider=None):
    bl, s, d = x.shape
    nq = lp // QB
    ride = _Ride(rider, 2, 1, bl * nq)

    def body(*refs):
        x_ref, m_ref, o_ref = ride.own(refs)
        j = pl.program_id(1)
        step = pl.program_id(0) * nq + j
        ride.before(refs, step)
        head = jnp.concatenate([jnp.zeros((PAD, d), F32), m_ref[...]], axis=0)
        o_ref[...] = jnp.where(j == 0, head, x_ref[...])
        ride.after(refs, step)

    return ride.call(
        body, name, (bl, nq),
        [pl.BlockSpec((None, QB, d), lambda b, j: (b, jnp.maximum(j - 1, 0), 0)),
         pl.BlockSpec(meta.shape, lambda b, j: (0, 0))], [x, meta],
        [pl.BlockSpec((QB, d), lambda b, j: (b * nq + j, 0))], [jax.ShapeDtypeStruct((bl * lp, d), F32)], [],
        ("parallel", "arbitrary"))


def _loss_bwd(name, h, g, target, lp):
    t, d = h.shape
    bl = target.shape[0]
    nq = lp // QB

    def body(h_ref, g_ref, t_ref, dh_ref, ls_ref, dg_ref):
        b, j = pl.program_id(0), pl.program_id(1)

        @pl.when((b == 0) & (j == 0))
        def _():
            ls_ref[...] = jnp.zeros_like(ls_ref)
            dg_ref[...] = jnp.zeros_like(dg_ref)
        xv = h_ref[...]
        r = lax.rsqrt(jnp.mean(xv * xv, axis=-1, keepdims=True) + EPS)
        xh = xv * r
        gv = g_ref[...]
        err = jnp.where(j >= 1, xh * gv - t_ref[...], 0.0)
        ls_ref[...] += _fold8(err * err) * (0.5 / d)
        dy = err * (1.0 / d)
        w = dy * gv
        dh_ref[...] = r * (w - xh * jnp.mean(w * xh, axis=-1, keepdims=True))
        dg_ref[...] += _fold8(dy * xh)

    return pl.pallas_call(
        body, name=name, grid=(bl, nq),
        in_specs=[pl.BlockSpec((QB, d), lambda b, j: (b * nq + j, 0)), pl.BlockSpec((1, d), lambda b, j: (0, 0)),
                  pl.BlockSpec((None, QB, d), lambda b, j: (b, jnp.maximum(j - 1, 0), 0))],
        out_specs=[pl.BlockSpec((QB, d), lambda b, j: (b * nq + j, 0)), pl.BlockSpec((8, d), lambda b, j: (0, 0)),
                   pl.BlockSpec((8, d), lambda b, j: (0, 0))],
        out_shape=[jax.ShapeDtypeStruct((t, d), F32), jax.ShapeDtypeStruct((8, d), F32),
                   jax.ShapeDtypeStruct((8, d), F32)],
        compiler_params=_params(("arbitrary", "arbitrary")),
    )(h, g.reshape(1, d), target)


def _pool_select(grp, a2, a4, a8, a16):
    return jnp.where(grp == 0, a2, jnp.where(grp == 1, a4, jnp.where(grp == 2, a8, a16)))


def _trailing_sums(v):
    s2 = v + pltpu.roll(v, 1, 0)
    s4 = s2 + pltpu.roll(s2, 2, 0)
    s8 = s4 + pltpu.roll(s4, 4, 0)
    s16 = s8 + pltpu.roll(s8, 8, 0)
    return s2, s4, s8, s16


def _leading_sums(v):
    n = v.shape[0]
    s2 = v + pltpu.roll(v, n - 1, 0)
    s4 = s2 + pltpu.roll(s2, n - 2, 0)
    s8 = s4 + pltpu.roll(s4, n - 4, 0)
    s16 = s8 + pltpu.roll(s8, n - 8, 0)
    return s2, s4, s8, s16


def _convpool_fwd(name, u_cp, wconv, wbd, pscale, lp, r):
    t = u_cp.shape[0]
    cw = u_cp.shape[1] // 4
    tps, hb = lp // r, r // HALO

    def body(cb_ref, cc_ref, cx_ref, pi_ref, cch_ref, cxh_ref, pih_ref, wc_ref, wbd_ref, ps_ref, y_ref):
        i = pl.program_id(0)
        lrow = (i % tps) * r + lax.broadcasted_iota(jnp.int32, (r, 1), 0)
        valid = lrow >= PAD
        xx = jnp.concatenate([cch_ref[...] * cxh_ref[...], cc_ref[...] * cx_ref[...]], axis=0)
        conv = (wc_ref[0:1, :] * pltpu.roll(xx, 2, 0) + wc_ref[1:2, :] * pltpu.roll(xx, 1, 0)
                + wc_ref[2:3, :] * xx)
        y_ref[:, 0:cw] = (cb_ref[...] * conv[HALO:]).astype(y_ref.dtype)
        p = pi_ref[...]
        grp = lax.broadcasted_iota(jnp.int32, (1, cw), 1) // (cw // 4)
        sel = _pool_select(grp, *_trailing_sums(jnp.concatenate([pih_ref[...], p], axis=0)))[HALO:]
        cnt = jnp.maximum(jnp.minimum((lrow - (PAD - 1)).astype(F32), _pool_select(grp, *POOL_WINDOWS)), 1.0)
        pooled = jnp.where(valid, sel / cnt - p, 0.0)
        y_ref[:, cw:2 * cw] = (_nn(pooled.astype(BF16), wbd_ref[...]) * ps_ref[...]).astype(y_ref.dtype)

    def main(col):
        return pl.BlockSpec((r, cw), lambda i: (i, col))

    def prev(col):
        return pl.BlockSpec((HALO, cw), lambda i: (jnp.maximum(i * hb - 1, 0), col))

    def whole(a):
        return pl.BlockSpec(a.shape, lambda i: (0, 0))

    return pl.pallas_call(
        body, name=name, grid=(t // r,),
        in_specs=[main(0), main(1), main(2), main(3), prev(1), prev(2), prev(3), whole(wconv), whole(wbd),
                  whole(pscale)],
        out_specs=pl.BlockSpec((r, 2 * cw), lambda i: (i, 0)),
        out_shape=jax.ShapeDtypeStruct((t, 2 * cw), BF16),
        compiler_params=_params(("parallel",)),
    )(u_cp, u_cp, u_cp, u_cp, u_cp, u_cp, u_cp, wconv, wbd, pscale)


def _convpool_bwd(name, u_cp, dy, wconv, wbd, pscale, lp, r):
    t = u_cp.shape[0]
    cw = u_cp.shape[1] // 4
    tps, hb = lp // r, r // HALO
    e = r + HALO

    def body(cb_ref, cc_ref, cx_ref, pi_ref, cbn_ref, cch_ref, cxh_ref, pih_ref, dyc_ref, dyp_ref, dycn_ref,
             dypn_ref, wc_ref, wbd_ref, ps_ref, du_ref, sm_ref, dwbd_ref):
        i = pl.program_id(0)

        @pl.when(i == 0)
        def _():
            sm_ref[...] = jnp.zeros_like(sm_ref)
            dwbd_ref[...] = jnp.zeros_like(dwbd_ref)
        lrow_e = (i % tps) * r + lax.broadcasted_iota(jnp.int32, (e, 1), 0)
        valid_e = (lrow_e >= PAD) & (lrow_e < lp)
        lrow, valid = lrow_e[:r], lrow_e[:r] >= PAD
        w0, w1, w2 = wc_ref[0:1, :], wc_ref[1:2, :], wc_ref[2:3, :]
        cb, cc, cx = cb_ref[...], cc_ref[...], cx_ref[...]
        prod = cc * cx
        xx = jnp.concatenate([cch_ref[...] * cxh_ref[...], prod], axis=0)
        back1, back2 = pltpu.roll(xx, 1, 0)[HALO:], pltpu.roll(xx, 2, 0)[HALO:]
        dyc = dyc_ref[...]
        du_ref[:, 0:cw] = (dyc * (w0 * back2 + w1 * back1 + w2 * prod)).astype(du_ref.dtype)
        dconv_e = jnp.where(valid_e, jnp.concatenate([dyc * cb, dycn_ref[...] * cbn_ref[...]], axis=0), 0.0)
        dconv = dconv_e[:r]
        dprod = (w2 * dconv + w1 * pltpu.roll(dconv_e, e - 1, 0)[:r] + w0 * pltpu.roll(dconv_e, e - 2, 0)[:r])
        du_ref[:, cw:2 * cw] = (dprod * cx).astype(du_ref.dtype)
        du_ref[:, 2 * cw:3 * cw] = (dprod * cc).astype(du_ref.dtype)
        sm_ref[0:8, :] += _fold8(dconv * back2)
        sm_ref[8:16, :] += _fold8(dconv * back1)
        sm_ref[16:24, :] += _fold8(dconv * prod)
        p = pi_ref[...]
        grp = lax.broadcasted_iota(jnp.int32, (1, cw), 1) // (cw // 4)
        win = _pool_select(grp, *POOL_WINDOWS)
        sel = _pool_select(grp, *_trailing_sums(jnp.concatenate([pih_ref[...], p], axis=0)))[HALO:]
        cnt_e = jnp.maximum(jnp.minimum((lrow_e - (PAD - 1)).astype(F32), win), 1.0)
        pooled = jnp.where(valid, sel / cnt_e[:r] - p, 0.0).astype(BF16)
        dyp = dyp_ref[...]
        sm_ref[24:32, :] += _fold8(dyp * _nn(pooled, wbd_ref[...]))
        dpre_e = (jnp.concatenate([dyp, dypn_ref[...]], axis=0) * ps_ref[...]).astype(BF16)
        dwbd_ref[...] += _tn(pooled, dpre_e[:r])
        dpooled_e = jnp.where(valid_e, _nt(dpre_e, wbd_ref[...]), 0.0)
        ahead = _pool_select(grp, *_leading_sums(dpooled_e / cnt_e))[:r]
        du_ref[:, 3 * cw:4 * cw] = (ahead - dpooled_e[:r]).astype(du_ref.dtype)

    last_halo = t // HALO - 1

    def main(col):
        return pl.BlockSpec((r, cw), lambda i: (i, col))

    def prev(col):
        return pl.BlockSpec((HALO, cw), lambda i: (jnp.maximum(i * hb - 1, 0), col))

    def nxt(col):
        return pl.BlockSpec((HALO, cw), lambda i: (jnp.minimum((i + 1) * hb, last_halo), col))

    def whole(a):
        return pl.BlockSpec(a.shape, lambda i: (0, 0))

    return pl.pallas_call(
        body, name=name, grid=(t // r,),
        in_specs=[main(0), main(1), main(2), main(3), nxt(0), prev(1), prev(2), prev(3), main(0), main(1), nxt(0),
                  nxt(1), whole(wconv), whole(wbd), whole(pscale)],
        out_specs=[pl.BlockSpec((r, 4 * cw), lambda i: (i, 0)), pl.BlockSpec((32, cw), lambda i: (0, 0)),
                   pl.BlockSpec((cw, cw), lambda i: (0, 0))],
        out_shape=[jax.ShapeDtypeStruct((t, 4 * cw), BF16), jax.ShapeDtypeStruct((32, cw), F32),
                   jax.ShapeDtypeStruct((cw, cw), F32)],
        compiler_params=_params(("arbitrary",)),
    )(u_cp, u_cp, u_cp, u_cp, u_cp, u_cp, u_cp, u_cp, dy, dy, dy, dy, wconv, wbd, pscale)


KW = 2 * QB
HP = 4
DECAY = 64.0


def _cumsum_matrix(before, kw):
    r = lax.broadcasted_iota(jnp.int32, (kw, kw), 0)
    c = lax.broadcasted_iota(jnp.int32, (kw, kw), 1)
    return ((r < c) if before else (r > c)).astype(BF16)


def _running(v, mat):
    m = v.shape[0]
    hi = v.astype(BF16)
    ext = _nn(jnp.concatenate([hi, (v - hi.astype(F32)).astype(BF16)], axis=0), mat)
    return ext[:m] + ext[m:]


def _log_sigmoid(z):
    neg_abs = lax.bitcast_convert_type(lax.bitcast_convert_type(z, jnp.int32) | jnp.int32(-2 ** 31), F32)
    return jnp.minimum(z, 0.0) - jnp.log(1.0 + jnp.exp(neg_abs))


def _stack_heads(v, head0):
    zero = jnp.zeros_like(v)
    return jnp.concatenate([jnp.where(head0, v, zero), jnp.where(head0, zero, v)], axis=0)


def _lanes(hp):
    return slice(hp * QB, (hp + 1) * QB)


def _key_walk(qi):
    prev = jnp.maximum(qi - 1, 0)
    return prev, (prev % 2) * QB, prev // 2, 1 - prev % 2


def _attn_fwd(name, qkv, bl, lp, rider=None):
    t = qkv.shape[0]
    nq, nblk = lp // QB, qkv.shape[1] // (3 * HP * QB)
    assert nblk == 1
    ride = _Ride(rider, 3, 3, bl * nblk * nq)

    def body(*refs):
        q_ref, k_ref, v_ref, o_ref, lt_ref, g0_ref, run_s, acc_s = ride.own(refs)
        qi = pl.program_id(2)
        step = (pl.program_id(0) * nblk + pl.program_id(1)) * nq + qi
        ride.before(refs, step)
        head0 = lax.broadcasted_iota(jnp.int32, (QB, QB), 1) < QB // 2
        q2 = [_stack_heads(q_ref[:, _lanes(hp)] * jnp.asarray(HEAD_SCALE, BF16), head0) for hp in range(HP)]
        later = {KW: _cumsum_matrix(False, KW), QB: _cumsum_matrix(False, QB)}
        q_pos = qi * QB + (lax.broadcasted_iota(jnp.int32, (2 * QB, KW), 0) & (QB - 1))
        col = lax.broadcasted_iota(jnp.int32, (2 * QB, KW), 1)
        prev, off, ngrp, lo_g = _key_walk(qi)

        def group(start, kw, masked):
            start = pl.multiple_of(start, QB)
            if masked:
                k_pos = start + col[:, :kw]
                valid = (k_pos < q_pos[:, :kw]) & (k_pos >= PAD)
            z = [_nt(q2[hp], k_ref[pl.ds(start, kw), _lanes(hp)]) for hp in range(HP)]
            logp, after, rs = [], [], []
            for hp in range(HP):
                lp_ = _log_sigmoid(z[hp])
                lk = lp_ - z[hp]
                if masked:
                    lk = jnp.where(valid, lk, 0.0)
                logp.append(lp_)
                rs.append(jnp.sum(lk, axis=1, keepdims=True))
                after.append(_running(lk, later[kw]))
            for hp in range(HP):
                run = run_s[hp]
                a = jnp.exp(logp[hp] + after[hp] + run)
                if masked:
                    a = jnp.where(valid, a, 0.0)
                run_s[hp] = run + rs[hp]
                acc_s[hp] += _nn(a.astype(BF16), v_ref[pl.ds(start, kw), _lanes(hp)])

        def alive():
            most = run_s[0]
            for hp in range(1, HP):
                most = jnp.maximum(most, run_s[hp])
            return jnp.max(most) > -DECAY

        def older(st):
            group(off + st[0] * KW, KW, False)
            return st[0] - 1, alive()

        run_s[...] = jnp.zeros_like(run_s)
        acc_s[...] = jnp.zeros_like(acc_s)
        group(prev * QB, KW, True)
        g, live = lax.while_loop(lambda st: (st[0] >= lo_g) & st[1], older, (ngrp - 1, alive()))
        bottom = (g < lo_g) & live & ((off > 0) | (ngrp >= 1))
        pl.when(bottom & (off > 0))(lambda: group(0, QB, True))
        pl.when(bottom & (off == 0))(lambda: group(0, KW, True))
        g0_ref[pl.program_id(0), qi] = jnp.where(bottom, -1, g + 1).astype(F32)
        for hp in range(HP):
            o_ref[:, _lanes(hp)] = jnp.where(head0, acc_s[hp, :QB], acc_s[hp, QB:]).astype(o_ref.dtype)
            lt_ref[:, _lanes(hp)] = jnp.where(head0, run_s[hp, :QB], run_s[hp, QB:])
        ride.after(refs, step)

    wb = HP * QB
    blk = pl.BlockSpec((QB, wb), lambda b, p, i: (b * nq + i, p))
    return ride.call(
        body, name, (bl, nblk, nq),
        [blk, pl.BlockSpec((lp, wb), lambda b, p, i: (b, nblk + p)),
         pl.BlockSpec((lp, wb), lambda b, p, i: (b, 2 * nblk + p))], [qkv, qkv, qkv],
        [blk, blk, pl.BlockSpec(memory_space=pltpu.SMEM)],
        [jax.ShapeDtypeStruct((t, nblk * wb), BF16), jax.ShapeDtypeStruct((t, nblk * wb), F32),
         jax.ShapeDtypeStruct((bl, nq), F32)],
        [pltpu.VMEM((HP, 2 * QB, 1), F32), pltpu.VMEM((HP, 2 * QB, QB), F32)])


def _attn_bwd(name, qkv, lt, g0, dy, bl, lp, rider=None):
    t = qkv.shape[0]
    nq, nblk = lp // QB, qkv.shape[1] // (3 * HP * QB)
    ride = _Ride(rider, 6, 3, bl * nblk * nq)

    def body(*refs):
        (q_ref, k_ref, v_ref, lt_ref, do_ref, g0_ref, dq_ref, dk_ref, dv_ref, dk_acc, dv_acc, seen_s, gsum_s,
         dq_s) = ride.own(refs)
        qi = pl.program_id(2)
        step = (pl.program_id(0) * nblk + pl.program_id(1)) * nq + qi
        ride.before(refs, step)

        @pl.when(qi == 0)
        def _():
            dk_acc[...] = jnp.zeros_like(dk_acc)
            dv_acc[...] = jnp.zeros_like(dv_acc)
        lane = lax.broadcasted_iota(jnp.int32, (QB, QB), 1)
        head0 = lane < QB // 2
        q2, do2, total = [], [], []
        for hp in range(HP):
            q2.append(_stack_heads(q_ref[:, _lanes(hp)] * jnp.asarray(HEAD_SCALE, BF16), head0))
            do2.append(_stack_heads(do_ref[:, _lanes(hp)].astype(BF16), head0))
            ltv = lt_ref[:, _lanes(hp)]
            total.append(jnp.concatenate(
                [jnp.sum(jnp.where(lane == 0, ltv, 0.0), axis=1, keepdims=True),
                 jnp.sum(jnp.where(lane == QB // 2, ltv, 0.0), axis=1, keepdims=True)], axis=0))
        later = {KW: _cumsum_matrix(False, KW), QB: _cumsum_matrix(False, QB)}
        earlier = {KW: _cumsum_matrix(True, KW), QB: _cumsum_matrix(True, QB)}
        q_pos = qi * QB + (lax.broadcasted_iota(jnp.int32, (2 * QB, KW), 0) & (QB - 1))
        col = lax.broadcasted_iota(jnp.int32, (2 * QB, KW), 1)
        prev, off, ngrp, lo_g = _key_walk(qi)

        def group(start, kw, masked):
            start = pl.multiple_of(start, QB)
            if masked:
                k_pos = start + col[:, :kw]
                valid = (k_pos < q_pos[:, :kw]) & (k_pos >= PAD)
            hps = range(HP)
            kg = [k_ref[pl.ds(start, kw), _lanes(hp)] for hp in hps]
            z = [_nt(q2[hp], kg[hp]) for hp in hps]
            da = [_nt(do2[hp], v_ref[pl.ds(start, kw), _lanes(hp)]) for hp in hps]
            logp, sig, after, rs = [], [], [], []
            for hp in hps:
                lp_ = _log_sigmoid(z[hp])
                lk = lp_ - z[hp]
                if masked:
                    lk = jnp.where(valid, lk, 0.0)
                logp.append(lp_)
                sig.append(jnp.exp(lp_))
                rs.append(jnp.sum(lk, axis=1, keepdims=True))
                after.append(_running(lk, later[kw]))
            a, gg, before = [], [], []
            for hp in hps:
                a_ = jnp.exp(logp[hp] + after[hp] + (total[hp] - seen_s[hp] - rs[hp]))
                if masked:
                    a_ = jnp.where(valid, a_, 0.0)
                a.append(a_.astype(BF16))
                gg.append(a_ * da[hp])
                before.append(_nn(gg[hp].astype(BF16), earlier[kw]))
            for hp in hps:
                gsum = gsum_s[hp]
                dz = gg[hp] - (gg[hp] + before[hp] + gsum) * sig[hp]
                if masked:
                    dz = jnp.where(valid, dz, 0.0)
                dz = dz.astype(BF16)
                dk_acc[pl.ds(start, kw), _lanes(hp)] += _tn(dz, q2[hp])
                dv_acc[pl.ds(start, kw), _lanes(hp)] += _tn(a[hp], do2[hp])
                seen_s[hp] += rs[hp]
                gsum_s[hp] = gsum + jnp.sum(gg[hp], axis=1, keepdims=True)
                dq_s[hp] += _nn(dz, kg[hp])

        def inner(g, _):
            group(off + g * KW, KW, False)
            return 0

        seen_s[...] = jnp.zeros_like(seen_s)
        gsum_s[...] = jnp.zeros_like(gsum_s)
        dq_s[...] = jnp.zeros_like(dq_s)
        first = g0_ref[pl.program_id(0), qi].astype(jnp.int32)
        pl.when((first < 0) & (off > 0))(lambda: group(0, QB, True))
        pl.when((first < 0) & (off == 0))(lambda: group(0, KW, True))
        lax.fori_loop(jnp.where(first < 0, lo_g, first), ngrp, inner, 0)
        group(prev * QB, KW, True)
        for hp in range(HP):
            dq_ref[:, _lanes(hp)] = (jnp.where(head0, dq_s[hp, :QB], dq_s[hp, QB:])
                                     * HEAD_SCALE).astype(dq_ref.dtype)

        @pl.when(qi == nq - 1)
        def _():
            dk_ref[...] = dk_acc[...].astype(dk_ref.dtype)
            dv_ref[...] = dv_acc[...].astype(dv_ref.dtype)
        ride.after(refs, step)

    wb = HP * QB
    blk = pl.BlockSpec((QB, wb), lambda b, p, i: (b * nq + i, p))
    seq = pl.BlockSpec((lp, wb), lambda b, p, i: (b, p))
    out = jax.ShapeDtypeStruct((t, nblk * wb), BF16)
    return ride.call(
        body, name, (bl, nblk, nq),
        [blk, pl.BlockSpec((lp, wb), lambda b, p, i: (b, nblk + p)),
         pl.BlockSpec((lp, wb), lambda b, p, i: (b, 2 * nblk + p)), blk,
         pl.BlockSpec((QB, wb), lambda b, p, i: (b * nq + i, nblk + p)), pl.BlockSpec(memory_space=pltpu.SMEM)],
        [qkv, qkv, qkv, lt, dy, g0],
        [blk, seq, seq], [out, out, out],
        [pltpu.VMEM((lp, wb), F32), pltpu.VMEM((lp, wb), F32), pltpu.VMEM((HP, 2 * QB, 1), F32),
         pltpu.VMEM((HP, 2 * QB, 1), F32), pltpu.VMEM((HP, 2 * QB, QB), F32)])


def _place():
    return lax.axis_index("x"), lax.axis_index("y"), lax.axis_index("c")


def _peers(chip):
    kx, ky = chip // 2, chip % 2
    return ((1 - kx, ky), (kx, 1 - ky), (1 - kx, 1 - ky))


def _hbm_specs(n):
    return [pl.BlockSpec(memory_space=pl.ANY) for _ in range(n)]


def _remote(src, dst, send, recv, k, to):
    return pltpu.make_async_remote_copy(src, dst, send.at[k], recv.at[k], device_id=to, device_id_type=MESH)


class _Rider:
    def __init__(self, ins, out_shapes, aliases, nsem, first, mid=None, last=None):
        self.ins, self.out_shapes, self.aliases, self.nsem = list(ins), list(out_shapes), dict(aliases), nsem
        self.first, self.mid, self.last = first, mid, last


def _by_chip(fn):
    def run(ins, outs, send, recv):
        x, y, c = _place()
        for me in range(4):
            pl.when(2 * x + y == me)(functools.partial(fn, ins, outs, send, recv, me, c, (x, y, 1 - c)))
    return run


def _run_rider(name, rider):
    ni, no = len(rider.ins), len(rider.out_shapes)

    def body(*refs):
        args = (refs[:ni], refs[ni:ni + no], refs[ni + no], refs[ni + no + 1])
        for hook in (rider.first, rider.mid, rider.last):
            if hook is not None:
                hook(*args)

    return pl.pallas_call(
        body, name=name, in_specs=_hbm_specs(ni), out_specs=_hbm_specs(no), out_shape=rider.out_shapes,
        input_output_aliases=rider.aliases,
        scratch_shapes=[pltpu.SemaphoreType.DMA((rider.nsem,)), pltpu.SemaphoreType.DMA((rider.nsem,))],
        compiler_params=pltpu.CompilerParams(has_side_effects=True),
    )(*rider.ins)


class _Ride:
    def __init__(self, rider, n_in, n_out, steps):
        self.rider, self.n_in, self.n_out, self.steps = rider, n_in, n_out, steps
        self.ri = len(rider.ins) if rider else 0
        self.ro = len(rider.out_shapes) if rider else 0

    def own(self, refs):
        refs = list(refs)
        a, b = self.n_in, self.n_in + self.ri + self.n_out
        tail = refs[b + self.ro:len(refs) - 2] if self.rider else refs[b + self.ro:]
        return refs[:a] + refs[a + self.ri:b] + tail

    def _args(self, refs):
        a, b = self.n_in, self.n_in + self.ri + self.n_out
        return refs[a:a + self.ri], refs[b:b + self.ro], refs[-2], refs[-1]

    def before(self, refs, step):
        if self.rider is None:
            return
        pl.when(step == 0)(functools.partial(self.rider.first, *self._args(refs)))
        if self.rider.mid is not None:
            pl.when(step == (7 * self.steps) // 8)(functools.partial(self.rider.mid, *self._args(refs)))

    def after(self, refs, step):
        if self.rider is not None and self.rider.last is not None:
            pl.when(step == self.steps - 1)(functools.partial(self.rider.last, *self._args(refs)))

    def call(self, body, name, grid, in_specs, args, out_specs, out_shape, scratch,
             sem=("parallel", "parallel", "arbitrary")):
        r = self.rider
        if r is None:
            return pl.pallas_call(body, name=name, grid=grid, in_specs=in_specs, out_specs=out_specs,
                                  out_shape=out_shape, scratch_shapes=scratch, compiler_params=_params(sem))(*args)
        return pl.pallas_call(
            body, name=name, grid=grid, in_specs=in_specs + _hbm_specs(self.ri),
            out_specs=out_specs + _hbm_specs(self.ro), out_shape=out_shape + r.out_shapes,
            input_output_aliases={self.n_in + i: self.n_out + o for i, o in r.aliases.items()},
            scratch_shapes=scratch + [pltpu.SemaphoreType.DMA((r.nsem,)), pltpu.SemaphoreType.DMA((r.nsem,))],
            compiler_params=pltpu.CompilerParams(dimension_semantics=("arbitrary",) * len(grid),
                                                 vmem_limit_bytes=VMEM_LIMIT, has_side_effects=True),
        )(*args, *r.ins)


def _core_view(a, axis):
    l, r, c = a.shape
    return a.reshape(l, 4, 2, r // 8, c) if axis == 0 else a.reshape(l, 2, r // 2, c)


def _shard_view(a):
    l, r, c = a.shape
    return a.reshape(l, 2, r // 2, c)


def _piece(ref, axis, layer, chip, core):
    if axis == 0:
        return ref.at[layer, chip, core]
    cs = ref.shape[-1] // 4
    return ref.at[layer, core, :, pl.ds(chip * cs, cs)]


def _place_shard(name, w, axis, kidx, tr):
    _, r, cdim = w.shape
    shp = [2, r, cdim]
    shp[1 + axis] *= 4
    nb = r // tr

    def body(k_ref, w_ref, o_ref):
        o_ref[...] = w_ref[...].astype(o_ref.dtype)

    if axis == 0:
        out_spec = pl.BlockSpec((None, tr, cdim), lambda l, i, k_ref: (l, k_ref[0] * nb + i, 0))
    else:
        out_spec = pl.BlockSpec((None, tr, cdim), lambda l, i, k_ref: (l, i, k_ref[0]))
    return pl.pallas_call(
        body, name=name,
        grid_spec=pltpu.PrefetchScalarGridSpec(
            num_scalar_prefetch=1, grid=(2, nb),
            in_specs=[pl.BlockSpec((None, tr, cdim), lambda l, i, k_ref: (l, i, 0))], out_specs=out_spec),
        out_shape=jax.ShapeDtypeStruct(tuple(shp), BF16),
        compiler_params=_params(("arbitrary", "arbitrary")),
    )(kidx, w)


def _gather_rider(views, axes, items):
    n = len(items)

    def first(ins, outs, send, recv, me, c, sib):
        for i, (w, l) in enumerate(items):
            for j, (px, py) in enumerate(_peers(me)):
                _remote(_piece(ins[w], axes[w], l, me, c), _piece(outs[w], axes[w], l, me, c), send, recv,
                        3 * i + j, (px, py, c)).start()

    def mid(ins, outs, send, recv, me, c, sib):
        for i, (w, l) in enumerate(items):
            for j, (px, py) in enumerate(_peers(me)):
                got = _piece(outs[w], axes[w], l, 2 * px + py, c)
                _remote(got, got, send, recv, 3 * i + j, (px, py, c)).wait_recv()
                _remote(got, got, send, recv, 3 * (n + i) + j, sib).start()

    def last(ins, outs, send, recv, me, c, sib):
        for i, (w, l) in enumerate(items):
            for j, (px, py) in enumerate(_peers(me)):
                mine, got = _piece(outs[w], axes[w], l, me, c), _piece(outs[w], axes[w], l, 2 * px + py, c)
                theirs = _piece(outs[w], axes[w], l, 2 * px + py, 1 - c)
                _remote(theirs, theirs, send, recv, 3 * (n + i) + j, sib).wait_recv()
                _remote(mine, mine, send, recv, 3 * i + j, (px, py, c)).wait_send()
                _remote(got, got, send, recv, 3 * (n + i) + j, sib).wait_send()

    return _Rider(views, [jax.ShapeDtypeStruct(v.shape, v.dtype) for v in views], {w: w for w in range(len(views))},
                  6 * n, _by_chip(first), _by_chip(mid), _by_chip(last))


def _swap_rider(views, axes, items):
    nv = len(views)

    def part(ref, w, l, core):
        return ref.at[l, :, core] if axes[w] == 0 else ref.at[l, core]

    def copies(ins, outs, send, recv):
        x, y, c = _place()
        return [_remote(part(ins[w], w, l, 1 - c), outs[nv + i], send, recv, i, (x, y, 1 - c))
                for i, (w, l) in enumerate(items)]

    def first(ins, outs, send, recv):
        for cp in copies(ins, outs, send, recv):
            cp.start()

    def last(ins, outs, send, recv):
        for cp in copies(ins, outs, send, recv):
            cp.wait()

    got = [jax.ShapeDtypeStruct(views[w].shape[1:2] + views[w].shape[3:] if axes[w] == 0 else views[w].shape[2:],
                                views[w].dtype) for w, _ in items]
    return _Rider(views, [jax.ShapeDtypeStruct(v.shape, v.dtype) for v in views] + got,
                  {w: w for w in range(nv)}, len(items), first, None, last)


def _add_core(name, view, got, axis, layer, cidx, tr):
    def body(c_ref, g_ref, r_ref, o_ref):
        o_ref[...] = (g_ref[...] + r_ref[...]).astype(o_ref.dtype)

    if axis == 0:
        _, nchip, _, pr, cdim = view.shape
        grid = (nchip, pr // tr)
        specs = [pl.BlockSpec((None, None, None, tr, cdim), lambda k, i, c_ref: (layer, k, c_ref[0], i, 0)),
                 pl.BlockSpec((None, tr, cdim), lambda k, i, c_ref: (k, i, 0))]
        out_spec = pl.BlockSpec((None, tr, cdim), lambda k, i, c_ref: (k, i, 0))
    else:
        _, _, pr, cdim = view.shape
        grid = (pr // tr,)
        specs = [pl.BlockSpec((None, None, tr, cdim), lambda i, c_ref: (layer, c_ref[0], i, 0)),
                 pl.BlockSpec((tr, cdim), lambda i, c_ref: (i, 0))]
        out_spec = pl.BlockSpec((tr, cdim), lambda i, c_ref: (i, 0))
    return pl.pallas_call(
        body, name=name,
        grid_spec=pltpu.PrefetchScalarGridSpec(num_scalar_prefetch=1, grid=grid, in_specs=specs,
                                               out_specs=out_spec),
        out_shape=jax.ShapeDtypeStruct(got.shape, BF16),
        compiler_params=_params(("arbitrary",) * len(grid)),
    )(cidx, view, got)


def _scatter_rider(sums, axes):
    def part(ref, i, chip):
        if axes[i] == 0:
            return ref.at[chip]
        cs = ref.shape[-1] // 4
        return ref.at[:, pl.ds(chip * cs, cs)]

    def copies(ins, outs, send, recv, me, c, sib):
        return [_remote(part(ins[i], i, 2 * px + py), outs[i].at[j], send, recv, 3 * i + j, (px, py, c))
                for i in range(len(sums)) for j, (px, py) in enumerate(_peers(me))]

    def first(*args):
        for cp in copies(*args):
            cp.start()

    def last(*args):
        for cp in copies(*args):
            cp.wait()

    shapes = [jax.ShapeDtypeStruct((3,) + (s.shape[1:] if ax == 0 else (s.shape[0], s.shape[1] // 4)), s.dtype)
              for s, ax in zip(sums, axes)]
    return _Rider(sums, shapes, {}, 3 * len(sums), _by_chip(first), None, _by_chip(last))


def _add_chips(name, own, got, axis, layer, kc_idx, tr, into, shard_shape):
    _, pr, pc = got.shape

    def body(k_ref, o_ref, g_ref, *rest):
        rest[-1][...] = (o_ref[...].astype(F32) + g_ref[0].astype(F32) + g_ref[1].astype(F32)
                         + g_ref[2].astype(F32))

    if axis == 0:
        own_spec = pl.BlockSpec((None, tr, pc), lambda i, k_ref: (k_ref[0], i, 0))
    else:
        own_spec = pl.BlockSpec((tr, pc), lambda i, k_ref: (i, k_ref[0]))
    specs = [own_spec, pl.BlockSpec((3, tr, pc), lambda i, k_ref: (0, i, 0))]
    args = [kc_idx, own, got]
    if into is not None:
        specs.append(pl.BlockSpec(memory_space=pl.ANY))
        args.append(into)
    return pl.pallas_call(
        body, name=name,
        grid_spec=pltpu.PrefetchScalarGridSpec(
            num_scalar_prefetch=1, grid=(pr // tr,), in_specs=specs,
            out_specs=pl.BlockSpec((None, None, tr, pc), lambda i, k_ref: (layer, k_ref[1], i, 0))),
        out_shape=jax.ShapeDtypeStruct(shard_shape, F32),
        input_output_aliases={} if into is None else {3: 0},
        compiler_params=_params(("arbitrary",)),
    )(*args)


def _join_rider(parts):
    def first(ins, outs, send, recv):
        x, y, c = _place()
        for w in range(len(parts)):
            _remote(ins[w].at[:, c], outs[w].at[:, c], send, recv, w, (x, y, 1 - c)).start()

    def last(ins, outs, send, recv):
        x, y, c = _place()
        for w in range(len(parts)):
            _remote(ins[w].at[:, c], outs[w].at[:, c], send, recv, w, (x, y, 1 - c)).wait_send()
            _remote(ins[w].at[:, c], outs[w].at[:, 1 - c], send, recv, w, (x, y, 1 - c)).wait_recv()

    return _Rider(parts, [jax.ShapeDtypeStruct(p.shape, p.dtype) for p in parts],
                  {w: w for w in range(len(parts))}, len(parts), first, None, last)


def _all_reduce_small(name, pack, lead, groups):
    nr, d = pack.shape
    nout = nr - (groups - 1) * lead

    def body(in_ref, sum_ref, mine, slots, send, recv):
        x, y, c = _place()
        me = 4 * x + 2 * y + c
        fold = in_ref[0:lead]
        for grp in range(1, groups):
            fold = fold + in_ref[grp * lead:(grp + 1) * lead]
        mine[0:lead] = fold
        mine[lead:] = in_ref[groups * lead:]
        slots[me] = mine[...]
        cps = []
        for r in range(1, 8):
            rx, ry, rc = r // 4, (r // 2) % 2, r % 2
            peer = (x + rx - 2 * x * rx, y + ry - 2 * y * ry, c + rc - 2 * c * rc)
            cp = pltpu.make_async_remote_copy(mine, slots.at[me], send.at[r - 1], recv.at[r - 1],
                                              device_id=peer, device_id_type=MESH)
            cp.start()
            cps.append(cp)
        for cp in cps:
            cp.wait()
        acc = slots[0]
        for dev in range(1, 8):
            acc = acc + slots[dev]
        sum_ref[...] = acc

    vmem = pl.BlockSpec(memory_space=pltpu.VMEM)
    return pl.pallas_call(
        body, name=name, in_specs=[vmem], out_specs=vmem, out_shape=jax.ShapeDtypeStruct((nout, d), F32),
        scratch_shapes=[pltpu.VMEM((nout, d), F32), pltpu.VMEM((8, nout, d), F32), pltpu.SemaphoreType.DMA((7,)),
                        pltpu.SemaphoreType.DMA((7,))],
        compiler_params=pltpu.CompilerParams(has_side_effects=True, vmem_limit_bytes=VMEM_LIMIT),
    )(pack)


def _adamw_math(w, g, m, v):
    m = B1 * m + (1.0 - B1) * g
    v = B2 * v + (1.0 - B2) * (g * g)
    m_hat = m / (1.0 - B1 ** STEP)
    v_hat = v / (1.0 - B2 ** STEP)
    return -LR * (m_hat / (jnp.sqrt(v_hat) + ADAM_EPS) + WD * w), m, v


def _adamw(name, w, g, m, v, tr):
    shape = w.shape
    flat = [a.reshape(-1, shape[-1]) for a in (w, g, m, v)]
    r, cdim = flat[0].shape

    def body(w_ref, g_ref, m_ref, v_ref, d_ref, nm_ref, nv_ref):
        d_ref[...], nm_ref[...], nv_ref[...] = _adamw_math(w_ref[...], g_ref[...], m_ref[...], v_ref[...])

    spec = pl.BlockSpec((tr, cdim), lambda i: (i, 0))
    outs = pl.pallas_call(
        body, name=name, grid=(r // tr,), in_specs=[spec] * 4, out_specs=[spec] * 3,
        out_shape=[jax.ShapeDtypeStruct((r, cdim), F32)] * 3,
        compiler_params=_params(("parallel",)),
    )(*flat)
    return [o.reshape(shape) for o in outs]


def _adamw_small(name, groups):
    n = len(groups)
    shapes = [grp[0].shape for grp in groups]
    flat = [a.reshape(-1, a.shape[-1]) for grp in groups for a in grp]

    def body(*refs):
        ins, outs = refs[:4 * n], refs[4 * n:]
        for i in range(n):
            w_ref, g_ref, m_ref, v_ref = ins[4 * i:4 * i + 4]
            outs[3 * i][...], outs[3 * i + 1][...], outs[3 * i + 2][...] = _adamw_math(
                w_ref[...], g_ref[...], m_ref[...], v_ref[...])

    vmem = pl.BlockSpec(memory_space=pltpu.VMEM)
    out_shape = [jax.ShapeDtypeStruct(flat[4 * i].shape, F32) for i in range(n) for _ in range(3)]
    outs = pl.pallas_call(body, name=name, in_specs=[vmem] * (4 * n), out_specs=[vmem] * (3 * n),
                          out_shape=out_shape)(*flat)
    return [[outs[3 * i + j].reshape(shapes[i]) for j in range(3)] for i in range(n)]


def _block_diag(w_grp):
    g, pg, _ = w_grp.shape
    eye = jnp.eye(g, dtype=w_grp.dtype)
    return (eye[:, None, :, None] * w_grp[:, :, None, :]).reshape(g * pg, g * pg)


def _diag_blocks(m, g):
    pg = m.shape[0] // g
    return jnp.stack([m[i * pg:(i + 1) * pg, i * pg:(i + 1) * pg] for i in range(g)])


BIG = ("w_in", "w_out", "w_up", "w_down")
AXES = (1, 0, 1, 0)
W_IN, W_OUT, W_UP, W_DOWN = range(4)


def kernel(x, meta_tokens, g_mix, w_in, w_conv, w_pool, pool_scale, w_out, g_mlp, w_up, w_down, g_final, loss_target, m_meta_tokens, m_g_mix, m_w_in, m_w_conv, m_w_pool, m_pool_scale, m_w_out, m_g_mlp, m_w_up, m_w_down, m_g_final, v_meta_tokens, v_g_mix, v_w_in, v_w_conv, v_w_pool, v_pool_scale, v_w_out, v_g_mlp, v_w_up, v_w_down, v_g_final):
    bl, s, d = x.shape
    depth = g_mix.shape[0]
    assert depth == 2
    lp = PAD + N_META + s
    tt = lp
    tm = lp // 4
    copy_rows, sum_rows, dw_tile = 256, 128, 1024
    cs = w_conv.shape[2]
    cw = 4 * cs
    ngrp = w_pool.shape[1]
    xi, yi, ci = _place()
    chip = (2 * xi + yi).astype(jnp.int32)
    cidx, kidx = ci.astype(jnp.int32).reshape(1), chip.reshape(1)
    kc_idx = jnp.stack([chip, ci.astype(jnp.int32)])
    shards = (w_in, w_out, w_up, w_down)

    views = [_core_view(_place_shard(f"place_{BIG[w]}", shards[w], AXES[w], kidx, copy_rows), AXES[w])
             for w in range(4)]

    def whole(w):
        return views[w].reshape(depth, -1, views[w].shape[-1])

    def gather_on(call, items):
        ws = sorted({w for w, _ in items})
        res = call(_gather_rider([views[w] for w in ws], [AXES[w] for w in ws],
                                 [(ws.index(w), layer) for w, layer in items]))
        for j, w in enumerate(ws):
            views[w] = res[len(res) - len(ws) + j]
        return res[:len(res) - len(ws)]

    placed = jnp.zeros((32, d), F32)
    placed = lax.dynamic_update_slice(placed, meta_tokens, (0, chip * meta_tokens.shape[1]))
    placed = lax.dynamic_update_slice(placed, w_conv.reshape(-1, cs), (N_META, chip * cs))
    placed = jnp.where(ci == 0, placed, 0.0)
    small = _all_reduce_small("gather_small", placed, 8, 1)
    meta_full = small[:N_META]
    conv_full = small[N_META:N_META + depth * 3, :cw].reshape(depth, 3, cw)

    (h,) = gather_on(lambda rider: _build_h("build_h", x, meta_full, lp, rider), [(W_IN, 0)])
    wbd = [_block_diag(w_pool[i]).astype(BF16) for i in range(depth)]
    saved = []
    for i in range(depth):
        hn, u_cp, qkv = _in_proj(f"in_proj{i}", h, g_mix[i], whole(W_IN), i, tm, 4 * cw)
        y_cp = _convpool_fwd(f"convpool{i}", u_cp, conv_full[i], wbd[i], pool_scale[i:i + 1], lp, tm)
        if i == 0:
            y_at, lt, g0 = gather_on(lambda rider: _attn_fwd(f"attn{i}", qkv, bl, lp, rider),
                                     [(W_OUT, 0), (W_UP, 0), (W_DOWN, 0)])
            h_mid = _out_proj(f"out_proj{i}", y_cp, y_at, h, whole(W_OUT), i, tm)
            w_up0 = whole(W_UP)
            hn2, m_pre, act = gather_on(lambda rider: _up_proj(f"up_proj{i}", h_mid, g_mlp[i], w_up0, i, tm, rider),
                                        [(W_OUT, 1), (W_DOWN, 1)])
            w_down0 = whole(W_DOWN)
            (h_next,) = gather_on(lambda rider: _down_proj(f"down_proj{i}", act, h_mid, w_down0, i, tm, rider),
                                  [(W_IN, 1), (W_UP, 1)])
        else:
            y_at, lt, g0 = _attn_fwd(f"attn{i}", qkv, bl, lp)
            h_mid = _out_proj(f"out_proj{i}", y_cp, y_at, h, whole(W_OUT), i, tm)
            hn2, m_pre, act = _up_proj(f"up_proj{i}", h_mid, g_mlp[i], whole(W_UP), i, tm)
            h_next = None
            dh, loss8, dgf8 = _down_proj_loss(f"down_proj_loss{i}", act, h_mid, whole(W_DOWN), i, tm, g_final,
                                              loss_target, lp)
        saved.append((h, hn, u_cp, qkv, y_cp, y_at, (lt, g0), h_mid, hn2, m_pre, act))
        h = h_next

    per_layer = {k: [None] * depth for k in ("g_mix", "w_conv", "w_pool", "pool_scale", "g_mlp")}

    gw = [None] * 4
    sums, arrived = {}, {}

    def dw(name, a, b, w, layer):
        shape = whole(w).shape
        into = None if gw[w] is None else gw[w].reshape(shape)
        if isinstance(a, list):
            res = _mm_tn_slab(name, a, b, tt // 2, into, shape, layer)
        else:
            res = _mm_tn(name, a, b, tt, dw_tile, dw_tile, into, shape, layer, 0, 0)
        gw[w] = _core_view(res, AXES[w])

    def swap_rider(ws):
        return _swap_rider([gw[w] for w, _ in ws], [AXES[w] for w, _ in ws],
                           [(j, layer) for j, (_, layer) in enumerate(ws)])

    def swapped(ws, outs):
        for j, (w, layer) in enumerate(ws):
            gw[w] = outs[j]
            sums[w, layer] = _add_core(f"chip_sum_{BIG[w]}{layer}", gw[w], outs[len(ws) + j], AXES[w], layer, cidx,
                                       sum_rows)

    def scatter_rider(items):
        return _scatter_rider([sums[it] for it in items], [AXES[w] for w, _ in items])

    def bwd_mlp(i, dh, swap_early, scatter_early=()):
        _, _, _, _, y_cp, y_at, _, h_mid, hn2, m_pre, act = saved[i]
        items = list(scatter_early)
        dm, *outs = _down_proj_dx(f"down_proj_dx{i}", dh, m_pre, whole(W_DOWN), i, tm,
                                  scatter_rider(items) if items else None)
        arrived.update(zip(items, outs))
        dw(f"down_proj_dw{i}", act, dh, W_DOWN, i)
        dw(f"up_proj_dw{i}", hn2, dm, W_UP, i)
        ws = [(W_DOWN, i), (W_UP, i)] if swap_early else []
        dh_mid, dy, dg8, *outs = _up_proj_dx(f"up_proj_dx{i}", dm, h_mid, dh, g_mlp[i], whole(W_UP), whole(W_OUT), i,
                                             tm, swap_rider(ws) if ws else None)
        swapped(ws, outs)
        per_layer["g_mlp"][i] = dg8.sum(0)
        dw(f"out_proj_dw{i}", [y_cp, y_at], [dh_mid], W_OUT, i)
        return dh_mid, dy

    def bwd_mix(i, dh_mid, dy, dus3, scatter_late):
        h_in, hn, u_cp = saved[i][:3]
        du_cp, sm, dwbd = _convpool_bwd(f"convpool_bwd{i}", u_cp, dy, conv_full[i], wbd[i], pool_scale[i:i + 1], lp,
                                        tm)
        sm = sm.reshape(4, 8, cw).sum(1)
        per_layer["w_conv"][i] = sm[0:3]
        per_layer["pool_scale"][i] = sm[3]
        per_layer["w_pool"][i] = _diag_blocks(dwbd, ngrp)
        dus = [du_cp, *dus3]
        dw(f"in_proj_dw{i}", [hn], dus, W_IN, i)
        rider = None
        if scatter_late:
            swapped([(W_IN, i)], _run_rider(f"grads_swap_in{i}", swap_rider([(W_IN, i)])))
            rider = scatter_rider([(W_IN, i)])
        dh, dg8, *outs = _in_proj_dx(f"in_proj_dx{i}", dus, h_in, dh_mid, g_mix[i], whole(W_IN), i, tm, rider)
        arrived.update(zip([(W_IN, i)], outs))
        per_layer["g_mix"][i] = dg8.sum(0)
        return dh

    def attn_bwd(i, dy, rider):
        qkv, (lt, g0) = saved[i][3], saved[i][6]
        res = _attn_bwd(f"attn_bwd{i}", qkv, lt, g0, dy, bl, lp, rider)
        return res[:3], res[3:]

    dh_mid, dy = bwd_mlp(1, dh, False)
    ws = [(W_DOWN, 1), (W_UP, 1), (W_OUT, 1)]
    dus3, outs = attn_bwd(1, dy, swap_rider(ws))
    swapped(ws, outs)
    dh = bwd_mix(1, dh_mid, dy, dus3, False)

    dh_mid, dy = bwd_mlp(0, dh, True, [(W_DOWN, 1)])
    ws = [(W_IN, 1), (W_OUT, 0)]
    swapped(ws, _run_rider("grads_swap0", swap_rider(ws)))
    items = [it for it in sums if it not in arrived]
    dus3, outs = attn_bwd(0, dy, scatter_rider(items))
    arrived.update(zip(items, outs))
    dh0 = bwd_mix(0, dh_mid, dy, dus3, True)

    finals = []
    for w in range(4):
        rs_, cs_ = shards[w].shape[1:]
        part = None
        for layer in reversed(range(depth)):
            part = _add_chips(f"reduce_{BIG[w]}{layer}", sums[w, layer], arrived[w, layer], AXES[w], layer, kc_idx,
                              sum_rows, part, (depth, 2, rs_ // 2, cs_))
        finals.append(part)
    finals = _run_rider("grads_join", _join_rider(finals))
    grad = {BIG[w]: finals[w].reshape(shards[w].shape) for w in range(4)}

    dh0 = dh0.reshape(bl, lp, d)
    grad_x = dh0[:, PAD + N_META:]
    local = {k: jnp.stack(v) for k, v in per_layer.items()}
    pieces = [dh0[:, PAD:PAD + N_META].reshape(bl * N_META, d), local["g_mix"], local["g_mlp"],
              dgf8.sum(0).reshape(1, d),
              jnp.pad(local["w_conv"].reshape(-1), (0, 2 * d - local["w_conv"].size)).reshape(2, d),
              jnp.pad(local["pool_scale"].reshape(-1), (0, d - local["pool_scale"].size)).reshape(1, d),
              jnp.pad(loss8.sum(0, keepdims=True), ((0, 7), (0, 0))), local["w_pool"].reshape(-1, d)]
    summed = _all_reduce_small("small_grads", jnp.concatenate(pieces, axis=0), N_META, bl)
    o = N_META
    grad.update({
        "meta_tokens": lax.dynamic_slice_in_dim(summed[:o], chip * meta_tokens.shape[1], meta_tokens.shape[1], 1),
        "g_mix": summed[o:o + 2], "g_mlp": summed[o + 2:o + 4], "g_final": summed[o + 4],
        "w_conv": lax.dynamic_slice_in_dim(summed[o + 5:o + 7].reshape(-1)[:2 * 3 * cw].reshape(2, 3, cw),
                                           chip * cs, cs, 2),
        "pool_scale": summed[o + 7].reshape(-1)[:pool_scale.size].reshape(pool_scale.shape),
        "w_pool": summed[o + 16:].reshape(w_pool.shape),
    })
    loss = jnp.sum(summed[o + 8])

    weights = dict(meta_tokens=meta_tokens, g_mix=g_mix, w_in=w_in, w_conv=w_conv, w_pool=w_pool,
                   pool_scale=pool_scale, w_out=w_out, g_mlp=g_mlp, w_up=w_up, w_down=w_down, g_final=g_final)
    ms = dict(meta_tokens=m_meta_tokens, g_mix=m_g_mix, w_in=m_w_in, w_conv=m_w_conv, w_pool=m_w_pool,
              pool_scale=m_pool_scale, w_out=m_w_out, g_mlp=m_g_mlp, w_up=m_w_up, w_down=m_w_down,
              g_final=m_g_final)
    vs = dict(meta_tokens=v_meta_tokens, g_mix=v_g_mix, w_in=v_w_in, w_conv=v_w_conv, w_pool=v_w_pool,
              pool_scale=v_pool_scale, w_out=v_w_out, g_mlp=v_g_mlp, w_up=v_w_up, w_down=v_w_down,
              g_final=v_g_final)
    order = list(weights)
    upd = {k: _adamw(f"adamw_{k}", weights[k], grad[k], ms[k], vs[k], copy_rows) for k in BIG}
    little = [k for k in order if k not in BIG]
    for k, res in zip(little, _adamw_small("adamw_small", [(weights[k], grad[k].reshape(weights[k].shape), ms[k],
                                                            vs[k]) for k in little])):
        upd[k] = res
    grad = {k: grad[k].reshape(weights[k].shape) for k in order}
    return (loss, grad_x, *[grad[k] for k in order], *[upd[k][0] for k in order], *[upd[k][1] for k in order],
            *[upd[k][2] for k in order])
```
